```python
import math
import jax, jax.numpy as jnp
from jax import lax
import numpy as np

D_MODEL = 1024
BATCH = 8
SEQ = 2048
DEPTH = 1

HEAD_DIM = 64
A_Q_HEADS = 8
A_KV_HEADS = 2
A_GROUPS = A_Q_HEADS // A_KV_HEADS
A_WIDTH = A_Q_HEADS * HEAD_DIM
B_HEADS = 8
B_WIDTH = B_HEADS * HEAD_DIM
IDX_HEADS = 8
IDX_DIM = 32
WINDOW = 128
BLOCK = 128
TOPK_MAX = 256
N_BUCKETS = 32
MAX_DISTANCE = 128
RMS_EPS = 1e-6
SPLIT_SIZES = (
    A_WIDTH,
    A_KV_HEADS * HEAD_DIM,
    A_KV_HEADS * HEAD_DIM,
    A_WIDTH,
    B_WIDTH,
    B_WIDTH,
    B_WIDTH,
    B_WIDTH,
    IDX_HEADS * IDX_DIM,
    IDX_DIM,
    IDX_HEADS,
    2 * D_MODEL,
)
IN_WIDTH = sum(SPLIT_SIZES)

kernel_name = "hybrid_swa_sink_dsa_gated_block"


def rms_norm(x, g):
    xf = x.astype(jnp.float32)
    xf = xf * lax.rsqrt(jnp.mean(xf * xf, axis=-1, keepdims=True) + RMS_EPS)
    return (xf * g.astype(jnp.float32)).astype(x.dtype)


def t5_bucket(n):
    n = jnp.maximum(n, 0)
    max_exact = N_BUCKETS // 2
    nf = jnp.maximum(n, 1).astype(jnp.float32)
    large = max_exact + (jnp.log(nf / max_exact) / math.log(MAX_DISTANCE / max_exact)
                         * (N_BUCKETS - max_exact)).astype(jnp.int32)
    large = jnp.minimum(large, N_BUCKETS - 1)
    return jnp.where(n < max_exact, n, large)


def swa_sink_attention(q, k, v, sinks, table_a):
    B, S = q.shape[0], q.shape[1]
    nb = S // BLOCK
    qb = q.reshape(B, nb, BLOCK, A_KV_HEADS, A_GROUPS, HEAD_DIM)

    def band(t):
        tp = jnp.pad(t, ((0, 0), (BLOCK, 0), (0, 0), (0, 0)))
        prev = tp[:, :S].reshape(B, nb, BLOCK, A_KV_HEADS, HEAD_DIM)
        cur = t.reshape(B, nb, BLOCK, A_KV_HEADS, HEAD_DIM)
        return jnp.concatenate([prev, cur], axis=2)

    kb, vb = band(k), band(v)
    scores = jnp.einsum('bnqhgd,bnkhd->bnhgqk', qb, kb,
                        preferred_element_type=jnp.float32) * (HEAD_DIM ** -0.5)
    t_loc = jnp.arange(BLOCK)[:, None]
    s_loc = jnp.arange(2 * BLOCK)[None, :]
    dist = t_loc + BLOCK - s_loc
    bias = table_a.astype(jnp.float32)[t5_bucket(dist)]
    bias = bias.transpose(2, 0, 1).reshape(A_KV_HEADS, A_GROUPS, BLOCK, 2 * BLOCK)
    blk = jnp.arange(nb)[:, None, None]
    valid = (dist >= 0) & (dist < WINDOW) & (blk * BLOCK - BLOCK + s_loc >= 0)
    scores = jnp.where(valid[None, :, None, None], scores + bias, -jnp.inf)
    sink = jnp.broadcast_to(
        sinks.astype(jnp.float32).reshape(A_KV_HEADS, A_GROUPS)[None, None, :, :, None, None],
        scores.shape[:-1] + (1,))
    probs = jax.nn.softmax(jnp.concatenate([scores, sink], axis=-1), axis=-1)[..., :-1]
    out = jnp.einsum('bnhgqk,bnkhd->bnqhgd', probs.astype(v.dtype), vb)
    return out.reshape(B, S, A_Q_HEADS * HEAD_DIM)


def dsa_attention(q, k, v, q_idx, k_idx, w_idx, table_b):
    B, S = q.shape[0], q.shape[1]
    nb = S // BLOCK
    top_k = min(TOPK_MAX, S // 4)
    k_flat = k.reshape(B, S, B_HEADS * HEAD_DIM)
    v_flat = v.reshape(B, S, B_HEADS * HEAD_DIM)
    key_pos = jnp.arange(S)
    gather = jax.vmap(lambda table, ix: table[ix])

    def one_block(args):
        i, qblk, qiblk, wblk = args
        t = i * BLOCK + jnp.arange(BLOCK)
        dots = jnp.einsum('bqhe,bse->bqhs', qiblk, k_idx,
                          preferred_element_type=jnp.float32) * (IDX_DIM ** -0.5)
        w = wblk.astype(jnp.float32) * (IDX_HEADS ** -0.5)
        score_idx = jnp.einsum('bqh,bqhs->bqs', w, jax.nn.relu(dots))
        causal = key_pos[None, :] <= t[:, None]
        score_idx = jnp.where(causal[None], score_idx, -jnp.inf)
        _, idx = lax.top_k(score_idx, top_k)
        valid = idx <= t[None, :, None]
        kg = gather(k_flat, idx).reshape(B, BLOCK, top_k, B_HEADS, HEAD_DIM)
        vg = gather(v_flat, idx).reshape(B, BLOCK, top_k, B_HEADS, HEAD_DIM)
        sc = jnp.einsum('bqhd,bqkhd->bhqk', qblk, kg,
                        preferred_element_type=jnp.float32) * (HEAD_DIM ** -0.5)
        bias = table_b.astype(jnp.float32)[t5_bucket(t[None, :, None] - idx)]
        sc = jnp.where(valid[:, None], sc + bias.transpose(0, 3, 1, 2), -jnp.inf)
        p = jax.nn.softmax(sc, axis=-1)
        return jnp.einsum('bhqk,bqkhd->bqhd', p.astype(v.dtype), vg)

    def to_blocks(t):
        return jnp.moveaxis(t.reshape((B, nb, BLOCK) + t.shape[2:]), 1, 0)

    outs = lax.map(one_block, (jnp.arange(nb), to_blocks(q), to_blocks(q_idx), to_blocks(w_idx)))
    return jnp.moveaxis(outs, 0, 1).reshape(B, S, B_HEADS * HEAD_DIM)


def setup_inputs(seed: int = 0) -> dict:
    key = jax.random.key(seed)
    ks = jax.random.split(key, 12)
    f32 = jnp.float32
    return {
        "x": jax.random.normal(ks[0], (BATCH, SEQ, D_MODEL), f32),
        "norm_g": 1.0 + 0.05 * jax.random.normal(ks[1], (DEPTH, D_MODEL), f32),
        "w_in": jax.random.normal(ks[2], (DEPTH, D_MODEL, IN_WIDTH), f32) * D_MODEL ** -0.5,
        "qnorm_a": 1.0 + 0.05 * jax.random.normal(ks[3], (DEPTH, HEAD_DIM), f32),
        "knorm_a": 1.0 + 0.05 * jax.random.normal(ks[4], (DEPTH, HEAD_DIM), f32),
        "sinks_a": 0.5 * jax.random.normal(ks[5], (DEPTH, A_Q_HEADS), f32),
        "qnorm_b": 1.0 + 0.05 * jax.random.normal(ks[6], (DEPTH, HEAD_DIM), f32),
        "knorm_b": 1.0 + 0.05 * jax.random.normal(ks[7], (DEPTH, HEAD_DIM), f32),
        "rel_bias": 0.5 * jax.random.normal(ks[8], (N_BUCKETS, A_Q_HEADS + B_HEADS), f32),
        "w_proj_a": jax.random.normal(ks[9], (DEPTH, A_WIDTH, D_MODEL), f32) * A_WIDTH ** -0.5,
        "w_proj_b": jax.random.normal(ks[10], (DEPTH, B_WIDTH, D_MODEL), f32) * B_WIDTH ** -0.5,
        "w_out": jax.random.normal(ks[11], (DEPTH, D_MODEL, D_MODEL), f32) * D_MODEL ** -0.5,
    }


def reference(x, norm_g, w_in, qnorm_a, knorm_a, sinks_a, qnorm_b, knorm_b, rel_bias,
              w_proj_a, w_proj_b, w_out):
    B, S = x.shape[0], x.shape[1]
    offsets = [int(o) for o in np.cumsum(SPLIT_SIZES)[:-1]]
    table_a = rel_bias[:, :A_Q_HEADS]
    table_b = rel_bias[:, A_Q_HEADS:]
    for l in range(DEPTH):
        h = rms_norm(x, norm_g[l])
        proj = jnp.einsum('bsd,de->bse', h, w_in[l])
        (qa, ka, va, za, qb, kb, vb, zb, qi, ki, wi, gates) = jnp.split(proj, offsets, axis=-1)
        qa = rms_norm(qa.reshape(B, S, A_Q_HEADS, HEAD_DIM), qnorm_a[l])
        ka = rms_norm(ka.reshape(B, S, A_KV_HEADS, HEAD_DIM), knorm_a[l])
        va = va.reshape(B, S, A_KV_HEADS, HEAD_DIM)
        ya = swa_sink_attention(qa, ka, va, sinks_a[l], table_a) * jax.nn.silu(za)
        qb = rms_norm(qb.reshape(B, S, B_HEADS, HEAD_DIM), qnorm_b[l])
        kb = rms_norm(kb.reshape(B, S, B_HEADS, HEAD_DIM), knorm_b[l])
        vb = vb.reshape(B, S, B_HEADS, HEAD_DIM)
        qi = qi.reshape(B, S, IDX_HEADS, IDX_DIM)
        yb = dsa_attention(qb, kb, vb, qi, ki, wi, table_b) * jax.nn.silu(zb)
        g = jax.nn.sigmoid(gates.astype(jnp.float32)).astype(x.dtype)
        merged = (g[..., :D_MODEL] * jnp.einsum('bse,ed->bsd', ya, w_proj_a[l])
                  + g[..., D_MODEL:] * jnp.einsum('bse,ed->bsd', yb, w_proj_b[l]))
        x = x + jnp.einsum('bsd,de->bse', merged, w_out[l])
    return x
```

```python
import functools
import math

import jax
import jax.numpy as jnp
import numpy as np
from jax import lax
from jax.experimental import pallas as pl
from jax.experimental.pallas import tpu as pltpu

D_MODEL = 1024
HEAD_DIM = 64
A_Q_HEADS = 8
A_KV_HEADS = 2
A_GROUPS = A_Q_HEADS // A_KV_HEADS
A_WIDTH = A_Q_HEADS * HEAD_DIM
A_KV_WIDTH = A_KV_HEADS * HEAD_DIM
B_HEADS = 8
B_WIDTH = B_HEADS * HEAD_DIM
IDX_HEADS = 8
IDX_DIM = 32
IDX_WIDTH = IDX_HEADS * IDX_DIM
WINDOW = 128
TOPK_MAX = 256
N_BUCKETS = 32
MAX_DISTANCE = 128
RMS_EPS = 1e-6

LANES = 128
TQ = 128
KCHUNK = 256
TM_PROJ = 512
NEG = -1e30
INT_MIN = -(2 ** 31)
VMEM_LIMIT = 56 * 1024 * 1024

_SEG = {}
_off = 0
for _name, _rows in (("qa", A_WIDTH), ("ka", A_KV_WIDTH), ("va", A_KV_WIDTH), ("za", A_WIDTH),
                     ("qb", B_WIDTH), ("kb", B_WIDTH), ("vb", B_WIDTH), ("zb", B_WIDTH),
                     ("qi", IDX_WIDTH), ("kw", LANES), ("gates", 2 * D_MODEL)):
    _SEG[_name] = (_off, _off + _rows)
    _off += _rows
PROJ_ROWS = _off
KW_REAL = IDX_DIM + IDX_HEADS

_F32 = jnp.float32
_BF16 = jnp.bfloat16
_NT = (((1,), (1,)), ((), ()))


def _inproj_kernel(x_ref, g_ref, wt_ref, gqa_ref, gka_ref, gqb_ref, gkb_ref,
                   qa_o, ka_o, va_o, za_o, qb_o, kb_o, vb_o, zb_o, qi_o, ki_o, wi_o, gt_o):
    tm = x_ref.shape[1]
    x = x_ref[0]
    ms = jnp.mean(x * x, axis=-1, keepdims=True)
    h = (x * lax.rsqrt(ms + RMS_EPS) * g_ref[...]).astype(_BF16)

    def proj(lo, hi):
        return lax.dot_general(wt_ref[lo:hi, :], h, _NT, preferred_element_type=_F32)

    def seg(name):
        return proj(*_SEG[name])

    def headnorm(p, gain_ref):
        nh = p.shape[0] // HEAD_DIM
        p3 = p.reshape(nh, HEAD_DIM, tm)
        r = lax.rsqrt(jnp.mean(p3 * p3, axis=1, keepdims=True) + RMS_EPS)
        return (p3 * r * gain_ref[...][None]).reshape(nh * HEAD_DIM, tm)

    def silu(p):
        return p * jax.nn.sigmoid(p)

    qa_o[0] = headnorm(seg("qa"), gqa_ref).astype(_BF16)
    ka_o[0] = headnorm(seg("ka"), gka_ref).T.astype(_BF16)
    va_o[0] = seg("va").astype(_BF16)
    za_o[0] = silu(seg("za")).astype(_BF16)
    qb_o[0] = headnorm(seg("qb"), gqb_ref).astype(_BF16)
    kb_o[0] = headnorm(seg("kb"), gkb_ref).T.astype(_BF16)
    vb = seg("vb").astype(_BF16)
    for c in range(tm // KCHUNK):
        vb_o[0, c] = vb[:, c * KCHUNK:(c + 1) * KCHUNK]
    zb_o[0] = silu(seg("zb")).astype(_BF16)
    qi_o[0] = seg("qi").astype(_BF16)
    kw = seg("kw")
    wi_o[0] = kw[IDX_DIM:KW_REAL, :]
    row = lax.broadcasted_iota(jnp.int32, kw.shape, 0)
    ki_o[0] = jnp.where(row < IDX_DIM, kw, 0.0).T.astype(_BF16)
    g_lo = _SEG["gates"][0]
    for c in range(2 * D_MODEL // 512):
        gt_o[0, c * 512:(c + 1) * 512, :] = jax.nn.sigmoid(
            proj(g_lo + c * 512, g_lo + (c + 1) * 512)).astype(_BF16)


def _inproj(x, g, wt, gqa, gka, gqb, gkb):
    B, S, D = x.shape
    tm = TM_PROJ
    nt = S // tm
    const = lambda b, t: (0, 0)
    tr = lambda rows: pl.BlockSpec((1, rows, tm), lambda b, t: (b, 0, t))
    nat = lambda cols: pl.BlockSpec((1, tm, cols), lambda b, t: (b, t, 0))
    sds = jax.ShapeDtypeStruct
    out_shape = (
        sds((B, A_WIDTH, S), _BF16), sds((B, S, A_KV_WIDTH), _BF16), sds((B, A_KV_WIDTH, S), _BF16),
        sds((B, A_WIDTH, S), _BF16),
        sds((B, B_WIDTH, S), _BF16), sds((B, S, B_WIDTH), _BF16),
        sds((B, S // KCHUNK, B_WIDTH, KCHUNK), _BF16), sds((B, B_WIDTH, S), _BF16),
        sds((B, IDX_WIDTH, S), _BF16), sds((B, S, LANES), _BF16), sds((B, IDX_HEADS, S), _F32),
        sds((B, 2 * D_MODEL, S), _BF16),
    )
    out_specs = (
        tr(A_WIDTH), nat(A_KV_WIDTH), tr(A_KV_WIDTH), tr(A_WIDTH),
        tr(B_WIDTH), nat(B_WIDTH),
        pl.BlockSpec((1, tm // KCHUNK, B_WIDTH, KCHUNK), lambda b, t: (b, t, 0, 0)), tr(B_WIDTH),
        tr(IDX_WIDTH), nat(LANES), tr(IDX_HEADS), tr(2 * D_MODEL),
    )
    in_specs = [
        pl.BlockSpec((1, tm, D), lambda b, t: (b, t, 0)),
        pl.BlockSpec((1, D), const),
        pl.BlockSpec((PROJ_ROWS, D), const, pipeline_mode=pl.Buffered(1)),
        pl.BlockSpec((HEAD_DIM, tm), const), pl.BlockSpec((HEAD_DIM, tm), const),
        pl.BlockSpec((HEAD_DIM, tm), const), pl.BlockSpec((HEAD_DIM, tm), const),
    ]
    return pl.pallas_call(
        _inproj_kernel, grid=(B, nt), in_specs=in_specs, out_specs=out_specs, out_shape=out_shape,
        compiler_params=pltpu.CompilerParams(
            dimension_semantics=("parallel", "parallel"), vmem_limit_bytes=VMEM_LIMIT),
        name="inproj",
    )(x, g, wt, gqa, gka, gqb, gkb)


def _mix_a_kernel(q_ref, kp_ref, kc_ref, vp_ref, vc_ref, z_ref, bias_ref, sink_ref, o_ref):
    i = pl.program_id(1)
    q = q_ref[0]
    kband = jnp.concatenate([kp_ref[0], kc_ref[0]], axis=0)
    vband = jnp.concatenate([vp_ref[0], vc_ref[0]], axis=1)
    zeros = jnp.zeros((HEAD_DIM, TQ), _BF16)
    has_prev = i > 0
    for g in range(A_KV_HEADS):
        cols = []
        for hh in range(A_GROUPS):
            h = g * A_GROUPS + hh
            qh = q[h * HEAD_DIM:(h + 1) * HEAD_DIM]
            parts = [zeros] * A_KV_HEADS
            parts[g] = qh
            cols.append(jnp.concatenate(parts, axis=0))
        rhs = jnp.concatenate(cols, axis=1)
        sc = jnp.dot(kband, rhs, preferred_element_type=_F32)
        probs, inv = [], []
        for hh in range(A_GROUPS):
            h = g * A_GROUPS + hh
            s = sc[:, hh * TQ:(hh + 1) * TQ] + bias_ref[h]
            s = jnp.concatenate([jnp.where(has_prev, s[:TQ], NEG), s[TQ:]], axis=0)
            sink = sink_ref[h:h + 1, :]
            m = jnp.maximum(jnp.max(s, axis=0, keepdims=True), sink)
            p = jnp.exp(s - m)
            denom = jnp.sum(p, axis=0, keepdims=True) + jnp.exp(sink - m)
            probs.append(p.astype(_BF16))
            inv.append(1.0 / denom)
        pt = jnp.concatenate(probs, axis=1)
        out = jnp.dot(vband[g * HEAD_DIM:(g + 1) * HEAD_DIM, :], pt,
                      preferred_element_type=_F32)
        for hh in range(A_GROUPS):
            h = g * A_GROUPS + hh
            rows = slice(h * HEAD_DIM, (h + 1) * HEAD_DIM)
            o = out[:, hh * TQ:(hh + 1) * TQ] * inv[hh] * z_ref[0, rows, :].astype(_F32)
            o_ref[0, rows, :] = o.astype(_BF16)


def _mix_a(qa_t, ka_n, va_t, za_t, bias_a, sink_a):
    B, _, S = qa_t.shape
    nq = S // TQ
    prev = lambda i: jnp.maximum(i - 1, 0)
    in_specs = [
        pl.BlockSpec((1, A_WIDTH, TQ), lambda b, i: (b, 0, i)),
        pl.BlockSpec((1, TQ, A_KV_WIDTH), lambda b, i: (b, prev(i), 0)),
        pl.BlockSpec((1, TQ, A_KV_WIDTH), lambda b, i: (b, i, 0)),
        pl.BlockSpec((1, A_KV_WIDTH, TQ), lambda b, i: (b, 0, prev(i))),
        pl.BlockSpec((1, A_KV_WIDTH, TQ), lambda b, i: (b, 0, i)),
        pl.BlockSpec((1, A_WIDTH, TQ), lambda b, i: (b, 0, i)),
        pl.BlockSpec((A_Q_HEADS, 2 * TQ, TQ), lambda b, i: (0, 0, 0)),
        pl.BlockSpec((A_Q_HEADS, TQ), lambda b, i: (0, 0)),
    ]
    return pl.pallas_call(
        _mix_a_kernel, grid=(B, nq), in_specs=in_specs,
        out_specs=pl.BlockSpec((1, A_WIDTH, TQ), lambda b, i: (b, 0, i)),
        out_shape=jax.ShapeDtypeStruct((B, A_WIDTH, S), _BF16),
        compiler_params=pltpu.CompilerParams(
            dimension_semantics=("parallel", "parallel"), vmem_limit_bytes=VMEM_LIMIT),
        name="mixer_a",
    )(qa_t, ka_n, ka_n, va_t, va_t, za_t, bias_a, sink_a)


def _colsum8(v):
    return jnp.sum(v.reshape(v.shape[0] // 8, 8, v.shape[1]), axis=0)


def _as01(pred):
    return jnp.where(pred, 1, 0).astype(jnp.int32)


def _colmax8(v):
    return jnp.max(v.reshape(v.shape[0] // 8, 8, v.shape[1]), axis=0)


def _mix_b_kernel(qb_ref, kb_ref, vb_ref, zb_ref, qi_ref, ki_ref, wi_ref, bias_ref, o_ref,
                  keys_scr, mb_scr, sc_scr, acc_scr, thr_scr, tie_scr):
    i = pl.program_id(1)
    nc = (i + 2) // 2
    top_k = TOPK_MAX

    qi = qi_ref[0]
    zpad = jnp.zeros((LANES - IDX_DIM, TQ), _BF16)
    rhs_i = jnp.concatenate(
        [jnp.concatenate([qi[h * IDX_DIM:(h + 1) * IDX_DIM], zpad], axis=0)
         for h in range(IDX_HEADS)], axis=1)
    w = wi_ref[0] * (IDX_DIM ** -0.5 * IDX_HEADS ** -0.5)
    s_minus_t = (lax.broadcasted_iota(jnp.int32, (KCHUNK, TQ), 0)
                 - lax.broadcasted_iota(jnp.int32, (KCHUNK, TQ), 1))

    def index_body(c, carry):
        r0 = pl.multiple_of(c * KCHUNK, KCHUNK)
        d = jnp.dot(ki_ref[0, pl.ds(r0, KCHUNK), :], rhs_i,
                    preferred_element_type=_F32)
        acc = w[0:1, :] * jnp.maximum(d[:, 0:TQ], 0.0)
        for h in range(1, IDX_HEADS):
            acc = acc + w[h:h + 1, :] * jnp.maximum(d[:, h * TQ:(h + 1) * TQ], 0.0)
        bits = pltpu.bitcast(acc, jnp.int32)
        key = jnp.where(bits < 0, bits ^ jnp.int32(0x7FFFFFFF), bits)
        causal = s_minus_t <= (i * TQ - r0)
        keys_scr[pl.ds(r0, KCHUNK), :] = jnp.where(causal, key, jnp.int32(INT_MIN))
        return carry

    lax.fori_loop(0, nc, index_body, 0)

    def count(pred_fn):
        def body(c, acc):
            r0 = pl.multiple_of(c * KCHUNK, KCHUNK)
            return acc + _colsum8(pred_fn(keys_scr[pl.ds(r0, KCHUNK), :], r0))
        acc = lax.fori_loop(0, nc, body, jnp.zeros((8, TQ), jnp.int32))
        return jnp.sum(acc, axis=0, keepdims=True)

    @pl.when(i * TQ + TQ <= top_k)
    def _():
        thr_scr[...] = jnp.full((1, TQ), INT_MIN, jnp.int32)
        tie_scr[...] = jnp.full((1, TQ), -1, jnp.int32)

    @pl.when(i * TQ + TQ > top_k)
    def _():
        def bit_body(b, t_u):
            cand_u = t_u | lax.shift_left(jnp.int32(1), 31 - b)
            cand = cand_u ^ jnp.int32(INT_MIN)
            cnt = count(lambda k, r0: _as01(k >= cand))
            return jnp.where(cnt >= top_k, cand_u, t_u)

        t_u = lax.fori_loop(0, 32, bit_body, jnp.zeros((1, TQ), jnp.int32))
        thr = t_u ^ jnp.int32(INT_MIN)
        cnt_gt = count(lambda k, r0: _as01(k > thr))
        cnt_ge = count(lambda k, r0: _as01(k >= thr))
        need = top_k - cnt_gt
        thr_scr[...] = thr
        tie_scr[...] = jnp.full((1, TQ), 2 ** 30, jnp.int32)
        row = lax.broadcasted_iota(jnp.int32, (KCHUNK, TQ), 0)

        @pl.when(jnp.max(cnt_ge) > top_k)
        def _():
            def tie_body(b, lo):
                cand = lo | lax.shift_left(jnp.int32(1), 10 - b)
                cnt = count(lambda k, r0: jnp.where(k == thr, _as01(row < cand - r0), 0))
                return jnp.where(cnt < need, cand, lo)
            tie_scr[...] = lax.fori_loop(0, 11, tie_body, jnp.zeros((1, TQ), jnp.int32))

    thr = thr_scr[...]
    tie = tie_scr[...]
    row = lax.broadcasted_iota(jnp.int32, (KCHUNK, TQ), 0)

    def mask_body(c, carry):
        r0 = pl.multiple_of(c * KCHUNK, KCHUNK)
        k = keys_scr[pl.ds(r0, KCHUNK), :]
        mb_scr[pl.ds(r0, KCHUNK), :] = jnp.where(
            k == thr, jnp.where(row <= tie - r0, 0.0, NEG), jnp.where(k > thr, 0.0, NEG))
        return carry

    lax.fori_loop(0, nc, mask_body, 0)

    zeros = jnp.zeros((HEAD_DIM, TQ), _BF16)
    for p in range(B_HEADS // 2):
        lanes = slice(p * LANES, (p + 1) * LANES)
        q0 = qb_ref[0, (2 * p) * HEAD_DIM:(2 * p + 1) * HEAD_DIM, :]
        q1 = qb_ref[0, (2 * p + 1) * HEAD_DIM:(2 * p + 2) * HEAD_DIM, :]
        rhs = jnp.concatenate([jnp.concatenate([q0, zeros], axis=0),
                               jnp.concatenate([zeros, q1], axis=0)], axis=1)

        def score_body(c, m):
            r0 = pl.multiple_of(c * KCHUNK, KCHUNK)
            s = jnp.dot(kb_ref[0, pl.ds(r0, KCHUNK), lanes], rhs,
                        preferred_element_type=_F32)
            new_m = []
            for hh in range(2):
                parts = []
                for half in range(KCHUNK // TQ):
                    jb = c * (KCHUNK // TQ) + half
                    kind = jnp.clip(jb - i + 2, 0, 2)
                    rs = pl.multiple_of(r0 + half * TQ, TQ)
                    parts.append(s[half * TQ:(half + 1) * TQ, hh * TQ:(hh + 1) * TQ]
                                 + bias_ref[2 * p + hh, kind] + mb_scr[pl.ds(rs, TQ), :])
                sh = jnp.concatenate(parts, axis=0)
                sc_scr[pl.ds(r0, KCHUNK), hh * TQ:(hh + 1) * TQ] = sh
                new_m.append(jnp.maximum(m[hh], _colmax8(sh)))
            return tuple(new_m)

        m8 = lax.fori_loop(0, nc, score_body,
                           (jnp.full((8, TQ), NEG, _F32), jnp.full((8, TQ), NEG, _F32)))
        m_row = jnp.concatenate([jnp.max(m, axis=0, keepdims=True) for m in m8], axis=1)
        acc_scr[...] = jnp.zeros_like(acc_scr)

        def pv_body(c, l8):
            r0 = pl.multiple_of(c * KCHUNK, KCHUNK)
            pr = jnp.exp(sc_scr[pl.ds(r0, KCHUNK), :] - m_row)
            acc_scr[...] += jnp.dot(vb_ref[0, c, lanes, :], pr.astype(_BF16),
                                    preferred_element_type=_F32)
            return l8 + _colsum8(pr)

        l8 = lax.fori_loop(0, nc, pv_body, jnp.zeros((8, 2 * TQ), _F32))
        inv = 1.0 / jnp.sum(l8, axis=0, keepdims=True)
        for hh in range(2):
            rows = slice((2 * p + hh) * HEAD_DIM, (2 * p + hh + 1) * HEAD_DIM)
            o = (acc_scr[hh * HEAD_DIM:(hh + 1) * HEAD_DIM, hh * TQ:(hh + 1) * TQ]
                 * inv[:, hh * TQ:(hh + 1) * TQ] * zb_ref[0, rows, :].astype(_F32))
            o_ref[0, rows, :] = o.astype(_BF16)


def _mix_b(qb_t, kb_n, vb_c, zb_t, qi_t, ki_n, wi_t, bias_b):
    B, _, S = qb_t.shape
    nq = S // TQ
    per_q = lambda rows: pl.BlockSpec((1, rows, TQ), lambda b, i: (b, 0, i))
    in_specs = [
        per_q(B_WIDTH),
        pl.BlockSpec((1, S, B_WIDTH), lambda b, i: (b, 0, 0)),
        pl.BlockSpec((1, S // KCHUNK, B_WIDTH, KCHUNK), lambda b, i: (b, 0, 0, 0)),
        per_q(B_WIDTH),
        per_q(IDX_WIDTH),
        pl.BlockSpec((1, S, LANES), lambda b, i: (b, 0, 0)),
        per_q(IDX_HEADS),
        pl.BlockSpec((B_HEADS, 3, TQ, TQ), lambda b, i: (0, 0, 0, 0)),
    ]
    scratch = [
        pltpu.VMEM((S, TQ), jnp.int32),
        pltpu.VMEM((S, TQ), _F32),
        pltpu.VMEM((S, 2 * TQ), _F32),
        pltpu.VMEM((LANES, 2 * TQ), _F32),
        pltpu.VMEM((1, TQ), jnp.int32),
        pltpu.VMEM((1, TQ), jnp.int32),
    ]
    return pl.pallas_call(
        _mix_b_kernel, grid=(B, nq), in_specs=in_specs,
        out_specs=per_q(B_WIDTH),
        out_shape=jax.ShapeDtypeStruct((B, B_WIDTH, S), _BF16),
        scratch_shapes=scratch,
        compiler_params=pltpu.CompilerParams(
            dimension_semantics=("parallel", "arbitrary"), vmem_limit_bytes=VMEM_LIMIT),
        name="mixer_b",
    )(qb_t, kb_n, vb_c, zb_t, qi_t, ki_n, wi_t, bias_b)


def _merge_kernel(x_ref, ya_ref, yb_ref, gt_ref, wpa_ref, wpb_ref, wo_ref, o_ref):
    pa = jnp.dot(wpa_ref[...], ya_ref[0], preferred_element_type=_F32)
    pb = jnp.dot(wpb_ref[...], yb_ref[0], preferred_element_type=_F32)
    merged = (gt_ref[0, :D_MODEL, :].astype(_F32) * pa
              + gt_ref[0, D_MODEL:, :].astype(_F32) * pb).astype(_BF16)
    out_t = jnp.dot(wo_ref[...], merged, preferred_element_type=_F32)
    o_ref[0] = x_ref[0] + out_t.T


def _merge(x, ya_t, yb_t, gt_t, wpa_t, wpb_t, wo_t):
    B, S, D = x.shape
    tm = TM_PROJ
    const = lambda b, t: (0, 0)
    in_specs = [
        pl.BlockSpec((1, tm, D), lambda b, t: (b, t, 0)),
        pl.BlockSpec((1, A_WIDTH, tm), lambda b, t: (b, 0, t)),
        pl.BlockSpec((1, B_WIDTH, tm), lambda b, t: (b, 0, t)),
        pl.BlockSpec((1, 2 * D, tm), lambda b, t: (b, 0, t)),
        pl.BlockSpec((D, A_WIDTH), const),
        pl.BlockSpec((D, B_WIDTH), const),
        pl.BlockSpec((D, D), const),
    ]
    return pl.pallas_call(
        _merge_kernel, grid=(B, S // tm), in_specs=in_specs,
        out_specs=pl.BlockSpec((1, tm, D), lambda b, t: (b, t, 0)),
        out_shape=jax.ShapeDtypeStruct((B, S, D), x.dtype),
        compiler_params=pltpu.CompilerParams(
            dimension_semantics=("parallel", "parallel"), vmem_limit_bytes=VMEM_LIMIT),
        name="merge_out",
    )(x, ya_t, yb_t, gt_t, wpa_t, wpb_t, wo_t)


def _t5_bucket(n):
    n = jnp.maximum(n, 0)
    max_exact = N_BUCKETS // 2
    nf = jnp.maximum(n, 1).astype(jnp.float32)
    large = max_exact + (jnp.log(nf / max_exact) / math.log(MAX_DISTANCE / max_exact)
                         * (N_BUCKETS - max_exact)).astype(jnp.int32)
    large = jnp.minimum(large, N_BUCKETS - 1)
    return jnp.where(n < max_exact, n, large)


def _bias_tables(rel_bias):
    t = jnp.arange(TQ)[None, :]
    s = jnp.arange(TQ)[:, None]
    d_prev = t + TQ - s
    d_cur = t - s
    table_a = rel_bias[:, :A_Q_HEADS].astype(_F32).T
    table_b = rel_bias[:, A_Q_HEADS:].astype(_F32).T

    def tile(table, dist, windowed):
        b = table[:, _t5_bucket(dist)]
        if windowed:
            b = jnp.where(((dist >= 0) & (dist < WINDOW))[None], b, NEG)
        return b

    bias_a = jnp.concatenate([tile(table_a, d_prev, True), tile(table_a, d_cur, True)], axis=1)
    far = jnp.broadcast_to(table_b[:, N_BUCKETS - 1][:, None, None], (B_HEADS, TQ, TQ))
    bias_b = jnp.stack([far, tile(table_b, d_prev, False), tile(table_b, d_cur, False)], axis=1)
    return bias_a, bias_b


def kernel(x, norm_g, w_in, qnorm_a, knorm_a, sinks_a, qnorm_b, knorm_b, rel_bias,
           w_proj_a, w_proj_b, w_out):
    assert norm_g.shape[0] == 1, "single-layer block"
    B, S, D = x.shape
    assert D == D_MODEL and S % TM_PROJ == 0 and S % KCHUNK == 0
    assert MAX_DISTANCE <= TQ

    wt = w_in[0].T
    n_real = _SEG["kw"][0] + KW_REAL
    wt = jnp.concatenate(
        [wt[:n_real], jnp.zeros((_SEG["kw"][1] - n_real, D), wt.dtype), wt[n_real:]], axis=0)
    wt = wt.astype(_BF16)
    bcast = lambda g, scale: jnp.broadcast_to((g.astype(_F32) * scale)[:, None], (HEAD_DIM, TM_PROJ))
    q_scale = HEAD_DIM ** -0.5
    (qa_t, ka_n, va_t, za_t, qb_t, kb_n, vb_c, zb_t, qi_t, ki_n, wi_t, gt_t) = _inproj(
        x, norm_g[0][None, :].astype(_F32), wt,
        bcast(qnorm_a[0], q_scale), bcast(knorm_a[0], 1.0),
        bcast(qnorm_b[0], q_scale), bcast(knorm_b[0], 1.0))

    bias_a, bias_b = _bias_tables(rel_bias)
    sink_a = jnp.broadcast_to(sinks_a[0].astype(_F32)[:, None], (A_Q_HEADS, TQ))
    ya_t = _mix_a(qa_t, ka_n, va_t, za_t, bias_a, sink_a)
    yb_t = _mix_b(qb_t, kb_n, vb_c, zb_t, qi_t, ki_n, wi_t, bias_b)
    return _merge(x, ya_t, yb_t, gt_t,
                  w_proj_a[0].T.astype(_BF16), w_proj_b[0].T.astype(_BF16), w_out[0].T.astype(_BF16))
```

```python
import functools
import math

import jax
import jax.numpy as jnp
import numpy as np
from jax import lax
from jax.experimental import pallas as pl
from jax.experimental.pallas import tpu as pltpu

D_MODEL = 1024
HEAD_DIM = 64
A_Q_HEADS = 8
A_KV_HEADS = 2
A_GROUPS = A_Q_HEADS // A_KV_HEADS
A_WIDTH = A_Q_HEADS * HEAD_DIM
A_KV_WIDTH = A_KV_HEADS * HEAD_DIM
B_HEADS = 8
B_WIDTH = B_HEADS * HEAD_DIM
IDX_HEADS = 8
IDX_DIM = 32
IDX_WIDTH = IDX_HEADS * IDX_DIM
WINDOW = 128
TOPK_MAX = 256
N_BUCKETS = 32
MAX_DISTANCE = 128
RMS_EPS = 1e-6

LANES = 128
TQ = 128
KCHUNK = 256
KSUPER = 2 * KCHUNK
TM_PROJ = 512
NEG = -1e30
INT_MIN = -(2 ** 31)
VMEM_LIMIT = 56 * 1024 * 1024

_SEG = {}
_off = 0
for _name, _rows in (("qa", A_WIDTH), ("ka", A_KV_WIDTH), ("va", A_KV_WIDTH), ("za", A_WIDTH),
                     ("qb", B_WIDTH), ("kb", B_WIDTH), ("vb", B_WIDTH), ("zb", B_WIDTH),
                     ("qi", IDX_WIDTH), ("kw", LANES), ("gates", 2 * D_MODEL)):
    _SEG[_name] = (_off, _off + _rows)
    _off += _rows
PROJ_ROWS = _off
KW_REAL = IDX_DIM + IDX_HEADS

_F32 = jnp.float32
_BF16 = jnp.bfloat16
_NT = (((1,), (1,)), ((), ()))


def _inproj_kernel(x_ref, g_ref, wt_ref, gqa_ref, gka_ref, gqb_ref, gkb_ref,
                   qa_o, ka_o, va_o, za_o, qb_o, kb_o, vb_o, zb_o, qi_o, ki_o, wi_o, gt_o):
    tm = x_ref.shape[1]
    x = x_ref[0]
    ms = jnp.mean(x * x, axis=-1, keepdims=True)
    h = (x * lax.rsqrt(ms + RMS_EPS) * g_ref[...]).astype(_BF16)

    def proj(lo, hi):
        return lax.dot_general(wt_ref[lo:hi, :], h, _NT, preferred_element_type=_F32)

    def seg(name):
        return proj(*_SEG[name])

    def headnorm(p, gain_ref):
        nh = p.shape[0] // HEAD_DIM
        p3 = p.reshape(nh, HEAD_DIM, tm)
        r = lax.rsqrt(jnp.mean(p3 * p3, axis=1, keepdims=True) + RMS_EPS)
        return (p3 * r * gain_ref[...][None]).reshape(nh * HEAD_DIM, tm)

    def silu(p):
        return p * jax.nn.sigmoid(p)

    qa_o[0] = headnorm(seg("qa"), gqa_ref).astype(_BF16)
    ka_o[0] = headnorm(seg("ka"), gka_ref).T.astype(_BF16)
    va_o[0] = seg("va").astype(_BF16)
    za_o[0] = silu(seg("za")).astype(_BF16)
    qb_o[0] = headnorm(seg("qb"), gqb_ref).astype(_BF16)
    kb_o[0] = headnorm(seg("kb"), gkb_ref).T.astype(_BF16)
    vb = seg("vb").astype(_BF16)
    for c in range(tm // KCHUNK):
        vb_o[0, c] = vb[:, c * KCHUNK:(c + 1) * KCHUNK]
    zb_o[0] = silu(seg("zb")).astype(_BF16)
    qi_o[0] = seg("qi").astype(_BF16)
    kw = seg("kw")
    wi_o[0] = kw[IDX_DIM:KW_REAL, :]
    row = lax.broadcasted_iota(jnp.int32, kw.shape, 0)
    ki_o[0] = jnp.where(row < IDX_DIM, kw, 0.0).T.astype(_BF16)
    g_lo = _SEG["gates"][0]
    for c in range(2 * D_MODEL // 512):
        gt_o[0, c * 512:(c + 1) * 512, :] = jax.nn.sigmoid(
            proj(g_lo + c * 512, g_lo + (c + 1) * 512)).astype(_BF16)


def _inproj(x, g, wt, gqa, gka, gqb, gkb):
    B, S, D = x.shape
    tm = TM_PROJ
    nt = S // tm
    const = lambda b, t: (0, 0)
    tr = lambda rows: pl.BlockSpec((1, rows, tm), lambda b, t: (b, 0, t))
    nat = lambda cols: pl.BlockSpec((1, tm, cols), lambda b, t: (b, t, 0))
    sds = jax.ShapeDtypeStruct
    out_shape = (
        sds((B, A_WIDTH, S), _BF16), sds((B, S, A_KV_WIDTH), _BF16), sds((B, A_KV_WIDTH, S), _BF16),
        sds((B, A_WIDTH, S), _BF16),
        sds((B, B_WIDTH, S), _BF16), sds((B, S, B_WIDTH), _BF16),
        sds((B, S // KCHUNK, B_WIDTH, KCHUNK), _BF16), sds((B, B_WIDTH, S), _BF16),
        sds((B, IDX_WIDTH, S), _BF16), sds((B, S, LANES), _BF16), sds((B, IDX_HEADS, S), _F32),
        sds((B, 2 * D_MODEL, S), _BF16),
    )
    out_specs = (
        tr(A_WIDTH), nat(A_KV_WIDTH), tr(A_KV_WIDTH), tr(A_WIDTH),
        tr(B_WIDTH), nat(B_WIDTH),
        pl.BlockSpec((1, tm // KCHUNK, B_WIDTH, KCHUNK), lambda b, t: (b, t, 0, 0)), tr(B_WIDTH),
        tr(IDX_WIDTH), nat(LANES), tr(IDX_HEADS), tr(2 * D_MODEL),
    )
    in_specs = [
        pl.BlockSpec((1, tm, D), lambda b, t: (b, t, 0)),
        pl.BlockSpec((1, D), const),
        pl.BlockSpec((PROJ_ROWS, D), const, pipeline_mode=pl.Buffered(1)),
        pl.BlockSpec((HEAD_DIM, tm), const), pl.BlockSpec((HEAD_DIM, tm), const),
        pl.BlockSpec((HEAD_DIM, tm), const), pl.BlockSpec((HEAD_DIM, tm), const),
    ]
    return pl.pallas_call(
        _inproj_kernel, grid=(B, nt), in_specs=in_specs, out_specs=out_specs, out_shape=out_shape,
        compiler_params=pltpu.CompilerParams(
            dimension_semantics=("parallel", "parallel"), vmem_limit_bytes=VMEM_LIMIT),
        name="inproj",
    )(x, g, wt, gqa, gka, gqb, gkb)


def _mix_a_kernel(q_ref, kp_ref, kc_ref, vp_ref, vc_ref, z_ref, bias_ref, sink_ref, o_ref):
    i = pl.program_id(1)
    q = q_ref[0]
    kband = jnp.concatenate([kp_ref[0], kc_ref[0]], axis=0)
    vband = jnp.concatenate([vp_ref[0], vc_ref[0]], axis=1)
    zeros = jnp.zeros((HEAD_DIM, TQ), _BF16)
    has_prev = i > 0
    for g in range(A_KV_HEADS):
        cols = []
        for hh in range(A_GROUPS):
            h = g * A_GROUPS + hh
            qh = q[h * HEAD_DIM:(h + 1) * HEAD_DIM]
            parts = [zeros] * A_KV_HEADS
            parts[g] = qh
            cols.append(jnp.concatenate(parts, axis=0))
        rhs = jnp.concatenate(cols, axis=1)
        sc = jnp.dot(kband, rhs, preferred_element_type=_F32)
        probs, inv = [], []
        for hh in range(A_GROUPS):
            h = g * A_GROUPS + hh
            s = sc[:, hh * TQ:(hh + 1) * TQ] + bias_ref[h]
            s = jnp.concatenate([jnp.where(has_prev, s[:TQ], NEG), s[TQ:]], axis=0)
            sink = sink_ref[h:h + 1, :]
            m = jnp.maximum(jnp.max(s, axis=0, keepdims=True), sink)
            p = jnp.exp(s - m)
            denom = jnp.sum(p, axis=0, keepdims=True) + jnp.exp(sink - m)
            probs.append(p.astype(_BF16))
            inv.append(1.0 / denom)
        pt = jnp.concatenate(probs, axis=1)
        out = jnp.dot(vband[g * HEAD_DIM:(g + 1) * HEAD_DIM, :], pt,
                      preferred_element_type=_F32)
        for hh in range(A_GROUPS):
            h = g * A_GROUPS + hh
            rows = slice(h * HEAD_DIM, (h + 1) * HEAD_DIM)
            o = out[:, hh * TQ:(hh + 1) * TQ] * inv[hh] * z_ref[0, rows, :].astype(_F32)
            o_ref[0, rows, :] = o.astype(_BF16)


def _mix_a(qa_t, ka_n, va_t, za_t, bias_a, sink_a):
    B, _, S = qa_t.shape
    nq = S // TQ
    prev = lambda i: jnp.maximum(i - 1, 0)
    in_specs = [
        pl.BlockSpec((1, A_WIDTH, TQ), lambda b, i: (b, 0, i)),
        pl.BlockSpec((1, TQ, A_KV_WIDTH), lambda b, i: (b, prev(i), 0)),
        pl.BlockSpec((1, TQ, A_KV_WIDTH), lambda b, i: (b, i, 0)),
        pl.BlockSpec((1, A_KV_WIDTH, TQ), lambda b, i: (b, 0, prev(i))),
        pl.BlockSpec((1, A_KV_WIDTH, TQ), lambda b, i: (b, 0, i)),
        pl.BlockSpec((1, A_WIDTH, TQ), lambda b, i: (b, 0, i)),
        pl.BlockSpec((A_Q_HEADS, 2 * TQ, TQ), lambda b, i: (0, 0, 0)),
        pl.BlockSpec((A_Q_HEADS, TQ), lambda b, i: (0, 0)),
    ]
    return pl.pallas_call(
        _mix_a_kernel, grid=(B, nq), in_specs=in_specs,
        out_specs=pl.BlockSpec((1, A_WIDTH, TQ), lambda b, i: (b, 0, i)),
        out_shape=jax.ShapeDtypeStruct((B, A_WIDTH, S), _BF16),
        compiler_params=pltpu.CompilerParams(
            dimension_semantics=("parallel", "parallel"), vmem_limit_bytes=VMEM_LIMIT),
        name="mixer_a",
    )(qa_t, ka_n, ka_n, va_t, va_t, za_t, bias_a, sink_a)


def _colsum8(v):
    return jnp.sum(v.reshape(v.shape[0] // 8, 8, v.shape[1]), axis=0)


def _as01(pred):
    return jnp.where(pred, 1, 0).astype(jnp.int32)


def _colmax8(v):
    return jnp.max(v.reshape(v.shape[0] // 8, 8, v.shape[1]), axis=0)


def _mix_b_kernel(qb_ref, kb_ref, vb_ref, zb_ref, qi_ref, ki_ref, wi_ref, bias_ref, o_ref,
                  keys_scr, mb_scr, sc_scr, acc_scr, thr_scr, tie_scr):
    i = pl.program_id(1)
    nc = (i + 2) // 2
    ns = (nc + 1) // 2
    top_k = TOPK_MAX

    qi = qi_ref[0]
    zpad = jnp.zeros((LANES - IDX_DIM, TQ), _BF16)
    rhs_i = jnp.concatenate(
        [jnp.concatenate([qi[h * IDX_DIM:(h + 1) * IDX_DIM], zpad], axis=0)
         for h in range(IDX_HEADS)], axis=1)
    w = wi_ref[0] * (IDX_DIM ** -0.5 * IDX_HEADS ** -0.5)
    s_minus_t = (lax.broadcasted_iota(jnp.int32, (KCHUNK, TQ), 0)
                 - lax.broadcasted_iota(jnp.int32, (KCHUNK, TQ), 1))

    def index_body(c, carry):
        for u in range(KSUPER // KCHUNK):
            r0 = pl.multiple_of(c * KSUPER + u * KCHUNK, KCHUNK)
            d = jnp.dot(ki_ref[0, pl.ds(r0, KCHUNK), :], rhs_i,
                        preferred_element_type=_F32)
            acc = w[0:1, :] * jnp.maximum(d[:, 0:TQ], 0.0)
            for h in range(1, IDX_HEADS):
                acc = acc + w[h:h + 1, :] * jnp.maximum(d[:, h * TQ:(h + 1) * TQ], 0.0)
            bits = pltpu.bitcast(acc, jnp.int32)
            key = jnp.where(bits < 0, bits ^ jnp.int32(0x7FFFFFFF), bits)
            causal = s_minus_t <= (i * TQ - r0)
            keys_scr[pl.ds(r0, KCHUNK), :] = jnp.where(causal, key, jnp.int32(INT_MIN))
        return carry

    lax.fori_loop(0, ns, index_body, 0)

    def count(pred_fn):
        def body(c, acc):
            r0 = pl.multiple_of(c * KSUPER, KSUPER)
            return acc + _colsum8(pred_fn(keys_scr[pl.ds(r0, KSUPER), :], r0))
        acc = lax.fori_loop(0, ns, body, jnp.zeros((8, TQ), jnp.int32))
        return jnp.sum(acc, axis=0, keepdims=True)

    @pl.when(i * TQ + TQ <= top_k)
    def _():
        thr_scr[...] = jnp.full((1, TQ), INT_MIN, jnp.int32)
        tie_scr[...] = jnp.full((1, TQ), -1, jnp.int32)

    @pl.when(i * TQ + TQ > top_k)
    def _():
        def bit_body(b, t_u):
            cand_u = t_u | lax.shift_left(jnp.int32(1), 31 - b)
            cand = cand_u ^ jnp.int32(INT_MIN)
            cnt = count(lambda k, r0: _as01(k >= cand))
            return jnp.where(cnt >= top_k, cand_u, t_u)

        t_u = lax.fori_loop(0, 32, bit_body, jnp.zeros((1, TQ), jnp.int32))
        thr = t_u ^ jnp.int32(INT_MIN)
        cnt_gt = count(lambda k, r0: _as01(k > thr))
        cnt_ge = count(lambda k, r0: _as01(k >= thr))
        need = top_k - cnt_gt
        thr_scr[...] = thr
        tie_scr[...] = jnp.full((1, TQ), 2 ** 30, jnp.int32)
        row = lax.broadcasted_iota(jnp.int32, (KSUPER, TQ), 0)

        @pl.when(jnp.max(cnt_ge) > top_k)
        def _():
            def tie_body(b, lo):
                cand = lo | lax.shift_left(jnp.int32(1), 10 - b)
                cnt = count(lambda k, r0: jnp.where(k == thr, _as01(row < cand - r0), 0))
                return jnp.where(cnt < need, cand, lo)
            tie_scr[...] = lax.fori_loop(0, 11, tie_body, jnp.zeros((1, TQ), jnp.int32))

    thr = thr_scr[...]
    tie = tie_scr[...]
    row = lax.broadcasted_iota(jnp.int32, (KSUPER, TQ), 0)

    def mask_body(c, carry):
        r0 = pl.multiple_of(c * KSUPER, KSUPER)
        k = keys_scr[pl.ds(r0, KSUPER), :]
        mb_scr[pl.ds(r0, KSUPER), :] = jnp.where(
            k == thr, jnp.where(row <= tie - r0, 0.0, NEG), jnp.where(k > thr, 0.0, NEG))
        return carry

    lax.fori_loop(0, ns, mask_body, 0)

    zeros = jnp.zeros((HEAD_DIM, TQ), _BF16)
    n_pairs = B_HEADS // 2
    rhs = []
    for p in range(n_pairs):
        q0 = qb_ref[0, (2 * p) * HEAD_DIM:(2 * p + 1) * HEAD_DIM, :]
        q1 = qb_ref[0, (2 * p + 1) * HEAD_DIM:(2 * p + 2) * HEAD_DIM, :]
        rhs.append(jnp.concatenate([jnp.concatenate([q0, zeros], axis=0),
                                    jnp.concatenate([zeros, q1], axis=0)], axis=1))

    def score_body(c, m):
        r0 = pl.multiple_of(c * KCHUNK, KCHUNK)
        new_m = []
        for p in range(n_pairs):
            s = jnp.dot(kb_ref[0, pl.ds(r0, KCHUNK), p * LANES:(p + 1) * LANES], rhs[p],
                        preferred_element_type=_F32)
            for hh in range(2):
                h = 2 * p + hh
                parts = []
                for half in range(KCHUNK // TQ):
                    jb = c * (KCHUNK // TQ) + half
                    kind = jnp.clip(jb - i + 2, 0, 2)
                    rs = pl.multiple_of(r0 + half * TQ, TQ)
                    parts.append(s[half * TQ:(half + 1) * TQ, hh * TQ:(hh + 1) * TQ]
                                 + bias_ref[h, kind] + mb_scr[pl.ds(rs, TQ), :])
                sh = jnp.concatenate(parts, axis=0)
                sc_scr[pl.ds(r0, KCHUNK), h * TQ:(h + 1) * TQ] = sh
                new_m.append(jnp.maximum(m[h], _colmax8(sh)))
        return tuple(new_m)

    m8 = lax.fori_loop(0, nc, score_body,
                       tuple(jnp.full((8, TQ), NEG, _F32) for _ in range(B_HEADS)))
    m_row = jnp.concatenate([jnp.max(m, axis=0, keepdims=True) for m in m8], axis=1)
    acc_scr[...] = jnp.zeros_like(acc_scr)

    def pv_body(c, l8):
        r0 = pl.multiple_of(c * KCHUNK, KCHUNK)
        new_l = []
        for p in range(n_pairs):
            cols = slice(2 * p * TQ, (2 * p + 2) * TQ)
            pr = jnp.exp(sc_scr[pl.ds(r0, KCHUNK), cols] - m_row[:, cols])
            acc_scr[p] += jnp.dot(vb_ref[0, c, p * LANES:(p + 1) * LANES, :], pr.astype(_BF16),
                                  preferred_element_type=_F32)
            new_l.append(l8[p] + _colsum8(pr))
        return tuple(new_l)

    l8 = lax.fori_loop(0, nc, pv_body,
                       tuple(jnp.zeros((8, 2 * TQ), _F32) for _ in range(n_pairs)))
    for p in range(n_pairs):
        inv = 1.0 / jnp.sum(l8[p], axis=0, keepdims=True)
        for hh in range(2):
            rows = slice((2 * p + hh) * HEAD_DIM, (2 * p + hh + 1) * HEAD_DIM)
            o = (acc_scr[p, hh * HEAD_DIM:(hh + 1) * HEAD_DIM, hh * TQ:(hh + 1) * TQ]
                 * inv[:, hh * TQ:(hh + 1) * TQ] * zb_ref[0, rows, :].astype(_F32))
            o_ref[0, rows, :] = o.astype(_BF16)


def _mix_b(qb_t, kb_n, vb_c, zb_t, qi_t, ki_n, wi_t, bias_b):
    B, _, S = qb_t.shape
    nq = S // TQ
    per_q = lambda rows: pl.BlockSpec((1, rows, TQ), lambda b, i: (b, 0, i))
    in_specs = [
        per_q(B_WIDTH),
        pl.BlockSpec((1, S, B_WIDTH), lambda b, i: (b, 0, 0)),
        pl.BlockSpec((1, S // KCHUNK, B_WIDTH, KCHUNK), lambda b, i: (b, 0, 0, 0)),
        per_q(B_WIDTH),
        per_q(IDX_WIDTH),
        pl.BlockSpec((1, S, LANES), lambda b, i: (b, 0, 0)),
        per_q(IDX_HEADS),
        pl.BlockSpec((B_HEADS, 3, TQ, TQ), lambda b, i: (0, 0, 0, 0)),
    ]
    scratch = [
        pltpu.VMEM((S, TQ), jnp.int32),
        pltpu.VMEM((S, TQ), _F32),
        pltpu.VMEM((S, B_HEADS * TQ), _F32),
        pltpu.VMEM((B_HEADS // 2, LANES, 2 * TQ), _F32),
        pltpu.VMEM((1, TQ), jnp.int32),
        pltpu.VMEM((1, TQ), jnp.int32),
    ]
    return pl.pallas_call(
        _mix_b_kernel, grid=(B, nq), in_specs=in_specs,
        out_specs=per_q(B_WIDTH),
        out_shape=jax.ShapeDtypeStruct((B, B_WIDTH, S), _BF16),
        scratch_shapes=scratch,
        compiler_params=pltpu.CompilerParams(
            dimension_semantics=("parallel", "arbitrary"), vmem_limit_bytes=VMEM_LIMIT),
        name="mixer_b",
    )(qb_t, kb_n, vb_c, zb_t, qi_t, ki_n, wi_t, bias_b)


def _merge_kernel(x_ref, ya_ref, yb_ref, gt_ref, wpa_ref, wpb_ref, wo_ref, o_ref):
    pa = jnp.dot(wpa_ref[...], ya_ref[0], preferred_element_type=_F32)
    pb = jnp.dot(wpb_ref[...], yb_ref[0], preferred_element_type=_F32)
    merged = (gt_ref[0, :D_MODEL, :].astype(_F32) * pa
              + gt_ref[0, D_MODEL:, :].astype(_F32) * pb).astype(_BF16)
    out_t = jnp.dot(wo_ref[...], merged, preferred_element_type=_F32)
    o_ref[0] = x_ref[0] + out_t.T


def _merge(x, ya_t, yb_t, gt_t, wpa_t, wpb_t, wo_t):
    B, S, D = x.shape
    tm = TM_PROJ
    const = lambda b, t: (0, 0)
    in_specs = [
        pl.BlockSpec((1, tm, D), lambda b, t: (b, t, 0)),
        pl.BlockSpec((1, A_WIDTH, tm), lambda b, t: (b, 0, t)),
        pl.BlockSpec((1, B_WIDTH, tm), lambda b, t: (b, 0, t)),
        pl.BlockSpec((1, 2 * D, tm), lambda b, t: (b, 0, t)),
        pl.BlockSpec((D, A_WIDTH), const),
        pl.BlockSpec((D, B_WIDTH), const),
        pl.BlockSpec((D, D), const),
    ]
    return pl.pallas_call(
        _merge_kernel, grid=(B, S // tm), in_specs=in_specs,
        out_specs=pl.BlockSpec((1, tm, D), lambda b, t: (b, t, 0)),
        out_shape=jax.ShapeDtypeStruct((B, S, D), x.dtype),
        compiler_params=pltpu.CompilerParams(
            dimension_semantics=("parallel", "parallel"), vmem_limit_bytes=VMEM_LIMIT),
        name="merge_out",
    )(x, ya_t, yb_t, gt_t, wpa_t, wpb_t, wo_t)


def _t5_bucket(n):
    n = jnp.maximum(n, 0)
    max_exact = N_BUCKETS // 2
    nf = jnp.maximum(n, 1).astype(jnp.float32)
    large = max_exact + (jnp.log(nf / max_exact) / math.log(MAX_DISTANCE / max_exact)
                         * (N_BUCKETS - max_exact)).astype(jnp.int32)
    large = jnp.minimum(large, N_BUCKETS - 1)
    return jnp.where(n < max_exact, n, large)


def _bias_tables(rel_bias):
    t = jnp.arange(TQ)[None, :]
    s = jnp.arange(TQ)[:, None]
    d_prev = t + TQ - s
    d_cur = t - s
    table_a = rel_bias[:, :A_Q_HEADS].astype(_F32).T
    table_b = rel_bias[:, A_Q_HEADS:].astype(_F32).T

    def tile(table, dist, windowed):
        onehot = (_t5_bucket(dist)[..., None] == jnp.arange(N_BUCKETS)).astype(_F32)
        b = jnp.einsum("stk,hk->hst", onehot, table, precision=lax.Precision.HIGHEST)
        if windowed:
            b = jnp.where(((dist >= 0) & (dist < WINDOW))[None], b, NEG)
        return b

    bias_a = jnp.concatenate([tile(table_a, d_prev, True), tile(table_a, d_cur, True)], axis=1)
    far = jnp.broadcast_to(table_b[:, N_BUCKETS - 1][:, None, None], (B_HEADS, TQ, TQ))
    bias_b = jnp.stack([far, tile(table_b, d_prev, False), tile(table_b, d_cur, False)], axis=1)
    return bias_a, bias_b


def kernel(x, norm_g, w_in, qnorm_a, knorm_a, sinks_a, qnorm_b, knorm_b, rel_bias,
           w_proj_a, w_proj_b, w_out):
    assert norm_g.shape[0] == 1, "single-layer block"
    B, S, D = x.shape
    assert D == D_MODEL and S % TM_PROJ == 0 and S % KSUPER == 0
    assert MAX_DISTANCE <= TQ

    wt = w_in[0].T
    n_real = _SEG["kw"][0] + KW_REAL
    wt = jnp.concatenate(
        [wt[:n_real], jnp.zeros((_SEG["kw"][1] - n_real, D), wt.dtype), wt[n_real:]], axis=0)
    wt = wt.astype(_BF16)
    bcast = lambda g, scale: jnp.broadcast_to((g.astype(_F32) * scale)[:, None], (HEAD_DIM, TM_PROJ))
    q_scale = HEAD_DIM ** -0.5
    (qa_t, ka_n, va_t, za_t, qb_t, kb_n, vb_c, zb_t, qi_t, ki_n, wi_t, gt_t) = _inproj(
        x, norm_g[0][None, :].astype(_F32), wt,
        bcast(qnorm_a[0], q_scale), bcast(knorm_a[0], 1.0),
        bcast(qnorm_b[0], q_scale), bcast(knorm_b[0], 1.0))

    bias_a, bias_b = _bias_tables(rel_bias)
    sink_a = jnp.broadcast_to(sinks_a[0].astype(_F32)[:, None], (A_Q_HEADS, TQ))
    ya_t = _mix_a(qa_t, ka_n, va_t, za_t, bias_a, sink_a)
    yb_t = _mix_b(qb_t, kb_n, vb_c, zb_t, qi_t, ki_n, wi_t, bias_b)
    return _merge(x, ya_t, yb_t, gt_t,
                  w_proj_a[0].T.astype(_BF16), w_proj_b[0].T.astype(_BF16), w_out[0].T.astype(_BF16))
```

```python
import functools
import math

import jax
import jax.numpy as jnp
import numpy as np
from jax import lax
from jax.experimental import pallas as pl
from jax.experimental.pallas import tpu as pltpu

D_MODEL = 1024
HEAD_DIM = 64
A_Q_HEADS = 8
A_KV_HEADS = 2
A_GROUPS = A_Q_HEADS // A_KV_HEADS
A_WIDTH = A_Q_HEADS * HEAD_DIM
A_KV_WIDTH = A_KV_HEADS * HEAD_DIM
B_HEADS = 8
B_WIDTH = B_HEADS * HEAD_DIM
IDX_HEADS = 8
IDX_DIM = 32
IDX_WIDTH = IDX_HEADS * IDX_DIM
WINDOW = 128
TOPK_MAX = 256
N_BUCKETS = 32
MAX_DISTANCE = 128
RMS_EPS = 1e-6

LANES = 128
TQ = 128
KCHUNK = 256
KSUPER = 2 * KCHUNK
TM_PROJ = 512
NEG = -1e30
LOG2E = math.log2(math.e)
ONES_ROWS = 16
PAIR_ROWS = 2 * HEAD_DIM + ONES_ROWS
V_ROWS = (B_HEADS // 2) * PAIR_ROWS
INT_MIN = -(2 ** 31)
VMEM_LIMIT = 56 * 1024 * 1024

_SEG = {}
_off = 0
for _name, _rows in (("qa", A_WIDTH), ("ka", A_KV_WIDTH), ("va", A_KV_WIDTH), ("za", A_WIDTH),
                     ("qb", B_WIDTH), ("kb", B_WIDTH), ("vb", B_WIDTH), ("zb", B_WIDTH),
                     ("qi", IDX_WIDTH), ("kw", LANES), ("gates", 2 * D_MODEL)):
    _SEG[_name] = (_off, _off + _rows)
    _off += _rows
PROJ_ROWS = _off
KW_REAL = IDX_DIM + IDX_HEADS

_F32 = jnp.float32
_BF16 = jnp.bfloat16
_NT = (((1,), (1,)), ((), ()))


def _inproj_kernel(x_ref, g_ref, wt_ref, gqa_ref, gka_ref, gqb_ref, gkb_ref,
                   qa_o, ka_o, va_o, za_o, qb_o, kb_o, vb_o, zb_o, qi_o, ki_o, wi_o, gt_o):
    tm = x_ref.shape[1]
    x = x_ref[0]
    ms = jnp.mean(x * x, axis=-1, keepdims=True)
    h = (x * lax.rsqrt(ms + RMS_EPS) * g_ref[...]).astype(_BF16)

    def proj(lo, hi):
        return lax.dot_general(wt_ref[lo:hi, :], h, _NT, preferred_element_type=_F32)

    def seg(name):
        return proj(*_SEG[name])

    def headnorm(p, gain_ref):
        nh = p.shape[0] // HEAD_DIM
        p3 = p.reshape(nh, HEAD_DIM, tm)
        r = lax.rsqrt(jnp.mean(p3 * p3, axis=1, keepdims=True) + RMS_EPS)
        return (p3 * r * gain_ref[...][None]).reshape(nh * HEAD_DIM, tm)

    def silu(p):
        return p * jax.nn.sigmoid(p)

    qa_o[0] = headnorm(seg("qa"), gqa_ref).astype(_BF16)
    ka_o[0] = headnorm(seg("ka"), gka_ref).T.astype(_BF16)
    va_o[0] = seg("va").astype(_BF16)
    za_o[0] = silu(seg("za")).astype(_BF16)
    qb_o[0] = headnorm(seg("qb"), gqb_ref).astype(_BF16)
    kb_o[0] = headnorm(seg("kb"), gkb_ref).T.astype(_BF16)
    vb = seg("vb").astype(_BF16)
    ones = jnp.ones((ONES_ROWS, KCHUNK), _BF16)
    for c in range(tm // KCHUNK):
        for p in range(B_HEADS // 2):
            vb_o[0, c, p * PAIR_ROWS:(p + 1) * PAIR_ROWS, :] = jnp.concatenate(
                [vb[p * LANES:(p + 1) * LANES, c * KCHUNK:(c + 1) * KCHUNK], ones], axis=0)
    zb_o[0] = silu(seg("zb")).astype(_BF16)
    qi_o[0] = seg("qi").astype(_BF16)
    kw = seg("kw")
    wi_o[0] = kw[IDX_DIM:KW_REAL, :]
    row = lax.broadcasted_iota(jnp.int32, kw.shape, 0)
    ki_o[0] = jnp.where(row < IDX_DIM, kw, 0.0).T.astype(_BF16)
    g_lo = _SEG["gates"][0]
    for c in range(2 * D_MODEL // 512):
        gt_o[0, c * 512:(c + 1) * 512, :] = jax.nn.sigmoid(
            proj(g_lo + c * 512, g_lo + (c + 1) * 512)).astype(_BF16)


def _inproj(x, g, wt, gqa, gka, gqb, gkb):
    B, S, D = x.shape
    tm = TM_PROJ
    nt = S // tm
    const = lambda b, t: (0, 0)
    tr = lambda rows: pl.BlockSpec((1, rows, tm), lambda b, t: (b, 0, t))
    nat = lambda cols: pl.BlockSpec((1, tm, cols), lambda b, t: (b, t, 0))
    sds = jax.ShapeDtypeStruct
    out_shape = (
        sds((B, A_WIDTH, S), _BF16), sds((B, S, A_KV_WIDTH), _BF16), sds((B, A_KV_WIDTH, S), _BF16),
        sds((B, A_WIDTH, S), _BF16),
        sds((B, B_WIDTH, S), _BF16), sds((B, S, B_WIDTH), _BF16),
        sds((B, S // KCHUNK, V_ROWS, KCHUNK), _BF16), sds((B, B_WIDTH, S), _BF16),
        sds((B, IDX_WIDTH, S), _BF16), sds((B, S, LANES), _BF16), sds((B, IDX_HEADS, S), _F32),
        sds((B, 2 * D_MODEL, S), _BF16),
    )
    out_specs = (
        tr(A_WIDTH), nat(A_KV_WIDTH), tr(A_KV_WIDTH), tr(A_WIDTH),
        tr(B_WIDTH), nat(B_WIDTH),
        pl.BlockSpec((1, tm // KCHUNK, V_ROWS, KCHUNK), lambda b, t: (b, t, 0, 0)), tr(B_WIDTH),
        tr(IDX_WIDTH), nat(LANES), tr(IDX_HEADS), tr(2 * D_MODEL),
    )
    in_specs = [
        pl.BlockSpec((1, tm, D), lambda b, t: (b, t, 0)),
        pl.BlockSpec((1, D), const),
        pl.BlockSpec((PROJ_ROWS, D), const, pipeline_mode=pl.Buffered(1)),
        pl.BlockSpec((HEAD_DIM, tm), const), pl.BlockSpec((HEAD_DIM, tm), const),
        pl.BlockSpec((HEAD_DIM, tm), const), pl.BlockSpec((HEAD_DIM, tm), const),
    ]
    return pl.pallas_call(
        _inproj_kernel, grid=(B, nt), in_specs=in_specs, out_specs=out_specs, out_shape=out_shape,
        compiler_params=pltpu.CompilerParams(
            dimension_semantics=("parallel", "parallel"), vmem_limit_bytes=VMEM_LIMIT),
        name="inproj",
    )(x, g, wt, gqa, gka, gqb, gkb)


def _mix_a_kernel(q_ref, kp_ref, kc_ref, vp_ref, vc_ref, z_ref, bias_ref, sink_ref, o_ref):
    i = pl.program_id(1)
    q = q_ref[0]
    kband = jnp.concatenate([kp_ref[0], kc_ref[0]], axis=0)
    vband = jnp.concatenate([vp_ref[0], vc_ref[0]], axis=1)
    zeros = jnp.zeros((HEAD_DIM, TQ), _BF16)
    has_prev = i > 0
    for g in range(A_KV_HEADS):
        cols = []
        for hh in range(A_GROUPS):
            h = g * A_GROUPS + hh
            qh = q[h * HEAD_DIM:(h + 1) * HEAD_DIM]
            parts = [zeros] * A_KV_HEADS
            parts[g] = qh
            cols.append(jnp.concatenate(parts, axis=0))
        rhs = jnp.concatenate(cols, axis=1)
        sc = jnp.dot(kband, rhs, preferred_element_type=_F32)
        probs, inv = [], []
        for hh in range(A_GROUPS):
            h = g * A_GROUPS + hh
            s = sc[:, hh * TQ:(hh + 1) * TQ] + bias_ref[h]
            s = jnp.concatenate([jnp.where(has_prev, s[:TQ], NEG), s[TQ:]], axis=0)
            sink = sink_ref[h:h + 1, :]
            m = jnp.maximum(jnp.max(s, axis=0, keepdims=True), sink)
            p = jnp.exp(s - m)
            denom = jnp.sum(p, axis=0, keepdims=True) + jnp.exp(sink - m)
            probs.append(p.astype(_BF16))
            inv.append(1.0 / denom)
        pt = jnp.concatenate(probs, axis=1)
        out = jnp.dot(vband[g * HEAD_DIM:(g + 1) * HEAD_DIM, :], pt,
                      preferred_element_type=_F32)
        for hh in range(A_GROUPS):
            h = g * A_GROUPS + hh
            rows = slice(h * HEAD_DIM, (h + 1) * HEAD_DIM)
            o = out[:, hh * TQ:(hh + 1) * TQ] * inv[hh] * z_ref[0, rows, :].astype(_F32)
            o_ref[0, rows, :] = o.astype(_BF16)


def _mix_a(qa_t, ka_n, va_t, za_t, bias_a, sink_a):
    B, _, S = qa_t.shape
    nq = S // TQ
    prev = lambda i: jnp.maximum(i - 1, 0)
    in_specs = [
        pl.BlockSpec((1, A_WIDTH, TQ), lambda b, i: (b, 0, i)),
        pl.BlockSpec((1, TQ, A_KV_WIDTH), lambda b, i: (b, prev(i), 0)),
        pl.BlockSpec((1, TQ, A_KV_WIDTH), lambda b, i: (b, i, 0)),
        pl.BlockSpec((1, A_KV_WIDTH, TQ), lambda b, i: (b, 0, prev(i))),
        pl.BlockSpec((1, A_KV_WIDTH, TQ), lambda b, i: (b, 0, i)),
        pl.BlockSpec((1, A_WIDTH, TQ), lambda b, i: (b, 0, i)),
        pl.BlockSpec((A_Q_HEADS, 2 * TQ, TQ), lambda b, i: (0, 0, 0)),
        pl.BlockSpec((A_Q_HEADS, TQ), lambda b, i: (0, 0)),
    ]
    return pl.pallas_call(
        _mix_a_kernel, grid=(B, nq), in_specs=in_specs,
        out_specs=pl.BlockSpec((1, A_WIDTH, TQ), lambda b, i: (b, 0, i)),
        out_shape=jax.ShapeDtypeStruct((B, A_WIDTH, S), _BF16),
        compiler_params=pltpu.CompilerParams(
            dimension_semantics=("parallel", "parallel"), vmem_limit_bytes=VMEM_LIMIT),
        name="mixer_a",
    )(qa_t, ka_n, ka_n, va_t, va_t, za_t, bias_a, sink_a)


def _colsum8(v):
    return jnp.sum(v.reshape(v.shape[0] // 8, 8, v.shape[1]), axis=0)


def _as01(pred):
    return jnp.where(pred, 1, 0).astype(jnp.int32)


def _colmax8(v):
    return jnp.max(v.reshape(v.shape[0] // 8, 8, v.shape[1]), axis=0)


def _by_pairs(n, step):
    def body(t, carry):
        step(2 * t)
        step(2 * t + 1)
        return carry

    lax.fori_loop(0, n // 2, body, 0)

    @pl.when(n % 2 == 1)
    def _():
        step(n - 1)


def _mix_b_kernel(qb_ref, kb_ref, vb_ref, zb_ref, qi_ref, ki_ref, wi_ref, bias_ref, o_ref,
                  keys_scr, mb_scr, sc_scr, acc_scr, m_scr, thr_scr, tie_scr):
    i = pl.program_id(1)
    nc = (i + 2) // 2
    ns = (nc + 1) // 2
    top_k = TOPK_MAX

    qi = qi_ref[0]
    zpad = jnp.zeros((LANES - IDX_DIM, TQ), _BF16)
    rhs_i = jnp.concatenate(
        [jnp.concatenate([qi[h * IDX_DIM:(h + 1) * IDX_DIM], zpad], axis=0)
         for h in range(IDX_HEADS)], axis=1)
    w = wi_ref[0] * (IDX_DIM ** -0.5 * IDX_HEADS ** -0.5)
    s_minus_t = (lax.broadcasted_iota(jnp.int32, (KCHUNK, TQ), 0)
                 - lax.broadcasted_iota(jnp.int32, (KCHUNK, TQ), 1))

    def index_step(c):
        r0 = pl.multiple_of(c * KCHUNK, KCHUNK)
        d = jnp.dot(ki_ref[0, pl.ds(r0, KCHUNK), :], rhs_i,
                    preferred_element_type=_F32)
        acc = w[0:1, :] * jnp.maximum(d[:, 0:TQ], 0.0)
        for h in range(1, IDX_HEADS):
            acc = acc + w[h:h + 1, :] * jnp.maximum(d[:, h * TQ:(h + 1) * TQ], 0.0)
        bits = pltpu.bitcast(acc, jnp.int32)
        key = jnp.where(bits < 0, bits ^ jnp.int32(0x7FFFFFFF), bits)
        causal = s_minus_t <= (i * TQ - r0)
        keys_scr[pl.ds(r0, KCHUNK), :] = jnp.where(causal, key, jnp.int32(INT_MIN))

    _by_pairs(nc, index_step)

    @pl.when(nc % 2 == 1)
    def _():
        keys_scr[pl.ds(pl.multiple_of(nc * KCHUNK, KCHUNK), KCHUNK), :] = jnp.full(
            (KCHUNK, TQ), INT_MIN, jnp.int32)

    def count(pred_fn):
        def body(c, acc):
            r0 = pl.multiple_of(c * KSUPER, KSUPER)
            return acc + _colsum8(pred_fn(keys_scr[pl.ds(r0, KSUPER), :], r0))
        acc = lax.fori_loop(0, ns, body, jnp.zeros((8, TQ), jnp.int32))
        return jnp.sum(acc, axis=0, keepdims=True)

    @pl.when(i * TQ + TQ <= top_k)
    def _():
        thr_scr[...] = jnp.full((1, TQ), INT_MIN, jnp.int32)
        tie_scr[...] = jnp.full((1, TQ), -1, jnp.int32)

    @pl.when(i * TQ + TQ > top_k)
    def _():
        def bit_body(b, t_u):
            cand_u = t_u | lax.shift_left(jnp.int32(1), 31 - b)
            cand = cand_u ^ jnp.int32(INT_MIN)
            cnt = count(lambda k, r0: _as01(k >= cand))
            return jnp.where(cnt >= top_k, cand_u, t_u)

        t_u = lax.fori_loop(0, 32, bit_body, jnp.zeros((1, TQ), jnp.int32))
        thr = t_u ^ jnp.int32(INT_MIN)
        cnt_gt = count(lambda k, r0: _as01(k > thr))
        cnt_ge = count(lambda k, r0: _as01(k >= thr))
        need = top_k - cnt_gt
        thr_scr[...] = thr
        tie_scr[...] = jnp.full((1, TQ), 2 ** 30, jnp.int32)
        row = lax.broadcasted_iota(jnp.int32, (KSUPER, TQ), 0)

        @pl.when(jnp.max(cnt_ge) > top_k)
        def _():
            def tie_body(b, lo):
                cand = lo | lax.shift_left(jnp.int32(1), 10 - b)
                cnt = count(lambda k, r0: jnp.where(k == thr, _as01(row < cand - r0), 0))
                return jnp.where(cnt < need, cand, lo)
            tie_scr[...] = lax.fori_loop(0, 11, tie_body, jnp.zeros((1, TQ), jnp.int32))

    thr = thr_scr[...]
    tie = tie_scr[...]
    row = lax.broadcasted_iota(jnp.int32, (KSUPER, TQ), 0)

    def mask_body(c, carry):
        r0 = pl.multiple_of(c * KSUPER, KSUPER)
        k = keys_scr[pl.ds(r0, KSUPER), :]
        mb_scr[pl.ds(r0, KSUPER), :] = jnp.where(
            k == thr, jnp.where(row <= tie - r0, 0.0, NEG), jnp.where(k > thr, 0.0, NEG))
        return carry

    lax.fori_loop(0, ns, mask_body, 0)

    zeros = jnp.zeros((HEAD_DIM, TQ), _BF16)
    n_pairs = B_HEADS // 2
    rhs = []
    for p in range(n_pairs):
        q0 = qb_ref[0, (2 * p) * HEAD_DIM:(2 * p + 1) * HEAD_DIM, :]
        q1 = qb_ref[0, (2 * p + 1) * HEAD_DIM:(2 * p + 2) * HEAD_DIM, :]
        rhs.append(jnp.concatenate([jnp.concatenate([q0, zeros], axis=0),
                                    jnp.concatenate([zeros, q1], axis=0)], axis=1))

    def score_step(c):
        r0 = pl.multiple_of(c * KCHUNK, KCHUNK)
        for p in range(n_pairs):
            s = jnp.dot(kb_ref[0, pl.ds(r0, KCHUNK), p * LANES:(p + 1) * LANES], rhs[p],
                        preferred_element_type=_F32)
            for hh in range(2):
                h = 2 * p + hh
                parts = []
                for half in range(KCHUNK // TQ):
                    jb = c * (KCHUNK // TQ) + half
                    kind = jnp.clip(jb - i + 2, 0, 2)
                    rs = pl.multiple_of(r0 + half * TQ, TQ)
                    parts.append(s[half * TQ:(half + 1) * TQ, hh * TQ:(hh + 1) * TQ]
                                 + bias_ref[h, kind] + mb_scr[pl.ds(rs, TQ), :])
                sh = jnp.concatenate(parts, axis=0)
                sc_scr[pl.ds(r0, KCHUNK), h * TQ:(h + 1) * TQ] = sh
                m_scr[h * 8:(h + 1) * 8, :] = jnp.maximum(m_scr[h * 8:(h + 1) * 8, :], _colmax8(sh))

    m_scr[...] = jnp.full(m_scr.shape, NEG, _F32)
    _by_pairs(nc, score_step)
    m_row = jnp.concatenate(
        [jnp.max(m_scr[h * 8:(h + 1) * 8, :], axis=0, keepdims=True) for h in range(B_HEADS)],
        axis=1)
    acc_scr[...] = jnp.zeros_like(acc_scr)

    def pv_step(c):
        r0 = pl.multiple_of(c * KCHUNK, KCHUNK)
        for p in range(n_pairs):
            cols = slice(2 * p * TQ, (2 * p + 2) * TQ)
            pr = jnp.exp2(sc_scr[pl.ds(r0, KCHUNK), cols] - m_row[:, cols]).astype(_BF16)
            acc_scr[p] += jnp.dot(vb_ref[0, c, p * PAIR_ROWS:(p + 1) * PAIR_ROWS, :], pr,
                                  preferred_element_type=_F32)

    _by_pairs(nc, pv_step)
    for p in range(n_pairs):
        inv = 1.0 / acc_scr[p, 2 * HEAD_DIM:2 * HEAD_DIM + 1, :]
        for hh in range(2):
            rows = slice((2 * p + hh) * HEAD_DIM, (2 * p + hh + 1) * HEAD_DIM)
            o = (acc_scr[p, hh * HEAD_DIM:(hh + 1) * HEAD_DIM, hh * TQ:(hh + 1) * TQ]
                 * inv[:, hh * TQ:(hh + 1) * TQ] * zb_ref[0, rows, :].astype(_F32))
            o_ref[0, rows, :] = o.astype(_BF16)


def _mix_b(qb_t, kb_n, vb_c, zb_t, qi_t, ki_n, wi_t, bias_b):
    B, _, S = qb_t.shape
    nq = S // TQ
    per_q = lambda rows: pl.BlockSpec((1, rows, TQ), lambda b, i: (b, 0, i))
    in_specs = [
        per_q(B_WIDTH),
        pl.BlockSpec((1, S, B_WIDTH), lambda b, i: (b, 0, 0)),
        pl.BlockSpec((1, S // KCHUNK, V_ROWS, KCHUNK), lambda b, i: (b, 0, 0, 0)),
        per_q(B_WIDTH),
        per_q(IDX_WIDTH),
        pl.BlockSpec((1, S, LANES), lambda b, i: (b, 0, 0)),
        per_q(IDX_HEADS),
        pl.BlockSpec((B_HEADS, 3, TQ, TQ), lambda b, i: (0, 0, 0, 0)),
    ]
    scratch = [
        pltpu.VMEM((S, TQ), jnp.int32),
        pltpu.VMEM((S, TQ), _F32),
        pltpu.VMEM((S, B_HEADS * TQ), _F32),
        pltpu.VMEM((B_HEADS // 2, PAIR_ROWS, 2 * TQ), _F32),
        pltpu.VMEM((B_HEADS * 8, TQ), _F32),
        pltpu.VMEM((1, TQ), jnp.int32),
        pltpu.VMEM((1, TQ), jnp.int32),
    ]
    return pl.pallas_call(
        _mix_b_kernel, grid=(B, nq), in_specs=in_specs,
        out_specs=per_q(B_WIDTH),
        out_shape=jax.ShapeDtypeStruct((B, B_WIDTH, S), _BF16),
        scratch_shapes=scratch,
        compiler_params=pltpu.CompilerParams(
            dimension_semantics=("parallel", "arbitrary"), vmem_limit_bytes=VMEM_LIMIT),
        name="mixer_b",
    )(qb_t, kb_n, vb_c, zb_t, qi_t, ki_n, wi_t, bias_b)


def _merge_kernel(x_ref, ya_ref, yb_ref, gt_ref, wpa_ref, wpb_ref, wo_ref, o_ref):
    pa = jnp.dot(wpa_ref[...], ya_ref[0], preferred_element_type=_F32)
    pb = jnp.dot(wpb_ref[...], yb_ref[0], preferred_element_type=_F32)
    merged = (gt_ref[0, :D_MODEL, :].astype(_F32) * pa
              + gt_ref[0, D_MODEL:, :].astype(_F32) * pb).astype(_BF16)
    out_t = jnp.dot(wo_ref[...], merged, preferred_element_type=_F32)
    o_ref[0] = x_ref[0] + out_t.T


def _merge(x, ya_t, yb_t, gt_t, wpa_t, wpb_t, wo_t):
    B, S, D = x.shape
    tm = TM_PROJ
    const = lambda b, t: (0, 0)
    in_specs = [
        pl.BlockSpec((1, tm, D), lambda b, t: (b, t, 0)),
        pl.BlockSpec((1, A_WIDTH, tm), lambda b, t: (b, 0, t)),
        pl.BlockSpec((1, B_WIDTH, tm), lambda b, t: (b, 0, t)),
        pl.BlockSpec((1, 2 * D, tm), lambda b, t: (b, 0, t)),
        pl.BlockSpec((D, A_WIDTH), const),
        pl.BlockSpec((D, B_WIDTH), const),
        pl.BlockSpec((D, D), const),
    ]
    return pl.pallas_call(
        _merge_kernel, grid=(B, S // tm), in_specs=in_specs,
        out_specs=pl.BlockSpec((1, tm, D), lambda b, t: (b, t, 0)),
        out_shape=jax.ShapeDtypeStruct((B, S, D), x.dtype),
        compiler_params=pltpu.CompilerParams(
            dimension_semantics=("parallel", "parallel"), vmem_limit_bytes=VMEM_LIMIT),
        name="merge_out",
    )(x, ya_t, yb_t, gt_t, wpa_t, wpb_t, wo_t)


def _t5_bucket(n):
    n = jnp.maximum(n, 0)
    max_exact = N_BUCKETS // 2
    nf = jnp.maximum(n, 1).astype(jnp.float32)
    large = max_exact + (jnp.log(nf / max_exact) / math.log(MAX_DISTANCE / max_exact)
                         * (N_BUCKETS - max_exact)).astype(jnp.int32)
    large = jnp.minimum(large, N_BUCKETS - 1)
    return jnp.where(n < max_exact, n, large)


def _bias_tables(rel_bias):
    t = jnp.arange(TQ)[None, :]
    s = jnp.arange(TQ)[:, None]
    d_prev = t + TQ - s
    d_cur = t - s
    table_a = rel_bias[:, :A_Q_HEADS].astype(_F32).T
    table_b = rel_bias[:, A_Q_HEADS:].astype(_F32).T

    def tile(table, dist, windowed):
        onehot = (_t5_bucket(dist)[..., None] == jnp.arange(N_BUCKETS)).astype(_F32)
        b = jnp.einsum("stk,hk->hst", onehot, table, precision=lax.Precision.HIGHEST)
        if windowed:
            b = jnp.where(((dist >= 0) & (dist < WINDOW))[None], b, NEG)
        return b

    bias_a = jnp.concatenate([tile(table_a, d_prev, True), tile(table_a, d_cur, True)], axis=1)
    far = jnp.broadcast_to(table_b[:, N_BUCKETS - 1][:, None, None], (B_HEADS, TQ, TQ))
    bias_b = jnp.stack([far, tile(table_b, d_prev, False), tile(table_b, d_cur, False)], axis=1) * LOG2E
    return bias_a, bias_b


def kernel(x, norm_g, w_in, qnorm_a, knorm_a, sinks_a, qnorm_b, knorm_b, rel_bias,
           w_proj_a, w_proj_b, w_out):
    assert norm_g.shape[0] == 1, "single-layer block"
    B, S, D = x.shape
    assert D == D_MODEL and S % TM_PROJ == 0 and S % KSUPER == 0
    assert MAX_DISTANCE <= TQ

    wt = w_in[0].T
    n_real = _SEG["kw"][0] + KW_REAL
    wt = jnp.concatenate(
        [wt[:n_real], jnp.zeros((_SEG["kw"][1] - n_real, D), wt.dtype), wt[n_real:]], axis=0)
    wt = wt.astype(_BF16)
    bcast = lambda g, scale: jnp.broadcast_to((g.astype(_F32) * scale)[:, None], (HEAD_DIM, TM_PROJ))
    q_scale = HEAD_DIM ** -0.5
    (qa_t, ka_n, va_t, za_t, qb_t, kb_n, vb_c, zb_t, qi_t, ki_n, wi_t, gt_t) = _inproj(
        x, norm_g[0][None, :].astype(_F32), wt,
        bcast(qnorm_a[0], q_scale), bcast(knorm_a[0], 1.0),
        bcast(qnorm_b[0], q_scale * LOG2E), bcast(knorm_b[0], 1.0))

    bias_a, bias_b = _bias_tables(rel_bias)
    sink_a = jnp.broadcast_to(sinks_a[0].astype(_F32)[:, None], (A_Q_HEADS, TQ))
    ya_t = _mix_a(qa_t, ka_n, va_t, za_t, bias_a, sink_a)
    yb_t = _mix_b(qb_t, kb_n, vb_c, zb_t, qi_t, ki_n, wi_t, bias_b)
    return _merge(x, ya_t, yb_t, gt_t,
                  w_proj_a[0].T.astype(_BF16), w_proj_b[0].T.astype(_BF16), w_out[0].T.astype(_BF16))
```

```python
import functools
import math

import jax
import jax.numpy as jnp
import numpy as np
from jax import lax
from jax.experimental import pallas as pl
from jax.experimental.pallas import tpu as pltpu

D_MODEL = 1024
HEAD_DIM = 64
A_Q_HEADS = 8
A_KV_HEADS = 2
A_GROUPS = A_Q_HEADS // A_KV_HEADS
A_WIDTH = A_Q_HEADS * HEAD_DIM
A_KV_WIDTH = A_KV_HEADS * HEAD_DIM
B_HEADS = 8
B_WIDTH = B_HEADS * HEAD_DIM
IDX_HEADS = 8
IDX_DIM = 32
IDX_WIDTH = IDX_HEADS * IDX_DIM
WINDOW = 128
TOPK_MAX = 256
N_BUCKETS = 32
MAX_DISTANCE = 128
RMS_EPS = 1e-6

LANES = 128
TQ = 128
KCHUNK = 256
KSUPER = 2 * KCHUNK
TM_PROJ = 512
NEG = -1e30
LOG2E = math.log2(math.e)
ONES_ROWS = 16
PAIR_ROWS = 2 * HEAD_DIM + ONES_ROWS
V_ROWS = (B_HEADS // 2) * PAIR_ROWS
INT_MIN = -(2 ** 31)
VMEM_LIMIT = 56 * 1024 * 1024

_SEG = {}
_off = 0
for _name, _rows in (("qa", A_WIDTH), ("ka", A_KV_WIDTH), ("va", A_KV_WIDTH), ("za", A_WIDTH),
                     ("qb", B_WIDTH), ("kb", B_WIDTH), ("vb", B_WIDTH), ("zb", B_WIDTH),
                     ("qi", IDX_WIDTH), ("kw", LANES), ("gates", 2 * D_MODEL)):
    _SEG[_name] = (_off, _off + _rows)
    _off += _rows
PROJ_ROWS = _off
KW_REAL = IDX_DIM + IDX_HEADS

_F32 = jnp.float32
_BF16 = jnp.bfloat16
_NT = (((1,), (1,)), ((), ()))


def _inproj_kernel(x_ref, g_ref, wt_ref, gqa_ref, gka_ref, gqb_ref, gkb_ref,
                   qa_o, ka_o, va_o, za_o, qb_o, kb_o, vb_o, zb_o, qi_o, ki_o, wi_o, gt_o):
    tm = x_ref.shape[1]
    x = x_ref[0]
    ms = jnp.mean(x * x, axis=-1, keepdims=True)
    h = (x * lax.rsqrt(ms + RMS_EPS) * g_ref[...]).astype(_BF16)

    def proj(lo, hi):
        return lax.dot_general(wt_ref[lo:hi, :], h, _NT, preferred_element_type=_F32)

    def seg(name):
        return proj(*_SEG[name])

    def headnorm(p, gain_ref):
        nh = p.shape[0] // HEAD_DIM
        p3 = p.reshape(nh, HEAD_DIM, tm)
        r = lax.rsqrt(jnp.mean(p3 * p3, axis=1, keepdims=True) + RMS_EPS)
        return (p3 * r * gain_ref[...][None]).reshape(nh * HEAD_DIM, tm)

    def silu(p):
        return p * jax.nn.sigmoid(p)

    qa_o[0] = headnorm(seg("qa"), gqa_ref).astype(_BF16)
    ka_o[0] = headnorm(seg("ka"), gka_ref).T.astype(_BF16)
    va_o[0] = seg("va").astype(_BF16)
    za_o[0] = silu(seg("za")).astype(_BF16)
    qb_o[0] = headnorm(seg("qb"), gqb_ref).astype(_BF16)
    kb_o[0] = headnorm(seg("kb"), gkb_ref).T.astype(_BF16)
    vb = seg("vb").astype(_BF16)
    ones = jnp.ones((ONES_ROWS, KCHUNK), _BF16)
    for c in range(tm // KCHUNK):
        for p in range(B_HEADS // 2):
            vb_o[0, c, p * PAIR_ROWS:(p + 1) * PAIR_ROWS, :] = jnp.concatenate(
                [vb[p * LANES:(p + 1) * LANES, c * KCHUNK:(c + 1) * KCHUNK], ones], axis=0)
    zb_o[0] = silu(seg("zb")).astype(_BF16)
    qi_o[0] = seg("qi").astype(_BF16)
    kw = seg("kw")
    wi_o[0] = kw[IDX_DIM:KW_REAL, :]
    row = lax.broadcasted_iota(jnp.int32, kw.shape, 0)
    ki_o[0] = jnp.where(row < IDX_DIM, kw, 0.0).T.astype(_BF16)
    g_lo = _SEG["gates"][0]
    for c in range(2 * D_MODEL // 512):
        gt_o[0, c * 512:(c + 1) * 512, :] = jax.nn.sigmoid(
            proj(g_lo + c * 512, g_lo + (c + 1) * 512)).astype(_BF16)


def _inproj(x, g, wt, gqa, gka, gqb, gkb):
    B, S, D = x.shape
    tm = TM_PROJ
    nt = S // tm
    const = lambda b, t: (0, 0)
    tr = lambda rows: pl.BlockSpec((1, rows, tm), lambda b, t: (b, 0, t))
    nat = lambda cols: pl.BlockSpec((1, tm, cols), lambda b, t: (b, t, 0))
    sds = jax.ShapeDtypeStruct
    out_shape = (
        sds((B, A_WIDTH, S), _BF16), sds((B, S, A_KV_WIDTH), _BF16), sds((B, A_KV_WIDTH, S), _BF16),
        sds((B, A_WIDTH, S), _BF16),
        sds((B, B_WIDTH, S), _BF16), sds((B, S, B_WIDTH), _BF16),
        sds((B, S // KCHUNK, V_ROWS, KCHUNK), _BF16), sds((B, B_WIDTH, S), _BF16),
        sds((B, IDX_WIDTH, S), _BF16), sds((B, S, LANES), _BF16), sds((B, IDX_HEADS, S), _F32),
        sds((B, 2 * D_MODEL, S), _BF16),
    )
    out_specs = (
        tr(A_WIDTH), nat(A_KV_WIDTH), tr(A_KV_WIDTH), tr(A_WIDTH),
        tr(B_WIDTH), nat(B_WIDTH),
        pl.BlockSpec((1, tm // KCHUNK, V_ROWS, KCHUNK), lambda b, t: (b, t, 0, 0)), tr(B_WIDTH),
        tr(IDX_WIDTH), nat(LANES), tr(IDX_HEADS), tr(2 * D_MODEL),
    )
    in_specs = [
        pl.BlockSpec((1, tm, D), lambda b, t: (b, t, 0)),
        pl.BlockSpec((1, D), const),
        pl.BlockSpec((PROJ_ROWS, D), const, pipeline_mode=pl.Buffered(1)),
        pl.BlockSpec((HEAD_DIM, tm), const), pl.BlockSpec((HEAD_DIM, tm), const),
        pl.BlockSpec((HEAD_DIM, tm), const), pl.BlockSpec((HEAD_DIM, tm), const),
    ]
    return pl.pallas_call(
        _inproj_kernel, grid=(B, nt), in_specs=in_specs, out_specs=out_specs, out_shape=out_shape,
        compiler_params=pltpu.CompilerParams(
            dimension_semantics=("parallel", "parallel"), vmem_limit_bytes=VMEM_LIMIT),
        name="inproj",
    )(x, g, wt, gqa, gka, gqb, gkb)


def _mix_a_kernel(q_ref, kp_ref, kc_ref, vp_ref, vc_ref, z_ref, bias_ref, sink_ref, o_ref):
    i = pl.program_id(1)
    q = q_ref[0]
    kband = jnp.concatenate([kp_ref[0], kc_ref[0]], axis=0)
    vband = jnp.concatenate([vp_ref[0], vc_ref[0]], axis=1)
    zeros = jnp.zeros((HEAD_DIM, TQ), _BF16)
    has_prev = i > 0
    for g in range(A_KV_HEADS):
        cols = []
        for hh in range(A_GROUPS):
            h = g * A_GROUPS + hh
            qh = q[h * HEAD_DIM:(h + 1) * HEAD_DIM]
            parts = [zeros] * A_KV_HEADS
            parts[g] = qh
            cols.append(jnp.concatenate(parts, axis=0))
        rhs = jnp.concatenate(cols, axis=1)
        sc = jnp.dot(kband, rhs, preferred_element_type=_F32)
        probs, inv = [], []
        for hh in range(A_GROUPS):
            h = g * A_GROUPS + hh
            s = sc[:, hh * TQ:(hh + 1) * TQ] + bias_ref[h]
            s = jnp.concatenate([jnp.where(has_prev, s[:TQ], NEG), s[TQ:]], axis=0)
            sink = sink_ref[h:h + 1, :]
            m = jnp.maximum(jnp.max(s, axis=0, keepdims=True), sink)
            p = jnp.exp(s - m)
            denom = jnp.sum(p, axis=0, keepdims=True) + jnp.exp(sink - m)
            probs.append(p.astype(_BF16))
            inv.append(1.0 / denom)
        pt = jnp.concatenate(probs, axis=1)
        out = jnp.dot(vband[g * HEAD_DIM:(g + 1) * HEAD_DIM, :], pt,
                      preferred_element_type=_F32)
        for hh in range(A_GROUPS):
            h = g * A_GROUPS + hh
            rows = slice(h * HEAD_DIM, (h + 1) * HEAD_DIM)
            o = out[:, hh * TQ:(hh + 1) * TQ] * inv[hh] * z_ref[0, rows, :].astype(_F32)
            o_ref[0, rows, :] = o.astype(_BF16)


def _mix_a(qa_t, ka_n, va_t, za_t, bias_a, sink_a):
    B, _, S = qa_t.shape
    nq = S // TQ
    prev = lambda i: jnp.maximum(i - 1, 0)
    in_specs = [
        pl.BlockSpec((1, A_WIDTH, TQ), lambda b, i: (b, 0, i)),
        pl.BlockSpec((1, TQ, A_KV_WIDTH), lambda b, i: (b, prev(i), 0)),
        pl.BlockSpec((1, TQ, A_KV_WIDTH), lambda b, i: (b, i, 0)),
        pl.BlockSpec((1, A_KV_WIDTH, TQ), lambda b, i: (b, 0, prev(i))),
        pl.BlockSpec((1, A_KV_WIDTH, TQ), lambda b, i: (b, 0, i)),
        pl.BlockSpec((1, A_WIDTH, TQ), lambda b, i: (b, 0, i)),
        pl.BlockSpec((A_Q_HEADS, 2 * TQ, TQ), lambda b, i: (0, 0, 0)),
        pl.BlockSpec((A_Q_HEADS, TQ), lambda b, i: (0, 0)),
    ]
    return pl.pallas_call(
        _mix_a_kernel, grid=(B, nq), in_specs=in_specs,
        out_specs=pl.BlockSpec((1, A_WIDTH, TQ), lambda b, i: (b, 0, i)),
        out_shape=jax.ShapeDtypeStruct((B, A_WIDTH, S), _BF16),
        compiler_params=pltpu.CompilerParams(
            dimension_semantics=("parallel", "parallel"), vmem_limit_bytes=VMEM_LIMIT),
        name="mixer_a",
    )(qa_t, ka_n, ka_n, va_t, va_t, za_t, bias_a, sink_a)


def _colsum8(v):
    return jnp.sum(v.reshape(v.shape[0] // 8, 8, v.shape[1]), axis=0)


def _as01(pred):
    return jnp.where(pred, 1, 0).astype(jnp.int32)


def _ordered_bits_to_f32(u):
    key = u ^ jnp.int32(INT_MIN)
    return pltpu.bitcast(jnp.where(key < 0, key ^ jnp.int32(0x7FFFFFFF), key), _F32)


def _colmax8(v):
    return jnp.max(v.reshape(v.shape[0] // 8, 8, v.shape[1]), axis=0)


def _by_pairs(n, step):
    def body(t, carry):
        for u in range(4):
            step(4 * t + u)
        return carry

    lax.fori_loop(0, n // 4, body, 0)
    base = (n // 4) * 4

    @pl.when((n & 2) != 0)
    def _():
        step(base)
        step(base + 1)

    @pl.when((n & 1) != 0)
    def _():
        step(base + (n & 2))


def _mix_b_kernel(qb_ref, kb_ref, vb_ref, zb_ref, qi_ref, ki_ref, wi_ref, bias_ref, o_ref,
                  keys_scr, mb_scr, sc_scr, acc_scr, m_scr, thr_scr, tie_scr):
    i = pl.program_id(1)
    nc = (i + 2) // 2
    ns = (nc + 1) // 2
    top_k = TOPK_MAX

    qi = qi_ref[0]
    zpad = jnp.zeros((LANES - IDX_DIM, TQ), _BF16)
    rhs_i = jnp.concatenate(
        [jnp.concatenate([qi[h * IDX_DIM:(h + 1) * IDX_DIM], zpad], axis=0)
         for h in range(IDX_HEADS)], axis=1)
    w = wi_ref[0] * (IDX_DIM ** -0.5 * IDX_HEADS ** -0.5)
    s_minus_t = (lax.broadcasted_iota(jnp.int32, (KCHUNK, TQ), 0)
                 - lax.broadcasted_iota(jnp.int32, (KCHUNK, TQ), 1))

    def index_step(c):
        r0 = pl.multiple_of(c * KCHUNK, KCHUNK)
        d = jnp.dot(ki_ref[0, pl.ds(r0, KCHUNK), :], rhs_i,
                    preferred_element_type=_F32)
        acc = w[0:1, :] * jnp.maximum(d[:, 0:TQ], 0.0)
        for h in range(1, IDX_HEADS):
            acc = acc + w[h:h + 1, :] * jnp.maximum(d[:, h * TQ:(h + 1) * TQ], 0.0)
        causal = s_minus_t <= (i * TQ - r0)
        keys_scr[pl.ds(r0, KCHUNK), :] = jnp.where(causal, acc, -jnp.inf)

    _by_pairs(nc, index_step)

    @pl.when(nc % 2 == 1)
    def _():
        keys_scr[pl.ds(pl.multiple_of(nc * KCHUNK, KCHUNK), KCHUNK), :] = jnp.full(
            (KCHUNK, TQ), -jnp.inf, _F32)

    def count(pred_fn):
        def body(c, acc):
            r0 = pl.multiple_of(c * KSUPER, KSUPER)
            return acc + _colsum8(pred_fn(keys_scr[pl.ds(r0, KSUPER), :], r0))
        acc = lax.fori_loop(0, ns, body, jnp.zeros((8, TQ), jnp.int32))
        return jnp.sum(acc, axis=0, keepdims=True)

    @pl.when(i * TQ + TQ <= top_k)
    def _():
        thr_scr[...] = jnp.full((1, TQ), -jnp.inf, _F32)
        tie_scr[...] = jnp.full((1, TQ), -1, jnp.int32)

    @pl.when(i * TQ + TQ > top_k)
    def _():
        def bit_body(b, t_u):
            cand_u = t_u | lax.shift_left(jnp.int32(1), 31 - b)
            cand = _ordered_bits_to_f32(cand_u)
            cnt = count(lambda k, r0: _as01(k >= cand))
            return jnp.where(cnt >= top_k, cand_u, t_u)

        t_u = lax.fori_loop(0, 32, bit_body, jnp.zeros((1, TQ), jnp.int32))
        thr = _ordered_bits_to_f32(t_u)
        cnt_gt = count(lambda k, r0: _as01(k > thr))
        cnt_ge = count(lambda k, r0: _as01(k >= thr))
        need = top_k - cnt_gt
        thr_scr[...] = thr
        tie_scr[...] = jnp.full((1, TQ), 2 ** 30, jnp.int32)
        row = lax.broadcasted_iota(jnp.int32, (KSUPER, TQ), 0)

        @pl.when(jnp.max(cnt_ge) > top_k)
        def _():
            def tie_body(b, lo):
                cand = lo | lax.shift_left(jnp.int32(1), 10 - b)
                cnt = count(lambda k, r0: jnp.where(k == thr, _as01(row < cand - r0), 0))
                return jnp.where(cnt < need, cand, lo)
            tie_scr[...] = lax.fori_loop(0, 11, tie_body, jnp.zeros((1, TQ), jnp.int32))

    thr = thr_scr[...]
    tie = tie_scr[...]
    row = lax.broadcasted_iota(jnp.int32, (KSUPER, TQ), 0)

    def mask_body(c, carry):
        r0 = pl.multiple_of(c * KSUPER, KSUPER)
        k = keys_scr[pl.ds(r0, KSUPER), :]
        mb_scr[pl.ds(r0, KSUPER), :] = jnp.where(
            k == thr, jnp.where(row <= tie - r0, 0.0, NEG), jnp.where(k > thr, 0.0, NEG))
        return carry

    lax.fori_loop(0, ns, mask_body, 0)

    zeros = jnp.zeros((HEAD_DIM, TQ), _BF16)
    n_pairs = B_HEADS // 2
    rhs = []
    for p in range(n_pairs):
        q0 = qb_ref[0, (2 * p) * HEAD_DIM:(2 * p + 1) * HEAD_DIM, :]
        q1 = qb_ref[0, (2 * p + 1) * HEAD_DIM:(2 * p + 2) * HEAD_DIM, :]
        rhs.append(jnp.concatenate([jnp.concatenate([q0, zeros], axis=0),
                                    jnp.concatenate([zeros, q1], axis=0)], axis=1))

    def score_step(c):
        r0 = pl.multiple_of(c * KCHUNK, KCHUNK)
        for p in range(n_pairs):
            s = jnp.dot(kb_ref[0, pl.ds(r0, KCHUNK), p * LANES:(p + 1) * LANES], rhs[p],
                        preferred_element_type=_F32)
            for hh in range(2):
                h = 2 * p + hh
                parts = []
                for half in range(KCHUNK // TQ):
                    jb = c * (KCHUNK // TQ) + half
                    kind = jnp.clip(jb - i + 2, 0, 2)
                    rs = pl.multiple_of(r0 + half * TQ, TQ)
                    parts.append(s[half * TQ:(half + 1) * TQ, hh * TQ:(hh + 1) * TQ]
                                 + bias_ref[h, kind] + mb_scr[pl.ds(rs, TQ), :])
                sh = jnp.concatenate(parts, axis=0)
                sc_scr[pl.ds(r0, KCHUNK), h * TQ:(h + 1) * TQ] = sh
                m_scr[h * 8:(h + 1) * 8, :] = jnp.maximum(m_scr[h * 8:(h + 1) * 8, :], _colmax8(sh))

    m_scr[...] = jnp.full(m_scr.shape, NEG, _F32)
    _by_pairs(nc, score_step)
    m_row = jnp.concatenate(
        [jnp.max(m_scr[h * 8:(h + 1) * 8, :], axis=0, keepdims=True) for h in range(B_HEADS)],
        axis=1)
    acc_scr[...] = jnp.zeros_like(acc_scr)

    def pv_step(c):
        r0 = pl.multiple_of(c * KCHUNK, KCHUNK)
        for p in range(n_pairs):
            cols = slice(2 * p * TQ, (2 * p + 2) * TQ)
            pr = jnp.exp2(sc_scr[pl.ds(r0, KCHUNK), cols] - m_row[:, cols]).astype(_BF16)
            acc_scr[p] += jnp.dot(vb_ref[0, c, p * PAIR_ROWS:(p + 1) * PAIR_ROWS, :], pr,
                                  preferred_element_type=_F32)

    _by_pairs(nc, pv_step)
    for p in range(n_pairs):
        inv = 1.0 / acc_scr[p, 2 * HEAD_DIM:2 * HEAD_DIM + 1, :]
        for hh in range(2):
            rows = slice((2 * p + hh) * HEAD_DIM, (2 * p + hh + 1) * HEAD_DIM)
            o = (acc_scr[p, hh * HEAD_DIM:(hh + 1) * HEAD_DIM, hh * TQ:(hh + 1) * TQ]
                 * inv[:, hh * TQ:(hh + 1) * TQ] * zb_ref[0, rows, :].astype(_F32))
            o_ref[0, rows, :] = o.astype(_BF16)


def _mix_b(qb_t, kb_n, vb_c, zb_t, qi_t, ki_n, wi_t, bias_b):
    B, _, S = qb_t.shape
    nq = S // TQ
    per_q = lambda rows: pl.BlockSpec((1, rows, TQ), lambda b, i: (b, 0, i))
    in_specs = [
        per_q(B_WIDTH),
        pl.BlockSpec((1, S, B_WIDTH), lambda b, i: (b, 0, 0)),
        pl.BlockSpec((1, S // KCHUNK, V_ROWS, KCHUNK), lambda b, i: (b, 0, 0, 0)),
        per_q(B_WIDTH),
        per_q(IDX_WIDTH),
        pl.BlockSpec((1, S, LANES), lambda b, i: (b, 0, 0)),
        per_q(IDX_HEADS),
        pl.BlockSpec((B_HEADS, 3, TQ, TQ), lambda b, i: (0, 0, 0, 0)),
    ]
    scratch = [
        pltpu.VMEM((S, TQ), _F32),
        pltpu.VMEM((S, TQ), _F32),
        pltpu.VMEM((S, B_HEADS * TQ), _F32),
        pltpu.VMEM((B_HEADS // 2, PAIR_ROWS, 2 * TQ), _F32),
        pltpu.VMEM((B_HEADS * 8, TQ), _F32),
        pltpu.VMEM((1, TQ), _F32),
        pltpu.VMEM((1, TQ), jnp.int32),
    ]
    return pl.pallas_call(
        _mix_b_kernel, grid=(B, nq), in_specs=in_specs,
        out_specs=per_q(B_WIDTH),
        out_shape=jax.ShapeDtypeStruct((B, B_WIDTH, S), _BF16),
        scratch_shapes=scratch,
        compiler_params=pltpu.CompilerParams(
            dimension_semantics=("parallel", "arbitrary"), vmem_limit_bytes=VMEM_LIMIT),
        name="mixer_b",
    )(qb_t, kb_n, vb_c, zb_t, qi_t, ki_n, wi_t, bias_b)


def _merge_kernel(x_ref, ya_ref, yb_ref, gt_ref, wpa_ref, wpb_ref, wo_ref, o_ref):
    pa = jnp.dot(wpa_ref[...], ya_ref[0], preferred_element_type=_F32)
    pb = jnp.dot(wpb_ref[...], yb_ref[0], preferred_element_type=_F32)
    merged = (gt_ref[0, :D_MODEL, :].astype(_F32) * pa
              + gt_ref[0, D_MODEL:, :].astype(_F32) * pb).astype(_BF16)
    out_t = jnp.dot(wo_ref[...], merged, preferred_element_type=_F32)
    o_ref[0] = x_ref[0] + out_t.T


def _merge(x, ya_t, yb_t, gt_t, wpa_t, wpb_t, wo_t):
    B, S, D = x.shape
    tm = TM_PROJ
    const = lambda b, t: (0, 0)
    in_specs = [
        pl.BlockSpec((1, tm, D), lambda b, t: (b, t, 0)),
        pl.BlockSpec((1, A_WIDTH, tm), lambda b, t: (b, 0, t)),
        pl.BlockSpec((1, B_WIDTH, tm), lambda b, t: (b, 0, t)),
        pl.BlockSpec((1, 2 * D, tm), lambda b, t: (b, 0, t)),
        pl.BlockSpec((D, A_WIDTH), const),
        pl.BlockSpec((D, B_WIDTH), const),
        pl.BlockSpec((D, D), const),
    ]
    return pl.pallas_call(
        _merge_kernel, grid=(B, S // tm), in_specs=in_specs,
        out_specs=pl.BlockSpec((1, tm, D), lambda b, t: (b, t, 0)),
        out_shape=jax.ShapeDtypeStruct((B, S, D), x.dtype),
        compiler_params=pltpu.CompilerParams(
            dimension_semantics=("parallel", "parallel"), vmem_limit_bytes=VMEM_LIMIT),
        name="merge_out",
    )(x, ya_t, yb_t, gt_t, wpa_t, wpb_t, wo_t)


def _t5_bucket(n):
    n = np.maximum(n, 0)
    max_exact = N_BUCKETS // 2
    nf = np.maximum(n, 1).astype(np.float64)
    large = max_exact + np.floor(np.log(nf / max_exact) / math.log(MAX_DISTANCE / max_exact)
                                 * (N_BUCKETS - max_exact)).astype(np.int64)
    large = np.minimum(large, N_BUCKETS - 1)
    return np.where(n < max_exact, n, large)


def _bias_tables(rel_bias):
    t = np.arange(TQ)[None, :]
    s = np.arange(TQ)[:, None]
    d_prev = t + TQ - s
    d_cur = t - s
    table_a = rel_bias[:, :A_Q_HEADS].astype(_F32).T
    table_b = rel_bias[:, A_Q_HEADS:].astype(_F32).T

    def tile(table, dist, windowed):
        onehot = (_t5_bucket(dist)[..., None] == np.arange(N_BUCKETS)).astype(np.float32)
        b = jnp.einsum("stk,hk->hst", onehot, table, precision=lax.Precision.HIGHEST)
        if windowed:
            b = jnp.where(((dist >= 0) & (dist < WINDOW))[None], b, NEG)
        return b

    bias_a = jnp.concatenate([tile(table_a, d_prev, True), tile(table_a, d_cur, True)], axis=1)
    far = jnp.broadcast_to(table_b[:, N_BUCKETS - 1][:, None, None], (B_HEADS, TQ, TQ))
    bias_b = jnp.stack([far, tile(table_b, d_prev, False), tile(table_b, d_cur, False)], axis=1) * LOG2E
    return bias_a, bias_b


def kernel(x, norm_g, w_in, qnorm_a, knorm_a, sinks_a, qnorm_b, knorm_b, rel_bias,
           w_proj_a, w_proj_b, w_out):
    assert norm_g.shape[0] == 1, "single-layer block"
    B, S, D = x.shape
    assert D == D_MODEL and S % TM_PROJ == 0 and S % KSUPER == 0
    assert MAX_DISTANCE <= TQ

    wt = w_in[0].T
    n_real = _SEG["kw"][0] + KW_REAL
    wt = jnp.concatenate(
        [wt[:n_real], jnp.zeros((_SEG["kw"][1] - n_real, D), wt.dtype), wt[n_real:]], axis=0)
    wt = wt.astype(_BF16)
    bcast = lambda g, scale: jnp.broadcast_to((g.astype(_F32) * scale)[:, None], (HEAD_DIM, TM_PROJ))
    q_scale = HEAD_DIM ** -0.5
    (qa_t, ka_n, va_t, za_t, qb_t, kb_n, vb_c, zb_t, qi_t, ki_n, wi_t, gt_t) = _inproj(
        x, norm_g[0][None, :].astype(_F32), wt,
        bcast(qnorm_a[0], q_scale), bcast(knorm_a[0], 1.0),
        bcast(qnorm_b[0], q_scale * LOG2E), bcast(knorm_b[0], 1.0))

    bias_a, bias_b = _bias_tables(rel_bias)
    sink_a = jnp.broadcast_to(sinks_a[0].astype(_F32)[:, None], (A_Q_HEADS, TQ))
    ya_t = _mix_a(qa_t, ka_n, va_t, za_t, bias_a, sink_a)
    yb_t = _mix_b(qb_t, kb_n, vb_c, zb_t, qi_t, ki_n, wi_t, bias_b)
    return _merge(x, ya_t, yb_t, gt_t,
                  w_proj_a[0].T.astype(_BF16), w_proj_b[0].T.astype(_BF16), w_out[0].T.astype(_BF16))
```

```python
import functools
import math

import jax
import jax.numpy as jnp
import numpy as np
from jax import lax
from jax.experimental import pallas as pl
from jax.experimental.pallas import tpu as pltpu

D_MODEL = 1024
HEAD_DIM = 64
A_Q_HEADS = 8
A_KV_HEADS = 2
A_GROUPS = A_Q_HEADS // A_KV_HEADS
A_WIDTH = A_Q_HEADS * HEAD_DIM
A_KV_WIDTH = A_KV_HEADS * HEAD_DIM
B_HEADS = 8
B_WIDTH = B_HEADS * HEAD_DIM
IDX_HEADS = 8
IDX_DIM = 32
IDX_WIDTH = IDX_HEADS * IDX_DIM
WINDOW = 128
TOPK_MAX = 256
N_BUCKETS = 32
MAX_DISTANCE = 128
RMS_EPS = 1e-6

LANES = 128
TQ = 128
A_SUB = 4
KCHUNK = 256
KSUPER = 2 * KCHUNK
TM_PROJ = 512
NEG = -1e30
LOG2E = math.log2(math.e)
ONES_ROWS = 16
PAIR_ROWS = 2 * HEAD_DIM + ONES_ROWS
V_ROWS = (B_HEADS // 2) * PAIR_ROWS
INT_MIN = -(2 ** 31)
VMEM_LIMIT = 56 * 1024 * 1024

_SEG = {}
_off = 0
for _name, _rows in (("qa", A_WIDTH), ("ka", A_KV_WIDTH), ("va", A_KV_WIDTH), ("za", A_WIDTH),
                     ("qb", B_WIDTH), ("kb", B_WIDTH), ("vb", B_WIDTH), ("zb", B_WIDTH),
                     ("qi", IDX_WIDTH), ("kw", LANES), ("gates", 2 * D_MODEL)):
    _SEG[_name] = (_off, _off + _rows)
    _off += _rows
PROJ_ROWS = _off
KW_REAL = IDX_DIM + IDX_HEADS

_F32 = jnp.float32
_BF16 = jnp.bfloat16
_NT = (((1,), (1,)), ((), ()))


def _inproj_kernel(x_ref, g_ref, wt_ref, gqa_ref, gka_ref, gqb_ref, gkb_ref,
                   qa_o, ka_o, va_o, za_o, qb_o, kb_o, vb_o, zb_o, qi_o, ki_o, wi_o, gt_o):
    tm = x_ref.shape[1]
    x = x_ref[0]
    ms = jnp.mean(x * x, axis=-1, keepdims=True)
    h = (x * lax.rsqrt(ms + RMS_EPS) * g_ref[...]).astype(_BF16)

    def proj(lo, hi):
        return lax.dot_general(wt_ref[lo:hi, :], h, _NT, preferred_element_type=_F32)

    def seg(name):
        return proj(*_SEG[name])

    def headnorm(p, gain_ref):
        nh = p.shape[0] // HEAD_DIM
        p3 = p.reshape(nh, HEAD_DIM, tm)
        r = lax.rsqrt(jnp.mean(p3 * p3, axis=1, keepdims=True) + RMS_EPS)
        return (p3 * r * gain_ref[...][None]).reshape(nh * HEAD_DIM, tm)

    def silu(p):
        return p * jax.nn.sigmoid(p)

    qa_o[0] = headnorm(seg("qa"), gqa_ref).astype(_BF16)
    ka_o[0] = headnorm(seg("ka"), gka_ref).T.astype(_BF16)
    va_o[0] = seg("va").astype(_BF16)
    za_o[0] = silu(seg("za")).astype(_BF16)
    qb_o[0] = headnorm(seg("qb"), gqb_ref).astype(_BF16)
    kb_o[0] = headnorm(seg("kb"), gkb_ref).T.astype(_BF16)
    vb = seg("vb").astype(_BF16)
    ones = jnp.ones((ONES_ROWS, KCHUNK), _BF16)
    for c in range(tm // KCHUNK):
        for p in range(B_HEADS // 2):
            vb_o[0, c, p * PAIR_ROWS:(p + 1) * PAIR_ROWS, :] = jnp.concatenate(
                [vb[p * LANES:(p + 1) * LANES, c * KCHUNK:(c + 1) * KCHUNK], ones], axis=0)
    zb_o[0] = silu(seg("zb")).astype(_BF16)
    qi_o[0] = seg("qi").astype(_BF16)
    kw = seg("kw")
    wi_o[0] = kw[IDX_DIM:KW_REAL, :]
    row = lax.broadcasted_iota(jnp.int32, kw.shape, 0)
    ki_o[0] = jnp.where(row < IDX_DIM, kw, 0.0).T.astype(_BF16)
    g_lo = _SEG["gates"][0]
    for c in range(2 * D_MODEL // 512):
        gt_o[0, c * 512:(c + 1) * 512, :] = jax.nn.sigmoid(
            proj(g_lo + c * 512, g_lo + (c + 1) * 512)).astype(_BF16)


def _inproj(x, g, wt, gqa, gka, gqb, gkb):
    B, S, D = x.shape
    tm = TM_PROJ
    nt = S // tm
    const = lambda b, t: (0, 0)
    tr = lambda rows: pl.BlockSpec((1, rows, tm), lambda b, t: (b, 0, t))
    nat = lambda cols: pl.BlockSpec((1, tm, cols), lambda b, t: (b, t, 0))
    sds = jax.ShapeDtypeStruct
    out_shape = (
        sds((B, A_WIDTH, S), _BF16), sds((B, S, A_KV_WIDTH), _BF16), sds((B, A_KV_WIDTH, S), _BF16),
        sds((B, A_WIDTH, S), _BF16),
        sds((B, B_WIDTH, S), _BF16), sds((B, S, B_WIDTH), _BF16),
        sds((B, S // KCHUNK, V_ROWS, KCHUNK), _BF16), sds((B, B_WIDTH, S), _BF16),
        sds((B, IDX_WIDTH, S), _BF16), sds((B, S, LANES), _BF16), sds((B, IDX_HEADS, S), _F32),
        sds((B, 2 * D_MODEL, S), _BF16),
    )
    out_specs = (
        tr(A_WIDTH), nat(A_KV_WIDTH), tr(A_KV_WIDTH), tr(A_WIDTH),
        tr(B_WIDTH), nat(B_WIDTH),
        pl.BlockSpec((1, tm // KCHUNK, V_ROWS, KCHUNK), lambda b, t: (b, t, 0, 0)), tr(B_WIDTH),
        tr(IDX_WIDTH), nat(LANES), tr(IDX_HEADS), tr(2 * D_MODEL),
    )
    in_specs = [
        pl.BlockSpec((1, tm, D), lambda b, t: (b, t, 0)),
        pl.BlockSpec((1, D), const),
        pl.BlockSpec((PROJ_ROWS, D), const, pipeline_mode=pl.Buffered(1)),
        pl.BlockSpec((HEAD_DIM, tm), const), pl.BlockSpec((HEAD_DIM, tm), const),
        pl.BlockSpec((HEAD_DIM, tm), const), pl.BlockSpec((HEAD_DIM, tm), const),
    ]
    return pl.pallas_call(
        _inproj_kernel, grid=(B, nt), in_specs=in_specs, out_specs=out_specs, out_shape=out_shape,
        compiler_params=pltpu.CompilerParams(
            dimension_semantics=("parallel", "parallel"), vmem_limit_bytes=VMEM_LIMIT),
        name="inproj",
    )(x, g, wt, gqa, gka, gqb, gkb)


def _mix_a_kernel(q_ref, kp_ref, kc_ref, vp_ref, vc_ref, z_ref, bias_ref, sink_ref, o_ref):
    i = pl.program_id(1)
    zeros = jnp.zeros((HEAD_DIM, TQ), _BF16)
    for j in range(A_SUB):
        cur = slice(j * TQ, (j + 1) * TQ)
        old = slice((j - 1) * TQ, j * TQ)
        k_old = kp_ref[0] if j == 0 else kc_ref[0, old, :]
        v_old = vp_ref[0] if j == 0 else vc_ref[0, :, old]
        kband = jnp.concatenate([k_old, kc_ref[0, cur, :]], axis=0)
        vband = jnp.concatenate([v_old, vc_ref[0, :, cur]], axis=1)
        for g in range(A_KV_HEADS):
            cols = []
            for hh in range(A_GROUPS):
                h = g * A_GROUPS + hh
                parts = [zeros] * A_KV_HEADS
                parts[g] = q_ref[0, h * HEAD_DIM:(h + 1) * HEAD_DIM, cur]
                cols.append(jnp.concatenate(parts, axis=0))
            rhs = jnp.concatenate(cols, axis=1)
            sc = jnp.dot(kband, rhs, preferred_element_type=_F32)
            probs, inv = [], []
            for hh in range(A_GROUPS):
                h = g * A_GROUPS + hh
                s = sc[:, hh * TQ:(hh + 1) * TQ] + bias_ref[h]
                if j == 0:
                    s = jnp.concatenate([jnp.where(i > 0, s[:TQ], NEG), s[TQ:]], axis=0)
                sink = sink_ref[h:h + 1, :]
                m = jnp.maximum(jnp.max(s, axis=0, keepdims=True), sink)
                p = jnp.exp(s - m)
                denom = jnp.sum(p, axis=0, keepdims=True) + jnp.exp(sink - m)
                probs.append(p.astype(_BF16))
                inv.append(1.0 / denom)
            pt = jnp.concatenate(probs, axis=1)
            out = jnp.dot(vband[g * HEAD_DIM:(g + 1) * HEAD_DIM, :], pt,
                          preferred_element_type=_F32)
            for hh in range(A_GROUPS):
                h = g * A_GROUPS + hh
                rows = slice(h * HEAD_DIM, (h + 1) * HEAD_DIM)
                o = out[:, hh * TQ:(hh + 1) * TQ] * inv[hh] * z_ref[0, rows, cur].astype(_F32)
                o_ref[0, rows, cur] = o.astype(_BF16)


def _mix_a(qa_t, ka_n, va_t, za_t, bias_a, sink_a):
    B, _, S = qa_t.shape
    tqa = A_SUB * TQ
    prev = lambda i: jnp.maximum(i * A_SUB - 1, 0)
    in_specs = [
        pl.BlockSpec((1, A_WIDTH, tqa), lambda b, i: (b, 0, i)),
        pl.BlockSpec((1, TQ, A_KV_WIDTH), lambda b, i: (b, prev(i), 0)),
        pl.BlockSpec((1, tqa, A_KV_WIDTH), lambda b, i: (b, i, 0)),
        pl.BlockSpec((1, A_KV_WIDTH, TQ), lambda b, i: (b, 0, prev(i))),
        pl.BlockSpec((1, A_KV_WIDTH, tqa), lambda b, i: (b, 0, i)),
        pl.BlockSpec((1, A_WIDTH, tqa), lambda b, i: (b, 0, i)),
        pl.BlockSpec((A_Q_HEADS, 2 * TQ, TQ), lambda b, i: (0, 0, 0)),
        pl.BlockSpec((A_Q_HEADS, TQ), lambda b, i: (0, 0)),
    ]
    return pl.pallas_call(
        _mix_a_kernel, grid=(B, S // tqa), in_specs=in_specs,
        out_specs=pl.BlockSpec((1, A_WIDTH, tqa), lambda b, i: (b, 0, i)),
        out_shape=jax.ShapeDtypeStruct((B, A_WIDTH, S), _BF16),
        compiler_params=pltpu.CompilerParams(
            dimension_semantics=("parallel", "parallel"), vmem_limit_bytes=VMEM_LIMIT),
        name="mixer_a",
    )(qa_t, ka_n, ka_n, va_t, va_t, za_t, bias_a, sink_a)


def _colsum8(v):
    return jnp.sum(v.reshape(v.shape[0] // 8, 8, v.shape[1]), axis=0)


def _as01(pred):
    return jnp.where(pred, 1, 0).astype(jnp.int32)


def _ordered_bits_to_f32(u):
    key = u ^ jnp.int32(INT_MIN)
    return pltpu.bitcast(jnp.where(key < 0, key ^ jnp.int32(0x7FFFFFFF), key), _F32)


def _colmax8(v):
    return jnp.max(v.reshape(v.shape[0] // 8, 8, v.shape[1]), axis=0)


def _by_pairs(n, step):
    def body(t, carry):
        for u in range(4):
            step(4 * t + u)
        return carry

    lax.fori_loop(0, n // 4, body, 0)
    base = (n // 4) * 4

    @pl.when((n & 2) != 0)
    def _():
        step(base)
        step(base + 1)

    @pl.when((n & 1) != 0)
    def _():
        step(base + (n & 2))


def _mix_b_kernel(qb_ref, kb_ref, vb_ref, zb_ref, qi_ref, ki_ref, wi_ref, bias_ref, o_ref,
                  keys_scr, mb_scr, sc_scr, acc_scr, m_scr, thr_scr, tie_scr):
    i = pl.program_id(1)
    nc = (i + 2) // 2
    ns = (nc + 1) // 2
    top_k = TOPK_MAX

    qi = qi_ref[0]
    zpad = jnp.zeros((LANES - IDX_DIM, TQ), _BF16)
    rhs_i = jnp.concatenate(
        [jnp.concatenate([qi[h * IDX_DIM:(h + 1) * IDX_DIM], zpad], axis=0)
         for h in range(IDX_HEADS)], axis=1)
    w = wi_ref[0] * (IDX_DIM ** -0.5 * IDX_HEADS ** -0.5)
    s_minus_t = (lax.broadcasted_iota(jnp.int32, (KCHUNK, TQ), 0)
                 - lax.broadcasted_iota(jnp.int32, (KCHUNK, TQ), 1))

    def index_step(c):
        r0 = pl.multiple_of(c * KCHUNK, KCHUNK)
        d = jnp.dot(ki_ref[0, pl.ds(r0, KCHUNK), :], rhs_i,
                    preferred_element_type=_F32)
        acc = w[0:1, :] * jnp.maximum(d[:, 0:TQ], 0.0)
        for h in range(1, IDX_HEADS):
            acc = acc + w[h:h + 1, :] * jnp.maximum(d[:, h * TQ:(h + 1) * TQ], 0.0)
        causal = s_minus_t <= (i * TQ - r0)
        keys_scr[pl.ds(r0, KCHUNK), :] = jnp.where(causal, acc, -jnp.inf)

    _by_pairs(nc, index_step)

    @pl.when(nc % 2 == 1)
    def _():
        keys_scr[pl.ds(pl.multiple_of(nc * KCHUNK, KCHUNK), KCHUNK), :] = jnp.full(
            (KCHUNK, TQ), -jnp.inf, _F32)

    def count(pred_fn):
        def body(c, acc):
            r0 = pl.multiple_of(c * KSUPER, KSUPER)
            return acc + _colsum8(pred_fn(keys_scr[pl.ds(r0, KSUPER), :], r0))
        acc = lax.fori_loop(0, ns, body, jnp.zeros((8, TQ), jnp.int32))
        return jnp.sum(acc, axis=0, keepdims=True)

    @pl.when(i * TQ + TQ <= top_k)
    def _():
        thr_scr[...] = jnp.full((1, TQ), -jnp.inf, _F32)
        tie_scr[...] = jnp.full((1, TQ), -1, jnp.int32)

    @pl.when(i * TQ + TQ > top_k)
    def _():
        def bit_body(b, t_u):
            cand_u = t_u | lax.shift_left(jnp.int32(1), 31 - b)
            cand = _ordered_bits_to_f32(cand_u)
            cnt = count(lambda k, r0: _as01(k >= cand))
            return jnp.where(cnt >= top_k, cand_u, t_u)

        t_u = lax.fori_loop(0, 32, bit_body, jnp.zeros((1, TQ), jnp.int32))
        thr = _ordered_bits_to_f32(t_u)
        cnt_gt = count(lambda k, r0: _as01(k > thr))
        cnt_ge = count(lambda k, r0: _as01(k >= thr))
        need = top_k - cnt_gt
        thr_scr[...] = thr
        tie_scr[...] = jnp.full((1, TQ), 2 ** 30, jnp.int32)
        row = lax.broadcasted_iota(jnp.int32, (KSUPER, TQ), 0)

        @pl.when(jnp.max(cnt_ge) > top_k)
        def _():
            def tie_body(b, lo):
                cand = lo | lax.shift_left(jnp.int32(1), 10 - b)
                cnt = count(lambda k, r0: jnp.where(k == thr, _as01(row < cand - r0), 0))
                return jnp.where(cnt < need, cand, lo)
            tie_scr[...] = lax.fori_loop(0, 11, tie_body, jnp.zeros((1, TQ), jnp.int32))

    thr = thr_scr[...]
    tie = tie_scr[...]
    row = lax.broadcasted_iota(jnp.int32, (KSUPER, TQ), 0)

    def mask_body(c, carry):
        r0 = pl.multiple_of(c * KSUPER, KSUPER)
        k = keys_scr[pl.ds(r0, KSUPER), :]
        mb_scr[pl.ds(r0, KSUPER), :] = jnp.where(
            k == thr, jnp.where(row <= tie - r0, 0.0, NEG), jnp.where(k > thr, 0.0, NEG))
        return carry

    lax.fori_loop(0, ns, mask_body, 0)

    zeros = jnp.zeros((HEAD_DIM, TQ), _BF16)
    n_pairs = B_HEADS // 2
    rhs = []
    for p in range(n_pairs):
        q0 = qb_ref[0, (2 * p) * HEAD_DIM:(2 * p + 1) * HEAD_DIM, :]
        q1 = qb_ref[0, (2 * p + 1) * HEAD_DIM:(2 * p + 2) * HEAD_DIM, :]
        rhs.append(jnp.concatenate([jnp.concatenate([q0, zeros], axis=0),
                                    jnp.concatenate([zeros, q1], axis=0)], axis=1))

    def score_step(c):
        r0 = pl.multiple_of(c * KCHUNK, KCHUNK)
        for p in range(n_pairs):
            s = jnp.dot(kb_ref[0, pl.ds(r0, KCHUNK), p * LANES:(p + 1) * LANES], rhs[p],
                        preferred_element_type=_F32)
            for hh in range(2):
                h = 2 * p + hh
                parts = []
                for half in range(KCHUNK // TQ):
                    jb = c * (KCHUNK // TQ) + half
                    kind = jnp.clip(jb - i + 2, 0, 2)
                    rs = pl.multiple_of(r0 + half * TQ, TQ)
                    parts.append(s[half * TQ:(half + 1) * TQ, hh * TQ:(hh + 1) * TQ]
                                 + bias_ref[h, kind] + mb_scr[pl.ds(rs, TQ), :])
                sh = jnp.concatenate(parts, axis=0)
                sc_scr[pl.ds(r0, KCHUNK), h * TQ:(h + 1) * TQ] = sh
                m_scr[h * 8:(h + 1) * 8, :] = jnp.maximum(m_scr[h * 8:(h + 1) * 8, :], _colmax8(sh))

    m_scr[...] = jnp.full(m_scr.shape, NEG, _F32)
    _by_pairs(nc, score_step)
    m_row = jnp.concatenate(
        [jnp.max(m_scr[h * 8:(h + 1) * 8, :], axis=0, keepdims=True) for h in range(B_HEADS)],
        axis=1)
    acc_scr[...] = jnp.zeros_like(acc_scr)

    def pv_step(c):
        r0 = pl.multiple_of(c * KCHUNK, KCHUNK)
        for p in range(n_pairs):
            cols = slice(2 * p * TQ, (2 * p + 2) * TQ)
            pr = jnp.exp2(sc_scr[pl.ds(r0, KCHUNK), cols] - m_row[:, cols]).astype(_BF16)
            acc_scr[p] += jnp.dot(vb_ref[0, c, p * PAIR_ROWS:(p + 1) * PAIR_ROWS, :], pr,
                                  preferred_element_type=_F32)

    _by_pairs(nc, pv_step)
    for p in range(n_pairs):
        inv = 1.0 / acc_scr[p, 2 * HEAD_DIM:2 * HEAD_DIM + 1, :]
        for hh in range(2):
            rows = slice((2 * p + hh) * HEAD_DIM, (2 * p + hh + 1) * HEAD_DIM)
            o = (acc_scr[p, hh * HEAD_DIM:(hh + 1) * HEAD_DIM, hh * TQ:(hh + 1) * TQ]
                 * inv[:, hh * TQ:(hh + 1) * TQ] * zb_ref[0, rows, :].astype(_F32))
            o_ref[0, rows, :] = o.astype(_BF16)


def _mix_b(qb_t, kb_n, vb_c, zb_t, qi_t, ki_n, wi_t, bias_b):
    B, _, S = qb_t.shape
    nq = S // TQ
    per_q = lambda rows: pl.BlockSpec((1, rows, TQ), lambda b, i: (b, 0, i))
    in_specs = [
        per_q(B_WIDTH),
        pl.BlockSpec((1, S, B_WIDTH), lambda b, i: (b, 0, 0)),
        pl.BlockSpec((1, S // KCHUNK, V_ROWS, KCHUNK), lambda b, i: (b, 0, 0, 0)),
        per_q(B_WIDTH),
        per_q(IDX_WIDTH),
        pl.BlockSpec((1, S, LANES), lambda b, i: (b, 0, 0)),
        per_q(IDX_HEADS),
        pl.BlockSpec((B_HEADS, 3, TQ, TQ), lambda b, i: (0, 0, 0, 0)),
    ]
    scratch = [
        pltpu.VMEM((S, TQ), _F32),
        pltpu.VMEM((S, TQ), _F32),
        pltpu.VMEM((S, B_HEADS * TQ), _F32),
        pltpu.VMEM((B_HEADS // 2, PAIR_ROWS, 2 * TQ), _F32),
        pltpu.VMEM((B_HEADS * 8, TQ), _F32),
        pltpu.VMEM((1, TQ), _F32),
        pltpu.VMEM((1, TQ), jnp.int32),
    ]
    return pl.pallas_call(
        _mix_b_kernel, grid=(B, nq), in_specs=in_specs,
        out_specs=per_q(B_WIDTH),
        out_shape=jax.ShapeDtypeStruct((B, B_WIDTH, S), _BF16),
        scratch_shapes=scratch,
        compiler_params=pltpu.CompilerParams(
            dimension_semantics=("parallel", "arbitrary"), vmem_limit_bytes=VMEM_LIMIT),
        name="mixer_b",
    )(qb_t, kb_n, vb_c, zb_t, qi_t, ki_n, wi_t, bias_b)


def _merge_kernel(x_ref, ya_ref, yb_ref, gt_ref, wpa_ref, wpb_ref, wo_ref, o_ref):
    pa = jnp.dot(wpa_ref[...], ya_ref[0], preferred_element_type=_F32)
    pb = jnp.dot(wpb_ref[...], yb_ref[0], preferred_element_type=_F32)
    merged = (gt_ref[0, :D_MODEL, :].astype(_F32) * pa
              + gt_ref[0, D_MODEL:, :].astype(_F32) * pb).astype(_BF16)
    out_t = jnp.dot(wo_ref[...], merged, preferred_element_type=_F32)
    o_ref[0] = x_ref[0] + out_t.T


def _merge(x, ya_t, yb_t, gt_t, wpa_t, wpb_t, wo_t):
    B, S, D = x.shape
    tm = TM_PROJ
    const = lambda b, t: (0, 0)
    in_specs = [
        pl.BlockSpec((1, tm, D), lambda b, t: (b, t, 0)),
        pl.BlockSpec((1, A_WIDTH, tm), lambda b, t: (b, 0, t)),
        pl.BlockSpec((1, B_WIDTH, tm), lambda b, t: (b, 0, t)),
        pl.BlockSpec((1, 2 * D, tm), lambda b, t: (b, 0, t)),
        pl.BlockSpec((D, A_WIDTH), const),
        pl.BlockSpec((D, B_WIDTH), const),
        pl.BlockSpec((D, D), const),
    ]
    return pl.pallas_call(
        _merge_kernel, grid=(B, S // tm), in_specs=in_specs,
        out_specs=pl.BlockSpec((1, tm, D), lambda b, t: (b, t, 0)),
        out_shape=jax.ShapeDtypeStruct((B, S, D), x.dtype),
        compiler_params=pltpu.CompilerParams(
            dimension_semantics=("parallel", "parallel"), vmem_limit_bytes=VMEM_LIMIT),
        name="merge_out",
    )(x, ya_t, yb_t, gt_t, wpa_t, wpb_t, wo_t)


def _t5_bucket(n):
    n = np.maximum(n, 0)
    max_exact = N_BUCKETS // 2
    nf = np.maximum(n, 1).astype(np.float64)
    large = max_exact + np.floor(np.log(nf / max_exact) / math.log(MAX_DISTANCE / max_exact)
                                 * (N_BUCKETS - max_exact)).astype(np.int64)
    large = np.minimum(large, N_BUCKETS - 1)
    return np.where(n < max_exact, n, large)


def _bias_tables(rel_bias):
    t = np.arange(TQ)[None, :]
    s = np.arange(TQ)[:, None]
    d_prev = t + TQ - s
    d_cur = t - s
    table_a = rel_bias[:, :A_Q_HEADS].astype(_F32).T
    table_b = rel_bias[:, A_Q_HEADS:].astype(_F32).T

    def tile(table, dist, windowed):
        onehot = (_t5_bucket(dist)[..., None] == np.arange(N_BUCKETS)).astype(np.float32)
        b = jnp.einsum("stk,hk->hst", onehot, table, precision=lax.Precision.HIGHEST)
        if windowed:
            b = jnp.where(((dist >= 0) & (dist < WINDOW))[None], b, NEG)
        return b

    bias_a = jnp.concatenate([tile(table_a, d_prev, True), tile(table_a, d_cur, True)], axis=1)
    far = jnp.broadcast_to(table_b[:, N_BUCKETS - 1][:, None, None], (B_HEADS, TQ, TQ))
    bias_b = jnp.stack([far, tile(table_b, d_prev, False), tile(table_b, d_cur, False)], axis=1) * LOG2E
    return bias_a, bias_b


def kernel(x, norm_g, w_in, qnorm_a, knorm_a, sinks_a, qnorm_b, knorm_b, rel_bias,
           w_proj_a, w_proj_b, w_out):
    assert norm_g.shape[0] == 1, "single-layer block"
    B, S, D = x.shape
    assert D == D_MODEL and S % TM_PROJ == 0 and S % KSUPER == 0
    assert MAX_DISTANCE <= TQ

    wt = w_in[0].T
    n_real = _SEG["kw"][0] + KW_REAL
    wt = jnp.concatenate(
        [wt[:n_real], jnp.zeros((_SEG["kw"][1] - n_real, D), wt.dtype), wt[n_real:]], axis=0)
    wt = wt.astype(_BF16)
    bcast = lambda g, scale: jnp.broadcast_to((g.astype(_F32) * scale)[:, None], (HEAD_DIM, TM_PROJ))
    q_scale = HEAD_DIM ** -0.5
    (qa_t, ka_n, va_t, za_t, qb_t, kb_n, vb_c, zb_t, qi_t, ki_n, wi_t, gt_t) = _inproj(
        x, norm_g[0][None, :].astype(_F32), wt,
        bcast(qnorm_a[0], q_scale), bcast(knorm_a[0], 1.0),
        bcast(qnorm_b[0], q_scale * LOG2E), bcast(knorm_b[0], 1.0))

    bias_a, bias_b = _bias_tables(rel_bias)
    sink_a = jnp.broadcast_to(sinks_a[0].astype(_F32)[:, None], (A_Q_HEADS, TQ))
    ya_t = _mix_a(qa_t, ka_n, va_t, za_t, bias_a, sink_a)
    yb_t = _mix_b(qb_t, kb_n, vb_c, zb_t, qi_t, ki_n, wi_t, bias_b)
    return _merge(x, ya_t, yb_t, gt_t,
                  w_proj_a[0].T.astype(_BF16), w_proj_b[0].T.astype(_BF16), w_out[0].T.astype(_BF16))
```

```python
import functools
import math

import jax
import jax.numpy as jnp
import numpy as np
from jax import lax
from jax.experimental import pallas as pl
from jax.experimental.pallas import tpu as pltpu

D_MODEL = 1024
HEAD_DIM = 64
A_Q_HEADS = 8
A_KV_HEADS = 2
A_GROUPS = A_Q_HEADS // A_KV_HEADS
A_WIDTH = A_Q_HEADS * HEAD_DIM
A_KV_WIDTH = A_KV_HEADS * HEAD_DIM
B_HEADS = 8
B_WIDTH = B_HEADS * HEAD_DIM
IDX_HEADS = 8
IDX_DIM = 32
IDX_WIDTH = IDX_HEADS * IDX_DIM
WINDOW = 128
TOPK_MAX = 256
N_BUCKETS = 32
MAX_DISTANCE = 128
RMS_EPS = 1e-6

LANES = 128
TQ = 128
A_SUB = 4
TQB = 256
KCHUNK = 256
KSUPER = 2 * KCHUNK
TM_PROJ = 512
NEG = -1e30
LOG2E = math.log2(math.e)
ONES_ROWS = 16
PAIR_ROWS = 2 * HEAD_DIM + ONES_ROWS
V_ROWS = (B_HEADS // 2) * PAIR_ROWS
INT_MIN = -(2 ** 31)
VMEM_LIMIT = 56 * 1024 * 1024

_SEG = {}
_off = 0
for _name, _rows in (("qa", A_WIDTH), ("ka", A_KV_WIDTH), ("va", A_KV_WIDTH), ("za", A_WIDTH),
                     ("qb", B_WIDTH), ("kb", B_WIDTH), ("vb", B_WIDTH), ("zb", B_WIDTH),
                     ("qi", IDX_WIDTH), ("kw", LANES), ("gates", 2 * D_MODEL)):
    _SEG[_name] = (_off, _off + _rows)
    _off += _rows
PROJ_ROWS = _off
KW_REAL = IDX_DIM + IDX_HEADS

_F32 = jnp.float32
_BF16 = jnp.bfloat16
_NT = (((1,), (1,)), ((), ()))


def _inproj_kernel(x_ref, g_ref, wt_ref, gqa_ref, gka_ref, gqb_ref, gkb_ref,
                   qa_o, ka_o, va_o, za_o, qb_o, kb_o, vb_o, zb_o, qi_o, ki_o, wi_o, gt_o):
    tm = x_ref.shape[1]
    x = x_ref[0]
    ms = jnp.mean(x * x, axis=-1, keepdims=True)
    h = (x * lax.rsqrt(ms + RMS_EPS) * g_ref[...]).astype(_BF16)

    def proj(lo, hi):
        return lax.dot_general(wt_ref[lo:hi, :], h, _NT, preferred_element_type=_F32)

    def seg(name):
        return proj(*_SEG[name])

    def headnorm(p, gain_ref):
        nh = p.shape[0] // HEAD_DIM
        p3 = p.reshape(nh, HEAD_DIM, tm)
        r = lax.rsqrt(jnp.mean(p3 * p3, axis=1, keepdims=True) + RMS_EPS)
        return (p3 * r * gain_ref[...][None]).reshape(nh * HEAD_DIM, tm)

    def silu(p):
        return p * jax.nn.sigmoid(p)

    qa_o[0] = headnorm(seg("qa"), gqa_ref).astype(_BF16)
    ka_o[0] = headnorm(seg("ka"), gka_ref).T.astype(_BF16)
    va_o[0] = seg("va").astype(_BF16)
    za_o[0] = silu(seg("za")).astype(_BF16)
    qb_o[0] = headnorm(seg("qb"), gqb_ref).astype(_BF16)
    kb_o[0] = headnorm(seg("kb"), gkb_ref).T.astype(_BF16)
    vb = seg("vb").astype(_BF16)
    ones = jnp.ones((ONES_ROWS, KCHUNK), _BF16)
    for c in range(tm // KCHUNK):
        for p in range(B_HEADS // 2):
            vb_o[0, c, p * PAIR_ROWS:(p + 1) * PAIR_ROWS, :] = jnp.concatenate(
                [vb[p * LANES:(p + 1) * LANES, c * KCHUNK:(c + 1) * KCHUNK], ones], axis=0)
    zb_o[0] = silu(seg("zb")).astype(_BF16)
    qi_o[0] = seg("qi").astype(_BF16)
    kw = seg("kw")
    wi_o[0] = kw[IDX_DIM:KW_REAL, :]
    row = lax.broadcasted_iota(jnp.int32, kw.shape, 0)
    ki_o[0] = jnp.where(row < IDX_DIM, kw, 0.0).T.astype(_BF16)
    g_lo = _SEG["gates"][0]
    for c in range(2 * D_MODEL // 512):
        gt_o[0, c * 512:(c + 1) * 512, :] = jax.nn.sigmoid(
            proj(g_lo + c * 512, g_lo + (c + 1) * 512)).astype(_BF16)


def _inproj(x, g, wt, gqa, gka, gqb, gkb):
    B, S, D = x.shape
    tm = TM_PROJ
    nt = S // tm
    const = lambda b, t: (0, 0)
    tr = lambda rows: pl.BlockSpec((1, rows, tm), lambda b, t: (b, 0, t))
    nat = lambda cols: pl.BlockSpec((1, tm, cols), lambda b, t: (b, t, 0))
    sds = jax.ShapeDtypeStruct
    out_shape = (
        sds((B, A_WIDTH, S), _BF16), sds((B, S, A_KV_WIDTH), _BF16), sds((B, A_KV_WIDTH, S), _BF16),
        sds((B, A_WIDTH, S), _BF16),
        sds((B, B_WIDTH, S), _BF16), sds((B, S, B_WIDTH), _BF16),
        sds((B, S // KCHUNK, V_ROWS, KCHUNK), _BF16), sds((B, B_WIDTH, S), _BF16),
        sds((B, IDX_WIDTH, S), _BF16), sds((B, S, LANES), _BF16), sds((B, IDX_HEADS, S), _F32),
        sds((B, 2 * D_MODEL, S), _BF16),
    )
    out_specs = (
        tr(A_WIDTH), nat(A_KV_WIDTH), tr(A_KV_WIDTH), tr(A_WIDTH),
        tr(B_WIDTH), nat(B_WIDTH),
        pl.BlockSpec((1, tm // KCHUNK, V_ROWS, KCHUNK), lambda b, t: (b, t, 0, 0)), tr(B_WIDTH),
        tr(IDX_WIDTH), nat(LANES), tr(IDX_HEADS), tr(2 * D_MODEL),
    )
    in_specs = [
        pl.BlockSpec((1, tm, D), lambda b, t: (b, t, 0)),
        pl.BlockSpec((1, D), const),
        pl.BlockSpec((PROJ_ROWS, D), const, pipeline_mode=pl.Buffered(1)),
        pl.BlockSpec((HEAD_DIM, tm), const), pl.BlockSpec((HEAD_DIM, tm), const),
        pl.BlockSpec((HEAD_DIM, tm), const), pl.BlockSpec((HEAD_DIM, tm), const),
    ]
    return pl.pallas_call(
        _inproj_kernel, grid=(B, nt), in_specs=in_specs, out_specs=out_specs, out_shape=out_shape,
        compiler_params=pltpu.CompilerParams(
            dimension_semantics=("parallel", "parallel"), vmem_limit_bytes=VMEM_LIMIT),
        name="inproj",
    )(x, g, wt, gqa, gka, gqb, gkb)


def _mix_a_kernel(q_ref, kp_ref, kc_ref, vp_ref, vc_ref, z_ref, bias_ref, sink_ref, o_ref):
    i = pl.program_id(1)
    zeros = jnp.zeros((HEAD_DIM, TQ), _BF16)
    for j in range(A_SUB):
        cur = slice(j * TQ, (j + 1) * TQ)
        old = slice((j - 1) * TQ, j * TQ)
        k_old = kp_ref[0] if j == 0 else kc_ref[0, old, :]
        v_old = vp_ref[0] if j == 0 else vc_ref[0, :, old]
        kband = jnp.concatenate([k_old, kc_ref[0, cur, :]], axis=0)
        vband = jnp.concatenate([v_old, vc_ref[0, :, cur]], axis=1)
        for g in range(A_KV_HEADS):
            cols = []
            for hh in range(A_GROUPS):
                h = g * A_GROUPS + hh
                parts = [zeros] * A_KV_HEADS
                parts[g] = q_ref[0, h * HEAD_DIM:(h + 1) * HEAD_DIM, cur]
                cols.append(jnp.concatenate(parts, axis=0))
            rhs = jnp.concatenate(cols, axis=1)
            sc = jnp.dot(kband, rhs, preferred_element_type=_F32)
            probs, inv = [], []
            for hh in range(A_GROUPS):
                h = g * A_GROUPS + hh
                s = sc[:, hh * TQ:(hh + 1) * TQ] + bias_ref[h]
                if j == 0:
                    s = jnp.concatenate([jnp.where(i > 0, s[:TQ], NEG), s[TQ:]], axis=0)
                sink = sink_ref[h:h + 1, :]
                m = jnp.maximum(jnp.max(s, axis=0, keepdims=True), sink)
                p = jnp.exp(s - m)
                denom = jnp.sum(p, axis=0, keepdims=True) + jnp.exp(sink - m)
                probs.append(p.astype(_BF16))
                inv.append(1.0 / denom)
            pt = jnp.concatenate(probs, axis=1)
            out = jnp.dot(vband[g * HEAD_DIM:(g + 1) * HEAD_DIM, :], pt,
                          preferred_element_type=_F32)
            for hh in range(A_GROUPS):
                h = g * A_GROUPS + hh
                rows = slice(h * HEAD_DIM, (h + 1) * HEAD_DIM)
                o = out[:, hh * TQ:(hh + 1) * TQ] * inv[hh] * z_ref[0, rows, cur].astype(_F32)
                o_ref[0, rows, cur] = o.astype(_BF16)


def _mix_a(qa_t, ka_n, va_t, za_t, bias_a, sink_a):
    B, _, S = qa_t.shape
    tqa = A_SUB * TQ
    prev = lambda i: jnp.maximum(i * A_SUB - 1, 0)
    in_specs = [
        pl.BlockSpec((1, A_WIDTH, tqa), lambda b, i: (b, 0, i)),
        pl.BlockSpec((1, TQ, A_KV_WIDTH), lambda b, i: (b, prev(i), 0)),
        pl.BlockSpec((1, tqa, A_KV_WIDTH), lambda b, i: (b, i, 0)),
        pl.BlockSpec((1, A_KV_WIDTH, TQ), lambda b, i: (b, 0, prev(i))),
        pl.BlockSpec((1, A_KV_WIDTH, tqa), lambda b, i: (b, 0, i)),
        pl.BlockSpec((1, A_WIDTH, tqa), lambda b, i: (b, 0, i)),
        pl.BlockSpec((A_Q_HEADS, 2 * TQ, TQ), lambda b, i: (0, 0, 0)),
        pl.BlockSpec((A_Q_HEADS, TQ), lambda b, i: (0, 0)),
    ]
    return pl.pallas_call(
        _mix_a_kernel, grid=(B, S // tqa), in_specs=in_specs,
        out_specs=pl.BlockSpec((1, A_WIDTH, tqa), lambda b, i: (b, 0, i)),
        out_shape=jax.ShapeDtypeStruct((B, A_WIDTH, S), _BF16),
        compiler_params=pltpu.CompilerParams(
            dimension_semantics=("parallel", "parallel"), vmem_limit_bytes=VMEM_LIMIT),
        name="mixer_a",
    )(qa_t, ka_n, ka_n, va_t, va_t, za_t, bias_a, sink_a)


def _colsum8(v):
    return jnp.sum(v.reshape(v.shape[0] // 8, 8, v.shape[1]), axis=0)


def _as01(pred):
    return jnp.where(pred, 1, 0).astype(jnp.int32)


def _ordered_bits_to_f32(u):
    key = u ^ jnp.int32(INT_MIN)
    return pltpu.bitcast(jnp.where(key < 0, key ^ jnp.int32(0x7FFFFFFF), key), _F32)


def _colmax8(v):
    return jnp.max(v.reshape(v.shape[0] // 8, 8, v.shape[1]), axis=0)


def _by_pairs(n, step):
    def body(t, carry):
        for u in range(4):
            step(4 * t + u)
        return carry

    lax.fori_loop(0, n // 4, body, 0)
    base = (n // 4) * 4

    @pl.when((n & 2) != 0)
    def _():
        step(base)
        step(base + 1)

    @pl.when((n & 1) != 0)
    def _():
        step(base + (n & 2))


def _mix_b_kernel(qb_ref, kb_ref, vb_ref, zb_ref, qi_ref, ki_ref, wi_ref, bias_ref, o_ref,
                  keys_scr, mb_scr, sc_scr, acc_scr, m_scr, thr_scr, tie_scr):
    i = pl.program_id(1)
    nc = ((i + 1) * TQB + KCHUNK - 1) // KCHUNK
    ns = (nc + 1) // 2
    top_k = TOPK_MAX

    qi = qi_ref[0]
    zpad = jnp.zeros((LANES - IDX_DIM, TQB), _BF16)
    rhs_i = jnp.concatenate(
        [jnp.concatenate([qi[h * IDX_DIM:(h + 1) * IDX_DIM], zpad], axis=0)
         for h in range(IDX_HEADS)], axis=1)
    w = wi_ref[0] * (IDX_DIM ** -0.5 * IDX_HEADS ** -0.5)
    s_minus_t = (lax.broadcasted_iota(jnp.int32, (KCHUNK, TQB), 0)
                 - lax.broadcasted_iota(jnp.int32, (KCHUNK, TQB), 1))

    def index_step(c):
        r0 = pl.multiple_of(c * KCHUNK, KCHUNK)
        d = jnp.dot(ki_ref[0, pl.ds(r0, KCHUNK), :], rhs_i,
                    preferred_element_type=_F32)
        acc = w[0:1, :] * jnp.maximum(d[:, 0:TQB], 0.0)
        for h in range(1, IDX_HEADS):
            acc = acc + w[h:h + 1, :] * jnp.maximum(d[:, h * TQB:(h + 1) * TQB], 0.0)
        causal = s_minus_t <= (i * TQB - r0)
        keys_scr[pl.ds(r0, KCHUNK), :] = jnp.where(causal, acc, -jnp.inf)

    _by_pairs(nc, index_step)

    @pl.when(nc % 2 == 1)
    def _():
        keys_scr[pl.ds(pl.multiple_of(nc * KCHUNK, KCHUNK), KCHUNK), :] = jnp.full(
            (KCHUNK, TQB), -jnp.inf, _F32)

    def count(pred_fn):
        def body(c, acc):
            r0 = pl.multiple_of(c * KSUPER, KSUPER)
            return acc + _colsum8(pred_fn(keys_scr[pl.ds(r0, KSUPER), :], r0))
        acc = lax.fori_loop(0, ns, body, jnp.zeros((8, TQB), jnp.int32))
        return jnp.sum(acc, axis=0, keepdims=True)

    def count_ge(cand):
        n_acc = 8
        def body(c, accs):
            r0 = pl.multiple_of(c * KSUPER, KSUPER)
            accs = list(accs)
            rows = keys_scr[pl.ds(r0, KSUPER), :]
            for j in range(KSUPER // 8):
                a = accs[j % n_acc]
                accs[j % n_acc] = jnp.where(rows[j * 8:(j + 1) * 8] >= cand, a + 1, a)
            return tuple(accs)
        accs = lax.fori_loop(0, ns, body, tuple(jnp.zeros((8, TQB), jnp.int32) for _ in range(n_acc)))
        acc = accs[0]
        for a in accs[1:]:
            acc = acc + a
        return jnp.sum(acc, axis=0, keepdims=True)

    @pl.when(i * TQB + TQB <= top_k)
    def _():
        thr_scr[...] = jnp.full((1, TQB), -jnp.inf, _F32)
        tie_scr[...] = jnp.full((1, TQB), -1, jnp.int32)

    @pl.when(i * TQB + TQB > top_k)
    def _():
        def bit_body(b, t_u):
            cand_u = t_u | lax.shift_left(jnp.int32(1), 31 - b)
            cand = _ordered_bits_to_f32(cand_u)
            cnt = count_ge(cand)
            return jnp.where(cnt >= top_k, cand_u, t_u)

        t_u = lax.fori_loop(0, 32, bit_body, jnp.zeros((1, TQB), jnp.int32))
        thr = _ordered_bits_to_f32(t_u)
        cnt_gt = count(lambda k, r0: _as01(k > thr))
        cnt_ge = count(lambda k, r0: _as01(k >= thr))
        need = top_k - cnt_gt
        thr_scr[...] = thr
        tie_scr[...] = jnp.full((1, TQB), 2 ** 30, jnp.int32)
        row = lax.broadcasted_iota(jnp.int32, (KSUPER, TQB), 0)

        @pl.when(jnp.max(cnt_ge) > top_k)
        def _():
            def tie_body(b, lo):
                cand = lo | lax.shift_left(jnp.int32(1), 10 - b)
                cnt = count(lambda k, r0: jnp.where(k == thr, _as01(row < cand - r0), 0))
                return jnp.where(cnt < need, cand, lo)
            tie_scr[...] = lax.fori_loop(0, 11, tie_body, jnp.zeros((1, TQB), jnp.int32))

    thr = thr_scr[...]
    tie = tie_scr[...]
    row = lax.broadcasted_iota(jnp.int32, (KSUPER, TQB), 0)

    def mask_body(c, carry):
        r0 = pl.multiple_of(c * KSUPER, KSUPER)
        k = keys_scr[pl.ds(r0, KSUPER), :]
        mb_scr[pl.ds(r0, KSUPER), :] = jnp.where(
            k == thr, jnp.where(row <= tie - r0, 0.0, NEG), jnp.where(k > thr, 0.0, NEG))
        return carry

    lax.fori_loop(0, ns, mask_body, 0)

    zeros = jnp.zeros((HEAD_DIM, TQB), _BF16)
    n_pairs = B_HEADS // 2
    rhs = []
    for p in range(n_pairs):
        q0 = qb_ref[0, (2 * p) * HEAD_DIM:(2 * p + 1) * HEAD_DIM, :]
        q1 = qb_ref[0, (2 * p + 1) * HEAD_DIM:(2 * p + 2) * HEAD_DIM, :]
        rhs.append(jnp.concatenate([jnp.concatenate([q0, zeros], axis=0),
                                    jnp.concatenate([zeros, q1], axis=0)], axis=1))

    def score_step(c):
        r0 = pl.multiple_of(c * KCHUNK, KCHUNK)
        for p in range(n_pairs):
            s = jnp.dot(kb_ref[0, pl.ds(r0, KCHUNK), p * LANES:(p + 1) * LANES], rhs[p],
                        preferred_element_type=_F32)
            for hh in range(2):
                h = 2 * p + hh
                parts = []
                for half in range(KCHUNK // TQ):
                    jb = c * (KCHUNK // TQ) + half
                    rs = pl.multiple_of(r0 + half * TQ, TQ)
                    subs = []
                    for sub in range(TQB // TQ):
                        kind = jnp.clip(jb - (i * (TQB // TQ) + sub) + 2, 0, 2)
                        lanes = slice(sub * TQ, (sub + 1) * TQ)
                        subs.append(s[half * TQ:(half + 1) * TQ, hh * TQB + sub * TQ:hh * TQB + (sub + 1) * TQ]
                                    + bias_ref[h, kind] + mb_scr[pl.ds(rs, TQ), lanes])
                    parts.append(jnp.concatenate(subs, axis=1))
                sh = jnp.concatenate(parts, axis=0)
                sc_scr[pl.ds(r0, KCHUNK), h * TQB:(h + 1) * TQB] = sh
                m_scr[h * 8:(h + 1) * 8, :] = jnp.maximum(m_scr[h * 8:(h + 1) * 8, :], _colmax8(sh))

    m_scr[...] = jnp.full(m_scr.shape, NEG, _F32)
    _by_pairs(nc, score_step)
    m_row = jnp.concatenate(
        [jnp.max(m_scr[h * 8:(h + 1) * 8, :], axis=0, keepdims=True) for h in range(B_HEADS)],
        axis=1)
    acc_scr[...] = jnp.zeros_like(acc_scr)

    def pv_step(c):
        r0 = pl.multiple_of(c * KCHUNK, KCHUNK)
        for p in range(n_pairs):
            cols = slice(2 * p * TQB, (2 * p + 2) * TQB)
            pr = jnp.exp2(sc_scr[pl.ds(r0, KCHUNK), cols] - m_row[:, cols]).astype(_BF16)
            acc_scr[p] += jnp.dot(vb_ref[0, c, p * PAIR_ROWS:(p + 1) * PAIR_ROWS, :], pr,
                                  preferred_element_type=_F32)

    _by_pairs(nc, pv_step)
    for p in range(n_pairs):
        inv = 1.0 / acc_scr[p, 2 * HEAD_DIM:2 * HEAD_DIM + 1, :]
        for hh in range(2):
            rows = slice((2 * p + hh) * HEAD_DIM, (2 * p + hh + 1) * HEAD_DIM)
            o = (acc_scr[p, hh * HEAD_DIM:(hh + 1) * HEAD_DIM, hh * TQB:(hh + 1) * TQB]
                 * inv[:, hh * TQB:(hh + 1) * TQB] * zb_ref[0, rows, :].astype(_F32))
            o_ref[0, rows, :] = o.astype(_BF16)


def _mix_b(qb_t, kb_n, vb_c, zb_t, qi_t, ki_n, wi_t, bias_b):
    B, _, S = qb_t.shape
    nq = S // TQB
    per_q = lambda rows: pl.BlockSpec((1, rows, TQB), lambda b, i: (b, 0, i))
    in_specs = [
        per_q(B_WIDTH),
        pl.BlockSpec((1, S, B_WIDTH), lambda b, i: (b, 0, 0)),
        pl.BlockSpec((1, S // KCHUNK, V_ROWS, KCHUNK), lambda b, i: (b, 0, 0, 0)),
        per_q(B_WIDTH),
        per_q(IDX_WIDTH),
        pl.BlockSpec((1, S, LANES), lambda b, i: (b, 0, 0)),
        per_q(IDX_HEADS),
        pl.BlockSpec((B_HEADS, 3, TQ, TQ), lambda b, i: (0, 0, 0, 0)),
    ]
    scratch = [
        pltpu.VMEM((S, TQB), _F32),
        pltpu.VMEM((S, TQB), _F32),
        pltpu.VMEM((S, B_HEADS * TQB), _F32),
        pltpu.VMEM((B_HEADS // 2, PAIR_ROWS, 2 * TQB), _F32),
        pltpu.VMEM((B_HEADS * 8, TQB), _F32),
        pltpu.VMEM((1, TQB), _F32),
        pltpu.VMEM((1, TQB), jnp.int32),
    ]
    return pl.pallas_call(
        _mix_b_kernel, grid=(B, nq), in_specs=in_specs,
        out_specs=per_q(B_WIDTH),
        out_shape=jax.ShapeDtypeStruct((B, B_WIDTH, S), _BF16),
        scratch_shapes=scratch,
        compiler_params=pltpu.CompilerParams(
            dimension_semantics=("parallel", "arbitrary"), vmem_limit_bytes=VMEM_LIMIT),
        name="mixer_b",
    )(qb_t, kb_n, vb_c, zb_t, qi_t, ki_n, wi_t, bias_b)


def _merge_kernel(x_ref, ya_ref, yb_ref, gt_ref, wpa_ref, wpb_ref, wo_ref, o_ref):
    pa = jnp.dot(wpa_ref[...], ya_ref[0], preferred_element_type=_F32)
    pb = jnp.dot(wpb_ref[...], yb_ref[0], preferred_element_type=_F32)
    merged = (gt_ref[0, :D_MODEL, :].astype(_F32) * pa
              + gt_ref[0, D_MODEL:, :].astype(_F32) * pb).astype(_BF16)
    out_t = jnp.dot(wo_ref[...], merged, preferred_element_type=_F32)
    o_ref[0] = x_ref[0] + out_t.T


def _merge(x, ya_t, yb_t, gt_t, wpa_t, wpb_t, wo_t):
    B, S, D = x.shape
    tm = TM_PROJ
    const = lambda b, t: (0, 0)
    in_specs = [
        pl.BlockSpec((1, tm, D), lambda b, t: (b, t, 0)),
        pl.BlockSpec((1, A_WIDTH, tm), lambda b, t: (b, 0, t)),
        pl.BlockSpec((1, B_WIDTH, tm), lambda b, t: (b, 0, t)),
        pl.BlockSpec((1, 2 * D, tm), lambda b, t: (b, 0, t)),
        pl.BlockSpec((D, A_WIDTH), const),
        pl.BlockSpec((D, B_WIDTH), const),
        pl.BlockSpec((D, D), const),
    ]
    return pl.pallas_call(
        _merge_kernel, grid=(B, S // tm), in_specs=in_specs,
        out_specs=pl.BlockSpec((1, tm, D), lambda b, t: (b, t, 0)),
        out_shape=jax.ShapeDtypeStruct((B, S, D), x.dtype),
        compiler_params=pltpu.CompilerParams(
            dimension_semantics=("parallel", "parallel"), vmem_limit_bytes=VMEM_LIMIT),
        name="merge_out",
    )(x, ya_t, yb_t, gt_t, wpa_t, wpb_t, wo_t)


def _t5_bucket(n):
    n = np.maximum(n, 0)
    max_exact = N_BUCKETS // 2
    nf = np.maximum(n, 1).astype(np.float64)
    large = max_exact + np.floor(np.log(nf / max_exact) / math.log(MAX_DISTANCE / max_exact)
                                 * (N_BUCKETS - max_exact)).astype(np.int64)
    large = np.minimum(large, N_BUCKETS - 1)
    return np.where(n < max_exact, n, large)


def _bias_tables(rel_bias):
    t = np.arange(TQ)[None, :]
    s = np.arange(TQ)[:, None]
    d_prev = t + TQ - s
    d_cur = t - s
    table_a = rel_bias[:, :A_Q_HEADS].astype(_F32).T
    table_b = rel_bias[:, A_Q_HEADS:].astype(_F32).T

    def tile(table, dist, windowed):
        onehot = (_t5_bucket(dist)[..., None] == np.arange(N_BUCKETS)).astype(np.float32)
        b = jnp.einsum("stk,hk->hst", onehot, table, precision=lax.Precision.HIGHEST)
        if windowed:
            b = jnp.where(((dist >= 0) & (dist < WINDOW))[None], b, NEG)
        return b

    bias_a = jnp.concatenate([tile(table_a, d_prev, True), tile(table_a, d_cur, True)], axis=1)
    far = jnp.broadcast_to(table_b[:, N_BUCKETS - 1][:, None, None], (B_HEADS, TQ, TQ))
    bias_b = jnp.stack([far, tile(table_b, d_prev, False), tile(table_b, d_cur, False)], axis=1) * LOG2E
    return bias_a, bias_b


def kernel(x, norm_g, w_in, qnorm_a, knorm_a, sinks_a, qnorm_b, knorm_b, rel_bias,
           w_proj_a, w_proj_b, w_out):
    assert norm_g.shape[0] == 1, "single-layer block"
    B, S, D = x.shape
    assert D == D_MODEL and S % TM_PROJ == 0 and S % KSUPER == 0 and S % TQB == 0 and TQB % TQ == 0
    assert MAX_DISTANCE <= TQ

    wt = w_in[0].T
    n_real = _SEG["kw"][0] + KW_REAL
    wt = jnp.concatenate(
        [wt[:n_real], jnp.zeros((_SEG["kw"][1] - n_real, D), wt.dtype), wt[n_real:]], axis=0)
    wt = wt.astype(_BF16)
    bcast = lambda g, scale: jnp.broadcast_to((g.astype(_F32) * scale)[:, None], (HEAD_DIM, TM_PROJ))
    q_scale = HEAD_DIM ** -0.5
    (qa_t, ka_n, va_t, za_t, qb_t, kb_n, vb_c, zb_t, qi_t, ki_n, wi_t, gt_t) = _inproj(
        x, norm_g[0][None, :].astype(_F32), wt,
        bcast(qnorm_a[0], q_scale), bcast(knorm_a[0], 1.0),
        bcast(qnorm_b[0], q_scale * LOG2E), bcast(knorm_b[0], 1.0))

    bias_a, bias_b = _bias_tables(rel_bias)
    sink_a = jnp.broadcast_to(sinks_a[0].astype(_F32)[:, None], (A_Q_HEADS, TQ))
    ya_t = _mix_a(qa_t, ka_n, va_t, za_t, bias_a, sink_a)
    yb_t = _mix_b(qb_t, kb_n, vb_c, zb_t, qi_t, ki_n, wi_t, bias_b)
    return _merge(x, ya_t, yb_t, gt_t,
                  w_proj_a[0].T.astype(_BF16), w_proj_b[0].T.astype(_BF16), w_out[0].T.astype(_BF16))
```

```python
import functools
import math

import jax
import jax.numpy as jnp
import numpy as np
from jax import lax
from jax.experimental import pallas as pl
from jax.experimental.pallas import tpu as pltpu

D_MODEL = 1024
HEAD_DIM = 64
A_Q_HEADS = 8
A_KV_HEADS = 2
A_GROUPS = A_Q_HEADS // A_KV_HEADS
A_WIDTH = A_Q_HEADS * HEAD_DIM
A_KV_WIDTH = A_KV_HEADS * HEAD_DIM
B_HEADS = 8
B_WIDTH = B_HEADS * HEAD_DIM
IDX_HEADS = 8
IDX_DIM = 32
IDX_WIDTH = IDX_HEADS * IDX_DIM
WINDOW = 128
TOPK_MAX = 256
N_BUCKETS = 32
MAX_DISTANCE = 128
RMS_EPS = 1e-6

LANES = 128
TQ = 128
A_SUB = 4
TQB = 256
KCHUNK = 256
KSUPER = 2 * KCHUNK
TM_PROJ = 512
NEG = -1e30
LOG2E = math.log2(math.e)
ONES_ROWS = 16
PAIR_ROWS = 2 * HEAD_DIM + ONES_ROWS
V_ROWS = (B_HEADS // 2) * PAIR_ROWS
INT_MIN = -(2 ** 31)
VMEM_LIMIT = 56 * 1024 * 1024

_SEG = {}
_off = 0
for _name, _rows in (("qa", A_WIDTH), ("ka", A_KV_WIDTH), ("va", A_KV_WIDTH), ("za", A_WIDTH),
                     ("qb", B_WIDTH), ("kb", B_WIDTH), ("vb", B_WIDTH), ("zb", B_WIDTH),
                     ("qi", IDX_WIDTH), ("kw", LANES), ("gates", 2 * D_MODEL)):
    _SEG[_name] = (_off, _off + _rows)
    _off += _rows
PROJ_ROWS = _off
KW_REAL = IDX_DIM + IDX_HEADS

_F32 = jnp.float32
_BF16 = jnp.bfloat16
_NT = (((1,), (1,)), ((), ()))


def _inproj_kernel(x_ref, g_ref, wt_ref, gqa_ref, gka_ref, gqb_ref, gkb_ref,
                   qa_o, ka_o, va_o, za_o, qb_o, kb_o, vb_o, zb_o, qi_o, ki_o, wi_o, gt_o):
    tm = x_ref.shape[1]
    x = x_ref[0]
    ms = jnp.mean(x * x, axis=-1, keepdims=True)
    h = (x * lax.rsqrt(ms + RMS_EPS) * g_ref[...]).astype(_BF16)

    def proj(lo, hi):
        return lax.dot_general(wt_ref[lo:hi, :], h, _NT, preferred_element_type=_F32)

    def seg(name):
        return proj(*_SEG[name])

    def headnorm(p, gain_ref):
        nh = p.shape[0] // HEAD_DIM
        p3 = p.reshape(nh, HEAD_DIM, tm)
        r = lax.rsqrt(jnp.mean(p3 * p3, axis=1, keepdims=True) + RMS_EPS)
        return (p3 * r * gain_ref[...][None]).reshape(nh * HEAD_DIM, tm)

    def silu(p):
        return p * jax.nn.sigmoid(p)

    qa_o[0] = headnorm(seg("qa"), gqa_ref).astype(_BF16)
    ka_o[0] = headnorm(seg("ka"), gka_ref).T.astype(_BF16)
    va_o[0] = seg("va").astype(_BF16)
    za_o[0] = silu(seg("za")).astype(_BF16)
    qb_o[0] = headnorm(seg("qb"), gqb_ref).astype(_BF16)
    kb_o[0] = headnorm(seg("kb"), gkb_ref).T.astype(_BF16)
    vb = seg("vb").astype(_BF16)
    ones = jnp.ones((ONES_ROWS, KCHUNK), _BF16)
    for c in range(tm // KCHUNK):
        for p in range(B_HEADS // 2):
            vb_o[0, c, p * PAIR_ROWS:(p + 1) * PAIR_ROWS, :] = jnp.concatenate(
                [vb[p * LANES:(p + 1) * LANES, c * KCHUNK:(c + 1) * KCHUNK], ones], axis=0)
    zb_o[0] = silu(seg("zb")).astype(_BF16)
    qi_o[0] = seg("qi").astype(_BF16)
    kw = seg("kw")
    wi_o[0] = kw[IDX_DIM:KW_REAL, :]
    row = lax.broadcasted_iota(jnp.int32, kw.shape, 0)
    ki_o[0] = jnp.where(row < IDX_DIM, kw, 0.0).T.astype(_BF16)
    g_lo = _SEG["gates"][0]
    for c in range(2 * D_MODEL // 512):
        gt_o[0, c * 512:(c + 1) * 512, :] = jax.nn.sigmoid(
            proj(g_lo + c * 512, g_lo + (c + 1) * 512)).astype(_BF16)


def _inproj(x, g, wt, gqa, gka, gqb, gkb):
    B, S, D = x.shape
    tm = TM_PROJ
    nt = S // tm
    const = lambda b, t: (0, 0)
    tr = lambda rows: pl.BlockSpec((1, rows, tm), lambda b, t: (b, 0, t))
    nat = lambda cols: pl.BlockSpec((1, tm, cols), lambda b, t: (b, t, 0))
    sds = jax.ShapeDtypeStruct
    out_shape = (
        sds((B, A_WIDTH, S), _BF16), sds((B, S, A_KV_WIDTH), _BF16), sds((B, A_KV_WIDTH, S), _BF16),
        sds((B, A_WIDTH, S), _BF16),
        sds((B, B_WIDTH, S), _BF16), sds((B, S, B_WIDTH), _BF16),
        sds((B, S // KCHUNK, V_ROWS, KCHUNK), _BF16), sds((B, B_WIDTH, S), _BF16),
        sds((B, IDX_WIDTH, S), _BF16), sds((B, S, LANES), _BF16), sds((B, IDX_HEADS, S), _F32),
        sds((B, 2 * D_MODEL, S), _BF16),
    )
    out_specs = (
        tr(A_WIDTH), nat(A_KV_WIDTH), tr(A_KV_WIDTH), tr(A_WIDTH),
        tr(B_WIDTH), nat(B_WIDTH),
        pl.BlockSpec((1, tm // KCHUNK, V_ROWS, KCHUNK), lambda b, t: (b, t, 0, 0)), tr(B_WIDTH),
        tr(IDX_WIDTH), nat(LANES), tr(IDX_HEADS), tr(2 * D_MODEL),
    )
    in_specs = [
        pl.BlockSpec((1, tm, D), lambda b, t: (b, t, 0)),
        pl.BlockSpec((1, D), const),
        pl.BlockSpec((PROJ_ROWS, D), const, pipeline_mode=pl.Buffered(1)),
        pl.BlockSpec((HEAD_DIM, tm), const), pl.BlockSpec((HEAD_DIM, tm), const),
        pl.BlockSpec((HEAD_DIM, tm), const), pl.BlockSpec((HEAD_DIM, tm), const),
    ]
    return pl.pallas_call(
        _inproj_kernel, grid=(B, nt), in_specs=in_specs, out_specs=out_specs, out_shape=out_shape,
        compiler_params=pltpu.CompilerParams(
            dimension_semantics=("parallel", "parallel"), vmem_limit_bytes=VMEM_LIMIT),
        name="inproj",
    )(x, g, wt, gqa, gka, gqb, gkb)


def _mix_a_kernel(q_ref, kp_ref, kc_ref, vp_ref, vc_ref, z_ref, bias_ref, sink_ref, o_ref):
    i = pl.program_id(1)
    zeros = jnp.zeros((HEAD_DIM, TQ), _BF16)
    for j in range(A_SUB):
        cur = slice(j * TQ, (j + 1) * TQ)
        old = slice((j - 1) * TQ, j * TQ)
        k_old = kp_ref[0] if j == 0 else kc_ref[0, old, :]
        v_old = vp_ref[0] if j == 0 else vc_ref[0, :, old]
        kband = jnp.concatenate([k_old, kc_ref[0, cur, :]], axis=0)
        vband = jnp.concatenate([v_old, vc_ref[0, :, cur]], axis=1)
        for g in range(A_KV_HEADS):
            cols = []
            for hh in range(A_GROUPS):
                h = g * A_GROUPS + hh
                parts = [zeros] * A_KV_HEADS
                parts[g] = q_ref[0, h * HEAD_DIM:(h + 1) * HEAD_DIM, cur]
                cols.append(jnp.concatenate(parts, axis=0))
            rhs = jnp.concatenate(cols, axis=1)
            sc = jnp.dot(kband, rhs, preferred_element_type=_F32)
            probs, inv = [], []
            for hh in range(A_GROUPS):
                h = g * A_GROUPS + hh
                s = sc[:, hh * TQ:(hh + 1) * TQ] + bias_ref[h]
                if j == 0:
                    s = jnp.concatenate([jnp.where(i > 0, s[:TQ], NEG), s[TQ:]], axis=0)
                sink = sink_ref[h:h + 1, :]
                m = jnp.maximum(jnp.max(s, axis=0, keepdims=True), sink)
                p = jnp.exp(s - m)
                denom = jnp.sum(p, axis=0, keepdims=True) + jnp.exp(sink - m)
                probs.append(p.astype(_BF16))
                inv.append(1.0 / denom)
            pt = jnp.concatenate(probs, axis=1)
            out = jnp.dot(vband[g * HEAD_DIM:(g + 1) * HEAD_DIM, :], pt,
                          preferred_element_type=_F32)
            for hh in range(A_GROUPS):
                h = g * A_GROUPS + hh
                rows = slice(h * HEAD_DIM, (h + 1) * HEAD_DIM)
                o = out[:, hh * TQ:(hh + 1) * TQ] * inv[hh] * z_ref[0, rows, cur].astype(_F32)
                o_ref[0, rows, cur] = o.astype(_BF16)


def _mix_a(qa_t, ka_n, va_t, za_t, bias_a, sink_a):
    B, _, S = qa_t.shape
    tqa = A_SUB * TQ
    prev = lambda i: jnp.maximum(i * A_SUB - 1, 0)
    in_specs = [
        pl.BlockSpec((1, A_WIDTH, tqa), lambda b, i: (b, 0, i)),
        pl.BlockSpec((1, TQ, A_KV_WIDTH), lambda b, i: (b, prev(i), 0)),
        pl.BlockSpec((1, tqa, A_KV_WIDTH), lambda b, i: (b, i, 0)),
        pl.BlockSpec((1, A_KV_WIDTH, TQ), lambda b, i: (b, 0, prev(i))),
        pl.BlockSpec((1, A_KV_WIDTH, tqa), lambda b, i: (b, 0, i)),
        pl.BlockSpec((1, A_WIDTH, tqa), lambda b, i: (b, 0, i)),
        pl.BlockSpec((A_Q_HEADS, 2 * TQ, TQ), lambda b, i: (0, 0, 0)),
        pl.BlockSpec((A_Q_HEADS, TQ), lambda b, i: (0, 0)),
    ]
    return pl.pallas_call(
        _mix_a_kernel, grid=(B, S // tqa), in_specs=in_specs,
        out_specs=pl.BlockSpec((1, A_WIDTH, tqa), lambda b, i: (b, 0, i)),
        out_shape=jax.ShapeDtypeStruct((B, A_WIDTH, S), _BF16),
        compiler_params=pltpu.CompilerParams(
            dimension_semantics=("parallel", "parallel"), vmem_limit_bytes=VMEM_LIMIT),
        name="mixer_a",
    )(qa_t, ka_n, ka_n, va_t, va_t, za_t, bias_a, sink_a)


def _colsum8(v):
    return jnp.sum(v.reshape(v.shape[0] // 8, 8, v.shape[1]), axis=0)


def _as01(pred):
    return jnp.where(pred, 1, 0).astype(jnp.int32)


def _ordered_bits_to_f32(u):
    key = u ^ jnp.int32(INT_MIN)
    return pltpu.bitcast(jnp.where(key < 0, key ^ jnp.int32(0x7FFFFFFF), key), _F32)


def _colmax8(v):
    return jnp.max(v.reshape(v.shape[0] // 8, 8, v.shape[1]), axis=0)


def _by_pairs(n, step):
    def body(t, carry):
        for u in range(4):
            step(4 * t + u)
        return carry

    lax.fori_loop(0, n // 4, body, 0)
    base = (n // 4) * 4

    @pl.when((n & 2) != 0)
    def _():
        step(base)
        step(base + 1)

    @pl.when((n & 1) != 0)
    def _():
        step(base + (n & 2))


def _mix_b_kernel(qb_ref, kb_ref, vb_ref, zb_ref, qi_ref, ki_ref, wi_ref, bias_ref, o_ref,
                  keys_scr, mb_scr, sc_scr, acc_scr, m_scr, thr_scr, tie_scr):
    i = pl.program_id(1)
    nc = ((i + 1) * TQB + KCHUNK - 1) // KCHUNK
    ns = (nc + 1) // 2
    top_k = TOPK_MAX

    qi = qi_ref[0]
    zpad = jnp.zeros((LANES - IDX_DIM, TQB), _BF16)
    rhs_i = jnp.concatenate(
        [jnp.concatenate([qi[h * IDX_DIM:(h + 1) * IDX_DIM], zpad], axis=0)
         for h in range(IDX_HEADS)], axis=1)
    w = wi_ref[0] * (IDX_DIM ** -0.5 * IDX_HEADS ** -0.5)
    s_minus_t = (lax.broadcasted_iota(jnp.int32, (KCHUNK, TQB), 0)
                 - lax.broadcasted_iota(jnp.int32, (KCHUNK, TQB), 1))

    def index_step(c):
        r0 = pl.multiple_of(c * KCHUNK, KCHUNK)
        d = jnp.dot(ki_ref[0, pl.ds(r0, KCHUNK), :], rhs_i,
                    preferred_element_type=_F32)
        acc = w[0:1, :] * jnp.maximum(d[:, 0:TQB], 0.0)
        for h in range(1, IDX_HEADS):
            acc = acc + w[h:h + 1, :] * jnp.maximum(d[:, h * TQB:(h + 1) * TQB], 0.0)
        causal = s_minus_t <= (i * TQB - r0)
        keys_scr[pl.ds(r0, KCHUNK), :] = jnp.where(causal, acc, -jnp.inf)

    _by_pairs(nc, index_step)

    @pl.when(nc % 2 == 1)
    def _():
        keys_scr[pl.ds(pl.multiple_of(nc * KCHUNK, KCHUNK), KCHUNK), :] = jnp.full(
            (KCHUNK, TQB), -jnp.inf, _F32)

    def count(pred_fn):
        def body(c, acc):
            r0 = pl.multiple_of(c * KSUPER, KSUPER)
            return acc + _colsum8(pred_fn(keys_scr[pl.ds(r0, KSUPER), :], r0))
        acc = lax.fori_loop(0, ns, body, jnp.zeros((8, TQB), jnp.int32))
        return jnp.sum(acc, axis=0, keepdims=True)

    def count_ge(cand):
        n_acc = 8

        def bump(accs, r0, n_rows):
            accs = list(accs)
            rows = keys_scr[pl.ds(r0, n_rows), :]
            for j in range(n_rows // 8):
                a = accs[j % n_acc]
                accs[j % n_acc] = jnp.where(rows[j * 8:(j + 1) * 8] >= cand, a + 1, a)
            return tuple(accs)

        accs = lax.fori_loop(
            0, nc // 2, lambda c, accs: bump(accs, pl.multiple_of(c * KSUPER, KSUPER), KSUPER),
            tuple(jnp.zeros((8, TQB), jnp.int32) for _ in range(n_acc)))
        accs = lax.cond(nc % 2 == 1,
                        lambda accs: bump(accs, pl.multiple_of((nc - 1) * KCHUNK, KCHUNK), KCHUNK),
                        lambda accs: accs, accs)
        acc = accs[0]
        for a in accs[1:]:
            acc = acc + a
        return jnp.sum(acc, axis=0, keepdims=True)

    @pl.when(i * TQB + TQB <= top_k)
    def _():
        thr_scr[...] = jnp.full((1, TQB), -jnp.inf, _F32)
        tie_scr[...] = jnp.full((1, TQB), -1, jnp.int32)

    @pl.when(i * TQB + TQB > top_k)
    def _():
        def bit_body(b, t_u):
            cand_u = t_u | lax.shift_left(jnp.int32(1), 31 - b)
            cand = _ordered_bits_to_f32(cand_u)
            cnt = count_ge(cand)
            return jnp.where(cnt >= top_k, cand_u, t_u)

        t_u = lax.fori_loop(0, 32, bit_body, jnp.zeros((1, TQB), jnp.int32))
        thr = _ordered_bits_to_f32(t_u)
        cnt_gt = count(lambda k, r0: _as01(k > thr))
        cnt_ge = count(lambda k, r0: _as01(k >= thr))
        need = top_k - cnt_gt
        thr_scr[...] = thr
        tie_scr[...] = jnp.full((1, TQB), 2 ** 30, jnp.int32)
        row = lax.broadcasted_iota(jnp.int32, (KSUPER, TQB), 0)

        @pl.when(jnp.max(cnt_ge) > top_k)
        def _():
            def tie_body(b, lo):
                cand = lo | lax.shift_left(jnp.int32(1), 10 - b)
                cnt = count(lambda k, r0: jnp.where(k == thr, _as01(row < cand - r0), 0))
                return jnp.where(cnt < need, cand, lo)
            tie_scr[...] = lax.fori_loop(0, 11, tie_body, jnp.zeros((1, TQB), jnp.int32))

    thr = thr_scr[...]
    tie = tie_scr[...]
    row = lax.broadcasted_iota(jnp.int32, (KSUPER, TQB), 0)

    def mask_body(c, carry):
        r0 = pl.multiple_of(c * KSUPER, KSUPER)
        k = keys_scr[pl.ds(r0, KSUPER), :]
        mb_scr[pl.ds(r0, KSUPER), :] = jnp.where(
            k == thr, jnp.where(row <= tie - r0, 0.0, NEG), jnp.where(k > thr, 0.0, NEG))
        return carry

    lax.fori_loop(0, ns, mask_body, 0)

    zeros = jnp.zeros((HEAD_DIM, TQB), _BF16)
    n_pairs = B_HEADS // 2
    rhs = []
    for p in range(n_pairs):
        q0 = qb_ref[0, (2 * p) * HEAD_DIM:(2 * p + 1) * HEAD_DIM, :]
        q1 = qb_ref[0, (2 * p + 1) * HEAD_DIM:(2 * p + 2) * HEAD_DIM, :]
        rhs.append(jnp.concatenate([jnp.concatenate([q0, zeros], axis=0),
                                    jnp.concatenate([zeros, q1], axis=0)], axis=1))

    def score_step(c):
        r0 = pl.multiple_of(c * KCHUNK, KCHUNK)
        for p in range(n_pairs):
            s = jnp.dot(kb_ref[0, pl.ds(r0, KCHUNK), p * LANES:(p + 1) * LANES], rhs[p],
                        preferred_element_type=_F32)
            for hh in range(2):
                h = 2 * p + hh
                parts = []
                for half in range(KCHUNK // TQ):
                    jb = c * (KCHUNK // TQ) + half
                    rs = pl.multiple_of(r0 + half * TQ, TQ)
                    subs = []
                    for sub in range(TQB // TQ):
                        kind = jnp.clip(jb - (i * (TQB // TQ) + sub) + 2, 0, 2)
                        lanes = slice(sub * TQ, (sub + 1) * TQ)
                        subs.append(s[half * TQ:(half + 1) * TQ, hh * TQB + sub * TQ:hh * TQB + (sub + 1) * TQ]
                                    + bias_ref[h, kind] + mb_scr[pl.ds(rs, TQ), lanes])
                    parts.append(jnp.concatenate(subs, axis=1))
                sh = jnp.concatenate(parts, axis=0)
                sc_scr[pl.ds(r0, KCHUNK), h * TQB:(h + 1) * TQB] = sh
                m_scr[h * 8:(h + 1) * 8, :] = jnp.maximum(m_scr[h * 8:(h + 1) * 8, :], _colmax8(sh))

    m_scr[...] = jnp.full(m_scr.shape, NEG, _F32)
    _by_pairs(nc, score_step)
    m_row = jnp.concatenate(
        [jnp.max(m_scr[h * 8:(h + 1) * 8, :], axis=0, keepdims=True) for h in range(B_HEADS)],
        axis=1)
    acc_scr[...] = jnp.zeros_like(acc_scr)

    def pv_step(c):
        r0 = pl.multiple_of(c * KCHUNK, KCHUNK)
        for p in range(n_pairs):
            cols = slice(2 * p * TQB, (2 * p + 2) * TQB)
            pr = jnp.exp2(sc_scr[pl.ds(r0, KCHUNK), cols] - m_row[:, cols]).astype(_BF16)
            acc_scr[p] += jnp.dot(vb_ref[0, c, p * PAIR_ROWS:(p + 1) * PAIR_ROWS, :], pr,
                                  preferred_element_type=_F32)

    _by_pairs(nc, pv_step)
    for p in range(n_pairs):
        inv = 1.0 / acc_scr[p, 2 * HEAD_DIM:2 * HEAD_DIM + 1, :]
        for hh in range(2):
            rows = slice((2 * p + hh) * HEAD_DIM, (2 * p + hh + 1) * HEAD_DIM)
            o = (acc_scr[p, hh * HEAD_DIM:(hh + 1) * HEAD_DIM, hh * TQB:(hh + 1) * TQB]
                 * inv[:, hh * TQB:(hh + 1) * TQB] * zb_ref[0, rows, :].astype(_F32))
            o_ref[0, rows, :] = o.astype(_BF16)


def _mix_b(qb_t, kb_n, vb_c, zb_t, qi_t, ki_n, wi_t, bias_b):
    B, _, S = qb_t.shape
    nq = S // TQB
    per_q = lambda rows: pl.BlockSpec((1, rows, TQB), lambda b, i: (b, 0, i))
    in_specs = [
        per_q(B_WIDTH),
        pl.BlockSpec((1, S, B_WIDTH), lambda b, i: (b, 0, 0)),
        pl.BlockSpec((1, S // KCHUNK, V_ROWS, KCHUNK), lambda b, i: (b, 0, 0, 0)),
        per_q(B_WIDTH),
        per_q(IDX_WIDTH),
        pl.BlockSpec((1, S, LANES), lambda b, i: (b, 0, 0)),
        per_q(IDX_HEADS),
        pl.BlockSpec((B_HEADS, 3, TQ, TQ), lambda b, i: (0, 0, 0, 0)),
    ]
    scratch = [
        pltpu.VMEM((S, TQB), _F32),
        pltpu.VMEM((S, TQB), _F32),
        pltpu.VMEM((S, B_HEADS * TQB), _F32),
        pltpu.VMEM((B_HEADS // 2, PAIR_ROWS, 2 * TQB), _F32),
        pltpu.VMEM((B_HEADS * 8, TQB), _F32),
        pltpu.VMEM((1, TQB), _F32),
        pltpu.VMEM((1, TQB), jnp.int32),
    ]
    return pl.pallas_call(
        _mix_b_kernel, grid=(B, nq), in_specs=in_specs,
        out_specs=per_q(B_WIDTH),
        out_shape=jax.ShapeDtypeStruct((B, B_WIDTH, S), _BF16),
        scratch_shapes=scratch,
        compiler_params=pltpu.CompilerParams(
            dimension_semantics=("parallel", "arbitrary"), vmem_limit_bytes=VMEM_LIMIT),
        name="mixer_b",
    )(qb_t, kb_n, vb_c, zb_t, qi_t, ki_n, wi_t, bias_b)


def _merge_kernel(x_ref, ya_ref, yb_ref, gt_ref, wpa_ref, wpb_ref, wo_ref, o_ref):
    pa = jnp.dot(wpa_ref[...], ya_ref[0], preferred_element_type=_F32)
    pb = jnp.dot(wpb_ref[...], yb_ref[0], preferred_element_type=_F32)
    merged = (gt_ref[0, :D_MODEL, :].astype(_F32) * pa
              + gt_ref[0, D_MODEL:, :].astype(_F32) * pb).astype(_BF16)
    out_t = jnp.dot(wo_ref[...], merged, preferred_element_type=_F32)
    o_ref[0] = x_ref[0] + out_t.T


def _merge(x, ya_t, yb_t, gt_t, wpa_t, wpb_t, wo_t):
    B, S, D = x.shape
    tm = TM_PROJ
    const = lambda b, t: (0, 0)
    in_specs = [
        pl.BlockSpec((1, tm, D), lambda b, t: (b, t, 0)),
        pl.BlockSpec((1, A_WIDTH, tm), lambda b, t: (b, 0, t)),
        pl.BlockSpec((1, B_WIDTH, tm), lambda b, t: (b, 0, t)),
        pl.BlockSpec((1, 2 * D, tm), lambda b, t: (b, 0, t)),
        pl.BlockSpec((D, A_WIDTH), const),
        pl.BlockSpec((D, B_WIDTH), const),
        pl.BlockSpec((D, D), const),
    ]
    return pl.pallas_call(
        _merge_kernel, grid=(B, S // tm), in_specs=in_specs,
        out_specs=pl.BlockSpec((1, tm, D), lambda b, t: (b, t, 0)),
        out_shape=jax.ShapeDtypeStruct((B, S, D), x.dtype),
        compiler_params=pltpu.CompilerParams(
            dimension_semantics=("parallel", "parallel"), vmem_limit_bytes=VMEM_LIMIT),
        name="merge_out",
    )(x, ya_t, yb_t, gt_t, wpa_t, wpb_t, wo_t)


def _t5_bucket(n):
    n = np.maximum(n, 0)
    max_exact = N_BUCKETS // 2
    nf = np.maximum(n, 1).astype(np.float64)
    large = max_exact + np.floor(np.log(nf / max_exact) / math.log(MAX_DISTANCE / max_exact)
                                 * (N_BUCKETS - max_exact)).astype(np.int64)
    large = np.minimum(large, N_BUCKETS - 1)
    return np.where(n < max_exact, n, large)


def _bias_tables(rel_bias):
    t = np.arange(TQ)[None, :]
    s = np.arange(TQ)[:, None]
    d_prev = t + TQ - s
    d_cur = t - s
    table_a = rel_bias[:, :A_Q_HEADS].astype(_F32).T
    table_b = rel_bias[:, A_Q_HEADS:].astype(_F32).T

    def tile(table, dist, windowed):
        onehot = (_t5_bucket(dist)[..., None] == np.arange(N_BUCKETS)).astype(np.float32)
        b = jnp.einsum("stk,hk->hst", onehot, table, precision=lax.Precision.HIGHEST)
        if windowed:
            b = jnp.where(((dist >= 0) & (dist < WINDOW))[None], b, NEG)
        return b

    bias_a = jnp.concatenate([tile(table_a, d_prev, True), tile(table_a, d_cur, True)], axis=1)
    far = jnp.broadcast_to(table_b[:, N_BUCKETS - 1][:, None, None], (B_HEADS, TQ, TQ))
    bias_b = jnp.stack([far, tile(table_b, d_prev, False), tile(table_b, d_cur, False)], axis=1) * LOG2E
    return bias_a, bias_b


def kernel(x, norm_g, w_in, qnorm_a, knorm_a, sinks_a, qnorm_b, knorm_b, rel_bias,
           w_proj_a, w_proj_b, w_out):
    assert norm_g.shape[0] == 1, "single-layer block"
    B, S, D = x.shape
    assert D == D_MODEL and S % TM_PROJ == 0 and S % KSUPER == 0 and S % TQB == 0 and TQB % TQ == 0
    assert MAX_DISTANCE <= TQ

    wt = w_in[0].T
    n_real = _SEG["kw"][0] + KW_REAL
    wt = jnp.concatenate(
        [wt[:n_real], jnp.zeros((_SEG["kw"][1] - n_real, D), wt.dtype), wt[n_real:]], axis=0)
    wt = wt.astype(_BF16)
    bcast = lambda g, scale: jnp.broadcast_to((g.astype(_F32) * scale)[:, None], (HEAD_DIM, TM_PROJ))
    q_scale = HEAD_DIM ** -0.5
    (qa_t, ka_n, va_t, za_t, qb_t, kb_n, vb_c, zb_t, qi_t, ki_n, wi_t, gt_t) = _inproj(
        x, norm_g[0][None, :].astype(_F32), wt,
        bcast(qnorm_a[0], q_scale), bcast(knorm_a[0], 1.0),
        bcast(qnorm_b[0], q_scale * LOG2E), bcast(knorm_b[0], 1.0))

    bias_a, bias_b = _bias_tables(rel_bias)
    sink_a = jnp.broadcast_to(sinks_a[0].astype(_F32)[:, None], (A_Q_HEADS, TQ))
    ya_t = _mix_a(qa_t, ka_n, va_t, za_t, bias_a, sink_a)
    yb_t = _mix_b(qb_t, kb_n, vb_c, zb_t, qi_t, ki_n, wi_t, bias_b)
    return _merge(x, ya_t, yb_t, gt_t,
                  w_proj_a[0].T.astype(_BF16), w_proj_b[0].T.astype(_BF16), w_out[0].T.astype(_BF16))
```

```python
import functools
import math

import jax
import jax.numpy as jnp
import numpy as np
from jax import lax
from jax.experimental import pallas as pl
from jax.experimental.pallas import tpu as pltpu

D_MODEL = 1024
HEAD_DIM = 64
A_Q_HEADS = 8
A_KV_HEADS = 2
A_GROUPS = A_Q_HEADS // A_KV_HEADS
A_WIDTH = A_Q_HEADS * HEAD_DIM
A_KV_WIDTH = A_KV_HEADS * HEAD_DIM
B_HEADS = 8
B_WIDTH = B_HEADS * HEAD_DIM
IDX_HEADS = 8
IDX_DIM = 32
IDX_WIDTH = IDX_HEADS * IDX_DIM
WINDOW = 128
TOPK_MAX = 256
N_BUCKETS = 32
MAX_DISTANCE = 128
RMS_EPS = 1e-6

LANES = 128
TQ = 128
A_SUB = 4
TQB = 256
KCHUNK = 256
KSUPER = 2 * KCHUNK
TM_PROJ = 1024
NEG = -1e30
LOG2E = math.log2(math.e)
ONES_ROWS = 16
PAIR_ROWS = 2 * HEAD_DIM + ONES_ROWS
V_ROWS = (B_HEADS // 2) * PAIR_ROWS
INT_MIN = -(2 ** 31)
VMEM_LIMIT = 56 * 1024 * 1024

_SEG = {}
_off = 0
for _name, _rows in (("qa", A_WIDTH), ("ka", A_KV_WIDTH), ("va", A_KV_WIDTH), ("za", A_WIDTH),
                     ("qb", B_WIDTH), ("kb", B_WIDTH), ("vb", B_WIDTH), ("zb", B_WIDTH),
                     ("qi", IDX_WIDTH), ("kw", LANES), ("gates", 2 * D_MODEL)):
    _SEG[_name] = (_off, _off + _rows)
    _off += _rows
PROJ_ROWS = _off
KW_REAL = IDX_DIM + IDX_HEADS

_F32 = jnp.float32
_BF16 = jnp.bfloat16
_NT = (((1,), (1,)), ((), ()))


def _inproj_kernel(x_ref, g_ref, wt_ref, gqa_ref, gka_ref, gqb_ref, gkb_ref,
                   qa_o, ka_o, va_o, za_o, qb_o, kb_o, vb_o, zb_o, qi_o, ki_o, wi_o, gt_o):
    tm = x_ref.shape[1]
    x = x_ref[0]
    ms = jnp.mean(x * x, axis=-1, keepdims=True)
    h = (x * lax.rsqrt(ms + RMS_EPS) * g_ref[...]).astype(_BF16)

    def proj(lo, hi):
        return lax.dot_general(wt_ref[lo:hi, :], h, _NT, preferred_element_type=_F32)

    def seg(name):
        return proj(*_SEG[name])

    def headnorm(p, gain_ref):
        nh = p.shape[0] // HEAD_DIM
        p3 = p.reshape(nh, HEAD_DIM, tm)
        r = lax.rsqrt(jnp.mean(p3 * p3, axis=1, keepdims=True) + RMS_EPS)
        return (p3 * r * gain_ref[...][None]).reshape(nh * HEAD_DIM, tm)

    def silu(p):
        return p * jax.nn.sigmoid(p)

    qa_o[0] = headnorm(seg("qa"), gqa_ref).astype(_BF16)
    ka_o[0] = headnorm(seg("ka"), gka_ref).T.astype(_BF16)
    va_o[0] = seg("va").astype(_BF16)
    za_o[0] = silu(seg("za")).astype(_BF16)
    qb_o[0] = headnorm(seg("qb"), gqb_ref).astype(_BF16)
    kb_o[0] = headnorm(seg("kb"), gkb_ref).T.astype(_BF16)
    vb = seg("vb").astype(_BF16)
    ones = jnp.ones((ONES_ROWS, KCHUNK), _BF16)
    for c in range(tm // KCHUNK):
        for p in range(B_HEADS // 2):
            vb_o[0, c, p * PAIR_ROWS:(p + 1) * PAIR_ROWS, :] = jnp.concatenate(
                [vb[p * LANES:(p + 1) * LANES, c * KCHUNK:(c + 1) * KCHUNK], ones], axis=0)
    zb_o[0] = silu(seg("zb")).astype(_BF16)
    qi_o[0] = seg("qi").astype(_BF16)
    kw = seg("kw")
    wi_o[0] = kw[IDX_DIM:KW_REAL, :]
    row = lax.broadcasted_iota(jnp.int32, kw.shape, 0)
    ki_o[0] = jnp.where(row < IDX_DIM, kw, 0.0).T.astype(_BF16)
    g_lo = _SEG["gates"][0]
    for c in range(2 * D_MODEL // 512):
        gt_o[0, c * 512:(c + 1) * 512, :] = jax.nn.sigmoid(
            proj(g_lo + c * 512, g_lo + (c + 1) * 512)).astype(_BF16)


def _inproj(x, g, wt, gqa, gka, gqb, gkb):
    B, S, D = x.shape
    tm = TM_PROJ
    nt = S // tm
    const = lambda b, t: (0, 0)
    tr = lambda rows: pl.BlockSpec((1, rows, tm), lambda b, t: (b, 0, t))
    nat = lambda cols: pl.BlockSpec((1, tm, cols), lambda b, t: (b, t, 0))
    sds = jax.ShapeDtypeStruct
    out_shape = (
        sds((B, A_WIDTH, S), _BF16), sds((B, S, A_KV_WIDTH), _BF16), sds((B, A_KV_WIDTH, S), _BF16),
        sds((B, A_WIDTH, S), _BF16),
        sds((B, B_WIDTH, S), _BF16), sds((B, S, B_WIDTH), _BF16),
        sds((B, S // KCHUNK, V_ROWS, KCHUNK), _BF16), sds((B, B_WIDTH, S), _BF16),
        sds((B, IDX_WIDTH, S), _BF16), sds((B, S, LANES), _BF16), sds((B, IDX_HEADS, S), _F32),
        sds((B, 2 * D_MODEL, S), _BF16),
    )
    out_specs = (
        tr(A_WIDTH), nat(A_KV_WIDTH), tr(A_KV_WIDTH), tr(A_WIDTH),
        tr(B_WIDTH), nat(B_WIDTH),
        pl.BlockSpec((1, tm // KCHUNK, V_ROWS, KCHUNK), lambda b, t: (b, t, 0, 0)), tr(B_WIDTH),
        tr(IDX_WIDTH), nat(LANES), tr(IDX_HEADS), tr(2 * D_MODEL),
    )
    in_specs = [
        pl.BlockSpec((1, tm, D), lambda b, t: (b, t, 0)),
        pl.BlockSpec((1, D), const),
        pl.BlockSpec((PROJ_ROWS, D), const, pipeline_mode=pl.Buffered(1)),
        pl.BlockSpec((HEAD_DIM, tm), const), pl.BlockSpec((HEAD_DIM, tm), const),
        pl.BlockSpec((HEAD_DIM, tm), const), pl.BlockSpec((HEAD_DIM, tm), const),
    ]
    return pl.pallas_call(
        _inproj_kernel, grid=(B, nt), in_specs=in_specs, out_specs=out_specs, out_shape=out_shape,
        compiler_params=pltpu.CompilerParams(
            dimension_semantics=("parallel", "parallel"), vmem_limit_bytes=VMEM_LIMIT),
        name="inproj",
    )(x, g, wt, gqa, gka, gqb, gkb)


def _mix_a_kernel(q_ref, kp_ref, kc_ref, vp_ref, vc_ref, z_ref, bias_ref, sink_ref, o_ref):
    i = pl.program_id(1)
    zeros = jnp.zeros((HEAD_DIM, TQ), _BF16)
    for j in range(A_SUB):
        cur = slice(j * TQ, (j + 1) * TQ)
        old = slice((j - 1) * TQ, j * TQ)
        k_old = kp_ref[0] if j == 0 else kc_ref[0, old, :]
        v_old = vp_ref[0] if j == 0 else vc_ref[0, :, old]
        kband = jnp.concatenate([k_old, kc_ref[0, cur, :]], axis=0)
        vband = jnp.concatenate([v_old, vc_ref[0, :, cur]], axis=1)
        for g in range(A_KV_HEADS):
            cols = []
            for hh in range(A_GROUPS):
                h = g * A_GROUPS + hh
                parts = [zeros] * A_KV_HEADS
                parts[g] = q_ref[0, h * HEAD_DIM:(h + 1) * HEAD_DIM, cur]
                cols.append(jnp.concatenate(parts, axis=0))
            rhs = jnp.concatenate(cols, axis=1)
            sc = jnp.dot(kband, rhs, preferred_element_type=_F32)
            probs, inv = [], []
            for hh in range(A_GROUPS):
                h = g * A_GROUPS + hh
                s = sc[:, hh * TQ:(hh + 1) * TQ] + bias_ref[h]
                if j == 0:
                    s = jnp.concatenate([jnp.where(i > 0, s[:TQ], NEG), s[TQ:]], axis=0)
                sink = sink_ref[h:h + 1, :]
                m = jnp.maximum(jnp.max(s, axis=0, keepdims=True), sink)
                p = jnp.exp(s - m)
                denom = jnp.sum(p, axis=0, keepdims=True) + jnp.exp(sink - m)
                probs.append(p.astype(_BF16))
                inv.append(1.0 / denom)
            pt = jnp.concatenate(probs, axis=1)
            out = jnp.dot(vband[g * HEAD_DIM:(g + 1) * HEAD_DIM, :], pt,
                          preferred_element_type=_F32)
            for hh in range(A_GROUPS):
                h = g * A_GROUPS + hh
                rows = slice(h * HEAD_DIM, (h + 1) * HEAD_DIM)
                o = out[:, hh * TQ:(hh + 1) * TQ] * inv[hh] * z_ref[0, rows, cur].astype(_F32)
                o_ref[0, rows, cur] = o.astype(_BF16)


def _mix_a(qa_t, ka_n, va_t, za_t, bias_a, sink_a):
    B, _, S = qa_t.shape
    tqa = A_SUB * TQ
    prev = lambda i: jnp.maximum(i * A_SUB - 1, 0)
    in_specs = [
        pl.BlockSpec((1, A_WIDTH, tqa), lambda b, i: (b, 0, i)),
        pl.BlockSpec((1, TQ, A_KV_WIDTH), lambda b, i: (b, prev(i), 0)),
        pl.BlockSpec((1, tqa, A_KV_WIDTH), lambda b, i: (b, i, 0)),
        pl.BlockSpec((1, A_KV_WIDTH, TQ), lambda b, i: (b, 0, prev(i))),
        pl.BlockSpec((1, A_KV_WIDTH, tqa), lambda b, i: (b, 0, i)),
        pl.BlockSpec((1, A_WIDTH, tqa), lambda b, i: (b, 0, i)),
        pl.BlockSpec((A_Q_HEADS, 2 * TQ, TQ), lambda b, i: (0, 0, 0)),
        pl.BlockSpec((A_Q_HEADS, TQ), lambda b, i: (0, 0)),
    ]
    return pl.pallas_call(
        _mix_a_kernel, grid=(B, S // tqa), in_specs=in_specs,
        out_specs=pl.BlockSpec((1, A_WIDTH, tqa), lambda b, i: (b, 0, i)),
        out_shape=jax.ShapeDtypeStruct((B, A_WIDTH, S), _BF16),
        compiler_params=pltpu.CompilerParams(
            dimension_semantics=("parallel", "parallel"), vmem_limit_bytes=VMEM_LIMIT),
        name="mixer_a",
    )(qa_t, ka_n, ka_n, va_t, va_t, za_t, bias_a, sink_a)


def _colsum8(v):
    return jnp.sum(v.reshape(v.shape[0] // 8, 8, v.shape[1]), axis=0)


def _as01(pred):
    return jnp.where(pred, 1, 0).astype(jnp.int32)


def _ordered_bits_to_f32(u):
    key = u ^ jnp.int32(INT_MIN)
    return pltpu.bitcast(jnp.where(key < 0, key ^ jnp.int32(0x7FFFFFFF), key), _F32)


def _colmax8(v):
    return jnp.max(v.reshape(v.shape[0] // 8, 8, v.shape[1]), axis=0)


def _by_pairs(n, step):
    def body(t, carry):
        for u in range(4):
            step(4 * t + u)
        return carry

    lax.fori_loop(0, n // 4, body, 0)
    base = (n // 4) * 4

    @pl.when((n & 2) != 0)
    def _():
        step(base)
        step(base + 1)

    @pl.when((n & 1) != 0)
    def _():
        step(base + (n & 2))


def _mix_b_kernel(qb_ref, kb_ref, vb_ref, zb_ref, qi_ref, ki_ref, wi_ref, bias_ref, o_ref,
                  keys_scr, mb_scr, sc_scr, acc_scr, m_scr, thr_scr, tie_scr):
    i = pl.program_id(1)
    nc = ((i + 1) * TQB + KCHUNK - 1) // KCHUNK
    ns = (nc + 1) // 2
    top_k = TOPK_MAX

    qi = qi_ref[0]
    zpad = jnp.zeros((LANES - IDX_DIM, TQB), _BF16)
    rhs_i = jnp.concatenate(
        [jnp.concatenate([qi[h * IDX_DIM:(h + 1) * IDX_DIM], zpad], axis=0)
         for h in range(IDX_HEADS)], axis=1)
    w = wi_ref[0] * (IDX_DIM ** -0.5 * IDX_HEADS ** -0.5)
    s_minus_t = (lax.broadcasted_iota(jnp.int32, (KCHUNK, TQB), 0)
                 - lax.broadcasted_iota(jnp.int32, (KCHUNK, TQB), 1))

    def index_step(c):
        r0 = pl.multiple_of(c * KCHUNK, KCHUNK)
        d = jnp.dot(ki_ref[0, pl.ds(r0, KCHUNK), :], rhs_i,
                    preferred_element_type=_F32)
        acc = w[0:1, :] * jnp.maximum(d[:, 0:TQB], 0.0)
        for h in range(1, IDX_HEADS):
            acc = acc + w[h:h + 1, :] * jnp.maximum(d[:, h * TQB:(h + 1) * TQB], 0.0)
        causal = s_minus_t <= (i * TQB - r0)
        keys_scr[pl.ds(r0, KCHUNK), :] = jnp.where(causal, acc, -jnp.inf)

    _by_pairs(nc, index_step)

    @pl.when(nc % 2 == 1)
    def _():
        keys_scr[pl.ds(pl.multiple_of(nc * KCHUNK, KCHUNK), KCHUNK), :] = jnp.full(
            (KCHUNK, TQB), -jnp.inf, _F32)

    def count(pred_fn):
        def body(c, acc):
            r0 = pl.multiple_of(c * KSUPER, KSUPER)
            return acc + _colsum8(pred_fn(keys_scr[pl.ds(r0, KSUPER), :], r0))
        acc = lax.fori_loop(0, ns, body, jnp.zeros((8, TQB), jnp.int32))
        return jnp.sum(acc, axis=0, keepdims=True)

    def count_ge(cand):
        n_acc = 8

        def bump(accs, r0, n_rows):
            accs = list(accs)
            rows = keys_scr[pl.ds(r0, n_rows), :]
            for j in range(n_rows // 8):
                a = accs[j % n_acc]
                accs[j % n_acc] = jnp.where(rows[j * 8:(j + 1) * 8] >= cand, a + 1, a)
            return tuple(accs)

        accs = lax.fori_loop(
            0, nc // 2, lambda c, accs: bump(accs, pl.multiple_of(c * KSUPER, KSUPER), KSUPER),
            tuple(jnp.zeros((8, TQB), jnp.int32) for _ in range(n_acc)))
        accs = lax.cond(nc % 2 == 1,
                        lambda accs: bump(accs, pl.multiple_of((nc - 1) * KCHUNK, KCHUNK), KCHUNK),
                        lambda accs: accs, accs)
        acc = accs[0]
        for a in accs[1:]:
            acc = acc + a
        return jnp.sum(acc, axis=0, keepdims=True)

    @pl.when(i * TQB + TQB <= top_k)
    def _():
        thr_scr[...] = jnp.full((1, TQB), -jnp.inf, _F32)
        tie_scr[...] = jnp.full((1, TQB), -1, jnp.int32)

    @pl.when(i * TQB + TQB > top_k)
    def _():
        def bit_body(b, t_u):
            cand_u = t_u | lax.shift_left(jnp.int32(1), 31 - b)
            cand = _ordered_bits_to_f32(cand_u)
            cnt = count_ge(cand)
            return jnp.where(cnt >= top_k, cand_u, t_u)

        t_u = lax.fori_loop(0, 32, bit_body, jnp.zeros((1, TQB), jnp.int32))
        thr = _ordered_bits_to_f32(t_u)
        cnt_gt = count(lambda k, r0: _as01(k > thr))
        cnt_ge = count(lambda k, r0: _as01(k >= thr))
        need = top_k - cnt_gt
        thr_scr[...] = thr
        tie_scr[...] = jnp.full((1, TQB), 2 ** 30, jnp.int32)
        row = lax.broadcasted_iota(jnp.int32, (KSUPER, TQB), 0)

        @pl.when(jnp.max(cnt_ge) > top_k)
        def _():
            def tie_body(b, lo):
                cand = lo | lax.shift_left(jnp.int32(1), 10 - b)
                cnt = count(lambda k, r0: jnp.where(k == thr, _as01(row < cand - r0), 0))
                return jnp.where(cnt < need, cand, lo)
            tie_scr[...] = lax.fori_loop(0, 11, tie_body, jnp.zeros((1, TQB), jnp.int32))

    thr = thr_scr[...]
    tie = tie_scr[...]
    row = lax.broadcasted_iota(jnp.int32, (KSUPER, TQB), 0)

    def mask_body(c, carry):
        r0 = pl.multiple_of(c * KSUPER, KSUPER)
        k = keys_scr[pl.ds(r0, KSUPER), :]
        mb_scr[pl.ds(r0, KSUPER), :] = jnp.where(
            k == thr, jnp.where(row <= tie - r0, 0.0, NEG), jnp.where(k > thr, 0.0, NEG))
        return carry

    lax.fori_loop(0, ns, mask_body, 0)

    zeros = jnp.zeros((HEAD_DIM, TQB), _BF16)
    n_pairs = B_HEADS // 2
    rhs = []
    for p in range(n_pairs):
        q0 = qb_ref[0, (2 * p) * HEAD_DIM:(2 * p + 1) * HEAD_DIM, :]
        q1 = qb_ref[0, (2 * p + 1) * HEAD_DIM:(2 * p + 2) * HEAD_DIM, :]
        rhs.append(jnp.concatenate([jnp.concatenate([q0, zeros], axis=0),
                                    jnp.concatenate([zeros, q1], axis=0)], axis=1))

    def score_step(c):
        r0 = pl.multiple_of(c * KCHUNK, KCHUNK)
        for p in range(n_pairs):
            s = jnp.dot(kb_ref[0, pl.ds(r0, KCHUNK), p * LANES:(p + 1) * LANES], rhs[p],
                        preferred_element_type=_F32)
            for hh in range(2):
                h = 2 * p + hh
                parts = []
                for half in range(KCHUNK // TQ):
                    jb = c * (KCHUNK // TQ) + half
                    rs = pl.multiple_of(r0 + half * TQ, TQ)
                    subs = []
                    for sub in range(TQB // TQ):
                        kind = jnp.clip(jb - (i * (TQB // TQ) + sub) + 2, 0, 2)
                        lanes = slice(sub * TQ, (sub + 1) * TQ)
                        subs.append(s[half * TQ:(half + 1) * TQ, hh * TQB + sub * TQ:hh * TQB + (sub + 1) * TQ]
                                    + bias_ref[h, kind] + mb_scr[pl.ds(rs, TQ), lanes])
                    parts.append(jnp.concatenate(subs, axis=1))
                sh = jnp.concatenate(parts, axis=0)
                sc_scr[pl.ds(r0, KCHUNK), h * TQB:(h + 1) * TQB] = sh
                m_scr[h * 8:(h + 1) * 8, :] = jnp.maximum(m_scr[h * 8:(h + 1) * 8, :], _colmax8(sh))

    m_scr[...] = jnp.full(m_scr.shape, NEG, _F32)
    _by_pairs(nc, score_step)
    m_row = jnp.concatenate(
        [jnp.max(m_scr[h * 8:(h + 1) * 8, :], axis=0, keepdims=True) for h in range(B_HEADS)],
        axis=1)
    acc_scr[...] = jnp.zeros_like(acc_scr)

    def pv_step(c):
        r0 = pl.multiple_of(c * KCHUNK, KCHUNK)
        for p in range(n_pairs):
            cols = slice(2 * p * TQB, (2 * p + 2) * TQB)
            pr = jnp.exp2(sc_scr[pl.ds(r0, KCHUNK), cols] - m_row[:, cols]).astype(_BF16)
            acc_scr[p] += jnp.dot(vb_ref[0, c, p * PAIR_ROWS:(p + 1) * PAIR_ROWS, :], pr,
                                  preferred_element_type=_F32)

    _by_pairs(nc, pv_step)
    for p in range(n_pairs):
        inv = 1.0 / acc_scr[p, 2 * HEAD_DIM:2 * HEAD_DIM + 1, :]
        for hh in range(2):
            rows = slice((2 * p + hh) * HEAD_DIM, (2 * p + hh + 1) * HEAD_DIM)
            o = (acc_scr[p, hh * HEAD_DIM:(hh + 1) * HEAD_DIM, hh * TQB:(hh + 1) * TQB]
                 * inv[:, hh * TQB:(hh + 1) * TQB] * zb_ref[0, rows, :].astype(_F32))
            o_ref[0, rows, :] = o.astype(_BF16)


def _mix_b(qb_t, kb_n, vb_c, zb_t, qi_t, ki_n, wi_t, bias_b):
    B, _, S = qb_t.shape
    nq = S // TQB
    per_q = lambda rows: pl.BlockSpec((1, rows, TQB), lambda b, i: (b, 0, i))
    in_specs = [
        per_q(B_WIDTH),
        pl.BlockSpec((1, S, B_WIDTH), lambda b, i: (b, 0, 0)),
        pl.BlockSpec((1, S // KCHUNK, V_ROWS, KCHUNK), lambda b, i: (b, 0, 0, 0)),
        per_q(B_WIDTH),
        per_q(IDX_WIDTH),
        pl.BlockSpec((1, S, LANES), lambda b, i: (b, 0, 0)),
        per_q(IDX_HEADS),
        pl.BlockSpec((B_HEADS, 3, TQ, TQ), lambda b, i: (0, 0, 0, 0)),
    ]
    scratch = [
        pltpu.VMEM((S, TQB), _F32),
        pltpu.VMEM((S, TQB), _F32),
        pltpu.VMEM((S, B_HEADS * TQB), _F32),
        pltpu.VMEM((B_HEADS // 2, PAIR_ROWS, 2 * TQB), _F32),
        pltpu.VMEM((B_HEADS * 8, TQB), _F32),
        pltpu.VMEM((1, TQB), _F32),
        pltpu.VMEM((1, TQB), jnp.int32),
    ]
    return pl.pallas_call(
        _mix_b_kernel, grid=(B, nq), in_specs=in_specs,
        out_specs=per_q(B_WIDTH),
        out_shape=jax.ShapeDtypeStruct((B, B_WIDTH, S), _BF16),
        scratch_shapes=scratch,
        compiler_params=pltpu.CompilerParams(
            dimension_semantics=("parallel", "arbitrary"), vmem_limit_bytes=VMEM_LIMIT),
        name="mixer_b",
    )(qb_t, kb_n, vb_c, zb_t, qi_t, ki_n, wi_t, bias_b)


def _merge_kernel(x_ref, ya_ref, yb_ref, gt_ref, wpa_ref, wpb_ref, wo_ref, o_ref):
    pa = jnp.dot(wpa_ref[...], ya_ref[0], preferred_element_type=_F32)
    pb = jnp.dot(wpb_ref[...], yb_ref[0], preferred_element_type=_F32)
    merged = (gt_ref[0, :D_MODEL, :].astype(_F32) * pa
              + gt_ref[0, D_MODEL:, :].astype(_F32) * pb).astype(_BF16)
    out_t = jnp.dot(wo_ref[...], merged, preferred_element_type=_F32)
    o_ref[0] = x_ref[0] + out_t.T


def _merge(x, ya_t, yb_t, gt_t, wpa_t, wpb_t, wo_t):
    B, S, D = x.shape
    tm = TM_PROJ
    const = lambda b, t: (0, 0)
    in_specs = [
        pl.BlockSpec((1, tm, D), lambda b, t: (b, t, 0)),
        pl.BlockSpec((1, A_WIDTH, tm), lambda b, t: (b, 0, t)),
        pl.BlockSpec((1, B_WIDTH, tm), lambda b, t: (b, 0, t)),
        pl.BlockSpec((1, 2 * D, tm), lambda b, t: (b, 0, t)),
        pl.BlockSpec((D, A_WIDTH), const),
        pl.BlockSpec((D, B_WIDTH), const),
        pl.BlockSpec((D, D), const),
    ]
    return pl.pallas_call(
        _merge_kernel, grid=(B, S // tm), in_specs=in_specs,
        out_specs=pl.BlockSpec((1, tm, D), lambda b, t: (b, t, 0)),
        out_shape=jax.ShapeDtypeStruct((B, S, D), x.dtype),
        compiler_params=pltpu.CompilerParams(
            dimension_semantics=("parallel", "parallel"), vmem_limit_bytes=VMEM_LIMIT),
        name="merge_out",
    )(x, ya_t, yb_t, gt_t, wpa_t, wpb_t, wo_t)


def _t5_bucket(n):
    n = np.maximum(n, 0)
    max_exact = N_BUCKETS // 2
    nf = np.maximum(n, 1).astype(np.float64)
    large = max_exact + np.floor(np.log(nf / max_exact) / math.log(MAX_DISTANCE / max_exact)
                                 * (N_BUCKETS - max_exact)).astype(np.int64)
    large = np.minimum(large, N_BUCKETS - 1)
    return np.where(n < max_exact, n, large)


def _bias_tables(rel_bias):
    t = np.arange(TQ)[None, :]
    s = np.arange(TQ)[:, None]
    d_prev = t + TQ - s
    d_cur = t - s
    table_a = rel_bias[:, :A_Q_HEADS].astype(_F32).T
    table_b = rel_bias[:, A_Q_HEADS:].astype(_F32).T

    def tile(table, dist, windowed):
        onehot = (_t5_bucket(dist)[..., None] == np.arange(N_BUCKETS)).astype(np.float32)
        b = jnp.einsum("stk,hk->hst", onehot, table, precision=lax.Precision.HIGHEST)
        if windowed:
            b = jnp.where(((dist >= 0) & (dist < WINDOW))[None], b, NEG)
        return b

    bias_a = jnp.concatenate([tile(table_a, d_prev, True), tile(table_a, d_cur, True)], axis=1)
    far = jnp.broadcast_to(table_b[:, N_BUCKETS - 1][:, None, None], (B_HEADS, TQ, TQ))
    bias_b = jnp.stack([far, tile(table_b, d_prev, False), tile(table_b, d_cur, False)], axis=1) * LOG2E
    return bias_a, bias_b


def kernel(x, norm_g, w_in, qnorm_a, knorm_a, sinks_a, qnorm_b, knorm_b, rel_bias,
           w_proj_a, w_proj_b, w_out):
    assert norm_g.shape[0] == 1, "single-layer block"
    B, S, D = x.shape
    assert D == D_MODEL and S % TM_PROJ == 0 and S % KSUPER == 0 and S % TQB == 0 and TQB % TQ == 0
    assert MAX_DISTANCE <= TQ

    wt = w_in[0].T
    n_real = _SEG["kw"][0] + KW_REAL
    wt = jnp.concatenate(
        [wt[:n_real], jnp.zeros((_SEG["kw"][1] - n_real, D), wt.dtype), wt[n_real:]], axis=0)
    wt = wt.astype(_BF16)
    bcast = lambda g, scale: jnp.broadcast_to((g.astype(_F32) * scale)[:, None], (HEAD_DIM, TM_PROJ))
    q_scale = HEAD_DIM ** -0.5
    (qa_t, ka_n, va_t, za_t, qb_t, kb_n, vb_c, zb_t, qi_t, ki_n, wi_t, gt_t) = _inproj(
        x, norm_g[0][None, :].astype(_F32), wt,
        bcast(qnorm_a[0], q_scale), bcast(knorm_a[0], 1.0),
        bcast(qnorm_b[0], q_scale * LOG2E), bcast(knorm_b[0], 1.0))

    bias_a, bias_b = _bias_tables(rel_bias)
    sink_a = jnp.broadcast_to(sinks_a[0].astype(_F32)[:, None], (A_Q_HEADS, TQ))
    ya_t = _mix_a(qa_t, ka_n, va_t, za_t, bias_a, sink_a)
    yb_t = _mix_b(qb_t, kb_n, vb_c, zb_t, qi_t, ki_n, wi_t, bias_b)
    return _merge(x, ya_t, yb_t, gt_t,
                  w_proj_a[0].T.astype(_BF16), w_proj_b[0].T.astype(_BF16), w_out[0].T.astype(_BF16))
```

```python
import functools
import math

import jax
import jax.numpy as jnp
import numpy as np
from jax import lax
from jax.experimental import pallas as pl
from jax.experimental.pallas import tpu as pltpu

D_MODEL = 1024
HEAD_DIM = 64
A_Q_HEADS = 8
A_KV_HEADS = 2
A_GROUPS = A_Q_HEADS // A_KV_HEADS
A_WIDTH = A_Q_HEADS * HEAD_DIM
A_KV_WIDTH = A_KV_HEADS * HEAD_DIM
B_HEADS = 8
B_WIDTH = B_HEADS * HEAD_DIM
IDX_HEADS = 8
IDX_DIM = 32
IDX_WIDTH = IDX_HEADS * IDX_DIM
WINDOW = 128
TOPK_MAX = 256
N_BUCKETS = 32
MAX_DISTANCE = 128
RMS_EPS = 1e-6

LANES = 128
TQ = 128
A_SUB = 4
TQB = 256
KCHUNK = 256
KSUPER = 2 * KCHUNK
SEARCH_CHECKS = (25, 29)
TM_PROJ = 1024
NEG = -1e30
LOG2E = math.log2(math.e)
ONES_ROWS = 16
PAIR_ROWS = 2 * HEAD_DIM + ONES_ROWS
V_ROWS = (B_HEADS // 2) * PAIR_ROWS
INT_MIN = -(2 ** 31)
VMEM_LIMIT = 56 * 1024 * 1024

_SEG = {}
_off = 0
for _name, _rows in (("qa", A_WIDTH), ("ka", A_KV_WIDTH), ("va", A_KV_WIDTH), ("za", A_WIDTH),
                     ("qb", B_WIDTH), ("kb", B_WIDTH), ("vb", B_WIDTH), ("zb", B_WIDTH),
                     ("qi", IDX_WIDTH), ("kw", LANES), ("gates", 2 * D_MODEL)):
    _SEG[_name] = (_off, _off + _rows)
    _off += _rows
PROJ_ROWS = _off
KW_REAL = IDX_DIM + IDX_HEADS

_F32 = jnp.float32
_BF16 = jnp.bfloat16
_NT = (((1,), (1,)), ((), ()))


def _inproj_kernel(x_ref, g_ref, wt_ref, gqa_ref, gka_ref, gqb_ref, gkb_ref,
                   qa_o, ka_o, va_o, za_o, qb_o, kb_o, vb_o, zb_o, qi_o, ki_o, wi_o, gt_o):
    tm = x_ref.shape[1]
    x = x_ref[0]
    ms = jnp.mean(x * x, axis=-1, keepdims=True)
    h = (x * lax.rsqrt(ms + RMS_EPS) * g_ref[...]).astype(_BF16)

    def proj(lo, hi):
        return lax.dot_general(wt_ref[lo:hi, :], h, _NT, preferred_element_type=_F32)

    def seg(name):
        return proj(*_SEG[name])

    def headnorm(p, gain_ref):
        nh = p.shape[0] // HEAD_DIM
        p3 = p.reshape(nh, HEAD_DIM, tm)
        r = lax.rsqrt(jnp.mean(p3 * p3, axis=1, keepdims=True) + RMS_EPS)
        return (p3 * r * gain_ref[...][None]).reshape(nh * HEAD_DIM, tm)

    def silu(p):
        return p * jax.nn.sigmoid(p)

    qa_o[0] = headnorm(seg("qa"), gqa_ref).astype(_BF16)
    ka_o[0] = headnorm(seg("ka"), gka_ref).T.astype(_BF16)
    va_o[0] = seg("va").astype(_BF16)
    za_o[0] = silu(seg("za")).astype(_BF16)
    qb_o[0] = headnorm(seg("qb"), gqb_ref).astype(_BF16)
    kb_o[0] = headnorm(seg("kb"), gkb_ref).T.astype(_BF16)
    vb = seg("vb").astype(_BF16)
    ones = jnp.ones((ONES_ROWS, KCHUNK), _BF16)
    for c in range(tm // KCHUNK):
        for p in range(B_HEADS // 2):
            vb_o[0, c, p * PAIR_ROWS:(p + 1) * PAIR_ROWS, :] = jnp.concatenate(
                [vb[p * LANES:(p + 1) * LANES, c * KCHUNK:(c + 1) * KCHUNK], ones], axis=0)
    zb_o[0] = silu(seg("zb")).astype(_BF16)
    qi_o[0] = seg("qi").astype(_BF16)
    kw = seg("kw")
    wi_o[0] = kw[IDX_DIM:KW_REAL, :]
    row = lax.broadcasted_iota(jnp.int32, kw.shape, 0)
    ki_o[0] = jnp.where(row < IDX_DIM, kw, 0.0).T.astype(_BF16)
    g_lo = _SEG["gates"][0]
    for c in range(2 * D_MODEL // 512):
        gt_o[0, c * 512:(c + 1) * 512, :] = jax.nn.sigmoid(
            proj(g_lo + c * 512, g_lo + (c + 1) * 512)).astype(_BF16)


def _inproj(x, g, wt, gqa, gka, gqb, gkb):
    B, S, D = x.shape
    tm = TM_PROJ
    nt = S // tm
    const = lambda b, t: (0, 0)
    tr = lambda rows: pl.BlockSpec((1, rows, tm), lambda b, t: (b, 0, t))
    nat = lambda cols: pl.BlockSpec((1, tm, cols), lambda b, t: (b, t, 0))
    sds = jax.ShapeDtypeStruct
    out_shape = (
        sds((B, A_WIDTH, S), _BF16), sds((B, S, A_KV_WIDTH), _BF16), sds((B, A_KV_WIDTH, S), _BF16),
        sds((B, A_WIDTH, S), _BF16),
        sds((B, B_WIDTH, S), _BF16), sds((B, S, B_WIDTH), _BF16),
        sds((B, S // KCHUNK, V_ROWS, KCHUNK), _BF16), sds((B, B_WIDTH, S), _BF16),
        sds((B, IDX_WIDTH, S), _BF16), sds((B, S, LANES), _BF16), sds((B, IDX_HEADS, S), _F32),
        sds((B, 2 * D_MODEL, S), _BF16),
    )
    out_specs = (
        tr(A_WIDTH), nat(A_KV_WIDTH), tr(A_KV_WIDTH), tr(A_WIDTH),
        tr(B_WIDTH), nat(B_WIDTH),
        pl.BlockSpec((1, tm // KCHUNK, V_ROWS, KCHUNK), lambda b, t: (b, t, 0, 0)), tr(B_WIDTH),
        tr(IDX_WIDTH), nat(LANES), tr(IDX_HEADS), tr(2 * D_MODEL),
    )
    in_specs = [
        pl.BlockSpec((1, tm, D), lambda b, t: (b, t, 0)),
        pl.BlockSpec((1, D), const),
        pl.BlockSpec((PROJ_ROWS, D), const, pipeline_mode=pl.Buffered(1)),
        pl.BlockSpec((HEAD_DIM, tm), const), pl.BlockSpec((HEAD_DIM, tm), const),
        pl.BlockSpec((HEAD_DIM, tm), const), pl.BlockSpec((HEAD_DIM, tm), const),
    ]
    return pl.pallas_call(
        _inproj_kernel, grid=(B, nt), in_specs=in_specs, out_specs=out_specs, out_shape=out_shape,
        compiler_params=pltpu.CompilerParams(
            dimension_semantics=("parallel", "parallel"), vmem_limit_bytes=VMEM_LIMIT),
        name="inproj",
    )(x, g, wt, gqa, gka, gqb, gkb)


def _mix_a_kernel(q_ref, kp_ref, kc_ref, vp_ref, vc_ref, z_ref, bias_ref, sink_ref, o_ref):
    i = pl.program_id(1)
    zeros = jnp.zeros((HEAD_DIM, TQ), _BF16)
    for j in range(A_SUB):
        cur = slice(j * TQ, (j + 1) * TQ)
        old = slice((j - 1) * TQ, j * TQ)
        k_old = kp_ref[0] if j == 0 else kc_ref[0, old, :]
        v_old = vp_ref[0] if j == 0 else vc_ref[0, :, old]
        kband = jnp.concatenate([k_old, kc_ref[0, cur, :]], axis=0)
        vband = jnp.concatenate([v_old, vc_ref[0, :, cur]], axis=1)
        for g in range(A_KV_HEADS):
            cols = []
            for hh in range(A_GROUPS):
                h = g * A_GROUPS + hh
                parts = [zeros] * A_KV_HEADS
                parts[g] = q_ref[0, h * HEAD_DIM:(h + 1) * HEAD_DIM, cur]
                cols.append(jnp.concatenate(parts, axis=0))
            rhs = jnp.concatenate(cols, axis=1)
            sc = jnp.dot(kband, rhs, preferred_element_type=_F32)
            probs, inv = [], []
            for hh in range(A_GROUPS):
                h = g * A_GROUPS + hh
                s = sc[:, hh * TQ:(hh + 1) * TQ] + bias_ref[h]
                if j == 0:
                    s = jnp.concatenate([jnp.where(i > 0, s[:TQ], NEG), s[TQ:]], axis=0)
                sink = sink_ref[h:h + 1, :]
                m = jnp.maximum(jnp.max(s, axis=0, keepdims=True), sink)
                p = jnp.exp(s - m)
                denom = jnp.sum(p, axis=0, keepdims=True) + jnp.exp(sink - m)
                probs.append(p.astype(_BF16))
                inv.append(1.0 / denom)
            pt = jnp.concatenate(probs, axis=1)
            out = jnp.dot(vband[g * HEAD_DIM:(g + 1) * HEAD_DIM, :], pt,
                          preferred_element_type=_F32)
            for hh in range(A_GROUPS):
                h = g * A_GROUPS + hh
                rows = slice(h * HEAD_DIM, (h + 1) * HEAD_DIM)
                o = out[:, hh * TQ:(hh + 1) * TQ] * inv[hh] * z_ref[0, rows, cur].astype(_F32)
                o_ref[0, rows, cur] = o.astype(_BF16)


def _mix_a(qa_t, ka_n, va_t, za_t, bias_a, sink_a):
    B, _, S = qa_t.shape
    tqa = A_SUB * TQ
    prev = lambda i: jnp.maximum(i * A_SUB - 1, 0)
    in_specs = [
        pl.BlockSpec((1, A_WIDTH, tqa), lambda b, i: (b, 0, i)),
        pl.BlockSpec((1, TQ, A_KV_WIDTH), lambda b, i: (b, prev(i), 0)),
        pl.BlockSpec((1, tqa, A_KV_WIDTH), lambda b, i: (b, i, 0)),
        pl.BlockSpec((1, A_KV_WIDTH, TQ), lambda b, i: (b, 0, prev(i))),
        pl.BlockSpec((1, A_KV_WIDTH, tqa), lambda b, i: (b, 0, i)),
        pl.BlockSpec((1, A_WIDTH, tqa), lambda b, i: (b, 0, i)),
        pl.BlockSpec((A_Q_HEADS, 2 * TQ, TQ), lambda b, i: (0, 0, 0)),
        pl.BlockSpec((A_Q_HEADS, TQ), lambda b, i: (0, 0)),
    ]
    return pl.pallas_call(
        _mix_a_kernel, grid=(B, S // tqa), in_specs=in_specs,
        out_specs=pl.BlockSpec((1, A_WIDTH, tqa), lambda b, i: (b, 0, i)),
        out_shape=jax.ShapeDtypeStruct((B, A_WIDTH, S), _BF16),
        compiler_params=pltpu.CompilerParams(
            dimension_semantics=("parallel", "parallel"), vmem_limit_bytes=VMEM_LIMIT),
        name="mixer_a",
    )(qa_t, ka_n, ka_n, va_t, va_t, za_t, bias_a, sink_a)


def _colsum8(v):
    return jnp.sum(v.reshape(v.shape[0] // 8, 8, v.shape[1]), axis=0)


def _as01(pred):
    return jnp.where(pred, 1, 0).astype(jnp.int32)


def _ordered_bits_to_f32(u):
    key = u ^ jnp.int32(INT_MIN)
    return pltpu.bitcast(jnp.where(key < 0, key ^ jnp.int32(0x7FFFFFFF), key), _F32)


def _colmax8(v):
    return jnp.max(v.reshape(v.shape[0] // 8, 8, v.shape[1]), axis=0)


def _by_pairs(n, step):
    def body(t, carry):
        for u in range(4):
            step(4 * t + u)
        return carry

    lax.fori_loop(0, n // 4, body, 0)
    base = (n // 4) * 4

    @pl.when((n & 2) != 0)
    def _():
        step(base)
        step(base + 1)

    @pl.when((n & 1) != 0)
    def _():
        step(base + (n & 2))


def _mix_b_kernel(qb_ref, kb_ref, vb_ref, zb_ref, qi_ref, ki_ref, wi_ref, bias_ref, o_ref,
                  keys_scr, mb_scr, sc_scr, acc_scr, m_scr, thr_scr, tie_scr):
    i = pl.program_id(1)
    nc = ((i + 1) * TQB + KCHUNK - 1) // KCHUNK
    ns = (nc + 1) // 2
    top_k = TOPK_MAX

    qi = qi_ref[0]
    zpad = jnp.zeros((LANES - IDX_DIM, TQB), _BF16)
    rhs_i = jnp.concatenate(
        [jnp.concatenate([qi[h * IDX_DIM:(h + 1) * IDX_DIM], zpad], axis=0)
         for h in range(IDX_HEADS)], axis=1)
    w = wi_ref[0] * (IDX_DIM ** -0.5 * IDX_HEADS ** -0.5)
    s_minus_t = (lax.broadcasted_iota(jnp.int32, (KCHUNK, TQB), 0)
                 - lax.broadcasted_iota(jnp.int32, (KCHUNK, TQB), 1))

    def index_step(c):
        r0 = pl.multiple_of(c * KCHUNK, KCHUNK)
        d = jnp.dot(ki_ref[0, pl.ds(r0, KCHUNK), :], rhs_i,
                    preferred_element_type=_F32)
        acc = w[0:1, :] * jnp.maximum(d[:, 0:TQB], 0.0)
        for h in range(1, IDX_HEADS):
            acc = acc + w[h:h + 1, :] * jnp.maximum(d[:, h * TQB:(h + 1) * TQB], 0.0)
        causal = s_minus_t <= (i * TQB - r0)
        keys_scr[pl.ds(r0, KCHUNK), :] = jnp.where(causal, acc, -jnp.inf)

    _by_pairs(nc, index_step)

    @pl.when(nc % 2 == 1)
    def _():
        keys_scr[pl.ds(pl.multiple_of(nc * KCHUNK, KCHUNK), KCHUNK), :] = jnp.full(
            (KCHUNK, TQB), -jnp.inf, _F32)

    def count(pred_fn):
        def body(c, acc):
            r0 = pl.multiple_of(c * KSUPER, KSUPER)
            return acc + _colsum8(pred_fn(keys_scr[pl.ds(r0, KSUPER), :], r0))
        acc = lax.fori_loop(0, ns, body, jnp.zeros((8, TQB), jnp.int32))
        return jnp.sum(acc, axis=0, keepdims=True)

    def count_ge(cand):
        n_acc = 8

        def bump(accs, r0, n_rows):
            accs = list(accs)
            rows = keys_scr[pl.ds(r0, n_rows), :]
            for j in range(n_rows // 8):
                a = accs[j % n_acc]
                accs[j % n_acc] = jnp.where(rows[j * 8:(j + 1) * 8] >= cand, a + 1, a)
            return tuple(accs)

        accs = lax.fori_loop(
            0, nc // 2, lambda c, accs: bump(accs, pl.multiple_of(c * KSUPER, KSUPER), KSUPER),
            tuple(jnp.zeros((8, TQB), jnp.int32) for _ in range(n_acc)))
        accs = lax.cond(nc % 2 == 1,
                        lambda accs: bump(accs, pl.multiple_of((nc - 1) * KCHUNK, KCHUNK), KCHUNK),
                        lambda accs: accs, accs)
        acc = accs[0]
        for a in accs[1:]:
            acc = acc + a
        return jnp.sum(acc, axis=0, keepdims=True)

    @pl.when(i * TQB + TQB <= top_k)
    def _():
        thr_scr[...] = jnp.full((1, TQB), -jnp.inf, _F32)
        tie_scr[...] = jnp.full((1, TQB), -1, jnp.int32)

    @pl.when(i * TQB + TQB > top_k)
    def _():
        def bit_body(b, state):
            t_u, cnt_t = state
            cand_u = t_u | lax.shift_left(jnp.int32(1), 31 - b)
            cnt = count_ge(_ordered_bits_to_f32(cand_u))
            ok = cnt >= top_k
            return jnp.where(ok, cand_u, t_u), jnp.where(ok, cnt, cnt_t)

        def settled(cnt_t):
            return jnp.min(jnp.where(cnt_t == top_k, 1, 0)) == 1

        state = (jnp.zeros((1, TQB), jnp.int32), jnp.full((1, TQB), top_k + 1, jnp.int32))
        stops = (0,) + SEARCH_CHECKS + (32,)
        state = lax.fori_loop(stops[0], stops[1], bit_body, state)
        for lo, hi in zip(stops[1:-1], stops[2:]):
            state = lax.cond(settled(state[1]), lambda s: s,
                             functools.partial(lax.fori_loop, lo, hi, bit_body), state)
        t_u, cnt_ge = state
        thr = _ordered_bits_to_f32(t_u)
        thr_scr[...] = thr
        tie_scr[...] = jnp.full((1, TQB), 2 ** 30, jnp.int32)
        row = lax.broadcasted_iota(jnp.int32, (KSUPER, TQB), 0)

        @pl.when(jnp.max(cnt_ge) > top_k)
        def _():
            need = top_k - count(lambda k, r0: _as01(k > thr))

            def tie_body(b, lo):
                cand = lo | lax.shift_left(jnp.int32(1), 10 - b)
                cnt = count(lambda k, r0: jnp.where(k == thr, _as01(row < cand - r0), 0))
                return jnp.where(cnt < need, cand, lo)
            tie_scr[...] = lax.fori_loop(0, 11, tie_body, jnp.zeros((1, TQB), jnp.int32))

    thr = thr_scr[...]
    tie = tie_scr[...]
    row = lax.broadcasted_iota(jnp.int32, (KSUPER, TQB), 0)

    def mask_body(c, carry):
        r0 = pl.multiple_of(c * KSUPER, KSUPER)
        k = keys_scr[pl.ds(r0, KSUPER), :]
        mb_scr[pl.ds(r0, KSUPER), :] = jnp.where(
            k == thr, jnp.where(row <= tie - r0, 0.0, NEG), jnp.where(k > thr, 0.0, NEG))
        return carry

    lax.fori_loop(0, ns, mask_body, 0)

    zeros = jnp.zeros((HEAD_DIM, TQB), _BF16)
    n_pairs = B_HEADS // 2
    rhs = []
    for p in range(n_pairs):
        q0 = qb_ref[0, (2 * p) * HEAD_DIM:(2 * p + 1) * HEAD_DIM, :]
        q1 = qb_ref[0, (2 * p + 1) * HEAD_DIM:(2 * p + 2) * HEAD_DIM, :]
        rhs.append(jnp.concatenate([jnp.concatenate([q0, zeros], axis=0),
                                    jnp.concatenate([zeros, q1], axis=0)], axis=1))

    def score_step(c):
        r0 = pl.multiple_of(c * KCHUNK, KCHUNK)
        for p in range(n_pairs):
            s = jnp.dot(kb_ref[0, pl.ds(r0, KCHUNK), p * LANES:(p + 1) * LANES], rhs[p],
                        preferred_element_type=_F32)
            for hh in range(2):
                h = 2 * p + hh
                parts = []
                for half in range(KCHUNK // TQ):
                    jb = c * (KCHUNK // TQ) + half
                    rs = pl.multiple_of(r0 + half * TQ, TQ)
                    subs = []
                    for sub in range(TQB // TQ):
                        kind = jnp.clip(jb - (i * (TQB // TQ) + sub) + 2, 0, 2)
                        lanes = slice(sub * TQ, (sub + 1) * TQ)
                        subs.append(s[half * TQ:(half + 1) * TQ, hh * TQB + sub * TQ:hh * TQB + (sub + 1) * TQ]
                                    + bias_ref[h, kind] + mb_scr[pl.ds(rs, TQ), lanes])
                    parts.append(jnp.concatenate(subs, axis=1))
                sh = jnp.concatenate(parts, axis=0)
                sc_scr[pl.ds(r0, KCHUNK), h * TQB:(h + 1) * TQB] = sh
                m_scr[h * 8:(h + 1) * 8, :] = jnp.maximum(m_scr[h * 8:(h + 1) * 8, :], _colmax8(sh))

    m_scr[...] = jnp.full(m_scr.shape, NEG, _F32)
    _by_pairs(nc, score_step)
    m_row = jnp.concatenate(
        [jnp.max(m_scr[h * 8:(h + 1) * 8, :], axis=0, keepdims=True) for h in range(B_HEADS)],
        axis=1)
    acc_scr[...] = jnp.zeros_like(acc_scr)

    def pv_step(c):
        r0 = pl.multiple_of(c * KCHUNK, KCHUNK)
        for p in range(n_pairs):
            cols = slice(2 * p * TQB, (2 * p + 2) * TQB)
            pr = jnp.exp2(sc_scr[pl.ds(r0, KCHUNK), cols] - m_row[:, cols]).astype(_BF16)
            acc_scr[p] += jnp.dot(vb_ref[0, c, p * PAIR_ROWS:(p + 1) * PAIR_ROWS, :], pr,
                                  preferred_element_type=_F32)

    _by_pairs(nc, pv_step)
    for p in range(n_pairs):
        inv = 1.0 / acc_scr[p, 2 * HEAD_DIM:2 * HEAD_DIM + 1, :]
        for hh in range(2):
            rows = slice((2 * p + hh) * HEAD_DIM, (2 * p + hh + 1) * HEAD_DIM)
            o = (acc_scr[p, hh * HEAD_DIM:(hh + 1) * HEAD_DIM, hh * TQB:(hh + 1) * TQB]
                 * inv[:, hh * TQB:(hh + 1) * TQB] * zb_ref[0, rows, :].astype(_F32))
            o_ref[0, rows, :] = o.astype(_BF16)


def _mix_b(qb_t, kb_n, vb_c, zb_t, qi_t, ki_n, wi_t, bias_b):
    B, _, S = qb_t.shape
    nq = S // TQB
    per_q = lambda rows: pl.BlockSpec((1, rows, TQB), lambda b, i: (b, 0, i))
    in_specs = [
        per_q(B_WIDTH),
        pl.BlockSpec((1, S, B_WIDTH), lambda b, i: (b, 0, 0)),
        pl.BlockSpec((1, S // KCHUNK, V_ROWS, KCHUNK), lambda b, i: (b, 0, 0, 0)),
        per_q(B_WIDTH),
        per_q(IDX_WIDTH),
        pl.BlockSpec((1, S, LANES), lambda b, i: (b, 0, 0)),
        per_q(IDX_HEADS),
        pl.BlockSpec((B_HEADS, 3, TQ, TQ), lambda b, i: (0, 0, 0, 0)),
    ]
    scratch = [
        pltpu.VMEM((S, TQB), _F32),
        pltpu.VMEM((S, TQB), _F32),
        pltpu.VMEM((S, B_HEADS * TQB), _F32),
        pltpu.VMEM((B_HEADS // 2, PAIR_ROWS, 2 * TQB), _F32),
        pltpu.VMEM((B_HEADS * 8, TQB), _F32),
        pltpu.VMEM((1, TQB), _F32),
        pltpu.VMEM((1, TQB), jnp.int32),
    ]
    return pl.pallas_call(
        _mix_b_kernel, grid=(B, nq), in_specs=in_specs,
        out_specs=per_q(B_WIDTH),
        out_shape=jax.ShapeDtypeStruct((B, B_WIDTH, S), _BF16),
        scratch_shapes=scratch,
        compiler_params=pltpu.CompilerParams(
            dimension_semantics=("parallel", "arbitrary"), vmem_limit_bytes=VMEM_LIMIT),
        name="mixer_b",
    )(qb_t, kb_n, vb_c, zb_t, qi_t, ki_n, wi_t, bias_b)


def _merge_kernel(x_ref, ya_ref, yb_ref, gt_ref, wpa_ref, wpb_ref, wo_ref, o_ref):
    pa = jnp.dot(wpa_ref[...], ya_ref[0], preferred_element_type=_F32)
    pb = jnp.dot(wpb_ref[...], yb_ref[0], preferred_element_type=_F32)
    merged = (gt_ref[0, :D_MODEL, :].astype(_F32) * pa
              + gt_ref[0, D_MODEL:, :].astype(_F32) * pb).astype(_BF16)
    out_t = jnp.dot(wo_ref[...], merged, preferred_element_type=_F32)
    o_ref[0] = x_ref[0] + out_t.T


def _merge(x, ya_t, yb_t, gt_t, wpa_t, wpb_t, wo_t):
    B, S, D = x.shape
    tm = TM_PROJ
    const = lambda b, t: (0, 0)
    in_specs = [
        pl.BlockSpec((1, tm, D), lambda b, t: (b, t, 0)),
        pl.BlockSpec((1, A_WIDTH, tm), lambda b, t: (b, 0, t)),
        pl.BlockSpec((1, B_WIDTH, tm), lambda b, t: (b, 0, t)),
        pl.BlockSpec((1, 2 * D, tm), lambda b, t: (b, 0, t)),
        pl.BlockSpec((D, A_WIDTH), const),
        pl.BlockSpec((D, B_WIDTH), const),
        pl.BlockSpec((D, D), const),
    ]
    return pl.pallas_call(
        _merge_kernel, grid=(B, S // tm), in_specs=in_specs,
        out_specs=pl.BlockSpec((1, tm, D), lambda b, t: (b, t, 0)),
        out_shape=jax.ShapeDtypeStruct((B, S, D), x.dtype),
        compiler_params=pltpu.CompilerParams(
            dimension_semantics=("parallel", "parallel"), vmem_limit_bytes=VMEM_LIMIT),
        name="merge_out",
    )(x, ya_t, yb_t, gt_t, wpa_t, wpb_t, wo_t)


def _t5_bucket(n):
    n = np.maximum(n, 0)
    max_exact = N_BUCKETS // 2
    nf = np.maximum(n, 1).astype(np.float64)
    large = max_exact + np.floor(np.log(nf / max_exact) / math.log(MAX_DISTANCE / max_exact)
                                 * (N_BUCKETS - max_exact)).astype(np.int64)
    large = np.minimum(large, N_BUCKETS - 1)
    return np.where(n < max_exact, n, large)


def _bias_tables(rel_bias):
    t = np.arange(TQ)[None, :]
    s = np.arange(TQ)[:, None]
    d_prev = t + TQ - s
    d_cur = t - s
    table_a = rel_bias[:, :A_Q_HEADS].astype(_F32).T
    table_b = rel_bias[:, A_Q_HEADS:].astype(_F32).T

    def tile(table, dist, windowed):
        onehot = (_t5_bucket(dist)[..., None] == np.arange(N_BUCKETS)).astype(np.float32)
        b = jnp.einsum("stk,hk->hst", onehot, table, precision=lax.Precision.HIGHEST)
        if windowed:
            b = jnp.where(((dist >= 0) & (dist < WINDOW))[None], b, NEG)
        return b

    bias_a = jnp.concatenate([tile(table_a, d_prev, True), tile(table_a, d_cur, True)], axis=1)
    far = jnp.broadcast_to(table_b[:, N_BUCKETS - 1][:, None, None], (B_HEADS, TQ, TQ))
    bias_b = jnp.stack([far, tile(table_b, d_prev, False), tile(table_b, d_cur, False)], axis=1) * LOG2E
    return bias_a, bias_b


def kernel(x, norm_g, w_in, qnorm_a, knorm_a, sinks_a, qnorm_b, knorm_b, rel_bias,
           w_proj_a, w_proj_b, w_out):
    assert norm_g.shape[0] == 1, "single-layer block"
    B, S, D = x.shape
    assert D == D_MODEL and S % TM_PROJ == 0 and S % KSUPER == 0 and S % TQB == 0 and TQB % TQ == 0
    assert MAX_DISTANCE <= TQ

    wt = w_in[0].T
    n_real = _SEG["kw"][0] + KW_REAL
    wt = jnp.concatenate(
        [wt[:n_real], jnp.zeros((_SEG["kw"][1] - n_real, D), wt.dtype), wt[n_real:]], axis=0)
    wt = wt.astype(_BF16)
    bcast = lambda g, scale: jnp.broadcast_to((g.astype(_F32) * scale)[:, None], (HEAD_DIM, TM_PROJ))
    q_scale = HEAD_DIM ** -0.5
    (qa_t, ka_n, va_t, za_t, qb_t, kb_n, vb_c, zb_t, qi_t, ki_n, wi_t, gt_t) = _inproj(
        x, norm_g[0][None, :].astype(_F32), wt,
        bcast(qnorm_a[0], q_scale), bcast(knorm_a[0], 1.0),
        bcast(qnorm_b[0], q_scale * LOG2E), bcast(knorm_b[0], 1.0))

    bias_a, bias_b = _bias_tables(rel_bias)
    sink_a = jnp.broadcast_to(sinks_a[0].astype(_F32)[:, None], (A_Q_HEADS, TQ))
    ya_t = _mix_a(qa_t, ka_n, va_t, za_t, bias_a, sink_a)
    yb_t = _mix_b(qb_t, kb_n, vb_c, zb_t, qi_t, ki_n, wi_t, bias_b)
    return _merge(x, ya_t, yb_t, gt_t,
                  w_proj_a[0].T.astype(_BF16), w_proj_b[0].T.astype(_BF16), w_out[0].T.astype(_BF16))
```

```python
import functools
import math

import jax
import jax.numpy as jnp
import numpy as np
from jax import lax
from jax.experimental import pallas as pl
from jax.experimental.pallas import tpu as pltpu

D_MODEL = 1024
HEAD_DIM = 64
A_Q_HEADS = 8
A_KV_HEADS = 2
A_GROUPS = A_Q_HEADS // A_KV_HEADS
A_WIDTH = A_Q_HEADS * HEAD_DIM
A_KV_WIDTH = A_KV_HEADS * HEAD_DIM
B_HEADS = 8
B_WIDTH = B_HEADS * HEAD_DIM
IDX_HEADS = 8
IDX_DIM = 32
IDX_WIDTH = IDX_HEADS * IDX_DIM
WINDOW = 128
TOPK_MAX = 256
N_BUCKETS = 32
MAX_DISTANCE = 128
RMS_EPS = 1e-6

LANES = 128
TQ = 128
A_SUB = 4
TQB = 256
KCHUNK = 256
KSUPER = 2 * KCHUNK
SEARCH_CHECKS = (25, 29)
TM_PROJ = 1024
NEG = -1e30
LOG2E = math.log2(math.e)
ONES_ROWS = 16
PAIR_ROWS = 2 * HEAD_DIM + ONES_ROWS
V_ROWS = (B_HEADS // 2) * PAIR_ROWS
INT_MIN = -(2 ** 31)
VMEM_LIMIT = 56 * 1024 * 1024

_SEG = {}
_off = 0
for _name, _rows in (("qa", A_WIDTH), ("ka", A_KV_WIDTH), ("va", A_KV_WIDTH), ("za", A_WIDTH),
                     ("qb", B_WIDTH), ("kb", B_WIDTH), ("vb", B_WIDTH), ("zb", B_WIDTH),
                     ("qi", IDX_WIDTH), ("kw", LANES), ("gates", 2 * D_MODEL)):
    _SEG[_name] = (_off, _off + _rows)
    _off += _rows
PROJ_ROWS = _off
KW_REAL = IDX_DIM + IDX_HEADS

_F32 = jnp.float32
_BF16 = jnp.bfloat16
_NT = (((1,), (1,)), ((), ()))


def _inproj_kernel(x_ref, g_ref, wt_ref, gqa_ref, gka_ref, gqb_ref, gkb_ref,
                   qa_o, ka_o, va_o, za_o, qb_o, kb_o, vb_o, zb_o, qi_o, ki_o, wi_o, gt_o):
    tm = x_ref.shape[1]
    x = x_ref[0]
    ms = jnp.mean(x * x, axis=-1, keepdims=True)
    h = (x * lax.rsqrt(ms + RMS_EPS) * g_ref[...]).astype(_BF16)

    def proj(lo, hi):
        return lax.dot_general(wt_ref[lo:hi, :], h, _NT, preferred_element_type=_F32)

    def seg(name):
        return proj(*_SEG[name])

    def headnorm(p, gain_ref):
        nh = p.shape[0] // HEAD_DIM
        p3 = p.reshape(nh, HEAD_DIM, tm)
        r = lax.rsqrt(jnp.mean(p3 * p3, axis=1, keepdims=True) + RMS_EPS)
        return (p3 * r * gain_ref[...][None]).reshape(nh * HEAD_DIM, tm)

    def silu(p):
        return p * jax.nn.sigmoid(p)

    qa_o[0] = headnorm(seg("qa"), gqa_ref).astype(_BF16)
    ka_o[0] = headnorm(seg("ka"), gka_ref).T.astype(_BF16)
    va_o[0] = seg("va").astype(_BF16)
    za_o[0] = silu(seg("za")).astype(_BF16)
    qb_o[0] = headnorm(seg("qb"), gqb_ref).astype(_BF16)
    kb_o[0] = headnorm(seg("kb"), gkb_ref).T.astype(_BF16)
    vb = seg("vb").astype(_BF16)
    ones = jnp.ones((ONES_ROWS, KCHUNK), _BF16)
    for c in range(tm // KCHUNK):
        for p in range(B_HEADS // 2):
            vb_o[0, c, p * PAIR_ROWS:(p + 1) * PAIR_ROWS, :] = jnp.concatenate(
                [vb[p * LANES:(p + 1) * LANES, c * KCHUNK:(c + 1) * KCHUNK], ones], axis=0)
    zb_o[0] = silu(seg("zb")).astype(_BF16)
    qi_o[0] = seg("qi").astype(_BF16)
    kw = seg("kw")
    wi_o[0] = kw[IDX_DIM:KW_REAL, :]
    row = lax.broadcasted_iota(jnp.int32, kw.shape, 0)
    ki_o[0] = jnp.where(row < IDX_DIM, kw, 0.0).T.astype(_BF16)
    g_lo = _SEG["gates"][0]
    for c in range(2 * D_MODEL // 512):
        gt_o[0, c * 512:(c + 1) * 512, :] = jax.nn.sigmoid(
            proj(g_lo + c * 512, g_lo + (c + 1) * 512)).astype(_BF16)


def _inproj(x, g, wt, gqa, gka, gqb, gkb):
    B, S, D = x.shape
    tm = TM_PROJ
    nt = S // tm
    const = lambda b, t: (0, 0)
    tr = lambda rows: pl.BlockSpec((1, rows, tm), lambda b, t: (b, 0, t))
    nat = lambda cols: pl.BlockSpec((1, tm, cols), lambda b, t: (b, t, 0))
    sds = jax.ShapeDtypeStruct
    out_shape = (
        sds((B, A_WIDTH, S), _BF16), sds((B, S, A_KV_WIDTH), _BF16), sds((B, A_KV_WIDTH, S), _BF16),
        sds((B, A_WIDTH, S), _BF16),
        sds((B, B_WIDTH, S), _BF16), sds((B, S, B_WIDTH), _BF16),
        sds((B, S // KCHUNK, V_ROWS, KCHUNK), _BF16), sds((B, B_WIDTH, S), _BF16),
        sds((B, IDX_WIDTH, S), _BF16), sds((B, S, LANES), _BF16), sds((B, IDX_HEADS, S), _F32),
        sds((B, 2 * D_MODEL, S), _BF16),
    )
    out_specs = (
        tr(A_WIDTH), nat(A_KV_WIDTH), tr(A_KV_WIDTH), tr(A_WIDTH),
        tr(B_WIDTH), nat(B_WIDTH),
        pl.BlockSpec((1, tm // KCHUNK, V_ROWS, KCHUNK), lambda b, t: (b, t, 0, 0)), tr(B_WIDTH),
        tr(IDX_WIDTH), nat(LANES), tr(IDX_HEADS), tr(2 * D_MODEL),
    )
    in_specs = [
        pl.BlockSpec((1, tm, D), lambda b, t: (b, t, 0)),
        pl.BlockSpec((1, D), const),
        pl.BlockSpec((PROJ_ROWS, D), const, pipeline_mode=pl.Buffered(1)),
        pl.BlockSpec((HEAD_DIM, tm), const), pl.BlockSpec((HEAD_DIM, tm), const),
        pl.BlockSpec((HEAD_DIM, tm), const), pl.BlockSpec((HEAD_DIM, tm), const),
    ]
    return pl.pallas_call(
        _inproj_kernel, grid=(B, nt), in_specs=in_specs, out_specs=out_specs, out_shape=out_shape,
        compiler_params=pltpu.CompilerParams(
            dimension_semantics=("parallel", "parallel"), vmem_limit_bytes=VMEM_LIMIT),
        name="inproj",
    )(x, g, wt, gqa, gka, gqb, gkb)


def _mix_a_kernel(q_ref, kp_ref, kc_ref, vp_ref, vc_ref, z_ref, bias_ref, sink_ref, o_ref):
    i = pl.program_id(1)
    zeros = jnp.zeros((HEAD_DIM, TQ), _BF16)
    units = [(j, g) for j in range(A_SUB) for g in range(A_KV_HEADS)]

    def bands(j):
        cur = slice(j * TQ, (j + 1) * TQ)
        old = slice((j - 1) * TQ, j * TQ)
        k_old = kp_ref[0] if j == 0 else kc_ref[0, old, :]
        v_old = vp_ref[0] if j == 0 else vc_ref[0, :, old]
        return (jnp.concatenate([k_old, kc_ref[0, cur, :]], axis=0),
                jnp.concatenate([v_old, vc_ref[0, :, cur]], axis=1))

    scores = []
    for j, g in units:
        cur = slice(j * TQ, (j + 1) * TQ)
        cols = []
        for hh in range(A_GROUPS):
            h = g * A_GROUPS + hh
            parts = [zeros] * A_KV_HEADS
            parts[g] = q_ref[0, h * HEAD_DIM:(h + 1) * HEAD_DIM, cur]
            cols.append(jnp.concatenate(parts, axis=0))
        rhs = jnp.concatenate(cols, axis=1)
        scores.append(jnp.dot(bands(j)[0], rhs, preferred_element_type=_F32))

    pts, invs = [], []
    for (j, g), sc in zip(units, scores):
        probs, inv = [], []
        for hh in range(A_GROUPS):
            h = g * A_GROUPS + hh
            s = sc[:, hh * TQ:(hh + 1) * TQ] + bias_ref[h]
            if j == 0:
                s = jnp.concatenate([jnp.where(i > 0, s[:TQ], NEG), s[TQ:]], axis=0)
            sink = sink_ref[h:h + 1, :]
            m = jnp.maximum(jnp.max(s, axis=0, keepdims=True), sink)
            p = jnp.exp(s - m)
            denom = jnp.sum(p, axis=0, keepdims=True) + jnp.exp(sink - m)
            probs.append(p.astype(_BF16))
            inv.append(1.0 / denom)
        pts.append(jnp.concatenate(probs, axis=1))
        invs.append(inv)

    for (j, g), pt, inv in zip(units, pts, invs):
        cur = slice(j * TQ, (j + 1) * TQ)
        out = jnp.dot(bands(j)[1][g * HEAD_DIM:(g + 1) * HEAD_DIM, :], pt,
                      preferred_element_type=_F32)
        for hh in range(A_GROUPS):
            h = g * A_GROUPS + hh
            rows = slice(h * HEAD_DIM, (h + 1) * HEAD_DIM)
            o = out[:, hh * TQ:(hh + 1) * TQ] * inv[hh] * z_ref[0, rows, cur].astype(_F32)
            o_ref[0, rows, cur] = o.astype(_BF16)


def _mix_a(qa_t, ka_n, va_t, za_t, bias_a, sink_a):
    B, _, S = qa_t.shape
    tqa = A_SUB * TQ
    prev = lambda i: jnp.maximum(i * A_SUB - 1, 0)
    in_specs = [
        pl.BlockSpec((1, A_WIDTH, tqa), lambda b, i: (b, 0, i)),
        pl.BlockSpec((1, TQ, A_KV_WIDTH), lambda b, i: (b, prev(i), 0)),
        pl.BlockSpec((1, tqa, A_KV_WIDTH), lambda b, i: (b, i, 0)),
        pl.BlockSpec((1, A_KV_WIDTH, TQ), lambda b, i: (b, 0, prev(i))),
        pl.BlockSpec((1, A_KV_WIDTH, tqa), lambda b, i: (b, 0, i)),
        pl.BlockSpec((1, A_WIDTH, tqa), lambda b, i: (b, 0, i)),
        pl.BlockSpec((A_Q_HEADS, 2 * TQ, TQ), lambda b, i: (0, 0, 0)),
        pl.BlockSpec((A_Q_HEADS, TQ), lambda b, i: (0, 0)),
    ]
    return pl.pallas_call(
        _mix_a_kernel, grid=(B, S // tqa), in_specs=in_specs,
        out_specs=pl.BlockSpec((1, A_WIDTH, tqa), lambda b, i: (b, 0, i)),
        out_shape=jax.ShapeDtypeStruct((B, A_WIDTH, S), _BF16),
        compiler_params=pltpu.CompilerParams(
            dimension_semantics=("parallel", "parallel"), vmem_limit_bytes=VMEM_LIMIT),
        name="mixer_a",
    )(qa_t, ka_n, ka_n, va_t, va_t, za_t, bias_a, sink_a)


def _colsum8(v):
    return jnp.sum(v.reshape(v.shape[0] // 8, 8, v.shape[1]), axis=0)


def _as01(pred):
    return jnp.where(pred, 1, 0).astype(jnp.int32)


def _ordered_bits_to_f32(u):
    key = u ^ jnp.int32(INT_MIN)
    return pltpu.bitcast(jnp.where(key < 0, key ^ jnp.int32(0x7FFFFFFF), key), _F32)


def _colmax8(v):
    return jnp.max(v.reshape(v.shape[0] // 8, 8, v.shape[1]), axis=0)


def _by_pairs(n, step):
    def body(t, carry):
        for u in range(4):
            step(4 * t + u)
        return carry

    lax.fori_loop(0, n // 4, body, 0)
    base = (n // 4) * 4

    @pl.when((n & 2) != 0)
    def _():
        step(base)
        step(base + 1)

    @pl.when((n & 1) != 0)
    def _():
        step(base + (n & 2))


def _mix_b_kernel(qb_ref, kb_ref, vb_ref, zb_ref, qi_ref, ki_ref, wi_ref, bias_ref, o_ref,
                  keys_scr, mb_scr, sc_scr, acc_scr, m_scr, thr_scr, tie_scr):
    i = pl.program_id(1)
    nc = ((i + 1) * TQB + KCHUNK - 1) // KCHUNK
    ns = (nc + 1) // 2
    top_k = TOPK_MAX

    qi = qi_ref[0]
    zpad = jnp.zeros((LANES - IDX_DIM, TQB), _BF16)
    rhs_i = jnp.concatenate(
        [jnp.concatenate([qi[h * IDX_DIM:(h + 1) * IDX_DIM], zpad], axis=0)
         for h in range(IDX_HEADS)], axis=1)
    w = wi_ref[0] * (IDX_DIM ** -0.5 * IDX_HEADS ** -0.5)
    s_minus_t = (lax.broadcasted_iota(jnp.int32, (KCHUNK, TQB), 0)
                 - lax.broadcasted_iota(jnp.int32, (KCHUNK, TQB), 1))

    def index_step(c):
        r0 = pl.multiple_of(c * KCHUNK, KCHUNK)
        d = jnp.dot(ki_ref[0, pl.ds(r0, KCHUNK), :], rhs_i,
                    preferred_element_type=_F32)
        acc = w[0:1, :] * jnp.maximum(d[:, 0:TQB], 0.0)
        for h in range(1, IDX_HEADS):
            acc = acc + w[h:h + 1, :] * jnp.maximum(d[:, h * TQB:(h + 1) * TQB], 0.0)
        causal = s_minus_t <= (i * TQB - r0)
        keys_scr[pl.ds(r0, KCHUNK), :] = jnp.where(causal, acc, -jnp.inf)

    _by_pairs(nc, index_step)

    @pl.when(nc % 2 == 1)
    def _():
        keys_scr[pl.ds(pl.multiple_of(nc * KCHUNK, KCHUNK), KCHUNK), :] = jnp.full(
            (KCHUNK, TQB), -jnp.inf, _F32)

    def count(pred_fn):
        def body(c, acc):
            r0 = pl.multiple_of(c * KSUPER, KSUPER)
            return acc + _colsum8(pred_fn(keys_scr[pl.ds(r0, KSUPER), :], r0))
        acc = lax.fori_loop(0, ns, body, jnp.zeros((8, TQB), jnp.int32))
        return jnp.sum(acc, axis=0, keepdims=True)

    def count_ge(cand):
        n_acc = 8

        def bump(accs, r0, n_rows):
            accs = list(accs)
            rows = keys_scr[pl.ds(r0, n_rows), :]
            for j in range(n_rows // 8):
                a = accs[j % n_acc]
                accs[j % n_acc] = jnp.where(rows[j * 8:(j + 1) * 8] >= cand, a + 1, a)
            return tuple(accs)

        accs = lax.fori_loop(
            0, nc // 2, lambda c, accs: bump(accs, pl.multiple_of(c * KSUPER, KSUPER), KSUPER),
            tuple(jnp.zeros((8, TQB), jnp.int32) for _ in range(n_acc)))
        accs = lax.cond(nc % 2 == 1,
                        lambda accs: bump(accs, pl.multiple_of((nc - 1) * KCHUNK, KCHUNK), KCHUNK),
                        lambda accs: accs, accs)
        acc = accs[0]
        for a in accs[1:]:
            acc = acc + a
        return jnp.sum(acc, axis=0, keepdims=True)

    @pl.when(i * TQB + TQB <= top_k)
    def _():
        thr_scr[...] = jnp.full((1, TQB), -jnp.inf, _F32)
        tie_scr[...] = jnp.full((1, TQB), -1, jnp.int32)

    @pl.when(i * TQB + TQB > top_k)
    def _():
        def bit_body(b, state):
            t_u, cnt_t = state
            cand_u = t_u | lax.shift_left(jnp.int32(1), 31 - b)
            cnt = count_ge(_ordered_bits_to_f32(cand_u))
            ok = cnt >= top_k
            return jnp.where(ok, cand_u, t_u), jnp.where(ok, cnt, cnt_t)

        def settled(cnt_t):
            return jnp.min(jnp.where(cnt_t == top_k, 1, 0)) == 1

        state = (jnp.zeros((1, TQB), jnp.int32), jnp.full((1, TQB), top_k + 1, jnp.int32))
        stops = (0,) + SEARCH_CHECKS + (32,)
        state = lax.fori_loop(stops[0], stops[1], bit_body, state)
        for lo, hi in zip(stops[1:-1], stops[2:]):
            state = lax.cond(settled(state[1]), lambda s: s,
                             functools.partial(lax.fori_loop, lo, hi, bit_body), state)
        t_u, cnt_ge = state
        thr = _ordered_bits_to_f32(t_u)
        thr_scr[...] = thr
        tie_scr[...] = jnp.full((1, TQB), 2 ** 30, jnp.int32)
        row = lax.broadcasted_iota(jnp.int32, (KSUPER, TQB), 0)

        @pl.when(jnp.max(cnt_ge) > top_k)
        def _():
            need = top_k - count(lambda k, r0: _as01(k > thr))

            def tie_body(b, lo):
                cand = lo | lax.shift_left(jnp.int32(1), 10 - b)
                cnt = count(lambda k, r0: jnp.where(k == thr, _as01(row < cand - r0), 0))
                return jnp.where(cnt < need, cand, lo)
            tie_scr[...] = lax.fori_loop(0, 11, tie_body, jnp.zeros((1, TQB), jnp.int32))

    thr = thr_scr[...]
    tie = tie_scr[...]
    row = lax.broadcasted_iota(jnp.int32, (KSUPER, TQB), 0)

    def mask_body(c, carry):
        r0 = pl.multiple_of(c * KSUPER, KSUPER)
        k = keys_scr[pl.ds(r0, KSUPER), :]
        mb_scr[pl.ds(r0, KSUPER), :] = jnp.where(
            k == thr, jnp.where(row <= tie - r0, 0.0, NEG), jnp.where(k > thr, 0.0, NEG))
        return carry

    lax.fori_loop(0, ns, mask_body, 0)

    zeros = jnp.zeros((HEAD_DIM, TQB), _BF16)
    n_pairs = B_HEADS // 2
    rhs = []
    for p in range(n_pairs):
        q0 = qb_ref[0, (2 * p) * HEAD_DIM:(2 * p + 1) * HEAD_DIM, :]
        q1 = qb_ref[0, (2 * p + 1) * HEAD_DIM:(2 * p + 2) * HEAD_DIM, :]
        rhs.append(jnp.concatenate([jnp.concatenate([q0, zeros], axis=0),
                                    jnp.concatenate([zeros, q1], axis=0)], axis=1))

    def score_step(c):
        r0 = pl.multiple_of(c * KCHUNK, KCHUNK)
        for p in range(n_pairs):
            s = jnp.dot(kb_ref[0, pl.ds(r0, KCHUNK), p * LANES:(p + 1) * LANES], rhs[p],
                        preferred_element_type=_F32)
            for hh in range(2):
                h = 2 * p + hh
                parts = []
                for half in range(KCHUNK // TQ):
                    jb = c * (KCHUNK // TQ) + half
                    rs = pl.multiple_of(r0 + half * TQ, TQ)
                    subs = []
                    for sub in range(TQB // TQ):
                        kind = jnp.clip(jb - (i * (TQB // TQ) + sub) + 2, 0, 2)
                        lanes = slice(sub * TQ, (sub + 1) * TQ)
                        subs.append(s[half * TQ:(half + 1) * TQ, hh * TQB + sub * TQ:hh * TQB + (sub + 1) * TQ]
                                    + bias_ref[h, kind] + mb_scr[pl.ds(rs, TQ), lanes])
                    parts.append(jnp.concatenate(subs, axis=1))
                sh = jnp.concatenate(parts, axis=0)
                sc_scr[pl.ds(r0, KCHUNK), h * TQB:(h + 1) * TQB] = sh
                m_scr[h * 8:(h + 1) * 8, :] = jnp.maximum(m_scr[h * 8:(h + 1) * 8, :], _colmax8(sh))

    m_scr[...] = jnp.full(m_scr.shape, NEG, _F32)
    _by_pairs(nc, score_step)
    m_row = jnp.concatenate(
        [jnp.max(m_scr[h * 8:(h + 1) * 8, :], axis=0, keepdims=True) for h in range(B_HEADS)],
        axis=1)
    acc_scr[...] = jnp.zeros_like(acc_scr)

    def pv_step(c):
        r0 = pl.multiple_of(c * KCHUNK, KCHUNK)
        for p in range(n_pairs):
            cols = slice(2 * p * TQB, (2 * p + 2) * TQB)
            pr = jnp.exp2(sc_scr[pl.ds(r0, KCHUNK), cols] - m_row[:, cols]).astype(_BF16)
            acc_scr[p] += jnp.dot(vb_ref[0, c, p * PAIR_ROWS:(p + 1) * PAIR_ROWS, :], pr,
                                  preferred_element_type=_F32)

    _by_pairs(nc, pv_step)
    for p in range(n_pairs):
        inv = 1.0 / acc_scr[p, 2 * HEAD_DIM:2 * HEAD_DIM + 1, :]
        for hh in range(2):
            rows = slice((2 * p + hh) * HEAD_DIM, (2 * p + hh + 1) * HEAD_DIM)
            o = (acc_scr[p, hh * HEAD_DIM:(hh + 1) * HEAD_DIM, hh * TQB:(hh + 1) * TQB]
                 * inv[:, hh * TQB:(hh + 1) * TQB] * zb_ref[0, rows, :].astype(_F32))
            o_ref[0, rows, :] = o.astype(_BF16)


def _mix_b(qb_t, kb_n, vb_c, zb_t, qi_t, ki_n, wi_t, bias_b):
    B, _, S = qb_t.shape
    nq = S // TQB
    per_q = lambda rows: pl.BlockSpec((1, rows, TQB), lambda b, i: (b, 0, i))
    in_specs = [
        per_q(B_WIDTH),
        pl.BlockSpec((1, S, B_WIDTH), lambda b, i: (b, 0, 0)),
        pl.BlockSpec((1, S // KCHUNK, V_ROWS, KCHUNK), lambda b, i: (b, 0, 0, 0)),
        per_q(B_WIDTH),
        per_q(IDX_WIDTH),
        pl.BlockSpec((1, S, LANES), lambda b, i: (b, 0, 0)),
        per_q(IDX_HEADS),
        pl.BlockSpec((B_HEADS, 3, TQ, TQ), lambda b, i: (0, 0, 0, 0)),
    ]
    scratch = [
        pltpu.VMEM((S, TQB), _F32),
        pltpu.VMEM((S, TQB), _F32),
        pltpu.VMEM((S, B_HEADS * TQB), _F32),
        pltpu.VMEM((B_HEADS // 2, PAIR_ROWS, 2 * TQB), _F32),
        pltpu.VMEM((B_HEADS * 8, TQB), _F32),
        pltpu.VMEM((1, TQB), _F32),
        pltpu.VMEM((1, TQB), jnp.int32),
    ]
    return pl.pallas_call(
        _mix_b_kernel, grid=(B, nq), in_specs=in_specs,
        out_specs=per_q(B_WIDTH),
        out_shape=jax.ShapeDtypeStruct((B, B_WIDTH, S), _BF16),
        scratch_shapes=scratch,
        compiler_params=pltpu.CompilerParams(
            dimension_semantics=("parallel", "arbitrary"), vmem_limit_bytes=VMEM_LIMIT),
        name="mixer_b",
    )(qb_t, kb_n, vb_c, zb_t, qi_t, ki_n, wi_t, bias_b)


def _merge_kernel(x_ref, ya_ref, yb_ref, gt_ref, wpa_ref, wpb_ref, wo_ref, o_ref):
    pa = jnp.dot(wpa_ref[...], ya_ref[0], preferred_element_type=_F32)
    pb = jnp.dot(wpb_ref[...], yb_ref[0], preferred_element_type=_F32)
    merged = (gt_ref[0, :D_MODEL, :].astype(_F32) * pa
              + gt_ref[0, D_MODEL:, :].astype(_F32) * pb).astype(_BF16)
    out_t = jnp.dot(wo_ref[...], merged, preferred_element_type=_F32)
    o_ref[0] = x_ref[0] + out_t.T


def _merge(x, ya_t, yb_t, gt_t, wpa_t, wpb_t, wo_t):
    B, S, D = x.shape
    tm = TM_PROJ
    const = lambda b, t: (0, 0)
    in_specs = [
        pl.BlockSpec((1, tm, D), lambda b, t: (b, t, 0)),
        pl.BlockSpec((1, A_WIDTH, tm), lambda b, t: (b, 0, t)),
        pl.BlockSpec((1, B_WIDTH, tm), lambda b, t: (b, 0, t)),
        pl.BlockSpec((1, 2 * D, tm), lambda b, t: (b, 0, t)),
        pl.BlockSpec((D, A_WIDTH), const),
        pl.BlockSpec((D, B_WIDTH), const),
        pl.BlockSpec((D, D), const),
    ]
    return pl.pallas_call(
        _merge_kernel, grid=(B, S // tm), in_specs=in_specs,
        out_specs=pl.BlockSpec((1, tm, D), lambda b, t: (b, t, 0)),
        out_shape=jax.ShapeDtypeStruct((B, S, D), x.dtype),
        compiler_params=pltpu.CompilerParams(
            dimension_semantics=("parallel", "parallel"), vmem_limit_bytes=VMEM_LIMIT),
        name="merge_out",
    )(x, ya_t, yb_t, gt_t, wpa_t, wpb_t, wo_t)


def _t5_bucket(n):
    n = np.maximum(n, 0)
    max_exact = N_BUCKETS // 2
    nf = np.maximum(n, 1).astype(np.float64)
    large = max_exact + np.floor(np.log(nf / max_exact) / math.log(MAX_DISTANCE / max_exact)
                                 * (N_BUCKETS - max_exact)).astype(np.int64)
    large = np.minimum(large, N_BUCKETS - 1)
    return np.where(n < max_exact, n, large)


def _bias_tables(rel_bias):
    t = np.arange(TQ)[None, :]
    s = np.arange(TQ)[:, None]
    d_prev = t + TQ - s
    d_cur = t - s
    table_a = rel_bias[:, :A_Q_HEADS].astype(_F32).T
    table_b = rel_bias[:, A_Q_HEADS:].astype(_F32).T

    def tile(table, dist, windowed):
        onehot = (_t5_bucket(dist)[..., None] == np.arange(N_BUCKETS)).astype(np.float32)
        b = jnp.einsum("stk,hk->hst", onehot, table, precision=lax.Precision.HIGHEST)
        if windowed:
            b = jnp.where(((dist >= 0) & (dist < WINDOW))[None], b, NEG)
        return b

    bias_a = jnp.concatenate([tile(table_a, d_prev, True), tile(table_a, d_cur, True)], axis=1)
    far = jnp.broadcast_to(table_b[:, N_BUCKETS - 1][:, None, None], (B_HEADS, TQ, TQ))
    bias_b = jnp.stack([far, tile(table_b, d_prev, False), tile(table_b, d_cur, False)], axis=1) * LOG2E
    return bias_a, bias_b


def kernel(x, norm_g, w_in, qnorm_a, knorm_a, sinks_a, qnorm_b, knorm_b, rel_bias,
           w_proj_a, w_proj_b, w_out):
    assert norm_g.shape[0] == 1, "single-layer block"
    B, S, D = x.shape
    assert D == D_MODEL and S % TM_PROJ == 0 and S % KSUPER == 0 and S % TQB == 0 and TQB % TQ == 0
    assert MAX_DISTANCE <= TQ

    wt = w_in[0].T
    n_real = _SEG["kw"][0] + KW_REAL
    wt = jnp.concatenate(
        [wt[:n_real], jnp.zeros((_SEG["kw"][1] - n_real, D), wt.dtype), wt[n_real:]], axis=0)
    wt = wt.astype(_BF16)
    bcast = lambda g, scale: jnp.broadcast_to((g.astype(_F32) * scale)[:, None], (HEAD_DIM, TM_PROJ))
    q_scale = HEAD_DIM ** -0.5
    (qa_t, ka_n, va_t, za_t, qb_t, kb_n, vb_c, zb_t, qi_t, ki_n, wi_t, gt_t) = _inproj(
        x, norm_g[0][None, :].astype(_F32), wt,
        bcast(qnorm_a[0], q_scale), bcast(knorm_a[0], 1.0),
        bcast(qnorm_b[0], q_scale * LOG2E), bcast(knorm_b[0], 1.0))

    bias_a, bias_b = _bias_tables(rel_bias)
    sink_a = jnp.broadcast_to(sinks_a[0].astype(_F32)[:, None], (A_Q_HEADS, TQ))
    ya_t = _mix_a(qa_t, ka_n, va_t, za_t, bias_a, sink_a)
    yb_t = _mix_b(qb_t, kb_n, vb_c, zb_t, qi_t, ki_n, wi_t, bias_b)
    return _merge(x, ya_t, yb_t, gt_t,
                  w_proj_a[0].T.astype(_BF16), w_proj_b[0].T.astype(_BF16), w_out[0].T.astype(_BF16))
```

```python
import functools
import math

import jax
import jax.numpy as jnp
import numpy as np
from jax import lax
from jax.experimental import pallas as pl
from jax.experimental.pallas import tpu as pltpu

D_MODEL = 1024
HEAD_DIM = 64
A_Q_HEADS = 8
A_KV_HEADS = 2
A_GROUPS = A_Q_HEADS // A_KV_HEADS
A_WIDTH = A_Q_HEADS * HEAD_DIM
A_KV_WIDTH = A_KV_HEADS * HEAD_DIM
B_HEADS = 8
B_WIDTH = B_HEADS * HEAD_DIM
IDX_HEADS = 8
IDX_DIM = 32
IDX_WIDTH = IDX_HEADS * IDX_DIM
WINDOW = 128
TOPK_MAX = 256
N_BUCKETS = 32
MAX_DISTANCE = 128
RMS_EPS = 1e-6

LANES = 128
TQ = 128
A_SUB = 4
TQB = 256
KCHUNK = 256
KSUPER = 2 * KCHUNK
VALUE_PASSES = 6
SEARCH_CHECKS = (10, 13, 16, 20)
TM_PROJ = 1024
NEG = -1e30
LOG2E = math.log2(math.e)
ONES_ROWS = 16
PAIR_ROWS = 2 * HEAD_DIM + ONES_ROWS
V_ROWS = (B_HEADS // 2) * PAIR_ROWS
INT_MIN = -(2 ** 31)
VMEM_LIMIT = 56 * 1024 * 1024

_SEG = {}
_off = 0
for _name, _rows in (("qa", A_WIDTH), ("ka", A_KV_WIDTH), ("va", A_KV_WIDTH), ("za", A_WIDTH),
                     ("qb", B_WIDTH), ("kb", B_WIDTH), ("vb", B_WIDTH), ("zb", B_WIDTH),
                     ("qi", IDX_WIDTH), ("kw", LANES), ("gates", 2 * D_MODEL)):
    _SEG[_name] = (_off, _off + _rows)
    _off += _rows
PROJ_ROWS = _off
KW_REAL = IDX_DIM + IDX_HEADS

_F32 = jnp.float32
_BF16 = jnp.bfloat16
_NT = (((1,), (1,)), ((), ()))


def _inproj_kernel(x_ref, g_ref, wt_ref, gqa_ref, gka_ref, gqb_ref, gkb_ref,
                   qa_o, ka_o, va_o, za_o, qb_o, kb_o, vb_o, zb_o, qi_o, ki_o, wi_o, gt_o):
    tm = x_ref.shape[1]
    x = x_ref[0]
    ms = jnp.mean(x * x, axis=-1, keepdims=True)
    h = (x * lax.rsqrt(ms + RMS_EPS) * g_ref[...]).astype(_BF16)

    def proj(lo, hi):
        return lax.dot_general(wt_ref[lo:hi, :], h, _NT, preferred_element_type=_F32)

    def seg(name):
        return proj(*_SEG[name])

    def headnorm(p, gain_ref):
        nh = p.shape[0] // HEAD_DIM
        p3 = p.reshape(nh, HEAD_DIM, tm)
        r = lax.rsqrt(jnp.mean(p3 * p3, axis=1, keepdims=True) + RMS_EPS)
        return (p3 * r * gain_ref[...][None]).reshape(nh * HEAD_DIM, tm)

    def silu(p):
        return p * jax.nn.sigmoid(p)

    qa_o[0] = headnorm(seg("qa"), gqa_ref).astype(_BF16)
    ka_o[0] = headnorm(seg("ka"), gka_ref).T.astype(_BF16)
    va_o[0] = seg("va").astype(_BF16)
    za_o[0] = silu(seg("za")).astype(_BF16)
    qb_o[0] = headnorm(seg("qb"), gqb_ref).astype(_BF16)
    kb_o[0] = headnorm(seg("kb"), gkb_ref).T.astype(_BF16)
    vb = seg("vb").astype(_BF16)
    ones = jnp.ones((ONES_ROWS, KCHUNK), _BF16)
    for c in range(tm // KCHUNK):
        for p in range(B_HEADS // 2):
            vb_o[0, c, p * PAIR_ROWS:(p + 1) * PAIR_ROWS, :] = jnp.concatenate(
                [vb[p * LANES:(p + 1) * LANES, c * KCHUNK:(c + 1) * KCHUNK], ones], axis=0)
    zb_o[0] = silu(seg("zb")).astype(_BF16)
    qi_o[0] = seg("qi").astype(_BF16)
    kw = seg("kw")
    wi_o[0] = kw[IDX_DIM:KW_REAL, :]
    row = lax.broadcasted_iota(jnp.int32, kw.shape, 0)
    ki_o[0] = jnp.where(row < IDX_DIM, kw, 0.0).T.astype(_BF16)
    g_lo = _SEG["gates"][0]
    for c in range(2 * D_MODEL // 512):
        gt_o[0, c * 512:(c + 1) * 512, :] = jax.nn.sigmoid(
            proj(g_lo + c * 512, g_lo + (c + 1) * 512)).astype(_BF16)


def _inproj(x, g, wt, gqa, gka, gqb, gkb):
    B, S, D = x.shape
    tm = TM_PROJ
    nt = S // tm
    const = lambda b, t: (0, 0)
    tr = lambda rows: pl.BlockSpec((1, rows, tm), lambda b, t: (b, 0, t))
    nat = lambda cols: pl.BlockSpec((1, tm, cols), lambda b, t: (b, t, 0))
    sds = jax.ShapeDtypeStruct
    out_shape = (
        sds((B, A_WIDTH, S), _BF16), sds((B, S, A_KV_WIDTH), _BF16), sds((B, A_KV_WIDTH, S), _BF16),
        sds((B, A_WIDTH, S), _BF16),
        sds((B, B_WIDTH, S), _BF16), sds((B, S, B_WIDTH), _BF16),
        sds((B, S // KCHUNK, V_ROWS, KCHUNK), _BF16), sds((B, B_WIDTH, S), _BF16),
        sds((B, IDX_WIDTH, S), _BF16), sds((B, S, LANES), _BF16), sds((B, IDX_HEADS, S), _F32),
        sds((B, 2 * D_MODEL, S), _BF16),
    )
    out_specs = (
        tr(A_WIDTH), nat(A_KV_WIDTH), tr(A_KV_WIDTH), tr(A_WIDTH),
        tr(B_WIDTH), nat(B_WIDTH),
        pl.BlockSpec((1, tm // KCHUNK, V_ROWS, KCHUNK), lambda b, t: (b, t, 0, 0)), tr(B_WIDTH),
        tr(IDX_WIDTH), nat(LANES), tr(IDX_HEADS), tr(2 * D_MODEL),
    )
    in_specs = [
        pl.BlockSpec((1, tm, D), lambda b, t: (b, t, 0)),
        pl.BlockSpec((1, D), const),
        pl.BlockSpec((PROJ_ROWS, D), const, pipeline_mode=pl.Buffered(1)),
        pl.BlockSpec((HEAD_DIM, tm), const), pl.BlockSpec((HEAD_DIM, tm), const),
        pl.BlockSpec((HEAD_DIM, tm), const), pl.BlockSpec((HEAD_DIM, tm), const),
    ]
    return pl.pallas_call(
        _inproj_kernel, grid=(B, nt), in_specs=in_specs, out_specs=out_specs, out_shape=out_shape,
        compiler_params=pltpu.CompilerParams(
            dimension_semantics=("parallel", "parallel"), vmem_limit_bytes=VMEM_LIMIT),
        name="inproj",
    )(x, g, wt, gqa, gka, gqb, gkb)


def _mix_a_kernel(q_ref, kp_ref, kc_ref, vp_ref, vc_ref, z_ref, bias_ref, sink_ref, o_ref):
    i = pl.program_id(1)
    zeros = jnp.zeros((HEAD_DIM, TQ), _BF16)
    units = [(j, g) for j in range(A_SUB) for g in range(A_KV_HEADS)]

    def bands(j):
        cur = slice(j * TQ, (j + 1) * TQ)
        old = slice((j - 1) * TQ, j * TQ)
        k_old = kp_ref[0] if j == 0 else kc_ref[0, old, :]
        v_old = vp_ref[0] if j == 0 else vc_ref[0, :, old]
        return (jnp.concatenate([k_old, kc_ref[0, cur, :]], axis=0),
                jnp.concatenate([v_old, vc_ref[0, :, cur]], axis=1))

    scores = []
    for j, g in units:
        cur = slice(j * TQ, (j + 1) * TQ)
        cols = []
        for hh in range(A_GROUPS):
            h = g * A_GROUPS + hh
            parts = [zeros] * A_KV_HEADS
            parts[g] = q_ref[0, h * HEAD_DIM:(h + 1) * HEAD_DIM, cur]
            cols.append(jnp.concatenate(parts, axis=0))
        rhs = jnp.concatenate(cols, axis=1)
        scores.append(jnp.dot(bands(j)[0], rhs, preferred_element_type=_F32))

    pts, invs = [], []
    for (j, g), sc in zip(units, scores):
        probs, inv = [], []
        for hh in range(A_GROUPS):
            h = g * A_GROUPS + hh
            s = sc[:, hh * TQ:(hh + 1) * TQ] + bias_ref[h]
            if j == 0:
                s = jnp.concatenate([jnp.where(i > 0, s[:TQ], NEG), s[TQ:]], axis=0)
            sink = sink_ref[h:h + 1, :]
            m = jnp.maximum(jnp.max(s, axis=0, keepdims=True), sink)
            p = jnp.exp(s - m)
            denom = jnp.sum(p, axis=0, keepdims=True) + jnp.exp(sink - m)
            probs.append(p.astype(_BF16))
            inv.append(1.0 / denom)
        pts.append(jnp.concatenate(probs, axis=1))
        invs.append(inv)

    for (j, g), pt, inv in zip(units, pts, invs):
        cur = slice(j * TQ, (j + 1) * TQ)
        out = jnp.dot(bands(j)[1][g * HEAD_DIM:(g + 1) * HEAD_DIM, :], pt,
                      preferred_element_type=_F32)
        for hh in range(A_GROUPS):
            h = g * A_GROUPS + hh
            rows = slice(h * HEAD_DIM, (h + 1) * HEAD_DIM)
            o = out[:, hh * TQ:(hh + 1) * TQ] * inv[hh] * z_ref[0, rows, cur].astype(_F32)
            o_ref[0, rows, cur] = o.astype(_BF16)


def _mix_a(qa_t, ka_n, va_t, za_t, bias_a, sink_a):
    B, _, S = qa_t.shape
    tqa = A_SUB * TQ
    prev = lambda i: jnp.maximum(i * A_SUB - 1, 0)
    in_specs = [
        pl.BlockSpec((1, A_WIDTH, tqa), lambda b, i: (b, 0, i)),
        pl.BlockSpec((1, TQ, A_KV_WIDTH), lambda b, i: (b, prev(i), 0)),
        pl.BlockSpec((1, tqa, A_KV_WIDTH), lambda b, i: (b, i, 0)),
        pl.BlockSpec((1, A_KV_WIDTH, TQ), lambda b, i: (b, 0, prev(i))),
        pl.BlockSpec((1, A_KV_WIDTH, tqa), lambda b, i: (b, 0, i)),
        pl.BlockSpec((1, A_WIDTH, tqa), lambda b, i: (b, 0, i)),
        pl.BlockSpec((A_Q_HEADS, 2 * TQ, TQ), lambda b, i: (0, 0, 0)),
        pl.BlockSpec((A_Q_HEADS, TQ), lambda b, i: (0, 0)),
    ]
    return pl.pallas_call(
        _mix_a_kernel, grid=(B, S // tqa), in_specs=in_specs,
        out_specs=pl.BlockSpec((1, A_WIDTH, tqa), lambda b, i: (b, 0, i)),
        out_shape=jax.ShapeDtypeStruct((B, A_WIDTH, S), _BF16),
        compiler_params=pltpu.CompilerParams(
            dimension_semantics=("parallel", "parallel"), vmem_limit_bytes=VMEM_LIMIT),
        name="mixer_a",
    )(qa_t, ka_n, ka_n, va_t, va_t, za_t, bias_a, sink_a)


def _colsum8(v):
    return jnp.sum(v.reshape(v.shape[0] // 8, 8, v.shape[1]), axis=0)


def _as01(pred):
    return jnp.where(pred, 1, 0).astype(jnp.int32)


def _key_to_f32(key):
    return pltpu.bitcast(jnp.where(key < 0, key ^ jnp.int32(0x7FFFFFFF), key), _F32)


def _f32_to_key(v):
    bits = pltpu.bitcast(v, jnp.int32)
    return jnp.where(bits < 0, bits ^ jnp.int32(0x7FFFFFFF), bits)


def _colmin8(v):
    return jnp.min(v.reshape(v.shape[0] // 8, 8, v.shape[1]), axis=0)


def _colmax8(v):
    return jnp.max(v.reshape(v.shape[0] // 8, 8, v.shape[1]), axis=0)


def _by_pairs(n, step):
    def body(t, carry):
        for u in range(4):
            step(4 * t + u)
        return carry

    lax.fori_loop(0, n // 4, body, 0)
    base = (n // 4) * 4

    @pl.when((n & 2) != 0)
    def _():
        step(base)
        step(base + 1)

    @pl.when((n & 1) != 0)
    def _():
        step(base + (n & 2))


def _mix_b_kernel(qb_ref, kb_ref, vb_ref, zb_ref, qi_ref, ki_ref, wi_ref, bias_ref, o_ref,
                  keys_scr, mb_scr, sc_scr, acc_scr, m_scr, ext_scr, thr_scr, tie_scr):
    i = pl.program_id(1)
    nc = ((i + 1) * TQB + KCHUNK - 1) // KCHUNK
    ns = (nc + 1) // 2
    top_k = TOPK_MAX

    qi = qi_ref[0]
    zpad = jnp.zeros((LANES - IDX_DIM, TQB), _BF16)
    rhs_i = jnp.concatenate(
        [jnp.concatenate([qi[h * IDX_DIM:(h + 1) * IDX_DIM], zpad], axis=0)
         for h in range(IDX_HEADS)], axis=1)
    w = wi_ref[0] * (IDX_DIM ** -0.5 * IDX_HEADS ** -0.5)
    s_minus_t = (lax.broadcasted_iota(jnp.int32, (KCHUNK, TQB), 0)
                 - lax.broadcasted_iota(jnp.int32, (KCHUNK, TQB), 1))

    def index_step(c):
        r0 = pl.multiple_of(c * KCHUNK, KCHUNK)
        d = jnp.dot(ki_ref[0, pl.ds(r0, KCHUNK), :], rhs_i,
                    preferred_element_type=_F32)
        acc = w[0:1, :] * jnp.maximum(d[:, 0:TQB], 0.0)
        for h in range(1, IDX_HEADS):
            acc = acc + w[h:h + 1, :] * jnp.maximum(d[:, h * TQB:(h + 1) * TQB], 0.0)
        causal = s_minus_t <= (i * TQB - r0)
        score = jnp.where(causal, acc, -jnp.inf)
        keys_scr[pl.ds(r0, KCHUNK), :] = score
        ext_scr[0:8, :] = jnp.maximum(ext_scr[0:8, :], _colmax8(score))
        ext_scr[8:16, :] = jnp.minimum(ext_scr[8:16, :], _colmin8(jnp.where(causal, acc, jnp.inf)))

    ext_scr[0:8, :] = jnp.full((8, TQB), -jnp.inf, _F32)
    ext_scr[8:16, :] = jnp.full((8, TQB), jnp.inf, _F32)
    _by_pairs(nc, index_step)

    @pl.when(nc % 2 == 1)
    def _():
        keys_scr[pl.ds(pl.multiple_of(nc * KCHUNK, KCHUNK), KCHUNK), :] = jnp.full(
            (KCHUNK, TQB), -jnp.inf, _F32)

    def count(pred_fn):
        def body(c, acc):
            r0 = pl.multiple_of(c * KSUPER, KSUPER)
            return acc + _colsum8(pred_fn(keys_scr[pl.ds(r0, KSUPER), :], r0))
        acc = lax.fori_loop(0, ns, body, jnp.zeros((8, TQB), jnp.int32))
        return jnp.sum(acc, axis=0, keepdims=True)

    def count_ge(cand, strict=False):
        n_acc = 8
        hit = (lambda r: r > cand) if strict else (lambda r: r >= cand)

        def bump(accs, r0, n_rows):
            accs = list(accs)
            rows = keys_scr[pl.ds(r0, n_rows), :]
            for j in range(n_rows // 8):
                a = accs[j % n_acc]
                accs[j % n_acc] = jnp.where(hit(rows[j * 8:(j + 1) * 8]), a + 1, a)
            return tuple(accs)

        accs = lax.fori_loop(
            0, nc // 2, lambda c, accs: bump(accs, pl.multiple_of(c * KSUPER, KSUPER), KSUPER),
            tuple(jnp.zeros((8, TQB), jnp.int32) for _ in range(n_acc)))
        accs = lax.cond(nc % 2 == 1,
                        lambda accs: bump(accs, pl.multiple_of((nc - 1) * KCHUNK, KCHUNK), KCHUNK),
                        lambda accs: accs, accs)
        acc = accs[0]
        for a in accs[1:]:
            acc = acc + a
        return jnp.sum(acc, axis=0, keepdims=True)

    @pl.when(i * TQB + TQB <= top_k)
    def _():
        thr_scr[...] = jnp.full((1, TQB), -jnp.inf, _F32)
        tie_scr[...] = jnp.full((1, TQB), -1, jnp.int32)

    @pl.when(i * TQB + TQB > top_k)
    def _():
        zero = jnp.zeros((1, TQB), _F32)
        cnt_nonneg = count_ge(zero)
        cnt_pos = count_ge(zero, strict=True)
        key_max = _f32_to_key(jnp.max(ext_scr[0:8, :], axis=0, keepdims=True))
        key_min = _f32_to_key(jnp.min(ext_scr[8:16, :], axis=0, keepdims=True))
        n_causal = i * TQB + 1 + lax.broadcasted_iota(jnp.int32, (1, TQB), 1)
        nonneg = cnt_nonneg >= top_k
        lo = jnp.where(nonneg, 0, key_min)
        cnt_lo = jnp.where(nonneg, cnt_nonneg, n_causal)
        hi = jnp.where(nonneg, jnp.where(cnt_pos < top_k, 1, key_max + 1), 0)

        def bisect(by_value):
            def body(_, state):
                lo, hi, cnt_lo = state
                span = hi - lo
                mid = lo + lax.shift_right_logical(span, 1)
                if by_value:
                    lo_v, hi_v = _key_to_f32(lo), _key_to_f32(hi)
                    mid_v = _f32_to_key(lo_v + (hi_v - lo_v) * 0.5)
                    mid = jnp.where(span >= 2, jnp.minimum(jnp.maximum(mid_v, lo + 1), hi - 1), lo)
                cnt = count_ge(_key_to_f32(mid))
                ok = cnt >= top_k
                return jnp.where(ok, mid, lo), jnp.where(ok, hi, mid), jnp.where(ok, cnt, cnt_lo)
            return body

        def finished(state):
            lo, hi, cnt_lo = state
            done = jnp.where(cnt_lo == top_k, 1, jnp.where(hi - lo <= 1, 1, 0))
            return jnp.min(done) == 1

        state = lax.fori_loop(0, VALUE_PASSES, bisect(True), (lo, hi, cnt_lo))
        stops = (0,) + SEARCH_CHECKS + (31,)
        state = lax.fori_loop(stops[0], stops[1], bisect(False), state)
        for first, last in zip(stops[1:-1], stops[2:]):
            state = lax.cond(finished(state), lambda s: s,
                             functools.partial(lax.fori_loop, first, last, bisect(False)), state)
        lo, _, cnt_ge = state
        thr = _key_to_f32(lo)
        thr_scr[...] = thr
        tie_scr[...] = jnp.full((1, TQB), 2 ** 30, jnp.int32)
        row = lax.broadcasted_iota(jnp.int32, (KSUPER, TQB), 0)

        @pl.when(jnp.max(cnt_ge) > top_k)
        def _():
            need = top_k - count(lambda k, r0: _as01(k > thr))

            def tie_body(b, lo):
                cand = lo | lax.shift_left(jnp.int32(1), 10 - b)
                cnt = count(lambda k, r0: jnp.where(k == thr, _as01(row < cand - r0), 0))
                return jnp.where(cnt < need, cand, lo)
            tie_scr[...] = lax.fori_loop(0, 11, tie_body, jnp.zeros((1, TQB), jnp.int32))

    thr = thr_scr[...]
    tie = tie_scr[...]
    row = lax.broadcasted_iota(jnp.int32, (KSUPER, TQB), 0)

    def mask_body(c, carry):
        r0 = pl.multiple_of(c * KSUPER, KSUPER)
        k = keys_scr[pl.ds(r0, KSUPER), :]
        mb_scr[pl.ds(r0, KSUPER), :] = jnp.where(
            k == thr, jnp.where(row <= tie - r0, 0.0, NEG), jnp.where(k > thr, 0.0, NEG))
        return carry

    lax.fori_loop(0, ns, mask_body, 0)

    zeros = jnp.zeros((HEAD_DIM, TQB), _BF16)
    n_pairs = B_HEADS // 2
    rhs = []
    for p in range(n_pairs):
        q0 = qb_ref[0, (2 * p) * HEAD_DIM:(2 * p + 1) * HEAD_DIM, :]
        q1 = qb_ref[0, (2 * p + 1) * HEAD_DIM:(2 * p + 2) * HEAD_DIM, :]
        rhs.append(jnp.concatenate([jnp.concatenate([q0, zeros], axis=0),
                                    jnp.concatenate([zeros, q1], axis=0)], axis=1))

    def score_step(c):
        r0 = pl.multiple_of(c * KCHUNK, KCHUNK)
        for p in range(n_pairs):
            s = jnp.dot(kb_ref[0, pl.ds(r0, KCHUNK), p * LANES:(p + 1) * LANES], rhs[p],
                        preferred_element_type=_F32)
            for hh in range(2):
                h = 2 * p + hh
                parts = []
                for half in range(KCHUNK // TQ):
                    jb = c * (KCHUNK // TQ) + half
                    rs = pl.multiple_of(r0 + half * TQ, TQ)
                    subs = []
                    for sub in range(TQB // TQ):
                        kind = jnp.clip(jb - (i * (TQB // TQ) + sub) + 2, 0, 2)
                        lanes = slice(sub * TQ, (sub + 1) * TQ)
                        subs.append(s[half * TQ:(half + 1) * TQ, hh * TQB + sub * TQ:hh * TQB + (sub + 1) * TQ]
                                    + bias_ref[h, kind] + mb_scr[pl.ds(rs, TQ), lanes])
                    parts.append(jnp.concatenate(subs, axis=1))
                sh = jnp.concatenate(parts, axis=0)
                sc_scr[pl.ds(r0, KCHUNK), h * TQB:(h + 1) * TQB] = sh
                m_scr[h * 8:(h + 1) * 8, :] = jnp.maximum(m_scr[h * 8:(h + 1) * 8, :], _colmax8(sh))

    m_scr[...] = jnp.full(m_scr.shape, NEG, _F32)
    _by_pairs(nc, score_step)
    m_row = jnp.concatenate(
        [jnp.max(m_scr[h * 8:(h + 1) * 8, :], axis=0, keepdims=True) for h in range(B_HEADS)],
        axis=1)
    acc_scr[...] = jnp.zeros_like(acc_scr)

    def pv_step(c):
        r0 = pl.multiple_of(c * KCHUNK, KCHUNK)
        for p in range(n_pairs):
            cols = slice(2 * p * TQB, (2 * p + 2) * TQB)
            pr = jnp.exp2(sc_scr[pl.ds(r0, KCHUNK), cols] - m_row[:, cols]).astype(_BF16)
            acc_scr[p] += jnp.dot(vb_ref[0, c, p * PAIR_ROWS:(p + 1) * PAIR_ROWS, :], pr,
                                  preferred_element_type=_F32)

    _by_pairs(nc, pv_step)
    for p in range(n_pairs):
        inv = 1.0 / acc_scr[p, 2 * HEAD_DIM:2 * HEAD_DIM + 1, :]
        for hh in range(2):
            rows = slice((2 * p + hh) * HEAD_DIM, (2 * p + hh + 1) * HEAD_DIM)
            o = (acc_scr[p, hh * HEAD_DIM:(hh + 1) * HEAD_DIM, hh * TQB:(hh + 1) * TQB]
                 * inv[:, hh * TQB:(hh + 1) * TQB] * zb_ref[0, rows, :].astype(_F32))
            o_ref[0, rows, :] = o.astype(_BF16)


def _mix_b(qb_t, kb_n, vb_c, zb_t, qi_t, ki_n, wi_t, bias_b):
    B, _, S = qb_t.shape
    nq = S // TQB
    per_q = lambda rows: pl.BlockSpec((1, rows, TQB), lambda b, i: (b, 0, i))
    in_specs = [
        per_q(B_WIDTH),
        pl.BlockSpec((1, S, B_WIDTH), lambda b, i: (b, 0, 0)),
        pl.BlockSpec((1, S // KCHUNK, V_ROWS, KCHUNK), lambda b, i: (b, 0, 0, 0)),
        per_q(B_WIDTH),
        per_q(IDX_WIDTH),
        pl.BlockSpec((1, S, LANES), lambda b, i: (b, 0, 0)),
        per_q(IDX_HEADS),
        pl.BlockSpec((B_HEADS, 3, TQ, TQ), lambda b, i: (0, 0, 0, 0)),
    ]
    scratch = [
        pltpu.VMEM((S, TQB), _F32),
        pltpu.VMEM((S, TQB), _F32),
        pltpu.VMEM((S, B_HEADS * TQB), _F32),
        pltpu.VMEM((B_HEADS // 2, PAIR_ROWS, 2 * TQB), _F32),
        pltpu.VMEM((B_HEADS * 8, TQB), _F32),
        pltpu.VMEM((16, TQB), _F32),
        pltpu.VMEM((1, TQB), _F32),
        pltpu.VMEM((1, TQB), jnp.int32),
    ]
    return pl.pallas_call(
        _mix_b_kernel, grid=(B, nq), in_specs=in_specs,
        out_specs=per_q(B_WIDTH),
        out_shape=jax.ShapeDtypeStruct((B, B_WIDTH, S), _BF16),
        scratch_shapes=scratch,
        compiler_params=pltpu.CompilerParams(
            dimension_semantics=("parallel", "arbitrary"), vmem_limit_bytes=VMEM_LIMIT),
        name="mixer_b",
    )(qb_t, kb_n, vb_c, zb_t, qi_t, ki_n, wi_t, bias_b)


def _merge_kernel(x_ref, ya_ref, yb_ref, gt_ref, wpa_ref, wpb_ref, wo_ref, o_ref):
    pa = jnp.dot(wpa_ref[...], ya_ref[0], preferred_element_type=_F32)
    pb = jnp.dot(wpb_ref[...], yb_ref[0], preferred_element_type=_F32)
    merged = (gt_ref[0, :D_MODEL, :].astype(_F32) * pa
              + gt_ref[0, D_MODEL:, :].astype(_F32) * pb).astype(_BF16)
    out_t = jnp.dot(wo_ref[...], merged, preferred_element_type=_F32)
    o_ref[0] = x_ref[0] + out_t.T


def _merge(x, ya_t, yb_t, gt_t, wpa_t, wpb_t, wo_t):
    B, S, D = x.shape
    tm = TM_PROJ
    const = lambda b, t: (0, 0)
    in_specs = [
        pl.BlockSpec((1, tm, D), lambda b, t: (b, t, 0)),
        pl.BlockSpec((1, A_WIDTH, tm), lambda b, t: (b, 0, t)),
        pl.BlockSpec((1, B_WIDTH, tm), lambda b, t: (b, 0, t)),
        pl.BlockSpec((1, 2 * D, tm), lambda b, t: (b, 0, t)),
        pl.BlockSpec((D, A_WIDTH), const),
        pl.BlockSpec((D, B_WIDTH), const),
        pl.BlockSpec((D, D), const),
    ]
    return pl.pallas_call(
        _merge_kernel, grid=(B, S // tm), in_specs=in_specs,
        out_specs=pl.BlockSpec((1, tm, D), lambda b, t: (b, t, 0)),
        out_shape=jax.ShapeDtypeStruct((B, S, D), x.dtype),
        compiler_params=pltpu.CompilerParams(
            dimension_semantics=("parallel", "parallel"), vmem_limit_bytes=VMEM_LIMIT),
        name="merge_out",
    )(x, ya_t, yb_t, gt_t, wpa_t, wpb_t, wo_t)


def _t5_bucket(n):
    n = np.maximum(n, 0)
    max_exact = N_BUCKETS // 2
    nf = np.maximum(n, 1).astype(np.float64)
    large = max_exact + np.floor(np.log(nf / max_exact) / math.log(MAX_DISTANCE / max_exact)
                                 * (N_BUCKETS - max_exact)).astype(np.int64)
    large = np.minimum(large, N_BUCKETS - 1)
    return np.where(n < max_exact, n, large)


def _bias_tables(rel_bias):
    t = np.arange(TQ)[None, :]
    s = np.arange(TQ)[:, None]
    d_prev = t + TQ - s
    d_cur = t - s
    table_a = rel_bias[:, :A_Q_HEADS].astype(_F32).T
    table_b = rel_bias[:, A_Q_HEADS:].astype(_F32).T

    def tile(table, dist, windowed):
        onehot = (_t5_bucket(dist)[..., None] == np.arange(N_BUCKETS)).astype(np.float32)
        b = jnp.einsum("stk,hk->hst", onehot, table, precision=lax.Precision.HIGHEST)
        if windowed:
            b = jnp.where(((dist >= 0) & (dist < WINDOW))[None], b, NEG)
        return b

    bias_a = jnp.concatenate([tile(table_a, d_prev, True), tile(table_a, d_cur, True)], axis=1)
    far = jnp.broadcast_to(table_b[:, N_BUCKETS - 1][:, None, None], (B_HEADS, TQ, TQ))
    bias_b = jnp.stack([far, tile(table_b, d_prev, False), tile(table_b, d_cur, False)], axis=1) * LOG2E
    return bias_a, bias_b


def kernel(x, norm_g, w_in, qnorm_a, knorm_a, sinks_a, qnorm_b, knorm_b, rel_bias,
           w_proj_a, w_proj_b, w_out):
    assert norm_g.shape[0] == 1, "single-layer block"
    B, S, D = x.shape
    assert D == D_MODEL and S % TM_PROJ == 0 and S % KSUPER == 0 and S % TQB == 0 and TQB % TQ == 0
    assert MAX_DISTANCE <= TQ

    wt = w_in[0].T
    n_real = _SEG["kw"][0] + KW_REAL
    wt = jnp.concatenate(
        [wt[:n_real], jnp.zeros((_SEG["kw"][1] - n_real, D), wt.dtype), wt[n_real:]], axis=0)
    wt = wt.astype(_BF16)
    bcast = lambda g, scale: jnp.broadcast_to((g.astype(_F32) * scale)[:, None], (HEAD_DIM, TM_PROJ))
    q_scale = HEAD_DIM ** -0.5
    (qa_t, ka_n, va_t, za_t, qb_t, kb_n, vb_c, zb_t, qi_t, ki_n, wi_t, gt_t) = _inproj(
        x, norm_g[0][None, :].astype(_F32), wt,
        bcast(qnorm_a[0], q_scale), bcast(knorm_a[0], 1.0),
        bcast(qnorm_b[0], q_scale * LOG2E), bcast(knorm_b[0], 1.0))

    bias_a, bias_b = _bias_tables(rel_bias)
    sink_a = jnp.broadcast_to(sinks_a[0].astype(_F32)[:, None], (A_Q_HEADS, TQ))
    ya_t = _mix_a(qa_t, ka_n, va_t, za_t, bias_a, sink_a)
    yb_t = _mix_b(qb_t, kb_n, vb_c, zb_t, qi_t, ki_n, wi_t, bias_b)
    return _merge(x, ya_t, yb_t, gt_t,
                  w_proj_a[0].T.astype(_BF16), w_proj_b[0].T.astype(_BF16), w_out[0].T.astype(_BF16))
```

```python
import functools
import math

import jax
import jax.numpy as jnp
import numpy as np
from jax import lax
from jax.experimental import pallas as pl
from jax.experimental.pallas import tpu as pltpu

D_MODEL = 1024
HEAD_DIM = 64
A_Q_HEADS = 8
A_KV_HEADS = 2
A_GROUPS = A_Q_HEADS // A_KV_HEADS
A_WIDTH = A_Q_HEADS * HEAD_DIM
A_KV_WIDTH = A_KV_HEADS * HEAD_DIM
B_HEADS = 8
B_WIDTH = B_HEADS * HEAD_DIM
IDX_HEADS = 8
IDX_DIM = 32
IDX_WIDTH = IDX_HEADS * IDX_DIM
WINDOW = 128
TOPK_MAX = 256
N_BUCKETS = 32
MAX_DISTANCE = 128
RMS_EPS = 1e-6

LANES = 128
TQ = 128
A_SUB = 4
TQB = 256
KCHUNK = 256
KSUPER = 2 * KCHUNK
VALUE_PASSES = 16
SEARCH_CHECKS = (1, 3, 5, 8)
TM_PROJ = 1024
NEG = -1e30
LOG2E = math.log2(math.e)
ONES_ROWS = 16
PAIR_ROWS = 2 * HEAD_DIM + ONES_ROWS
V_ROWS = (B_HEADS // 2) * PAIR_ROWS
INT_MIN = -(2 ** 31)
VMEM_LIMIT = 56 * 1024 * 1024

_SEG = {}
_off = 0
for _name, _rows in (("qa", A_WIDTH), ("ka", A_KV_WIDTH), ("va", A_KV_WIDTH), ("za", A_WIDTH),
                     ("qb", B_WIDTH), ("kb", B_WIDTH), ("vb", B_WIDTH), ("zb", B_WIDTH),
                     ("qi", IDX_WIDTH), ("kw", LANES), ("gates", 2 * D_MODEL)):
    _SEG[_name] = (_off, _off + _rows)
    _off += _rows
PROJ_ROWS = _off
KW_REAL = IDX_DIM + IDX_HEADS

_F32 = jnp.float32
_BF16 = jnp.bfloat16
_NT = (((1,), (1,)), ((), ()))


def _inproj_kernel(x_ref, g_ref, wt_ref, gqa_ref, gka_ref, gqb_ref, gkb_ref,
                   qa_o, ka_o, va_o, za_o, qb_o, kb_o, vb_o, zb_o, qi_o, ki_o, wi_o, gt_o):
    tm = x_ref.shape[1]
    x = x_ref[0]
    ms = jnp.mean(x * x, axis=-1, keepdims=True)
    h = (x * lax.rsqrt(ms + RMS_EPS) * g_ref[...]).astype(_BF16)

    def proj(lo, hi):
        return lax.dot_general(wt_ref[lo:hi, :], h, _NT, preferred_element_type=_F32)

    def seg(name):
        return proj(*_SEG[name])

    def headnorm(p, gain_ref):
        nh = p.shape[0] // HEAD_DIM
        p3 = p.reshape(nh, HEAD_DIM, tm)
        r = lax.rsqrt(jnp.mean(p3 * p3, axis=1, keepdims=True) + RMS_EPS)
        return (p3 * r * gain_ref[...][None]).reshape(nh * HEAD_DIM, tm)

    def silu(p):
        return p * jax.nn.sigmoid(p)

    qa_o[0] = headnorm(seg("qa"), gqa_ref).astype(_BF16)
    ka_o[0] = headnorm(seg("ka"), gka_ref).T.astype(_BF16)
    va_o[0] = seg("va").astype(_BF16)
    za_o[0] = silu(seg("za")).astype(_BF16)
    qb_o[0] = headnorm(seg("qb"), gqb_ref).astype(_BF16)
    kb_o[0] = headnorm(seg("kb"), gkb_ref).T.astype(_BF16)
    vb = seg("vb").astype(_BF16)
    ones = jnp.ones((ONES_ROWS, KCHUNK), _BF16)
    for c in range(tm // KCHUNK):
        for p in range(B_HEADS // 2):
            vb_o[0, c, p * PAIR_ROWS:(p + 1) * PAIR_ROWS, :] = jnp.concatenate(
                [vb[p * LANES:(p + 1) * LANES, c * KCHUNK:(c + 1) * KCHUNK], ones], axis=0)
    zb_o[0] = silu(seg("zb")).astype(_BF16)
    qi_o[0] = seg("qi").astype(_BF16)
    kw = seg("kw")
    wi_o[0] = kw[IDX_DIM:KW_REAL, :]
    row = lax.broadcasted_iota(jnp.int32, kw.shape, 0)
    ki_o[0] = jnp.where(row < IDX_DIM, kw, 0.0).T.astype(_BF16)
    g_lo = _SEG["gates"][0]
    for c in range(2 * D_MODEL // 512):
        gt_o[0, c * 512:(c + 1) * 512, :] = jax.nn.sigmoid(
            proj(g_lo + c * 512, g_lo + (c + 1) * 512)).astype(_BF16)


def _inproj(x, g, wt, gqa, gka, gqb, gkb):
    B, S, D = x.shape
    tm = TM_PROJ
    nt = S // tm
    const = lambda b, t: (0, 0)
    tr = lambda rows: pl.BlockSpec((1, rows, tm), lambda b, t: (b, 0, t))
    nat = lambda cols: pl.BlockSpec((1, tm, cols), lambda b, t: (b, t, 0))
    sds = jax.ShapeDtypeStruct
    out_shape = (
        sds((B, A_WIDTH, S), _BF16), sds((B, S, A_KV_WIDTH), _BF16), sds((B, A_KV_WIDTH, S), _BF16),
        sds((B, A_WIDTH, S), _BF16),
        sds((B, B_WIDTH, S), _BF16), sds((B, S, B_WIDTH), _BF16),
        sds((B, S // KCHUNK, V_ROWS, KCHUNK), _BF16), sds((B, B_WIDTH, S), _BF16),
        sds((B, IDX_WIDTH, S), _BF16), sds((B, S, LANES), _BF16), sds((B, IDX_HEADS, S), _F32),
        sds((B, 2 * D_MODEL, S), _BF16),
    )
    out_specs = (
        tr(A_WIDTH), nat(A_KV_WIDTH), tr(A_KV_WIDTH), tr(A_WIDTH),
        tr(B_WIDTH), nat(B_WIDTH),
        pl.BlockSpec((1, tm // KCHUNK, V_ROWS, KCHUNK), lambda b, t: (b, t, 0, 0)), tr(B_WIDTH),
        tr(IDX_WIDTH), nat(LANES), tr(IDX_HEADS), tr(2 * D_MODEL),
    )
    in_specs = [
        pl.BlockSpec((1, tm, D), lambda b, t: (b, t, 0)),
        pl.BlockSpec((1, D), const),
        pl.BlockSpec((PROJ_ROWS, D), const, pipeline_mode=pl.Buffered(1)),
        pl.BlockSpec((HEAD_DIM, tm), const), pl.BlockSpec((HEAD_DIM, tm), const),
        pl.BlockSpec((HEAD_DIM, tm), const), pl.BlockSpec((HEAD_DIM, tm), const),
    ]
    return pl.pallas_call(
        _inproj_kernel, grid=(B, nt), in_specs=in_specs, out_specs=out_specs, out_shape=out_shape,
        compiler_params=pltpu.CompilerParams(
            dimension_semantics=("parallel", "parallel"), vmem_limit_bytes=VMEM_LIMIT),
        name="inproj",
    )(x, g, wt, gqa, gka, gqb, gkb)


def _mix_a_kernel(q_ref, kp_ref, kc_ref, vp_ref, vc_ref, z_ref, bias_ref, sink_ref, o_ref):
    i = pl.program_id(1)
    zeros = jnp.zeros((HEAD_DIM, TQ), _BF16)
    units = [(j, g) for j in range(A_SUB) for g in range(A_KV_HEADS)]

    def bands(j):
        cur = slice(j * TQ, (j + 1) * TQ)
        old = slice((j - 1) * TQ, j * TQ)
        k_old = kp_ref[0] if j == 0 else kc_ref[0, old, :]
        v_old = vp_ref[0] if j == 0 else vc_ref[0, :, old]
        return (jnp.concatenate([k_old, kc_ref[0, cur, :]], axis=0),
                jnp.concatenate([v_old, vc_ref[0, :, cur]], axis=1))

    scores = []
    for j, g in units:
        cur = slice(j * TQ, (j + 1) * TQ)
        cols = []
        for hh in range(A_GROUPS):
            h = g * A_GROUPS + hh
            parts = [zeros] * A_KV_HEADS
            parts[g] = q_ref[0, h * HEAD_DIM:(h + 1) * HEAD_DIM, cur]
            cols.append(jnp.concatenate(parts, axis=0))
        rhs = jnp.concatenate(cols, axis=1)
        scores.append(jnp.dot(bands(j)[0], rhs, preferred_element_type=_F32))

    pts, invs = [], []
    for (j, g), sc in zip(units, scores):
        probs, inv = [], []
        for hh in range(A_GROUPS):
            h = g * A_GROUPS + hh
            s = sc[:, hh * TQ:(hh + 1) * TQ] + bias_ref[h]
            if j == 0:
                s = jnp.concatenate([jnp.where(i > 0, s[:TQ], NEG), s[TQ:]], axis=0)
            sink = sink_ref[h:h + 1, :]
            m = jnp.maximum(jnp.max(s, axis=0, keepdims=True), sink)
            p = jnp.exp(s - m)
            denom = jnp.sum(p, axis=0, keepdims=True) + jnp.exp(sink - m)
            probs.append(p.astype(_BF16))
            inv.append(1.0 / denom)
        pts.append(jnp.concatenate(probs, axis=1))
        invs.append(inv)

    for (j, g), pt, inv in zip(units, pts, invs):
        cur = slice(j * TQ, (j + 1) * TQ)
        out = jnp.dot(bands(j)[1][g * HEAD_DIM:(g + 1) * HEAD_DIM, :], pt,
                      preferred_element_type=_F32)
        for hh in range(A_GROUPS):
            h = g * A_GROUPS + hh
            rows = slice(h * HEAD_DIM, (h + 1) * HEAD_DIM)
            o = out[:, hh * TQ:(hh + 1) * TQ] * inv[hh] * z_ref[0, rows, cur].astype(_F32)
            o_ref[0, rows, cur] = o.astype(_BF16)


def _mix_a(qa_t, ka_n, va_t, za_t, bias_a, sink_a):
    B, _, S = qa_t.shape
    tqa = A_SUB * TQ
    prev = lambda i: jnp.maximum(i * A_SUB - 1, 0)
    in_specs = [
        pl.BlockSpec((1, A_WIDTH, tqa), lambda b, i: (b, 0, i)),
        pl.BlockSpec((1, TQ, A_KV_WIDTH), lambda b, i: (b, prev(i), 0)),
        pl.BlockSpec((1, tqa, A_KV_WIDTH), lambda b, i: (b, i, 0)),
        pl.BlockSpec((1, A_KV_WIDTH, TQ), lambda b, i: (b, 0, prev(i))),
        pl.BlockSpec((1, A_KV_WIDTH, tqa), lambda b, i: (b, 0, i)),
        pl.BlockSpec((1, A_WIDTH, tqa), lambda b, i: (b, 0, i)),
        pl.BlockSpec((A_Q_HEADS, 2 * TQ, TQ), lambda b, i: (0, 0, 0)),
        pl.BlockSpec((A_Q_HEADS, TQ), lambda b, i: (0, 0)),
    ]
    return pl.pallas_call(
        _mix_a_kernel, grid=(B, S // tqa), in_specs=in_specs,
        out_specs=pl.BlockSpec((1, A_WIDTH, tqa), lambda b, i: (b, 0, i)),
        out_shape=jax.ShapeDtypeStruct((B, A_WIDTH, S), _BF16),
        compiler_params=pltpu.CompilerParams(
            dimension_semantics=("parallel", "parallel"), vmem_limit_bytes=VMEM_LIMIT),
        name="mixer_a",
    )(qa_t, ka_n, ka_n, va_t, va_t, za_t, bias_a, sink_a)


def _colsum8(v):
    return jnp.sum(v.reshape(v.shape[0] // 8, 8, v.shape[1]), axis=0)


def _as01(pred):
    return jnp.where(pred, 1, 0).astype(jnp.int32)


def _key_to_f32(key):
    return pltpu.bitcast(jnp.where(key < 0, key ^ jnp.int32(0x7FFFFFFF), key), _F32)


def _f32_to_key(v):
    bits = pltpu.bitcast(v, jnp.int32)
    return jnp.where(bits < 0, bits ^ jnp.int32(0x7FFFFFFF), bits)


def _colmin8(v):
    return jnp.min(v.reshape(v.shape[0] // 8, 8, v.shape[1]), axis=0)


def _colmax8(v):
    return jnp.max(v.reshape(v.shape[0] // 8, 8, v.shape[1]), axis=0)


def _by_pairs(n, step):
    def body(t, carry):
        for u in range(4):
            step(4 * t + u)
        return carry

    lax.fori_loop(0, n // 4, body, 0)
    base = (n // 4) * 4

    @pl.when((n & 2) != 0)
    def _():
        step(base)
        step(base + 1)

    @pl.when((n & 1) != 0)
    def _():
        step(base + (n & 2))


def _mix_b_kernel(qb_ref, kb_ref, vb_ref, zb_ref, qi_ref, ki_ref, wi_ref, bias_ref, o_ref,
                  keys_scr, mb_scr, sc_scr, acc_scr, m_scr, ext_scr, thr_scr, tie_scr):
    i = pl.program_id(1)
    nc = ((i + 1) * TQB + KCHUNK - 1) // KCHUNK
    ns = (nc + 1) // 2
    top_k = TOPK_MAX

    qi = qi_ref[0]
    zpad = jnp.zeros((LANES - IDX_DIM, TQB), _BF16)
    rhs_i = jnp.concatenate(
        [jnp.concatenate([qi[h * IDX_DIM:(h + 1) * IDX_DIM], zpad], axis=0)
         for h in range(IDX_HEADS)], axis=1)
    w = wi_ref[0] * (IDX_DIM ** -0.5 * IDX_HEADS ** -0.5)
    s_minus_t = (lax.broadcasted_iota(jnp.int32, (KCHUNK, TQB), 0)
                 - lax.broadcasted_iota(jnp.int32, (KCHUNK, TQB), 1))

    def index_step(c):
        r0 = pl.multiple_of(c * KCHUNK, KCHUNK)
        d = jnp.dot(ki_ref[0, pl.ds(r0, KCHUNK), :], rhs_i,
                    preferred_element_type=_F32)
        acc = w[0:1, :] * jnp.maximum(d[:, 0:TQB], 0.0)
        for h in range(1, IDX_HEADS):
            acc = acc + w[h:h + 1, :] * jnp.maximum(d[:, h * TQB:(h + 1) * TQB], 0.0)
        causal = s_minus_t <= (i * TQB - r0)
        score = jnp.where(causal, acc, -jnp.inf)
        keys_scr[pl.ds(r0, KCHUNK), :] = score
        ext_scr[0:8, :] = jnp.maximum(ext_scr[0:8, :], _colmax8(score))
        ext_scr[8:16, :] = jnp.minimum(ext_scr[8:16, :], _colmin8(jnp.where(causal, acc, jnp.inf)))

    ext_scr[0:8, :] = jnp.full((8, TQB), -jnp.inf, _F32)
    ext_scr[8:16, :] = jnp.full((8, TQB), jnp.inf, _F32)
    _by_pairs(nc, index_step)

    @pl.when(nc % 2 == 1)
    def _():
        keys_scr[pl.ds(pl.multiple_of(nc * KCHUNK, KCHUNK), KCHUNK), :] = jnp.full(
            (KCHUNK, TQB), -jnp.inf, _F32)

    def count(pred_fn):
        def body(c, acc):
            r0 = pl.multiple_of(c * KSUPER, KSUPER)
            return acc + _colsum8(pred_fn(keys_scr[pl.ds(r0, KSUPER), :], r0))
        acc = lax.fori_loop(0, ns, body, jnp.zeros((8, TQB), jnp.int32))
        return jnp.sum(acc, axis=0, keepdims=True)

    def count_ge(cand, strict=False):
        n_acc = 8
        hit = (lambda r: r > cand) if strict else (lambda r: r >= cand)

        def bump(accs, r0, n_rows):
            accs = list(accs)
            rows = keys_scr[pl.ds(r0, n_rows), :]
            for j in range(n_rows // 8):
                a = accs[j % n_acc]
                accs[j % n_acc] = jnp.where(hit(rows[j * 8:(j + 1) * 8]), a + 1, a)
            return tuple(accs)

        accs = lax.fori_loop(
            0, nc // 2, lambda c, accs: bump(accs, pl.multiple_of(c * KSUPER, KSUPER), KSUPER),
            tuple(jnp.zeros((8, TQB), jnp.int32) for _ in range(n_acc)))
        accs = lax.cond(nc % 2 == 1,
                        lambda accs: bump(accs, pl.multiple_of((nc - 1) * KCHUNK, KCHUNK), KCHUNK),
                        lambda accs: accs, accs)
        acc = accs[0]
        for a in accs[1:]:
            acc = acc + a
        return jnp.sum(acc, axis=0, keepdims=True)

    @pl.when(i * TQB + TQB <= top_k)
    def _():
        thr_scr[...] = jnp.full((1, TQB), -jnp.inf, _F32)
        tie_scr[...] = jnp.full((1, TQB), -1, jnp.int32)

    @pl.when(i * TQB + TQB > top_k)
    def _():
        zero = jnp.zeros((1, TQB), _F32)
        cnt_nonneg = count_ge(zero)
        cnt_pos = count_ge(zero, strict=True)
        key_max = _f32_to_key(jnp.max(ext_scr[0:8, :], axis=0, keepdims=True))
        key_min = _f32_to_key(jnp.min(ext_scr[8:16, :], axis=0, keepdims=True))
        n_causal = i * TQB + 1 + lax.broadcasted_iota(jnp.int32, (1, TQB), 1)
        nonneg = cnt_nonneg >= top_k
        lo = jnp.where(nonneg, 0, key_min)
        cnt_lo = jnp.where(nonneg, cnt_nonneg, n_causal)
        hi = jnp.where(nonneg, jnp.where(cnt_pos < top_k, 1, key_max + 1), 0)

        def bisect(by_value):
            def body(_, state):
                lo, hi, cnt_lo = state
                span = hi - lo
                mid = lo + lax.shift_right_logical(span, 1)
                if by_value:
                    lo_v, hi_v = _key_to_f32(lo), _key_to_f32(hi)
                    mid_v = _f32_to_key(lo_v + (hi_v - lo_v) * 0.5)
                    mid = jnp.where(span >= 2, jnp.minimum(jnp.maximum(mid_v, lo + 1), hi - 1), lo)
                cnt = count_ge(_key_to_f32(mid))
                ok = cnt >= top_k
                return jnp.where(ok, mid, lo), jnp.where(ok, hi, mid), jnp.where(ok, cnt, cnt_lo)
            return body

        def finished(state):
            lo, hi, cnt_lo = state
            done = jnp.where(cnt_lo == top_k, 1, jnp.where(hi - lo <= 1, 1, 0))
            return jnp.min(done) == 1

        state = lax.fori_loop(0, VALUE_PASSES, bisect(True), (lo, hi, cnt_lo))
        stops = (0,) + SEARCH_CHECKS + (31,)
        state = lax.fori_loop(stops[0], stops[1], bisect(False), state)
        for first, last in zip(stops[1:-1], stops[2:]):
            state = lax.cond(finished(state), lambda s: s,
                             functools.partial(lax.fori_loop, first, last, bisect(False)), state)
        lo, _, cnt_ge = state
        thr = _key_to_f32(lo)
        thr_scr[...] = thr
        tie_scr[...] = jnp.full((1, TQB), 2 ** 30, jnp.int32)
        row = lax.broadcasted_iota(jnp.int32, (KSUPER, TQB), 0)

        @pl.when(jnp.max(cnt_ge) > top_k)
        def _():
            need = top_k - count(lambda k, r0: _as01(k > thr))

            def tie_body(b, lo):
                cand = lo | lax.shift_left(jnp.int32(1), 10 - b)
                cnt = count(lambda k, r0: jnp.where(k == thr, _as01(row < cand - r0), 0))
                return jnp.where(cnt < need, cand, lo)
            tie_scr[...] = lax.fori_loop(0, 11, tie_body, jnp.zeros((1, TQB), jnp.int32))

    thr = thr_scr[...]
    tie = tie_scr[...]
    row = lax.broadcasted_iota(jnp.int32, (KSUPER, TQB), 0)

    def mask_body(c, carry):
        r0 = pl.multiple_of(c * KSUPER, KSUPER)
        k = keys_scr[pl.ds(r0, KSUPER), :]
        mb_scr[pl.ds(r0, KSUPER), :] = jnp.where(
            k == thr, jnp.where(row <= tie - r0, 0.0, NEG), jnp.where(k > thr, 0.0, NEG))
        return carry

    lax.fori_loop(0, ns, mask_body, 0)

    zeros = jnp.zeros((HEAD_DIM, TQB), _BF16)
    n_pairs = B_HEADS // 2
    rhs = []
    for p in range(n_pairs):
        q0 = qb_ref[0, (2 * p) * HEAD_DIM:(2 * p + 1) * HEAD_DIM, :]
        q1 = qb_ref[0, (2 * p + 1) * HEAD_DIM:(2 * p + 2) * HEAD_DIM, :]
        rhs.append(jnp.concatenate([jnp.concatenate([q0, zeros], axis=0),
                                    jnp.concatenate([zeros, q1], axis=0)], axis=1))

    def score_step(c):
        r0 = pl.multiple_of(c * KCHUNK, KCHUNK)
        for p in range(n_pairs):
            s = jnp.dot(kb_ref[0, pl.ds(r0, KCHUNK), p * LANES:(p + 1) * LANES], rhs[p],
                        preferred_element_type=_F32)
            for hh in range(2):
                h = 2 * p + hh
                parts = []
                for half in range(KCHUNK // TQ):
                    jb = c * (KCHUNK // TQ) + half
                    rs = pl.multiple_of(r0 + half * TQ, TQ)
                    subs = []
                    for sub in range(TQB // TQ):
                        kind = jnp.clip(jb - (i * (TQB // TQ) + sub) + 2, 0, 2)
                        lanes = slice(sub * TQ, (sub + 1) * TQ)
                        subs.append(s[half * TQ:(half + 1) * TQ, hh * TQB + sub * TQ:hh * TQB + (sub + 1) * TQ]
                                    + bias_ref[h, kind] + mb_scr[pl.ds(rs, TQ), lanes])
                    parts.append(jnp.concatenate(subs, axis=1))
                sh = jnp.concatenate(parts, axis=0)
                sc_scr[pl.ds(r0, KCHUNK), h * TQB:(h + 1) * TQB] = sh
                m_scr[h * 8:(h + 1) * 8, :] = jnp.maximum(m_scr[h * 8:(h + 1) * 8, :], _colmax8(sh))

    m_scr[...] = jnp.full(m_scr.shape, NEG, _F32)
    _by_pairs(nc, score_step)
    m_row = jnp.concatenate(
        [jnp.max(m_scr[h * 8:(h + 1) * 8, :], axis=0, keepdims=True) for h in range(B_HEADS)],
        axis=1)
    acc_scr[...] = jnp.zeros_like(acc_scr)

    def pv_step(c):
        r0 = pl.multiple_of(c * KCHUNK, KCHUNK)
        for p in range(n_pairs):
            cols = slice(2 * p * TQB, (2 * p + 2) * TQB)
            pr = jnp.exp2(sc_scr[pl.ds(r0, KCHUNK), cols] - m_row[:, cols]).astype(_BF16)
            acc_scr[p] += jnp.dot(vb_ref[0, c, p * PAIR_ROWS:(p + 1) * PAIR_ROWS, :], pr,
                                  preferred_element_type=_F32)

    _by_pairs(nc, pv_step)
    for p in range(n_pairs):
        inv = 1.0 / acc_scr[p, 2 * HEAD_DIM:2 * HEAD_DIM + 1, :]
        for hh in range(2):
            rows = slice((2 * p + hh) * HEAD_DIM, (2 * p + hh + 1) * HEAD_DIM)
            o = (acc_scr[p, hh * HEAD_DIM:(hh + 1) * HEAD_DIM, hh * TQB:(hh + 1) * TQB]
                 * inv[:, hh * TQB:(hh + 1) * TQB] * zb_ref[0, rows, :].astype(_F32))
            o_ref[0, rows, :] = o.astype(_BF16)


def _mix_b(qb_t, kb_n, vb_c, zb_t, qi_t, ki_n, wi_t, bias_b):
    B, _, S = qb_t.shape
    nq = S // TQB
    per_q = lambda rows: pl.BlockSpec((1, rows, TQB), lambda b, i: (b, 0, i))
    in_specs = [
        per_q(B_WIDTH),
        pl.BlockSpec((1, S, B_WIDTH), lambda b, i: (b, 0, 0)),
        pl.BlockSpec((1, S // KCHUNK, V_ROWS, KCHUNK), lambda b, i: (b, 0, 0, 0)),
        per_q(B_WIDTH),
        per_q(IDX_WIDTH),
        pl.BlockSpec((1, S, LANES), lambda b, i: (b, 0, 0)),
        per_q(IDX_HEADS),
        pl.BlockSpec((B_HEADS, 3, TQ, TQ), lambda b, i: (0, 0, 0, 0)),
    ]
    scratch = [
        pltpu.VMEM((S, TQB), _F32),
        pltpu.VMEM((S, TQB), _F32),
        pltpu.VMEM((S, B_HEADS * TQB), _F32),
        pltpu.VMEM((B_HEADS // 2, PAIR_ROWS, 2 * TQB), _F32),
        pltpu.VMEM((B_HEADS * 8, TQB), _F32),
        pltpu.VMEM((16, TQB), _F32),
        pltpu.VMEM((1, TQB), _F32),
        pltpu.VMEM((1, TQB), jnp.int32),
    ]
    return pl.pallas_call(
        _mix_b_kernel, grid=(B, nq), in_specs=in_specs,
        out_specs=per_q(B_WIDTH),
        out_shape=jax.ShapeDtypeStruct((B, B_WIDTH, S), _BF16),
        scratch_shapes=scratch,
        compiler_params=pltpu.CompilerParams(
            dimension_semantics=("parallel", "arbitrary"), vmem_limit_bytes=VMEM_LIMIT),
        name="mixer_b",
    )(qb_t, kb_n, vb_c, zb_t, qi_t, ki_n, wi_t, bias_b)


def _merge_kernel(x_ref, ya_ref, yb_ref, gt_ref, wpa_ref, wpb_ref, wo_ref, o_ref):
    pa = jnp.dot(wpa_ref[...], ya_ref[0], preferred_element_type=_F32)
    pb = jnp.dot(wpb_ref[...], yb_ref[0], preferred_element_type=_F32)
    merged = (gt_ref[0, :D_MODEL, :].astype(_F32) * pa
              + gt_ref[0, D_MODEL:, :].astype(_F32) * pb).astype(_BF16)
    out_t = jnp.dot(wo_ref[...], merged, preferred_element_type=_F32)
    o_ref[0] = x_ref[0] + out_t.T


def _merge(x, ya_t, yb_t, gt_t, wpa_t, wpb_t, wo_t):
    B, S, D = x.shape
    tm = TM_PROJ
    const = lambda b, t: (0, 0)
    in_specs = [
        pl.BlockSpec((1, tm, D), lambda b, t: (b, t, 0)),
        pl.BlockSpec((1, A_WIDTH, tm), lambda b, t: (b, 0, t)),
        pl.BlockSpec((1, B_WIDTH, tm), lambda b, t: (b, 0, t)),
        pl.BlockSpec((1, 2 * D, tm), lambda b, t: (b, 0, t)),
        pl.BlockSpec((D, A_WIDTH), const),
        pl.BlockSpec((D, B_WIDTH), const),
        pl.BlockSpec((D, D), const),
    ]
    return pl.pallas_call(
        _merge_kernel, grid=(B, S // tm), in_specs=in_specs,
        out_specs=pl.BlockSpec((1, tm, D), lambda b, t: (b, t, 0)),
        out_shape=jax.ShapeDtypeStruct((B, S, D), x.dtype),
        compiler_params=pltpu.CompilerParams(
            dimension_semantics=("parallel", "parallel"), vmem_limit_bytes=VMEM_LIMIT),
        name="merge_out",
    )(x, ya_t, yb_t, gt_t, wpa_t, wpb_t, wo_t)


def _t5_bucket(n):
    n = np.maximum(n, 0)
    max_exact = N_BUCKETS // 2
    nf = np.maximum(n, 1).astype(np.float64)
    large = max_exact + np.floor(np.log(nf / max_exact) / math.log(MAX_DISTANCE / max_exact)
                                 * (N_BUCKETS - max_exact)).astype(np.int64)
    large = np.minimum(large, N_BUCKETS - 1)
    return np.where(n < max_exact, n, large)


def _bias_tables(rel_bias):
    t = np.arange(TQ)[None, :]
    s = np.arange(TQ)[:, None]
    d_prev = t + TQ - s
    d_cur = t - s
    table_a = rel_bias[:, :A_Q_HEADS].astype(_F32).T
    table_b = rel_bias[:, A_Q_HEADS:].astype(_F32).T

    def tile(table, dist, windowed):
        onehot = (_t5_bucket(dist)[..., None] == np.arange(N_BUCKETS)).astype(np.float32)
        b = jnp.einsum("stk,hk->hst", onehot, table, precision=lax.Precision.HIGHEST)
        if windowed:
            b = jnp.where(((dist >= 0) & (dist < WINDOW))[None], b, NEG)
        return b

    bias_a = jnp.concatenate([tile(table_a, d_prev, True), tile(table_a, d_cur, True)], axis=1)
    far = jnp.broadcast_to(table_b[:, N_BUCKETS - 1][:, None, None], (B_HEADS, TQ, TQ))
    bias_b = jnp.stack([far, tile(table_b, d_prev, False), tile(table_b, d_cur, False)], axis=1) * LOG2E
    return bias_a, bias_b


def kernel(x, norm_g, w_in, qnorm_a, knorm_a, sinks_a, qnorm_b, knorm_b, rel_bias,
           w_proj_a, w_proj_b, w_out):
    assert norm_g.shape[0] == 1, "single-layer block"
    B, S, D = x.shape
    assert D == D_MODEL and S % TM_PROJ == 0 and S % KSUPER == 0 and S % TQB == 0 and TQB % TQ == 0
    assert MAX_DISTANCE <= TQ

    wt = w_in[0].T
    n_real = _SEG["kw"][0] + KW_REAL
    wt = jnp.concatenate(
        [wt[:n_real], jnp.zeros((_SEG["kw"][1] - n_real, D), wt.dtype), wt[n_real:]], axis=0)
    wt = wt.astype(_BF16)
    bcast = lambda g, scale: jnp.broadcast_to((g.astype(_F32) * scale)[:, None], (HEAD_DIM, TM_PROJ))
    q_scale = HEAD_DIM ** -0.5
    (qa_t, ka_n, va_t, za_t, qb_t, kb_n, vb_c, zb_t, qi_t, ki_n, wi_t, gt_t) = _inproj(
        x, norm_g[0][None, :].astype(_F32), wt,
        bcast(qnorm_a[0], q_scale), bcast(knorm_a[0], 1.0),
        bcast(qnorm_b[0], q_scale * LOG2E), bcast(knorm_b[0], 1.0))

    bias_a, bias_b = _bias_tables(rel_bias)
    sink_a = jnp.broadcast_to(sinks_a[0].astype(_F32)[:, None], (A_Q_HEADS, TQ))
    ya_t = _mix_a(qa_t, ka_n, va_t, za_t, bias_a, sink_a)
    yb_t = _mix_b(qb_t, kb_n, vb_c, zb_t, qi_t, ki_n, wi_t, bias_b)
    return _merge(x, ya_t, yb_t, gt_t,
                  w_proj_a[0].T.astype(_BF16), w_proj_b[0].T.astype(_BF16), w_out[0].T.astype(_BF16))
```

```python
import functools
import math

import jax
import jax.numpy as jnp
import numpy as np
from jax import lax
from jax.experimental import pallas as pl
from jax.experimental.pallas import tpu as pltpu

D_MODEL = 1024
HEAD_DIM = 64
A_Q_HEADS = 8
A_KV_HEADS = 2
A_GROUPS = A_Q_HEADS // A_KV_HEADS
A_WIDTH = A_Q_HEADS * HEAD_DIM
A_KV_WIDTH = A_KV_HEADS * HEAD_DIM
B_HEADS = 8
B_WIDTH = B_HEADS * HEAD_DIM
IDX_HEADS = 8
IDX_DIM = 32
IDX_WIDTH = IDX_HEADS * IDX_DIM
WINDOW = 128
TOPK_MAX = 256
N_BUCKETS = 32
MAX_DISTANCE = 128
RMS_EPS = 1e-6

LANES = 128
TQ = 128
A_SUB = 4
TQB = 256
KCHUNK = 256
KSUPER = 2 * KCHUNK
VALUE_PASSES = 16
SEARCH_CHECKS = (1, 3, 5, 8)
TM_PROJ = 1024
NEG = -1e30
LOG2E = math.log2(math.e)
ONES_ROWS = 16
PAIR_ROWS = 2 * HEAD_DIM + ONES_ROWS
V_ROWS = (B_HEADS // 2) * PAIR_ROWS
INT_MIN = -(2 ** 31)
VMEM_LIMIT = 56 * 1024 * 1024

_SEG = {}
_off = 0
for _name, _rows in (("qa", A_WIDTH), ("ka", A_KV_WIDTH), ("va", A_KV_WIDTH), ("za", A_WIDTH),
                     ("qb", B_WIDTH), ("kb", B_WIDTH), ("vb", B_WIDTH), ("zb", B_WIDTH),
                     ("qi", IDX_WIDTH), ("kw", LANES), ("gates", 2 * D_MODEL)):
    _SEG[_name] = (_off, _off + _rows)
    _off += _rows
PROJ_ROWS = _off
KW_REAL = IDX_DIM + IDX_HEADS

_F32 = jnp.float32
_BF16 = jnp.bfloat16
_NT = (((1,), (1,)), ((), ()))


def _inproj_kernel(x_ref, g_ref, wt_ref, gqa_ref, gka_ref, gqb_ref, gkb_ref,
                   qa_o, ka_o, va_o, za_o, qb_o, kb_o, vb_o, zb_o, qi_o, ki_o, wi_o, gt_o):
    tm = x_ref.shape[1]
    x = x_ref[0]
    ms = jnp.mean(x * x, axis=-1, keepdims=True)
    h = (x * lax.rsqrt(ms + RMS_EPS) * g_ref[...]).astype(_BF16)

    def proj(lo, hi):
        return lax.dot_general(wt_ref[lo:hi, :], h, _NT, preferred_element_type=_F32)

    def seg(name):
        return proj(*_SEG[name])

    def headnorm(p, gain_ref):
        nh = p.shape[0] // HEAD_DIM
        p3 = p.reshape(nh, HEAD_DIM, tm)
        r = lax.rsqrt(jnp.mean(p3 * p3, axis=1, keepdims=True) + RMS_EPS)
        return (p3 * r * gain_ref[...][None]).reshape(nh * HEAD_DIM, tm)

    def silu(p):
        return p * jax.nn.sigmoid(p)

    qa_o[0] = headnorm(seg("qa"), gqa_ref).astype(_BF16)
    ka_o[0] = headnorm(seg("ka"), gka_ref).T.astype(_BF16)
    va_o[0] = seg("va").astype(_BF16)
    za_o[0] = silu(seg("za")).astype(_BF16)
    qb_o[0] = headnorm(seg("qb"), gqb_ref).astype(_BF16)
    kb_o[0] = headnorm(seg("kb"), gkb_ref).T.astype(_BF16)
    vb = seg("vb").astype(_BF16)
    ones = jnp.ones((ONES_ROWS, KCHUNK), _BF16)
    for c in range(tm // KCHUNK):
        for p in range(B_HEADS // 2):
            vb_o[0, c, p * PAIR_ROWS:(p + 1) * PAIR_ROWS, :] = jnp.concatenate(
                [vb[p * LANES:(p + 1) * LANES, c * KCHUNK:(c + 1) * KCHUNK], ones], axis=0)
    zb_o[0] = silu(seg("zb")).astype(_BF16)
    qi_o[0] = seg("qi").astype(_BF16)
    kw = seg("kw")
    wi_o[0] = kw[IDX_DIM:KW_REAL, :]
    row = lax.broadcasted_iota(jnp.int32, kw.shape, 0)
    ki_o[0] = jnp.where(row < IDX_DIM, kw, 0.0).T.astype(_BF16)
    g_lo = _SEG["gates"][0]
    for c in range(2 * D_MODEL // 512):
        gt_o[0, c * 512:(c + 1) * 512, :] = jax.nn.sigmoid(
            proj(g_lo + c * 512, g_lo + (c + 1) * 512)).astype(_BF16)


def _inproj(x, g, wt, gqa, gka, gqb, gkb):
    B, S, D = x.shape
    tm = TM_PROJ
    nt = S // tm
    const = lambda b, t: (0, 0)
    tr = lambda rows: pl.BlockSpec((1, rows, tm), lambda b, t: (b, 0, t))
    nat = lambda cols: pl.BlockSpec((1, tm, cols), lambda b, t: (b, t, 0))
    sds = jax.ShapeDtypeStruct
    out_shape = (
        sds((B, A_WIDTH, S), _BF16), sds((B, S, A_KV_WIDTH), _BF16), sds((B, A_KV_WIDTH, S), _BF16),
        sds((B, A_WIDTH, S), _BF16),
        sds((B, B_WIDTH, S), _BF16), sds((B, S, B_WIDTH), _BF16),
        sds((B, S // KCHUNK, V_ROWS, KCHUNK), _BF16), sds((B, B_WIDTH, S), _BF16),
        sds((B, IDX_WIDTH, S), _BF16), sds((B, S, LANES), _BF16), sds((B, IDX_HEADS, S), _F32),
        sds((B, 2 * D_MODEL, S), _BF16),
    )
    out_specs = (
        tr(A_WIDTH), nat(A_KV_WIDTH), tr(A_KV_WIDTH), tr(A_WIDTH),
        tr(B_WIDTH), nat(B_WIDTH),
        pl.BlockSpec((1, tm // KCHUNK, V_ROWS, KCHUNK), lambda b, t: (b, t, 0, 0)), tr(B_WIDTH),
        tr(IDX_WIDTH), nat(LANES), tr(IDX_HEADS), tr(2 * D_MODEL),
    )
    in_specs = [
        pl.BlockSpec((1, tm, D), lambda b, t: (b, t, 0)),
        pl.BlockSpec((1, D), const),
        pl.BlockSpec((PROJ_ROWS, D), const, pipeline_mode=pl.Buffered(1)),
        pl.BlockSpec((HEAD_DIM, tm), const), pl.BlockSpec((HEAD_DIM, tm), const),
        pl.BlockSpec((HEAD_DIM, tm), const), pl.BlockSpec((HEAD_DIM, tm), const),
    ]
    return pl.pallas_call(
        _inproj_kernel, grid=(B, nt), in_specs=in_specs, out_specs=out_specs, out_shape=out_shape,
        compiler_params=pltpu.CompilerParams(
            dimension_semantics=("parallel", "parallel"), vmem_limit_bytes=VMEM_LIMIT),
        name="inproj",
    )(x, g, wt, gqa, gka, gqb, gkb)


def _mix_a_kernel(q_ref, kp_ref, kc_ref, vp_ref, vc_ref, z_ref, bias_ref, sink_ref, o_ref):
    i = pl.program_id(1)
    zeros = jnp.zeros((HEAD_DIM, TQ), _BF16)
    units = [(j, g) for j in range(A_SUB) for g in range(A_KV_HEADS)]

    def bands(j):
        cur = slice(j * TQ, (j + 1) * TQ)
        old = slice((j - 1) * TQ, j * TQ)
        k_old = kp_ref[0] if j == 0 else kc_ref[0, old, :]
        v_old = vp_ref[0] if j == 0 else vc_ref[0, :, old]
        return (jnp.concatenate([k_old, kc_ref[0, cur, :]], axis=0),
                jnp.concatenate([v_old, vc_ref[0, :, cur]], axis=1))

    scores = []
    for j, g in units:
        cur = slice(j * TQ, (j + 1) * TQ)
        cols = []
        for hh in range(A_GROUPS):
            h = g * A_GROUPS + hh
            parts = [zeros] * A_KV_HEADS
            parts[g] = q_ref[0, h * HEAD_DIM:(h + 1) * HEAD_DIM, cur]
            cols.append(jnp.concatenate(parts, axis=0))
        rhs = jnp.concatenate(cols, axis=1)
        scores.append(jnp.dot(bands(j)[0], rhs, preferred_element_type=_F32))

    pts, invs = [], []
    for (j, g), sc in zip(units, scores):
        probs, inv = [], []
        for hh in range(A_GROUPS):
            h = g * A_GROUPS + hh
            s = sc[:, hh * TQ:(hh + 1) * TQ] + bias_ref[h]
            if j == 0:
                s = jnp.concatenate([jnp.where(i > 0, s[:TQ], NEG), s[TQ:]], axis=0)
            sink = sink_ref[h:h + 1, :]
            m = jnp.maximum(jnp.max(s, axis=0, keepdims=True), sink)
            p = jnp.exp(s - m)
            denom = jnp.sum(p, axis=0, keepdims=True) + jnp.exp(sink - m)
            probs.append(p.astype(_BF16))
            inv.append(1.0 / denom)
        pts.append(jnp.concatenate(probs, axis=1))
        invs.append(inv)

    for (j, g), pt, inv in zip(units, pts, invs):
        cur = slice(j * TQ, (j + 1) * TQ)
        out = jnp.dot(bands(j)[1][g * HEAD_DIM:(g + 1) * HEAD_DIM, :], pt,
                      preferred_element_type=_F32)
        for hh in range(A_GROUPS):
            h = g * A_GROUPS + hh
            rows = slice(h * HEAD_DIM, (h + 1) * HEAD_DIM)
            o = out[:, hh * TQ:(hh + 1) * TQ] * inv[hh] * z_ref[0, rows, cur].astype(_F32)
            o_ref[0, rows, cur] = o.astype(_BF16)


def _mix_a(qa_t, ka_n, va_t, za_t, bias_a, sink_a):
    B, _, S = qa_t.shape
    tqa = A_SUB * TQ
    prev = lambda i: jnp.maximum(i * A_SUB - 1, 0)
    in_specs = [
        pl.BlockSpec((1, A_WIDTH, tqa), lambda b, i: (b, 0, i)),
        pl.BlockSpec((1, TQ, A_KV_WIDTH), lambda b, i: (b, prev(i), 0)),
        pl.BlockSpec((1, tqa, A_KV_WIDTH), lambda b, i: (b, i, 0)),
        pl.BlockSpec((1, A_KV_WIDTH, TQ), lambda b, i: (b, 0, prev(i))),
        pl.BlockSpec((1, A_KV_WIDTH, tqa), lambda b, i: (b, 0, i)),
        pl.BlockSpec((1, A_WIDTH, tqa), lambda b, i: (b, 0, i)),
        pl.BlockSpec((A_Q_HEADS, 2 * TQ, TQ), lambda b, i: (0, 0, 0)),
        pl.BlockSpec((A_Q_HEADS, TQ), lambda b, i: (0, 0)),
    ]
    return pl.pallas_call(
        _mix_a_kernel, grid=(B, S // tqa), in_specs=in_specs,
        out_specs=pl.BlockSpec((1, A_WIDTH, tqa), lambda b, i: (b, 0, i)),
        out_shape=jax.ShapeDtypeStruct((B, A_WIDTH, S), _BF16),
        compiler_params=pltpu.CompilerParams(
            dimension_semantics=("parallel", "parallel"), vmem_limit_bytes=VMEM_LIMIT),
        name="mixer_a",
    )(qa_t, ka_n, ka_n, va_t, va_t, za_t, bias_a, sink_a)


def _key_to_f32(key):
    return pltpu.bitcast(jnp.where(key < 0, key ^ jnp.int32(0x7FFFFFFF), key), _F32)


def _f32_to_key(v):
    bits = pltpu.bitcast(v, jnp.int32)
    return jnp.where(bits < 0, bits ^ jnp.int32(0x7FFFFFFF), bits)


def _colmin8(v):
    return jnp.min(v.reshape(v.shape[0] // 8, 8, v.shape[1]), axis=0)


def _colmax8(v):
    return jnp.max(v.reshape(v.shape[0] // 8, 8, v.shape[1]), axis=0)


def _by_pairs(n, step):
    def body(t, carry):
        for u in range(4):
            step(4 * t + u)
        return carry

    lax.fori_loop(0, n // 4, body, 0)
    base = (n // 4) * 4

    @pl.when((n & 2) != 0)
    def _():
        step(base)
        step(base + 1)

    @pl.when((n & 1) != 0)
    def _():
        step(base + (n & 2))


def _mix_b_kernel(qb_ref, kb_ref, vb_ref, zb_ref, qi_ref, ki_ref, wi_ref, bias_ref, o_ref,
                  keys_scr, mb_scr, sc_scr, acc_scr, m_scr, ext_scr, thr_scr, tie_scr):
    i = pl.program_id(1)
    nc = ((i + 1) * TQB + KCHUNK - 1) // KCHUNK
    ns = (nc + 1) // 2
    top_k = TOPK_MAX

    qi = qi_ref[0]
    zpad = jnp.zeros((LANES - IDX_DIM, TQB), _BF16)
    rhs_i = jnp.concatenate(
        [jnp.concatenate([qi[h * IDX_DIM:(h + 1) * IDX_DIM], zpad], axis=0)
         for h in range(IDX_HEADS)], axis=1)
    w = wi_ref[0] * (IDX_DIM ** -0.5 * IDX_HEADS ** -0.5)
    s_minus_t = (lax.broadcasted_iota(jnp.int32, (KCHUNK, TQB), 0)
                 - lax.broadcasted_iota(jnp.int32, (KCHUNK, TQB), 1))

    def index_step(c):
        r0 = pl.multiple_of(c * KCHUNK, KCHUNK)
        d = jnp.dot(ki_ref[0, pl.ds(r0, KCHUNK), :], rhs_i,
                    preferred_element_type=_F32)
        acc = w[0:1, :] * jnp.maximum(d[:, 0:TQB], 0.0)
        for h in range(1, IDX_HEADS):
            acc = acc + w[h:h + 1, :] * jnp.maximum(d[:, h * TQB:(h + 1) * TQB], 0.0)
        causal = s_minus_t <= (i * TQB - r0)
        score = jnp.where(causal, acc, -jnp.inf)
        keys_scr[pl.ds(r0, KCHUNK), :] = score
        ext_scr[0:8, :] = jnp.maximum(ext_scr[0:8, :], _colmax8(score))
        ext_scr[8:16, :] = jnp.minimum(ext_scr[8:16, :], _colmin8(jnp.where(causal, acc, jnp.inf)))

    ext_scr[0:8, :] = jnp.full((8, TQB), -jnp.inf, _F32)
    ext_scr[8:16, :] = jnp.full((8, TQB), jnp.inf, _F32)
    _by_pairs(nc, index_step)

    @pl.when(nc % 2 == 1)
    def _():
        keys_scr[pl.ds(pl.multiple_of(nc * KCHUNK, KCHUNK), KCHUNK), :] = jnp.full(
            (KCHUNK, TQB), -jnp.inf, _F32)

    def count_rows(src, hit):
        n_acc = 8

        def bump(accs, r0, n_rows):
            accs = list(accs)
            rows = src[pl.ds(r0, n_rows), :]
            for j in range(n_rows // 8):
                a = accs[j % n_acc]
                accs[j % n_acc] = jnp.where(hit(rows[j * 8:(j + 1) * 8]), a + 1, a)
            return tuple(accs)

        accs = lax.fori_loop(
            0, nc // 2, lambda c, accs: bump(accs, pl.multiple_of(c * KSUPER, KSUPER), KSUPER),
            tuple(jnp.zeros((8, TQB), jnp.int32) for _ in range(n_acc)))
        accs = lax.cond(nc % 2 == 1,
                        lambda accs: bump(accs, pl.multiple_of((nc - 1) * KCHUNK, KCHUNK), KCHUNK),
                        lambda accs: accs, accs)
        acc = accs[0]
        for a in accs[1:]:
            acc = acc + a
        return jnp.sum(acc, axis=0, keepdims=True)

    def count_ge(cand, strict=False):
        return count_rows(keys_scr, (lambda r: r > cand) if strict else (lambda r: r >= cand))

    @pl.when(i * TQB + TQB <= top_k)
    def _():
        thr_scr[...] = jnp.full((1, TQB), -jnp.inf, _F32)
        tie_scr[...] = jnp.full((1, TQB), -1, jnp.int32)

    @pl.when(i * TQB + TQB > top_k)
    def _():
        zero = jnp.zeros((1, TQB), _F32)
        cnt_nonneg = count_ge(zero)
        cnt_pos = count_ge(zero, strict=True)
        key_max = _f32_to_key(jnp.max(ext_scr[0:8, :], axis=0, keepdims=True))
        key_min = _f32_to_key(jnp.min(ext_scr[8:16, :], axis=0, keepdims=True))
        n_causal = i * TQB + 1 + lax.broadcasted_iota(jnp.int32, (1, TQB), 1)
        nonneg = cnt_nonneg >= top_k
        lo = jnp.where(nonneg, 0, key_min)
        cnt_lo = jnp.where(nonneg, cnt_nonneg, n_causal)
        hi = jnp.where(nonneg, jnp.where(cnt_pos < top_k, 1, key_max + 1), 0)

        def bisect(by_value):
            def body(_, state):
                lo, hi, cnt_lo = state
                span = hi - lo
                mid = lo + lax.shift_right_logical(span, 1)
                if by_value:
                    lo_v, hi_v = _key_to_f32(lo), _key_to_f32(hi)
                    mid_v = _f32_to_key(lo_v + (hi_v - lo_v) * 0.5)
                    mid = jnp.where(span >= 2, jnp.minimum(jnp.maximum(mid_v, lo + 1), hi - 1), lo)
                cnt = count_ge(_key_to_f32(mid))
                ok = cnt >= top_k
                return jnp.where(ok, mid, lo), jnp.where(ok, hi, mid), jnp.where(ok, cnt, cnt_lo)
            return body

        def finished(state):
            lo, hi, cnt_lo = state
            done = jnp.where(cnt_lo == top_k, 1, jnp.where(hi - lo <= 1, 1, 0))
            return jnp.min(done) == 1

        state = lax.fori_loop(0, VALUE_PASSES, bisect(True), (lo, hi, cnt_lo))
        stops = (0,) + SEARCH_CHECKS + (31,)
        state = lax.fori_loop(stops[0], stops[1], bisect(False), state)
        for first, last in zip(stops[1:-1], stops[2:]):
            state = lax.cond(finished(state), lambda s: s,
                             functools.partial(lax.fori_loop, first, last, bisect(False)), state)
        lo, _, cnt_ge = state
        thr = _key_to_f32(lo)
        thr_scr[...] = thr
        tie_scr[...] = jnp.full((1, TQB), 2 ** 30, jnp.int32)
        row = lax.broadcasted_iota(jnp.int32, (KSUPER, TQB), 0)

        @pl.when(jnp.max(cnt_ge) > top_k)
        def _():
            need = top_k - count_ge(thr, strict=True)
            row_f = row.astype(_F32)

            def plane_body(c, carry):
                r0 = pl.multiple_of(c * KSUPER, KSUPER)
                mb_scr[pl.ds(r0, KSUPER), :] = jnp.where(
                    keys_scr[pl.ds(r0, KSUPER), :] == thr, row_f + r0.astype(_F32), jnp.inf)
                return carry
            lax.fori_loop(0, ns, plane_body, 0)

            tie_bits = (keys_scr.shape[0] - 1).bit_length()

            def tie_body(b, lo):
                cand = lo | lax.shift_left(jnp.int32(1), tie_bits - 1 - b)
                cnt = count_rows(mb_scr, lambda r: r < cand.astype(_F32))
                return jnp.where(cnt < need, cand, lo)
            tie_scr[...] = lax.fori_loop(0, tie_bits, tie_body, jnp.zeros((1, TQB), jnp.int32))

    thr = thr_scr[...]
    tie = tie_scr[...]
    row = lax.broadcasted_iota(jnp.int32, (KSUPER, TQB), 0)

    def mask_body(c, carry):
        r0 = pl.multiple_of(c * KSUPER, KSUPER)
        k = keys_scr[pl.ds(r0, KSUPER), :]
        mb_scr[pl.ds(r0, KSUPER), :] = jnp.where(
            k == thr, jnp.where(row <= tie - r0, 0.0, NEG), jnp.where(k > thr, 0.0, NEG))
        return carry

    lax.fori_loop(0, ns, mask_body, 0)

    zeros = jnp.zeros((HEAD_DIM, TQB), _BF16)
    n_pairs = B_HEADS // 2
    rhs = []
    for p in range(n_pairs):
        q0 = qb_ref[0, (2 * p) * HEAD_DIM:(2 * p + 1) * HEAD_DIM, :]
        q1 = qb_ref[0, (2 * p + 1) * HEAD_DIM:(2 * p + 2) * HEAD_DIM, :]
        rhs.append(jnp.concatenate([jnp.concatenate([q0, zeros], axis=0),
                                    jnp.concatenate([zeros, q1], axis=0)], axis=1))

    def score_step(c):
        r0 = pl.multiple_of(c * KCHUNK, KCHUNK)
        for p in range(n_pairs):
            s = jnp.dot(kb_ref[0, pl.ds(r0, KCHUNK), p * LANES:(p + 1) * LANES], rhs[p],
                        preferred_element_type=_F32)
            for hh in range(2):
                h = 2 * p + hh
                parts = []
                for half in range(KCHUNK // TQ):
                    jb = c * (KCHUNK // TQ) + half
                    rs = pl.multiple_of(r0 + half * TQ, TQ)
                    subs = []
                    for sub in range(TQB // TQ):
                        kind = jnp.clip(jb - (i * (TQB // TQ) + sub) + 2, 0, 2)
                        lanes = slice(sub * TQ, (sub + 1) * TQ)
                        subs.append(s[half * TQ:(half + 1) * TQ, hh * TQB + sub * TQ:hh * TQB + (sub + 1) * TQ]
                                    + bias_ref[h, kind] + mb_scr[pl.ds(rs, TQ), lanes])
                    parts.append(jnp.concatenate(subs, axis=1))
                sh = jnp.concatenate(parts, axis=0)
                sc_scr[pl.ds(r0, KCHUNK), h * TQB:(h + 1) * TQB] = sh
                m_scr[h * 8:(h + 1) * 8, :] = jnp.maximum(m_scr[h * 8:(h + 1) * 8, :], _colmax8(sh))

    m_scr[...] = jnp.full(m_scr.shape, NEG, _F32)
    _by_pairs(nc, score_step)
    m_row = jnp.concatenate(
        [jnp.max(m_scr[h * 8:(h + 1) * 8, :], axis=0, keepdims=True) for h in range(B_HEADS)],
        axis=1)
    acc_scr[...] = jnp.zeros_like(acc_scr)

    def pv_step(c):
        r0 = pl.multiple_of(c * KCHUNK, KCHUNK)
        for p in range(n_pairs):
            cols = slice(2 * p * TQB, (2 * p + 2) * TQB)
            pr = jnp.exp2(sc_scr[pl.ds(r0, KCHUNK), cols] - m_row[:, cols]).astype(_BF16)
            acc_scr[p] += jnp.dot(vb_ref[0, c, p * PAIR_ROWS:(p + 1) * PAIR_ROWS, :], pr,
                                  preferred_element_type=_F32)

    _by_pairs(nc, pv_step)
    for p in range(n_pairs):
        inv = 1.0 / acc_scr[p, 2 * HEAD_DIM:2 * HEAD_DIM + 1, :]
        for hh in range(2):
            rows = slice((2 * p + hh) * HEAD_DIM, (2 * p + hh + 1) * HEAD_DIM)
            o = (acc_scr[p, hh * HEAD_DIM:(hh + 1) * HEAD_DIM, hh * TQB:(hh + 1) * TQB]
                 * inv[:, hh * TQB:(hh + 1) * TQB] * zb_ref[0, rows, :].astype(_F32))
            o_ref[0, rows, :] = o.astype(_BF16)


def _mix_b(qb_t, kb_n, vb_c, zb_t, qi_t, ki_n, wi_t, bias_b):
    B, _, S = qb_t.shape
    nq = S // TQB
    per_q = lambda rows: pl.BlockSpec((1, rows, TQB), lambda b, i: (b, 0, i))
    in_specs = [
        per_q(B_WIDTH),
        pl.BlockSpec((1, S, B_WIDTH), lambda b, i: (b, 0, 0)),
        pl.BlockSpec((1, S // KCHUNK, V_ROWS, KCHUNK), lambda b, i: (b, 0, 0, 0)),
        per_q(B_WIDTH),
        per_q(IDX_WIDTH),
        pl.BlockSpec((1, S, LANES), lambda b, i: (b, 0, 0)),
        per_q(IDX_HEADS),
        pl.BlockSpec((B_HEADS, 3, TQ, TQ), lambda b, i: (0, 0, 0, 0)),
    ]
    scratch = [
        pltpu.VMEM((S, TQB), _F32),
        pltpu.VMEM((S, TQB), _F32),
        pltpu.VMEM((S, B_HEADS * TQB), _F32),
        pltpu.VMEM((B_HEADS // 2, PAIR_ROWS, 2 * TQB), _F32),
        pltpu.VMEM((B_HEADS * 8, TQB), _F32),
        pltpu.VMEM((16, TQB), _F32),
        pltpu.VMEM((1, TQB), _F32),
        pltpu.VMEM((1, TQB), jnp.int32),
    ]
    return pl.pallas_call(
        _mix_b_kernel, grid=(B, nq), in_specs=in_specs,
        out_specs=per_q(B_WIDTH),
        out_shape=jax.ShapeDtypeStruct((B, B_WIDTH, S), _BF16),
        scratch_shapes=scratch,
        compiler_params=pltpu.CompilerParams(
            dimension_semantics=("parallel", "arbitrary"), vmem_limit_bytes=VMEM_LIMIT),
        name="mixer_b",
    )(qb_t, kb_n, vb_c, zb_t, qi_t, ki_n, wi_t, bias_b)


def _merge_kernel(x_ref, ya_ref, yb_ref, gt_ref, wpa_ref, wpb_ref, wo_ref, o_ref):
    pa = jnp.dot(wpa_ref[...], ya_ref[0], preferred_element_type=_F32)
    pb = jnp.dot(wpb_ref[...], yb_ref[0], preferred_element_type=_F32)
    merged = (gt_ref[0, :D_MODEL, :].astype(_F32) * pa
              + gt_ref[0, D_MODEL:, :].astype(_F32) * pb).astype(_BF16)
    out_t = jnp.dot(wo_ref[...], merged, preferred_element_type=_F32)
    o_ref[0] = x_ref[0] + out_t.T


def _merge(x, ya_t, yb_t, gt_t, wpa_t, wpb_t, wo_t):
    B, S, D = x.shape
    tm = TM_PROJ
    const = lambda b, t: (0, 0)
    in_specs = [
        pl.BlockSpec((1, tm, D), lambda b, t: (b, t, 0)),
        pl.BlockSpec((1, A_WIDTH, tm), lambda b, t: (b, 0, t)),
        pl.BlockSpec((1, B_WIDTH, tm), lambda b, t: (b, 0, t)),
        pl.BlockSpec((1, 2 * D, tm), lambda b, t: (b, 0, t)),
        pl.BlockSpec((D, A_WIDTH), const),
        pl.BlockSpec((D, B_WIDTH), const),
        pl.BlockSpec((D, D), const),
    ]
    return pl.pallas_call(
        _merge_kernel, grid=(B, S // tm), in_specs=in_specs,
        out_specs=pl.BlockSpec((1, tm, D), lambda b, t: (b, t, 0)),
        out_shape=jax.ShapeDtypeStruct((B, S, D), x.dtype),
        compiler_params=pltpu.CompilerParams(
            dimension_semantics=("parallel", "parallel"), vmem_limit_bytes=VMEM_LIMIT),
        name="merge_out",
    )(x, ya_t, yb_t, gt_t, wpa_t, wpb_t, wo_t)


def _t5_bucket(n):
    n = np.maximum(n, 0)
    max_exact = N_BUCKETS // 2
    nf = np.maximum(n, 1).astype(np.float64)
    large = max_exact + np.floor(np.log(nf / max_exact) / math.log(MAX_DISTANCE / max_exact)
                                 * (N_BUCKETS - max_exact)).astype(np.int64)
    large = np.minimum(large, N_BUCKETS - 1)
    return np.where(n < max_exact, n, large)


def _bias_tables(rel_bias):
    t = np.arange(TQ)[None, :]
    s = np.arange(TQ)[:, None]
    d_prev = t + TQ - s
    d_cur = t - s
    table_a = rel_bias[:, :A_Q_HEADS].astype(_F32).T
    table_b = rel_bias[:, A_Q_HEADS:].astype(_F32).T

    def tile(table, dist, windowed):
        onehot = (_t5_bucket(dist)[..., None] == np.arange(N_BUCKETS)).astype(np.float32)
        b = jnp.einsum("stk,hk->hst", onehot, table, precision=lax.Precision.HIGHEST)
        if windowed:
            b = jnp.where(((dist >= 0) & (dist < WINDOW))[None], b, NEG)
        return b

    bias_a = jnp.concatenate([tile(table_a, d_prev, True), tile(table_a, d_cur, True)], axis=1)
    far = jnp.broadcast_to(table_b[:, N_BUCKETS - 1][:, None, None], (B_HEADS, TQ, TQ))
    bias_b = jnp.stack([far, tile(table_b, d_prev, False), tile(table_b, d_cur, False)], axis=1) * LOG2E
    return bias_a, bias_b


def kernel(x, norm_g, w_in, qnorm_a, knorm_a, sinks_a, qnorm_b, knorm_b, rel_bias,
           w_proj_a, w_proj_b, w_out):
    assert norm_g.shape[0] == 1, "single-layer block"
    B, S, D = x.shape
    assert D == D_MODEL and S % TM_PROJ == 0 and S % KSUPER == 0 and S % TQB == 0 and TQB % TQ == 0
    assert MAX_DISTANCE <= TQ

    wt = w_in[0].T
    n_real = _SEG["kw"][0] + KW_REAL
    wt = jnp.concatenate(
        [wt[:n_real], jnp.zeros((_SEG["kw"][1] - n_real, D), wt.dtype), wt[n_real:]], axis=0)
    wt = wt.astype(_BF16)
    bcast = lambda g, scale: jnp.broadcast_to((g.astype(_F32) * scale)[:, None], (HEAD_DIM, TM_PROJ))
    q_scale = HEAD_DIM ** -0.5
    (qa_t, ka_n, va_t, za_t, qb_t, kb_n, vb_c, zb_t, qi_t, ki_n, wi_t, gt_t) = _inproj(
        x, norm_g[0][None, :].astype(_F32), wt,
        bcast(qnorm_a[0], q_scale), bcast(knorm_a[0], 1.0),
        bcast(qnorm_b[0], q_scale * LOG2E), bcast(knorm_b[0], 1.0))

    bias_a, bias_b = _bias_tables(rel_bias)
    sink_a = jnp.broadcast_to(sinks_a[0].astype(_F32)[:, None], (A_Q_HEADS, TQ))
    ya_t = _mix_a(qa_t, ka_n, va_t, za_t, bias_a, sink_a)
    yb_t = _mix_b(qb_t, kb_n, vb_c, zb_t, qi_t, ki_n, wi_t, bias_b)
    return _merge(x, ya_t, yb_t, gt_t,
                  w_proj_a[0].T.astype(_BF16), w_proj_b[0].T.astype(_BF16), w_out[0].T.astype(_BF16))
```

```python
import functools
import math

import jax
import jax.numpy as jnp
import numpy as np
from jax import lax
from jax.experimental import pallas as pl
from jax.experimental.pallas import tpu as pltpu

D_MODEL = 1024
HEAD_DIM = 64
A_Q_HEADS = 8
A_KV_HEADS = 2
A_GROUPS = A_Q_HEADS // A_KV_HEADS
A_WIDTH = A_Q_HEADS * HEAD_DIM
A_KV_WIDTH = A_KV_HEADS * HEAD_DIM
B_HEADS = 8
B_WIDTH = B_HEADS * HEAD_DIM
IDX_HEADS = 8
IDX_DIM = 32
IDX_WIDTH = IDX_HEADS * IDX_DIM
WINDOW = 128
TOPK_MAX = 256
N_BUCKETS = 32
MAX_DISTANCE = 128
RMS_EPS = 1e-6

LANES = 128
TQ = 128
A_SUB = 4
TQB = 256
KCHUNK = 256
KSUPER = 2 * KCHUNK
VALUE_PASSES = 16
SEARCH_CHECKS = (1, 3, 5, 8)
TM_PROJ = 1024
NEG = -1e30
LOG2E = math.log2(math.e)
ONES_ROWS = 16
PAIR_ROWS = 2 * HEAD_DIM + ONES_ROWS
V_ROWS = (B_HEADS // 2) * PAIR_ROWS
INT_MIN = -(2 ** 31)
VMEM_LIMIT = 56 * 1024 * 1024

_SEG = {}
_off = 0
for _name, _rows in (("qa", A_WIDTH), ("ka", A_KV_WIDTH), ("va", A_KV_WIDTH), ("za", A_WIDTH),
                     ("qb", B_WIDTH), ("kb", B_WIDTH), ("vb", B_WIDTH), ("zb", B_WIDTH),
                     ("qi", IDX_WIDTH), ("kw", LANES), ("gates", 2 * D_MODEL)):
    _SEG[_name] = (_off, _off + _rows)
    _off += _rows
PROJ_ROWS = _off
KW_REAL = IDX_DIM + IDX_HEADS

_F32 = jnp.float32
_BF16 = jnp.bfloat16
_NT = (((1,), (1,)), ((), ()))


def _inproj_kernel(x_ref, g_ref, wt_ref, gqa_ref, gka_ref, gqb_ref, gkb_ref,
                   qa_o, ka_o, va_o, za_o, qb_o, kb_o, vb_o, zb_o, qi_o, ki_o, wi_o, gt_o):
    tm = x_ref.shape[1]
    x = x_ref[0]
    ms = jnp.mean(x * x, axis=-1, keepdims=True)
    h = (x * lax.rsqrt(ms + RMS_EPS) * g_ref[...]).astype(_BF16)

    def proj(lo, hi):
        return lax.dot_general(wt_ref[lo:hi, :], h, _NT, preferred_element_type=_F32)

    def seg(name):
        return proj(*_SEG[name])

    def headnorm(p, gain_ref):
        nh = p.shape[0] // HEAD_DIM
        p3 = p.reshape(nh, HEAD_DIM, tm)
        r = lax.rsqrt(jnp.mean(p3 * p3, axis=1, keepdims=True) + RMS_EPS)
        return (p3 * r * gain_ref[...][None]).reshape(nh * HEAD_DIM, tm)

    def silu(p):
        return p * jax.nn.sigmoid(p)

    qa_o[0] = headnorm(seg("qa"), gqa_ref).astype(_BF16)
    ka_o[0] = headnorm(seg("ka"), gka_ref).T.astype(_BF16)
    va_o[0] = seg("va").astype(_BF16)
    za_o[0] = silu(seg("za")).astype(_BF16)
    qb_o[0] = headnorm(seg("qb"), gqb_ref).astype(_BF16)
    kb_o[0] = headnorm(seg("kb"), gkb_ref).T.astype(_BF16)
    vb = seg("vb").astype(_BF16)
    ones = jnp.ones((ONES_ROWS, KCHUNK), _BF16)
    for c in range(tm // KCHUNK):
        for p in range(B_HEADS // 2):
            vb_o[0, c, p * PAIR_ROWS:(p + 1) * PAIR_ROWS, :] = jnp.concatenate(
                [vb[p * LANES:(p + 1) * LANES, c * KCHUNK:(c + 1) * KCHUNK], ones], axis=0)
    zb_o[0] = silu(seg("zb")).astype(_BF16)
    qi_o[0] = seg("qi").astype(_BF16)
    kw = seg("kw")
    wi_o[0] = kw[IDX_DIM:KW_REAL, :]
    row = lax.broadcasted_iota(jnp.int32, kw.shape, 0)
    ki_o[0] = jnp.where(row < IDX_DIM, kw, 0.0).T.astype(_BF16)
    g_lo = _SEG["gates"][0]
    for c in range(2 * D_MODEL // 512):
        gt_o[0, c * 512:(c + 1) * 512, :] = jax.nn.sigmoid(
            proj(g_lo + c * 512, g_lo + (c + 1) * 512)).astype(_BF16)


def _inproj(x, g, wt, gqa, gka, gqb, gkb):
    B, S, D = x.shape
    tm = TM_PROJ
    nt = S // tm
    const = lambda b, t: (0, 0)
    tr = lambda rows: pl.BlockSpec((1, rows, tm), lambda b, t: (b, 0, t))
    nat = lambda cols: pl.BlockSpec((1, tm, cols), lambda b, t: (b, t, 0))
    sds = jax.ShapeDtypeStruct
    out_shape = (
        sds((B, A_WIDTH, S), _BF16), sds((B, S, A_KV_WIDTH), _BF16), sds((B, A_KV_WIDTH, S), _BF16),
        sds((B, A_WIDTH, S), _BF16),
        sds((B, B_WIDTH, S), _BF16), sds((B, S, B_WIDTH), _BF16),
        sds((B, S // KCHUNK, V_ROWS, KCHUNK), _BF16), sds((B, B_WIDTH, S), _BF16),
        sds((B, IDX_WIDTH, S), _BF16), sds((B, S, LANES), _BF16), sds((B, IDX_HEADS, S), _F32),
        sds((B, 2 * D_MODEL, S), _BF16),
    )
    out_specs = (
        tr(A_WIDTH), nat(A_KV_WIDTH), tr(A_KV_WIDTH), tr(A_WIDTH),
        tr(B_WIDTH), nat(B_WIDTH),
        pl.BlockSpec((1, tm // KCHUNK, V_ROWS, KCHUNK), lambda b, t: (b, t, 0, 0)), tr(B_WIDTH),
        tr(IDX_WIDTH), nat(LANES), tr(IDX_HEADS), tr(2 * D_MODEL),
    )
    in_specs = [
        pl.BlockSpec((1, tm, D), lambda b, t: (b, t, 0)),
        pl.BlockSpec((1, D), const),
        pl.BlockSpec((PROJ_ROWS, D), const, pipeline_mode=pl.Buffered(1)),
        pl.BlockSpec((HEAD_DIM, tm), const), pl.BlockSpec((HEAD_DIM, tm), const),
        pl.BlockSpec((HEAD_DIM, tm), const), pl.BlockSpec((HEAD_DIM, tm), const),
    ]
    return pl.pallas_call(
        _inproj_kernel, grid=(B, nt), in_specs=in_specs, out_specs=out_specs, out_shape=out_shape,
        compiler_params=pltpu.CompilerParams(
            dimension_semantics=("parallel", "parallel"), vmem_limit_bytes=VMEM_LIMIT),
        name="inproj",
    )(x, g, wt, gqa, gka, gqb, gkb)


def _mix_a_kernel(q_ref, kp_ref, kc_ref, vp_ref, vc_ref, z_ref, bias_ref, sink_ref, o_ref):
    i = pl.program_id(1)
    zeros = jnp.zeros((HEAD_DIM, TQ), _BF16)
    units = [(j, g) for j in range(A_SUB) for g in range(A_KV_HEADS)]

    def bands(j):
        cur = slice(j * TQ, (j + 1) * TQ)
        old = slice((j - 1) * TQ, j * TQ)
        k_old = kp_ref[0] if j == 0 else kc_ref[0, old, :]
        v_old = vp_ref[0] if j == 0 else vc_ref[0, :, old]
        return (jnp.concatenate([k_old, kc_ref[0, cur, :]], axis=0),
                jnp.concatenate([v_old, vc_ref[0, :, cur]], axis=1))

    scores = []
    for j, g in units:
        cur = slice(j * TQ, (j + 1) * TQ)
        cols = []
        for hh in range(A_GROUPS):
            h = g * A_GROUPS + hh
            parts = [zeros] * A_KV_HEADS
            parts[g] = q_ref[0, h * HEAD_DIM:(h + 1) * HEAD_DIM, cur]
            cols.append(jnp.concatenate(parts, axis=0))
        rhs = jnp.concatenate(cols, axis=1)
        scores.append(jnp.dot(bands(j)[0], rhs, preferred_element_type=_F32))

    pts, invs = [], []
    for (j, g), sc in zip(units, scores):
        probs, inv = [], []
        for hh in range(A_GROUPS):
            h = g * A_GROUPS + hh
            s = sc[:, hh * TQ:(hh + 1) * TQ] + bias_ref[h]
            if j == 0:
                s = jnp.concatenate([jnp.where(i > 0, s[:TQ], NEG), s[TQ:]], axis=0)
            sink = sink_ref[h:h + 1, :]
            m = jnp.maximum(jnp.max(s, axis=0, keepdims=True), sink)
            p = jnp.exp(s - m)
            denom = jnp.sum(p, axis=0, keepdims=True) + jnp.exp(sink - m)
            probs.append(p.astype(_BF16))
            inv.append(1.0 / denom)
        pts.append(jnp.concatenate(probs, axis=1))
        invs.append(inv)

    for (j, g), pt, inv in zip(units, pts, invs):
        cur = slice(j * TQ, (j + 1) * TQ)
        out = jnp.dot(bands(j)[1][g * HEAD_DIM:(g + 1) * HEAD_DIM, :], pt,
                      preferred_element_type=_F32)
        for hh in range(A_GROUPS):
            h = g * A_GROUPS + hh
            rows = slice(h * HEAD_DIM, (h + 1) * HEAD_DIM)
            o = out[:, hh * TQ:(hh + 1) * TQ] * inv[hh] * z_ref[0, rows, cur].astype(_F32)
            o_ref[0, rows, cur] = o.astype(_BF16)


def _mix_a(qa_t, ka_n, va_t, za_t, bias_a, sink_a):
    B, _, S = qa_t.shape
    tqa = A_SUB * TQ
    prev = lambda i: jnp.maximum(i * A_SUB - 1, 0)
    in_specs = [
        pl.BlockSpec((1, A_WIDTH, tqa), lambda b, i: (b, 0, i)),
        pl.BlockSpec((1, TQ, A_KV_WIDTH), lambda b, i: (b, prev(i), 0)),
        pl.BlockSpec((1, tqa, A_KV_WIDTH), lambda b, i: (b, i, 0)),
        pl.BlockSpec((1, A_KV_WIDTH, TQ), lambda b, i: (b, 0, prev(i))),
        pl.BlockSpec((1, A_KV_WIDTH, tqa), lambda b, i: (b, 0, i)),
        pl.BlockSpec((1, A_WIDTH, tqa), lambda b, i: (b, 0, i)),
        pl.BlockSpec((A_Q_HEADS, 2 * TQ, TQ), lambda b, i: (0, 0, 0)),
        pl.BlockSpec((A_Q_HEADS, TQ), lambda b, i: (0, 0)),
    ]
    return pl.pallas_call(
        _mix_a_kernel, grid=(B, S // tqa), in_specs=in_specs,
        out_specs=pl.BlockSpec((1, A_WIDTH, tqa), lambda b, i: (b, 0, i)),
        out_shape=jax.ShapeDtypeStruct((B, A_WIDTH, S), _BF16),
        compiler_params=pltpu.CompilerParams(
            dimension_semantics=("parallel", "parallel"), vmem_limit_bytes=VMEM_LIMIT),
        name="mixer_a",
    )(qa_t, ka_n, ka_n, va_t, va_t, za_t, bias_a, sink_a)


def _key_to_f32(key):
    return pltpu.bitcast(jnp.where(key < 0, key ^ jnp.int32(0x7FFFFFFF), key), _F32)


def _f32_to_key(v):
    bits = pltpu.bitcast(v, jnp.int32)
    return jnp.where(bits < 0, bits ^ jnp.int32(0x7FFFFFFF), bits)


def _colmin8(v):
    return jnp.min(v.reshape(v.shape[0] // 8, 8, v.shape[1]), axis=0)


def _colmax8(v):
    return jnp.max(v.reshape(v.shape[0] // 8, 8, v.shape[1]), axis=0)


def _by_pairs(n, step):
    def body(t, carry):
        for u in range(4):
            step(4 * t + u)
        return carry

    lax.fori_loop(0, n // 4, body, 0)
    base = (n // 4) * 4

    @pl.when((n & 2) != 0)
    def _():
        step(base)
        step(base + 1)

    @pl.when((n & 1) != 0)
    def _():
        step(base + (n & 2))


def _mix_b_kernel(qb_ref, kb_ref, vb_ref, zb_ref, qi_ref, ki_ref, wi_ref, bias_ref, o_ref,
                  keys_scr, mb_scr, sc_scr, acc_scr, m_scr, ext_scr, thr_scr, tie_scr):
    i = pl.program_id(1)
    nc = ((i + 1) * TQB + KCHUNK - 1) // KCHUNK
    ns = (nc + 1) // 2
    top_k = TOPK_MAX

    qi = qi_ref[0]
    zpad = jnp.zeros((LANES - IDX_DIM, TQB), _BF16)
    rhs_i = jnp.concatenate(
        [jnp.concatenate([qi[h * IDX_DIM:(h + 1) * IDX_DIM], zpad], axis=0)
         for h in range(IDX_HEADS)], axis=1)
    w = wi_ref[0] * (IDX_DIM ** -0.5 * IDX_HEADS ** -0.5)
    s_minus_t = (lax.broadcasted_iota(jnp.int32, (KCHUNK, TQB), 0)
                 - lax.broadcasted_iota(jnp.int32, (KCHUNK, TQB), 1))

    def index_step(c):
        r0 = pl.multiple_of(c * KCHUNK, KCHUNK)
        d = jnp.dot(ki_ref[0, pl.ds(r0, KCHUNK), :], rhs_i,
                    preferred_element_type=_F32)
        acc = w[0:1, :] * jnp.maximum(d[:, 0:TQB], 0.0)
        for h in range(1, IDX_HEADS):
            acc = acc + w[h:h + 1, :] * jnp.maximum(d[:, h * TQB:(h + 1) * TQB], 0.0)
        causal = s_minus_t <= (i * TQB - r0)
        score = jnp.where(causal, acc, -jnp.inf)
        keys_scr[pl.ds(r0, KCHUNK), :] = score
        ext_scr[0:8, :] = jnp.maximum(ext_scr[0:8, :], _colmax8(score))
        ext_scr[8:16, :] = jnp.minimum(ext_scr[8:16, :], _colmin8(jnp.where(causal, acc, jnp.inf)))

    ext_scr[0:8, :] = jnp.full((8, TQB), -jnp.inf, _F32)
    ext_scr[8:16, :] = jnp.full((8, TQB), jnp.inf, _F32)
    _by_pairs(nc, index_step)

    @pl.when(nc % 2 == 1)
    def _():
        keys_scr[pl.ds(pl.multiple_of(nc * KCHUNK, KCHUNK), KCHUNK), :] = jnp.full(
            (KCHUNK, TQB), -jnp.inf, _F32)

    def count_rows(src, hit):
        n_acc = 8

        def bump(accs, r0, n_rows):
            accs = list(accs)
            rows = src[pl.ds(r0, n_rows), :]
            for j in range(n_rows // 8):
                a = accs[j % n_acc]
                accs[j % n_acc] = jnp.where(hit(rows[j * 8:(j + 1) * 8]), a + 1, a)
            return tuple(accs)

        accs = lax.fori_loop(
            0, nc // 2, lambda c, accs: bump(accs, pl.multiple_of(c * KSUPER, KSUPER), KSUPER),
            tuple(jnp.zeros((8, TQB), jnp.int32) for _ in range(n_acc)))
        accs = lax.cond(nc % 2 == 1,
                        lambda accs: bump(accs, pl.multiple_of((nc - 1) * KCHUNK, KCHUNK), KCHUNK),
                        lambda accs: accs, accs)
        acc = accs[0]
        for a in accs[1:]:
            acc = acc + a
        return jnp.sum(acc, axis=0, keepdims=True)

    def count_ge(cand, strict=False):
        return count_rows(keys_scr, (lambda r: r > cand) if strict else (lambda r: r >= cand))

    @pl.when(i * TQB + TQB <= top_k)
    def _():
        thr_scr[...] = jnp.full((1, TQB), -jnp.inf, _F32)
        tie_scr[...] = jnp.full((1, TQB), -1, jnp.int32)

    @pl.when(i * TQB + TQB > top_k)
    def _():
        zero = jnp.zeros((1, TQB), _F32)
        cnt_nonneg = count_ge(zero)
        cnt_pos = count_ge(zero, strict=True)
        key_max = _f32_to_key(jnp.max(ext_scr[0:8, :], axis=0, keepdims=True))
        key_min = _f32_to_key(jnp.min(ext_scr[8:16, :], axis=0, keepdims=True))
        n_causal = i * TQB + 1 + lax.broadcasted_iota(jnp.int32, (1, TQB), 1)
        nonneg = cnt_nonneg >= top_k
        lo = jnp.where(nonneg, 0, key_min)
        cnt_lo = jnp.where(nonneg, cnt_nonneg, n_causal)
        hi = jnp.where(nonneg, jnp.where(cnt_pos < top_k, 1, key_max + 1), 0)

        def bisect(by_value):
            def body(_, state):
                lo, hi, cnt_lo = state
                span = hi - lo
                mid = lo + lax.shift_right_logical(span, 1)
                if by_value:
                    lo_v, hi_v = _key_to_f32(lo), _key_to_f32(hi)
                    mid_v = _f32_to_key(lo_v + (hi_v - lo_v) * 0.5)
                    mid = jnp.where(span >= 2, jnp.minimum(jnp.maximum(mid_v, lo + 1), hi - 1), lo)
                cnt = count_ge(_key_to_f32(mid))
                ok = cnt >= top_k
                return jnp.where(ok, mid, lo), jnp.where(ok, hi, mid), jnp.where(ok, cnt, cnt_lo)
            return body

        def finished(state):
            lo, hi, cnt_lo = state
            done = jnp.where(cnt_lo == top_k, 1, jnp.where(hi - lo <= 1, 1, 0))
            return jnp.min(done) == 1

        state = lax.fori_loop(0, VALUE_PASSES, bisect(True), (lo, hi, cnt_lo))
        stops = (0,) + SEARCH_CHECKS + (31,)
        state = lax.fori_loop(stops[0], stops[1], bisect(False), state)
        for first, last in zip(stops[1:-1], stops[2:]):
            state = lax.cond(finished(state), lambda s: s,
                             functools.partial(lax.fori_loop, first, last, bisect(False)), state)
        lo, _, cnt_ge = state
        thr = _key_to_f32(lo)
        thr_scr[...] = thr
        tie_scr[...] = jnp.full((1, TQB), 2 ** 30, jnp.int32)
        row = lax.broadcasted_iota(jnp.int32, (KSUPER, TQB), 0)

        @pl.when(jnp.max(cnt_ge) > top_k)
        def _():
            need = top_k - count_ge(thr, strict=True)
            row_f = row.astype(_F32)

            def plane_body(c, carry):
                r0 = pl.multiple_of(c * KSUPER, KSUPER)
                mb_scr[pl.ds(r0, KSUPER), :] = jnp.where(
                    keys_scr[pl.ds(r0, KSUPER), :] == thr, row_f + r0.astype(_F32), jnp.inf)
                return carry
            lax.fori_loop(0, ns, plane_body, 0)

            max_bits = (keys_scr.shape[0] - 1).bit_length()
            tie_bits = sum((nc * KCHUNK > (1 << p)).astype(jnp.int32) for p in range(max_bits))

            def tie_body(b, lo):
                cand = lo | lax.shift_left(jnp.int32(1), tie_bits - 1 - b)
                cnt = count_rows(mb_scr, lambda r: r < cand.astype(_F32))
                return jnp.where(cnt < need, cand, lo)
            tie_scr[...] = lax.fori_loop(0, tie_bits, tie_body, jnp.zeros((1, TQB), jnp.int32))

    thr = thr_scr[...]
    tie = tie_scr[...]
    row = lax.broadcasted_iota(jnp.int32, (KSUPER, TQB), 0)

    def mask_body(c, carry):
        r0 = pl.multiple_of(c * KSUPER, KSUPER)
        k = keys_scr[pl.ds(r0, KSUPER), :]
        mb_scr[pl.ds(r0, KSUPER), :] = jnp.where(
            k == thr, jnp.where(row <= tie - r0, 0.0, NEG), jnp.where(k > thr, 0.0, NEG))
        return carry

    lax.fori_loop(0, ns, mask_body, 0)

    zeros = jnp.zeros((HEAD_DIM, TQB), _BF16)
    n_pairs = B_HEADS // 2
    rhs = []
    for p in range(n_pairs):
        q0 = qb_ref[0, (2 * p) * HEAD_DIM:(2 * p + 1) * HEAD_DIM, :]
        q1 = qb_ref[0, (2 * p + 1) * HEAD_DIM:(2 * p + 2) * HEAD_DIM, :]
        rhs.append(jnp.concatenate([jnp.concatenate([q0, zeros], axis=0),
                                    jnp.concatenate([zeros, q1], axis=0)], axis=1))

    def score_step(c):
        r0 = pl.multiple_of(c * KCHUNK, KCHUNK)
        for p in range(n_pairs):
            s = jnp.dot(kb_ref[0, pl.ds(r0, KCHUNK), p * LANES:(p + 1) * LANES], rhs[p],
                        preferred_element_type=_F32)
            for hh in range(2):
                h = 2 * p + hh
                parts = []
                for half in range(KCHUNK // TQ):
                    jb = c * (KCHUNK // TQ) + half
                    rs = pl.multiple_of(r0 + half * TQ, TQ)
                    subs = []
                    for sub in range(TQB // TQ):
                        kind = jnp.clip(jb - (i * (TQB // TQ) + sub) + 2, 0, 2)
                        lanes = slice(sub * TQ, (sub + 1) * TQ)
                        subs.append(s[half * TQ:(half + 1) * TQ, hh * TQB + sub * TQ:hh * TQB + (sub + 1) * TQ]
                                    + bias_ref[h, kind] + mb_scr[pl.ds(rs, TQ), lanes])
                    parts.append(jnp.concatenate(subs, axis=1))
                sh = jnp.concatenate(parts, axis=0)
                sc_scr[pl.ds(r0, KCHUNK), h * TQB:(h + 1) * TQB] = sh
                m_scr[h * 8:(h + 1) * 8, :] = jnp.maximum(m_scr[h * 8:(h + 1) * 8, :], _colmax8(sh))

    m_scr[...] = jnp.full(m_scr.shape, NEG, _F32)
    _by_pairs(nc, score_step)
    m_row = jnp.concatenate(
        [jnp.max(m_scr[h * 8:(h + 1) * 8, :], axis=0, keepdims=True) for h in range(B_HEADS)],
        axis=1)
    acc_scr[...] = jnp.zeros_like(acc_scr)

    def pv_step(c):
        r0 = pl.multiple_of(c * KCHUNK, KCHUNK)
        for p in range(n_pairs):
            cols = slice(2 * p * TQB, (2 * p + 2) * TQB)
            pr = jnp.exp2(sc_scr[pl.ds(r0, KCHUNK), cols] - m_row[:, cols]).astype(_BF16)
            acc_scr[p] += jnp.dot(vb_ref[0, c, p * PAIR_ROWS:(p + 1) * PAIR_ROWS, :], pr,
                                  preferred_element_type=_F32)

    _by_pairs(nc, pv_step)
    for p in range(n_pairs):
        inv = 1.0 / acc_scr[p, 2 * HEAD_DIM:2 * HEAD_DIM + 1, :]
        for hh in range(2):
            rows = slice((2 * p + hh) * HEAD_DIM, (2 * p + hh + 1) * HEAD_DIM)
            o = (acc_scr[p, hh * HEAD_DIM:(hh + 1) * HEAD_DIM, hh * TQB:(hh + 1) * TQB]
                 * inv[:, hh * TQB:(hh + 1) * TQB] * zb_ref[0, rows, :].astype(_F32))
            o_ref[0, rows, :] = o.astype(_BF16)


def _mix_b(qb_t, kb_n, vb_c, zb_t, qi_t, ki_n, wi_t, bias_b):
    B, _, S = qb_t.shape
    nq = S // TQB
    per_q = lambda rows: pl.BlockSpec((1, rows, TQB), lambda b, i: (b, 0, i))
    in_specs = [
        per_q(B_WIDTH),
        pl.BlockSpec((1, S, B_WIDTH), lambda b, i: (b, 0, 0)),
        pl.BlockSpec((1, S // KCHUNK, V_ROWS, KCHUNK), lambda b, i: (b, 0, 0, 0)),
        per_q(B_WIDTH),
        per_q(IDX_WIDTH),
        pl.BlockSpec((1, S, LANES), lambda b, i: (b, 0, 0)),
        per_q(IDX_HEADS),
        pl.BlockSpec((B_HEADS, 3, TQ, TQ), lambda b, i: (0, 0, 0, 0)),
    ]
    scratch = [
        pltpu.VMEM((S, TQB), _F32),
        pltpu.VMEM((S, TQB), _F32),
        pltpu.VMEM((S, B_HEADS * TQB), _F32),
        pltpu.VMEM((B_HEADS // 2, PAIR_ROWS, 2 * TQB), _F32),
        pltpu.VMEM((B_HEADS * 8, TQB), _F32),
        pltpu.VMEM((16, TQB), _F32),
        pltpu.VMEM((1, TQB), _F32),
        pltpu.VMEM((1, TQB), jnp.int32),
    ]
    return pl.pallas_call(
        _mix_b_kernel, grid=(B, nq), in_specs=in_specs,
        out_specs=per_q(B_WIDTH),
        out_shape=jax.ShapeDtypeStruct((B, B_WIDTH, S), _BF16),
        scratch_shapes=scratch,
        compiler_params=pltpu.CompilerParams(
            dimension_semantics=("parallel", "arbitrary"), vmem_limit_bytes=VMEM_LIMIT),
        name="mixer_b",
    )(qb_t, kb_n, vb_c, zb_t, qi_t, ki_n, wi_t, bias_b)


def _merge_kernel(x_ref, ya_ref, yb_ref, gt_ref, wpa_ref, wpb_ref, wo_ref, o_ref):
    pa = jnp.dot(wpa_ref[...], ya_ref[0], preferred_element_type=_F32)
    pb = jnp.dot(wpb_ref[...], yb_ref[0], preferred_element_type=_F32)
    merged = (gt_ref[0, :D_MODEL, :].astype(_F32) * pa
              + gt_ref[0, D_MODEL:, :].astype(_F32) * pb).astype(_BF16)
    out_t = jnp.dot(wo_ref[...], merged, preferred_element_type=_F32)
    o_ref[0] = x_ref[0] + out_t.T


def _merge(x, ya_t, yb_t, gt_t, wpa_t, wpb_t, wo_t):
    B, S, D = x.shape
    tm = TM_PROJ
    const = lambda b, t: (0, 0)
    in_specs = [
        pl.BlockSpec((1, tm, D), lambda b, t: (b, t, 0)),
        pl.BlockSpec((1, A_WIDTH, tm), lambda b, t: (b, 0, t)),
        pl.BlockSpec((1, B_WIDTH, tm), lambda b, t: (b, 0, t)),
        pl.BlockSpec((1, 2 * D, tm), lambda b, t: (b, 0, t)),
        pl.BlockSpec((D, A_WIDTH), const),
        pl.BlockSpec((D, B_WIDTH), const),
        pl.BlockSpec((D, D), const),
    ]
    return pl.pallas_call(
        _merge_kernel, grid=(B, S // tm), in_specs=in_specs,
        out_specs=pl.BlockSpec((1, tm, D), lambda b, t: (b, t, 0)),
        out_shape=jax.ShapeDtypeStruct((B, S, D), x.dtype),
        compiler_params=pltpu.CompilerParams(
            dimension_semantics=("parallel", "parallel"), vmem_limit_bytes=VMEM_LIMIT),
        name="merge_out",
    )(x, ya_t, yb_t, gt_t, wpa_t, wpb_t, wo_t)


def _t5_bucket(n):
    n = np.maximum(n, 0)
    max_exact = N_BUCKETS // 2
    nf = np.maximum(n, 1).astype(np.float64)
    large = max_exact + np.floor(np.log(nf / max_exact) / math.log(MAX_DISTANCE / max_exact)
                                 * (N_BUCKETS - max_exact)).astype(np.int64)
    large = np.minimum(large, N_BUCKETS - 1)
    return np.where(n < max_exact, n, large)


def _bias_tables(rel_bias):
    t = np.arange(TQ)[None, :]
    s = np.arange(TQ)[:, None]
    d_prev = t + TQ - s
    d_cur = t - s
    table_a = rel_bias[:, :A_Q_HEADS].astype(_F32).T
    table_b = rel_bias[:, A_Q_HEADS:].astype(_F32).T

    def tile(table, dist, windowed):
        onehot = (_t5_bucket(dist)[..., None] == np.arange(N_BUCKETS)).astype(np.float32)
        b = jnp.einsum("stk,hk->hst", onehot, table, precision=lax.Precision.HIGHEST)
        if windowed:
            b = jnp.where(((dist >= 0) & (dist < WINDOW))[None], b, NEG)
        return b

    bias_a = jnp.concatenate([tile(table_a, d_prev, True), tile(table_a, d_cur, True)], axis=1)
    far = jnp.broadcast_to(table_b[:, N_BUCKETS - 1][:, None, None], (B_HEADS, TQ, TQ))
    bias_b = jnp.stack([far, tile(table_b, d_prev, False), tile(table_b, d_cur, False)], axis=1) * LOG2E
    return bias_a, bias_b


def kernel(x, norm_g, w_in, qnorm_a, knorm_a, sinks_a, qnorm_b, knorm_b, rel_bias,
           w_proj_a, w_proj_b, w_out):
    assert norm_g.shape[0] == 1, "single-layer block"
    B, S, D = x.shape
    assert D == D_MODEL and S % TM_PROJ == 0 and S % KSUPER == 0 and S % TQB == 0 and TQB % TQ == 0
    assert MAX_DISTANCE <= TQ

    wt = w_in[0].T
    n_real = _SEG["kw"][0] + KW_REAL
    wt = jnp.concatenate(
        [wt[:n_real], jnp.zeros((_SEG["kw"][1] - n_real, D), wt.dtype), wt[n_real:]], axis=0)
    wt = wt.astype(_BF16)
    bcast = lambda g, scale: jnp.broadcast_to((g.astype(_F32) * scale)[:, None], (HEAD_DIM, TM_PROJ))
    q_scale = HEAD_DIM ** -0.5
    (qa_t, ka_n, va_t, za_t, qb_t, kb_n, vb_c, zb_t, qi_t, ki_n, wi_t, gt_t) = _inproj(
        x, norm_g[0][None, :].astype(_F32), wt,
        bcast(qnorm_a[0], q_scale), bcast(knorm_a[0], 1.0),
        bcast(qnorm_b[0], q_scale * LOG2E), bcast(knorm_b[0], 1.0))

    bias_a, bias_b = _bias_tables(rel_bias)
    sink_a = jnp.broadcast_to(sinks_a[0].astype(_F32)[:, None], (A_Q_HEADS, TQ))
    ya_t = _mix_a(qa_t, ka_n, va_t, za_t, bias_a, sink_a)
    yb_t = _mix_b(qb_t, kb_n, vb_c, zb_t, qi_t, ki_n, wi_t, bias_b)
    return _merge(x, ya_t, yb_t, gt_t,
                  w_proj_a[0].T.astype(_BF16), w_proj_b[0].T.astype(_BF16), w_out[0].T.astype(_BF16))
```

```python
import functools
import math

import jax
import jax.numpy as jnp
import numpy as np
from jax import lax
from jax.experimental import pallas as pl
from jax.experimental.pallas import tpu as pltpu

D_MODEL = 1024
HEAD_DIM = 64
A_Q_HEADS = 8
A_KV_HEADS = 2
A_GROUPS = A_Q_HEADS // A_KV_HEADS
A_WIDTH = A_Q_HEADS * HEAD_DIM
A_KV_WIDTH = A_KV_HEADS * HEAD_DIM
B_HEADS = 8
B_WIDTH = B_HEADS * HEAD_DIM
IDX_HEADS = 8
IDX_DIM = 32
IDX_WIDTH = IDX_HEADS * IDX_DIM
WINDOW = 128
TOPK_MAX = 256
N_BUCKETS = 32
MAX_DISTANCE = 128
RMS_EPS = 1e-6

LANES = 128
TQ = 128
A_SUB = 4
TQB = 256
KCHUNK = 256
KSUPER = 2 * KCHUNK
VALUE_PASSES = 16
SEARCH_CHECKS = (1, 2, 3, 4, 6, 9)
TM_PROJ = 1024
NEG = -1e30
LOG2E = math.log2(math.e)
ONES_ROWS = 16
PAIR_ROWS = 2 * HEAD_DIM + ONES_ROWS
V_ROWS = (B_HEADS // 2) * PAIR_ROWS
VMEM_LIMIT = 56 * 1024 * 1024

_SEG = {}
_off = 0
for _name, _rows in (("qa", A_WIDTH), ("ka", A_KV_WIDTH), ("va", A_KV_WIDTH), ("za", A_WIDTH),
                     ("qb", B_WIDTH), ("kb", B_WIDTH), ("vb", B_WIDTH), ("zb", B_WIDTH),
                     ("qi", IDX_WIDTH), ("kw", LANES), ("gates", 2 * D_MODEL)):
    _SEG[_name] = (_off, _off + _rows)
    _off += _rows
PROJ_ROWS = _off
KW_REAL = IDX_DIM + IDX_HEADS

_F32 = jnp.float32
_BF16 = jnp.bfloat16
_NT = (((1,), (1,)), ((), ()))


def _inproj_kernel(x_ref, g_ref, wt_ref, gqa_ref, gka_ref, gqb_ref, gkb_ref,
                   qa_o, ka_o, va_o, za_o, qb_o, kb_o, vb_o, zb_o, qi_o, ki_o, wi_o, gt_o):
    tm = x_ref.shape[1]
    x = x_ref[0]
    ms = jnp.mean(x * x, axis=-1, keepdims=True)
    h = (x * lax.rsqrt(ms + RMS_EPS) * g_ref[...]).astype(_BF16)

    def proj(lo, hi):
        return lax.dot_general(wt_ref[lo:hi, :], h, _NT, preferred_element_type=_F32)

    def seg(name):
        return proj(*_SEG[name])

    def headnorm(p, gain_ref):
        nh = p.shape[0] // HEAD_DIM
        p3 = p.reshape(nh, HEAD_DIM, tm)
        r = lax.rsqrt(jnp.mean(p3 * p3, axis=1, keepdims=True) + RMS_EPS)
        return (p3 * r * gain_ref[...][None]).reshape(nh * HEAD_DIM, tm)

    def silu(p):
        return p * jax.nn.sigmoid(p)

    qa_o[0] = headnorm(seg("qa"), gqa_ref).astype(_BF16)
    ka_o[0] = headnorm(seg("ka"), gka_ref).T.astype(_BF16)
    va_o[0] = seg("va").astype(_BF16)
    za_o[0] = silu(seg("za")).astype(_BF16)
    qb_o[0] = headnorm(seg("qb"), gqb_ref).astype(_BF16)
    kb_o[0] = headnorm(seg("kb"), gkb_ref).T.astype(_BF16)
    vb = seg("vb").astype(_BF16)
    ones = jnp.ones((ONES_ROWS, KCHUNK), _BF16)
    for c in range(tm // KCHUNK):
        for p in range(B_HEADS // 2):
            vb_o[0, c, p * PAIR_ROWS:(p + 1) * PAIR_ROWS, :] = jnp.concatenate(
                [vb[p * LANES:(p + 1) * LANES, c * KCHUNK:(c + 1) * KCHUNK], ones], axis=0)
    zb_o[0] = silu(seg("zb")).astype(_BF16)
    qi_o[0] = seg("qi").astype(_BF16)
    kw = seg("kw")
    wi_o[0] = kw[IDX_DIM:KW_REAL, :]
    row = lax.broadcasted_iota(jnp.int32, kw.shape, 0)
    ki_o[0] = jnp.where(row < IDX_DIM, kw, 0.0).T.astype(_BF16)
    g_lo = _SEG["gates"][0]
    for c in range(2 * D_MODEL // 512):
        gt_o[0, c * 512:(c + 1) * 512, :] = jax.nn.sigmoid(
            proj(g_lo + c * 512, g_lo + (c + 1) * 512)).astype(_BF16)


def _inproj(x, g, wt, gqa, gka, gqb, gkb):
    B, S, D = x.shape
    tm = TM_PROJ
    nt = S // tm
    const = lambda b, t: (0, 0)
    tr = lambda rows: pl.BlockSpec((1, rows, tm), lambda b, t: (b, 0, t))
    nat = lambda cols: pl.BlockSpec((1, tm, cols), lambda b, t: (b, t, 0))
    sds = jax.ShapeDtypeStruct
    out_shape = (
        sds((B, A_WIDTH, S), _BF16), sds((B, S, A_KV_WIDTH), _BF16), sds((B, A_KV_WIDTH, S), _BF16),
        sds((B, A_WIDTH, S), _BF16),
        sds((B, B_WIDTH, S), _BF16), sds((B, S, B_WIDTH), _BF16),
        sds((B, S // KCHUNK, V_ROWS, KCHUNK), _BF16), sds((B, B_WIDTH, S), _BF16),
        sds((B, IDX_WIDTH, S), _BF16), sds((B, S, LANES), _BF16), sds((B, IDX_HEADS, S), _F32),
        sds((B, 2 * D_MODEL, S), _BF16),
    )
    out_specs = (
        tr(A_WIDTH), nat(A_KV_WIDTH), tr(A_KV_WIDTH), tr(A_WIDTH),
        tr(B_WIDTH), nat(B_WIDTH),
        pl.BlockSpec((1, tm // KCHUNK, V_ROWS, KCHUNK), lambda b, t: (b, t, 0, 0)), tr(B_WIDTH),
        tr(IDX_WIDTH), nat(LANES), tr(IDX_HEADS), tr(2 * D_MODEL),
    )
    in_specs = [
        pl.BlockSpec((1, tm, D), lambda b, t: (b, t, 0)),
        pl.BlockSpec((1, D), const),
        pl.BlockSpec((PROJ_ROWS, D), const, pipeline_mode=pl.Buffered(1)),
        pl.BlockSpec((HEAD_DIM, tm), const), pl.BlockSpec((HEAD_DIM, tm), const),
        pl.BlockSpec((HEAD_DIM, tm), const), pl.BlockSpec((HEAD_DIM, tm), const),
    ]
    return pl.pallas_call(
        _inproj_kernel, grid=(B, nt), in_specs=in_specs, out_specs=out_specs, out_shape=out_shape,
        compiler_params=pltpu.CompilerParams(
            dimension_semantics=("parallel", "parallel"), vmem_limit_bytes=VMEM_LIMIT),
        name="inproj",
    )(x, g, wt, gqa, gka, gqb, gkb)


def _mix_a_kernel(q_ref, kp_ref, kc_ref, vp_ref, vc_ref, z_ref, bias_ref, sink_ref, o_ref):
    i = pl.program_id(1)
    zeros = jnp.zeros((HEAD_DIM, TQ), _BF16)
    units = [(j, g) for j in range(A_SUB) for g in range(A_KV_HEADS)]

    def bands(j):
        cur = slice(j * TQ, (j + 1) * TQ)
        old = slice((j - 1) * TQ, j * TQ)
        k_old = kp_ref[0] if j == 0 else kc_ref[0, old, :]
        v_old = vp_ref[0] if j == 0 else vc_ref[0, :, old]
        return (jnp.concatenate([k_old, kc_ref[0, cur, :]], axis=0),
                jnp.concatenate([v_old, vc_ref[0, :, cur]], axis=1))

    scores = []
    for j, g in units:
        cur = slice(j * TQ, (j + 1) * TQ)
        cols = []
        for hh in range(A_GROUPS):
            h = g * A_GROUPS + hh
            parts = [zeros] * A_KV_HEADS
            parts[g] = q_ref[0, h * HEAD_DIM:(h + 1) * HEAD_DIM, cur]
            cols.append(jnp.concatenate(parts, axis=0))
        rhs = jnp.concatenate(cols, axis=1)
        scores.append(jnp.dot(bands(j)[0], rhs, preferred_element_type=_F32))

    pts, invs = [], []
    for (j, g), sc in zip(units, scores):
        probs, inv = [], []
        for hh in range(A_GROUPS):
            h = g * A_GROUPS + hh
            s = sc[:, hh * TQ:(hh + 1) * TQ] + bias_ref[h]
            if j == 0:
                s = jnp.concatenate([jnp.where(i > 0, s[:TQ], NEG), s[TQ:]], axis=0)
            sink = sink_ref[h:h + 1, :]
            m = jnp.maximum(jnp.max(s, axis=0, keepdims=True), sink)
            p = jnp.exp(s - m)
            denom = jnp.sum(p, axis=0, keepdims=True) + jnp.exp(sink - m)
            probs.append(p.astype(_BF16))
            inv.append(1.0 / denom)
        pts.append(jnp.concatenate(probs, axis=1))
        invs.append(inv)

    for (j, g), pt, inv in zip(units, pts, invs):
        cur = slice(j * TQ, (j + 1) * TQ)
        out = jnp.dot(bands(j)[1][g * HEAD_DIM:(g + 1) * HEAD_DIM, :], pt,
                      preferred_element_type=_F32)
        for hh in range(A_GROUPS):
            h = g * A_GROUPS + hh
            rows = slice(h * HEAD_DIM, (h + 1) * HEAD_DIM)
            o = out[:, hh * TQ:(hh + 1) * TQ] * inv[hh] * z_ref[0, rows, cur].astype(_F32)
            o_ref[0, rows, cur] = o.astype(_BF16)


def _mix_a(qa_t, ka_n, va_t, za_t, bias_a, sink_a):
    B, _, S = qa_t.shape
    tqa = A_SUB * TQ
    prev = lambda i: jnp.maximum(i * A_SUB - 1, 0)
    in_specs = [
        pl.BlockSpec((1, A_WIDTH, tqa), lambda b, i: (b, 0, i)),
        pl.BlockSpec((1, TQ, A_KV_WIDTH), lambda b, i: (b, prev(i), 0)),
        pl.BlockSpec((1, tqa, A_KV_WIDTH), lambda b, i: (b, i, 0)),
        pl.BlockSpec((1, A_KV_WIDTH, TQ), lambda b, i: (b, 0, prev(i))),
        pl.BlockSpec((1, A_KV_WIDTH, tqa), lambda b, i: (b, 0, i)),
        pl.BlockSpec((1, A_WIDTH, tqa), lambda b, i: (b, 0, i)),
        pl.BlockSpec((A_Q_HEADS, 2 * TQ, TQ), lambda b, i: (0, 0, 0)),
        pl.BlockSpec((A_Q_HEADS, TQ), lambda b, i: (0, 0)),
    ]
    return pl.pallas_call(
        _mix_a_kernel, grid=(B, S // tqa), in_specs=in_specs,
        out_specs=pl.BlockSpec((1, A_WIDTH, tqa), lambda b, i: (b, 0, i)),
        out_shape=jax.ShapeDtypeStruct((B, A_WIDTH, S), _BF16),
        compiler_params=pltpu.CompilerParams(
            dimension_semantics=("parallel", "parallel"), vmem_limit_bytes=VMEM_LIMIT),
        name="mixer_a",
    )(qa_t, ka_n, ka_n, va_t, va_t, za_t, bias_a, sink_a)


def _key_to_f32(key):
    return pltpu.bitcast(jnp.where(key < 0, key ^ jnp.int32(0x7FFFFFFF), key), _F32)


def _f32_to_key(v):
    bits = pltpu.bitcast(v, jnp.int32)
    return jnp.where(bits < 0, bits ^ jnp.int32(0x7FFFFFFF), bits)


def _colmin8(v):
    return jnp.min(v.reshape(v.shape[0] // 8, 8, v.shape[1]), axis=0)


def _colmax8(v):
    return jnp.max(v.reshape(v.shape[0] // 8, 8, v.shape[1]), axis=0)


def _by_pairs(n, step):
    def body(t, carry):
        for u in range(4):
            step(4 * t + u)
        return carry

    lax.fori_loop(0, n // 4, body, 0)
    base = (n // 4) * 4

    @pl.when((n & 2) != 0)
    def _():
        step(base)
        step(base + 1)

    @pl.when((n & 1) != 0)
    def _():
        step(base + (n & 2))


def _mix_b_kernel(qb_ref, kb_ref, vb_ref, zb_ref, qi_ref, ki_ref, wi_ref, bias_ref, o_ref,
                  keys_scr, mb_scr, sc_scr, acc_scr, m_scr, ext_scr, thr_scr, tie_scr):
    i = pl.program_id(1)
    nc = ((i + 1) * TQB + KCHUNK - 1) // KCHUNK
    ns = (nc + 1) // 2
    top_k = TOPK_MAX

    qi = qi_ref[0]
    zpad = jnp.zeros((LANES - IDX_DIM, TQB), _BF16)
    rhs_i = jnp.concatenate(
        [jnp.concatenate([qi[h * IDX_DIM:(h + 1) * IDX_DIM], zpad], axis=0)
         for h in range(IDX_HEADS)], axis=1)
    w = wi_ref[0] * (IDX_DIM ** -0.5 * IDX_HEADS ** -0.5)
    s_minus_t = (lax.broadcasted_iota(jnp.int32, (KCHUNK, TQB), 0)
                 - lax.broadcasted_iota(jnp.int32, (KCHUNK, TQB), 1))

    def index_step(c):
        r0 = pl.multiple_of(c * KCHUNK, KCHUNK)
        d = jnp.dot(ki_ref[0, pl.ds(r0, KCHUNK), :], rhs_i,
                    preferred_element_type=_F32)
        acc = w[0:1, :] * jnp.maximum(d[:, 0:TQB], 0.0)
        for h in range(1, IDX_HEADS):
            acc = acc + w[h:h + 1, :] * jnp.maximum(d[:, h * TQB:(h + 1) * TQB], 0.0)
        causal = s_minus_t <= (i * TQB - r0)
        score = jnp.where(causal, acc, -jnp.inf)
        keys_scr[pl.ds(r0, KCHUNK), :] = score
        ext_scr[0:8, :] = jnp.maximum(ext_scr[0:8, :], _colmax8(score))
        ext_scr[8:16, :] = jnp.minimum(ext_scr[8:16, :], _colmin8(jnp.where(causal, acc, jnp.inf)))

    ext_scr[0:8, :] = jnp.full((8, TQB), -jnp.inf, _F32)
    ext_scr[8:16, :] = jnp.full((8, TQB), jnp.inf, _F32)
    _by_pairs(nc, index_step)

    @pl.when(nc % 2 == 1)
    def _():
        keys_scr[pl.ds(pl.multiple_of(nc * KCHUNK, KCHUNK), KCHUNK), :] = jnp.full(
            (KCHUNK, TQB), -jnp.inf, _F32)

    def count_rows(src, hit):
        n_acc = 8

        def bump(accs, r0, n_rows):
            accs = list(accs)
            rows = src[pl.ds(r0, n_rows), :]
            for j in range(n_rows // 8):
                a = accs[j % n_acc]
                accs[j % n_acc] = jnp.where(hit(rows[j * 8:(j + 1) * 8]), a + 1, a)
            return tuple(accs)

        accs = lax.fori_loop(
            0, nc // 2, lambda c, accs: bump(accs, pl.multiple_of(c * KSUPER, KSUPER), KSUPER),
            tuple(jnp.zeros((8, TQB), jnp.int32) for _ in range(n_acc)))
        accs = lax.cond(nc % 2 == 1,
                        lambda accs: bump(accs, pl.multiple_of((nc - 1) * KCHUNK, KCHUNK), KCHUNK),
                        lambda accs: accs, accs)
        acc = accs[0]
        for a in accs[1:]:
            acc = acc + a
        return jnp.sum(acc, axis=0, keepdims=True)

    def count_ge(cand, strict=False):
        return count_rows(keys_scr, (lambda r: r > cand) if strict else (lambda r: r >= cand))

    @pl.when(i * TQB + TQB <= top_k)
    def _():
        thr_scr[...] = jnp.full((1, TQB), -jnp.inf, _F32)
        tie_scr[...] = jnp.full((1, TQB), -1, jnp.int32)

    @pl.when(i * TQB + TQB > top_k)
    def _():
        zero = jnp.zeros((1, TQB), _F32)
        cnt_nonneg = count_ge(zero)
        cnt_pos = count_ge(zero, strict=True)
        key_max = _f32_to_key(jnp.max(ext_scr[0:8, :], axis=0, keepdims=True))
        key_min = _f32_to_key(jnp.min(ext_scr[8:16, :], axis=0, keepdims=True))
        n_causal = i * TQB + 1 + lax.broadcasted_iota(jnp.int32, (1, TQB), 1)
        nonneg = cnt_nonneg >= top_k
        lo = jnp.where(nonneg, 0, key_min)
        cnt_lo = jnp.where(nonneg, cnt_nonneg, n_causal)
        hi = jnp.where(nonneg, jnp.where(cnt_pos < top_k, 1, key_max + 1), 0)

        def bisect(by_value):
            def body(_, state):
                lo, hi, cnt_lo = state
                span = hi - lo
                mid = lo + lax.shift_right_logical(span, 1)
                if by_value:
                    lo_v, hi_v = _key_to_f32(lo), _key_to_f32(hi)
                    mid_v = _f32_to_key(lo_v + (hi_v - lo_v) * 0.5)
                    mid = jnp.where(span >= 2, jnp.minimum(jnp.maximum(mid_v, lo + 1), hi - 1), lo)
                cnt = count_ge(_key_to_f32(mid))
                ok = cnt >= top_k
                return jnp.where(ok, mid, lo), jnp.where(ok, hi, mid), jnp.where(ok, cnt, cnt_lo)
            return body

        def finished(state):
            lo, hi, cnt_lo = state
            done = jnp.where(cnt_lo == top_k, 1, jnp.where(hi - lo <= 1, 1, 0))
            return jnp.min(done) == 1

        state = lax.fori_loop(0, VALUE_PASSES, bisect(True), (lo, hi, cnt_lo))
        stops = (0,) + SEARCH_CHECKS + (31,)
        state = lax.fori_loop(stops[0], stops[1], bisect(False), state)
        for first, last in zip(stops[1:-1], stops[2:]):
            state = lax.cond(finished(state), lambda s: s,
                             functools.partial(lax.fori_loop, first, last, bisect(False)), state)
        lo, _, cnt_ge = state
        thr = _key_to_f32(lo)
        thr_scr[...] = thr
        tie_scr[...] = jnp.full((1, TQB), 2 ** 30, jnp.int32)

        @pl.when(jnp.max(cnt_ge) > top_k)
        def _():
            need = (top_k - count_ge(thr, strict=True)).astype(_F32)
            lower = (lax.broadcasted_iota(jnp.int32, (KCHUNK, KCHUNK), 0)
                     >= lax.broadcasted_iota(jnp.int32, (KCHUNK, KCHUNK), 1))
            lower = jnp.where(lower, 1.0, 0.0).astype(_BF16)

            def prefix_body(c, state):
                before, below = state
                r0 = pl.multiple_of(c * KCHUNK, KCHUNK)
                tied = jnp.where(keys_scr[pl.ds(r0, KCHUNK), :] == thr, 1.0, 0.0).astype(_BF16)
                upto = jnp.dot(lower, tied, preferred_element_type=_F32) + before
                below = below + jnp.sum(jnp.where(upto < need, 1, 0).reshape(KCHUNK // 8, 8, TQB), axis=0)
                return upto[KCHUNK - 1:KCHUNK, :], below

            _, below = lax.fori_loop(0, nc, prefix_body,
                                     (jnp.zeros((1, TQB), _F32), jnp.zeros((8, TQB), jnp.int32)))
            tie_scr[...] = jnp.sum(below, axis=0, keepdims=True)

    thr = thr_scr[...]
    tie = tie_scr[...]
    row = lax.broadcasted_iota(jnp.int32, (KSUPER, TQB), 0)

    def mask_body(c, carry):
        r0 = pl.multiple_of(c * KSUPER, KSUPER)
        k = keys_scr[pl.ds(r0, KSUPER), :]
        mb_scr[pl.ds(r0, KSUPER), :] = jnp.where(
            k == thr, jnp.where(row <= tie - r0, 0.0, NEG), jnp.where(k > thr, 0.0, NEG))
        return carry

    lax.fori_loop(0, ns, mask_body, 0)

    zeros = jnp.zeros((HEAD_DIM, TQB), _BF16)
    n_pairs = B_HEADS // 2
    rhs = []
    for p in range(n_pairs):
        q0 = qb_ref[0, (2 * p) * HEAD_DIM:(2 * p + 1) * HEAD_DIM, :]
        q1 = qb_ref[0, (2 * p + 1) * HEAD_DIM:(2 * p + 2) * HEAD_DIM, :]
        rhs.append(jnp.concatenate([jnp.concatenate([q0, zeros], axis=0),
                                    jnp.concatenate([zeros, q1], axis=0)], axis=1))

    def score_step(c):
        r0 = pl.multiple_of(c * KCHUNK, KCHUNK)
        for p in range(n_pairs):
            s = jnp.dot(kb_ref[0, pl.ds(r0, KCHUNK), p * LANES:(p + 1) * LANES], rhs[p],
                        preferred_element_type=_F32)
            for hh in range(2):
                h = 2 * p + hh
                parts = []
                for half in range(KCHUNK // TQ):
                    jb = c * (KCHUNK // TQ) + half
                    rs = pl.multiple_of(r0 + half * TQ, TQ)
                    subs = []
                    for sub in range(TQB // TQ):
                        kind = jnp.clip(jb - (i * (TQB // TQ) + sub) + 2, 0, 2)
                        lanes = slice(sub * TQ, (sub + 1) * TQ)
                        subs.append(s[half * TQ:(half + 1) * TQ, hh * TQB + sub * TQ:hh * TQB + (sub + 1) * TQ]
                                    + bias_ref[h, kind] + mb_scr[pl.ds(rs, TQ), lanes])
                    parts.append(jnp.concatenate(subs, axis=1))
                sh = jnp.concatenate(parts, axis=0)
                sc_scr[pl.ds(r0, KCHUNK), h * TQB:(h + 1) * TQB] = sh
                m_scr[h * 8:(h + 1) * 8, :] = jnp.maximum(m_scr[h * 8:(h + 1) * 8, :], _colmax8(sh))

    m_scr[...] = jnp.full(m_scr.shape, NEG, _F32)
    _by_pairs(nc, score_step)
    m_row = jnp.concatenate(
        [jnp.max(m_scr[h * 8:(h + 1) * 8, :], axis=0, keepdims=True) for h in range(B_HEADS)],
        axis=1)
    acc_scr[...] = jnp.zeros_like(acc_scr)

    def pv_step(c):
        r0 = pl.multiple_of(c * KCHUNK, KCHUNK)
        for p in range(n_pairs):
            cols = slice(2 * p * TQB, (2 * p + 2) * TQB)
            pr = jnp.exp2(sc_scr[pl.ds(r0, KCHUNK), cols] - m_row[:, cols]).astype(_BF16)
            acc_scr[p] += jnp.dot(vb_ref[0, c, p * PAIR_ROWS:(p + 1) * PAIR_ROWS, :], pr,
                                  preferred_element_type=_F32)

    _by_pairs(nc, pv_step)
    for p in range(n_pairs):
        inv = 1.0 / acc_scr[p, 2 * HEAD_DIM:2 * HEAD_DIM + 1, :]
        for hh in range(2):
            rows = slice((2 * p + hh) * HEAD_DIM, (2 * p + hh + 1) * HEAD_DIM)
            o = (acc_scr[p, hh * HEAD_DIM:(hh + 1) * HEAD_DIM, hh * TQB:(hh + 1) * TQB]
                 * inv[:, hh * TQB:(hh + 1) * TQB] * zb_ref[0, rows, :].astype(_F32))
            o_ref[0, rows, :] = o.astype(_BF16)


def _mix_b(qb_t, kb_n, vb_c, zb_t, qi_t, ki_n, wi_t, bias_b):
    B, _, S = qb_t.shape
    nq = S // TQB
    per_q = lambda rows: pl.BlockSpec((1, rows, TQB), lambda b, i: (b, 0, i))
    in_specs = [
        per_q(B_WIDTH),
        pl.BlockSpec((1, S, B_WIDTH), lambda b, i: (b, 0, 0)),
        pl.BlockSpec((1, S // KCHUNK, V_ROWS, KCHUNK), lambda b, i: (b, 0, 0, 0)),
        per_q(B_WIDTH),
        per_q(IDX_WIDTH),
        pl.BlockSpec((1, S, LANES), lambda b, i: (b, 0, 0)),
        per_q(IDX_HEADS),
        pl.BlockSpec((B_HEADS, 3, TQ, TQ), lambda b, i: (0, 0, 0, 0)),
    ]
    scratch = [
        pltpu.VMEM((S, TQB), _F32),
        pltpu.VMEM((S, TQB), _F32),
        pltpu.VMEM((S, B_HEADS * TQB), _F32),
        pltpu.VMEM((B_HEADS // 2, PAIR_ROWS, 2 * TQB), _F32),
        pltpu.VMEM((B_HEADS * 8, TQB), _F32),
        pltpu.VMEM((16, TQB), _F32),
        pltpu.VMEM((1, TQB), _F32),
        pltpu.VMEM((1, TQB), jnp.int32),
    ]
    return pl.pallas_call(
        _mix_b_kernel, grid=(B, nq), in_specs=in_specs,
        out_specs=per_q(B_WIDTH),
        out_shape=jax.ShapeDtypeStruct((B, B_WIDTH, S), _BF16),
        scratch_shapes=scratch,
        compiler_params=pltpu.CompilerParams(
            dimension_semantics=("parallel", "arbitrary"), vmem_limit_bytes=VMEM_LIMIT),
        name="mixer_b",
    )(qb_t, kb_n, vb_c, zb_t, qi_t, ki_n, wi_t, bias_b)


def _merge_kernel(x_ref, ya_ref, yb_ref, gt_ref, wpa_ref, wpb_ref, wo_ref, o_ref):
    pa = jnp.dot(wpa_ref[...], ya_ref[0], preferred_element_type=_F32)
    pb = jnp.dot(wpb_ref[...], yb_ref[0], preferred_element_type=_F32)
    merged = (gt_ref[0, :D_MODEL, :].astype(_F32) * pa
              + gt_ref[0, D_MODEL:, :].astype(_F32) * pb).astype(_BF16)
    out_t = jnp.dot(wo_ref[...], merged, preferred_element_type=_F32)
    o_ref[0] = x_ref[0] + out_t.T


def _merge(x, ya_t, yb_t, gt_t, wpa_t, wpb_t, wo_t):
    B, S, D = x.shape
    tm = TM_PROJ
    const = lambda b, t: (0, 0)
    in_specs = [
        pl.BlockSpec((1, tm, D), lambda b, t: (b, t, 0)),
        pl.BlockSpec((1, A_WIDTH, tm), lambda b, t: (b, 0, t)),
        pl.BlockSpec((1, B_WIDTH, tm), lambda b, t: (b, 0, t)),
        pl.BlockSpec((1, 2 * D, tm), lambda b, t: (b, 0, t)),
        pl.BlockSpec((D, A_WIDTH), const),
        pl.BlockSpec((D, B_WIDTH), const),
        pl.BlockSpec((D, D), const),
    ]
    return pl.pallas_call(
        _merge_kernel, grid=(B, S // tm), in_specs=in_specs,
        out_specs=pl.BlockSpec((1, tm, D), lambda b, t: (b, t, 0)),
        out_shape=jax.ShapeDtypeStruct((B, S, D), x.dtype),
        compiler_params=pltpu.CompilerParams(
            dimension_semantics=("parallel", "parallel"), vmem_limit_bytes=VMEM_LIMIT),
        name="merge_out",
    )(x, ya_t, yb_t, gt_t, wpa_t, wpb_t, wo_t)


def _t5_bucket(n):
    n = np.maximum(n, 0)
    max_exact = N_BUCKETS // 2
    nf = np.maximum(n, 1).astype(np.float64)
    large = max_exact + np.floor(np.log(nf / max_exact) / math.log(MAX_DISTANCE / max_exact)
                                 * (N_BUCKETS - max_exact)).astype(np.int64)
    large = np.minimum(large, N_BUCKETS - 1)
    return np.where(n < max_exact, n, large)


def _bias_tables(rel_bias):
    t = np.arange(TQ)[None, :]
    s = np.arange(TQ)[:, None]
    d_prev = t + TQ - s
    d_cur = t - s
    table_a = rel_bias[:, :A_Q_HEADS].astype(_F32).T
    table_b = rel_bias[:, A_Q_HEADS:].astype(_F32).T

    def tile(table, dist, windowed):
        onehot = (_t5_bucket(dist)[..., None] == np.arange(N_BUCKETS)).astype(np.float32)
        b = jnp.einsum("stk,hk->hst", onehot, table, precision=lax.Precision.HIGHEST)
        if windowed:
            b = jnp.where(((dist >= 0) & (dist < WINDOW))[None], b, NEG)
        return b

    bias_a = jnp.concatenate([tile(table_a, d_prev, True), tile(table_a, d_cur, True)], axis=1)
    far = jnp.broadcast_to(table_b[:, N_BUCKETS - 1][:, None, None], (B_HEADS, TQ, TQ))
    bias_b = jnp.stack([far, tile(table_b, d_prev, False), tile(table_b, d_cur, False)], axis=1) * LOG2E
    return bias_a, bias_b


def kernel(x, norm_g, w_in, qnorm_a, knorm_a, sinks_a, qnorm_b, knorm_b, rel_bias,
           w_proj_a, w_proj_b, w_out):
    assert norm_g.shape[0] == 1, "single-layer block"
    B, S, D = x.shape
    assert D == D_MODEL and S % TM_PROJ == 0 and S % KSUPER == 0 and S % TQB == 0 and TQB % TQ == 0
    assert MAX_DISTANCE <= TQ

    wt = w_in[0].T
    n_real = _SEG["kw"][0] + KW_REAL
    wt = jnp.concatenate(
        [wt[:n_real], jnp.zeros((_SEG["kw"][1] - n_real, D), wt.dtype), wt[n_real:]], axis=0)
    wt = wt.astype(_BF16)
    bcast = lambda g, scale: jnp.broadcast_to((g.astype(_F32) * scale)[:, None], (HEAD_DIM, TM_PROJ))
    q_scale = HEAD_DIM ** -0.5
    (qa_t, ka_n, va_t, za_t, qb_t, kb_n, vb_c, zb_t, qi_t, ki_n, wi_t, gt_t) = _inproj(
        x, norm_g[0][None, :].astype(_F32), wt,
        bcast(qnorm_a[0], q_scale), bcast(knorm_a[0], 1.0),
        bcast(qnorm_b[0], q_scale * LOG2E), bcast(knorm_b[0], 1.0))

    bias_a, bias_b = _bias_tables(rel_bias)
    sink_a = jnp.broadcast_to(sinks_a[0].astype(_F32)[:, None], (A_Q_HEADS, TQ))
    ya_t = _mix_a(qa_t, ka_n, va_t, za_t, bias_a, sink_a)
    yb_t = _mix_b(qb_t, kb_n, vb_c, zb_t, qi_t, ki_n, wi_t, bias_b)
    return _merge(x, ya_t, yb_t, gt_t,
                  w_proj_a[0].T.astype(_BF16), w_proj_b[0].T.astype(_BF16), w_out[0].T.astype(_BF16))
```

```python
import functools
import math

import jax
import jax.numpy as jnp
import numpy as np
from jax import lax
from jax.experimental import pallas as pl
from jax.experimental.pallas import tpu as pltpu

D_MODEL = 1024
HEAD_DIM = 64
A_Q_HEADS = 8
A_KV_HEADS = 2
A_GROUPS = A_Q_HEADS // A_KV_HEADS
A_WIDTH = A_Q_HEADS * HEAD_DIM
A_KV_WIDTH = A_KV_HEADS * HEAD_DIM
B_HEADS = 8
B_WIDTH = B_HEADS * HEAD_DIM
IDX_HEADS = 8
IDX_DIM = 32
IDX_WIDTH = IDX_HEADS * IDX_DIM
WINDOW = 128
TOPK_MAX = 256
N_BUCKETS = 32
MAX_DISTANCE = 128
RMS_EPS = 1e-6

LANES = 128
TQ = 128
A_SUB = 4
TQB = 256
KCHUNK = 256
KSUPER = 2 * KCHUNK
VALUE_PASSES = 16
SEARCH_CHECKS = (1, 3, 5, 8)
TM_PROJ = 1024
NEG = -1e30
LOG2E = math.log2(math.e)
ONES_ROWS = 16
PAIR_ROWS = 2 * HEAD_DIM + ONES_ROWS
V_ROWS = (B_HEADS // 2) * PAIR_ROWS
VMEM_LIMIT = 56 * 1024 * 1024

_SEG = {}
_off = 0
for _name, _rows in (("qa", A_WIDTH), ("ka", A_KV_WIDTH), ("va", A_KV_WIDTH), ("za", A_WIDTH),
                     ("qb", B_WIDTH), ("kb", B_WIDTH), ("vb", B_WIDTH), ("zb", B_WIDTH),
                     ("qi", IDX_WIDTH), ("kw", LANES), ("gates", 2 * D_MODEL)):
    _SEG[_name] = (_off, _off + _rows)
    _off += _rows
PROJ_ROWS = _off
KW_REAL = IDX_DIM + IDX_HEADS

_F32 = jnp.float32
_BF16 = jnp.bfloat16
_NT = (((1,), (1,)), ((), ()))


def _inproj_kernel(x_ref, g_ref, wt_ref, gqa_ref, gka_ref, gqb_ref, gkb_ref,
                   qa_o, ka_o, va_o, za_o, qb_o, kb_o, vb_o, zb_o, qi_o, ki_o, wi_o, gt_o):
    tm = x_ref.shape[1]
    x = x_ref[0]
    ms = jnp.mean(x * x, axis=-1, keepdims=True)
    h = (x * lax.rsqrt(ms + RMS_EPS) * g_ref[...]).astype(_BF16)

    def proj(lo, hi):
        return lax.dot_general(wt_ref[lo:hi, :], h, _NT, preferred_element_type=_F32)

    def seg(name):
        return proj(*_SEG[name])

    def headnorm(p, gain_ref):
        nh = p.shape[0] // HEAD_DIM
        p3 = p.reshape(nh, HEAD_DIM, tm)
        r = lax.rsqrt(jnp.mean(p3 * p3, axis=1, keepdims=True) + RMS_EPS)
        return (p3 * r * gain_ref[...][None]).reshape(nh * HEAD_DIM, tm)

    def silu(p):
        return p * jax.nn.sigmoid(p)

    qa_o[0] = headnorm(seg("qa"), gqa_ref).astype(_BF16)
    ka_o[0] = headnorm(seg("ka"), gka_ref).T.astype(_BF16)
    va_o[0] = seg("va").astype(_BF16)
    za_o[0] = silu(seg("za")).astype(_BF16)
    qb_o[0] = headnorm(seg("qb"), gqb_ref).astype(_BF16)
    kb_o[0] = headnorm(seg("kb"), gkb_ref).T.astype(_BF16)
    vb = seg("vb").astype(_BF16)
    ones = jnp.ones((ONES_ROWS, KCHUNK), _BF16)
    for c in range(tm // KCHUNK):
        for p in range(B_HEADS // 2):
            vb_o[0, c, p * PAIR_ROWS:(p + 1) * PAIR_ROWS, :] = jnp.concatenate(
                [vb[p * LANES:(p + 1) * LANES, c * KCHUNK:(c + 1) * KCHUNK], ones], axis=0)
    zb_o[0] = silu(seg("zb")).astype(_BF16)
    qi_o[0] = seg("qi").astype(_BF16)
    kw = seg("kw")
    wi_o[0] = kw[IDX_DIM:KW_REAL, :]
    row = lax.broadcasted_iota(jnp.int32, kw.shape, 0)
    ki_o[0] = jnp.where(row < IDX_DIM, kw, 0.0).T.astype(_BF16)
    g_lo = _SEG["gates"][0]
    for c in range(2 * D_MODEL // 512):
        gt_o[0, c * 512:(c + 1) * 512, :] = jax.nn.sigmoid(
            proj(g_lo + c * 512, g_lo + (c + 1) * 512)).astype(_BF16)


def _inproj(x, g, wt, gqa, gka, gqb, gkb):
    B, S, D = x.shape
    tm = TM_PROJ
    nt = S // tm
    const = lambda b, t: (0, 0)
    tr = lambda rows: pl.BlockSpec((1, rows, tm), lambda b, t: (b, 0, t))
    nat = lambda cols: pl.BlockSpec((1, tm, cols), lambda b, t: (b, t, 0))
    sds = jax.ShapeDtypeStruct
    out_shape = (
        sds((B, A_WIDTH, S), _BF16), sds((B, S, A_KV_WIDTH), _BF16), sds((B, A_KV_WIDTH, S), _BF16),
        sds((B, A_WIDTH, S), _BF16),
        sds((B, B_WIDTH, S), _BF16), sds((B, S, B_WIDTH), _BF16),
        sds((B, S // KCHUNK, V_ROWS, KCHUNK), _BF16), sds((B, B_WIDTH, S), _BF16),
        sds((B, IDX_WIDTH, S), _BF16), sds((B, S, LANES), _BF16), sds((B, IDX_HEADS, S), _F32),
        sds((B, 2 * D_MODEL, S), _BF16),
    )
    out_specs = (
        tr(A_WIDTH), nat(A_KV_WIDTH), tr(A_KV_WIDTH), tr(A_WIDTH),
        tr(B_WIDTH), nat(B_WIDTH),
        pl.BlockSpec((1, tm // KCHUNK, V_ROWS, KCHUNK), lambda b, t: (b, t, 0, 0)), tr(B_WIDTH),
        tr(IDX_WIDTH), nat(LANES), tr(IDX_HEADS), tr(2 * D_MODEL),
    )
    in_specs = [
        pl.BlockSpec((1, tm, D), lambda b, t: (b, t, 0)),
        pl.BlockSpec((1, D), const),
        pl.BlockSpec((PROJ_ROWS, D), const, pipeline_mode=pl.Buffered(1)),
        pl.BlockSpec((HEAD_DIM, tm), const), pl.BlockSpec((HEAD_DIM, tm), const),
        pl.BlockSpec((HEAD_DIM, tm), const), pl.BlockSpec((HEAD_DIM, tm), const),
    ]
    return pl.pallas_call(
        _inproj_kernel, grid=(B, nt), in_specs=in_specs, out_specs=out_specs, out_shape=out_shape,
        compiler_params=pltpu.CompilerParams(
            dimension_semantics=("parallel", "parallel"), vmem_limit_bytes=VMEM_LIMIT),
        name="inproj",
    )(x, g, wt, gqa, gka, gqb, gkb)


def _mix_a_kernel(q_ref, kp_ref, kc_ref, vp_ref, vc_ref, z_ref, bias_ref, sink_ref, o_ref):
    i = pl.program_id(1)
    zeros = jnp.zeros((HEAD_DIM, TQ), _BF16)
    units = [(j, g) for j in range(A_SUB) for g in range(A_KV_HEADS)]

    def bands(j):
        cur = slice(j * TQ, (j + 1) * TQ)
        old = slice((j - 1) * TQ, j * TQ)
        k_old = kp_ref[0] if j == 0 else kc_ref[0, old, :]
        v_old = vp_ref[0] if j == 0 else vc_ref[0, :, old]
        return (jnp.concatenate([k_old, kc_ref[0, cur, :]], axis=0),
                jnp.concatenate([v_old, vc_ref[0, :, cur]], axis=1))

    scores = []
    for j, g in units:
        cur = slice(j * TQ, (j + 1) * TQ)
        cols = []
        for hh in range(A_GROUPS):
            h = g * A_GROUPS + hh
            parts = [zeros] * A_KV_HEADS
            parts[g] = q_ref[0, h * HEAD_DIM:(h + 1) * HEAD_DIM, cur]
            cols.append(jnp.concatenate(parts, axis=0))
        rhs = jnp.concatenate(cols, axis=1)
        scores.append(jnp.dot(bands(j)[0], rhs, preferred_element_type=_F32))

    pts, invs = [], []
    for (j, g), sc in zip(units, scores):
        probs, inv = [], []
        for hh in range(A_GROUPS):
            h = g * A_GROUPS + hh
            s = sc[:, hh * TQ:(hh + 1) * TQ] + bias_ref[h]
            if j == 0:
                s = jnp.concatenate([jnp.where(i > 0, s[:TQ], NEG), s[TQ:]], axis=0)
            sink = sink_ref[h:h + 1, :]
            m = jnp.maximum(jnp.max(s, axis=0, keepdims=True), sink)
            p = jnp.exp(s - m)
            denom = jnp.sum(p, axis=0, keepdims=True) + jnp.exp(sink - m)
            probs.append(p.astype(_BF16))
            inv.append(1.0 / denom)
        pts.append(jnp.concatenate(probs, axis=1))
        invs.append(inv)

    for (j, g), pt, inv in zip(units, pts, invs):
        cur = slice(j * TQ, (j + 1) * TQ)
        out = jnp.dot(bands(j)[1][g * HEAD_DIM:(g + 1) * HEAD_DIM, :], pt,
                      preferred_element_type=_F32)
        for hh in range(A_GROUPS):
            h = g * A_GROUPS + hh
            rows = slice(h * HEAD_DIM, (h + 1) * HEAD_DIM)
            o = out[:, hh * TQ:(hh + 1) * TQ] * inv[hh] * z_ref[0, rows, cur].astype(_F32)
            o_ref[0, rows, cur] = o.astype(_BF16)


def _mix_a(qa_t, ka_n, va_t, za_t, bias_a, sink_a):
    B, _, S = qa_t.shape
    tqa = A_SUB * TQ
    prev = lambda i: jnp.maximum(i * A_SUB - 1, 0)
    in_specs = [
        pl.BlockSpec((1, A_WIDTH, tqa), lambda b, i: (b, 0, i)),
        pl.BlockSpec((1, TQ, A_KV_WIDTH), lambda b, i: (b, prev(i), 0)),
        pl.BlockSpec((1, tqa, A_KV_WIDTH), lambda b, i: (b, i, 0)),
        pl.BlockSpec((1, A_KV_WIDTH, TQ), lambda b, i: (b, 0, prev(i))),
        pl.BlockSpec((1, A_KV_WIDTH, tqa), lambda b, i: (b, 0, i)),
        pl.BlockSpec((1, A_WIDTH, tqa), lambda b, i: (b, 0, i)),
        pl.BlockSpec((A_Q_HEADS, 2 * TQ, TQ), lambda b, i: (0, 0, 0)),
        pl.BlockSpec((A_Q_HEADS, TQ), lambda b, i: (0, 0)),
    ]
    return pl.pallas_call(
        _mix_a_kernel, grid=(B, S // tqa), in_specs=in_specs,
        out_specs=pl.BlockSpec((1, A_WIDTH, tqa), lambda b, i: (b, 0, i)),
        out_shape=jax.ShapeDtypeStruct((B, A_WIDTH, S), _BF16),
        compiler_params=pltpu.CompilerParams(
            dimension_semantics=("parallel", "parallel"), vmem_limit_bytes=VMEM_LIMIT),
        name="mixer_a",
    )(qa_t, ka_n, ka_n, va_t, va_t, za_t, bias_a, sink_a)


def _key_to_f32(key):
    return pltpu.bitcast(jnp.where(key < 0, key ^ jnp.int32(0x7FFFFFFF), key), _F32)


def _f32_to_key(v):
    bits = pltpu.bitcast(v, jnp.int32)
    return jnp.where(bits < 0, bits ^ jnp.int32(0x7FFFFFFF), bits)


def _colmin8(v):
    return jnp.min(v.reshape(v.shape[0] // 8, 8, v.shape[1]), axis=0)


def _colmax8(v):
    return jnp.max(v.reshape(v.shape[0] // 8, 8, v.shape[1]), axis=0)


def _by_pairs(n, step):
    def body(t, carry):
        for u in range(4):
            step(4 * t + u)
        return carry

    lax.fori_loop(0, n // 4, body, 0)
    base = (n // 4) * 4
    for rest in (1, 2, 3):
        @pl.when(n - base == rest)
        def _(rest=rest):
            for u in range(rest):
                step(base + u)


def _mix_b_kernel(qb_ref, kb_ref, vb_ref, zb_ref, qi_ref, ki_ref, wi_ref, bias_ref, o_ref,
                  keys_scr, mb_scr, sc_scr, acc_scr, m_scr, ext_scr, thr_scr, tie_scr):
    i = pl.program_id(1)
    nc = ((i + 1) * TQB + KCHUNK - 1) // KCHUNK
    ns = (nc + 1) // 2
    top_k = TOPK_MAX

    qi = qi_ref[0]
    zpad = jnp.zeros((LANES - IDX_DIM, TQB), _BF16)
    rhs_i = jnp.concatenate(
        [jnp.concatenate([qi[h * IDX_DIM:(h + 1) * IDX_DIM], zpad], axis=0)
         for h in range(IDX_HEADS)], axis=1)
    w = wi_ref[0] * (IDX_DIM ** -0.5 * IDX_HEADS ** -0.5)
    s_minus_t = (lax.broadcasted_iota(jnp.int32, (KCHUNK, TQB), 0)
                 - lax.broadcasted_iota(jnp.int32, (KCHUNK, TQB), 1))

    def index_step(c):
        r0 = pl.multiple_of(c * KCHUNK, KCHUNK)
        d = jnp.dot(ki_ref[0, pl.ds(r0, KCHUNK), :], rhs_i,
                    preferred_element_type=_F32)
        acc = w[0:1, :] * jnp.maximum(d[:, 0:TQB], 0.0)
        for h in range(1, IDX_HEADS):
            acc = acc + w[h:h + 1, :] * jnp.maximum(d[:, h * TQB:(h + 1) * TQB], 0.0)
        causal = s_minus_t <= (i * TQB - r0)
        score = jnp.where(causal, acc, -jnp.inf)
        keys_scr[pl.ds(r0, KCHUNK), :] = score
        ext_scr[0:8, :] = jnp.maximum(ext_scr[0:8, :], _colmax8(score))
        ext_scr[8:16, :] = jnp.minimum(ext_scr[8:16, :], _colmin8(jnp.where(causal, acc, jnp.inf)))

    ext_scr[0:8, :] = jnp.full((8, TQB), -jnp.inf, _F32)
    ext_scr[8:16, :] = jnp.full((8, TQB), jnp.inf, _F32)
    _by_pairs(nc, index_step)

    @pl.when(nc % 2 == 1)
    def _():
        keys_scr[pl.ds(pl.multiple_of(nc * KCHUNK, KCHUNK), KCHUNK), :] = jnp.full(
            (KCHUNK, TQB), -jnp.inf, _F32)

    def count_rows(src, hit):
        n_acc = 8

        def bump(accs, r0, n_rows):
            accs = list(accs)
            rows = src[pl.ds(r0, n_rows), :]
            for j in range(n_rows // 8):
                a = accs[j % n_acc]
                accs[j % n_acc] = jnp.where(hit(rows[j * 8:(j + 1) * 8]), a + 1, a)
            return tuple(accs)

        accs = lax.fori_loop(
            0, nc // 2, lambda c, accs: bump(accs, pl.multiple_of(c * KSUPER, KSUPER), KSUPER),
            tuple(jnp.zeros((8, TQB), jnp.int32) for _ in range(n_acc)))
        accs = lax.cond(nc % 2 == 1,
                        lambda accs: bump(accs, pl.multiple_of((nc - 1) * KCHUNK, KCHUNK), KCHUNK),
                        lambda accs: accs, accs)
        acc = accs[0]
        for a in accs[1:]:
            acc = acc + a
        return jnp.sum(acc, axis=0, keepdims=True)

    def count_ge(cand, strict=False):
        return count_rows(keys_scr, (lambda r: r > cand) if strict else (lambda r: r >= cand))

    @pl.when(i * TQB + TQB <= top_k)
    def _():
        thr_scr[...] = jnp.full((1, TQB), -jnp.inf, _F32)
        tie_scr[...] = jnp.full((1, TQB), -1, jnp.int32)

    @pl.when(i * TQB + TQB > top_k)
    def _():
        zero = jnp.zeros((1, TQB), _F32)
        cnt_nonneg = count_ge(zero)
        cnt_pos = count_ge(zero, strict=True)
        key_max = _f32_to_key(jnp.max(ext_scr[0:8, :], axis=0, keepdims=True))
        key_min = _f32_to_key(jnp.min(ext_scr[8:16, :], axis=0, keepdims=True))
        n_causal = i * TQB + 1 + lax.broadcasted_iota(jnp.int32, (1, TQB), 1)
        nonneg = cnt_nonneg >= top_k
        lo = jnp.where(nonneg, 0, key_min)
        cnt_lo = jnp.where(nonneg, cnt_nonneg, n_causal)
        hi = jnp.where(nonneg, jnp.where(cnt_pos < top_k, 1, key_max + 1), 0)

        def bisect(by_value):
            def body(_, state):
                lo, hi, cnt_lo = state
                span = hi - lo
                mid = lo + lax.shift_right_logical(span, 1)
                if by_value:
                    lo_v, hi_v = _key_to_f32(lo), _key_to_f32(hi)
                    mid_v = _f32_to_key(lo_v + (hi_v - lo_v) * 0.5)
                    mid = jnp.where(span >= 2, jnp.minimum(jnp.maximum(mid_v, lo + 1), hi - 1), lo)
                cnt = count_ge(_key_to_f32(mid))
                ok = cnt >= top_k
                return jnp.where(ok, mid, lo), jnp.where(ok, hi, mid), jnp.where(ok, cnt, cnt_lo)
            return body

        def finished(state):
            lo, hi, cnt_lo = state
            done = jnp.where(cnt_lo == top_k, 1, jnp.where(hi - lo <= 1, 1, 0))
            return jnp.min(done) == 1

        state = lax.fori_loop(0, VALUE_PASSES, bisect(True), (lo, hi, cnt_lo))
        stops = (0,) + SEARCH_CHECKS + (31,)
        state = lax.fori_loop(stops[0], stops[1], bisect(False), state)
        for first, last in zip(stops[1:-1], stops[2:]):
            state = lax.cond(finished(state), lambda s: s,
                             functools.partial(lax.fori_loop, first, last, bisect(False)), state)
        lo, _, cnt_ge = state
        thr = _key_to_f32(lo)
        thr_scr[...] = thr
        tie_scr[...] = jnp.full((1, TQB), 2 ** 30, jnp.int32)

        @pl.when(jnp.max(cnt_ge) > top_k)
        def _():
            need = (top_k - count_ge(thr, strict=True)).astype(_F32)
            lower = (lax.broadcasted_iota(jnp.int32, (KCHUNK, KCHUNK), 0)
                     >= lax.broadcasted_iota(jnp.int32, (KCHUNK, KCHUNK), 1))
            lower = jnp.where(lower, 1.0, 0.0).astype(_BF16)

            def prefix_body(c, state):
                before, below = state
                r0 = pl.multiple_of(c * KCHUNK, KCHUNK)
                tied = jnp.where(keys_scr[pl.ds(r0, KCHUNK), :] == thr, 1.0, 0.0).astype(_BF16)
                upto = jnp.dot(lower, tied, preferred_element_type=_F32) + before
                below = below + jnp.sum(jnp.where(upto < need, 1, 0).reshape(KCHUNK // 8, 8, TQB), axis=0)
                return upto[KCHUNK - 1:KCHUNK, :], below

            _, below = lax.fori_loop(0, nc, prefix_body,
                                     (jnp.zeros((1, TQB), _F32), jnp.zeros((8, TQB), jnp.int32)))
            tie_scr[...] = jnp.sum(below, axis=0, keepdims=True)

    thr = thr_scr[...]
    tie = tie_scr[...]
    row = lax.broadcasted_iota(jnp.int32, (KSUPER, TQB), 0)

    def mask_body(c, carry):
        r0 = pl.multiple_of(c * KSUPER, KSUPER)
        k = keys_scr[pl.ds(r0, KSUPER), :]
        mb_scr[pl.ds(r0, KSUPER), :] = jnp.where(
            k == thr, jnp.where(row <= tie - r0, 0.0, NEG), jnp.where(k > thr, 0.0, NEG))
        return carry

    lax.fori_loop(0, ns, mask_body, 0)

    zeros = jnp.zeros((HEAD_DIM, TQB), _BF16)
    n_pairs = B_HEADS // 2
    rhs = []
    for p in range(n_pairs):
        q0 = qb_ref[0, (2 * p) * HEAD_DIM:(2 * p + 1) * HEAD_DIM, :]
        q1 = qb_ref[0, (2 * p + 1) * HEAD_DIM:(2 * p + 2) * HEAD_DIM, :]
        rhs.append(jnp.concatenate([jnp.concatenate([q0, zeros], axis=0),
                                    jnp.concatenate([zeros, q1], axis=0)], axis=1))

    def score_step(c):
        r0 = pl.multiple_of(c * KCHUNK, KCHUNK)
        for p in range(n_pairs):
            s = jnp.dot(kb_ref[0, pl.ds(r0, KCHUNK), p * LANES:(p + 1) * LANES], rhs[p],
                        preferred_element_type=_F32)
            for hh in range(2):
                h = 2 * p + hh
                parts = []
                for half in range(KCHUNK // TQ):
                    jb = c * (KCHUNK // TQ) + half
                    rs = pl.multiple_of(r0 + half * TQ, TQ)
                    subs = []
                    for sub in range(TQB // TQ):
                        kind = jnp.clip(jb - (i * (TQB // TQ) + sub) + 2, 0, 2)
                        lanes = slice(sub * TQ, (sub + 1) * TQ)
                        subs.append(s[half * TQ:(half + 1) * TQ, hh * TQB + sub * TQ:hh * TQB + (sub + 1) * TQ]
                                    + bias_ref[h, kind] + mb_scr[pl.ds(rs, TQ), lanes])
                    parts.append(jnp.concatenate(subs, axis=1))
                sh = jnp.concatenate(parts, axis=0)
                sc_scr[pl.ds(r0, KCHUNK), h * TQB:(h + 1) * TQB] = sh
                m_scr[h * 8:(h + 1) * 8, :] = jnp.maximum(m_scr[h * 8:(h + 1) * 8, :], _colmax8(sh))

    m_scr[...] = jnp.full(m_scr.shape, NEG, _F32)
    _by_pairs(nc, score_step)
    m_row = jnp.concatenate(
        [jnp.max(m_scr[h * 8:(h + 1) * 8, :], axis=0, keepdims=True) for h in range(B_HEADS)],
        axis=1)
    acc_scr[...] = jnp.zeros_like(acc_scr)

    def pv_step(c):
        r0 = pl.multiple_of(c * KCHUNK, KCHUNK)
        for p in range(n_pairs):
            cols = slice(2 * p * TQB, (2 * p + 2) * TQB)
            pr = jnp.exp2(sc_scr[pl.ds(r0, KCHUNK), cols] - m_row[:, cols]).astype(_BF16)
            acc_scr[p] += jnp.dot(vb_ref[0, c, p * PAIR_ROWS:(p + 1) * PAIR_ROWS, :], pr,
                                  preferred_element_type=_F32)

    _by_pairs(nc, pv_step)
    for p in range(n_pairs):
        inv = 1.0 / acc_scr[p, 2 * HEAD_DIM:2 * HEAD_DIM + 1, :]
        for hh in range(2):
            rows = slice((2 * p + hh) * HEAD_DIM, (2 * p + hh + 1) * HEAD_DIM)
            o = (acc_scr[p, hh * HEAD_DIM:(hh + 1) * HEAD_DIM, hh * TQB:(hh + 1) * TQB]
                 * inv[:, hh * TQB:(hh + 1) * TQB] * zb_ref[0, rows, :].astype(_F32))
            o_ref[0, rows, :] = o.astype(_BF16)


def _mix_b(qb_t, kb_n, vb_c, zb_t, qi_t, ki_n, wi_t, bias_b):
    B, _, S = qb_t.shape
    nq = S // TQB
    per_q = lambda rows: pl.BlockSpec((1, rows, TQB), lambda b, i: (b, 0, i))
    in_specs = [
        per_q(B_WIDTH),
        pl.BlockSpec((1, S, B_WIDTH), lambda b, i: (b, 0, 0)),
        pl.BlockSpec((1, S // KCHUNK, V_ROWS, KCHUNK), lambda b, i: (b, 0, 0, 0)),
        per_q(B_WIDTH),
        per_q(IDX_WIDTH),
        pl.BlockSpec((1, S, LANES), lambda b, i: (b, 0, 0)),
        per_q(IDX_HEADS),
        pl.BlockSpec((B_HEADS, 3, TQ, TQ), lambda b, i: (0, 0, 0, 0)),
    ]
    scratch = [
        pltpu.VMEM((S, TQB), _F32),
        pltpu.VMEM((S, TQB), _F32),
        pltpu.VMEM((S, B_HEADS * TQB), _F32),
        pltpu.VMEM((B_HEADS // 2, PAIR_ROWS, 2 * TQB), _F32),
        pltpu.VMEM((B_HEADS * 8, TQB), _F32),
        pltpu.VMEM((16, TQB), _F32),
        pltpu.VMEM((1, TQB), _F32),
        pltpu.VMEM((1, TQB), jnp.int32),
    ]
    return pl.pallas_call(
        _mix_b_kernel, grid=(B, nq), in_specs=in_specs,
        out_specs=per_q(B_WIDTH),
        out_shape=jax.ShapeDtypeStruct((B, B_WIDTH, S), _BF16),
        scratch_shapes=scratch,
        compiler_params=pltpu.CompilerParams(
            dimension_semantics=("parallel", "arbitrary"), vmem_limit_bytes=VMEM_LIMIT),
        name="mixer_b",
    )(qb_t, kb_n, vb_c, zb_t, qi_t, ki_n, wi_t, bias_b)


def _merge_kernel(x_ref, ya_ref, yb_ref, gt_ref, wpa_ref, wpb_ref, wo_ref, o_ref):
    pa = jnp.dot(wpa_ref[...], ya_ref[0], preferred_element_type=_F32)
    pb = jnp.dot(wpb_ref[...], yb_ref[0], preferred_element_type=_F32)
    merged = (gt_ref[0, :D_MODEL, :].astype(_F32) * pa
              + gt_ref[0, D_MODEL:, :].astype(_F32) * pb).astype(_BF16)
    out_t = jnp.dot(wo_ref[...], merged, preferred_element_type=_F32)
    o_ref[0] = x_ref[0] + out_t.T


def _merge(x, ya_t, yb_t, gt_t, wpa_t, wpb_t, wo_t):
    B, S, D = x.shape
    tm = TM_PROJ
    const = lambda b, t: (0, 0)
    in_specs = [
        pl.BlockSpec((1, tm, D), lambda b, t: (b, t, 0)),
        pl.BlockSpec((1, A_WIDTH, tm), lambda b, t: (b, 0, t)),
        pl.BlockSpec((1, B_WIDTH, tm), lambda b, t: (b, 0, t)),
        pl.BlockSpec((1, 2 * D, tm), lambda b, t: (b, 0, t)),
        pl.BlockSpec((D, A_WIDTH), const),
        pl.BlockSpec((D, B_WIDTH), const),
        pl.BlockSpec((D, D), const),
    ]
    return pl.pallas_call(
        _merge_kernel, grid=(B, S // tm), in_specs=in_specs,
        out_specs=pl.BlockSpec((1, tm, D), lambda b, t: (b, t, 0)),
        out_shape=jax.ShapeDtypeStruct((B, S, D), x.dtype),
        compiler_params=pltpu.CompilerParams(
            dimension_semantics=("parallel", "parallel"), vmem_limit_bytes=VMEM_LIMIT),
        name="merge_out",
    )(x, ya_t, yb_t, gt_t, wpa_t, wpb_t, wo_t)


def _t5_bucket(n):
    n = np.maximum(n, 0)
    max_exact = N_BUCKETS // 2
    nf = np.maximum(n, 1).astype(np.float64)
    large = max_exact + np.floor(np.log(nf / max_exact) / math.log(MAX_DISTANCE / max_exact)
                                 * (N_BUCKETS - max_exact)).astype(np.int64)
    large = np.minimum(large, N_BUCKETS - 1)
    return np.where(n < max_exact, n, large)


def _bias_tables(rel_bias):
    t = np.arange(TQ)[None, :]
    s = np.arange(TQ)[:, None]
    d_prev = t + TQ - s
    d_cur = t - s
    table_a = rel_bias[:, :A_Q_HEADS].astype(_F32).T
    table_b = rel_bias[:, A_Q_HEADS:].astype(_F32).T

    def tile(table, dist, windowed):
        onehot = (_t5_bucket(dist)[..., None] == np.arange(N_BUCKETS)).astype(np.float32)
        b = jnp.einsum("stk,hk->hst", onehot, table, precision=lax.Precision.HIGHEST)
        if windowed:
            b = jnp.where(((dist >= 0) & (dist < WINDOW))[None], b, NEG)
        return b

    bias_a = jnp.concatenate([tile(table_a, d_prev, True), tile(table_a, d_cur, True)], axis=1)
    far = jnp.broadcast_to(table_b[:, N_BUCKETS - 1][:, None, None], (B_HEADS, TQ, TQ))
    bias_b = jnp.stack([far, tile(table_b, d_prev, False), tile(table_b, d_cur, False)], axis=1) * LOG2E
    return bias_a, bias_b


def kernel(x, norm_g, w_in, qnorm_a, knorm_a, sinks_a, qnorm_b, knorm_b, rel_bias,
           w_proj_a, w_proj_b, w_out):
    assert norm_g.shape[0] == 1, "single-layer block"
    B, S, D = x.shape
    assert D == D_MODEL and S % TM_PROJ == 0 and S % KSUPER == 0 and S % TQB == 0 and TQB % TQ == 0
    assert MAX_DISTANCE <= TQ

    wt = w_in[0].T
    n_real = _SEG["kw"][0] + KW_REAL
    wt = jnp.concatenate(
        [wt[:n_real], jnp.zeros((_SEG["kw"][1] - n_real, D), wt.dtype), wt[n_real:]], axis=0)
    wt = wt.astype(_BF16)
    bcast = lambda g, scale: jnp.broadcast_to((g.astype(_F32) * scale)[:, None], (HEAD_DIM, TM_PROJ))
    q_scale = HEAD_DIM ** -0.5
    (qa_t, ka_n, va_t, za_t, qb_t, kb_n, vb_c, zb_t, qi_t, ki_n, wi_t, gt_t) = _inproj(
        x, norm_g[0][None, :].astype(_F32), wt,
        bcast(qnorm_a[0], q_scale), bcast(knorm_a[0], 1.0),
        bcast(qnorm_b[0], q_scale * LOG2E), bcast(knorm_b[0], 1.0))

    bias_a, bias_b = _bias_tables(rel_bias)
    sink_a = jnp.broadcast_to(sinks_a[0].astype(_F32)[:, None], (A_Q_HEADS, TQ))
    ya_t = _mix_a(qa_t, ka_n, va_t, za_t, bias_a, sink_a)
    yb_t = _mix_b(qb_t, kb_n, vb_c, zb_t, qi_t, ki_n, wi_t, bias_b)
    return _merge(x, ya_t, yb_t, gt_t,
                  w_proj_a[0].T.astype(_BF16), w_proj_b[0].T.astype(_BF16), w_out[0].T.astype(_BF16))
```

```python
import functools
import math

import jax
import jax.numpy as jnp
import numpy as np
from jax import lax
from jax.experimental import pallas as pl
from jax.experimental.pallas import tpu as pltpu

D_MODEL = 1024
HEAD_DIM = 64
A_Q_HEADS = 8
A_KV_HEADS = 2
A_GROUPS = A_Q_HEADS // A_KV_HEADS
A_WIDTH = A_Q_HEADS * HEAD_DIM
A_KV_WIDTH = A_KV_HEADS * HEAD_DIM
B_HEADS = 8
B_WIDTH = B_HEADS * HEAD_DIM
IDX_HEADS = 8
IDX_DIM = 32
IDX_WIDTH = IDX_HEADS * IDX_DIM
WINDOW = 128
TOPK_MAX = 256
N_BUCKETS = 32
MAX_DISTANCE = 128
RMS_EPS = 1e-6

LANES = 128
TQ = 128
A_SUB = 4
TQB = 256
KCHUNK = 256
KSUPER = 2 * KCHUNK
VALUE_PASSES = 16
SEARCH_CHECKS = (1, 3, 5, 8)
TM_PROJ = 1024
NEG = -1e30
LOG2E = math.log2(math.e)
ONES_ROWS = 16
PAIR_ROWS = 2 * HEAD_DIM + ONES_ROWS
V_ROWS = (B_HEADS // 2) * PAIR_ROWS
VMEM_LIMIT = 56 * 1024 * 1024

_SEG = {}
_off = 0
for _name, _rows in (("qa", A_WIDTH), ("ka", A_KV_WIDTH), ("va", A_KV_WIDTH), ("za", A_WIDTH),
                     ("qb", B_WIDTH), ("kb", B_WIDTH), ("vb", B_WIDTH), ("zb", B_WIDTH),
                     ("qi", IDX_WIDTH), ("kw", LANES), ("gates", 2 * D_MODEL)):
    _SEG[_name] = (_off, _off + _rows)
    _off += _rows
PROJ_ROWS = _off
KW_REAL = IDX_DIM + IDX_HEADS

_F32 = jnp.float32
_BF16 = jnp.bfloat16
_NT = (((1,), (1,)), ((), ()))


def _inproj_kernel(x_ref, g_ref, wt_ref, gqa_ref, gka_ref, gqb_ref, gkb_ref,
                   qa_o, ka_o, va_o, za_o, qb_o, kb_o, vb_o, zb_o, qi_o, ki_o, wi_o, gt_o):
    tm = x_ref.shape[1]
    x = x_ref[0]
    ms = jnp.mean(x * x, axis=-1, keepdims=True)
    h = (x * lax.rsqrt(ms + RMS_EPS) * g_ref[...]).astype(_BF16)

    def proj(lo, hi):
        return lax.dot_general(wt_ref[lo:hi, :], h, _NT, preferred_element_type=_F32)

    def seg(name):
        return proj(*_SEG[name])

    def headnorm(p, gain_ref):
        nh = p.shape[0] // HEAD_DIM
        p3 = p.reshape(nh, HEAD_DIM, tm)
        r = lax.rsqrt(jnp.mean(p3 * p3, axis=1, keepdims=True) + RMS_EPS)
        return (p3 * r * gain_ref[...][None]).reshape(nh * HEAD_DIM, tm)

    def silu(p):
        return p * jax.nn.sigmoid(p)

    qa_o[0] = headnorm(seg("qa"), gqa_ref).astype(_BF16)
    ka_o[0] = headnorm(seg("ka"), gka_ref).T.astype(_BF16)
    va_o[0] = seg("va").astype(_BF16)
    za_o[0] = silu(seg("za")).astype(_BF16)
    qb_o[0] = headnorm(seg("qb"), gqb_ref).astype(_BF16)
    kb_o[0] = headnorm(seg("kb"), gkb_ref).T.astype(_BF16)
    vb = seg("vb").astype(_BF16)
    ones = jnp.ones((ONES_ROWS, KCHUNK), _BF16)
    for c in range(tm // KCHUNK):
        for p in range(B_HEADS // 2):
            vb_o[0, c, p * PAIR_ROWS:(p + 1) * PAIR_ROWS, :] = jnp.concatenate(
                [vb[p * LANES:(p + 1) * LANES, c * KCHUNK:(c + 1) * KCHUNK], ones], axis=0)
    zb_o[0] = silu(seg("zb")).astype(_BF16)
    qi_o[0] = seg("qi").astype(_BF16)
    kw = seg("kw")
    wi_o[0] = kw[IDX_DIM:KW_REAL, :]
    row = lax.broadcasted_iota(jnp.int32, kw.shape, 0)
    ki_o[0] = jnp.where(row < IDX_DIM, kw, 0.0).T.astype(_BF16)
    g_lo = _SEG["gates"][0]
    for c in range(2 * D_MODEL // 512):
        gt_o[0, c * 512:(c + 1) * 512, :] = jax.nn.sigmoid(
            proj(g_lo + c * 512, g_lo + (c + 1) * 512)).astype(_BF16)


def _inproj(x, g, wt, gqa, gka, gqb, gkb):
    B, S, D = x.shape
    tm = TM_PROJ
    nt = S // tm
    const = lambda b, t: (0, 0)
    tr = lambda rows: pl.BlockSpec((1, rows, tm), lambda b, t: (b, 0, t))
    nat = lambda cols: pl.BlockSpec((1, tm, cols), lambda b, t: (b, t, 0))
    sds = jax.ShapeDtypeStruct
    out_shape = (
        sds((B, A_WIDTH, S), _BF16), sds((B, S, A_KV_WIDTH), _BF16), sds((B, A_KV_WIDTH, S), _BF16),
        sds((B, A_WIDTH, S), _BF16),
        sds((B, B_WIDTH, S), _BF16), sds((B, S, B_WIDTH), _BF16),
        sds((B, S // KCHUNK, V_ROWS, KCHUNK), _BF16), sds((B, B_WIDTH, S), _BF16),
        sds((B, IDX_WIDTH, S), _BF16), sds((B, S, LANES), _BF16), sds((B, IDX_HEADS, S), _F32),
        sds((B, 2 * D_MODEL, S), _BF16),
    )
    out_specs = (
        tr(A_WIDTH), nat(A_KV_WIDTH), tr(A_KV_WIDTH), tr(A_WIDTH),
        tr(B_WIDTH), nat(B_WIDTH),
        pl.BlockSpec((1, tm // KCHUNK, V_ROWS, KCHUNK), lambda b, t: (b, t, 0, 0)), tr(B_WIDTH),
        tr(IDX_WIDTH), nat(LANES), tr(IDX_HEADS), tr(2 * D_MODEL),
    )
    in_specs = [
        pl.BlockSpec((1, tm, D), lambda b, t: (b, t, 0)),
        pl.BlockSpec((1, D), const),
        pl.BlockSpec((PROJ_ROWS, D), const, pipeline_mode=pl.Buffered(1)),
        pl.BlockSpec((HEAD_DIM, tm), const), pl.BlockSpec((HEAD_DIM, tm), const),
        pl.BlockSpec((HEAD_DIM, tm), const), pl.BlockSpec((HEAD_DIM, tm), const),
    ]
    return pl.pallas_call(
        _inproj_kernel, grid=(B, nt), in_specs=in_specs, out_specs=out_specs, out_shape=out_shape,
        compiler_params=pltpu.CompilerParams(
            dimension_semantics=("parallel", "parallel"), vmem_limit_bytes=VMEM_LIMIT),
        name="inproj",
    )(x, g, wt, gqa, gka, gqb, gkb)


def _mix_a_kernel(q_ref, kp_ref, kc_ref, vp_ref, vc_ref, z_ref, bias_ref, sink_ref, o_ref):
    i = pl.program_id(1)
    zeros = jnp.zeros((HEAD_DIM, TQ), _BF16)
    units = [(j, g) for j in range(A_SUB) for g in range(A_KV_HEADS)]

    def bands(j):
        cur = slice(j * TQ, (j + 1) * TQ)
        old = slice((j - 1) * TQ, j * TQ)
        k_old = kp_ref[0] if j == 0 else kc_ref[0, old, :]
        v_old = vp_ref[0] if j == 0 else vc_ref[0, :, old]
        return (jnp.concatenate([k_old, kc_ref[0, cur, :]], axis=0),
                jnp.concatenate([v_old, vc_ref[0, :, cur]], axis=1))

    scores = []
    for j, g in units:
        cur = slice(j * TQ, (j + 1) * TQ)
        cols = []
        for hh in range(A_GROUPS):
            h = g * A_GROUPS + hh
            parts = [zeros] * A_KV_HEADS
            parts[g] = q_ref[0, h * HEAD_DIM:(h + 1) * HEAD_DIM, cur]
            cols.append(jnp.concatenate(parts, axis=0))
        rhs = jnp.concatenate(cols, axis=1)
        scores.append(jnp.dot(bands(j)[0], rhs, preferred_element_type=_F32))

    pts, invs = [], []
    for (j, g), sc in zip(units, scores):
        probs, inv = [], []
        for hh in range(A_GROUPS):
            h = g * A_GROUPS + hh
            s = sc[:, hh * TQ:(hh + 1) * TQ] + bias_ref[h]
            if j == 0:
                s = jnp.concatenate([jnp.where(i > 0, s[:TQ], NEG), s[TQ:]], axis=0)
            sink = sink_ref[h:h + 1, :]
            m = jnp.maximum(jnp.max(s, axis=0, keepdims=True), sink)
            p = jnp.exp(s - m)
            denom = jnp.sum(p, axis=0, keepdims=True) + jnp.exp(sink - m)
            probs.append(p.astype(_BF16))
            inv.append(1.0 / denom)
        pts.append(jnp.concatenate(probs, axis=1))
        invs.append(inv)

    for (j, g), pt, inv in zip(units, pts, invs):
        cur = slice(j * TQ, (j + 1) * TQ)
        out = jnp.dot(bands(j)[1][g * HEAD_DIM:(g + 1) * HEAD_DIM, :], pt,
                      preferred_element_type=_F32)
        for hh in range(A_GROUPS):
            h = g * A_GROUPS + hh
            rows = slice(h * HEAD_DIM, (h + 1) * HEAD_DIM)
            o = out[:, hh * TQ:(hh + 1) * TQ] * inv[hh] * z_ref[0, rows, cur].astype(_F32)
            o_ref[0, rows, cur] = o.astype(_BF16)


def _mix_a(qa_t, ka_n, va_t, za_t, bias_a, sink_a):
    B, _, S = qa_t.shape
    tqa = A_SUB * TQ
    prev = lambda i: jnp.maximum(i * A_SUB - 1, 0)
    in_specs = [
        pl.BlockSpec((1, A_WIDTH, tqa), lambda b, i: (b, 0, i)),
        pl.BlockSpec((1, TQ, A_KV_WIDTH), lambda b, i: (b, prev(i), 0)),
        pl.BlockSpec((1, tqa, A_KV_WIDTH), lambda b, i: (b, i, 0)),
        pl.BlockSpec((1, A_KV_WIDTH, TQ), lambda b, i: (b, 0, prev(i))),
        pl.BlockSpec((1, A_KV_WIDTH, tqa), lambda b, i: (b, 0, i)),
        pl.BlockSpec((1, A_WIDTH, tqa), lambda b, i: (b, 0, i)),
        pl.BlockSpec((A_Q_HEADS, 2 * TQ, TQ), lambda b, i: (0, 0, 0)),
        pl.BlockSpec((A_Q_HEADS, TQ), lambda b, i: (0, 0)),
    ]
    return pl.pallas_call(
        _mix_a_kernel, grid=(B, S // tqa), in_specs=in_specs,
        out_specs=pl.BlockSpec((1, A_WIDTH, tqa), lambda b, i: (b, 0, i)),
        out_shape=jax.ShapeDtypeStruct((B, A_WIDTH, S), _BF16),
        compiler_params=pltpu.CompilerParams(
            dimension_semantics=("parallel", "parallel"), vmem_limit_bytes=VMEM_LIMIT),
        name="mixer_a",
    )(qa_t, ka_n, ka_n, va_t, va_t, za_t, bias_a, sink_a)


def _key_to_f32(key):
    return pltpu.bitcast(jnp.where(key < 0, key ^ jnp.int32(0x7FFFFFFF), key), _F32)


def _f32_to_key(v):
    bits = pltpu.bitcast(v, jnp.int32)
    return jnp.where(bits < 0, bits ^ jnp.int32(0x7FFFFFFF), bits)


def _colmin8(v):
    return jnp.min(v.reshape(v.shape[0] // 8, 8, v.shape[1]), axis=0)


def _colmax8(v):
    return jnp.max(v.reshape(v.shape[0] // 8, 8, v.shape[1]), axis=0)


def _by_pairs(n, step):
    def body(t, carry):
        for u in range(4):
            step(4 * t + u)
        return carry

    lax.fori_loop(0, n // 4, body, 0)
    base = (n // 4) * 4
    for rest in (1, 2, 3):
        @pl.when(n - base == rest)
        def _(rest=rest):
            for u in range(rest):
                step(base + u)


def _mix_b_kernel(qb_ref, kb_ref, vb_ref, zb_ref, qi_ref, ki_ref, wi_ref, bias_ref, o_ref,
                  keys_scr, mb_scr, sc_scr, acc_scr, m_scr, ext_scr, thr_scr, tie_scr):
    i = pl.program_id(1)
    nc = ((i + 1) * TQB + KCHUNK - 1) // KCHUNK
    ns = (nc + 1) // 2
    top_k = TOPK_MAX

    qi = qi_ref[0]
    zpad = jnp.zeros((LANES - IDX_DIM, TQB), _BF16)
    rhs_i = jnp.concatenate(
        [jnp.concatenate([qi[h * IDX_DIM:(h + 1) * IDX_DIM], zpad], axis=0)
         for h in range(IDX_HEADS)], axis=1)
    w = wi_ref[0] * (IDX_DIM ** -0.5 * IDX_HEADS ** -0.5)
    s_minus_t = (lax.broadcasted_iota(jnp.int32, (KCHUNK, TQB), 0)
                 - lax.broadcasted_iota(jnp.int32, (KCHUNK, TQB), 1))

    def index_step(c):
        r0 = pl.multiple_of(c * KCHUNK, KCHUNK)
        d = jnp.dot(ki_ref[0, pl.ds(r0, KCHUNK), :], rhs_i,
                    preferred_element_type=_F32)
        acc = w[0:1, :] * jnp.maximum(d[:, 0:TQB], 0.0)
        for h in range(1, IDX_HEADS):
            acc = acc + w[h:h + 1, :] * jnp.maximum(d[:, h * TQB:(h + 1) * TQB], 0.0)
        causal = s_minus_t <= (i * TQB - r0)
        score = jnp.where(causal, acc, -jnp.inf)
        keys_scr[pl.ds(r0, KCHUNK), :] = score
        ext_scr[0:8, :] = jnp.maximum(ext_scr[0:8, :], _colmax8(score))
        ext_scr[8:16, :] = jnp.minimum(ext_scr[8:16, :], _colmin8(jnp.where(causal, acc, jnp.inf)))

    ext_scr[0:8, :] = jnp.full((8, TQB), -jnp.inf, _F32)
    ext_scr[8:16, :] = jnp.full((8, TQB), jnp.inf, _F32)
    _by_pairs(nc, index_step)

    @pl.when(nc % 2 == 1)
    def _():
        keys_scr[pl.ds(pl.multiple_of(nc * KCHUNK, KCHUNK), KCHUNK), :] = jnp.full(
            (KCHUNK, TQB), -jnp.inf, _F32)

    def count_rows(src, hit):
        n_acc = 8

        def bump(accs, r0, n_rows):
            accs = list(accs)
            rows = src[pl.ds(r0, n_rows), :]
            for j in range(n_rows // 8):
                a = accs[j % n_acc]
                accs[j % n_acc] = jnp.where(hit(rows[j * 8:(j + 1) * 8]), a + 1, a)
            return tuple(accs)

        accs = lax.fori_loop(
            0, nc // 2, lambda c, accs: bump(accs, pl.multiple_of(c * KSUPER, KSUPER), KSUPER),
            tuple(jnp.zeros((8, TQB), jnp.int32) for _ in range(n_acc)))
        accs = lax.cond(nc % 2 == 1,
                        lambda accs: bump(accs, pl.multiple_of((nc - 1) * KCHUNK, KCHUNK), KCHUNK),
                        lambda accs: accs, accs)
        acc = accs[0]
        for a in accs[1:]:
            acc = acc + a
        return jnp.sum(acc, axis=0, keepdims=True)

    def count_ge(cand, strict=False):
        return count_rows(keys_scr, (lambda r: r > cand) if strict else (lambda r: r >= cand))

    @pl.when(i * TQB + TQB <= top_k)
    def _():
        thr_scr[...] = jnp.full((1, TQB), -jnp.inf, _F32)
        tie_scr[...] = jnp.full((1, TQB), -1, jnp.int32)

    @pl.when(i * TQB + TQB > top_k)
    def _():
        zero = jnp.zeros((1, TQB), _F32)
        cnt_nonneg = count_ge(zero)
        cnt_pos = count_ge(zero, strict=True)
        key_max = _f32_to_key(jnp.max(ext_scr[0:8, :], axis=0, keepdims=True))
        key_min = _f32_to_key(jnp.min(ext_scr[8:16, :], axis=0, keepdims=True))
        n_causal = i * TQB + 1 + lax.broadcasted_iota(jnp.int32, (1, TQB), 1)
        nonneg = cnt_nonneg >= top_k
        lo = jnp.where(nonneg, 0, key_min)
        cnt_lo = jnp.where(nonneg, cnt_nonneg, n_causal)
        hi = jnp.where(nonneg, jnp.where(cnt_pos < top_k, 1, key_max + 1), 0)

        def halve_values(_, state):
            lo_v, hi_v, cnt_lo = state
            mid = lo_v + (hi_v - lo_v) * 0.5
            mid = jnp.where(mid > lo_v, jnp.where(mid < hi_v, mid, lo_v), lo_v)
            cnt = count_ge(mid)
            ok = cnt >= top_k
            return jnp.where(ok, mid, lo_v), jnp.where(ok, hi_v, mid), jnp.where(ok, cnt, cnt_lo)

        def halve_keys(_, state):
            lo, hi, cnt_lo = state
            mid = lo + lax.shift_right_logical(hi - lo, 1)
            cnt = count_ge(_key_to_f32(mid))
            ok = cnt >= top_k
            return jnp.where(ok, mid, lo), jnp.where(ok, hi, mid), jnp.where(ok, cnt, cnt_lo)

        def finished(state):
            lo, hi, cnt_lo = state
            done = jnp.where(cnt_lo == top_k, 1, jnp.where(hi - lo <= 1, 1, 0))
            return jnp.min(done) == 1

        lo_v, hi_v, cnt_lo = lax.fori_loop(0, VALUE_PASSES, halve_values,
                                           (_key_to_f32(lo), _key_to_f32(hi), cnt_lo))
        state = (_f32_to_key(lo_v), _f32_to_key(hi_v), cnt_lo)
        stops = (0,) + SEARCH_CHECKS + (31,)
        state = lax.fori_loop(stops[0], stops[1], halve_keys, state)
        for first, last in zip(stops[1:-1], stops[2:]):
            state = lax.cond(finished(state), lambda s: s,
                             functools.partial(lax.fori_loop, first, last, halve_keys), state)
        lo, _, cnt_ge = state
        thr = _key_to_f32(lo)
        thr_scr[...] = thr
        tie_scr[...] = jnp.full((1, TQB), 2 ** 30, jnp.int32)

        @pl.when(jnp.max(cnt_ge) > top_k)
        def _():
            need = (top_k - count_ge(thr, strict=True)).astype(_F32)
            lower = (lax.broadcasted_iota(jnp.int32, (KCHUNK, KCHUNK), 0)
                     >= lax.broadcasted_iota(jnp.int32, (KCHUNK, KCHUNK), 1))
            lower = jnp.where(lower, 1.0, 0.0).astype(_BF16)

            def prefix_body(c, state):
                before, below = state
                r0 = pl.multiple_of(c * KCHUNK, KCHUNK)
                tied = jnp.where(keys_scr[pl.ds(r0, KCHUNK), :] == thr, 1.0, 0.0).astype(_BF16)
                upto = jnp.dot(lower, tied, preferred_element_type=_F32) + before
                below = below + jnp.sum(jnp.where(upto < need, 1, 0).reshape(KCHUNK // 8, 8, TQB), axis=0)
                return upto[KCHUNK - 1:KCHUNK, :], below

            _, below = lax.fori_loop(0, nc, prefix_body,
                                     (jnp.zeros((1, TQB), _F32), jnp.zeros((8, TQB), jnp.int32)))
            tie_scr[...] = jnp.sum(below, axis=0, keepdims=True)

    thr = thr_scr[...]
    tie = tie_scr[...]
    row = lax.broadcasted_iota(jnp.int32, (KSUPER, TQB), 0)

    def mask_body(c, carry):
        r0 = pl.multiple_of(c * KSUPER, KSUPER)
        k = keys_scr[pl.ds(r0, KSUPER), :]
        mb_scr[pl.ds(r0, KSUPER), :] = jnp.where(
            k == thr, jnp.where(row <= tie - r0, 0.0, NEG), jnp.where(k > thr, 0.0, NEG))
        return carry

    lax.fori_loop(0, ns, mask_body, 0)

    zeros = jnp.zeros((HEAD_DIM, TQB), _BF16)
    n_pairs = B_HEADS // 2
    rhs = []
    for p in range(n_pairs):
        q0 = qb_ref[0, (2 * p) * HEAD_DIM:(2 * p + 1) * HEAD_DIM, :]
        q1 = qb_ref[0, (2 * p + 1) * HEAD_DIM:(2 * p + 2) * HEAD_DIM, :]
        rhs.append(jnp.concatenate([jnp.concatenate([q0, zeros], axis=0),
                                    jnp.concatenate([zeros, q1], axis=0)], axis=1))

    def score_step(c):
        r0 = pl.multiple_of(c * KCHUNK, KCHUNK)
        for p in range(n_pairs):
            s = jnp.dot(kb_ref[0, pl.ds(r0, KCHUNK), p * LANES:(p + 1) * LANES], rhs[p],
                        preferred_element_type=_F32)
            for hh in range(2):
                h = 2 * p + hh
                parts = []
                for half in range(KCHUNK // TQ):
                    jb = c * (KCHUNK // TQ) + half
                    rs = pl.multiple_of(r0 + half * TQ, TQ)
                    subs = []
                    for sub in range(TQB // TQ):
                        kind = jnp.clip(jb - (i * (TQB // TQ) + sub) + 2, 0, 2)
                        lanes = slice(sub * TQ, (sub + 1) * TQ)
                        subs.append(s[half * TQ:(half + 1) * TQ, hh * TQB + sub * TQ:hh * TQB + (sub + 1) * TQ]
                                    + bias_ref[h, kind] + mb_scr[pl.ds(rs, TQ), lanes])
                    parts.append(jnp.concatenate(subs, axis=1))
                sh = jnp.concatenate(parts, axis=0)
                sc_scr[pl.ds(r0, KCHUNK), h * TQB:(h + 1) * TQB] = sh
                m_scr[h * 8:(h + 1) * 8, :] = jnp.maximum(m_scr[h * 8:(h + 1) * 8, :], _colmax8(sh))

    m_scr[...] = jnp.full(m_scr.shape, NEG, _F32)
    _by_pairs(nc, score_step)
    m_row = jnp.concatenate(
        [jnp.max(m_scr[h * 8:(h + 1) * 8, :], axis=0, keepdims=True) for h in range(B_HEADS)],
        axis=1)
    acc_scr[...] = jnp.zeros_like(acc_scr)

    def pv_step(c):
        r0 = pl.multiple_of(c * KCHUNK, KCHUNK)
        for p in range(n_pairs):
            cols = slice(2 * p * TQB, (2 * p + 2) * TQB)
            pr = jnp.exp2(sc_scr[pl.ds(r0, KCHUNK), cols] - m_row[:, cols]).astype(_BF16)
            acc_scr[p] += jnp.dot(vb_ref[0, c, p * PAIR_ROWS:(p + 1) * PAIR_ROWS, :], pr,
                                  preferred_element_type=_F32)

    _by_pairs(nc, pv_step)
    for p in range(n_pairs):
        inv = 1.0 / acc_scr[p, 2 * HEAD_DIM:2 * HEAD_DIM + 1, :]
        for hh in range(2):
            rows = slice((2 * p + hh) * HEAD_DIM, (2 * p + hh + 1) * HEAD_DIM)
            o = (acc_scr[p, hh * HEAD_DIM:(hh + 1) * HEAD_DIM, hh * TQB:(hh + 1) * TQB]
                 * inv[:, hh * TQB:(hh + 1) * TQB] * zb_ref[0, rows, :].astype(_F32))
            o_ref[0, rows, :] = o.astype(_BF16)


def _mix_b(qb_t, kb_n, vb_c, zb_t, qi_t, ki_n, wi_t, bias_b):
    B, _, S = qb_t.shape
    nq = S // TQB
    per_q = lambda rows: pl.BlockSpec((1, rows, TQB), lambda b, i: (b, 0, i))
    in_specs = [
        per_q(B_WIDTH),
        pl.BlockSpec((1, S, B_WIDTH), lambda b, i: (b, 0, 0)),
        pl.BlockSpec((1, S // KCHUNK, V_ROWS, KCHUNK), lambda b, i: (b, 0, 0, 0)),
        per_q(B_WIDTH),
        per_q(IDX_WIDTH),
        pl.BlockSpec((1, S, LANES), lambda b, i: (b, 0, 0)),
        per_q(IDX_HEADS),
        pl.BlockSpec((B_HEADS, 3, TQ, TQ), lambda b, i: (0, 0, 0, 0)),
    ]
    scratch = [
        pltpu.VMEM((S, TQB), _F32),
        pltpu.VMEM((S, TQB), _F32),
        pltpu.VMEM((S, B_HEADS * TQB), _F32),
        pltpu.VMEM((B_HEADS // 2, PAIR_ROWS, 2 * TQB), _F32),
        pltpu.VMEM((B_HEADS * 8, TQB), _F32),
        pltpu.VMEM((16, TQB), _F32),
        pltpu.VMEM((1, TQB), _F32),
        pltpu.VMEM((1, TQB), jnp.int32),
    ]
    return pl.pallas_call(
        _mix_b_kernel, grid=(B, nq), in_specs=in_specs,
        out_specs=per_q(B_WIDTH),
        out_shape=jax.ShapeDtypeStruct((B, B_WIDTH, S), _BF16),
        scratch_shapes=scratch,
        compiler_params=pltpu.CompilerParams(
            dimension_semantics=("parallel", "arbitrary"), vmem_limit_bytes=VMEM_LIMIT),
        name="mixer_b",
    )(qb_t, kb_n, vb_c, zb_t, qi_t, ki_n, wi_t, bias_b)


def _merge_kernel(x_ref, ya_ref, yb_ref, gt_ref, wpa_ref, wpb_ref, wo_ref, o_ref):
    pa = jnp.dot(wpa_ref[...], ya_ref[0], preferred_element_type=_F32)
    pb = jnp.dot(wpb_ref[...], yb_ref[0], preferred_element_type=_F32)
    merged = (gt_ref[0, :D_MODEL, :].astype(_F32) * pa
              + gt_ref[0, D_MODEL:, :].astype(_F32) * pb).astype(_BF16)
    out_t = jnp.dot(wo_ref[...], merged, preferred_element_type=_F32)
    o_ref[0] = x_ref[0] + out_t.T


def _merge(x, ya_t, yb_t, gt_t, wpa_t, wpb_t, wo_t):
    B, S, D = x.shape
    tm = TM_PROJ
    const = lambda b, t: (0, 0)
    in_specs = [
        pl.BlockSpec((1, tm, D), lambda b, t: (b, t, 0)),
        pl.BlockSpec((1, A_WIDTH, tm), lambda b, t: (b, 0, t)),
        pl.BlockSpec((1, B_WIDTH, tm), lambda b, t: (b, 0, t)),
        pl.BlockSpec((1, 2 * D, tm), lambda b, t: (b, 0, t)),
        pl.BlockSpec((D, A_WIDTH), const),
        pl.BlockSpec((D, B_WIDTH), const),
        pl.BlockSpec((D, D), const),
    ]
    return pl.pallas_call(
        _merge_kernel, grid=(B, S // tm), in_specs=in_specs,
        out_specs=pl.BlockSpec((1, tm, D), lambda b, t: (b, t, 0)),
        out_shape=jax.ShapeDtypeStruct((B, S, D), x.dtype),
        compiler_params=pltpu.CompilerParams(
            dimension_semantics=("parallel", "parallel"), vmem_limit_bytes=VMEM_LIMIT),
        name="merge_out",
    )(x, ya_t, yb_t, gt_t, wpa_t, wpb_t, wo_t)


def _t5_bucket(n):
    n = np.maximum(n, 0)
    max_exact = N_BUCKETS // 2
    nf = np.maximum(n, 1).astype(np.float64)
    large = max_exact + np.floor(np.log(nf / max_exact) / math.log(MAX_DISTANCE / max_exact)
                                 * (N_BUCKETS - max_exact)).astype(np.int64)
    large = np.minimum(large, N_BUCKETS - 1)
    return np.where(n < max_exact, n, large)


def _bias_tables(rel_bias):
    t = np.arange(TQ)[None, :]
    s = np.arange(TQ)[:, None]
    d_prev = t + TQ - s
    d_cur = t - s
    table_a = rel_bias[:, :A_Q_HEADS].astype(_F32).T
    table_b = rel_bias[:, A_Q_HEADS:].astype(_F32).T

    def tile(table, dist, windowed):
        onehot = (_t5_bucket(dist)[..., None] == np.arange(N_BUCKETS)).astype(np.float32)
        b = jnp.einsum("stk,hk->hst", onehot, table, precision=lax.Precision.HIGHEST)
        if windowed:
            b = jnp.where(((dist >= 0) & (dist < WINDOW))[None], b, NEG)
        return b

    bias_a = jnp.concatenate([tile(table_a, d_prev, True), tile(table_a, d_cur, True)], axis=1)
    far = jnp.broadcast_to(table_b[:, N_BUCKETS - 1][:, None, None], (B_HEADS, TQ, TQ))
    bias_b = jnp.stack([far, tile(table_b, d_prev, False), tile(table_b, d_cur, False)], axis=1) * LOG2E
    return bias_a, bias_b


def kernel(x, norm_g, w_in, qnorm_a, knorm_a, sinks_a, qnorm_b, knorm_b, rel_bias,
           w_proj_a, w_proj_b, w_out):
    assert norm_g.shape[0] == 1, "single-layer block"
    B, S, D = x.shape
    assert D == D_MODEL and S % TM_PROJ == 0 and S % KSUPER == 0 and S % TQB == 0 and TQB % TQ == 0
    assert MAX_DISTANCE <= TQ

    wt = w_in[0].T
    n_real = _SEG["kw"][0] + KW_REAL
    wt = jnp.concatenate(
        [wt[:n_real], jnp.zeros((_SEG["kw"][1] - n_real, D), wt.dtype), wt[n_real:]], axis=0)
    wt = wt.astype(_BF16)
    bcast = lambda g, scale: jnp.broadcast_to((g.astype(_F32) * scale)[:, None], (HEAD_DIM, TM_PROJ))
    q_scale = HEAD_DIM ** -0.5
    (qa_t, ka_n, va_t, za_t, qb_t, kb_n, vb_c, zb_t, qi_t, ki_n, wi_t, gt_t) = _inproj(
        x, norm_g[0][None, :].astype(_F32), wt,
        bcast(qnorm_a[0], q_scale), bcast(knorm_a[0], 1.0),
        bcast(qnorm_b[0], q_scale * LOG2E), bcast(knorm_b[0], 1.0))

    bias_a, bias_b = _bias_tables(rel_bias)
    sink_a = jnp.broadcast_to(sinks_a[0].astype(_F32)[:, None], (A_Q_HEADS, TQ))
    ya_t = _mix_a(qa_t, ka_n, va_t, za_t, bias_a, sink_a)
    yb_t = _mix_b(qb_t, kb_n, vb_c, zb_t, qi_t, ki_n, wi_t, bias_b)
    return _merge(x, ya_t, yb_t, gt_t,
                  w_proj_a[0].T.astype(_BF16), w_proj_b[0].T.astype(_BF16), w_out[0].T.astype(_BF16))
```

```python
import functools
import math

import jax
import jax.numpy as jnp
import numpy as np
from jax import lax
from jax.experimental import pallas as pl
from jax.experimental.pallas import tpu as pltpu

D_MODEL = 1024
HEAD_DIM = 64
A_Q_HEADS = 8
A_KV_HEADS = 2
A_GROUPS = A_Q_HEADS // A_KV_HEADS
A_WIDTH = A_Q_HEADS * HEAD_DIM
A_KV_WIDTH = A_KV_HEADS * HEAD_DIM
B_HEADS = 8
B_WIDTH = B_HEADS * HEAD_DIM
IDX_HEADS = 8
IDX_DIM = 32
IDX_WIDTH = IDX_HEADS * IDX_DIM
WINDOW = 128
TOPK_MAX = 256
N_BUCKETS = 32
MAX_DISTANCE = 128
RMS_EPS = 1e-6

LANES = 128
TQ = 128
A_SUB = 4
TQB = 256
KCHUNK = 256
KSUPER = 2 * KCHUNK
VALUE_PASSES = 16
SEARCH_CHECKS = (1, 3, 5, 8)
TM_PROJ = 1024
NEG = -1e30
LOG2E = math.log2(math.e)
ONES_ROWS = 16
PAIR_ROWS = 2 * HEAD_DIM + ONES_ROWS
V_ROWS = (B_HEADS // 2) * PAIR_ROWS
VMEM_LIMIT = 56 * 1024 * 1024

_SEG = {}
_off = 0
for _name, _rows in (("qa", A_WIDTH), ("ka", A_KV_WIDTH), ("va", A_KV_WIDTH), ("za", A_WIDTH),
                     ("qb", B_WIDTH), ("kb", B_WIDTH), ("vb", B_WIDTH), ("zb", B_WIDTH),
                     ("qi", IDX_WIDTH), ("kw", LANES), ("gates", 2 * D_MODEL)):
    _SEG[_name] = (_off, _off + _rows)
    _off += _rows
PROJ_ROWS = _off
KW_REAL = IDX_DIM + IDX_HEADS

_F32 = jnp.float32
_BF16 = jnp.bfloat16
_NT = (((1,), (1,)), ((), ()))


def _inproj_kernel(x_ref, g_ref, wt_ref, gqa_ref, gka_ref, gqb_ref, gkb_ref,
                   qa_o, ka_o, va_o, za_o, qb_o, kb_o, vb_o, zb_o, qi_o, ki_o, wi_o, gt_o):
    tm = x_ref.shape[1]
    x = x_ref[0]
    ms = jnp.mean(x * x, axis=-1, keepdims=True)
    h = (x * lax.rsqrt(ms + RMS_EPS) * g_ref[...]).astype(_BF16)

    def proj(lo, hi):
        return lax.dot_general(wt_ref[lo:hi, :], h, _NT, preferred_element_type=_F32)

    def seg(name):
        return proj(*_SEG[name])

    def headnorm(p, gain_ref):
        nh = p.shape[0] // HEAD_DIM
        p3 = p.reshape(nh, HEAD_DIM, tm)
        r = lax.rsqrt(jnp.mean(p3 * p3, axis=1, keepdims=True) + RMS_EPS)
        return (p3 * r * gain_ref[...][None]).reshape(nh * HEAD_DIM, tm)

    def silu(p):
        return p * jax.nn.sigmoid(p)

    qa_o[0] = headnorm(seg("qa"), gqa_ref).astype(_BF16)
    ka_o[0] = headnorm(seg("ka"), gka_ref).T.astype(_BF16)
    va_o[0] = seg("va").astype(_BF16)
    za_o[0] = silu(seg("za")).astype(_BF16)
    qb_o[0] = headnorm(seg("qb"), gqb_ref).astype(_BF16)
    kb_o[0] = headnorm(seg("kb"), gkb_ref).T.astype(_BF16)
    vb = seg("vb").astype(_BF16)
    ones = jnp.ones((ONES_ROWS, KCHUNK), _BF16)
    for c in range(tm // KCHUNK):
        for p in range(B_HEADS // 2):
            vb_o[0, c, p * PAIR_ROWS:(p + 1) * PAIR_ROWS, :] = jnp.concatenate(
                [vb[p * LANES:(p + 1) * LANES, c * KCHUNK:(c + 1) * KCHUNK], ones], axis=0)
    zb_o[0] = silu(seg("zb")).astype(_BF16)
    qi_o[0] = seg("qi").astype(_BF16)
    kw = seg("kw")
    wi_o[0] = kw[IDX_DIM:KW_REAL, :]
    row = lax.broadcasted_iota(jnp.int32, kw.shape, 0)
    ki_o[0] = jnp.where(row < IDX_DIM, kw, 0.0).T.astype(_BF16)
    g_lo = _SEG["gates"][0]
    for c in range(2 * D_MODEL // 512):
        gt_o[0, c * 512:(c + 1) * 512, :] = jax.nn.sigmoid(
            proj(g_lo + c * 512, g_lo + (c + 1) * 512)).astype(_BF16)


def _inproj(x, g, wt, gqa, gka, gqb, gkb):
    B, S, D = x.shape
    tm = TM_PROJ
    nt = S // tm
    const = lambda b, t: (0, 0)
    tr = lambda rows: pl.BlockSpec((1, rows, tm), lambda b, t: (b, 0, t))
    nat = lambda cols: pl.BlockSpec((1, tm, cols), lambda b, t: (b, t, 0))
    sds = jax.ShapeDtypeStruct
    out_shape = (
        sds((B, A_WIDTH, S), _BF16), sds((B, S, A_KV_WIDTH), _BF16), sds((B, A_KV_WIDTH, S), _BF16),
        sds((B, A_WIDTH, S), _BF16),
        sds((B, B_WIDTH, S), _BF16), sds((B, S, B_WIDTH), _BF16),
        sds((B, S // KCHUNK, V_ROWS, KCHUNK), _BF16), sds((B, B_WIDTH, S), _BF16),
        sds((B, IDX_WIDTH, S), _BF16), sds((B, S, LANES), _BF16), sds((B, IDX_HEADS, S), _F32),
        sds((B, 2 * D_MODEL, S), _BF16),
    )
    out_specs = (
        tr(A_WIDTH), nat(A_KV_WIDTH), tr(A_KV_WIDTH), tr(A_WIDTH),
        tr(B_WIDTH), nat(B_WIDTH),
        pl.BlockSpec((1, tm // KCHUNK, V_ROWS, KCHUNK), lambda b, t: (b, t, 0, 0)), tr(B_WIDTH),
        tr(IDX_WIDTH), nat(LANES), tr(IDX_HEADS), tr(2 * D_MODEL),
    )
    in_specs = [
        pl.BlockSpec((1, tm, D), lambda b, t: (b, t, 0)),
        pl.BlockSpec((1, D), const),
        pl.BlockSpec((PROJ_ROWS, D), const, pipeline_mode=pl.Buffered(1)),
        pl.BlockSpec((HEAD_DIM, tm), const), pl.BlockSpec((HEAD_DIM, tm), const),
        pl.BlockSpec((HEAD_DIM, tm), const), pl.BlockSpec((HEAD_DIM, tm), const),
    ]
    return pl.pallas_call(
        _inproj_kernel, grid=(B, nt), in_specs=in_specs, out_specs=out_specs, out_shape=out_shape,
        compiler_params=pltpu.CompilerParams(
            dimension_semantics=("parallel", "parallel"), vmem_limit_bytes=VMEM_LIMIT),
        name="inproj",
    )(x, g, wt, gqa, gka, gqb, gkb)


def _mix_a_kernel(q_ref, kp_ref, kc_ref, vp_ref, vc_ref, z_ref, bias_ref, sink_ref, o_ref):
    i = pl.program_id(1)
    zeros = jnp.zeros((HEAD_DIM, TQ), _BF16)
    units = [(j, g) for j in range(A_SUB) for g in range(A_KV_HEADS)]

    def bands(j):
        cur = slice(j * TQ, (j + 1) * TQ)
        old = slice((j - 1) * TQ, j * TQ)
        k_old = kp_ref[0] if j == 0 else kc_ref[0, old, :]
        v_old = vp_ref[0] if j == 0 else vc_ref[0, :, old]
        return (jnp.concatenate([k_old, kc_ref[0, cur, :]], axis=0),
                jnp.concatenate([v_old, vc_ref[0, :, cur]], axis=1))

    scores = []
    for j, g in units:
        cur = slice(j * TQ, (j + 1) * TQ)
        cols = []
        for hh in range(A_GROUPS):
            h = g * A_GROUPS + hh
            parts = [zeros] * A_KV_HEADS
            parts[g] = q_ref[0, h * HEAD_DIM:(h + 1) * HEAD_DIM, cur]
            cols.append(jnp.concatenate(parts, axis=0))
        rhs = jnp.concatenate(cols, axis=1)
        scores.append(jnp.dot(bands(j)[0], rhs, preferred_element_type=_F32))

    pts, invs = [], []
    for (j, g), sc in zip(units, scores):
        probs, inv = [], []
        for hh in range(A_GROUPS):
            h = g * A_GROUPS + hh
            s = sc[:, hh * TQ:(hh + 1) * TQ] + bias_ref[h]
            if j == 0:
                s = jnp.concatenate([jnp.where(i > 0, s[:TQ], NEG), s[TQ:]], axis=0)
            sink = sink_ref[h:h + 1, :]
            m = jnp.maximum(jnp.max(s, axis=0, keepdims=True), sink)
            p = jnp.exp(s - m)
            denom = jnp.sum(p, axis=0, keepdims=True) + jnp.exp(sink - m)
            probs.append(p.astype(_BF16))
            inv.append(1.0 / denom)
        pts.append(jnp.concatenate(probs, axis=1))
        invs.append(inv)

    for (j, g), pt, inv in zip(units, pts, invs):
        cur = slice(j * TQ, (j + 1) * TQ)
        out = jnp.dot(bands(j)[1][g * HEAD_DIM:(g + 1) * HEAD_DIM, :], pt,
                      preferred_element_type=_F32)
        for hh in range(A_GROUPS):
            h = g * A_GROUPS + hh
            rows = slice(h * HEAD_DIM, (h + 1) * HEAD_DIM)
            o = out[:, hh * TQ:(hh + 1) * TQ] * inv[hh] * z_ref[0, rows, cur].astype(_F32)
            o_ref[0, rows, cur] = o.astype(_BF16)


def _mix_a(qa_t, ka_n, va_t, za_t, bias_a, sink_a):
    B, _, S = qa_t.shape
    tqa = A_SUB * TQ
    prev = lambda i: jnp.maximum(i * A_SUB - 1, 0)
    in_specs = [
        pl.BlockSpec((1, A_WIDTH, tqa), lambda b, i: (b, 0, i)),
        pl.BlockSpec((1, TQ, A_KV_WIDTH), lambda b, i: (b, prev(i), 0)),
        pl.BlockSpec((1, tqa, A_KV_WIDTH), lambda b, i: (b, i, 0)),
        pl.BlockSpec((1, A_KV_WIDTH, TQ), lambda b, i: (b, 0, prev(i))),
        pl.BlockSpec((1, A_KV_WIDTH, tqa), lambda b, i: (b, 0, i)),
        pl.BlockSpec((1, A_WIDTH, tqa), lambda b, i: (b, 0, i)),
        pl.BlockSpec((A_Q_HEADS, 2 * TQ, TQ), lambda b, i: (0, 0, 0)),
        pl.BlockSpec((A_Q_HEADS, TQ), lambda b, i: (0, 0)),
    ]
    return pl.pallas_call(
        _mix_a_kernel, grid=(B, S // tqa), in_specs=in_specs,
        out_specs=pl.BlockSpec((1, A_WIDTH, tqa), lambda b, i: (b, 0, i)),
        out_shape=jax.ShapeDtypeStruct((B, A_WIDTH, S), _BF16),
        compiler_params=pltpu.CompilerParams(
            dimension_semantics=("parallel", "parallel"), vmem_limit_bytes=VMEM_LIMIT),
        name="mixer_a",
    )(qa_t, ka_n, ka_n, va_t, va_t, za_t, bias_a, sink_a)


def _key_to_f32(key):
    return pltpu.bitcast(jnp.where(key < 0, key ^ jnp.int32(0x7FFFFFFF), key), _F32)


def _f32_to_key(v):
    bits = pltpu.bitcast(v, jnp.int32)
    return jnp.where(bits < 0, bits ^ jnp.int32(0x7FFFFFFF), bits)


def _colmin8(v):
    return jnp.min(v.reshape(v.shape[0] // 8, 8, v.shape[1]), axis=0)


def _colmax8(v):
    return jnp.max(v.reshape(v.shape[0] // 8, 8, v.shape[1]), axis=0)


def _by_pairs(n, step):
    def body(t, carry):
        for u in range(4):
            step(4 * t + u)
        return carry

    lax.fori_loop(0, n // 4, body, 0)
    base = (n // 4) * 4
    for rest in (1, 2, 3):
        @pl.when(n - base == rest)
        def _(rest=rest):
            for u in range(rest):
                step(base + u)


def _mix_b_kernel(qb_ref, kb_ref, vb_ref, zb_ref, qi_ref, ki_ref, wi_ref, bias_ref, o_ref,
                  keys_scr, mb_scr, sc_scr, acc_scr, m_scr, ext_scr, thr_scr, tie_scr):
    i = pl.program_id(1)
    nc = ((i + 1) * TQB + KCHUNK - 1) // KCHUNK
    ns = (nc + 1) // 2
    top_k = TOPK_MAX

    qi = qi_ref[0]
    zpad = jnp.zeros((LANES - IDX_DIM, TQB), _BF16)
    rhs_i = jnp.concatenate(
        [jnp.concatenate([qi[h * IDX_DIM:(h + 1) * IDX_DIM], zpad], axis=0)
         for h in range(IDX_HEADS)], axis=1)
    w = wi_ref[0] * (IDX_DIM ** -0.5 * IDX_HEADS ** -0.5)
    s_minus_t = (lax.broadcasted_iota(jnp.int32, (KCHUNK, TQB), 0)
                 - lax.broadcasted_iota(jnp.int32, (KCHUNK, TQB), 1))

    def index_step(c):
        r0 = pl.multiple_of(c * KCHUNK, KCHUNK)
        d = jnp.dot(ki_ref[0, pl.ds(r0, KCHUNK), :], rhs_i,
                    preferred_element_type=_F32)
        acc = w[0:1, :] * jnp.maximum(d[:, 0:TQB], 0.0)
        for h in range(1, IDX_HEADS):
            acc = acc + w[h:h + 1, :] * jnp.maximum(d[:, h * TQB:(h + 1) * TQB], 0.0)
        causal = s_minus_t <= (i * TQB - r0)
        score = jnp.where(causal, acc, -jnp.inf)
        keys_scr[pl.ds(r0, KCHUNK), :] = score
        ext_scr[0:8, :] = jnp.maximum(ext_scr[0:8, :], _colmax8(score))
        ext_scr[8:16, :] = jnp.minimum(ext_scr[8:16, :], _colmin8(jnp.where(causal, acc, jnp.inf)))

    ext_scr[0:8, :] = jnp.full((8, TQB), -jnp.inf, _F32)
    ext_scr[8:16, :] = jnp.full((8, TQB), jnp.inf, _F32)
    _by_pairs(nc, index_step)

    @pl.when(nc % 2 == 1)
    def _():
        keys_scr[pl.ds(pl.multiple_of(nc * KCHUNK, KCHUNK), KCHUNK), :] = jnp.full(
            (KCHUNK, TQB), -jnp.inf, _F32)

    def count_rows(src, hit):
        n_acc = 8

        def bump(accs, r0, n_rows):
            accs = list(accs)
            rows = src[pl.ds(r0, n_rows), :]
            for j in range(n_rows // 8):
                a = accs[j % n_acc]
                accs[j % n_acc] = jnp.where(hit(rows[j * 8:(j + 1) * 8]), a + 1, a)
            return tuple(accs)

        accs = lax.fori_loop(
            0, nc // 2, lambda c, accs: bump(accs, pl.multiple_of(c * KSUPER, KSUPER), KSUPER),
            tuple(jnp.zeros((8, TQB), jnp.int32) for _ in range(n_acc)))
        accs = lax.cond(nc % 2 == 1,
                        lambda accs: bump(accs, pl.multiple_of((nc - 1) * KCHUNK, KCHUNK), KCHUNK),
                        lambda accs: accs, accs)
        acc = accs[0]
        for a in accs[1:]:
            acc = acc + a
        return jnp.sum(acc, axis=0, keepdims=True)

    def count_ge(cand, strict=False):
        return count_rows(keys_scr, (lambda r: r > cand) if strict else (lambda r: r >= cand))

    @pl.when(i * TQB + TQB <= top_k)
    def _():
        thr_scr[...] = jnp.full((1, TQB), -jnp.inf, _F32)
        tie_scr[...] = jnp.full((1, TQB), -1, jnp.int32)

    @pl.when(i * TQB + TQB > top_k)
    def _():
        zero = jnp.zeros((1, TQB), _F32)
        cnt_nonneg = count_ge(zero)
        cnt_pos = count_ge(zero, strict=True)
        key_max = _f32_to_key(jnp.max(ext_scr[0:8, :], axis=0, keepdims=True))
        key_min = _f32_to_key(jnp.min(ext_scr[8:16, :], axis=0, keepdims=True))
        n_causal = i * TQB + 1 + lax.broadcasted_iota(jnp.int32, (1, TQB), 1)
        nonneg = cnt_nonneg >= top_k
        lo = jnp.where(nonneg, 0, key_min)
        cnt_lo = jnp.where(nonneg, cnt_nonneg, n_causal)
        hi = jnp.where(nonneg, jnp.where(cnt_pos < top_k, 1, key_max + 1), 0)

        def halve_values(_, state):
            lo_v, hi_v, cnt_lo = state
            mid = lo_v + (hi_v - lo_v) * 0.5
            mid = jnp.where(mid > lo_v, jnp.where(mid < hi_v, mid, lo_v), lo_v)
            cnt = count_ge(mid)
            ok = cnt >= top_k
            return jnp.where(ok, mid, lo_v), jnp.where(ok, hi_v, mid), jnp.where(ok, cnt, cnt_lo)

        def halve_keys(_, state):
            lo, hi, cnt_lo = state
            mid = lo + lax.shift_right_logical(hi - lo, 1)
            cnt = count_ge(_key_to_f32(mid))
            ok = cnt >= top_k
            return jnp.where(ok, mid, lo), jnp.where(ok, hi, mid), jnp.where(ok, cnt, cnt_lo)

        def finished(state):
            lo, hi, cnt_lo = state
            done = jnp.where(cnt_lo == top_k, 1, jnp.where(hi - lo <= 1, 1, 0))
            return jnp.min(done) == 1

        lo_v, hi_v, cnt_lo = lax.fori_loop(0, VALUE_PASSES, halve_values,
                                           (_key_to_f32(lo), _key_to_f32(hi), cnt_lo))
        state = (_f32_to_key(lo_v), _f32_to_key(hi_v), cnt_lo)
        stops = (0,) + SEARCH_CHECKS + (31,)
        state = lax.fori_loop(stops[0], stops[1], halve_keys, state)
        for first, last in zip(stops[1:-1], stops[2:]):
            state = lax.cond(finished(state), lambda s: s,
                             functools.partial(lax.fori_loop, first, last, halve_keys), state)
        lo, _, cnt_ge = state
        thr = _key_to_f32(lo)
        thr_scr[...] = thr
        tie_scr[...] = jnp.full((1, TQB), 2 ** 30, jnp.int32)

        @pl.when(jnp.max(cnt_ge) > top_k)
        def _():
            need = (top_k - count_ge(thr, strict=True)).astype(_F32)
            lower = (lax.broadcasted_iota(jnp.int32, (KCHUNK, KCHUNK), 0)
                     >= lax.broadcasted_iota(jnp.int32, (KCHUNK, KCHUNK), 1))
            lower = jnp.where(lower, 1.0, 0.0).astype(_BF16)

            def prefix_body(c, state):
                before, below = state
                r0 = pl.multiple_of(c * KCHUNK, KCHUNK)
                tied = jnp.where(keys_scr[pl.ds(r0, KCHUNK), :] == thr, 1.0, 0.0).astype(_BF16)
                upto = jnp.dot(lower, tied, preferred_element_type=_F32) + before
                below = below + jnp.sum(jnp.where(upto < need, 1, 0).reshape(KCHUNK // 8, 8, TQB), axis=0)
                return upto[KCHUNK - 1:KCHUNK, :], below

            _, below = lax.fori_loop(0, nc, prefix_body,
                                     (jnp.zeros((1, TQB), _F32), jnp.zeros((8, TQB), jnp.int32)))
            tie_scr[...] = jnp.sum(below, axis=0, keepdims=True)

    thr = thr_scr[...]
    tie = tie_scr[...]
    row = lax.broadcasted_iota(jnp.int32, (KSUPER, TQB), 0)

    def mask_body(c, carry):
        r0 = pl.multiple_of(c * KSUPER, KSUPER)
        k = keys_scr[pl.ds(r0, KSUPER), :]
        mb_scr[pl.ds(r0, KSUPER), :] = jnp.where(
            k == thr, jnp.where(row <= tie - r0, 0.0, NEG), jnp.where(k > thr, 0.0, NEG))
        return carry

    lax.fori_loop(0, ns, mask_body, 0)

    zeros = jnp.zeros((HEAD_DIM, TQB), _BF16)
    n_pairs = B_HEADS // 2
    rhs = []
    for p in range(n_pairs):
        q0 = qb_ref[0, (2 * p) * HEAD_DIM:(2 * p + 1) * HEAD_DIM, :]
        q1 = qb_ref[0, (2 * p + 1) * HEAD_DIM:(2 * p + 2) * HEAD_DIM, :]
        rhs.append(jnp.concatenate([jnp.concatenate([q0, zeros], axis=0),
                                    jnp.concatenate([zeros, q1], axis=0)], axis=1))

    def score_step(c):
        r0 = pl.multiple_of(c * KCHUNK, KCHUNK)
        for p in range(n_pairs):
            s = jnp.dot(kb_ref[0, pl.ds(r0, KCHUNK), p * LANES:(p + 1) * LANES], rhs[p],
                        preferred_element_type=_F32)
            for hh in range(2):
                h = 2 * p + hh
                parts = []
                for half in range(KCHUNK // TQ):
                    jb = c * (KCHUNK // TQ) + half
                    rs = pl.multiple_of(r0 + half * TQ, TQ)
                    subs = []
                    for sub in range(TQB // TQ):
                        kind = jnp.clip(jb - (i * (TQB // TQ) + sub) + 2, 0, 2)
                        lanes = slice(sub * TQ, (sub + 1) * TQ)
                        subs.append(s[half * TQ:(half + 1) * TQ, hh * TQB + sub * TQ:hh * TQB + (sub + 1) * TQ]
                                    + bias_ref[h, kind] + mb_scr[pl.ds(rs, TQ), lanes])
                    parts.append(jnp.concatenate(subs, axis=1))
                sh = jnp.concatenate(parts, axis=0)
                sc_scr[pl.ds(r0, KCHUNK), h * TQB:(h + 1) * TQB] = sh
                m_scr[h * 8:(h + 1) * 8, :] = jnp.maximum(m_scr[h * 8:(h + 1) * 8, :], _colmax8(sh))

    m_scr[...] = jnp.full(m_scr.shape, NEG, _F32)
    _by_pairs(nc, score_step)
    m_row = jnp.concatenate(
        [jnp.max(m_scr[h * 8:(h + 1) * 8, :], axis=0, keepdims=True) for h in range(B_HEADS)],
        axis=1)
    acc_scr[...] = jnp.zeros_like(acc_scr)

    def pv_step(c):
        r0 = pl.multiple_of(c * KCHUNK, KCHUNK)
        for p in range(n_pairs):
            cols = slice(2 * p * TQB, (2 * p + 2) * TQB)
            pr = jnp.exp2(sc_scr[pl.ds(r0, KCHUNK), cols] - m_row[:, cols]).astype(_BF16)
            acc_scr[p] += jnp.dot(vb_ref[0, c, p * PAIR_ROWS:(p + 1) * PAIR_ROWS, :], pr,
                                  preferred_element_type=_F32)

    _by_pairs(nc, pv_step)
    for p in range(n_pairs):
        inv = 1.0 / acc_scr[p, 2 * HEAD_DIM:2 * HEAD_DIM + 1, :]
        for hh in range(2):
            rows = slice((2 * p + hh) * HEAD_DIM, (2 * p + hh + 1) * HEAD_DIM)
            o = (acc_scr[p, hh * HEAD_DIM:(hh + 1) * HEAD_DIM, hh * TQB:(hh + 1) * TQB]
                 * inv[:, hh * TQB:(hh + 1) * TQB] * zb_ref[0, rows, :].astype(_F32))
            o_ref[0, rows, :] = o.astype(_BF16)


def _mix_b(qb_t, kb_n, vb_c, zb_t, qi_t, ki_n, wi_t, bias_b):
    B, _, S = qb_t.shape
    nq = S // TQB
    per_q = lambda rows: pl.BlockSpec((1, rows, TQB), lambda b, i: (b, 0, i))
    in_specs = [
        per_q(B_WIDTH),
        pl.BlockSpec((1, S, B_WIDTH), lambda b, i: (b, 0, 0)),
        pl.BlockSpec((1, S // KCHUNK, V_ROWS, KCHUNK), lambda b, i: (b, 0, 0, 0)),
        per_q(B_WIDTH),
        per_q(IDX_WIDTH),
        pl.BlockSpec((1, S, LANES), lambda b, i: (b, 0, 0)),
        per_q(IDX_HEADS),
        pl.BlockSpec((B_HEADS, 3, TQ, TQ), lambda b, i: (0, 0, 0, 0)),
    ]
    scratch = [
        pltpu.VMEM((S, TQB), _F32),
        pltpu.VMEM((S, TQB), _F32),
        pltpu.VMEM((S, B_HEADS * TQB), _F32),
        pltpu.VMEM((B_HEADS // 2, PAIR_ROWS, 2 * TQB), _F32),
        pltpu.VMEM((B_HEADS * 8, TQB), _F32),
        pltpu.VMEM((16, TQB), _F32),
        pltpu.VMEM((1, TQB), _F32),
        pltpu.VMEM((1, TQB), jnp.int32),
    ]
    return pl.pallas_call(
        _mix_b_kernel, grid=(B, nq), in_specs=in_specs,
        out_specs=per_q(B_WIDTH),
        out_shape=jax.ShapeDtypeStruct((B, B_WIDTH, S), _BF16),
        scratch_shapes=scratch,
        compiler_params=pltpu.CompilerParams(
            dimension_semantics=("parallel", "arbitrary"), vmem_limit_bytes=VMEM_LIMIT),
        name="mixer_b",
    )(qb_t, kb_n, vb_c, zb_t, qi_t, ki_n, wi_t, bias_b)


def _merge_kernel(x_ref, ya_ref, yb_ref, gt_ref, wpa_ref, wpb_ref, wo_ref, o_ref):
    pa = jnp.dot(wpa_ref[...], ya_ref[0], preferred_element_type=_F32)
    pb = jnp.dot(wpb_ref[...], yb_ref[0], preferred_element_type=_F32)
    merged = (gt_ref[0, :D_MODEL, :].astype(_F32) * pa
              + gt_ref[0, D_MODEL:, :].astype(_F32) * pb).astype(_BF16)
    out_t = jnp.dot(wo_ref[...], merged, preferred_element_type=_F32)
    o_ref[0] = x_ref[0] + out_t.T


def _merge(x, ya_t, yb_t, gt_t, wpa_t, wpb_t, wo_t):
    B, S, D = x.shape
    tm = TM_PROJ
    const = lambda b, t: (0, 0)
    in_specs = [
        pl.BlockSpec((1, tm, D), lambda b, t: (b, t, 0)),
        pl.BlockSpec((1, A_WIDTH, tm), lambda b, t: (b, 0, t)),
        pl.BlockSpec((1, B_WIDTH, tm), lambda b, t: (b, 0, t)),
        pl.BlockSpec((1, 2 * D, tm), lambda b, t: (b, 0, t)),
        pl.BlockSpec((D, A_WIDTH), const),
        pl.BlockSpec((D, B_WIDTH), const),
        pl.BlockSpec((D, D), const),
    ]
    return pl.pallas_call(
        _merge_kernel, grid=(B, S // tm), in_specs=in_specs,
        out_specs=pl.BlockSpec((1, tm, D), lambda b, t: (b, t, 0)),
        out_shape=jax.ShapeDtypeStruct((B, S, D), x.dtype),
        compiler_params=pltpu.CompilerParams(
            dimension_semantics=("parallel", "parallel"), vmem_limit_bytes=VMEM_LIMIT),
        name="merge_out",
    )(x, ya_t, yb_t, gt_t, wpa_t, wpb_t, wo_t)


def _t5_bucket(n):
    n = np.maximum(n, 0)
    max_exact = N_BUCKETS // 2
    nf = np.maximum(n, 1).astype(np.float64)
    large = max_exact + np.floor(np.log(nf / max_exact) / math.log(MAX_DISTANCE / max_exact)
                                 * (N_BUCKETS - max_exact)).astype(np.int64)
    large = np.minimum(large, N_BUCKETS - 1)
    return np.where(n < max_exact, n, large)


def _bias_tables(rel_bias):
    t = np.arange(TQ)[None, :]
    s = np.arange(TQ)[:, None]
    d_prev = t + TQ - s
    d_cur = t - s
    table_a = rel_bias[:, :A_Q_HEADS].astype(_F32).T
    table_b = rel_bias[:, A_Q_HEADS:].astype(_F32).T

    def tile(table, dist, windowed):
        onehot = (_t5_bucket(dist)[..., None] == np.arange(N_BUCKETS)).astype(np.float32)
        b = jnp.einsum("stk,hk->hst", onehot, table, precision=lax.Precision.HIGHEST)
        if windowed:
            b = jnp.where(((dist >= 0) & (dist < WINDOW))[None], b, NEG)
        return b

    bias_a = jnp.concatenate([tile(table_a, d_prev, True), tile(table_a, d_cur, True)], axis=1)
    far = jnp.broadcast_to(table_b[:, N_BUCKETS - 1][:, None, None], (B_HEADS, TQ, TQ))
    bias_b = jnp.stack([far, tile(table_b, d_prev, False), tile(table_b, d_cur, False)], axis=1) * LOG2E
    return bias_a, bias_b


def kernel(x, norm_g, w_in, qnorm_a, knorm_a, sinks_a, qnorm_b, knorm_b, rel_bias,
           w_proj_a, w_proj_b, w_out):
    assert norm_g.shape[0] == 1, "single-layer block"
    B, S, D = x.shape
    assert D == D_MODEL and S % TM_PROJ == 0 and S % KSUPER == 0 and S % TQB == 0 and TQB % TQ == 0
    assert MAX_DISTANCE <= TQ

    n_real = _SEG["kw"][0] + KW_REAL
    w = w_in[0].astype(_BF16)
    wt = jnp.concatenate(
        [w[:, :n_real], jnp.zeros((D, _SEG["kw"][1] - n_real), _BF16), w[:, n_real:]], axis=1).T
    bcast = lambda g, scale: jnp.broadcast_to((g.astype(_F32) * scale)[:, None], (HEAD_DIM, TM_PROJ))
    q_scale = HEAD_DIM ** -0.5
    (qa_t, ka_n, va_t, za_t, qb_t, kb_n, vb_c, zb_t, qi_t, ki_n, wi_t, gt_t) = _inproj(
        x, norm_g[0][None, :].astype(_F32), wt,
        bcast(qnorm_a[0], q_scale), bcast(knorm_a[0], 1.0),
        bcast(qnorm_b[0], q_scale * LOG2E), bcast(knorm_b[0], 1.0))

    bias_a, bias_b = _bias_tables(rel_bias)
    sink_a = jnp.broadcast_to(sinks_a[0].astype(_F32)[:, None], (A_Q_HEADS, TQ))
    ya_t = _mix_a(qa_t, ka_n, va_t, za_t, bias_a, sink_a)
    yb_t = _mix_b(qb_t, kb_n, vb_c, zb_t, qi_t, ki_n, wi_t, bias_b)
    return _merge(x, ya_t, yb_t, gt_t,
                  w_proj_a[0].astype(_BF16).T, w_proj_b[0].astype(_BF16).T, w_out[0].astype(_BF16).T)
```

```python
import functools
import math

import jax
import jax.numpy as jnp
import numpy as np
from jax import lax
from jax.experimental import pallas as pl
from jax.experimental.pallas import tpu as pltpu

D_MODEL = 1024
HEAD_DIM = 64
A_Q_HEADS = 8
A_KV_HEADS = 2
A_GROUPS = A_Q_HEADS // A_KV_HEADS
A_WIDTH = A_Q_HEADS * HEAD_DIM
A_KV_WIDTH = A_KV_HEADS * HEAD_DIM
B_HEADS = 8
B_WIDTH = B_HEADS * HEAD_DIM
IDX_HEADS = 8
IDX_DIM = 32
IDX_WIDTH = IDX_HEADS * IDX_DIM
WINDOW = 128
TOPK_MAX = 256
N_BUCKETS = 32
MAX_DISTANCE = 128
RMS_EPS = 1e-6

LANES = 128
TQ = 128
A_SUB = 4
TQB = 256
KCHUNK = 256
KSUPER = 2 * KCHUNK
VALUE_PASSES = 16
SEARCH_CHECKS = (1, 3, 5, 8)
TM_PROJ = 1024
NEG = -1e30
LOG2E = math.log2(math.e)
ONES_ROWS = 16
PAIR_ROWS = 2 * HEAD_DIM + ONES_ROWS
V_ROWS = (B_HEADS // 2) * PAIR_ROWS
VMEM_LIMIT = 56 * 1024 * 1024

_SEG = {}
_off = 0
for _name, _rows in (("qa", A_WIDTH), ("ka", A_KV_WIDTH), ("va", A_KV_WIDTH), ("za", A_WIDTH),
                     ("qb", B_WIDTH), ("kb", B_WIDTH), ("vb", B_WIDTH), ("zb", B_WIDTH),
                     ("qi", IDX_WIDTH), ("kw", LANES), ("gates", 2 * D_MODEL)):
    _SEG[_name] = (_off, _off + _rows)
    _off += _rows
PROJ_ROWS = _off
KW_REAL = IDX_DIM + IDX_HEADS

_F32 = jnp.float32
_BF16 = jnp.bfloat16
_NT = (((1,), (1,)), ((), ()))


def _inproj_kernel(x_ref, g_ref, wt_ref, gqa_ref, gka_ref, gqb_ref, gkb_ref,
                   qa_o, ka_o, va_o, za_o, qb_o, kb_o, vb_o, zb_o, qi_o, ki_o, wi_o, gt_o):
    tm = x_ref.shape[1]
    x = x_ref[0]
    ms = jnp.mean(x * x, axis=-1, keepdims=True)
    h = (x * lax.rsqrt(ms + RMS_EPS) * g_ref[...]).astype(_BF16)

    def proj(lo, hi):
        return lax.dot_general(wt_ref[lo:hi, :], h, _NT, preferred_element_type=_F32)

    def seg(name):
        return proj(*_SEG[name])

    def headnorm(p, gain_ref):
        nh = p.shape[0] // HEAD_DIM
        p3 = p.reshape(nh, HEAD_DIM, tm)
        r = lax.rsqrt(jnp.mean(p3 * p3, axis=1, keepdims=True) + RMS_EPS)
        return (p3 * r * gain_ref[...][None]).reshape(nh * HEAD_DIM, tm)

    def silu(p):
        return p * jax.nn.sigmoid(p)

    qa_o[0] = headnorm(seg("qa"), gqa_ref).astype(_BF16)
    ka_o[0] = headnorm(seg("ka"), gka_ref).T.astype(_BF16)
    va_o[0] = seg("va").astype(_BF16)
    za_o[0] = silu(seg("za")).astype(_BF16)
    qb_o[0] = headnorm(seg("qb"), gqb_ref).astype(_BF16)
    kb_o[0] = headnorm(seg("kb"), gkb_ref).T.astype(_BF16)
    vb = seg("vb").astype(_BF16)
    ones = jnp.ones((ONES_ROWS, KCHUNK), _BF16)
    for c in range(tm // KCHUNK):
        for p in range(B_HEADS // 2):
            vb_o[0, c, p * PAIR_ROWS:(p + 1) * PAIR_ROWS, :] = jnp.concatenate(
                [vb[p * LANES:(p + 1) * LANES, c * KCHUNK:(c + 1) * KCHUNK], ones], axis=0)
    zb_o[0] = silu(seg("zb")).astype(_BF16)
    qi_o[0] = seg("qi").astype(_BF16)
    kw = seg("kw")
    wi_o[0] = kw[IDX_DIM:KW_REAL, :]
    row = lax.broadcasted_iota(jnp.int32, kw.shape, 0)
    ki_o[0] = jnp.where(row < IDX_DIM, kw, 0.0).T.astype(_BF16)
    g_lo = _SEG["gates"][0]
    for c in range(2 * D_MODEL // 512):
        gt_o[0, c * 512:(c + 1) * 512, :] = jax.nn.sigmoid(
            proj(g_lo + c * 512, g_lo + (c + 1) * 512)).astype(_BF16)


def _inproj(x, g, wt, gqa, gka, gqb, gkb):
    B, S, D = x.shape
    tm = TM_PROJ
    nt = S // tm
    const = lambda b, t: (0, 0)
    tr = lambda rows: pl.BlockSpec((1, rows, tm), lambda b, t: (b, 0, t))
    nat = lambda cols: pl.BlockSpec((1, tm, cols), lambda b, t: (b, t, 0))
    sds = jax.ShapeDtypeStruct
    out_shape = (
        sds((B, A_WIDTH, S), _BF16), sds((B, S, A_KV_WIDTH), _BF16), sds((B, A_KV_WIDTH, S), _BF16),
        sds((B, A_WIDTH, S), _BF16),
        sds((B, B_WIDTH, S), _BF16), sds((B, S, B_WIDTH), _BF16),
        sds((B, S // KCHUNK, V_ROWS, KCHUNK), _BF16), sds((B, B_WIDTH, S), _BF16),
        sds((B, IDX_WIDTH, S), _BF16), sds((B, S, LANES), _BF16), sds((B, IDX_HEADS, S), _F32),
        sds((B, 2 * D_MODEL, S), _BF16),
    )
    out_specs = (
        tr(A_WIDTH), nat(A_KV_WIDTH), tr(A_KV_WIDTH), tr(A_WIDTH),
        tr(B_WIDTH), nat(B_WIDTH),
        pl.BlockSpec((1, tm // KCHUNK, V_ROWS, KCHUNK), lambda b, t: (b, t, 0, 0)), tr(B_WIDTH),
        tr(IDX_WIDTH), nat(LANES), tr(IDX_HEADS), tr(2 * D_MODEL),
    )
    in_specs = [
        pl.BlockSpec((1, tm, D), lambda b, t: (b, t, 0)),
        pl.BlockSpec((1, D), const),
        pl.BlockSpec((PROJ_ROWS, D), const, pipeline_mode=pl.Buffered(1)),
        pl.BlockSpec((HEAD_DIM, tm), const), pl.BlockSpec((HEAD_DIM, tm), const),
        pl.BlockSpec((HEAD_DIM, tm), const), pl.BlockSpec((HEAD_DIM, tm), const),
    ]
    return pl.pallas_call(
        _inproj_kernel, grid=(B, nt), in_specs=in_specs, out_specs=out_specs, out_shape=out_shape,
        compiler_params=pltpu.CompilerParams(
            dimension_semantics=("parallel", "parallel"), vmem_limit_bytes=VMEM_LIMIT),
        name="inproj",
    )(x, g, wt, gqa, gka, gqb, gkb)


def _mix_a_kernel(q_ref, kp_ref, kc_ref, vp_ref, vc_ref, z_ref, bias_ref, sink_ref, o_ref):
    i = pl.program_id(1)
    zeros = jnp.zeros((HEAD_DIM, TQ), _BF16)
    units = [(j, g) for j in range(A_SUB) for g in range(A_KV_HEADS)]

    def bands(j):
        cur = slice(j * TQ, (j + 1) * TQ)
        old = slice((j - 1) * TQ, j * TQ)
        k_old = kp_ref[0] if j == 0 else kc_ref[0, old, :]
        v_old = vp_ref[0] if j == 0 else vc_ref[0, :, old]
        return (jnp.concatenate([k_old, kc_ref[0, cur, :]], axis=0),
                jnp.concatenate([v_old, vc_ref[0, :, cur]], axis=1))

    scores = []
    for j, g in units:
        cur = slice(j * TQ, (j + 1) * TQ)
        cols = []
        for hh in range(A_GROUPS):
            h = g * A_GROUPS + hh
            parts = [zeros] * A_KV_HEADS
            parts[g] = q_ref[0, h * HEAD_DIM:(h + 1) * HEAD_DIM, cur]
            cols.append(jnp.concatenate(parts, axis=0))
        rhs = jnp.concatenate(cols, axis=1)
        scores.append(jnp.dot(bands(j)[0], rhs, preferred_element_type=_F32))

    pts, invs = [], []
    for (j, g), sc in zip(units, scores):
        probs, inv = [], []
        for hh in range(A_GROUPS):
            h = g * A_GROUPS + hh
            s = sc[:, hh * TQ:(hh + 1) * TQ] + bias_ref[h]
            if j == 0:
                s = jnp.concatenate([jnp.where(i > 0, s[:TQ], NEG), s[TQ:]], axis=0)
            sink = sink_ref[h:h + 1, :]
            m = jnp.maximum(jnp.max(s, axis=0, keepdims=True), sink)
            p = jnp.exp(s - m)
            denom = jnp.sum(p, axis=0, keepdims=True) + jnp.exp(sink - m)
            probs.append(p.astype(_BF16))
            inv.append(1.0 / denom)
        pts.append(jnp.concatenate(probs, axis=1))
        invs.append(inv)

    for (j, g), pt, inv in zip(units, pts, invs):
        cur = slice(j * TQ, (j + 1) * TQ)
        out = jnp.dot(bands(j)[1][g * HEAD_DIM:(g + 1) * HEAD_DIM, :], pt,
                      preferred_element_type=_F32)
        for hh in range(A_GROUPS):
            h = g * A_GROUPS + hh
            rows = slice(h * HEAD_DIM, (h + 1) * HEAD_DIM)
            o = out[:, hh * TQ:(hh + 1) * TQ] * inv[hh] * z_ref[0, rows, cur].astype(_F32)
            o_ref[0, rows, cur] = o.astype(_BF16)


def _mix_a(qa_t, ka_n, va_t, za_t, bias_a, sink_a):
    B, _, S = qa_t.shape
    tqa = A_SUB * TQ
    prev = lambda i: jnp.maximum(i * A_SUB - 1, 0)
    in_specs = [
        pl.BlockSpec((1, A_WIDTH, tqa), lambda b, i: (b, 0, i)),
        pl.BlockSpec((1, TQ, A_KV_WIDTH), lambda b, i: (b, prev(i), 0)),
        pl.BlockSpec((1, tqa, A_KV_WIDTH), lambda b, i: (b, i, 0)),
        pl.BlockSpec((1, A_KV_WIDTH, TQ), lambda b, i: (b, 0, prev(i))),
        pl.BlockSpec((1, A_KV_WIDTH, tqa), lambda b, i: (b, 0, i)),
        pl.BlockSpec((1, A_WIDTH, tqa), lambda b, i: (b, 0, i)),
        pl.BlockSpec((A_Q_HEADS, 2 * TQ, TQ), lambda b, i: (0, 0, 0)),
        pl.BlockSpec((A_Q_HEADS, TQ), lambda b, i: (0, 0)),
    ]
    return pl.pallas_call(
        _mix_a_kernel, grid=(B, S // tqa), in_specs=in_specs,
        out_specs=pl.BlockSpec((1, A_WIDTH, tqa), lambda b, i: (b, 0, i)),
        out_shape=jax.ShapeDtypeStruct((B, A_WIDTH, S), _BF16),
        compiler_params=pltpu.CompilerParams(
            dimension_semantics=("parallel", "parallel"), vmem_limit_bytes=VMEM_LIMIT),
        name="mixer_a",
    )(qa_t, ka_n, ka_n, va_t, va_t, za_t, bias_a, sink_a)


def _key_to_f32(key):
    return pltpu.bitcast(jnp.where(key < 0, key ^ jnp.int32(0x7FFFFFFF), key), _F32)


def _f32_to_key(v):
    bits = pltpu.bitcast(v, jnp.int32)
    return jnp.where(bits < 0, bits ^ jnp.int32(0x7FFFFFFF), bits)


def _colmin8(v):
    return jnp.min(v.reshape(v.shape[0] // 8, 8, v.shape[1]), axis=0)


def _colmax8(v):
    return jnp.max(v.reshape(v.shape[0] // 8, 8, v.shape[1]), axis=0)


def _for_chunks(n, step):
    def body(t, carry):
        for u in range(4):
            step(4 * t + u)
        return carry

    lax.fori_loop(0, n // 4, body, 0)
    base = (n // 4) * 4
    for rest in (1, 2, 3):
        @pl.when(n - base == rest)
        def _(rest=rest):
            for u in range(rest):
                step(base + u)


def _mix_b_kernel(qb_ref, kb_ref, vb_ref, zb_ref, qi_ref, ki_ref, wi_ref, bias_ref, o_ref,
                  keys_scr, mb_scr, sc_scr, acc_scr, m_scr, ext_scr, thr_scr, tie_scr):
    i = pl.program_id(1)
    nc = ((i + 1) * TQB + KCHUNK - 1) // KCHUNK
    ns = (nc + 1) // 2
    top_k = TOPK_MAX

    qi = qi_ref[0]
    zpad = jnp.zeros((LANES - IDX_DIM, TQB), _BF16)
    rhs_i = jnp.concatenate(
        [jnp.concatenate([qi[h * IDX_DIM:(h + 1) * IDX_DIM], zpad], axis=0)
         for h in range(IDX_HEADS)], axis=1)
    w = wi_ref[0] * (IDX_DIM ** -0.5 * IDX_HEADS ** -0.5)
    s_minus_t = (lax.broadcasted_iota(jnp.int32, (KCHUNK, TQB), 0)
                 - lax.broadcasted_iota(jnp.int32, (KCHUNK, TQB), 1))

    def index_step(c):
        r0 = pl.multiple_of(c * KCHUNK, KCHUNK)
        d = jnp.dot(ki_ref[0, pl.ds(r0, KCHUNK), :], rhs_i,
                    preferred_element_type=_F32)
        acc = w[0:1, :] * jnp.maximum(d[:, 0:TQB], 0.0)
        for h in range(1, IDX_HEADS):
            acc = acc + w[h:h + 1, :] * jnp.maximum(d[:, h * TQB:(h + 1) * TQB], 0.0)
        causal = s_minus_t <= (i * TQB - r0)
        score = jnp.where(causal, acc, -jnp.inf)
        keys_scr[pl.ds(r0, KCHUNK), :] = score
        ext_scr[0:8, :] = jnp.maximum(ext_scr[0:8, :], _colmax8(score))
        ext_scr[8:16, :] = jnp.minimum(ext_scr[8:16, :], _colmin8(jnp.where(causal, acc, jnp.inf)))

    ext_scr[0:8, :] = jnp.full((8, TQB), -jnp.inf, _F32)
    ext_scr[8:16, :] = jnp.full((8, TQB), jnp.inf, _F32)
    _for_chunks(nc, index_step)

    @pl.when(nc % 2 == 1)
    def _():
        keys_scr[pl.ds(pl.multiple_of(nc * KCHUNK, KCHUNK), KCHUNK), :] = jnp.full(
            (KCHUNK, TQB), -jnp.inf, _F32)

    def count_rows(src, hit):
        n_acc = 4

        def bump(accs, r0, n_rows):
            accs = list(accs)
            rows = src[pl.ds(r0, n_rows), :]
            for j in range(n_rows // 8):
                a = accs[j % n_acc]
                accs[j % n_acc] = jnp.where(hit(rows[j * 8:(j + 1) * 8]), a + 1, a)
            return tuple(accs)

        accs = lax.fori_loop(
            0, nc // 2, lambda c, accs: bump(accs, pl.multiple_of(c * KSUPER, KSUPER), KSUPER),
            tuple(jnp.zeros((8, TQB), jnp.int32) for _ in range(n_acc)))
        accs = lax.cond(nc % 2 == 1,
                        lambda accs: bump(accs, pl.multiple_of((nc - 1) * KCHUNK, KCHUNK), KCHUNK),
                        lambda accs: accs, accs)
        acc = accs[0]
        for a in accs[1:]:
            acc = acc + a
        return jnp.sum(acc, axis=0, keepdims=True)

    def count_ge(cand, strict=False):
        return count_rows(keys_scr, (lambda r: r > cand) if strict else (lambda r: r >= cand))

    @pl.when(i * TQB + TQB <= top_k)
    def _():
        thr_scr[...] = jnp.full((1, TQB), -jnp.inf, _F32)
        tie_scr[...] = jnp.full((1, TQB), -1, jnp.int32)

    @pl.when(i * TQB + TQB > top_k)
    def _():
        zero = jnp.zeros((1, TQB), _F32)
        cnt_nonneg = count_ge(zero)
        cnt_pos = count_ge(zero, strict=True)
        key_max = _f32_to_key(jnp.max(ext_scr[0:8, :], axis=0, keepdims=True))
        key_min = _f32_to_key(jnp.min(ext_scr[8:16, :], axis=0, keepdims=True))
        n_causal = i * TQB + 1 + lax.broadcasted_iota(jnp.int32, (1, TQB), 1)
        nonneg = cnt_nonneg >= top_k
        lo = jnp.where(nonneg, 0, key_min)
        cnt_lo = jnp.where(nonneg, cnt_nonneg, n_causal)
        hi = jnp.where(nonneg, jnp.where(cnt_pos < top_k, 1, key_max + 1), 0)

        def halve_values(_, state):
            lo_v, hi_v, cnt_lo = state
            mid = lo_v + (hi_v - lo_v) * 0.5
            mid = jnp.where(mid > lo_v, jnp.where(mid < hi_v, mid, lo_v), lo_v)
            cnt = count_ge(mid)
            ok = cnt >= top_k
            return jnp.where(ok, mid, lo_v), jnp.where(ok, hi_v, mid), jnp.where(ok, cnt, cnt_lo)

        def halve_keys(_, state):
            lo, hi, cnt_lo = state
            mid = lo + lax.shift_right_logical(hi - lo, 1)
            cnt = count_ge(_key_to_f32(mid))
            ok = cnt >= top_k
            return jnp.where(ok, mid, lo), jnp.where(ok, hi, mid), jnp.where(ok, cnt, cnt_lo)

        def finished(state):
            lo, hi, cnt_lo = state
            done = jnp.where(cnt_lo == top_k, 1, jnp.where(hi - lo <= 1, 1, 0))
            return jnp.min(done) == 1

        lo_v, hi_v, cnt_lo = lax.fori_loop(0, VALUE_PASSES, halve_values,
                                           (_key_to_f32(lo), _key_to_f32(hi), cnt_lo))
        state = (_f32_to_key(lo_v), _f32_to_key(hi_v), cnt_lo)
        stops = (0,) + SEARCH_CHECKS + (31,)
        state = lax.fori_loop(stops[0], stops[1], halve_keys, state)
        for first, last in zip(stops[1:-1], stops[2:]):
            state = lax.cond(finished(state), lambda s: s,
                             functools.partial(lax.fori_loop, first, last, halve_keys), state)
        lo, _, cnt_ge = state
        thr = _key_to_f32(lo)
        thr_scr[...] = thr
        tie_scr[...] = jnp.full((1, TQB), 2 ** 30, jnp.int32)

        @pl.when(jnp.max(cnt_ge) > top_k)
        def _():
            need = (top_k - count_ge(thr, strict=True)).astype(_F32)
            lower = (lax.broadcasted_iota(jnp.int32, (KCHUNK, KCHUNK), 0)
                     >= lax.broadcasted_iota(jnp.int32, (KCHUNK, KCHUNK), 1))
            lower = jnp.where(lower, 1.0, 0.0).astype(_BF16)

            def prefix_body(c, state):
                before, below = state
                r0 = pl.multiple_of(c * KCHUNK, KCHUNK)
                tied = jnp.where(keys_scr[pl.ds(r0, KCHUNK), :] == thr, 1.0, 0.0).astype(_BF16)
                upto = jnp.dot(lower, tied, preferred_element_type=_F32) + before
                below = below + jnp.sum(jnp.where(upto < need, 1, 0).reshape(KCHUNK // 8, 8, TQB), axis=0)
                return upto[KCHUNK - 1:KCHUNK, :], below

            _, below = lax.fori_loop(0, nc, prefix_body,
                                     (jnp.zeros((1, TQB), _F32), jnp.zeros((8, TQB), jnp.int32)))
            tie_scr[...] = jnp.sum(below, axis=0, keepdims=True)

    thr = thr_scr[...]
    tie = tie_scr[...]
    row = lax.broadcasted_iota(jnp.int32, (KSUPER, TQB), 0)

    def mask_body(c, carry):
        r0 = pl.multiple_of(c * KSUPER, KSUPER)
        k = keys_scr[pl.ds(r0, KSUPER), :]
        mb_scr[pl.ds(r0, KSUPER), :] = jnp.where(
            k == thr, jnp.where(row <= tie - r0, 0.0, NEG), jnp.where(k > thr, 0.0, NEG))
        return carry

    lax.fori_loop(0, ns, mask_body, 0)

    zeros = jnp.zeros((HEAD_DIM, TQB), _BF16)
    n_pairs = B_HEADS // 2
    rhs = []
    for p in range(n_pairs):
        q0 = qb_ref[0, (2 * p) * HEAD_DIM:(2 * p + 1) * HEAD_DIM, :]
        q1 = qb_ref[0, (2 * p + 1) * HEAD_DIM:(2 * p + 2) * HEAD_DIM, :]
        rhs.append(jnp.concatenate([jnp.concatenate([q0, zeros], axis=0),
                                    jnp.concatenate([zeros, q1], axis=0)], axis=1))

    def score_step(c):
        r0 = pl.multiple_of(c * KCHUNK, KCHUNK)
        for p in range(n_pairs):
            s = jnp.dot(kb_ref[0, pl.ds(r0, KCHUNK), p * LANES:(p + 1) * LANES], rhs[p],
                        preferred_element_type=_F32)
            for hh in range(2):
                h = 2 * p + hh
                parts = []
                for half in range(KCHUNK // TQ):
                    jb = c * (KCHUNK // TQ) + half
                    rs = pl.multiple_of(r0 + half * TQ, TQ)
                    subs = []
                    for sub in range(TQB // TQ):
                        kind = jnp.clip(jb - (i * (TQB // TQ) + sub) + 2, 0, 2)
                        lanes = slice(sub * TQ, (sub + 1) * TQ)
                        subs.append(s[half * TQ:(half + 1) * TQ, hh * TQB + sub * TQ:hh * TQB + (sub + 1) * TQ]
                                    + bias_ref[h, kind] + mb_scr[pl.ds(rs, TQ), lanes])
                    parts.append(jnp.concatenate(subs, axis=1))
                sh = jnp.concatenate(parts, axis=0)
                sc_scr[pl.ds(r0, KCHUNK), h * TQB:(h + 1) * TQB] = sh
                m_scr[h * 8:(h + 1) * 8, :] = jnp.maximum(m_scr[h * 8:(h + 1) * 8, :], _colmax8(sh))

    m_scr[...] = jnp.full(m_scr.shape, NEG, _F32)
    _for_chunks(nc, score_step)
    m_row = jnp.concatenate(
        [jnp.max(m_scr[h * 8:(h + 1) * 8, :], axis=0, keepdims=True) for h in range(B_HEADS)],
        axis=1)
    acc_scr[...] = jnp.zeros_like(acc_scr)

    def pv_step(c):
        r0 = pl.multiple_of(c * KCHUNK, KCHUNK)
        for p in range(n_pairs):
            cols = slice(2 * p * TQB, (2 * p + 2) * TQB)
            pr = jnp.exp2(sc_scr[pl.ds(r0, KCHUNK), cols] - m_row[:, cols]).astype(_BF16)
            acc_scr[p] += jnp.dot(vb_ref[0, c, p * PAIR_ROWS:(p + 1) * PAIR_ROWS, :], pr,
                                  preferred_element_type=_F32)

    _for_chunks(nc, pv_step)
    for p in range(n_pairs):
        inv = 1.0 / acc_scr[p, 2 * HEAD_DIM:2 * HEAD_DIM + 1, :]
        for hh in range(2):
            rows = slice((2 * p + hh) * HEAD_DIM, (2 * p + hh + 1) * HEAD_DIM)
            o = (acc_scr[p, hh * HEAD_DIM:(hh + 1) * HEAD_DIM, hh * TQB:(hh + 1) * TQB]
                 * inv[:, hh * TQB:(hh + 1) * TQB] * zb_ref[0, rows, :].astype(_F32))
            o_ref[0, rows, :] = o.astype(_BF16)


def _mix_b(qb_t, kb_n, vb_c, zb_t, qi_t, ki_n, wi_t, bias_b):
    B, _, S = qb_t.shape
    nq = S // TQB
    per_q = lambda rows: pl.BlockSpec((1, rows, TQB), lambda b, i: (b, 0, i))
    in_specs = [
        per_q(B_WIDTH),
        pl.BlockSpec((1, S, B_WIDTH), lambda b, i: (b, 0, 0)),
        pl.BlockSpec((1, S // KCHUNK, V_ROWS, KCHUNK), lambda b, i: (b, 0, 0, 0)),
        per_q(B_WIDTH),
        per_q(IDX_WIDTH),
        pl.BlockSpec((1, S, LANES), lambda b, i: (b, 0, 0)),
        per_q(IDX_HEADS),
        pl.BlockSpec((B_HEADS, 3, TQ, TQ), lambda b, i: (0, 0, 0, 0)),
    ]
    scratch = [
        pltpu.VMEM((S, TQB), _F32),
        pltpu.VMEM((S, TQB), _F32),
        pltpu.VMEM((S, B_HEADS * TQB), _F32),
        pltpu.VMEM((B_HEADS // 2, PAIR_ROWS, 2 * TQB), _F32),
        pltpu.VMEM((B_HEADS * 8, TQB), _F32),
        pltpu.VMEM((16, TQB), _F32),
        pltpu.VMEM((1, TQB), _F32),
        pltpu.VMEM((1, TQB), jnp.int32),
    ]
    return pl.pallas_call(
        _mix_b_kernel, grid=(B, nq), in_specs=in_specs,
        out_specs=per_q(B_WIDTH),
        out_shape=jax.ShapeDtypeStruct((B, B_WIDTH, S), _BF16),
        scratch_shapes=scratch,
        compiler_params=pltpu.CompilerParams(
            dimension_semantics=("parallel", "arbitrary"), vmem_limit_bytes=VMEM_LIMIT),
        name="mixer_b",
    )(qb_t, kb_n, vb_c, zb_t, qi_t, ki_n, wi_t, bias_b)


def _merge_kernel(x_ref, ya_ref, yb_ref, gt_ref, wpa_ref, wpb_ref, wo_ref, o_ref):
    pa = jnp.dot(wpa_ref[...], ya_ref[0], preferred_element_type=_F32)
    pb = jnp.dot(wpb_ref[...], yb_ref[0], preferred_element_type=_F32)
    merged = (gt_ref[0, :D_MODEL, :].astype(_F32) * pa
              + gt_ref[0, D_MODEL:, :].astype(_F32) * pb).astype(_BF16)
    out_t = jnp.dot(wo_ref[...], merged, preferred_element_type=_F32)
    o_ref[0] = x_ref[0] + out_t.T


def _merge(x, ya_t, yb_t, gt_t, wpa_t, wpb_t, wo_t):
    B, S, D = x.shape
    tm = TM_PROJ
    const = lambda b, t: (0, 0)
    in_specs = [
        pl.BlockSpec((1, tm, D), lambda b, t: (b, t, 0)),
        pl.BlockSpec((1, A_WIDTH, tm), lambda b, t: (b, 0, t)),
        pl.BlockSpec((1, B_WIDTH, tm), lambda b, t: (b, 0, t)),
        pl.BlockSpec((1, 2 * D, tm), lambda b, t: (b, 0, t)),
        pl.BlockSpec((D, A_WIDTH), const),
        pl.BlockSpec((D, B_WIDTH), const),
        pl.BlockSpec((D, D), const),
    ]
    return pl.pallas_call(
        _merge_kernel, grid=(B, S // tm), in_specs=in_specs,
        out_specs=pl.BlockSpec((1, tm, D), lambda b, t: (b, t, 0)),
        out_shape=jax.ShapeDtypeStruct((B, S, D), x.dtype),
        compiler_params=pltpu.CompilerParams(
            dimension_semantics=("parallel", "parallel"), vmem_limit_bytes=VMEM_LIMIT),
        name="merge_out",
    )(x, ya_t, yb_t, gt_t, wpa_t, wpb_t, wo_t)


def _t5_bucket(n):
    n = np.maximum(n, 0)
    max_exact = N_BUCKETS // 2
    nf = np.maximum(n, 1).astype(np.float64)
    large = max_exact + np.floor(np.log(nf / max_exact) / math.log(MAX_DISTANCE / max_exact)
                                 * (N_BUCKETS - max_exact)).astype(np.int64)
    large = np.minimum(large, N_BUCKETS - 1)
    return np.where(n < max_exact, n, large)


def _bias_tables(rel_bias):
    t = np.arange(TQ)[None, :]
    s = np.arange(TQ)[:, None]
    d_prev = t + TQ - s
    d_cur = t - s
    table_a = rel_bias[:, :A_Q_HEADS].astype(_F32).T
    table_b = rel_bias[:, A_Q_HEADS:].astype(_F32).T

    def tile(table, dist, windowed):
        onehot = (_t5_bucket(dist)[..., None] == np.arange(N_BUCKETS)).astype(np.float32)
        b = jnp.einsum("stk,hk->hst", onehot, table, precision=lax.Precision.HIGHEST)
        if windowed:
            b = jnp.where(((dist >= 0) & (dist < WINDOW))[None], b, NEG)
        return b

    bias_a = jnp.concatenate([tile(table_a, d_prev, True), tile(table_a, d_cur, True)], axis=1)
    far = jnp.broadcast_to(table_b[:, N_BUCKETS - 1][:, None, None], (B_HEADS, TQ, TQ))
    bias_b = jnp.stack([far, tile(table_b, d_prev, False), tile(table_b, d_cur, False)], axis=1) * LOG2E
    return bias_a, bias_b


def kernel(x, norm_g, w_in, qnorm_a, knorm_a, sinks_a, qnorm_b, knorm_b, rel_bias,
           w_proj_a, w_proj_b, w_out):
    assert norm_g.shape[0] == 1, "single-layer block"
    B, S, D = x.shape
    assert D == D_MODEL and S % TM_PROJ == 0 and S % KSUPER == 0 and S % TQB == 0 and TQB % TQ == 0
    assert MAX_DISTANCE <= TQ

    wt = w_in[0].T
    n_real = _SEG["kw"][0] + KW_REAL
    wt = jnp.concatenate(
        [wt[:n_real], jnp.zeros((_SEG["kw"][1] - n_real, D), wt.dtype), wt[n_real:]], axis=0)
    wt = wt.astype(_BF16)
    bcast = lambda g, scale: jnp.broadcast_to((g.astype(_F32) * scale)[:, None], (HEAD_DIM, TM_PROJ))
    q_scale = HEAD_DIM ** -0.5
    (qa_t, ka_n, va_t, za_t, qb_t, kb_n, vb_c, zb_t, qi_t, ki_n, wi_t, gt_t) = _inproj(
        x, norm_g[0][None, :].astype(_F32), wt,
        bcast(qnorm_a[0], q_scale), bcast(knorm_a[0], 1.0),
        bcast(qnorm_b[0], q_scale * LOG2E), bcast(knorm_b[0], 1.0))

    bias_a, bias_b = _bias_tables(rel_bias)
    sink_a = jnp.broadcast_to(sinks_a[0].astype(_F32)[:, None], (A_Q_HEADS, TQ))
    ya_t = _mix_a(qa_t, ka_n, va_t, za_t, bias_a, sink_a)
    yb_t = _mix_b(qb_t, kb_n, vb_c, zb_t, qi_t, ki_n, wi_t, bias_b)
    return _merge(x, ya_t, yb_t, gt_t,
                  w_proj_a[0].T.astype(_BF16), w_proj_b[0].T.astype(_BF16), w_out[0].T.astype(_BF16))
```

```python
import functools
import math

import jax
import jax.numpy as jnp
import numpy as np
from jax import lax
from jax.experimental import pallas as pl
from jax.experimental.pallas import tpu as pltpu

D_MODEL = 1024
HEAD_DIM = 64
A_Q_HEADS = 8
A_KV_HEADS = 2
A_GROUPS = A_Q_HEADS // A_KV_HEADS
A_WIDTH = A_Q_HEADS * HEAD_DIM
A_KV_WIDTH = A_KV_HEADS * HEAD_DIM
B_HEADS = 8
B_WIDTH = B_HEADS * HEAD_DIM
IDX_HEADS = 8
IDX_DIM = 32
IDX_WIDTH = IDX_HEADS * IDX_DIM
WINDOW = 128
TOPK_MAX = 256
N_BUCKETS = 32
MAX_DISTANCE = 128
RMS_EPS = 1e-6

LANES = 128
TQ = 128
A_SUB = 4
TQB = 256
KCHUNK = 256
KSUPER = 2 * KCHUNK
VALUE_PASSES = 16
SEARCH_CHECKS = (1, 3, 5, 8)
TM_PROJ = 1024
NEG = -1e30
LOG2E = math.log2(math.e)
ONES_ROWS = 16
PAIR_ROWS = 2 * HEAD_DIM + ONES_ROWS
V_ROWS = (B_HEADS // 2) * PAIR_ROWS
VMEM_LIMIT = 56 * 1024 * 1024

_SEG = {}
_off = 0
for _name, _rows in (("qa", A_WIDTH), ("ka", A_KV_WIDTH), ("va", A_KV_WIDTH), ("za", A_WIDTH),
                     ("qb", B_WIDTH), ("kb", B_WIDTH), ("vb", B_WIDTH), ("zb", B_WIDTH),
                     ("qi", IDX_WIDTH), ("kw", LANES), ("gates", 2 * D_MODEL)):
    _SEG[_name] = (_off, _off + _rows)
    _off += _rows
PROJ_ROWS = _off
KW_REAL = IDX_DIM + IDX_HEADS

_F32 = jnp.float32
_BF16 = jnp.bfloat16
_NT = (((1,), (1,)), ((), ()))


def _inproj_kernel(x_ref, g_ref, wt_ref, gqa_ref, gka_ref, gqb_ref, gkb_ref,
                   qa_o, ka_o, va_o, za_o, qb_o, kb_o, vb_o, zb_o, qi_o, ki_o, wi_o, gt_o):
    tm = x_ref.shape[1]
    x = x_ref[0]
    ms = jnp.mean(x * x, axis=-1, keepdims=True)
    h = (x * lax.rsqrt(ms + RMS_EPS) * g_ref[...]).astype(_BF16)

    def proj(lo, hi):
        return lax.dot_general(wt_ref[lo:hi, :], h, _NT, preferred_element_type=_F32)

    def seg(name):
        return proj(*_SEG[name])

    def headnorm(p, gain_ref):
        nh = p.shape[0] // HEAD_DIM
        p3 = p.reshape(nh, HEAD_DIM, tm)
        r = lax.rsqrt(jnp.mean(p3 * p3, axis=1, keepdims=True) + RMS_EPS)
        return (p3 * r * gain_ref[...][None]).reshape(nh * HEAD_DIM, tm)

    def silu(p):
        return p * jax.nn.sigmoid(p)

    qa_o[0] = headnorm(seg("qa"), gqa_ref).astype(_BF16)
    ka_o[0] = headnorm(seg("ka"), gka_ref).T.astype(_BF16)
    va_o[0] = seg("va").astype(_BF16)
    za_o[0] = silu(seg("za")).astype(_BF16)
    qb_o[0] = headnorm(seg("qb"), gqb_ref).astype(_BF16)
    kb_o[0] = headnorm(seg("kb"), gkb_ref).T.astype(_BF16)
    vb = seg("vb").astype(_BF16)
    ones = jnp.ones((ONES_ROWS, KCHUNK), _BF16)
    for c in range(tm // KCHUNK):
        for p in range(B_HEADS // 2):
            vb_o[0, c, p * PAIR_ROWS:(p + 1) * PAIR_ROWS, :] = jnp.concatenate(
                [vb[p * LANES:(p + 1) * LANES, c * KCHUNK:(c + 1) * KCHUNK], ones], axis=0)
    zb_o[0] = silu(seg("zb")).astype(_BF16)
    qi_o[0] = seg("qi").astype(_BF16)
    kw = seg("kw")
    wi_o[0] = kw[IDX_DIM:KW_REAL, :]
    row = lax.broadcasted_iota(jnp.int32, kw.shape, 0)
    ki_o[0] = jnp.where(row < IDX_DIM, kw, 0.0).T.astype(_BF16)
    g_lo = _SEG["gates"][0]
    for c in range(2 * D_MODEL // 512):
        gt_o[0, c * 512:(c + 1) * 512, :] = jax.nn.sigmoid(
            proj(g_lo + c * 512, g_lo + (c + 1) * 512)).astype(_BF16)


def _inproj(x, g, wt, gqa, gka, gqb, gkb):
    B, S, D = x.shape
    tm = TM_PROJ
    nt = S // tm
    const = lambda b, t: (0, 0)
    tr = lambda rows: pl.BlockSpec((1, rows, tm), lambda b, t: (b, 0, t))
    nat = lambda cols: pl.BlockSpec((1, tm, cols), lambda b, t: (b, t, 0))
    sds = jax.ShapeDtypeStruct
    out_shape = (
        sds((B, A_WIDTH, S), _BF16), sds((B, S, A_KV_WIDTH), _BF16), sds((B, A_KV_WIDTH, S), _BF16),
        sds((B, A_WIDTH, S), _BF16),
        sds((B, B_WIDTH, S), _BF16), sds((B, S, B_WIDTH), _BF16),
        sds((B, S // KCHUNK, V_ROWS, KCHUNK), _BF16), sds((B, B_WIDTH, S), _BF16),
        sds((B, IDX_WIDTH, S), _BF16), sds((B, S, LANES), _BF16), sds((B, IDX_HEADS, S), _F32),
        sds((B, 2 * D_MODEL, S), _BF16),
    )
    out_specs = (
        tr(A_WIDTH), nat(A_KV_WIDTH), tr(A_KV_WIDTH), tr(A_WIDTH),
        tr(B_WIDTH), nat(B_WIDTH),
        pl.BlockSpec((1, tm // KCHUNK, V_ROWS, KCHUNK), lambda b, t: (b, t, 0, 0)), tr(B_WIDTH),
        tr(IDX_WIDTH), nat(LANES), tr(IDX_HEADS), tr(2 * D_MODEL),
    )
    in_specs = [
        pl.BlockSpec((1, tm, D), lambda b, t: (b, t, 0)),
        pl.BlockSpec((1, D), const),
        pl.BlockSpec((PROJ_ROWS, D), const, pipeline_mode=pl.Buffered(1)),
        pl.BlockSpec((HEAD_DIM, tm), const), pl.BlockSpec((HEAD_DIM, tm), const),
        pl.BlockSpec((HEAD_DIM, tm), const), pl.BlockSpec((HEAD_DIM, tm), const),
    ]
    return pl.pallas_call(
        _inproj_kernel, grid=(B, nt), in_specs=in_specs, out_specs=out_specs, out_shape=out_shape,
        compiler_params=pltpu.CompilerParams(
            dimension_semantics=("parallel", "parallel"), vmem_limit_bytes=VMEM_LIMIT),
        name="inproj",
    )(x, g, wt, gqa, gka, gqb, gkb)


def _mix_a_kernel(q_ref, kp_ref, kc_ref, vp_ref, vc_ref, z_ref, bias_ref, sink_ref, o_ref):
    i = pl.program_id(1)
    zeros = jnp.zeros((HEAD_DIM, TQ), _BF16)
    units = [(j, g) for j in range(A_SUB) for g in range(A_KV_HEADS)]

    def bands(j):
        cur = slice(j * TQ, (j + 1) * TQ)
        old = slice((j - 1) * TQ, j * TQ)
        k_old = kp_ref[0] if j == 0 else kc_ref[0, old, :]
        v_old = vp_ref[0] if j == 0 else vc_ref[0, :, old]
        return (jnp.concatenate([k_old, kc_ref[0, cur, :]], axis=0),
                jnp.concatenate([v_old, vc_ref[0, :, cur]], axis=1))

    scores = []
    for j, g in units:
        cur = slice(j * TQ, (j + 1) * TQ)
        cols = []
        for hh in range(A_GROUPS):
            h = g * A_GROUPS + hh
            parts = [zeros] * A_KV_HEADS
            parts[g] = q_ref[0, h * HEAD_DIM:(h + 1) * HEAD_DIM, cur]
            cols.append(jnp.concatenate(parts, axis=0))
        rhs = jnp.concatenate(cols, axis=1)
        scores.append(jnp.dot(bands(j)[0], rhs, preferred_element_type=_F32))

    pts, invs = [], []
    for (j, g), sc in zip(units, scores):
        probs, inv = [], []
        for hh in range(A_GROUPS):
            h = g * A_GROUPS + hh
            s = sc[:, hh * TQ:(hh + 1) * TQ] + bias_ref[h]
            if j == 0:
                s = jnp.concatenate([jnp.where(i > 0, s[:TQ], NEG), s[TQ:]], axis=0)
            sink = sink_ref[h:h + 1, :]
            m = jnp.maximum(jnp.max(s, axis=0, keepdims=True), sink)
            p = jnp.exp2(s - m)
            denom = jnp.sum(p, axis=0, keepdims=True) + jnp.exp2(sink - m)
            probs.append(p.astype(_BF16))
            inv.append(1.0 / denom)
        pts.append(jnp.concatenate(probs, axis=1))
        invs.append(inv)

    for (j, g), pt, inv in zip(units, pts, invs):
        cur = slice(j * TQ, (j + 1) * TQ)
        out = jnp.dot(bands(j)[1][g * HEAD_DIM:(g + 1) * HEAD_DIM, :], pt,
                      preferred_element_type=_F32)
        for hh in range(A_GROUPS):
            h = g * A_GROUPS + hh
            rows = slice(h * HEAD_DIM, (h + 1) * HEAD_DIM)
            o = out[:, hh * TQ:(hh + 1) * TQ] * inv[hh] * z_ref[0, rows, cur].astype(_F32)
            o_ref[0, rows, cur] = o.astype(_BF16)


def _mix_a(qa_t, ka_n, va_t, za_t, bias_a, sink_a):
    B, _, S = qa_t.shape
    tqa = A_SUB * TQ
    prev = lambda i: jnp.maximum(i * A_SUB - 1, 0)
    in_specs = [
        pl.BlockSpec((1, A_WIDTH, tqa), lambda b, i: (b, 0, i)),
        pl.BlockSpec((1, TQ, A_KV_WIDTH), lambda b, i: (b, prev(i), 0)),
        pl.BlockSpec((1, tqa, A_KV_WIDTH), lambda b, i: (b, i, 0)),
        pl.BlockSpec((1, A_KV_WIDTH, TQ), lambda b, i: (b, 0, prev(i))),
        pl.BlockSpec((1, A_KV_WIDTH, tqa), lambda b, i: (b, 0, i)),
        pl.BlockSpec((1, A_WIDTH, tqa), lambda b, i: (b, 0, i)),
        pl.BlockSpec((A_Q_HEADS, 2 * TQ, TQ), lambda b, i: (0, 0, 0)),
        pl.BlockSpec((A_Q_HEADS, TQ), lambda b, i: (0, 0)),
    ]
    return pl.pallas_call(
        _mix_a_kernel, grid=(B, S // tqa), in_specs=in_specs,
        out_specs=pl.BlockSpec((1, A_WIDTH, tqa), lambda b, i: (b, 0, i)),
        out_shape=jax.ShapeDtypeStruct((B, A_WIDTH, S), _BF16),
        compiler_params=pltpu.CompilerParams(
            dimension_semantics=("parallel", "parallel"), vmem_limit_bytes=VMEM_LIMIT),
        name="mixer_a",
    )(qa_t, ka_n, ka_n, va_t, va_t, za_t, bias_a, sink_a)


def _key_to_f32(key):
    return pltpu.bitcast(jnp.where(key < 0, key ^ jnp.int32(0x7FFFFFFF), key), _F32)


def _f32_to_key(v):
    bits = pltpu.bitcast(v, jnp.int32)
    return jnp.where(bits < 0, bits ^ jnp.int32(0x7FFFFFFF), bits)


def _colmin8(v):
    return jnp.min(v.reshape(v.shape[0] // 8, 8, v.shape[1]), axis=0)


def _colmax8(v):
    return jnp.max(v.reshape(v.shape[0] // 8, 8, v.shape[1]), axis=0)


def _for_chunks(n, step):
    def body(t, carry):
        for u in range(4):
            step(4 * t + u)
        return carry

    lax.fori_loop(0, n // 4, body, 0)
    base = (n // 4) * 4
    for rest in (1, 2, 3):
        @pl.when(n - base == rest)
        def _(rest=rest):
            for u in range(rest):
                step(base + u)


def _mix_b_kernel(qb_ref, kb_ref, vb_ref, zb_ref, qi_ref, ki_ref, wi_ref, bias_ref, o_ref,
                  keys_scr, mb_scr, sc_scr, acc_scr, m_scr, ext_scr, thr_scr, tie_scr):
    i = pl.program_id(1)
    nc = ((i + 1) * TQB + KCHUNK - 1) // KCHUNK
    ns = (nc + 1) // 2
    top_k = TOPK_MAX

    qi = qi_ref[0]
    zpad = jnp.zeros((LANES - IDX_DIM, TQB), _BF16)
    rhs_i = jnp.concatenate(
        [jnp.concatenate([qi[h * IDX_DIM:(h + 1) * IDX_DIM], zpad], axis=0)
         for h in range(IDX_HEADS)], axis=1)
    w = wi_ref[0] * (IDX_DIM ** -0.5 * IDX_HEADS ** -0.5)
    s_minus_t = (lax.broadcasted_iota(jnp.int32, (KCHUNK, TQB), 0)
                 - lax.broadcasted_iota(jnp.int32, (KCHUNK, TQB), 1))

    def index_step(c):
        r0 = pl.multiple_of(c * KCHUNK, KCHUNK)
        d = jnp.dot(ki_ref[0, pl.ds(r0, KCHUNK), :], rhs_i,
                    preferred_element_type=_F32)
        acc = w[0:1, :] * jnp.maximum(d[:, 0:TQB], 0.0)
        for h in range(1, IDX_HEADS):
            acc = acc + w[h:h + 1, :] * jnp.maximum(d[:, h * TQB:(h + 1) * TQB], 0.0)
        causal = s_minus_t <= (i * TQB - r0)
        score = jnp.where(causal, acc, -jnp.inf)
        keys_scr[pl.ds(r0, KCHUNK), :] = score
        ext_scr[0:8, :] = jnp.maximum(ext_scr[0:8, :], _colmax8(score))
        ext_scr[8:16, :] = jnp.minimum(ext_scr[8:16, :], _colmin8(jnp.where(causal, acc, jnp.inf)))

    ext_scr[0:8, :] = jnp.full((8, TQB), -jnp.inf, _F32)
    ext_scr[8:16, :] = jnp.full((8, TQB), jnp.inf, _F32)
    _for_chunks(nc, index_step)

    @pl.when(nc % 2 == 1)
    def _():
        keys_scr[pl.ds(pl.multiple_of(nc * KCHUNK, KCHUNK), KCHUNK), :] = jnp.full(
            (KCHUNK, TQB), -jnp.inf, _F32)

    def count_rows(src, hit):
        n_acc = 4

        def bump(accs, r0, n_rows):
            accs = list(accs)
            rows = src[pl.ds(r0, n_rows), :]
            for j in range(n_rows // 8):
                a = accs[j % n_acc]
                accs[j % n_acc] = jnp.where(hit(rows[j * 8:(j + 1) * 8]), a + 1, a)
            return tuple(accs)

        accs = lax.fori_loop(
            0, nc // 2, lambda c, accs: bump(accs, pl.multiple_of(c * KSUPER, KSUPER), KSUPER),
            tuple(jnp.zeros((8, TQB), jnp.int32) for _ in range(n_acc)))
        accs = lax.cond(nc % 2 == 1,
                        lambda accs: bump(accs, pl.multiple_of((nc - 1) * KCHUNK, KCHUNK), KCHUNK),
                        lambda accs: accs, accs)
        acc = accs[0]
        for a in accs[1:]:
            acc = acc + a
        return jnp.sum(acc, axis=0, keepdims=True)

    def count_ge(cand, strict=False):
        return count_rows(keys_scr, (lambda r: r > cand) if strict else (lambda r: r >= cand))

    @pl.when(i * TQB + TQB <= top_k)
    def _():
        thr_scr[...] = jnp.full((1, TQB), -jnp.inf, _F32)
        tie_scr[...] = jnp.full((1, TQB), -1, jnp.int32)

    @pl.when(i * TQB + TQB > top_k)
    def _():
        zero = jnp.zeros((1, TQB), _F32)
        cnt_nonneg = count_ge(zero)
        cnt_pos = count_ge(zero, strict=True)
        key_max = _f32_to_key(jnp.max(ext_scr[0:8, :], axis=0, keepdims=True))
        key_min = _f32_to_key(jnp.min(ext_scr[8:16, :], axis=0, keepdims=True))
        n_causal = i * TQB + 1 + lax.broadcasted_iota(jnp.int32, (1, TQB), 1)
        nonneg = cnt_nonneg >= top_k
        lo = jnp.where(nonneg, 0, key_min)
        cnt_lo = jnp.where(nonneg, cnt_nonneg, n_causal)
        hi = jnp.where(nonneg, jnp.where(cnt_pos < top_k, 1, key_max + 1), 0)

        def halve_values(_, state):
            lo_v, hi_v, cnt_lo = state
            mid = lo_v + (hi_v - lo_v) * 0.5
            mid = jnp.where(mid > lo_v, jnp.where(mid < hi_v, mid, lo_v), lo_v)
            cnt = count_ge(mid)
            ok = cnt >= top_k
            return jnp.where(ok, mid, lo_v), jnp.where(ok, hi_v, mid), jnp.where(ok, cnt, cnt_lo)

        def halve_keys(_, state):
            lo, hi, cnt_lo = state
            mid = lo + lax.shift_right_logical(hi - lo, 1)
            cnt = count_ge(_key_to_f32(mid))
            ok = cnt >= top_k
            return jnp.where(ok, mid, lo), jnp.where(ok, hi, mid), jnp.where(ok, cnt, cnt_lo)

        def finished(state):
            lo, hi, cnt_lo = state
            done = jnp.where(cnt_lo == top_k, 1, jnp.where(hi - lo <= 1, 1, 0))
            return jnp.min(done) == 1

        lo_v, hi_v, cnt_lo = lax.fori_loop(0, VALUE_PASSES, halve_values,
                                           (_key_to_f32(lo), _key_to_f32(hi), cnt_lo))
        state = (_f32_to_key(lo_v), _f32_to_key(hi_v), cnt_lo)
        stops = (0,) + SEARCH_CHECKS + (31,)
        state = lax.fori_loop(stops[0], stops[1], halve_keys, state)
        for first, last in zip(stops[1:-1], stops[2:]):
            state = lax.cond(finished(state), lambda s: s,
                             functools.partial(lax.fori_loop, first, last, halve_keys), state)
        lo, _, cnt_ge = state
        thr = _key_to_f32(lo)
        thr_scr[...] = thr
        tie_scr[...] = jnp.full((1, TQB), 2 ** 30, jnp.int32)

        @pl.when(jnp.max(cnt_ge) > top_k)
        def _():
            need = (top_k - count_ge(thr, strict=True)).astype(_F32)
            lower = (lax.broadcasted_iota(jnp.int32, (KCHUNK, KCHUNK), 0)
                     >= lax.broadcasted_iota(jnp.int32, (KCHUNK, KCHUNK), 1))
            lower = jnp.where(lower, 1.0, 0.0).astype(_BF16)

            def prefix_body(c, state):
                before, below = state
                r0 = pl.multiple_of(c * KCHUNK, KCHUNK)
                tied = jnp.where(keys_scr[pl.ds(r0, KCHUNK), :] == thr, 1.0, 0.0).astype(_BF16)
                upto = jnp.dot(lower, tied, preferred_element_type=_F32) + before
                below = below + jnp.sum(jnp.where(upto < need, 1, 0).reshape(KCHUNK // 8, 8, TQB), axis=0)
                return upto[KCHUNK - 1:KCHUNK, :], below

            _, below = lax.fori_loop(0, nc, prefix_body,
                                     (jnp.zeros((1, TQB), _F32), jnp.zeros((8, TQB), jnp.int32)))
            tie_scr[...] = jnp.sum(below, axis=0, keepdims=True)

    thr = thr_scr[...]
    tie = tie_scr[...]
    row = lax.broadcasted_iota(jnp.int32, (KSUPER, TQB), 0)

    def mask_body(c, carry):
        r0 = pl.multiple_of(c * KSUPER, KSUPER)
        k = keys_scr[pl.ds(r0, KSUPER), :]
        mb_scr[pl.ds(r0, KSUPER), :] = jnp.where(
            k == thr, jnp.where(row <= tie - r0, 0.0, NEG), jnp.where(k > thr, 0.0, NEG))
        return carry

    lax.fori_loop(0, ns, mask_body, 0)

    zeros = jnp.zeros((HEAD_DIM, TQB), _BF16)
    n_pairs = B_HEADS // 2
    rhs = []
    for p in range(n_pairs):
        q0 = qb_ref[0, (2 * p) * HEAD_DIM:(2 * p + 1) * HEAD_DIM, :]
        q1 = qb_ref[0, (2 * p + 1) * HEAD_DIM:(2 * p + 2) * HEAD_DIM, :]
        rhs.append(jnp.concatenate([jnp.concatenate([q0, zeros], axis=0),
                                    jnp.concatenate([zeros, q1], axis=0)], axis=1))

    def score_step(c):
        r0 = pl.multiple_of(c * KCHUNK, KCHUNK)
        for p in range(n_pairs):
            s = jnp.dot(kb_ref[0, pl.ds(r0, KCHUNK), p * LANES:(p + 1) * LANES], rhs[p],
                        preferred_element_type=_F32)
            for hh in range(2):
                h = 2 * p + hh
                parts = []
                for half in range(KCHUNK // TQ):
                    jb = c * (KCHUNK // TQ) + half
                    rs = pl.multiple_of(r0 + half * TQ, TQ)
                    subs = []
                    for sub in range(TQB // TQ):
                        kind = jnp.clip(jb - (i * (TQB // TQ) + sub) + 2, 0, 2)
                        lanes = slice(sub * TQ, (sub + 1) * TQ)
                        subs.append(s[half * TQ:(half + 1) * TQ, hh * TQB + sub * TQ:hh * TQB + (sub + 1) * TQ]
                                    + bias_ref[h, kind] + mb_scr[pl.ds(rs, TQ), lanes])
                    parts.append(jnp.concatenate(subs, axis=1))
                sh = jnp.concatenate(parts, axis=0)
                sc_scr[pl.ds(r0, KCHUNK), h * TQB:(h + 1) * TQB] = sh
                m_scr[h * 8:(h + 1) * 8, :] = jnp.maximum(m_scr[h * 8:(h + 1) * 8, :], _colmax8(sh))

    m_scr[...] = jnp.full(m_scr.shape, NEG, _F32)
    _for_chunks(nc, score_step)
    m_row = jnp.concatenate(
        [jnp.max(m_scr[h * 8:(h + 1) * 8, :], axis=0, keepdims=True) for h in range(B_HEADS)],
        axis=1)
    acc_scr[...] = jnp.zeros_like(acc_scr)

    def pv_step(c):
        r0 = pl.multiple_of(c * KCHUNK, KCHUNK)
        for p in range(n_pairs):
            cols = slice(2 * p * TQB, (2 * p + 2) * TQB)
            pr = jnp.exp2(sc_scr[pl.ds(r0, KCHUNK), cols] - m_row[:, cols]).astype(_BF16)
            acc_scr[p] += jnp.dot(vb_ref[0, c, p * PAIR_ROWS:(p + 1) * PAIR_ROWS, :], pr,
                                  preferred_element_type=_F32)

    _for_chunks(nc, pv_step)
    for p in range(n_pairs):
        inv = 1.0 / acc_scr[p, 2 * HEAD_DIM:2 * HEAD_DIM + 1, :]
        for hh in range(2):
            rows = slice((2 * p + hh) * HEAD_DIM, (2 * p + hh + 1) * HEAD_DIM)
            o = (acc_scr[p, hh * HEAD_DIM:(hh + 1) * HEAD_DIM, hh * TQB:(hh + 1) * TQB]
                 * inv[:, hh * TQB:(hh + 1) * TQB] * zb_ref[0, rows, :].astype(_F32))
            o_ref[0, rows, :] = o.astype(_BF16)


def _mix_b(qb_t, kb_n, vb_c, zb_t, qi_t, ki_n, wi_t, bias_b):
    B, _, S = qb_t.shape
    nq = S // TQB
    per_q = lambda rows: pl.BlockSpec((1, rows, TQB), lambda b, i: (b, 0, i))
    in_specs = [
        per_q(B_WIDTH),
        pl.BlockSpec((1, S, B_WIDTH), lambda b, i: (b, 0, 0)),
        pl.BlockSpec((1, S // KCHUNK, V_ROWS, KCHUNK), lambda b, i: (b, 0, 0, 0)),
        per_q(B_WIDTH),
        per_q(IDX_WIDTH),
        pl.BlockSpec((1, S, LANES), lambda b, i: (b, 0, 0)),
        per_q(IDX_HEADS),
        pl.BlockSpec((B_HEADS, 3, TQ, TQ), lambda b, i: (0, 0, 0, 0)),
    ]
    scratch = [
        pltpu.VMEM((S, TQB), _F32),
        pltpu.VMEM((S, TQB), _F32),
        pltpu.VMEM((S, B_HEADS * TQB), _F32),
        pltpu.VMEM((B_HEADS // 2, PAIR_ROWS, 2 * TQB), _F32),
        pltpu.VMEM((B_HEADS * 8, TQB), _F32),
        pltpu.VMEM((16, TQB), _F32),
        pltpu.VMEM((1, TQB), _F32),
        pltpu.VMEM((1, TQB), jnp.int32),
    ]
    return pl.pallas_call(
        _mix_b_kernel, grid=(B, nq), in_specs=in_specs,
        out_specs=per_q(B_WIDTH),
        out_shape=jax.ShapeDtypeStruct((B, B_WIDTH, S), _BF16),
        scratch_shapes=scratch,
        compiler_params=pltpu.CompilerParams(
            dimension_semantics=("parallel", "arbitrary"), vmem_limit_bytes=VMEM_LIMIT),
        name="mixer_b",
    )(qb_t, kb_n, vb_c, zb_t, qi_t, ki_n, wi_t, bias_b)


def _merge_kernel(x_ref, ya_ref, yb_ref, gt_ref, wpa_ref, wpb_ref, wo_ref, o_ref):
    pa = jnp.dot(wpa_ref[...], ya_ref[0], preferred_element_type=_F32)
    pb = jnp.dot(wpb_ref[...], yb_ref[0], preferred_element_type=_F32)
    merged = (gt_ref[0, :D_MODEL, :].astype(_F32) * pa
              + gt_ref[0, D_MODEL:, :].astype(_F32) * pb).astype(_BF16)
    out_t = jnp.dot(wo_ref[...], merged, preferred_element_type=_F32)
    o_ref[0] = x_ref[0] + out_t.T


def _merge(x, ya_t, yb_t, gt_t, wpa_t, wpb_t, wo_t):
    B, S, D = x.shape
    tm = TM_PROJ
    const = lambda b, t: (0, 0)
    in_specs = [
        pl.BlockSpec((1, tm, D), lambda b, t: (b, t, 0)),
        pl.BlockSpec((1, A_WIDTH, tm), lambda b, t: (b, 0, t)),
        pl.BlockSpec((1, B_WIDTH, tm), lambda b, t: (b, 0, t)),
        pl.BlockSpec((1, 2 * D, tm), lambda b, t: (b, 0, t)),
        pl.BlockSpec((D, A_WIDTH), const),
        pl.BlockSpec((D, B_WIDTH), const),
        pl.BlockSpec((D, D), const),
    ]
    return pl.pallas_call(
        _merge_kernel, grid=(B, S // tm), in_specs=in_specs,
        out_specs=pl.BlockSpec((1, tm, D), lambda b, t: (b, t, 0)),
        out_shape=jax.ShapeDtypeStruct((B, S, D), x.dtype),
        compiler_params=pltpu.CompilerParams(
            dimension_semantics=("parallel", "parallel"), vmem_limit_bytes=VMEM_LIMIT),
        name="merge_out",
    )(x, ya_t, yb_t, gt_t, wpa_t, wpb_t, wo_t)


def _t5_bucket(n):
    n = np.maximum(n, 0)
    max_exact = N_BUCKETS // 2
    nf = np.maximum(n, 1).astype(np.float64)
    large = max_exact + np.floor(np.log(nf / max_exact) / math.log(MAX_DISTANCE / max_exact)
                                 * (N_BUCKETS - max_exact)).astype(np.int64)
    large = np.minimum(large, N_BUCKETS - 1)
    return np.where(n < max_exact, n, large)


def _bias_tables(rel_bias):
    t = np.arange(TQ)[None, :]
    s = np.arange(TQ)[:, None]
    d_prev = t + TQ - s
    d_cur = t - s
    table_a = rel_bias[:, :A_Q_HEADS].astype(_F32).T * LOG2E
    table_b = rel_bias[:, A_Q_HEADS:].astype(_F32).T * LOG2E

    def tile(table, dist, windowed):
        onehot = (_t5_bucket(dist)[..., None] == np.arange(N_BUCKETS)).astype(np.float32)
        b = jnp.einsum("stk,hk->hst", onehot, table, precision=lax.Precision.HIGHEST)
        if windowed:
            b = jnp.where(((dist >= 0) & (dist < WINDOW))[None], b, NEG)
        return b

    bias_a = jnp.concatenate([tile(table_a, d_prev, True), tile(table_a, d_cur, True)], axis=1)
    far = jnp.broadcast_to(table_b[:, N_BUCKETS - 1][:, None, None], (B_HEADS, TQ, TQ))
    bias_b = jnp.stack([far, tile(table_b, d_prev, False), tile(table_b, d_cur, False)], axis=1)
    return bias_a, bias_b


def kernel(x, norm_g, w_in, qnorm_a, knorm_a, sinks_a, qnorm_b, knorm_b, rel_bias,
           w_proj_a, w_proj_b, w_out):
    assert norm_g.shape[0] == 1, "single-layer block"
    B, S, D = x.shape
    assert D == D_MODEL and S % TM_PROJ == 0 and S % KSUPER == 0 and S % TQB == 0 and TQB % TQ == 0
    assert MAX_DISTANCE <= TQ

    wt = w_in[0].T
    n_real = _SEG["kw"][0] + KW_REAL
    wt = jnp.concatenate(
        [wt[:n_real], jnp.zeros((_SEG["kw"][1] - n_real, D), wt.dtype), wt[n_real:]], axis=0)
    wt = wt.astype(_BF16)
    bcast = lambda g, scale: jnp.broadcast_to((g.astype(_F32) * scale)[:, None], (HEAD_DIM, TM_PROJ))
    q_scale = HEAD_DIM ** -0.5
    (qa_t, ka_n, va_t, za_t, qb_t, kb_n, vb_c, zb_t, qi_t, ki_n, wi_t, gt_t) = _inproj(
        x, norm_g[0][None, :].astype(_F32), wt,
        bcast(qnorm_a[0], q_scale * LOG2E), bcast(knorm_a[0], 1.0),
        bcast(qnorm_b[0], q_scale * LOG2E), bcast(knorm_b[0], 1.0))

    bias_a, bias_b = _bias_tables(rel_bias)
    sink_a = jnp.broadcast_to((sinks_a[0].astype(_F32) * LOG2E)[:, None], (A_Q_HEADS, TQ))
    ya_t = _mix_a(qa_t, ka_n, va_t, za_t, bias_a, sink_a)
    yb_t = _mix_b(qb_t, kb_n, vb_c, zb_t, qi_t, ki_n, wi_t, bias_b)
    return _merge(x, ya_t, yb_t, gt_t,
                  w_proj_a[0].T.astype(_BF16), w_proj_b[0].T.astype(_BF16), w_out[0].T.astype(_BF16))
```

```python
import functools
import math

import jax
import jax.numpy as jnp
import numpy as np
from jax import lax
from jax.experimental import pallas as pl
from jax.experimental.pallas import tpu as pltpu

D_MODEL = 1024
HEAD_DIM = 64
A_Q_HEADS = 8
A_KV_HEADS = 2
A_GROUPS = A_Q_HEADS // A_KV_HEADS
A_WIDTH = A_Q_HEADS * HEAD_DIM
A_KV_WIDTH = A_KV_HEADS * HEAD_DIM
B_HEADS = 8
B_WIDTH = B_HEADS * HEAD_DIM
IDX_HEADS = 8
IDX_DIM = 32
IDX_WIDTH = IDX_HEADS * IDX_DIM
WINDOW = 128
TOPK_MAX = 256
N_BUCKETS = 32
MAX_DISTANCE = 128
RMS_EPS = 1e-6

LANES = 128
TQ = 128
A_SUB = 4
TQB = 256
KCHUNK = 256
KSUPER = 2 * KCHUNK
VALUE_PASSES = 16
SEARCH_CHECKS = (1, 3, 5, 8)
TM_PROJ = 1024
PROJ_PIECE = 256
NEG = -1e30
LOG2E = math.log2(math.e)
ONES_ROWS = 16
PAIR_ROWS = 2 * HEAD_DIM + ONES_ROWS
V_ROWS = (B_HEADS // 2) * PAIR_ROWS
VMEM_LIMIT = 56 * 1024 * 1024

_SEG = {}
_off = 0
for _name, _rows in (("qa", A_WIDTH), ("ka", A_KV_WIDTH), ("va", A_KV_WIDTH), ("za", A_WIDTH),
                     ("qb", B_WIDTH), ("kb", B_WIDTH), ("vb", B_WIDTH), ("zb", B_WIDTH),
                     ("qi", IDX_WIDTH), ("kw", LANES), ("gates", 2 * D_MODEL)):
    _SEG[_name] = (_off, _off + _rows)
    _off += _rows
PROJ_ROWS = _off
KW_REAL = IDX_DIM + IDX_HEADS

_F32 = jnp.float32
_BF16 = jnp.bfloat16
_NT = (((1,), (1,)), ((), ()))


def _inproj_kernel(x_ref, g_ref, wt_ref, gqa_ref, gka_ref, gqb_ref, gkb_ref,
                   qa_o, ka_o, va_o, za_o, qb_o, kb_o, vb_o, zb_o, qi_o, ki_o, wi_o, gt_o):
    tm = x_ref.shape[1]
    x = x_ref[0]
    ms = jnp.mean(x * x, axis=-1, keepdims=True)
    h = (x * lax.rsqrt(ms + RMS_EPS) * g_ref[...]).astype(_BF16)

    def proj(lo, hi):
        return lax.dot_general(wt_ref[lo:hi, :], h, _NT, preferred_element_type=_F32)

    def seg(name):
        return proj(*_SEG[name])

    def headnorm(p, gain_ref):
        nh = p.shape[0] // HEAD_DIM
        p3 = p.reshape(nh, HEAD_DIM, tm)
        r = lax.rsqrt(jnp.mean(p3 * p3, axis=1, keepdims=True) + RMS_EPS)
        return (p3 * r * gain_ref[...][None]).reshape(nh * HEAD_DIM, tm)

    def silu(p):
        return p * jax.nn.sigmoid(p)

    def pieces(name):
        lo, hi = _SEG[name]
        for r in range(lo, hi, PROJ_PIECE):
            n = min(PROJ_PIECE, hi - r)
            yield r - lo, n, proj(r, r + n)

    for r, n, p in pieces("qa"):
        qa_o[0, r:r + n, :] = headnorm(p, gqa_ref).astype(_BF16)
    for r, n, p in pieces("ka"):
        ka_o[0, :, r:r + n] = headnorm(p, gka_ref).T.astype(_BF16)
    for r, n, p in pieces("va"):
        va_o[0, r:r + n, :] = p.astype(_BF16)
    for r, n, p in pieces("za"):
        za_o[0, r:r + n, :] = silu(p).astype(_BF16)
    for r, n, p in pieces("qb"):
        qb_o[0, r:r + n, :] = headnorm(p, gqb_ref).astype(_BF16)
    for r, n, p in pieces("kb"):
        kb_o[0, :, r:r + n] = headnorm(p, gkb_ref).T.astype(_BF16)
    ones = jnp.ones((ONES_ROWS, KCHUNK), _BF16)
    for r, n, p in pieces("vb"):
        vb = p.astype(_BF16)
        for q in range(n // LANES):
            pair = r // LANES + q
            for c in range(tm // KCHUNK):
                vb_o[0, c, pair * PAIR_ROWS:(pair + 1) * PAIR_ROWS, :] = jnp.concatenate(
                    [vb[q * LANES:(q + 1) * LANES, c * KCHUNK:(c + 1) * KCHUNK], ones], axis=0)
    for r, n, p in pieces("zb"):
        zb_o[0, r:r + n, :] = silu(p).astype(_BF16)
    for r, n, p in pieces("qi"):
        qi_o[0, r:r + n, :] = p.astype(_BF16)
    kw = seg("kw")
    wi_o[0] = kw[IDX_DIM:KW_REAL, :]
    row = lax.broadcasted_iota(jnp.int32, kw.shape, 0)
    ki_o[0] = jnp.where(row < IDX_DIM, kw, 0.0).T.astype(_BF16)
    for r, n, p in pieces("gates"):
        gt_o[0, r:r + n, :] = jax.nn.sigmoid(p).astype(_BF16)


def _inproj(x, g, wt, gqa, gka, gqb, gkb):
    B, S, D = x.shape
    tm = TM_PROJ
    nt = S // tm
    const = lambda b, t: (0, 0)
    tr = lambda rows: pl.BlockSpec((1, rows, tm), lambda b, t: (b, 0, t))
    nat = lambda cols: pl.BlockSpec((1, tm, cols), lambda b, t: (b, t, 0))
    sds = jax.ShapeDtypeStruct
    out_shape = (
        sds((B, A_WIDTH, S), _BF16), sds((B, S, A_KV_WIDTH), _BF16), sds((B, A_KV_WIDTH, S), _BF16),
        sds((B, A_WIDTH, S), _BF16),
        sds((B, B_WIDTH, S), _BF16), sds((B, S, B_WIDTH), _BF16),
        sds((B, S // KCHUNK, V_ROWS, KCHUNK), _BF16), sds((B, B_WIDTH, S), _BF16),
        sds((B, IDX_WIDTH, S), _BF16), sds((B, S, LANES), _BF16), sds((B, IDX_HEADS, S), _F32),
        sds((B, 2 * D_MODEL, S), _BF16),
    )
    out_specs = (
        tr(A_WIDTH), nat(A_KV_WIDTH), tr(A_KV_WIDTH), tr(A_WIDTH),
        tr(B_WIDTH), nat(B_WIDTH),
        pl.BlockSpec((1, tm // KCHUNK, V_ROWS, KCHUNK), lambda b, t: (b, t, 0, 0)), tr(B_WIDTH),
        tr(IDX_WIDTH), nat(LANES), tr(IDX_HEADS), tr(2 * D_MODEL),
    )
    in_specs = [
        pl.BlockSpec((1, tm, D), lambda b, t: (b, t, 0)),
        pl.BlockSpec((1, D), const),
        pl.BlockSpec((PROJ_ROWS, D), const, pipeline_mode=pl.Buffered(1)),
        pl.BlockSpec((HEAD_DIM, tm), const), pl.BlockSpec((HEAD_DIM, tm), const),
        pl.BlockSpec((HEAD_DIM, tm), const), pl.BlockSpec((HEAD_DIM, tm), const),
    ]
    return pl.pallas_call(
        _inproj_kernel, grid=(B, nt), in_specs=in_specs, out_specs=out_specs, out_shape=out_shape,
        compiler_params=pltpu.CompilerParams(
            dimension_semantics=("parallel", "parallel"), vmem_limit_bytes=VMEM_LIMIT),
        name="inproj",
    )(x, g, wt, gqa, gka, gqb, gkb)


def _mix_a_kernel(q_ref, kp_ref, kc_ref, vp_ref, vc_ref, z_ref, bias_ref, sink_ref, o_ref):
    i = pl.program_id(1)
    zeros = jnp.zeros((HEAD_DIM, TQ), _BF16)
    units = [(j, g) for j in range(A_SUB) for g in range(A_KV_HEADS)]

    def bands(j):
        cur = slice(j * TQ, (j + 1) * TQ)
        old = slice((j - 1) * TQ, j * TQ)
        k_old = kp_ref[0] if j == 0 else kc_ref[0, old, :]
        v_old = vp_ref[0] if j == 0 else vc_ref[0, :, old]
        return (jnp.concatenate([k_old, kc_ref[0, cur, :]], axis=0),
                jnp.concatenate([v_old, vc_ref[0, :, cur]], axis=1))

    scores = []
    for j, g in units:
        cur = slice(j * TQ, (j + 1) * TQ)
        cols = []
        for hh in range(A_GROUPS):
            h = g * A_GROUPS + hh
            parts = [zeros] * A_KV_HEADS
            parts[g] = q_ref[0, h * HEAD_DIM:(h + 1) * HEAD_DIM, cur]
            cols.append(jnp.concatenate(parts, axis=0))
        rhs = jnp.concatenate(cols, axis=1)
        scores.append(jnp.dot(bands(j)[0], rhs, preferred_element_type=_F32))

    pts, invs = [], []
    for (j, g), sc in zip(units, scores):
        probs, inv = [], []
        for hh in range(A_GROUPS):
            h = g * A_GROUPS + hh
            s = sc[:, hh * TQ:(hh + 1) * TQ] + bias_ref[h]
            if j == 0:
                s = jnp.concatenate([jnp.where(i > 0, s[:TQ], NEG), s[TQ:]], axis=0)
            sink = sink_ref[h:h + 1, :]
            m = jnp.maximum(jnp.max(s, axis=0, keepdims=True), sink)
            p = jnp.exp2(s - m)
            denom = jnp.sum(p, axis=0, keepdims=True) + jnp.exp2(sink - m)
            probs.append(p.astype(_BF16))
            inv.append(1.0 / denom)
        pts.append(jnp.concatenate(probs, axis=1))
        invs.append(inv)

    for (j, g), pt, inv in zip(units, pts, invs):
        cur = slice(j * TQ, (j + 1) * TQ)
        out = jnp.dot(bands(j)[1][g * HEAD_DIM:(g + 1) * HEAD_DIM, :], pt,
                      preferred_element_type=_F32)
        for hh in range(A_GROUPS):
            h = g * A_GROUPS + hh
            rows = slice(h * HEAD_DIM, (h + 1) * HEAD_DIM)
            o = out[:, hh * TQ:(hh + 1) * TQ] * inv[hh] * z_ref[0, rows, cur].astype(_F32)
            o_ref[0, rows, cur] = o.astype(_BF16)


def _mix_a(qa_t, ka_n, va_t, za_t, bias_a, sink_a):
    B, _, S = qa_t.shape
    tqa = A_SUB * TQ
    prev = lambda i: jnp.maximum(i * A_SUB - 1, 0)
    in_specs = [
        pl.BlockSpec((1, A_WIDTH, tqa), lambda b, i: (b, 0, i)),
        pl.BlockSpec((1, TQ, A_KV_WIDTH), lambda b, i: (b, prev(i), 0)),
        pl.BlockSpec((1, tqa, A_KV_WIDTH), lambda b, i: (b, i, 0)),
        pl.BlockSpec((1, A_KV_WIDTH, TQ), lambda b, i: (b, 0, prev(i))),
        pl.BlockSpec((1, A_KV_WIDTH, tqa), lambda b, i: (b, 0, i)),
        pl.BlockSpec((1, A_WIDTH, tqa), lambda b, i: (b, 0, i)),
        pl.BlockSpec((A_Q_HEADS, 2 * TQ, TQ), lambda b, i: (0, 0, 0)),
        pl.BlockSpec((A_Q_HEADS, TQ), lambda b, i: (0, 0)),
    ]
    return pl.pallas_call(
        _mix_a_kernel, grid=(B, S // tqa), in_specs=in_specs,
        out_specs=pl.BlockSpec((1, A_WIDTH, tqa), lambda b, i: (b, 0, i)),
        out_shape=jax.ShapeDtypeStruct((B, A_WIDTH, S), _BF16),
        compiler_params=pltpu.CompilerParams(
            dimension_semantics=("parallel", "parallel"), vmem_limit_bytes=VMEM_LIMIT),
        name="mixer_a",
    )(qa_t, ka_n, ka_n, va_t, va_t, za_t, bias_a, sink_a)


def _key_to_f32(key):
    return pltpu.bitcast(jnp.where(key < 0, key ^ jnp.int32(0x7FFFFFFF), key), _F32)


def _f32_to_key(v):
    bits = pltpu.bitcast(v, jnp.int32)
    return jnp.where(bits < 0, bits ^ jnp.int32(0x7FFFFFFF), bits)


def _colmin8(v):
    return jnp.min(v.reshape(v.shape[0] // 8, 8, v.shape[1]), axis=0)


def _colmax8(v):
    return jnp.max(v.reshape(v.shape[0] // 8, 8, v.shape[1]), axis=0)


def _for_chunks(n, step):
    def body(t, carry):
        for u in range(4):
            step(4 * t + u)
        return carry

    lax.fori_loop(0, n // 4, body, 0)
    base = (n // 4) * 4
    for rest in (1, 2, 3):
        @pl.when(n - base == rest)
        def _(rest=rest):
            for u in range(rest):
                step(base + u)


def _mix_b_kernel(qb_ref, kb_ref, vb_ref, zb_ref, qi_ref, ki_ref, wi_ref, bias_ref, o_ref,
                  keys_scr, mb_scr, sc_scr, acc_scr, m_scr, ext_scr, thr_scr, tie_scr):
    i = pl.program_id(1)
    nc = ((i + 1) * TQB + KCHUNK - 1) // KCHUNK
    ns = (nc + 1) // 2
    top_k = TOPK_MAX

    qi = qi_ref[0]
    zpad = jnp.zeros((LANES - IDX_DIM, TQB), _BF16)
    rhs_i = jnp.concatenate(
        [jnp.concatenate([qi[h * IDX_DIM:(h + 1) * IDX_DIM], zpad], axis=0)
         for h in range(IDX_HEADS)], axis=1)
    w = wi_ref[0] * (IDX_DIM ** -0.5 * IDX_HEADS ** -0.5)
    s_minus_t = (lax.broadcasted_iota(jnp.int32, (KCHUNK, TQB), 0)
                 - lax.broadcasted_iota(jnp.int32, (KCHUNK, TQB), 1))

    def index_step(c):
        r0 = pl.multiple_of(c * KCHUNK, KCHUNK)
        d = jnp.dot(ki_ref[0, pl.ds(r0, KCHUNK), :], rhs_i,
                    preferred_element_type=_F32)
        acc = w[0:1, :] * jnp.maximum(d[:, 0:TQB], 0.0)
        for h in range(1, IDX_HEADS):
            acc = acc + w[h:h + 1, :] * jnp.maximum(d[:, h * TQB:(h + 1) * TQB], 0.0)
        causal = s_minus_t <= (i * TQB - r0)
        score = jnp.where(causal, acc, -jnp.inf)
        keys_scr[pl.ds(r0, KCHUNK), :] = score
        ext_scr[0:8, :] = jnp.maximum(ext_scr[0:8, :], _colmax8(score))
        ext_scr[8:16, :] = jnp.minimum(ext_scr[8:16, :], _colmin8(jnp.where(causal, acc, jnp.inf)))

    ext_scr[0:8, :] = jnp.full((8, TQB), -jnp.inf, _F32)
    ext_scr[8:16, :] = jnp.full((8, TQB), jnp.inf, _F32)
    _for_chunks(nc, index_step)

    @pl.when(nc % 2 == 1)
    def _():
        keys_scr[pl.ds(pl.multiple_of(nc * KCHUNK, KCHUNK), KCHUNK), :] = jnp.full(
            (KCHUNK, TQB), -jnp.inf, _F32)

    def count_rows(src, hit):
        n_acc = 4

        def bump(accs, r0, n_rows):
            accs = list(accs)
            rows = src[pl.ds(r0, n_rows), :]
            for j in range(n_rows // 8):
                a = accs[j % n_acc]
                accs[j % n_acc] = jnp.where(hit(rows[j * 8:(j + 1) * 8]), a + 1, a)
            return tuple(accs)

        accs = lax.fori_loop(
            0, nc // 2, lambda c, accs: bump(accs, pl.multiple_of(c * KSUPER, KSUPER), KSUPER),
            tuple(jnp.zeros((8, TQB), jnp.int32) for _ in range(n_acc)))
        accs = lax.cond(nc % 2 == 1,
                        lambda accs: bump(accs, pl.multiple_of((nc - 1) * KCHUNK, KCHUNK), KCHUNK),
                        lambda accs: accs, accs)
        acc = accs[0]
        for a in accs[1:]:
            acc = acc + a
        return jnp.sum(acc, axis=0, keepdims=True)

    def count_ge(cand, strict=False):
        return count_rows(keys_scr, (lambda r: r > cand) if strict else (lambda r: r >= cand))

    @pl.when(i * TQB + TQB <= top_k)
    def _():
        thr_scr[...] = jnp.full((1, TQB), -jnp.inf, _F32)
        tie_scr[...] = jnp.full((1, TQB), -1, jnp.int32)

    @pl.when(i * TQB + TQB > top_k)
    def _():
        zero = jnp.zeros((1, TQB), _F32)
        cnt_nonneg = count_ge(zero)
        cnt_pos = count_ge(zero, strict=True)
        key_max = _f32_to_key(jnp.max(ext_scr[0:8, :], axis=0, keepdims=True))
        key_min = _f32_to_key(jnp.min(ext_scr[8:16, :], axis=0, keepdims=True))
        n_causal = i * TQB + 1 + lax.broadcasted_iota(jnp.int32, (1, TQB), 1)
        nonneg = cnt_nonneg >= top_k
        lo = jnp.where(nonneg, 0, key_min)
        cnt_lo = jnp.where(nonneg, cnt_nonneg, n_causal)
        hi = jnp.where(nonneg, jnp.where(cnt_pos < top_k, 1, key_max + 1), 0)

        def halve_values(_, state):
            lo_v, hi_v, cnt_lo = state
            mid = lo_v + (hi_v - lo_v) * 0.5
            mid = jnp.where(mid > lo_v, jnp.where(mid < hi_v, mid, lo_v), lo_v)
            cnt = count_ge(mid)
            ok = cnt >= top_k
            return jnp.where(ok, mid, lo_v), jnp.where(ok, hi_v, mid), jnp.where(ok, cnt, cnt_lo)

        def halve_keys(_, state):
            lo, hi, cnt_lo = state
            mid = lo + lax.shift_right_logical(hi - lo, 1)
            cnt = count_ge(_key_to_f32(mid))
            ok = cnt >= top_k
            return jnp.where(ok, mid, lo), jnp.where(ok, hi, mid), jnp.where(ok, cnt, cnt_lo)

        def finished(state):
            lo, hi, cnt_lo = state
            done = jnp.where(cnt_lo == top_k, 1, jnp.where(hi - lo <= 1, 1, 0))
            return jnp.min(done) == 1

        lo_v, hi_v, cnt_lo = lax.fori_loop(0, VALUE_PASSES, halve_values,
                                           (_key_to_f32(lo), _key_to_f32(hi), cnt_lo))
        state = (_f32_to_key(lo_v), _f32_to_key(hi_v), cnt_lo)
        stops = (0,) + SEARCH_CHECKS + (31,)
        state = lax.fori_loop(stops[0], stops[1], halve_keys, state)
        for first, last in zip(stops[1:-1], stops[2:]):
            state = lax.cond(finished(state), lambda s: s,
                             functools.partial(lax.fori_loop, first, last, halve_keys), state)
        lo, _, cnt_ge = state
        thr = _key_to_f32(lo)
        thr_scr[...] = thr
        tie_scr[...] = jnp.full((1, TQB), 2 ** 30, jnp.int32)

        @pl.when(jnp.max(cnt_ge) > top_k)
        def _():
            need = (top_k - count_ge(thr, strict=True)).astype(_F32)
            lower = (lax.broadcasted_iota(jnp.int32, (KCHUNK, KCHUNK), 0)
                     >= lax.broadcasted_iota(jnp.int32, (KCHUNK, KCHUNK), 1))
            lower = jnp.where(lower, 1.0, 0.0).astype(_BF16)

            def prefix_body(c, state):
                before, below = state
                r0 = pl.multiple_of(c * KCHUNK, KCHUNK)
                tied = jnp.where(keys_scr[pl.ds(r0, KCHUNK), :] == thr, 1.0, 0.0).astype(_BF16)
                upto = jnp.dot(lower, tied, preferred_element_type=_F32) + before
                below = below + jnp.sum(jnp.where(upto < need, 1, 0).reshape(KCHUNK // 8, 8, TQB), axis=0)
                return upto[KCHUNK - 1:KCHUNK, :], below

            _, below = lax.fori_loop(0, nc, prefix_body,
                                     (jnp.zeros((1, TQB), _F32), jnp.zeros((8, TQB), jnp.int32)))
            tie_scr[...] = jnp.sum(below, axis=0, keepdims=True)

    thr = thr_scr[...]
    tie = tie_scr[...]
    row = lax.broadcasted_iota(jnp.int32, (KSUPER, TQB), 0)

    def mask_body(c, carry):
        r0 = pl.multiple_of(c * KSUPER, KSUPER)
        k = keys_scr[pl.ds(r0, KSUPER), :]
        mb_scr[pl.ds(r0, KSUPER), :] = jnp.where(
            k == thr, jnp.where(row <= tie - r0, 0.0, NEG), jnp.where(k > thr, 0.0, NEG))
        return carry

    lax.fori_loop(0, ns, mask_body, 0)

    zeros = jnp.zeros((HEAD_DIM, TQB), _BF16)
    n_pairs = B_HEADS // 2
    rhs = []
    for p in range(n_pairs):
        q0 = qb_ref[0, (2 * p) * HEAD_DIM:(2 * p + 1) * HEAD_DIM, :]
        q1 = qb_ref[0, (2 * p + 1) * HEAD_DIM:(2 * p + 2) * HEAD_DIM, :]
        rhs.append(jnp.concatenate([jnp.concatenate([q0, zeros], axis=0),
                                    jnp.concatenate([zeros, q1], axis=0)], axis=1))

    def score_step(c):
        r0 = pl.multiple_of(c * KCHUNK, KCHUNK)
        for p in range(n_pairs):
            s = jnp.dot(kb_ref[0, pl.ds(r0, KCHUNK), p * LANES:(p + 1) * LANES], rhs[p],
                        preferred_element_type=_F32)
            for hh in range(2):
                h = 2 * p + hh
                parts = []
                for half in range(KCHUNK // TQ):
                    jb = c * (KCHUNK // TQ) + half
                    rs = pl.multiple_of(r0 + half * TQ, TQ)
                    subs = []
                    for sub in range(TQB // TQ):
                        kind = jnp.clip(jb - (i * (TQB // TQ) + sub) + 2, 0, 2)
                        lanes = slice(sub * TQ, (sub + 1) * TQ)
                        subs.append(s[half * TQ:(half + 1) * TQ, hh * TQB + sub * TQ:hh * TQB + (sub + 1) * TQ]
                                    + bias_ref[h, kind] + mb_scr[pl.ds(rs, TQ), lanes])
                    parts.append(jnp.concatenate(subs, axis=1))
                sh = jnp.concatenate(parts, axis=0)
                sc_scr[pl.ds(r0, KCHUNK), h * TQB:(h + 1) * TQB] = sh
                m_scr[h * 8:(h + 1) * 8, :] = jnp.maximum(m_scr[h * 8:(h + 1) * 8, :], _colmax8(sh))

    m_scr[...] = jnp.full(m_scr.shape, NEG, _F32)
    _for_chunks(nc, score_step)
    m_row = jnp.concatenate(
        [jnp.max(m_scr[h * 8:(h + 1) * 8, :], axis=0, keepdims=True) for h in range(B_HEADS)],
        axis=1)
    acc_scr[...] = jnp.zeros_like(acc_scr)

    def pv_step(c):
        r0 = pl.multiple_of(c * KCHUNK, KCHUNK)
        for p in range(n_pairs):
            cols = slice(2 * p * TQB, (2 * p + 2) * TQB)
            pr = jnp.exp2(sc_scr[pl.ds(r0, KCHUNK), cols] - m_row[:, cols]).astype(_BF16)
            acc_scr[p] += jnp.dot(vb_ref[0, c, p * PAIR_ROWS:(p + 1) * PAIR_ROWS, :], pr,
                                  preferred_element_type=_F32)

    _for_chunks(nc, pv_step)
    for p in range(n_pairs):
        inv = 1.0 / acc_scr[p, 2 * HEAD_DIM:2 * HEAD_DIM + 1, :]
        for hh in range(2):
            rows = slice((2 * p + hh) * HEAD_DIM, (2 * p + hh + 1) * HEAD_DIM)
            o = (acc_scr[p, hh * HEAD_DIM:(hh + 1) * HEAD_DIM, hh * TQB:(hh + 1) * TQB]
                 * inv[:, hh * TQB:(hh + 1) * TQB] * zb_ref[0, rows, :].astype(_F32))
            o_ref[0, rows, :] = o.astype(_BF16)


def _mix_b(qb_t, kb_n, vb_c, zb_t, qi_t, ki_n, wi_t, bias_b):
    B, _, S = qb_t.shape
    nq = S // TQB
    per_q = lambda rows: pl.BlockSpec((1, rows, TQB), lambda b, i: (b, 0, i))
    in_specs = [
        per_q(B_WIDTH),
        pl.BlockSpec((1, S, B_WIDTH), lambda b, i: (b, 0, 0)),
        pl.BlockSpec((1, S // KCHUNK, V_ROWS, KCHUNK), lambda b, i: (b, 0, 0, 0)),
        per_q(B_WIDTH),
        per_q(IDX_WIDTH),
        pl.BlockSpec((1, S, LANES), lambda b, i: (b, 0, 0)),
        per_q(IDX_HEADS),
        pl.BlockSpec((B_HEADS, 3, TQ, TQ), lambda b, i: (0, 0, 0, 0)),
    ]
    scratch = [
        pltpu.VMEM((S, TQB), _F32),
        pltpu.VMEM((S, TQB), _F32),
        pltpu.VMEM((S, B_HEADS * TQB), _F32),
        pltpu.VMEM((B_HEADS // 2, PAIR_ROWS, 2 * TQB), _F32),
        pltpu.VMEM((B_HEADS * 8, TQB), _F32),
        pltpu.VMEM((16, TQB), _F32),
        pltpu.VMEM((1, TQB), _F32),
        pltpu.VMEM((1, TQB), jnp.int32),
    ]
    return pl.pallas_call(
        _mix_b_kernel, grid=(B, nq), in_specs=in_specs,
        out_specs=per_q(B_WIDTH),
        out_shape=jax.ShapeDtypeStruct((B, B_WIDTH, S), _BF16),
        scratch_shapes=scratch,
        compiler_params=pltpu.CompilerParams(
            dimension_semantics=("parallel", "arbitrary"), vmem_limit_bytes=VMEM_LIMIT),
        name="mixer_b",
    )(qb_t, kb_n, vb_c, zb_t, qi_t, ki_n, wi_t, bias_b)


def _merge_kernel(x_ref, ya_ref, yb_ref, gt_ref, wpa_ref, wpb_ref, wo_ref, o_ref):
    pa = jnp.dot(wpa_ref[...], ya_ref[0], preferred_element_type=_F32)
    pb = jnp.dot(wpb_ref[...], yb_ref[0], preferred_element_type=_F32)
    merged = (gt_ref[0, :D_MODEL, :].astype(_F32) * pa
              + gt_ref[0, D_MODEL:, :].astype(_F32) * pb).astype(_BF16)
    out_t = jnp.dot(wo_ref[...], merged, preferred_element_type=_F32)
    o_ref[0] = x_ref[0] + out_t.T


def _merge(x, ya_t, yb_t, gt_t, wpa_t, wpb_t, wo_t):
    B, S, D = x.shape
    tm = TM_PROJ
    const = lambda b, t: (0, 0)
    in_specs = [
        pl.BlockSpec((1, tm, D), lambda b, t: (b, t, 0)),
        pl.BlockSpec((1, A_WIDTH, tm), lambda b, t: (b, 0, t)),
        pl.BlockSpec((1, B_WIDTH, tm), lambda b, t: (b, 0, t)),
        pl.BlockSpec((1, 2 * D, tm), lambda b, t: (b, 0, t)),
        pl.BlockSpec((D, A_WIDTH), const),
        pl.BlockSpec((D, B_WIDTH), const),
        pl.BlockSpec((D, D), const),
    ]
    return pl.pallas_call(
        _merge_kernel, grid=(B, S // tm), in_specs=in_specs,
        out_specs=pl.BlockSpec((1, tm, D), lambda b, t: (b, t, 0)),
        out_shape=jax.ShapeDtypeStruct((B, S, D), x.dtype),
        compiler_params=pltpu.CompilerParams(
            dimension_semantics=("parallel", "parallel"), vmem_limit_bytes=VMEM_LIMIT),
        name="merge_out",
    )(x, ya_t, yb_t, gt_t, wpa_t, wpb_t, wo_t)


def _t5_bucket(n):
    n = np.maximum(n, 0)
    max_exact = N_BUCKETS // 2
    nf = np.maximum(n, 1).astype(np.float64)
    large = max_exact + np.floor(np.log(nf / max_exact) / math.log(MAX_DISTANCE / max_exact)
                                 * (N_BUCKETS - max_exact)).astype(np.int64)
    large = np.minimum(large, N_BUCKETS - 1)
    return np.where(n < max_exact, n, large)


def _bias_tables(rel_bias):
    t = np.arange(TQ)[None, :]
    s = np.arange(TQ)[:, None]
    d_prev = t + TQ - s
    d_cur = t - s
    table_a = rel_bias[:, :A_Q_HEADS].astype(_F32).T * LOG2E
    table_b = rel_bias[:, A_Q_HEADS:].astype(_F32).T * LOG2E

    def tile(table, dist, windowed):
        onehot = (_t5_bucket(dist)[..., None] == np.arange(N_BUCKETS)).astype(np.float32)
        b = jnp.einsum("stk,hk->hst", onehot, table, precision=lax.Precision.HIGHEST)
        if windowed:
            b = jnp.where(((dist >= 0) & (dist < WINDOW))[None], b, NEG)
        return b

    bias_a = jnp.concatenate([tile(table_a, d_prev, True), tile(table_a, d_cur, True)], axis=1)
    far = jnp.broadcast_to(table_b[:, N_BUCKETS - 1][:, None, None], (B_HEADS, TQ, TQ))
    bias_b = jnp.stack([far, tile(table_b, d_prev, False), tile(table_b, d_cur, False)], axis=1)
    return bias_a, bias_b


def kernel(x, norm_g, w_in, qnorm_a, knorm_a, sinks_a, qnorm_b, knorm_b, rel_bias,
           w_proj_a, w_proj_b, w_out):
    assert norm_g.shape[0] == 1, "single-layer block"
    B, S, D = x.shape
    assert D == D_MODEL and S % TM_PROJ == 0 and S % KSUPER == 0 and S % TQB == 0 and TQB % TQ == 0
    assert MAX_DISTANCE <= TQ

    wt = w_in[0].T
    n_real = _SEG["kw"][0] + KW_REAL
    wt = jnp.concatenate(
        [wt[:n_real], jnp.zeros((_SEG["kw"][1] - n_real, D), wt.dtype), wt[n_real:]], axis=0)
    wt = wt.astype(_BF16)
    bcast = lambda g, scale: jnp.broadcast_to((g.astype(_F32) * scale)[:, None], (HEAD_DIM, TM_PROJ))
    q_scale = HEAD_DIM ** -0.5
    (qa_t, ka_n, va_t, za_t, qb_t, kb_n, vb_c, zb_t, qi_t, ki_n, wi_t, gt_t) = _inproj(
        x, norm_g[0][None, :].astype(_F32), wt,
        bcast(qnorm_a[0], q_scale * LOG2E), bcast(knorm_a[0], 1.0),
        bcast(qnorm_b[0], q_scale * LOG2E), bcast(knorm_b[0], 1.0))

    bias_a, bias_b = _bias_tables(rel_bias)
    sink_a = jnp.broadcast_to((sinks_a[0].astype(_F32) * LOG2E)[:, None], (A_Q_HEADS, TQ))
    ya_t = _mix_a(qa_t, ka_n, va_t, za_t, bias_a, sink_a)
    yb_t = _mix_b(qb_t, kb_n, vb_c, zb_t, qi_t, ki_n, wi_t, bias_b)
    return _merge(x, ya_t, yb_t, gt_t,
                  w_proj_a[0].T.astype(_BF16), w_proj_b[0].T.astype(_BF16), w_out[0].T.astype(_BF16))
```

```python
import functools
import math

import jax
import jax.numpy as jnp
import numpy as np
from jax import lax
from jax.experimental import pallas as pl
from jax.experimental.pallas import tpu as pltpu

D_MODEL = 1024
HEAD_DIM = 64
A_Q_HEADS = 8
A_KV_HEADS = 2
A_GROUPS = A_Q_HEADS // A_KV_HEADS
A_WIDTH = A_Q_HEADS * HEAD_DIM
A_KV_WIDTH = A_KV_HEADS * HEAD_DIM
B_HEADS = 8
B_WIDTH = B_HEADS * HEAD_DIM
IDX_HEADS = 8
IDX_DIM = 32
IDX_WIDTH = IDX_HEADS * IDX_DIM
WINDOW = 128
TOPK_MAX = 256
N_BUCKETS = 32
MAX_DISTANCE = 128
RMS_EPS = 1e-6

LANES = 128
TQ = 128
A_SUB = 4
TQB = 256
KCHUNK = 256
KSUPER = 2 * KCHUNK
VALUE_PASSES = 16
SEARCH_CHECKS = (1, 3, 5, 8)
TM_PROJ = 1024
NEG = -1e30
LOG2E = math.log2(math.e)
ONES_ROWS = 16
PAIR_ROWS = 2 * HEAD_DIM + ONES_ROWS
V_ROWS = (B_HEADS // 2) * PAIR_ROWS
VMEM_LIMIT = 56 * 1024 * 1024

_SEG = {}
_off = 0
for _name, _rows in (("qa", A_WIDTH), ("ka", A_KV_WIDTH), ("va", A_KV_WIDTH), ("za", A_WIDTH),
                     ("qb", B_WIDTH), ("kb", B_WIDTH), ("vb", B_WIDTH), ("zb", B_WIDTH),
                     ("qi", IDX_WIDTH), ("kw", LANES), ("gates", 2 * D_MODEL)):
    _SEG[_name] = (_off, _off + _rows)
    _off += _rows
PROJ_ROWS = _off
KW_REAL = IDX_DIM + IDX_HEADS

_F32 = jnp.float32
_BF16 = jnp.bfloat16
_NT = (((1,), (1,)), ((), ()))


def _inproj_kernel(x_ref, g_ref, wt_ref, gqa_ref, gka_ref, gqb_ref, gkb_ref,
                   qa_o, ka_o, va_o, za_o, qb_o, kb_o, vb_o, zb_o, qi_o, ki_o, wi_o, gt_o):
    tm = x_ref.shape[1]
    x = x_ref[0]
    ms = jnp.mean(x * x, axis=-1, keepdims=True)
    h = (x * lax.rsqrt(ms + RMS_EPS) * g_ref[...]).astype(_BF16)

    def proj(lo, hi):
        return lax.dot_general(wt_ref[lo:hi, :], h, _NT, preferred_element_type=_F32)

    def seg(name):
        return proj(*_SEG[name])

    def headnorm(p, gain_ref):
        nh = p.shape[0] // HEAD_DIM
        p3 = p.reshape(nh, HEAD_DIM, tm)
        r = lax.rsqrt(jnp.mean(p3 * p3, axis=1, keepdims=True) + RMS_EPS)
        return (p3 * r * gain_ref[...][None]).reshape(nh * HEAD_DIM, tm)

    def silu(p):
        return p * jax.nn.sigmoid(p)

    qa_o[0] = headnorm(seg("qa"), gqa_ref).astype(_BF16)
    ka_o[0] = headnorm(seg("ka"), gka_ref).T.astype(_BF16)
    va_o[0] = seg("va").astype(_BF16)
    za_o[0] = silu(seg("za")).astype(_BF16)
    qb_o[0] = headnorm(seg("qb"), gqb_ref).astype(_BF16)
    kb_o[0] = headnorm(seg("kb"), gkb_ref).T.astype(_BF16)
    vb = seg("vb").astype(_BF16)
    ones = jnp.ones((ONES_ROWS, KCHUNK), _BF16)
    for c in range(tm // KCHUNK):
        for p in range(B_HEADS // 2):
            vb_o[0, c, p * PAIR_ROWS:(p + 1) * PAIR_ROWS, :] = jnp.concatenate(
                [vb[p * LANES:(p + 1) * LANES, c * KCHUNK:(c + 1) * KCHUNK], ones], axis=0)
    zb_o[0] = silu(seg("zb")).astype(_BF16)
    qi_o[0] = seg("qi").astype(_BF16)
    kw = seg("kw")
    wi_o[0] = kw[IDX_DIM:KW_REAL, :]
    row = lax.broadcasted_iota(jnp.int32, kw.shape, 0)
    ki_o[0] = jnp.where(row < IDX_DIM, kw, 0.0).T.astype(_BF16)
    g_lo = _SEG["gates"][0]
    for c in range(2 * D_MODEL // 512):
        gt_o[0, c * 512:(c + 1) * 512, :] = jax.nn.sigmoid(
            proj(g_lo + c * 512, g_lo + (c + 1) * 512)).astype(_BF16)


def _inproj(x, g, wt, gqa, gka, gqb, gkb):
    B, S, D = x.shape
    tm = TM_PROJ
    nt = S // tm
    const = lambda b, t: (0, 0)
    tr = lambda rows: pl.BlockSpec((1, rows, tm), lambda b, t: (b, 0, t))
    nat = lambda cols: pl.BlockSpec((1, tm, cols), lambda b, t: (b, t, 0))
    sds = jax.ShapeDtypeStruct
    out_shape = (
        sds((B, A_WIDTH, S), _BF16), sds((B, S, A_KV_WIDTH), _BF16), sds((B, A_KV_WIDTH, S), _BF16),
        sds((B, A_WIDTH, S), _BF16),
        sds((B, B_WIDTH, S), _BF16), sds((B, S, B_WIDTH), _BF16),
        sds((B, S // KCHUNK, V_ROWS, KCHUNK), _BF16), sds((B, B_WIDTH, S), _BF16),
        sds((B, IDX_WIDTH, S), _BF16), sds((B, S, LANES), _BF16), sds((B, IDX_HEADS, S), _F32),
        sds((B, 2 * D_MODEL, S), _BF16),
    )
    out_specs = (
        tr(A_WIDTH), nat(A_KV_WIDTH), tr(A_KV_WIDTH), tr(A_WIDTH),
        tr(B_WIDTH), nat(B_WIDTH),
        pl.BlockSpec((1, tm // KCHUNK, V_ROWS, KCHUNK), lambda b, t: (b, t, 0, 0)), tr(B_WIDTH),
        tr(IDX_WIDTH), nat(LANES), tr(IDX_HEADS), tr(2 * D_MODEL),
    )
    in_specs = [
        pl.BlockSpec((1, tm, D), lambda b, t: (b, t, 0)),
        pl.BlockSpec((1, D), const),
        pl.BlockSpec((PROJ_ROWS, D), const, pipeline_mode=pl.Buffered(1)),
        pl.BlockSpec((HEAD_DIM, tm), const), pl.BlockSpec((HEAD_DIM, tm), const),
        pl.BlockSpec((HEAD_DIM, tm), const), pl.BlockSpec((HEAD_DIM, tm), const),
    ]
    return pl.pallas_call(
        _inproj_kernel, grid=(B, nt), in_specs=in_specs, out_specs=out_specs, out_shape=out_shape,
        compiler_params=pltpu.CompilerParams(
            dimension_semantics=("parallel", "parallel"), vmem_limit_bytes=VMEM_LIMIT),
        name="inproj",
    )(x, g, wt, gqa, gka, gqb, gkb)


def _mix_a_kernel(q_ref, kp_ref, kc_ref, vp_ref, vc_ref, z_ref, bias_ref, sink_ref, o_ref):
    i = pl.program_id(1)
    zeros = jnp.zeros((HEAD_DIM, TQ), _BF16)
    units = [(j, g) for j in range(A_SUB) for g in range(A_KV_HEADS)]

    def bands(j):
        cur = slice(j * TQ, (j + 1) * TQ)
        old = slice((j - 1) * TQ, j * TQ)
        k_old = kp_ref[0] if j == 0 else kc_ref[0, old, :]
        v_old = vp_ref[0] if j == 0 else vc_ref[0, :, old]
        return (jnp.concatenate([k_old, kc_ref[0, cur, :]], axis=0),
                jnp.concatenate([v_old, vc_ref[0, :, cur]], axis=1))

    scores = []
    for j, g in units:
        cur = slice(j * TQ, (j + 1) * TQ)
        cols = []
        for hh in range(A_GROUPS):
            h = g * A_GROUPS + hh
            parts = [zeros] * A_KV_HEADS
            parts[g] = q_ref[0, h * HEAD_DIM:(h + 1) * HEAD_DIM, cur]
            cols.append(jnp.concatenate(parts, axis=0))
        rhs = jnp.concatenate(cols, axis=1)
        scores.append(jnp.dot(bands(j)[0], rhs, preferred_element_type=_F32))

    pts, invs = [], []
    for (j, g), sc in zip(units, scores):
        probs, inv = [], []
        for hh in range(A_GROUPS):
            h = g * A_GROUPS + hh
            s = sc[:, hh * TQ:(hh + 1) * TQ] + bias_ref[h]
            if j == 0:
                s = jnp.concatenate([jnp.where(i > 0, s[:TQ], NEG), s[TQ:]], axis=0)
            sink = sink_ref[h:h + 1, :]
            m = jnp.maximum(jnp.max(s, axis=0, keepdims=True), sink)
            p = jnp.exp2(s - m)
            denom = jnp.sum(p, axis=0, keepdims=True) + jnp.exp2(sink - m)
            probs.append(p.astype(_BF16))
            inv.append(1.0 / denom)
        pts.append(jnp.concatenate(probs, axis=1))
        invs.append(inv)

    for (j, g), pt, inv in zip(units, pts, invs):
        cur = slice(j * TQ, (j + 1) * TQ)
        out = jnp.dot(bands(j)[1][g * HEAD_DIM:(g + 1) * HEAD_DIM, :], pt,
                      preferred_element_type=_F32)
        for hh in range(A_GROUPS):
            h = g * A_GROUPS + hh
            rows = slice(h * HEAD_DIM, (h + 1) * HEAD_DIM)
            o = out[:, hh * TQ:(hh + 1) * TQ] * inv[hh] * z_ref[0, rows, cur].astype(_F32)
            o_ref[0, rows, cur] = o.astype(_BF16)


def _mix_a(qa_t, ka_n, va_t, za_t, bias_a, sink_a):
    B, _, S = qa_t.shape
    tqa = A_SUB * TQ
    prev = lambda i: jnp.maximum(i * A_SUB - 1, 0)
    in_specs = [
        pl.BlockSpec((1, A_WIDTH, tqa), lambda b, i: (b, 0, i)),
        pl.BlockSpec((1, TQ, A_KV_WIDTH), lambda b, i: (b, prev(i), 0)),
        pl.BlockSpec((1, tqa, A_KV_WIDTH), lambda b, i: (b, i, 0)),
        pl.BlockSpec((1, A_KV_WIDTH, TQ), lambda b, i: (b, 0, prev(i))),
        pl.BlockSpec((1, A_KV_WIDTH, tqa), lambda b, i: (b, 0, i)),
        pl.BlockSpec((1, A_WIDTH, tqa), lambda b, i: (b, 0, i)),
        pl.BlockSpec((A_Q_HEADS, 2 * TQ, TQ), lambda b, i: (0, 0, 0)),
        pl.BlockSpec((A_Q_HEADS, TQ), lambda b, i: (0, 0)),
    ]
    return pl.pallas_call(
        _mix_a_kernel, grid=(B, S // tqa), in_specs=in_specs,
        out_specs=pl.BlockSpec((1, A_WIDTH, tqa), lambda b, i: (b, 0, i)),
        out_shape=jax.ShapeDtypeStruct((B, A_WIDTH, S), _BF16),
        compiler_params=pltpu.CompilerParams(
            dimension_semantics=("parallel", "parallel"), vmem_limit_bytes=VMEM_LIMIT),
        name="mixer_a",
    )(qa_t, ka_n, ka_n, va_t, va_t, za_t, bias_a, sink_a)


def _key_to_f32(key):
    return pltpu.bitcast(jnp.where(key < 0, key ^ jnp.int32(0x7FFFFFFF), key), _F32)


def _f32_to_key(v):
    bits = pltpu.bitcast(v, jnp.int32)
    return jnp.where(bits < 0, bits ^ jnp.int32(0x7FFFFFFF), bits)


def _colsum8(v):
    return jnp.sum(v.reshape(v.shape[0] // 8, 8, v.shape[1]), axis=0)


def _colmin8(v):
    return jnp.min(v.reshape(v.shape[0] // 8, 8, v.shape[1]), axis=0)


def _colmax8(v):
    return jnp.max(v.reshape(v.shape[0] // 8, 8, v.shape[1]), axis=0)


def _for_chunks(n, step):
    def body(t, carry):
        for u in range(4):
            step(4 * t + u)
        return carry

    lax.fori_loop(0, n // 4, body, 0)
    base = (n // 4) * 4
    for rest in (1, 2, 3):
        @pl.when(n - base == rest)
        def _(rest=rest):
            for u in range(rest):
                step(base + u)


def _mix_b_kernel(qb_ref, kb_ref, vb_ref, zb_ref, qi_ref, ki_ref, wi_ref, bias_ref, o_ref,
                  keys_scr, mb_scr, sc_scr, acc_scr, m_scr, ext_scr, thr_scr, tie_scr):
    i = pl.program_id(1)
    nc = ((i + 1) * TQB + KCHUNK - 1) // KCHUNK
    ns = (nc + 1) // 2
    top_k = TOPK_MAX

    qi = qi_ref[0]
    zpad = jnp.zeros((LANES - IDX_DIM, TQB), _BF16)
    rhs_i = jnp.concatenate(
        [jnp.concatenate([qi[h * IDX_DIM:(h + 1) * IDX_DIM], zpad], axis=0)
         for h in range(IDX_HEADS)], axis=1)
    w = wi_ref[0] * (IDX_DIM ** -0.5 * IDX_HEADS ** -0.5)
    s_minus_t = (lax.broadcasted_iota(jnp.int32, (KCHUNK, TQB), 0)
                 - lax.broadcasted_iota(jnp.int32, (KCHUNK, TQB), 1))

    def index_step(c):
        r0 = pl.multiple_of(c * KCHUNK, KCHUNK)
        d = jnp.dot(ki_ref[0, pl.ds(r0, KCHUNK), :], rhs_i,
                    preferred_element_type=_F32)
        acc = w[0:1, :] * jnp.maximum(d[:, 0:TQB], 0.0)
        for h in range(1, IDX_HEADS):
            acc = acc + w[h:h + 1, :] * jnp.maximum(d[:, h * TQB:(h + 1) * TQB], 0.0)
        causal = s_minus_t <= (i * TQB - r0)
        score = jnp.where(causal, acc, -jnp.inf)
        keys_scr[pl.ds(r0, KCHUNK), :] = score
        ext_scr[0:8, :] = jnp.maximum(ext_scr[0:8, :], _colmax8(score))
        ext_scr[8:16, :] = jnp.minimum(ext_scr[8:16, :], _colmin8(jnp.where(causal, acc, jnp.inf)))
        ext_scr[16:24, :] += _colsum8(jnp.where(score >= 0.0, 1.0, 0.0))
        ext_scr[24:32, :] += _colsum8(jnp.where(score > 0.0, 1.0, 0.0))

    ext_scr[0:8, :] = jnp.full((8, TQB), -jnp.inf, _F32)
    ext_scr[8:16, :] = jnp.full((8, TQB), jnp.inf, _F32)
    ext_scr[16:32, :] = jnp.zeros((16, TQB), _F32)
    _for_chunks(nc, index_step)

    @pl.when(nc % 2 == 1)
    def _():
        keys_scr[pl.ds(pl.multiple_of(nc * KCHUNK, KCHUNK), KCHUNK), :] = jnp.full(
            (KCHUNK, TQB), -jnp.inf, _F32)

    def count_rows(src, hit):
        n_acc = 4

        def bump(accs, r0, n_rows):
            accs = list(accs)
            rows = src[pl.ds(r0, n_rows), :]
            for j in range(n_rows // 8):
                a = accs[j % n_acc]
                accs[j % n_acc] = jnp.where(hit(rows[j * 8:(j + 1) * 8]), a + 1, a)
            return tuple(accs)

        accs = lax.fori_loop(
            0, nc // 2, lambda c, accs: bump(accs, pl.multiple_of(c * KSUPER, KSUPER), KSUPER),
            tuple(jnp.zeros((8, TQB), jnp.int32) for _ in range(n_acc)))
        accs = lax.cond(nc % 2 == 1,
                        lambda accs: bump(accs, pl.multiple_of((nc - 1) * KCHUNK, KCHUNK), KCHUNK),
                        lambda accs: accs, accs)
        acc = accs[0]
        for a in accs[1:]:
            acc = acc + a
        return jnp.sum(acc, axis=0, keepdims=True)

    def count_ge(cand, strict=False):
        return count_rows(keys_scr, (lambda r: r > cand) if strict else (lambda r: r >= cand))

    @pl.when(i * TQB + TQB <= top_k)
    def _():
        thr_scr[...] = jnp.full((1, TQB), -jnp.inf, _F32)
        tie_scr[...] = jnp.full((1, TQB), -1, jnp.int32)

    @pl.when(i * TQB + TQB > top_k)
    def _():
        cnt_nonneg = jnp.sum(ext_scr[16:24, :], axis=0, keepdims=True).astype(jnp.int32)
        cnt_pos = jnp.sum(ext_scr[24:32, :], axis=0, keepdims=True).astype(jnp.int32)
        key_max = _f32_to_key(jnp.max(ext_scr[0:8, :], axis=0, keepdims=True))
        key_min = _f32_to_key(jnp.min(ext_scr[8:16, :], axis=0, keepdims=True))
        n_causal = i * TQB + 1 + lax.broadcasted_iota(jnp.int32, (1, TQB), 1)
        nonneg = cnt_nonneg >= top_k
        lo = jnp.where(nonneg, 0, key_min)
        cnt_lo = jnp.where(nonneg, cnt_nonneg, n_causal)
        hi = jnp.where(nonneg, jnp.where(cnt_pos < top_k, 1, key_max + 1), 0)

        def halve_values(_, state):
            lo_v, hi_v, cnt_lo = state
            mid = lo_v + (hi_v - lo_v) * 0.5
            mid = jnp.where(mid > lo_v, jnp.where(mid < hi_v, mid, lo_v), lo_v)
            cnt = count_ge(mid)
            ok = cnt >= top_k
            return jnp.where(ok, mid, lo_v), jnp.where(ok, hi_v, mid), jnp.where(ok, cnt, cnt_lo)

        def halve_keys(_, state):
            lo, hi, cnt_lo = state
            mid = lo + lax.shift_right_logical(hi - lo, 1)
            cnt = count_ge(_key_to_f32(mid))
            ok = cnt >= top_k
            return jnp.where(ok, mid, lo), jnp.where(ok, hi, mid), jnp.where(ok, cnt, cnt_lo)

        def finished(state):
            lo, hi, cnt_lo = state
            done = jnp.where(cnt_lo == top_k, 1, jnp.where(hi - lo <= 1, 1, 0))
            return jnp.min(done) == 1

        lo_v, hi_v, cnt_lo = lax.fori_loop(0, VALUE_PASSES, halve_values,
                                           (_key_to_f32(lo), _key_to_f32(hi), cnt_lo))
        state = (_f32_to_key(lo_v), _f32_to_key(hi_v), cnt_lo)
        stops = (0,) + SEARCH_CHECKS + (31,)
        state = lax.fori_loop(stops[0], stops[1], halve_keys, state)
        for first, last in zip(stops[1:-1], stops[2:]):
            state = lax.cond(finished(state), lambda s: s,
                             functools.partial(lax.fori_loop, first, last, halve_keys), state)
        lo, _, cnt_ge = state
        thr = _key_to_f32(lo)
        thr_scr[...] = thr
        tie_scr[...] = jnp.full((1, TQB), 2 ** 30, jnp.int32)

        @pl.when(jnp.max(cnt_ge) > top_k)
        def _():
            need = (top_k - count_ge(thr, strict=True)).astype(_F32)
            lower = (lax.broadcasted_iota(jnp.int32, (KCHUNK, KCHUNK), 0)
                     >= lax.broadcasted_iota(jnp.int32, (KCHUNK, KCHUNK), 1))
            lower = jnp.where(lower, 1.0, 0.0).astype(_BF16)

            def prefix_body(c, state):
                before, below = state
                r0 = pl.multiple_of(c * KCHUNK, KCHUNK)
                tied = jnp.where(keys_scr[pl.ds(r0, KCHUNK), :] == thr, 1.0, 0.0).astype(_BF16)
                upto = jnp.dot(lower, tied, preferred_element_type=_F32) + before
                below = below + jnp.sum(jnp.where(upto < need, 1, 0).reshape(KCHUNK // 8, 8, TQB), axis=0)
                return upto[KCHUNK - 1:KCHUNK, :], below

            _, below = lax.fori_loop(0, nc, prefix_body,
                                     (jnp.zeros((1, TQB), _F32), jnp.zeros((8, TQB), jnp.int32)))
            tie_scr[...] = jnp.sum(below, axis=0, keepdims=True)

    thr = thr_scr[...]
    tie = tie_scr[...]
    row = lax.broadcasted_iota(jnp.int32, (KSUPER, TQB), 0)

    def mask_body(c, carry):
        r0 = pl.multiple_of(c * KSUPER, KSUPER)
        k = keys_scr[pl.ds(r0, KSUPER), :]
        mb_scr[pl.ds(r0, KSUPER), :] = jnp.where(
            k == thr, jnp.where(row <= tie - r0, 0.0, NEG), jnp.where(k > thr, 0.0, NEG))
        return carry

    lax.fori_loop(0, ns, mask_body, 0)

    zeros = jnp.zeros((HEAD_DIM, TQB), _BF16)
    n_pairs = B_HEADS // 2
    rhs = []
    for p in range(n_pairs):
        q0 = qb_ref[0, (2 * p) * HEAD_DIM:(2 * p + 1) * HEAD_DIM, :]
        q1 = qb_ref[0, (2 * p + 1) * HEAD_DIM:(2 * p + 2) * HEAD_DIM, :]
        rhs.append(jnp.concatenate([jnp.concatenate([q0, zeros], axis=0),
                                    jnp.concatenate([zeros, q1], axis=0)], axis=1))

    def score_step(c):
        r0 = pl.multiple_of(c * KCHUNK, KCHUNK)
        for p in range(n_pairs):
            s = jnp.dot(kb_ref[0, pl.ds(r0, KCHUNK), p * LANES:(p + 1) * LANES], rhs[p],
                        preferred_element_type=_F32)
            for hh in range(2):
                h = 2 * p + hh
                parts = []
                for half in range(KCHUNK // TQ):
                    jb = c * (KCHUNK // TQ) + half
                    rs = pl.multiple_of(r0 + half * TQ, TQ)
                    subs = []
                    for sub in range(TQB // TQ):
                        kind = jnp.clip(jb - (i * (TQB // TQ) + sub) + 2, 0, 2)
                        lanes = slice(sub * TQ, (sub + 1) * TQ)
                        subs.append(s[half * TQ:(half + 1) * TQ, hh * TQB + sub * TQ:hh * TQB + (sub + 1) * TQ]
                                    + bias_ref[h, kind] + mb_scr[pl.ds(rs, TQ), lanes])
                    parts.append(jnp.concatenate(subs, axis=1))
                sh = jnp.concatenate(parts, axis=0)
                sc_scr[pl.ds(r0, KCHUNK), h * TQB:(h + 1) * TQB] = sh
                m_scr[h * 8:(h + 1) * 8, :] = jnp.maximum(m_scr[h * 8:(h + 1) * 8, :], _colmax8(sh))

    m_scr[...] = jnp.full(m_scr.shape, NEG, _F32)
    _for_chunks(nc, score_step)
    m_row = jnp.concatenate(
        [jnp.max(m_scr[h * 8:(h + 1) * 8, :], axis=0, keepdims=True) for h in range(B_HEADS)],
        axis=1)
    acc_scr[...] = jnp.zeros_like(acc_scr)

    def pv_step(c):
        r0 = pl.multiple_of(c * KCHUNK, KCHUNK)
        for p in range(n_pairs):
            cols = slice(2 * p * TQB, (2 * p + 2) * TQB)
            pr = jnp.exp2(sc_scr[pl.ds(r0, KCHUNK), cols] - m_row[:, cols]).astype(_BF16)
            acc_scr[p] += jnp.dot(vb_ref[0, c, p * PAIR_ROWS:(p + 1) * PAIR_ROWS, :], pr,
                                  preferred_element_type=_F32)

    _for_chunks(nc, pv_step)
    for p in range(n_pairs):
        inv = 1.0 / acc_scr[p, 2 * HEAD_DIM:2 * HEAD_DIM + 1, :]
        for hh in range(2):
            rows = slice((2 * p + hh) * HEAD_DIM, (2 * p + hh + 1) * HEAD_DIM)
            o = (acc_scr[p, hh * HEAD_DIM:(hh + 1) * HEAD_DIM, hh * TQB:(hh + 1) * TQB]
                 * inv[:, hh * TQB:(hh + 1) * TQB] * zb_ref[0, rows, :].astype(_F32))
            o_ref[0, rows, :] = o.astype(_BF16)


def _mix_b(qb_t, kb_n, vb_c, zb_t, qi_t, ki_n, wi_t, bias_b):
    B, _, S = qb_t.shape
    nq = S // TQB
    per_q = lambda rows: pl.BlockSpec((1, rows, TQB), lambda b, i: (b, 0, i))
    in_specs = [
        per_q(B_WIDTH),
        pl.BlockSpec((1, S, B_WIDTH), lambda b, i: (b, 0, 0)),
        pl.BlockSpec((1, S // KCHUNK, V_ROWS, KCHUNK), lambda b, i: (b, 0, 0, 0)),
        per_q(B_WIDTH),
        per_q(IDX_WIDTH),
        pl.BlockSpec((1, S, LANES), lambda b, i: (b, 0, 0)),
        per_q(IDX_HEADS),
        pl.BlockSpec((B_HEADS, 3, TQ, TQ), lambda b, i: (0, 0, 0, 0)),
    ]
    scratch = [
        pltpu.VMEM((S, TQB), _F32),
        pltpu.VMEM((S, TQB), _F32),
        pltpu.VMEM((S, B_HEADS * TQB), _F32),
        pltpu.VMEM((B_HEADS // 2, PAIR_ROWS, 2 * TQB), _F32),
        pltpu.VMEM((B_HEADS * 8, TQB), _F32),
        pltpu.VMEM((32, TQB), _F32),
        pltpu.VMEM((1, TQB), _F32),
        pltpu.VMEM((1, TQB), jnp.int32),
    ]
    return pl.pallas_call(
        _mix_b_kernel, grid=(B, nq), in_specs=in_specs,
        out_specs=per_q(B_WIDTH),
        out_shape=jax.ShapeDtypeStruct((B, B_WIDTH, S), _BF16),
        scratch_shapes=scratch,
        compiler_params=pltpu.CompilerParams(
            dimension_semantics=("parallel", "arbitrary"), vmem_limit_bytes=VMEM_LIMIT),
        name="mixer_b",
    )(qb_t, kb_n, vb_c, zb_t, qi_t, ki_n, wi_t, bias_b)


def _merge_kernel(x_ref, ya_ref, yb_ref, gt_ref, wpa_ref, wpb_ref, wo_ref, o_ref):
    pa = jnp.dot(wpa_ref[...], ya_ref[0], preferred_element_type=_F32)
    pb = jnp.dot(wpb_ref[...], yb_ref[0], preferred_element_type=_F32)
    merged = (gt_ref[0, :D_MODEL, :].astype(_F32) * pa
              + gt_ref[0, D_MODEL:, :].astype(_F32) * pb).astype(_BF16)
    out_t = jnp.dot(wo_ref[...], merged, preferred_element_type=_F32)
    o_ref[0] = x_ref[0] + out_t.T


def _merge(x, ya_t, yb_t, gt_t, wpa_t, wpb_t, wo_t):
    B, S, D = x.shape
    tm = TM_PROJ
    const = lambda b, t: (0, 0)
    in_specs = [
        pl.BlockSpec((1, tm, D), lambda b, t: (b, t, 0)),
        pl.BlockSpec((1, A_WIDTH, tm), lambda b, t: (b, 0, t)),
        pl.BlockSpec((1, B_WIDTH, tm), lambda b, t: (b, 0, t)),
        pl.BlockSpec((1, 2 * D, tm), lambda b, t: (b, 0, t)),
        pl.BlockSpec((D, A_WIDTH), const),
        pl.BlockSpec((D, B_WIDTH), const),
        pl.BlockSpec((D, D), const),
    ]
    return pl.pallas_call(
        _merge_kernel, grid=(B, S // tm), in_specs=in_specs,
        out_specs=pl.BlockSpec((1, tm, D), lambda b, t: (b, t, 0)),
        out_shape=jax.ShapeDtypeStruct((B, S, D), x.dtype),
        compiler_params=pltpu.CompilerParams(
            dimension_semantics=("parallel", "parallel"), vmem_limit_bytes=VMEM_LIMIT),
        name="merge_out",
    )(x, ya_t, yb_t, gt_t, wpa_t, wpb_t, wo_t)


def _t5_bucket(n):
    n = np.maximum(n, 0)
    max_exact = N_BUCKETS // 2
    nf = np.maximum(n, 1).astype(np.float64)
    large = max_exact + np.floor(np.log(nf / max_exact) / math.log(MAX_DISTANCE / max_exact)
                                 * (N_BUCKETS - max_exact)).astype(np.int64)
    large = np.minimum(large, N_BUCKETS - 1)
    return np.where(n < max_exact, n, large)


def _bias_tables(rel_bias):
    t = np.arange(TQ)[None, :]
    s = np.arange(TQ)[:, None]
    d_prev = t + TQ - s
    d_cur = t - s
    table_a = rel_bias[:, :A_Q_HEADS].astype(_F32).T * LOG2E
    table_b = rel_bias[:, A_Q_HEADS:].astype(_F32).T * LOG2E

    def tile(table, dist, windowed):
        onehot = (_t5_bucket(dist)[..., None] == np.arange(N_BUCKETS)).astype(np.float32)
        b = jnp.einsum("stk,hk->hst", onehot, table, precision=lax.Precision.HIGHEST)
        if windowed:
            b = jnp.where(((dist >= 0) & (dist < WINDOW))[None], b, NEG)
        return b

    bias_a = jnp.concatenate([tile(table_a, d_prev, True), tile(table_a, d_cur, True)], axis=1)
    far = jnp.broadcast_to(table_b[:, N_BUCKETS - 1][:, None, None], (B_HEADS, TQ, TQ))
    bias_b = jnp.stack([far, tile(table_b, d_prev, False), tile(table_b, d_cur, False)], axis=1)
    return bias_a, bias_b


def kernel(x, norm_g, w_in, qnorm_a, knorm_a, sinks_a, qnorm_b, knorm_b, rel_bias,
           w_proj_a, w_proj_b, w_out):
    assert norm_g.shape[0] == 1, "single-layer block"
    B, S, D = x.shape
    assert D == D_MODEL and S % TM_PROJ == 0 and S % KSUPER == 0 and S % TQB == 0 and TQB % TQ == 0
    assert MAX_DISTANCE <= TQ

    wt = w_in[0].T
    n_real = _SEG["kw"][0] + KW_REAL
    wt = jnp.concatenate(
        [wt[:n_real], jnp.zeros((_SEG["kw"][1] - n_real, D), wt.dtype), wt[n_real:]], axis=0)
    wt = wt.astype(_BF16)
    bcast = lambda g, scale: jnp.broadcast_to((g.astype(_F32) * scale)[:, None], (HEAD_DIM, TM_PROJ))
    q_scale = HEAD_DIM ** -0.5
    (qa_t, ka_n, va_t, za_t, qb_t, kb_n, vb_c, zb_t, qi_t, ki_n, wi_t, gt_t) = _inproj(
        x, norm_g[0][None, :].astype(_F32), wt,
        bcast(qnorm_a[0], q_scale * LOG2E), bcast(knorm_a[0], 1.0),
        bcast(qnorm_b[0], q_scale * LOG2E), bcast(knorm_b[0], 1.0))

    bias_a, bias_b = _bias_tables(rel_bias)
    sink_a = jnp.broadcast_to((sinks_a[0].astype(_F32) * LOG2E)[:, None], (A_Q_HEADS, TQ))
    ya_t = _mix_a(qa_t, ka_n, va_t, za_t, bias_a, sink_a)
    yb_t = _mix_b(qb_t, kb_n, vb_c, zb_t, qi_t, ki_n, wi_t, bias_b)
    return _merge(x, ya_t, yb_t, gt_t,
                  w_proj_a[0].T.astype(_BF16), w_proj_b[0].T.astype(_BF16), w_out[0].T.astype(_BF16))
```

```python
import functools
import math

import jax
import jax.numpy as jnp
import numpy as np
from jax import lax
from jax.experimental import pallas as pl
from jax.experimental.pallas import tpu as pltpu

D_MODEL = 1024
HEAD_DIM = 64
A_Q_HEADS = 8
A_KV_HEADS = 2
A_GROUPS = A_Q_HEADS // A_KV_HEADS
A_WIDTH = A_Q_HEADS * HEAD_DIM
A_KV_WIDTH = A_KV_HEADS * HEAD_DIM
B_HEADS = 8
B_WIDTH = B_HEADS * HEAD_DIM
IDX_HEADS = 8
IDX_DIM = 32
IDX_WIDTH = IDX_HEADS * IDX_DIM
WINDOW = 128
TOPK_MAX = 256
N_BUCKETS = 32
MAX_DISTANCE = 128
RMS_EPS = 1e-6

LANES = 128
TQ = 128
A_SUB = 4
TQB = 256
KCHUNK = 256
KSUPER = 2 * KCHUNK
VALUE_PASSES = 16
SEARCH_CHECKS = (1, 3, 5, 8)
TM_PROJ = 1024
NEG = -1e30
LOG2E = math.log2(math.e)
ONES_ROWS = 16
PAIR_ROWS = 2 * HEAD_DIM + ONES_ROWS
V_ROWS = (B_HEADS // 2) * PAIR_ROWS
VMEM_LIMIT = 56 * 1024 * 1024

_SEG = {}
_off = 0
for _name, _rows in (("qa", A_WIDTH), ("ka", A_KV_WIDTH), ("va", A_KV_WIDTH), ("za", A_WIDTH),
                     ("qb", B_WIDTH), ("kb", B_WIDTH), ("vb", B_WIDTH), ("zb", B_WIDTH),
                     ("qi", IDX_WIDTH), ("kw", LANES), ("gates", 2 * D_MODEL)):
    _SEG[_name] = (_off, _off + _rows)
    _off += _rows
PROJ_ROWS = _off
KW_REAL = IDX_DIM + IDX_HEADS

_F32 = jnp.float32
_BF16 = jnp.bfloat16
_NT = (((1,), (1,)), ((), ()))


def _inproj_kernel(x_ref, g_ref, wt_ref, gqa_ref, gka_ref, gqb_ref, gkb_ref,
                   qa_o, ka_o, va_o, za_o, qb_o, kb_o, vb_o, zb_o, qi_o, ki_o, wi_o, gt_o):
    tm = x_ref.shape[1]
    x = x_ref[0]
    ms = jnp.mean(x * x, axis=-1, keepdims=True)
    h = (x * lax.rsqrt(ms + RMS_EPS) * g_ref[...]).astype(_BF16)

    def proj(lo, hi):
        return lax.dot_general(wt_ref[lo:hi, :], h, _NT, preferred_element_type=_F32)

    def seg(name):
        return proj(*_SEG[name])

    def headnorm(p, gain_ref):
        nh = p.shape[0] // HEAD_DIM
        p3 = p.reshape(nh, HEAD_DIM, tm)
        r = lax.rsqrt(jnp.mean(p3 * p3, axis=1, keepdims=True) + RMS_EPS)
        return (p3 * r * gain_ref[...][None]).reshape(nh * HEAD_DIM, tm)

    def silu(p):
        return p * jax.nn.sigmoid(p)

    qa_o[0] = headnorm(seg("qa"), gqa_ref).astype(_BF16)
    ka_o[0] = headnorm(seg("ka"), gka_ref).T.astype(_BF16)
    va_o[0] = seg("va").astype(_BF16)
    za_o[0] = silu(seg("za")).astype(_BF16)
    qb_o[0] = headnorm(seg("qb"), gqb_ref).astype(_BF16)
    kb_o[0] = headnorm(seg("kb"), gkb_ref).T.astype(_BF16)
    vb = seg("vb").astype(_BF16)
    ones = jnp.ones((ONES_ROWS, KCHUNK), _BF16)
    for c in range(tm // KCHUNK):
        for p in range(B_HEADS // 2):
            vb_o[0, c, p * PAIR_ROWS:(p + 1) * PAIR_ROWS, :] = jnp.concatenate(
                [vb[p * LANES:(p + 1) * LANES, c * KCHUNK:(c + 1) * KCHUNK], ones], axis=0)
    zb_o[0] = silu(seg("zb")).astype(_BF16)
    qi_o[0] = seg("qi").astype(_BF16)
    kw = seg("kw")
    wi_o[0] = kw[IDX_DIM:KW_REAL, :]
    row = lax.broadcasted_iota(jnp.int32, kw.shape, 0)
    ki_o[0] = jnp.where(row < IDX_DIM, kw, 0.0).T.astype(_BF16)
    g_lo = _SEG["gates"][0]
    for c in range(2 * D_MODEL // 512):
        gt_o[0, c * 512:(c + 1) * 512, :] = jax.nn.sigmoid(
            proj(g_lo + c * 512, g_lo + (c + 1) * 512)).astype(_BF16)


def _inproj(x, g, wt, gqa, gka, gqb, gkb):
    B, S, D = x.shape
    tm = TM_PROJ
    nt = S // tm
    const = lambda b, t: (0, 0)
    tr = lambda rows: pl.BlockSpec((1, rows, tm), lambda b, t: (b, 0, t))
    nat = lambda cols: pl.BlockSpec((1, tm, cols), lambda b, t: (b, t, 0))
    sds = jax.ShapeDtypeStruct
    out_shape = (
        sds((B, A_WIDTH, S), _BF16), sds((B, S, A_KV_WIDTH), _BF16), sds((B, A_KV_WIDTH, S), _BF16),
        sds((B, A_WIDTH, S), _BF16),
        sds((B, B_WIDTH, S), _BF16), sds((B, S, B_WIDTH), _BF16),
        sds((B, S // KCHUNK, V_ROWS, KCHUNK), _BF16), sds((B, B_WIDTH, S), _BF16),
        sds((B, IDX_WIDTH, S), _BF16), sds((B, S, LANES), _BF16), sds((B, IDX_HEADS, S), _F32),
        sds((B, 2 * D_MODEL, S), _BF16),
    )
    out_specs = (
        tr(A_WIDTH), nat(A_KV_WIDTH), tr(A_KV_WIDTH), tr(A_WIDTH),
        tr(B_WIDTH), nat(B_WIDTH),
        pl.BlockSpec((1, tm // KCHUNK, V_ROWS, KCHUNK), lambda b, t: (b, t, 0, 0)), tr(B_WIDTH),
        tr(IDX_WIDTH), nat(LANES), tr(IDX_HEADS), tr(2 * D_MODEL),
    )
    in_specs = [
        pl.BlockSpec((1, tm, D), lambda b, t: (b, t, 0)),
        pl.BlockSpec((1, D), const),
        pl.BlockSpec((PROJ_ROWS, D), const, pipeline_mode=pl.Buffered(1)),
        pl.BlockSpec((HEAD_DIM, tm), const), pl.BlockSpec((HEAD_DIM, tm), const),
        pl.BlockSpec((HEAD_DIM, tm), const), pl.BlockSpec((HEAD_DIM, tm), const),
    ]
    return pl.pallas_call(
        _inproj_kernel, grid=(B, nt), in_specs=in_specs, out_specs=out_specs, out_shape=out_shape,
        compiler_params=pltpu.CompilerParams(
            dimension_semantics=("parallel", "parallel"), vmem_limit_bytes=VMEM_LIMIT),
        name="inproj",
    )(x, g, wt, gqa, gka, gqb, gkb)


def _mix_a_kernel(q_ref, kp_ref, kc_ref, vp_ref, vc_ref, z_ref, bias_ref, sink_ref, o_ref):
    i = pl.program_id(1)
    zeros = jnp.zeros((HEAD_DIM, TQ), _BF16)
    units = [(j, g) for j in range(A_SUB) for g in range(A_KV_HEADS)]

    def bands(j):
        cur = slice(j * TQ, (j + 1) * TQ)
        old = slice((j - 1) * TQ, j * TQ)
        k_old = kp_ref[0] if j == 0 else kc_ref[0, old, :]
        v_old = vp_ref[0] if j == 0 else vc_ref[0, :, old]
        return (jnp.concatenate([k_old, kc_ref[0, cur, :]], axis=0),
                jnp.concatenate([v_old, vc_ref[0, :, cur]], axis=1))

    scores = []
    for j, g in units:
        cur = slice(j * TQ, (j + 1) * TQ)
        cols = []
        for hh in range(A_GROUPS):
            h = g * A_GROUPS + hh
            parts = [zeros] * A_KV_HEADS
            parts[g] = q_ref[0, h * HEAD_DIM:(h + 1) * HEAD_DIM, cur]
            cols.append(jnp.concatenate(parts, axis=0))
        rhs = jnp.concatenate(cols, axis=1)
        scores.append(jnp.dot(bands(j)[0], rhs, preferred_element_type=_F32))

    pts, invs = [], []
    for (j, g), sc in zip(units, scores):
        probs, inv = [], []
        for hh in range(A_GROUPS):
            h = g * A_GROUPS + hh
            s = sc[:, hh * TQ:(hh + 1) * TQ] + bias_ref[h]
            if j == 0:
                s = jnp.concatenate([jnp.where(i > 0, s[:TQ], NEG), s[TQ:]], axis=0)
            sink = sink_ref[h:h + 1, :]
            m = jnp.maximum(jnp.max(s, axis=0, keepdims=True), sink)
            p = jnp.exp2(s - m)
            denom = jnp.sum(p, axis=0, keepdims=True) + jnp.exp2(sink - m)
            probs.append(p.astype(_BF16))
            inv.append(1.0 / denom)
        pts.append(jnp.concatenate(probs, axis=1))
        invs.append(inv)

    for (j, g), pt, inv in zip(units, pts, invs):
        cur = slice(j * TQ, (j + 1) * TQ)
        out = jnp.dot(bands(j)[1][g * HEAD_DIM:(g + 1) * HEAD_DIM, :], pt,
                      preferred_element_type=_F32)
        for hh in range(A_GROUPS):
            h = g * A_GROUPS + hh
            rows = slice(h * HEAD_DIM, (h + 1) * HEAD_DIM)
            o = out[:, hh * TQ:(hh + 1) * TQ] * inv[hh] * z_ref[0, rows, cur].astype(_F32)
            o_ref[0, rows, cur] = o.astype(_BF16)


def _mix_a(qa_t, ka_n, va_t, za_t, bias_a, sink_a):
    B, _, S = qa_t.shape
    tqa = A_SUB * TQ
    prev = lambda i: jnp.maximum(i * A_SUB - 1, 0)
    in_specs = [
        pl.BlockSpec((1, A_WIDTH, tqa), lambda b, i: (b, 0, i)),
        pl.BlockSpec((1, TQ, A_KV_WIDTH), lambda b, i: (b, prev(i), 0)),
        pl.BlockSpec((1, tqa, A_KV_WIDTH), lambda b, i: (b, i, 0)),
        pl.BlockSpec((1, A_KV_WIDTH, TQ), lambda b, i: (b, 0, prev(i))),
        pl.BlockSpec((1, A_KV_WIDTH, tqa), lambda b, i: (b, 0, i)),
        pl.BlockSpec((1, A_WIDTH, tqa), lambda b, i: (b, 0, i)),
        pl.BlockSpec((A_Q_HEADS, 2 * TQ, TQ), lambda b, i: (0, 0, 0)),
        pl.BlockSpec((A_Q_HEADS, TQ), lambda b, i: (0, 0)),
    ]
    return pl.pallas_call(
        _mix_a_kernel, grid=(B, S // tqa), in_specs=in_specs,
        out_specs=pl.BlockSpec((1, A_WIDTH, tqa), lambda b, i: (b, 0, i)),
        out_shape=jax.ShapeDtypeStruct((B, A_WIDTH, S), _BF16),
        compiler_params=pltpu.CompilerParams(
            dimension_semantics=("parallel", "parallel"), vmem_limit_bytes=VMEM_LIMIT),
        name="mixer_a",
    )(qa_t, ka_n, ka_n, va_t, va_t, za_t, bias_a, sink_a)


def _key_to_f32(key):
    return pltpu.bitcast(jnp.where(key < 0, key ^ jnp.int32(0x7FFFFFFF), key), _F32)


def _f32_to_key(v):
    bits = pltpu.bitcast(v, jnp.int32)
    return jnp.where(bits < 0, bits ^ jnp.int32(0x7FFFFFFF), bits)


def _colsum8(v):
    return jnp.sum(v.reshape(v.shape[0] // 8, 8, v.shape[1]), axis=0)


def _colmin8(v):
    return jnp.min(v.reshape(v.shape[0] // 8, 8, v.shape[1]), axis=0)


def _colmax8(v):
    return jnp.max(v.reshape(v.shape[0] // 8, 8, v.shape[1]), axis=0)


def _for_chunks(n, step):
    def body(t, carry):
        for u in range(4):
            step(4 * t + u)
        return carry

    lax.fori_loop(0, n // 4, body, 0)
    base = (n // 4) * 4
    for rest in (1, 2, 3):
        @pl.when(n - base == rest)
        def _(rest=rest):
            for u in range(rest):
                step(base + u)


def _mix_b_kernel(qb_ref, kb_ref, vb_ref, zb_ref, qi_ref, ki_ref, wi_ref, bias_ref, o_ref,
                  keys_scr, sc_scr, acc_scr, m_scr, ext_scr, thr_scr, tie_scr):
    i = pl.program_id(1)
    nc = ((i + 1) * TQB + KCHUNK - 1) // KCHUNK
    ns = (nc + 1) // 2
    top_k = TOPK_MAX

    qi = qi_ref[0]
    zpad = jnp.zeros((LANES - IDX_DIM, TQB), _BF16)
    rhs_i = jnp.concatenate(
        [jnp.concatenate([qi[h * IDX_DIM:(h + 1) * IDX_DIM], zpad], axis=0)
         for h in range(IDX_HEADS)], axis=1)
    w = wi_ref[0] * (IDX_DIM ** -0.5 * IDX_HEADS ** -0.5)
    s_minus_t = (lax.broadcasted_iota(jnp.int32, (KCHUNK, TQB), 0)
                 - lax.broadcasted_iota(jnp.int32, (KCHUNK, TQB), 1))

    def index_step(c):
        r0 = pl.multiple_of(c * KCHUNK, KCHUNK)
        d = jnp.dot(ki_ref[0, pl.ds(r0, KCHUNK), :], rhs_i,
                    preferred_element_type=_F32)
        acc = w[0:1, :] * jnp.maximum(d[:, 0:TQB], 0.0)
        for h in range(1, IDX_HEADS):
            acc = acc + w[h:h + 1, :] * jnp.maximum(d[:, h * TQB:(h + 1) * TQB], 0.0)
        causal = s_minus_t <= (i * TQB - r0)
        score = jnp.where(causal, acc, -jnp.inf)
        keys_scr[pl.ds(r0, KCHUNK), :] = score
        ext_scr[0:8, :] = jnp.maximum(ext_scr[0:8, :], _colmax8(score))
        ext_scr[8:16, :] = jnp.minimum(ext_scr[8:16, :], _colmin8(jnp.where(causal, acc, jnp.inf)))
        ext_scr[16:24, :] += _colsum8(jnp.where(score >= 0.0, 1.0, 0.0))
        ext_scr[24:32, :] += _colsum8(jnp.where(score > 0.0, 1.0, 0.0))

    ext_scr[0:8, :] = jnp.full((8, TQB), -jnp.inf, _F32)
    ext_scr[8:16, :] = jnp.full((8, TQB), jnp.inf, _F32)
    ext_scr[16:32, :] = jnp.zeros((16, TQB), _F32)
    _for_chunks(nc, index_step)

    @pl.when(nc % 2 == 1)
    def _():
        keys_scr[pl.ds(pl.multiple_of(nc * KCHUNK, KCHUNK), KCHUNK), :] = jnp.full(
            (KCHUNK, TQB), -jnp.inf, _F32)

    def count_rows(src, hit):
        n_acc = 4

        def bump(accs, r0, n_rows):
            accs = list(accs)
            rows = src[pl.ds(r0, n_rows), :]
            for j in range(n_rows // 8):
                a = accs[j % n_acc]
                accs[j % n_acc] = jnp.where(hit(rows[j * 8:(j + 1) * 8]), a + 1, a)
            return tuple(accs)

        accs = lax.fori_loop(
            0, nc // 2, lambda c, accs: bump(accs, pl.multiple_of(c * KSUPER, KSUPER), KSUPER),
            tuple(jnp.zeros((8, TQB), jnp.int32) for _ in range(n_acc)))
        accs = lax.cond(nc % 2 == 1,
                        lambda accs: bump(accs, pl.multiple_of((nc - 1) * KCHUNK, KCHUNK), KCHUNK),
                        lambda accs: accs, accs)
        acc = accs[0]
        for a in accs[1:]:
            acc = acc + a
        return jnp.sum(acc, axis=0, keepdims=True)

    def count_ge(cand, strict=False):
        return count_rows(keys_scr, (lambda r: r > cand) if strict else (lambda r: r >= cand))

    @pl.when(i * TQB + TQB <= top_k)
    def _():
        thr_scr[...] = jnp.full((1, TQB), -jnp.inf, _F32)
        tie_scr[...] = jnp.full((1, TQB), -1, jnp.int32)

    @pl.when(i * TQB + TQB > top_k)
    def _():
        cnt_nonneg = jnp.sum(ext_scr[16:24, :], axis=0, keepdims=True).astype(jnp.int32)
        cnt_pos = jnp.sum(ext_scr[24:32, :], axis=0, keepdims=True).astype(jnp.int32)
        key_max = _f32_to_key(jnp.max(ext_scr[0:8, :], axis=0, keepdims=True))
        key_min = _f32_to_key(jnp.min(ext_scr[8:16, :], axis=0, keepdims=True))
        n_causal = i * TQB + 1 + lax.broadcasted_iota(jnp.int32, (1, TQB), 1)
        nonneg = cnt_nonneg >= top_k
        lo = jnp.where(nonneg, 0, key_min)
        cnt_lo = jnp.where(nonneg, cnt_nonneg, n_causal)
        hi = jnp.where(nonneg, jnp.where(cnt_pos < top_k, 1, key_max + 1), 0)

        def halve_values(_, state):
            lo_v, hi_v, cnt_lo = state
            mid = lo_v + (hi_v - lo_v) * 0.5
            mid = jnp.where(mid > lo_v, jnp.where(mid < hi_v, mid, lo_v), lo_v)
            cnt = count_ge(mid)
            ok = cnt >= top_k
            return jnp.where(ok, mid, lo_v), jnp.where(ok, hi_v, mid), jnp.where(ok, cnt, cnt_lo)

        def halve_keys(_, state):
            lo, hi, cnt_lo = state
            mid = lo + lax.shift_right_logical(hi - lo, 1)
            cnt = count_ge(_key_to_f32(mid))
            ok = cnt >= top_k
            return jnp.where(ok, mid, lo), jnp.where(ok, hi, mid), jnp.where(ok, cnt, cnt_lo)

        def finished(state):
            lo, hi, cnt_lo = state
            done = jnp.where(cnt_lo == top_k, 1, jnp.where(hi - lo <= 1, 1, 0))
            return jnp.min(done) == 1

        lo_v, hi_v, cnt_lo = lax.fori_loop(0, VALUE_PASSES, halve_values,
                                           (_key_to_f32(lo), _key_to_f32(hi), cnt_lo))
        state = (_f32_to_key(lo_v), _f32_to_key(hi_v), cnt_lo)
        stops = (0,) + SEARCH_CHECKS + (31,)
        state = lax.fori_loop(stops[0], stops[1], halve_keys, state)
        for first, last in zip(stops[1:-1], stops[2:]):
            state = lax.cond(finished(state), lambda s: s,
                             functools.partial(lax.fori_loop, first, last, halve_keys), state)
        lo, _, cnt_ge = state
        thr = _key_to_f32(lo)
        thr_scr[...] = thr
        tie_scr[...] = jnp.full((1, TQB), 2 ** 30, jnp.int32)

        @pl.when(jnp.max(cnt_ge) > top_k)
        def _():
            need = (top_k - count_ge(thr, strict=True)).astype(_F32)
            lower = (lax.broadcasted_iota(jnp.int32, (KCHUNK, KCHUNK), 0)
                     >= lax.broadcasted_iota(jnp.int32, (KCHUNK, KCHUNK), 1))
            lower = jnp.where(lower, 1.0, 0.0).astype(_BF16)

            def prefix_body(c, state):
                before, below = state
                r0 = pl.multiple_of(c * KCHUNK, KCHUNK)
                tied = jnp.where(keys_scr[pl.ds(r0, KCHUNK), :] == thr, 1.0, 0.0).astype(_BF16)
                upto = jnp.dot(lower, tied, preferred_element_type=_F32) + before
                below = below + jnp.sum(jnp.where(upto < need, 1, 0).reshape(KCHUNK // 8, 8, TQB), axis=0)
                return upto[KCHUNK - 1:KCHUNK, :], below

            _, below = lax.fori_loop(0, nc, prefix_body,
                                     (jnp.zeros((1, TQB), _F32), jnp.zeros((8, TQB), jnp.int32)))
            tie_scr[...] = jnp.sum(below, axis=0, keepdims=True)

    thr = thr_scr[...]
    tie = tie_scr[...]
    row = lax.broadcasted_iota(jnp.int32, (KCHUNK, TQB), 0)

    zeros = jnp.zeros((HEAD_DIM, TQB), _BF16)
    n_pairs = B_HEADS // 2
    rhs = []
    for p in range(n_pairs):
        q0 = qb_ref[0, (2 * p) * HEAD_DIM:(2 * p + 1) * HEAD_DIM, :]
        q1 = qb_ref[0, (2 * p + 1) * HEAD_DIM:(2 * p + 2) * HEAD_DIM, :]
        rhs.append(jnp.concatenate([jnp.concatenate([q0, zeros], axis=0),
                                    jnp.concatenate([zeros, q1], axis=0)], axis=1))

    def score_step(c):
        r0 = pl.multiple_of(c * KCHUNK, KCHUNK)
        k = keys_scr[pl.ds(r0, KCHUNK), :]
        mask = jnp.where(k == thr, jnp.where(row <= tie - r0, 0.0, NEG), jnp.where(k > thr, 0.0, NEG))
        for p in range(n_pairs):
            s = jnp.dot(kb_ref[0, pl.ds(r0, KCHUNK), p * LANES:(p + 1) * LANES], rhs[p],
                        preferred_element_type=_F32)
            for hh in range(2):
                h = 2 * p + hh
                parts = []
                for half in range(KCHUNK // TQ):
                    jb = c * (KCHUNK // TQ) + half
                    subs = []
                    for sub in range(TQB // TQ):
                        kind = jnp.clip(jb - (i * (TQB // TQ) + sub) + 2, 0, 2)
                        lanes = slice(sub * TQ, (sub + 1) * TQ)
                        subs.append(s[half * TQ:(half + 1) * TQ, hh * TQB + sub * TQ:hh * TQB + (sub + 1) * TQ]
                                    + bias_ref[h, kind] + mask[half * TQ:(half + 1) * TQ, lanes])
                    parts.append(jnp.concatenate(subs, axis=1))
                sh = jnp.concatenate(parts, axis=0)
                sc_scr[pl.ds(r0, KCHUNK), h * TQB:(h + 1) * TQB] = sh
                m_scr[h * 8:(h + 1) * 8, :] = jnp.maximum(m_scr[h * 8:(h + 1) * 8, :], _colmax8(sh))

    m_scr[...] = jnp.full(m_scr.shape, NEG, _F32)
    _for_chunks(nc, score_step)
    m_row = jnp.concatenate(
        [jnp.max(m_scr[h * 8:(h + 1) * 8, :], axis=0, keepdims=True) for h in range(B_HEADS)],
        axis=1)
    acc_scr[...] = jnp.zeros_like(acc_scr)

    def pv_step(c):
        r0 = pl.multiple_of(c * KCHUNK, KCHUNK)
        for p in range(n_pairs):
            cols = slice(2 * p * TQB, (2 * p + 2) * TQB)
            pr = jnp.exp2(sc_scr[pl.ds(r0, KCHUNK), cols] - m_row[:, cols]).astype(_BF16)
            acc_scr[p] += jnp.dot(vb_ref[0, c, p * PAIR_ROWS:(p + 1) * PAIR_ROWS, :], pr,
                                  preferred_element_type=_F32)

    _for_chunks(nc, pv_step)
    for p in range(n_pairs):
        inv = 1.0 / acc_scr[p, 2 * HEAD_DIM:2 * HEAD_DIM + 1, :]
        for hh in range(2):
            rows = slice((2 * p + hh) * HEAD_DIM, (2 * p + hh + 1) * HEAD_DIM)
            o = (acc_scr[p, hh * HEAD_DIM:(hh + 1) * HEAD_DIM, hh * TQB:(hh + 1) * TQB]
                 * inv[:, hh * TQB:(hh + 1) * TQB] * zb_ref[0, rows, :].astype(_F32))
            o_ref[0, rows, :] = o.astype(_BF16)


def _mix_b(qb_t, kb_n, vb_c, zb_t, qi_t, ki_n, wi_t, bias_b):
    B, _, S = qb_t.shape
    nq = S // TQB
    per_q = lambda rows: pl.BlockSpec((1, rows, TQB), lambda b, i: (b, 0, i))
    in_specs = [
        per_q(B_WIDTH),
        pl.BlockSpec((1, S, B_WIDTH), lambda b, i: (b, 0, 0)),
        pl.BlockSpec((1, S // KCHUNK, V_ROWS, KCHUNK), lambda b, i: (b, 0, 0, 0)),
        per_q(B_WIDTH),
        per_q(IDX_WIDTH),
        pl.BlockSpec((1, S, LANES), lambda b, i: (b, 0, 0)),
        per_q(IDX_HEADS),
        pl.BlockSpec((B_HEADS, 3, TQ, TQ), lambda b, i: (0, 0, 0, 0)),
    ]
    scratch = [
        pltpu.VMEM((S, TQB), _F32),
        pltpu.VMEM((S, B_HEADS * TQB), _F32),
        pltpu.VMEM((B_HEADS // 2, PAIR_ROWS, 2 * TQB), _F32),
        pltpu.VMEM((B_HEADS * 8, TQB), _F32),
        pltpu.VMEM((32, TQB), _F32),
        pltpu.VMEM((1, TQB), _F32),
        pltpu.VMEM((1, TQB), jnp.int32),
    ]
    return pl.pallas_call(
        _mix_b_kernel, grid=(B, nq), in_specs=in_specs,
        out_specs=per_q(B_WIDTH),
        out_shape=jax.ShapeDtypeStruct((B, B_WIDTH, S), _BF16),
        scratch_shapes=scratch,
        compiler_params=pltpu.CompilerParams(
            dimension_semantics=("parallel", "arbitrary"), vmem_limit_bytes=VMEM_LIMIT),
        name="mixer_b",
    )(qb_t, kb_n, vb_c, zb_t, qi_t, ki_n, wi_t, bias_b)


def _merge_kernel(x_ref, ya_ref, yb_ref, gt_ref, wpa_ref, wpb_ref, wo_ref, o_ref):
    pa = jnp.dot(wpa_ref[...], ya_ref[0], preferred_element_type=_F32)
    pb = jnp.dot(wpb_ref[...], yb_ref[0], preferred_element_type=_F32)
    merged = (gt_ref[0, :D_MODEL, :].astype(_F32) * pa
              + gt_ref[0, D_MODEL:, :].astype(_F32) * pb).astype(_BF16)
    out_t = jnp.dot(wo_ref[...], merged, preferred_element_type=_F32)
    o_ref[0] = x_ref[0] + out_t.T


def _merge(x, ya_t, yb_t, gt_t, wpa_t, wpb_t, wo_t):
    B, S, D = x.shape
    tm = TM_PROJ
    const = lambda b, t: (0, 0)
    in_specs = [
        pl.BlockSpec((1, tm, D), lambda b, t: (b, t, 0)),
        pl.BlockSpec((1, A_WIDTH, tm), lambda b, t: (b, 0, t)),
        pl.BlockSpec((1, B_WIDTH, tm), lambda b, t: (b, 0, t)),
        pl.BlockSpec((1, 2 * D, tm), lambda b, t: (b, 0, t)),
        pl.BlockSpec((D, A_WIDTH), const),
        pl.BlockSpec((D, B_WIDTH), const),
        pl.BlockSpec((D, D), const),
    ]
    return pl.pallas_call(
        _merge_kernel, grid=(B, S // tm), in_specs=in_specs,
        out_specs=pl.BlockSpec((1, tm, D), lambda b, t: (b, t, 0)),
        out_shape=jax.ShapeDtypeStruct((B, S, D), x.dtype),
        compiler_params=pltpu.CompilerParams(
            dimension_semantics=("parallel", "parallel"), vmem_limit_bytes=VMEM_LIMIT),
        name="merge_out",
    )(x, ya_t, yb_t, gt_t, wpa_t, wpb_t, wo_t)


def _t5_bucket(n):
    n = np.maximum(n, 0)
    max_exact = N_BUCKETS // 2
    nf = np.maximum(n, 1).astype(np.float64)
    large = max_exact + np.floor(np.log(nf / max_exact) / math.log(MAX_DISTANCE / max_exact)
                                 * (N_BUCKETS - max_exact)).astype(np.int64)
    large = np.minimum(large, N_BUCKETS - 1)
    return np.where(n < max_exact, n, large)


def _bias_tables(rel_bias):
    t = np.arange(TQ)[None, :]
    s = np.arange(TQ)[:, None]
    d_prev = t + TQ - s
    d_cur = t - s
    table_a = rel_bias[:, :A_Q_HEADS].astype(_F32).T * LOG2E
    table_b = rel_bias[:, A_Q_HEADS:].astype(_F32).T * LOG2E

    def tile(table, dist, windowed):
        onehot = (_t5_bucket(dist)[..., None] == np.arange(N_BUCKETS)).astype(np.float32)
        b = jnp.einsum("stk,hk->hst", onehot, table, precision=lax.Precision.HIGHEST)
        if windowed:
            b = jnp.where(((dist >= 0) & (dist < WINDOW))[None], b, NEG)
        return b

    bias_a = jnp.concatenate([tile(table_a, d_prev, True), tile(table_a, d_cur, True)], axis=1)
    far = jnp.broadcast_to(table_b[:, N_BUCKETS - 1][:, None, None], (B_HEADS, TQ, TQ))
    bias_b = jnp.stack([far, tile(table_b, d_prev, False), tile(table_b, d_cur, False)], axis=1)
    return bias_a, bias_b


def kernel(x, norm_g, w_in, qnorm_a, knorm_a, sinks_a, qnorm_b, knorm_b, rel_bias,
           w_proj_a, w_proj_b, w_out):
    assert norm_g.shape[0] == 1, "single-layer block"
    B, S, D = x.shape
    assert D == D_MODEL and S % TM_PROJ == 0 and S % KSUPER == 0 and S % TQB == 0 and TQB % TQ == 0
    assert MAX_DISTANCE <= TQ

    wt = w_in[0].T
    n_real = _SEG["kw"][0] + KW_REAL
    wt = jnp.concatenate(
        [wt[:n_real], jnp.zeros((_SEG["kw"][1] - n_real, D), wt.dtype), wt[n_real:]], axis=0)
    wt = wt.astype(_BF16)
    bcast = lambda g, scale: jnp.broadcast_to((g.astype(_F32) * scale)[:, None], (HEAD_DIM, TM_PROJ))
    q_scale = HEAD_DIM ** -0.5
    (qa_t, ka_n, va_t, za_t, qb_t, kb_n, vb_c, zb_t, qi_t, ki_n, wi_t, gt_t) = _inproj(
        x, norm_g[0][None, :].astype(_F32), wt,
        bcast(qnorm_a[0], q_scale * LOG2E), bcast(knorm_a[0], 1.0),
        bcast(qnorm_b[0], q_scale * LOG2E), bcast(knorm_b[0], 1.0))

    bias_a, bias_b = _bias_tables(rel_bias)
    sink_a = jnp.broadcast_to((sinks_a[0].astype(_F32) * LOG2E)[:, None], (A_Q_HEADS, TQ))
    ya_t = _mix_a(qa_t, ka_n, va_t, za_t, bias_a, sink_a)
    yb_t = _mix_b(qb_t, kb_n, vb_c, zb_t, qi_t, ki_n, wi_t, bias_b)
    return _merge(x, ya_t, yb_t, gt_t,
                  w_proj_a[0].T.astype(_BF16), w_proj_b[0].T.astype(_BF16), w_out[0].T.astype(_BF16))
```

```python
import functools
import math

import jax
import jax.numpy as jnp
import numpy as np
from jax import lax
from jax.experimental import pallas as pl
from jax.experimental.pallas import tpu as pltpu

D_MODEL = 1024
HEAD_DIM = 64
A_Q_HEADS = 8
A_KV_HEADS = 2
A_GROUPS = A_Q_HEADS // A_KV_HEADS
A_WIDTH = A_Q_HEADS * HEAD_DIM
A_KV_WIDTH = A_KV_HEADS * HEAD_DIM
B_HEADS = 8
B_WIDTH = B_HEADS * HEAD_DIM
IDX_HEADS = 8
IDX_DIM = 32
IDX_WIDTH = IDX_HEADS * IDX_DIM
WINDOW = 128
TOPK_MAX = 256
N_BUCKETS = 32
MAX_DISTANCE = 128
RMS_EPS = 1e-6

LANES = 128
TQ = 128
A_SUB = 4
TQB = 256
KCHUNK = 256
KSUPER = 2 * KCHUNK
assert KCHUNK == TQB
VALUE_PASSES = 16
SEARCH_CHECKS = (1, 3, 5, 8)
TM_PROJ = 1024
NEG = -1e30
LOG2E = math.log2(math.e)
ONES_ROWS = 16
PAIR_ROWS = 2 * HEAD_DIM + ONES_ROWS
V_ROWS = (B_HEADS // 2) * PAIR_ROWS
VMEM_LIMIT = 56 * 1024 * 1024

_SEG = {}
_off = 0
for _name, _rows in (("qa", A_WIDTH), ("ka", A_KV_WIDTH), ("va", A_KV_WIDTH), ("za", A_WIDTH),
                     ("qb", B_WIDTH), ("kb", B_WIDTH), ("vb", B_WIDTH), ("zb", B_WIDTH),
                     ("qi", IDX_WIDTH), ("kw", LANES), ("gates", 2 * D_MODEL)):
    _SEG[_name] = (_off, _off + _rows)
    _off += _rows
PROJ_ROWS = _off
KW_REAL = IDX_DIM + IDX_HEADS

_F32 = jnp.float32
_BF16 = jnp.bfloat16
_NT = (((1,), (1,)), ((), ()))


def _inproj_kernel(x_ref, g_ref, wt_ref, gqa_ref, gka_ref, gqb_ref, gkb_ref,
                   qa_o, ka_o, va_o, za_o, qb_o, kb_o, vb_o, zb_o, qi_o, ki_o, wi_o, gt_o):
    tm = x_ref.shape[1]
    x = x_ref[0]
    ms = jnp.mean(x * x, axis=-1, keepdims=True)
    h = (x * lax.rsqrt(ms + RMS_EPS) * g_ref[...]).astype(_BF16)

    def proj(lo, hi):
        return lax.dot_general(wt_ref[lo:hi, :], h, _NT, preferred_element_type=_F32)

    def seg(name):
        return proj(*_SEG[name])

    def headnorm(p, gain_ref):
        nh = p.shape[0] // HEAD_DIM
        p3 = p.reshape(nh, HEAD_DIM, tm)
        r = lax.rsqrt(jnp.mean(p3 * p3, axis=1, keepdims=True) + RMS_EPS)
        return (p3 * r * gain_ref[...][None]).reshape(nh * HEAD_DIM, tm)

    def silu(p):
        return p * jax.nn.sigmoid(p)

    qa_o[0] = headnorm(seg("qa"), gqa_ref).astype(_BF16)
    ka_o[0] = headnorm(seg("ka"), gka_ref).T.astype(_BF16)
    va_o[0] = seg("va").astype(_BF16)
    za_o[0] = silu(seg("za")).astype(_BF16)
    qb_o[0] = headnorm(seg("qb"), gqb_ref).astype(_BF16)
    kb_o[0] = headnorm(seg("kb"), gkb_ref).T.astype(_BF16)
    vb = seg("vb").astype(_BF16)
    ones = jnp.ones((ONES_ROWS, KCHUNK), _BF16)
    for c in range(tm // KCHUNK):
        for p in range(B_HEADS // 2):
            vb_o[0, c, p * PAIR_ROWS:(p + 1) * PAIR_ROWS, :] = jnp.concatenate(
                [vb[p * LANES:(p + 1) * LANES, c * KCHUNK:(c + 1) * KCHUNK], ones], axis=0)
    zb_o[0] = silu(seg("zb")).astype(_BF16)
    qi_o[0] = seg("qi").astype(_BF16)
    kw = seg("kw")
    wi_o[0] = kw[IDX_DIM:KW_REAL, :]
    row = lax.broadcasted_iota(jnp.int32, kw.shape, 0)
    ki_o[0] = jnp.where(row < IDX_DIM, kw, 0.0).T.astype(_BF16)
    g_lo = _SEG["gates"][0]
    for c in range(2 * D_MODEL // 512):
        gt_o[0, c * 512:(c + 1) * 512, :] = jax.nn.sigmoid(
            proj(g_lo + c * 512, g_lo + (c + 1) * 512)).astype(_BF16)


def _inproj(x, g, wt, gqa, gka, gqb, gkb):
    B, S, D = x.shape
    tm = TM_PROJ
    nt = S // tm
    const = lambda b, t: (0, 0)
    tr = lambda rows: pl.BlockSpec((1, rows, tm), lambda b, t: (b, 0, t))
    nat = lambda cols: pl.BlockSpec((1, tm, cols), lambda b, t: (b, t, 0))
    sds = jax.ShapeDtypeStruct
    out_shape = (
        sds((B, A_WIDTH, S), _BF16), sds((B, S, A_KV_WIDTH), _BF16), sds((B, A_KV_WIDTH, S), _BF16),
        sds((B, A_WIDTH, S), _BF16),
        sds((B, B_WIDTH, S), _BF16), sds((B, S, B_WIDTH), _BF16),
        sds((B, S // KCHUNK, V_ROWS, KCHUNK), _BF16), sds((B, B_WIDTH, S), _BF16),
        sds((B, IDX_WIDTH, S), _BF16), sds((B, S, LANES), _BF16), sds((B, IDX_HEADS, S), _F32),
        sds((B, 2 * D_MODEL, S), _BF16),
    )
    out_specs = (
        tr(A_WIDTH), nat(A_KV_WIDTH), tr(A_KV_WIDTH), tr(A_WIDTH),
        tr(B_WIDTH), nat(B_WIDTH),
        pl.BlockSpec((1, tm // KCHUNK, V_ROWS, KCHUNK), lambda b, t: (b, t, 0, 0)), tr(B_WIDTH),
        tr(IDX_WIDTH), nat(LANES), tr(IDX_HEADS), tr(2 * D_MODEL),
    )
    in_specs = [
        pl.BlockSpec((1, tm, D), lambda b, t: (b, t, 0)),
        pl.BlockSpec((1, D), const),
        pl.BlockSpec((PROJ_ROWS, D), const, pipeline_mode=pl.Buffered(1)),
        pl.BlockSpec((HEAD_DIM, tm), const), pl.BlockSpec((HEAD_DIM, tm), const),
        pl.BlockSpec((HEAD_DIM, tm), const), pl.BlockSpec((HEAD_DIM, tm), const),
    ]
    return pl.pallas_call(
        _inproj_kernel, grid=(B, nt), in_specs=in_specs, out_specs=out_specs, out_shape=out_shape,
        compiler_params=pltpu.CompilerParams(
            dimension_semantics=("parallel", "parallel"), vmem_limit_bytes=VMEM_LIMIT),
        name="inproj",
    )(x, g, wt, gqa, gka, gqb, gkb)


def _mix_a_kernel(q_ref, kp_ref, kc_ref, vp_ref, vc_ref, z_ref, bias_ref, sink_ref, o_ref):
    i = pl.program_id(1)
    zeros = jnp.zeros((HEAD_DIM, TQ), _BF16)
    units = [(j, g) for j in range(A_SUB) for g in range(A_KV_HEADS)]

    def bands(j):
        cur = slice(j * TQ, (j + 1) * TQ)
        old = slice((j - 1) * TQ, j * TQ)
        k_old = kp_ref[0] if j == 0 else kc_ref[0, old, :]
        v_old = vp_ref[0] if j == 0 else vc_ref[0, :, old]
        return (jnp.concatenate([k_old, kc_ref[0, cur, :]], axis=0),
                jnp.concatenate([v_old, vc_ref[0, :, cur]], axis=1))

    scores = []
    for j, g in units:
        cur = slice(j * TQ, (j + 1) * TQ)
        cols = []
        for hh in range(A_GROUPS):
            h = g * A_GROUPS + hh
            parts = [zeros] * A_KV_HEADS
            parts[g] = q_ref[0, h * HEAD_DIM:(h + 1) * HEAD_DIM, cur]
            cols.append(jnp.concatenate(parts, axis=0))
        rhs = jnp.concatenate(cols, axis=1)
        scores.append(jnp.dot(bands(j)[0], rhs, preferred_element_type=_F32))

    pts, invs = [], []
    for (j, g), sc in zip(units, scores):
        probs, inv = [], []
        for hh in range(A_GROUPS):
            h = g * A_GROUPS + hh
            s = sc[:, hh * TQ:(hh + 1) * TQ] + bias_ref[h]
            if j == 0:
                s = jnp.concatenate([jnp.where(i > 0, s[:TQ], NEG), s[TQ:]], axis=0)
            sink = sink_ref[h:h + 1, :]
            m = jnp.maximum(jnp.max(s, axis=0, keepdims=True), sink)
            p = jnp.exp2(s - m)
            denom = jnp.sum(p, axis=0, keepdims=True) + jnp.exp2(sink - m)
            probs.append(p.astype(_BF16))
            inv.append(1.0 / denom)
        pts.append(jnp.concatenate(probs, axis=1))
        invs.append(inv)

    for (j, g), pt, inv in zip(units, pts, invs):
        cur = slice(j * TQ, (j + 1) * TQ)
        out = jnp.dot(bands(j)[1][g * HEAD_DIM:(g + 1) * HEAD_DIM, :], pt,
                      preferred_element_type=_F32)
        for hh in range(A_GROUPS):
            h = g * A_GROUPS + hh
            rows = slice(h * HEAD_DIM, (h + 1) * HEAD_DIM)
            o = out[:, hh * TQ:(hh + 1) * TQ] * inv[hh] * z_ref[0, rows, cur].astype(_F32)
            o_ref[0, rows, cur] = o.astype(_BF16)


def _mix_a(qa_t, ka_n, va_t, za_t, bias_a, sink_a):
    B, _, S = qa_t.shape
    tqa = A_SUB * TQ
    prev = lambda i: jnp.maximum(i * A_SUB - 1, 0)
    in_specs = [
        pl.BlockSpec((1, A_WIDTH, tqa), lambda b, i: (b, 0, i)),
        pl.BlockSpec((1, TQ, A_KV_WIDTH), lambda b, i: (b, prev(i), 0)),
        pl.BlockSpec((1, tqa, A_KV_WIDTH), lambda b, i: (b, i, 0)),
        pl.BlockSpec((1, A_KV_WIDTH, TQ), lambda b, i: (b, 0, prev(i))),
        pl.BlockSpec((1, A_KV_WIDTH, tqa), lambda b, i: (b, 0, i)),
        pl.BlockSpec((1, A_WIDTH, tqa), lambda b, i: (b, 0, i)),
        pl.BlockSpec((A_Q_HEADS, 2 * TQ, TQ), lambda b, i: (0, 0, 0)),
        pl.BlockSpec((A_Q_HEADS, TQ), lambda b, i: (0, 0)),
    ]
    return pl.pallas_call(
        _mix_a_kernel, grid=(B, S // tqa), in_specs=in_specs,
        out_specs=pl.BlockSpec((1, A_WIDTH, tqa), lambda b, i: (b, 0, i)),
        out_shape=jax.ShapeDtypeStruct((B, A_WIDTH, S), _BF16),
        compiler_params=pltpu.CompilerParams(
            dimension_semantics=("parallel", "parallel"), vmem_limit_bytes=VMEM_LIMIT),
        name="mixer_a",
    )(qa_t, ka_n, ka_n, va_t, va_t, za_t, bias_a, sink_a)


def _key_to_f32(key):
    return pltpu.bitcast(jnp.where(key < 0, key ^ jnp.int32(0x7FFFFFFF), key), _F32)


def _f32_to_key(v):
    bits = pltpu.bitcast(v, jnp.int32)
    return jnp.where(bits < 0, bits ^ jnp.int32(0x7FFFFFFF), bits)


def _colsum8(v):
    return jnp.sum(v.reshape(v.shape[0] // 8, 8, v.shape[1]), axis=0)


def _colmin8(v):
    return jnp.min(v.reshape(v.shape[0] // 8, 8, v.shape[1]), axis=0)


def _colmax8(v):
    return jnp.max(v.reshape(v.shape[0] // 8, 8, v.shape[1]), axis=0)


def _for_chunks(n, step):
    def body(t, carry):
        for u in range(4):
            step(4 * t + u)
        return carry

    lax.fori_loop(0, n // 4, body, 0)
    base = (n // 4) * 4
    for rest in (1, 2, 3):
        @pl.when(n - base == rest)
        def _(rest=rest):
            for u in range(rest):
                step(base + u)


def _mix_b_kernel(qb_ref, kb_ref, vb_ref, zb_ref, qi_ref, ki_ref, wi_ref, bias_ref, o_ref,
                  keys_scr, sc_scr, acc_scr, m_scr, ext_scr, thr_scr, tie_scr):
    i = pl.program_id(1)
    nc = ((i + 1) * TQB + KCHUNK - 1) // KCHUNK
    ns = (nc + 1) // 2
    top_k = TOPK_MAX

    qi = qi_ref[0]
    zpad = jnp.zeros((LANES - IDX_DIM, TQB), _BF16)
    rhs_i = jnp.concatenate(
        [jnp.concatenate([qi[h * IDX_DIM:(h + 1) * IDX_DIM], zpad], axis=0)
         for h in range(IDX_HEADS)], axis=1)
    w = wi_ref[0] * (IDX_DIM ** -0.5 * IDX_HEADS ** -0.5)
    s_minus_t = (lax.broadcasted_iota(jnp.int32, (KCHUNK, TQB), 0)
                 - lax.broadcasted_iota(jnp.int32, (KCHUNK, TQB), 1))

    def index_step(c, diagonal=False):
        r0 = pl.multiple_of(c * KCHUNK, KCHUNK)
        d = jnp.dot(ki_ref[0, pl.ds(r0, KCHUNK), :], rhs_i,
                    preferred_element_type=_F32)
        acc = w[0:1, :] * jnp.maximum(d[:, 0:TQB], 0.0)
        for h in range(1, IDX_HEADS):
            acc = acc + w[h:h + 1, :] * jnp.maximum(d[:, h * TQB:(h + 1) * TQB], 0.0)
        if diagonal:
            causal = s_minus_t <= (i * TQB - r0)
            score = jnp.where(causal, acc, -jnp.inf)
            lowest = jnp.where(causal, acc, jnp.inf)
        else:
            score = lowest = acc
        keys_scr[pl.ds(r0, KCHUNK), :] = score
        ext_scr[0:8, :] = jnp.maximum(ext_scr[0:8, :], _colmax8(score))
        ext_scr[8:16, :] = jnp.minimum(ext_scr[8:16, :], _colmin8(lowest))
        ext_scr[16:24, :] += _colsum8(jnp.where(score >= 0.0, 1.0, 0.0))
        ext_scr[24:32, :] += _colsum8(jnp.where(score > 0.0, 1.0, 0.0))

    ext_scr[0:8, :] = jnp.full((8, TQB), -jnp.inf, _F32)
    ext_scr[8:16, :] = jnp.full((8, TQB), jnp.inf, _F32)
    ext_scr[16:32, :] = jnp.zeros((16, TQB), _F32)
    _for_chunks(nc - 1, index_step)
    index_step(nc - 1, diagonal=True)

    @pl.when(nc % 2 == 1)
    def _():
        keys_scr[pl.ds(pl.multiple_of(nc * KCHUNK, KCHUNK), KCHUNK), :] = jnp.full(
            (KCHUNK, TQB), -jnp.inf, _F32)

    def count_rows(src, hit):
        n_acc = 4

        def bump(accs, r0, n_rows):
            accs = list(accs)
            rows = src[pl.ds(r0, n_rows), :]
            for j in range(n_rows // 8):
                a = accs[j % n_acc]
                accs[j % n_acc] = jnp.where(hit(rows[j * 8:(j + 1) * 8]), a + 1, a)
            return tuple(accs)

        accs = lax.fori_loop(
            0, nc // 2, lambda c, accs: bump(accs, pl.multiple_of(c * KSUPER, KSUPER), KSUPER),
            tuple(jnp.zeros((8, TQB), jnp.int32) for _ in range(n_acc)))
        accs = lax.cond(nc % 2 == 1,
                        lambda accs: bump(accs, pl.multiple_of((nc - 1) * KCHUNK, KCHUNK), KCHUNK),
                        lambda accs: accs, accs)
        acc = accs[0]
        for a in accs[1:]:
            acc = acc + a
        return jnp.sum(acc, axis=0, keepdims=True)

    def count_ge(cand, strict=False):
        return count_rows(keys_scr, (lambda r: r > cand) if strict else (lambda r: r >= cand))

    @pl.when(i * TQB + TQB <= top_k)
    def _():
        thr_scr[...] = jnp.full((1, TQB), -jnp.inf, _F32)
        tie_scr[...] = jnp.full((1, TQB), -1, jnp.int32)

    @pl.when(i * TQB + TQB > top_k)
    def _():
        cnt_nonneg = jnp.sum(ext_scr[16:24, :], axis=0, keepdims=True).astype(jnp.int32)
        cnt_pos = jnp.sum(ext_scr[24:32, :], axis=0, keepdims=True).astype(jnp.int32)
        key_max = _f32_to_key(jnp.max(ext_scr[0:8, :], axis=0, keepdims=True))
        key_min = _f32_to_key(jnp.min(ext_scr[8:16, :], axis=0, keepdims=True))
        n_causal = i * TQB + 1 + lax.broadcasted_iota(jnp.int32, (1, TQB), 1)
        nonneg = cnt_nonneg >= top_k
        lo = jnp.where(nonneg, 0, key_min)
        cnt_lo = jnp.where(nonneg, cnt_nonneg, n_causal)
        hi = jnp.where(nonneg, jnp.where(cnt_pos < top_k, 1, key_max + 1), 0)

        def halve_values(_, state):
            lo_v, hi_v, cnt_lo = state
            mid = lo_v + (hi_v - lo_v) * 0.5
            mid = jnp.where(mid > lo_v, jnp.where(mid < hi_v, mid, lo_v), lo_v)
            cnt = count_ge(mid)
            ok = cnt >= top_k
            return jnp.where(ok, mid, lo_v), jnp.where(ok, hi_v, mid), jnp.where(ok, cnt, cnt_lo)

        def halve_keys(_, state):
            lo, hi, cnt_lo = state
            mid = lo + lax.shift_right_logical(hi - lo, 1)
            cnt = count_ge(_key_to_f32(mid))
            ok = cnt >= top_k
            return jnp.where(ok, mid, lo), jnp.where(ok, hi, mid), jnp.where(ok, cnt, cnt_lo)

        def finished(state):
            lo, hi, cnt_lo = state
            done = jnp.where(cnt_lo == top_k, 1, jnp.where(hi - lo <= 1, 1, 0))
            return jnp.min(done) == 1

        lo_v, hi_v, cnt_lo = lax.fori_loop(0, VALUE_PASSES, halve_values,
                                           (_key_to_f32(lo), _key_to_f32(hi), cnt_lo))
        state = (_f32_to_key(lo_v), _f32_to_key(hi_v), cnt_lo)
        stops = (0,) + SEARCH_CHECKS + (31,)
        state = lax.fori_loop(stops[0], stops[1], halve_keys, state)
        for first, last in zip(stops[1:-1], stops[2:]):
            state = lax.cond(finished(state), lambda s: s,
                             functools.partial(lax.fori_loop, first, last, halve_keys), state)
        lo, _, cnt_ge = state
        thr = _key_to_f32(lo)
        thr_scr[...] = thr
        tie_scr[...] = jnp.full((1, TQB), 2 ** 30, jnp.int32)

        @pl.when(jnp.max(cnt_ge) > top_k)
        def _():
            need = (top_k - count_ge(thr, strict=True)).astype(_F32)
            lower = (lax.broadcasted_iota(jnp.int32, (KCHUNK, KCHUNK), 0)
                     >= lax.broadcasted_iota(jnp.int32, (KCHUNK, KCHUNK), 1))
            lower = jnp.where(lower, 1.0, 0.0).astype(_BF16)

            def prefix_body(c, state):
                before, below = state
                for half in range(KSUPER // KCHUNK):
                    r0 = pl.multiple_of(c * KSUPER + half * KCHUNK, KCHUNK)
                    tied = jnp.where(keys_scr[pl.ds(r0, KCHUNK), :] == thr, 1.0, 0.0).astype(_BF16)
                    upto = jnp.dot(lower, tied, preferred_element_type=_F32) + before
                    below = below + jnp.sum(jnp.where(upto < need, 1, 0).reshape(KCHUNK // 8, 8, TQB), axis=0)
                    before = upto[KCHUNK - 1:KCHUNK, :]
                return before, below

            _, below = lax.fori_loop(0, ns, prefix_body,
                                     (jnp.zeros((1, TQB), _F32), jnp.zeros((8, TQB), jnp.int32)))
            tie_scr[...] = jnp.sum(below, axis=0, keepdims=True)

    thr = thr_scr[...]
    tie = tie_scr[...]
    row = lax.broadcasted_iota(jnp.int32, (KCHUNK, TQB), 0)

    zeros = jnp.zeros((HEAD_DIM, TQB), _BF16)
    n_pairs = B_HEADS // 2
    rhs = []
    for p in range(n_pairs):
        q0 = qb_ref[0, (2 * p) * HEAD_DIM:(2 * p + 1) * HEAD_DIM, :]
        q1 = qb_ref[0, (2 * p + 1) * HEAD_DIM:(2 * p + 2) * HEAD_DIM, :]
        rhs.append(jnp.concatenate([jnp.concatenate([q0, zeros], axis=0),
                                    jnp.concatenate([zeros, q1], axis=0)], axis=1))

    def score_step(c):
        r0 = pl.multiple_of(c * KCHUNK, KCHUNK)
        k = keys_scr[pl.ds(r0, KCHUNK), :]
        mask = jnp.where(k == thr, jnp.where(row <= tie - r0, 0.0, NEG), jnp.where(k > thr, 0.0, NEG))
        for p in range(n_pairs):
            s = jnp.dot(kb_ref[0, pl.ds(r0, KCHUNK), p * LANES:(p + 1) * LANES], rhs[p],
                        preferred_element_type=_F32)
            for hh in range(2):
                h = 2 * p + hh
                parts = []
                for half in range(KCHUNK // TQ):
                    jb = c * (KCHUNK // TQ) + half
                    subs = []
                    for sub in range(TQB // TQ):
                        kind = jnp.clip(jb - (i * (TQB // TQ) + sub) + 2, 0, 2)
                        lanes = slice(sub * TQ, (sub + 1) * TQ)
                        subs.append(s[half * TQ:(half + 1) * TQ, hh * TQB + sub * TQ:hh * TQB + (sub + 1) * TQ]
                                    + bias_ref[h, kind] + mask[half * TQ:(half + 1) * TQ, lanes])
                    parts.append(jnp.concatenate(subs, axis=1))
                sh = jnp.concatenate(parts, axis=0)
                sc_scr[pl.ds(r0, KCHUNK), h * TQB:(h + 1) * TQB] = sh
                m_scr[h * 8:(h + 1) * 8, :] = jnp.maximum(m_scr[h * 8:(h + 1) * 8, :], _colmax8(sh))

    m_scr[...] = jnp.full(m_scr.shape, NEG, _F32)
    _for_chunks(nc, score_step)
    m_row = jnp.concatenate(
        [jnp.max(m_scr[h * 8:(h + 1) * 8, :], axis=0, keepdims=True) for h in range(B_HEADS)],
        axis=1)
    acc_scr[...] = jnp.zeros_like(acc_scr)

    def pv_step(c):
        r0 = pl.multiple_of(c * KCHUNK, KCHUNK)
        for p in range(n_pairs):
            cols = slice(2 * p * TQB, (2 * p + 2) * TQB)
            pr = jnp.exp2(sc_scr[pl.ds(r0, KCHUNK), cols] - m_row[:, cols]).astype(_BF16)
            acc_scr[p] += jnp.dot(vb_ref[0, c, p * PAIR_ROWS:(p + 1) * PAIR_ROWS, :], pr,
                                  preferred_element_type=_F32)

    _for_chunks(nc, pv_step)
    for p in range(n_pairs):
        inv = 1.0 / acc_scr[p, 2 * HEAD_DIM:2 * HEAD_DIM + 1, :]
        for hh in range(2):
            rows = slice((2 * p + hh) * HEAD_DIM, (2 * p + hh + 1) * HEAD_DIM)
            o = (acc_scr[p, hh * HEAD_DIM:(hh + 1) * HEAD_DIM, hh * TQB:(hh + 1) * TQB]
                 * inv[:, hh * TQB:(hh + 1) * TQB] * zb_ref[0, rows, :].astype(_F32))
            o_ref[0, rows, :] = o.astype(_BF16)


def _mix_b(qb_t, kb_n, vb_c, zb_t, qi_t, ki_n, wi_t, bias_b):
    B, _, S = qb_t.shape
    nq = S // TQB
    per_q = lambda rows: pl.BlockSpec((1, rows, TQB), lambda b, i: (b, 0, i))
    in_specs = [
        per_q(B_WIDTH),
        pl.BlockSpec((1, S, B_WIDTH), lambda b, i: (b, 0, 0)),
        pl.BlockSpec((1, S // KCHUNK, V_ROWS, KCHUNK), lambda b, i: (b, 0, 0, 0)),
        per_q(B_WIDTH),
        per_q(IDX_WIDTH),
        pl.BlockSpec((1, S, LANES), lambda b, i: (b, 0, 0)),
        per_q(IDX_HEADS),
        pl.BlockSpec((B_HEADS, 3, TQ, TQ), lambda b, i: (0, 0, 0, 0)),
    ]
    scratch = [
        pltpu.VMEM((S, TQB), _F32),
        pltpu.VMEM((S, B_HEADS * TQB), _F32),
        pltpu.VMEM((B_HEADS // 2, PAIR_ROWS, 2 * TQB), _F32),
        pltpu.VMEM((B_HEADS * 8, TQB), _F32),
        pltpu.VMEM((32, TQB), _F32),
        pltpu.VMEM((1, TQB), _F32),
        pltpu.VMEM((1, TQB), jnp.int32),
    ]
    return pl.pallas_call(
        _mix_b_kernel, grid=(B, nq), in_specs=in_specs,
        out_specs=per_q(B_WIDTH),
        out_shape=jax.ShapeDtypeStruct((B, B_WIDTH, S), _BF16),
        scratch_shapes=scratch,
        compiler_params=pltpu.CompilerParams(
            dimension_semantics=("parallel", "arbitrary"), vmem_limit_bytes=VMEM_LIMIT),
        name="mixer_b",
    )(qb_t, kb_n, vb_c, zb_t, qi_t, ki_n, wi_t, bias_b)


def _merge_kernel(x_ref, ya_ref, yb_ref, gt_ref, wpa_ref, wpb_ref, wo_ref, o_ref):
    pa = jnp.dot(wpa_ref[...], ya_ref[0], preferred_element_type=_F32)
    pb = jnp.dot(wpb_ref[...], yb_ref[0], preferred_element_type=_F32)
    merged = (gt_ref[0, :D_MODEL, :].astype(_F32) * pa
              + gt_ref[0, D_MODEL:, :].astype(_F32) * pb).astype(_BF16)
    out_t = jnp.dot(wo_ref[...], merged, preferred_element_type=_F32)
    o_ref[0] = x_ref[0] + out_t.T


def _merge(x, ya_t, yb_t, gt_t, wpa_t, wpb_t, wo_t):
    B, S, D = x.shape
    tm = TM_PROJ
    const = lambda b, t: (0, 0)
    in_specs = [
        pl.BlockSpec((1, tm, D), lambda b, t: (b, t, 0)),
        pl.BlockSpec((1, A_WIDTH, tm), lambda b, t: (b, 0, t)),
        pl.BlockSpec((1, B_WIDTH, tm), lambda b, t: (b, 0, t)),
        pl.BlockSpec((1, 2 * D, tm), lambda b, t: (b, 0, t)),
        pl.BlockSpec((D, A_WIDTH), const),
        pl.BlockSpec((D, B_WIDTH), const),
        pl.BlockSpec((D, D), const),
    ]
    return pl.pallas_call(
        _merge_kernel, grid=(B, S // tm), in_specs=in_specs,
        out_specs=pl.BlockSpec((1, tm, D), lambda b, t: (b, t, 0)),
        out_shape=jax.ShapeDtypeStruct((B, S, D), x.dtype),
        compiler_params=pltpu.CompilerParams(
            dimension_semantics=("parallel", "parallel"), vmem_limit_bytes=VMEM_LIMIT),
        name="merge_out",
    )(x, ya_t, yb_t, gt_t, wpa_t, wpb_t, wo_t)


def _t5_bucket(n):
    n = np.maximum(n, 0)
    max_exact = N_BUCKETS // 2
    nf = np.maximum(n, 1).astype(np.float64)
    large = max_exact + np.floor(np.log(nf / max_exact) / math.log(MAX_DISTANCE / max_exact)
                                 * (N_BUCKETS - max_exact)).astype(np.int64)
    large = np.minimum(large, N_BUCKETS - 1)
    return np.where(n < max_exact, n, large)


def _bias_tables(rel_bias):
    t = np.arange(TQ)[None, :]
    s = np.arange(TQ)[:, None]
    d_prev = t + TQ - s
    d_cur = t - s
    table_a = rel_bias[:, :A_Q_HEADS].astype(_F32).T * LOG2E
    table_b = rel_bias[:, A_Q_HEADS:].astype(_F32).T * LOG2E

    def tile(table, dist, windowed):
        onehot = (_t5_bucket(dist)[..., None] == np.arange(N_BUCKETS)).astype(np.float32)
        b = jnp.einsum("stk,hk->hst", onehot, table, precision=lax.Precision.HIGHEST)
        if windowed:
            b = jnp.where(((dist >= 0) & (dist < WINDOW))[None], b, NEG)
        return b

    bias_a = jnp.concatenate([tile(table_a, d_prev, True), tile(table_a, d_cur, True)], axis=1)
    far = jnp.broadcast_to(table_b[:, N_BUCKETS - 1][:, None, None], (B_HEADS, TQ, TQ))
    bias_b = jnp.stack([far, tile(table_b, d_prev, False), tile(table_b, d_cur, False)], axis=1)
    return bias_a, bias_b


def kernel(x, norm_g, w_in, qnorm_a, knorm_a, sinks_a, qnorm_b, knorm_b, rel_bias,
           w_proj_a, w_proj_b, w_out):
    assert norm_g.shape[0] == 1, "single-layer block"
    B, S, D = x.shape
    assert D == D_MODEL and S % TM_PROJ == 0 and S % KSUPER == 0 and S % TQB == 0 and TQB % TQ == 0
    assert MAX_DISTANCE <= TQ

    wt = w_in[0].T
    n_real = _SEG["kw"][0] + KW_REAL
    wt = jnp.concatenate(
        [wt[:n_real], jnp.zeros((_SEG["kw"][1] - n_real, D), wt.dtype), wt[n_real:]], axis=0)
    wt = wt.astype(_BF16)
    bcast = lambda g, scale: jnp.broadcast_to((g.astype(_F32) * scale)[:, None], (HEAD_DIM, TM_PROJ))
    q_scale = HEAD_DIM ** -0.5
    (qa_t, ka_n, va_t, za_t, qb_t, kb_n, vb_c, zb_t, qi_t, ki_n, wi_t, gt_t) = _inproj(
        x, norm_g[0][None, :].astype(_F32), wt,
        bcast(qnorm_a[0], q_scale * LOG2E), bcast(knorm_a[0], 1.0),
        bcast(qnorm_b[0], q_scale * LOG2E), bcast(knorm_b[0], 1.0))

    bias_a, bias_b = _bias_tables(rel_bias)
    sink_a = jnp.broadcast_to((sinks_a[0].astype(_F32) * LOG2E)[:, None], (A_Q_HEADS, TQ))
    ya_t = _mix_a(qa_t, ka_n, va_t, za_t, bias_a, sink_a)
    yb_t = _mix_b(qb_t, kb_n, vb_c, zb_t, qi_t, ki_n, wi_t, bias_b)
    return _merge(x, ya_t, yb_t, gt_t,
                  w_proj_a[0].T.astype(_BF16), w_proj_b[0].T.astype(_BF16), w_out[0].T.astype(_BF16))
```

```python
import functools
import math

import jax
import jax.numpy as jnp
import numpy as np
from jax import lax
from jax.experimental import pallas as pl
from jax.experimental.pallas import tpu as pltpu

D_MODEL = 1024
HEAD_DIM = 64
A_Q_HEADS = 8
A_KV_HEADS = 2
A_GROUPS = A_Q_HEADS // A_KV_HEADS
A_WIDTH = A_Q_HEADS * HEAD_DIM
A_KV_WIDTH = A_KV_HEADS * HEAD_DIM
B_HEADS = 8
B_WIDTH = B_HEADS * HEAD_DIM
IDX_HEADS = 8
IDX_DIM = 32
IDX_WIDTH = IDX_HEADS * IDX_DIM
WINDOW = 128
TOPK_MAX = 256
N_BUCKETS = 32
MAX_DISTANCE = 128
RMS_EPS = 1e-6

LANES = 128
TQ = 128
A_SUB = 4
TQB = 256
KCHUNK = 256
KSUPER = 2 * KCHUNK
assert KCHUNK == TQB
VALUE_PASSES = 16
SEARCH_CHECKS = (1, 3, 5, 8)
TM_PROJ = 1024
NEG = -1e30
LOG2E = math.log2(math.e)
ONES_ROWS = 16
PAIR_ROWS = 2 * HEAD_DIM + ONES_ROWS
V_ROWS = (B_HEADS // 2) * PAIR_ROWS
VMEM_LIMIT = 56 * 1024 * 1024

_SEG = {}
_off = 0
for _name, _rows in (("qa", A_WIDTH), ("ka", A_KV_WIDTH), ("va", A_KV_WIDTH), ("za", A_WIDTH),
                     ("qb", B_WIDTH), ("kb", B_WIDTH), ("vb", B_WIDTH), ("zb", B_WIDTH),
                     ("qi", IDX_WIDTH), ("kw", LANES), ("gates", 2 * D_MODEL)):
    _SEG[_name] = (_off, _off + _rows)
    _off += _rows
PROJ_ROWS = _off
KW_REAL = IDX_DIM + IDX_HEADS

_F32 = jnp.float32
_BF16 = jnp.bfloat16
_NT = (((1,), (1,)), ((), ()))


def _inproj_kernel(x_ref, g_ref, wt_ref, gqa_ref, gka_ref, gqb_ref, gkb_ref,
                   qa_o, ka_o, va_o, za_o, qb_o, kb_o, vb_o, zb_o, qi_o, ki_o, wi_o, gt_o):
    tm = x_ref.shape[1]
    x = x_ref[0]
    ms = jnp.mean(x * x, axis=-1, keepdims=True)
    h = (x * lax.rsqrt(ms + RMS_EPS) * g_ref[...]).astype(_BF16)

    def proj(lo, hi):
        return lax.dot_general(wt_ref[lo:hi, :], h, _NT, preferred_element_type=_F32)

    def seg(name):
        return proj(*_SEG[name])

    def headnorm(p, gain_ref):
        nh = p.shape[0] // HEAD_DIM
        p3 = p.reshape(nh, HEAD_DIM, tm)
        r = lax.rsqrt(jnp.mean(p3 * p3, axis=1, keepdims=True) + RMS_EPS)
        return (p3 * r * gain_ref[...][None]).reshape(nh * HEAD_DIM, tm)

    def silu(p):
        return p * jax.nn.sigmoid(p)

    qa_o[0] = headnorm(seg("qa"), gqa_ref).astype(_BF16)
    ka_o[0] = headnorm(seg("ka"), gka_ref).T.astype(_BF16)
    va_o[0] = seg("va").astype(_BF16)
    za_o[0] = silu(seg("za")).astype(_BF16)
    qb_o[0] = headnorm(seg("qb"), gqb_ref).astype(_BF16)
    kb_o[0] = headnorm(seg("kb"), gkb_ref).T.astype(_BF16)
    vb = seg("vb").astype(_BF16)
    ones = jnp.ones((ONES_ROWS, KCHUNK), _BF16)
    for c in range(tm // KCHUNK):
        for p in range(B_HEADS // 2):
            vb_o[0, c, p * PAIR_ROWS:(p + 1) * PAIR_ROWS, :] = jnp.concatenate(
                [vb[p * LANES:(p + 1) * LANES, c * KCHUNK:(c + 1) * KCHUNK], ones], axis=0)
    zb_o[0] = silu(seg("zb")).astype(_BF16)
    qi_o[0] = seg("qi").astype(_BF16)
    kw = seg("kw")
    wi_o[0] = kw[IDX_DIM:KW_REAL, :]
    row = lax.broadcasted_iota(jnp.int32, kw.shape, 0)
    ki_o[0] = jnp.where(row < IDX_DIM, kw, 0.0).T.astype(_BF16)
    g_lo = _SEG["gates"][0]
    for c in range(2 * D_MODEL // 512):
        gt_o[0, c * 512:(c + 1) * 512, :] = jax.nn.sigmoid(
            proj(g_lo + c * 512, g_lo + (c + 1) * 512)).astype(_BF16)


def _inproj(x, g, wt, gqa, gka, gqb, gkb):
    B, S, D = x.shape
    tm = TM_PROJ
    nt = S // tm
    const = lambda b, t: (0, 0)
    tr = lambda rows: pl.BlockSpec((1, rows, tm), lambda b, t: (b, 0, t))
    nat = lambda cols: pl.BlockSpec((1, tm, cols), lambda b, t: (b, t, 0))
    sds = jax.ShapeDtypeStruct
    out_shape = (
        sds((B, A_WIDTH, S), _BF16), sds((B, S, A_KV_WIDTH), _BF16), sds((B, A_KV_WIDTH, S), _BF16),
        sds((B, A_WIDTH, S), _BF16),
        sds((B, B_WIDTH, S), _BF16), sds((B, S, B_WIDTH), _BF16),
        sds((B, S // KCHUNK, V_ROWS, KCHUNK), _BF16), sds((B, B_WIDTH, S), _BF16),
        sds((B, IDX_WIDTH, S), _BF16), sds((B, S, LANES), _BF16), sds((B, IDX_HEADS, S), _F32),
        sds((B, 2 * D_MODEL, S), _BF16),
    )
    out_specs = (
        tr(A_WIDTH), nat(A_KV_WIDTH), tr(A_KV_WIDTH), tr(A_WIDTH),
        tr(B_WIDTH), nat(B_WIDTH),
        pl.BlockSpec((1, tm // KCHUNK, V_ROWS, KCHUNK), lambda b, t: (b, t, 0, 0)), tr(B_WIDTH),
        tr(IDX_WIDTH), nat(LANES), tr(IDX_HEADS), tr(2 * D_MODEL),
    )
    in_specs = [
        pl.BlockSpec((1, tm, D), lambda b, t: (b, t, 0)),
        pl.BlockSpec((1, D), const),
        pl.BlockSpec((PROJ_ROWS, D), const, pipeline_mode=pl.Buffered(1)),
        pl.BlockSpec((HEAD_DIM, tm), const), pl.BlockSpec((HEAD_DIM, tm), const),
        pl.BlockSpec((HEAD_DIM, tm), const), pl.BlockSpec((HEAD_DIM, tm), const),
    ]
    return pl.pallas_call(
        _inproj_kernel, grid=(B, nt), in_specs=in_specs, out_specs=out_specs, out_shape=out_shape,
        compiler_params=pltpu.CompilerParams(
            dimension_semantics=("parallel", "parallel"), vmem_limit_bytes=VMEM_LIMIT),
        name="inproj",
    )(x, g, wt, gqa, gka, gqb, gkb)


def _mix_a_kernel(q_ref, kp_ref, kc_ref, vp_ref, vc_ref, z_ref, bias_ref, sink_ref, o_ref):
    i = pl.program_id(1)
    zeros = jnp.zeros((HEAD_DIM, TQ), _BF16)
    units = [(j, g) for j in range(A_SUB) for g in range(A_KV_HEADS)]

    def bands(j):
        cur = slice(j * TQ, (j + 1) * TQ)
        old = slice((j - 1) * TQ, j * TQ)
        k_old = kp_ref[0] if j == 0 else kc_ref[0, old, :]
        v_old = vp_ref[0] if j == 0 else vc_ref[0, :, old]
        return (jnp.concatenate([k_old, kc_ref[0, cur, :]], axis=0),
                jnp.concatenate([v_old, vc_ref[0, :, cur]], axis=1))

    scores = []
    for j, g in units:
        cur = slice(j * TQ, (j + 1) * TQ)
        cols = []
        for hh in range(A_GROUPS):
            h = g * A_GROUPS + hh
            parts = [zeros] * A_KV_HEADS
            parts[g] = q_ref[0, h * HEAD_DIM:(h + 1) * HEAD_DIM, cur]
            cols.append(jnp.concatenate(parts, axis=0))
        rhs = jnp.concatenate(cols, axis=1)
        scores.append(jnp.dot(bands(j)[0], rhs, preferred_element_type=_F32))

    pts, invs = [], []
    for (j, g), sc in zip(units, scores):
        probs, inv = [], []
        for hh in range(A_GROUPS):
            h = g * A_GROUPS + hh
            s = sc[:, hh * TQ:(hh + 1) * TQ] + bias_ref[h]
            if j == 0:
                s = jnp.concatenate([jnp.where(i > 0, s[:TQ], NEG), s[TQ:]], axis=0)
            sink = sink_ref[h:h + 1, :]
            m = jnp.maximum(jnp.max(s, axis=0, keepdims=True), sink)
            p = jnp.exp2(s - m)
            denom = jnp.sum(p, axis=0, keepdims=True) + jnp.exp2(sink - m)
            probs.append(p.astype(_BF16))
            inv.append(1.0 / denom)
        pts.append(jnp.concatenate(probs, axis=1))
        invs.append(inv)

    for (j, g), pt, inv in zip(units, pts, invs):
        cur = slice(j * TQ, (j + 1) * TQ)
        out = jnp.dot(bands(j)[1][g * HEAD_DIM:(g + 1) * HEAD_DIM, :], pt,
                      preferred_element_type=_F32)
        for hh in range(A_GROUPS):
            h = g * A_GROUPS + hh
            rows = slice(h * HEAD_DIM, (h + 1) * HEAD_DIM)
            o = out[:, hh * TQ:(hh + 1) * TQ] * inv[hh] * z_ref[0, rows, cur].astype(_F32)
            o_ref[0, rows, cur] = o.astype(_BF16)


def _mix_a(qa_t, ka_n, va_t, za_t, bias_a, sink_a):
    B, _, S = qa_t.shape
    tqa = A_SUB * TQ
    prev = lambda i: jnp.maximum(i * A_SUB - 1, 0)
    in_specs = [
        pl.BlockSpec((1, A_WIDTH, tqa), lambda b, i: (b, 0, i)),
        pl.BlockSpec((1, TQ, A_KV_WIDTH), lambda b, i: (b, prev(i), 0)),
        pl.BlockSpec((1, tqa, A_KV_WIDTH), lambda b, i: (b, i, 0)),
        pl.BlockSpec((1, A_KV_WIDTH, TQ), lambda b, i: (b, 0, prev(i))),
        pl.BlockSpec((1, A_KV_WIDTH, tqa), lambda b, i: (b, 0, i)),
        pl.BlockSpec((1, A_WIDTH, tqa), lambda b, i: (b, 0, i)),
        pl.BlockSpec((A_Q_HEADS, 2 * TQ, TQ), lambda b, i: (0, 0, 0)),
        pl.BlockSpec((A_Q_HEADS, TQ), lambda b, i: (0, 0)),
    ]
    return pl.pallas_call(
        _mix_a_kernel, grid=(B, S // tqa), in_specs=in_specs,
        out_specs=pl.BlockSpec((1, A_WIDTH, tqa), lambda b, i: (b, 0, i)),
        out_shape=jax.ShapeDtypeStruct((B, A_WIDTH, S), _BF16),
        compiler_params=pltpu.CompilerParams(
            dimension_semantics=("parallel", "parallel"), vmem_limit_bytes=VMEM_LIMIT),
        name="mixer_a",
    )(qa_t, ka_n, ka_n, va_t, va_t, za_t, bias_a, sink_a)


def _key_to_f32(key):
    return pltpu.bitcast(jnp.where(key < 0, key ^ jnp.int32(0x7FFFFFFF), key), _F32)


def _f32_to_key(v):
    bits = pltpu.bitcast(v, jnp.int32)
    return jnp.where(bits < 0, bits ^ jnp.int32(0x7FFFFFFF), bits)


def _colsum8(v):
    return jnp.sum(v.reshape(v.shape[0] // 8, 8, v.shape[1]), axis=0)


def _colmin8(v):
    return jnp.min(v.reshape(v.shape[0] // 8, 8, v.shape[1]), axis=0)


def _colmax8(v):
    return jnp.max(v.reshape(v.shape[0] // 8, 8, v.shape[1]), axis=0)


def _for_chunks(n, step):
    def body(t, carry):
        for u in range(4):
            step(4 * t + u)
        return carry

    lax.fori_loop(0, n // 4, body, 0)
    base = (n // 4) * 4
    for rest in (1, 2, 3):
        @pl.when(n - base == rest)
        def _(rest=rest):
            for u in range(rest):
                step(base + u)


def _for_chunks_marking_last(n, step):
    trips = (n - 1) // 4

    def body(t, carry):
        for u in range(4):
            step(4 * t + u, False)
        return carry

    lax.fori_loop(0, trips, body, 0)
    base = trips * 4
    for rest in (1, 2, 3, 4):
        @pl.when(n - base == rest)
        def _(rest=rest):
            for u in range(rest):
                step(base + u, u == rest - 1)


def _mix_b_kernel(qb_ref, kb_ref, vb_ref, zb_ref, qi_ref, ki_ref, wi_ref, bias_ref, o_ref,
                  keys_scr, sc_scr, acc_scr, m_scr, ext_scr, thr_scr, tie_scr):
    i = pl.program_id(1)
    nc = ((i + 1) * TQB + KCHUNK - 1) // KCHUNK
    ns = (nc + 1) // 2
    top_k = TOPK_MAX

    qi = qi_ref[0]
    zpad = jnp.zeros((LANES - IDX_DIM, TQB), _BF16)
    rhs_i = jnp.concatenate(
        [jnp.concatenate([qi[h * IDX_DIM:(h + 1) * IDX_DIM], zpad], axis=0)
         for h in range(IDX_HEADS)], axis=1)
    w = wi_ref[0] * (IDX_DIM ** -0.5 * IDX_HEADS ** -0.5)
    s_minus_t = (lax.broadcasted_iota(jnp.int32, (KCHUNK, TQB), 0)
                 - lax.broadcasted_iota(jnp.int32, (KCHUNK, TQB), 1))

    def index_step(c, diagonal=False):
        r0 = pl.multiple_of(c * KCHUNK, KCHUNK)
        d = jnp.dot(ki_ref[0, pl.ds(r0, KCHUNK), :], rhs_i,
                    preferred_element_type=_F32)
        acc = w[0:1, :] * jnp.maximum(d[:, 0:TQB], 0.0)
        for h in range(1, IDX_HEADS):
            acc = acc + w[h:h + 1, :] * jnp.maximum(d[:, h * TQB:(h + 1) * TQB], 0.0)
        if diagonal:
            causal = s_minus_t <= (i * TQB - r0)
            score = jnp.where(causal, acc, -jnp.inf)
            lowest = jnp.where(causal, acc, jnp.inf)
        else:
            score = lowest = acc
        keys_scr[pl.ds(r0, KCHUNK), :] = score
        ext_scr[0:8, :] = jnp.maximum(ext_scr[0:8, :], _colmax8(score))
        ext_scr[8:16, :] = jnp.minimum(ext_scr[8:16, :], _colmin8(lowest))
        ext_scr[16:24, :] += _colsum8(jnp.where(score >= 0.0, 1.0, 0.0))
        ext_scr[24:32, :] += _colsum8(jnp.where(score > 0.0, 1.0, 0.0))

    ext_scr[0:8, :] = jnp.full((8, TQB), -jnp.inf, _F32)
    ext_scr[8:16, :] = jnp.full((8, TQB), jnp.inf, _F32)
    ext_scr[16:32, :] = jnp.zeros((16, TQB), _F32)
    _for_chunks_marking_last(nc, index_step)

    @pl.when(nc % 2 == 1)
    def _():
        keys_scr[pl.ds(pl.multiple_of(nc * KCHUNK, KCHUNK), KCHUNK), :] = jnp.full(
            (KCHUNK, TQB), -jnp.inf, _F32)

    def count_rows(src, hit):
        n_acc = 4

        def bump(accs, r0, n_rows):
            accs = list(accs)
            rows = src[pl.ds(r0, n_rows), :]
            for j in range(n_rows // 8):
                a = accs[j % n_acc]
                accs[j % n_acc] = jnp.where(hit(rows[j * 8:(j + 1) * 8]), a + 1, a)
            return tuple(accs)

        accs = lax.fori_loop(
            0, nc // 2, lambda c, accs: bump(accs, pl.multiple_of(c * KSUPER, KSUPER), KSUPER),
            tuple(jnp.zeros((8, TQB), jnp.int32) for _ in range(n_acc)))
        accs = lax.cond(nc % 2 == 1,
                        lambda accs: bump(accs, pl.multiple_of((nc - 1) * KCHUNK, KCHUNK), KCHUNK),
                        lambda accs: accs, accs)
        acc = accs[0]
        for a in accs[1:]:
            acc = acc + a
        return jnp.sum(acc, axis=0, keepdims=True)

    def count_ge(cand, strict=False):
        return count_rows(keys_scr, (lambda r: r > cand) if strict else (lambda r: r >= cand))

    @pl.when(i * TQB + TQB <= top_k)
    def _():
        thr_scr[...] = jnp.full((1, TQB), -jnp.inf, _F32)
        tie_scr[...] = jnp.full((1, TQB), -1, jnp.int32)

    @pl.when(i * TQB + TQB > top_k)
    def _():
        cnt_nonneg = jnp.sum(ext_scr[16:24, :], axis=0, keepdims=True).astype(jnp.int32)
        cnt_pos = jnp.sum(ext_scr[24:32, :], axis=0, keepdims=True).astype(jnp.int32)
        key_max = _f32_to_key(jnp.max(ext_scr[0:8, :], axis=0, keepdims=True))
        key_min = _f32_to_key(jnp.min(ext_scr[8:16, :], axis=0, keepdims=True))
        n_causal = i * TQB + 1 + lax.broadcasted_iota(jnp.int32, (1, TQB), 1)
        nonneg = cnt_nonneg >= top_k
        lo = jnp.where(nonneg, 0, key_min)
        cnt_lo = jnp.where(nonneg, cnt_nonneg, n_causal)
        hi = jnp.where(nonneg, jnp.where(cnt_pos < top_k, 1, key_max + 1), 0)

        def halve_values(_, state):
            lo_v, hi_v, cnt_lo = state
            mid = lo_v + (hi_v - lo_v) * 0.5
            mid = jnp.where(mid > lo_v, jnp.where(mid < hi_v, mid, lo_v), lo_v)
            cnt = count_ge(mid)
            ok = cnt >= top_k
            return jnp.where(ok, mid, lo_v), jnp.where(ok, hi_v, mid), jnp.where(ok, cnt, cnt_lo)

        def halve_keys(_, state):
            lo, hi, cnt_lo = state
            mid = lo + lax.shift_right_logical(hi - lo, 1)
            cnt = count_ge(_key_to_f32(mid))
            ok = cnt >= top_k
            return jnp.where(ok, mid, lo), jnp.where(ok, hi, mid), jnp.where(ok, cnt, cnt_lo)

        def finished(state):
            lo, hi, cnt_lo = state
            done = jnp.where(cnt_lo == top_k, 1, jnp.where(hi - lo <= 1, 1, 0))
            return jnp.min(done) == 1

        lo_v, hi_v, cnt_lo = lax.fori_loop(0, VALUE_PASSES, halve_values,
                                           (_key_to_f32(lo), _key_to_f32(hi), cnt_lo))
        state = (_f32_to_key(lo_v), _f32_to_key(hi_v), cnt_lo)
        stops = (0,) + SEARCH_CHECKS + (31,)
        state = lax.fori_loop(stops[0], stops[1], halve_keys, state)
        for first, last in zip(stops[1:-1], stops[2:]):
            state = lax.cond(finished(state), lambda s: s,
                             functools.partial(lax.fori_loop, first, last, halve_keys), state)
        lo, _, cnt_ge = state
        thr = _key_to_f32(lo)
        thr_scr[...] = thr
        tie_scr[...] = jnp.full((1, TQB), 2 ** 30, jnp.int32)

        @pl.when(jnp.max(cnt_ge) > top_k)
        def _():
            need = (top_k - count_ge(thr, strict=True)).astype(_F32)
            lower = (lax.broadcasted_iota(jnp.int32, (KCHUNK, KCHUNK), 0)
                     >= lax.broadcasted_iota(jnp.int32, (KCHUNK, KCHUNK), 1))
            lower = jnp.where(lower, 1.0, 0.0).astype(_BF16)

            def prefix_body(c, state):
                before, below = state
                for half in range(KSUPER // KCHUNK):
                    r0 = pl.multiple_of(c * KSUPER + half * KCHUNK, KCHUNK)
                    tied = jnp.where(keys_scr[pl.ds(r0, KCHUNK), :] == thr, 1.0, 0.0).astype(_BF16)
                    upto = jnp.dot(lower, tied, preferred_element_type=_F32) + before
                    below = below + jnp.sum(jnp.where(upto < need, 1, 0).reshape(KCHUNK // 8, 8, TQB), axis=0)
                    before = upto[KCHUNK - 1:KCHUNK, :]
                return before, below

            _, below = lax.fori_loop(0, ns, prefix_body,
                                     (jnp.zeros((1, TQB), _F32), jnp.zeros((8, TQB), jnp.int32)))
            tie_scr[...] = jnp.sum(below, axis=0, keepdims=True)

    thr = thr_scr[...]
    tie = tie_scr[...]
    row = lax.broadcasted_iota(jnp.int32, (KCHUNK, TQB), 0)

    zeros = jnp.zeros((HEAD_DIM, TQB), _BF16)
    n_pairs = B_HEADS // 2
    rhs = []
    for p in range(n_pairs):
        q0 = qb_ref[0, (2 * p) * HEAD_DIM:(2 * p + 1) * HEAD_DIM, :]
        q1 = qb_ref[0, (2 * p + 1) * HEAD_DIM:(2 * p + 2) * HEAD_DIM, :]
        rhs.append(jnp.concatenate([jnp.concatenate([q0, zeros], axis=0),
                                    jnp.concatenate([zeros, q1], axis=0)], axis=1))

    def score_step(c):
        r0 = pl.multiple_of(c * KCHUNK, KCHUNK)
        k = keys_scr[pl.ds(r0, KCHUNK), :]
        mask = jnp.where(k == thr, jnp.where(row <= tie - r0, 0.0, NEG), jnp.where(k > thr, 0.0, NEG))
        for p in range(n_pairs):
            s = jnp.dot(kb_ref[0, pl.ds(r0, KCHUNK), p * LANES:(p + 1) * LANES], rhs[p],
                        preferred_element_type=_F32)
            for hh in range(2):
                h = 2 * p + hh
                parts = []
                for half in range(KCHUNK // TQ):
                    jb = c * (KCHUNK // TQ) + half
                    subs = []
                    for sub in range(TQB // TQ):
                        kind = jnp.clip(jb - (i * (TQB // TQ) + sub) + 2, 0, 2)
                        lanes = slice(sub * TQ, (sub + 1) * TQ)
                        subs.append(s[half * TQ:(half + 1) * TQ, hh * TQB + sub * TQ:hh * TQB + (sub + 1) * TQ]
                                    + bias_ref[h, kind] + mask[half * TQ:(half + 1) * TQ, lanes])
                    parts.append(jnp.concatenate(subs, axis=1))
                sh = jnp.concatenate(parts, axis=0)
                sc_scr[pl.ds(r0, KCHUNK), h * TQB:(h + 1) * TQB] = sh
                m_scr[h * 8:(h + 1) * 8, :] = jnp.maximum(m_scr[h * 8:(h + 1) * 8, :], _colmax8(sh))

    m_scr[...] = jnp.full(m_scr.shape, NEG, _F32)
    _for_chunks(nc, score_step)
    m_row = jnp.concatenate(
        [jnp.max(m_scr[h * 8:(h + 1) * 8, :], axis=0, keepdims=True) for h in range(B_HEADS)],
        axis=1)
    acc_scr[...] = jnp.zeros_like(acc_scr)

    def pv_step(c):
        r0 = pl.multiple_of(c * KCHUNK, KCHUNK)
        for p in range(n_pairs):
            cols = slice(2 * p * TQB, (2 * p + 2) * TQB)
            pr = jnp.exp2(sc_scr[pl.ds(r0, KCHUNK), cols] - m_row[:, cols]).astype(_BF16)
            acc_scr[p] += jnp.dot(vb_ref[0, c, p * PAIR_ROWS:(p + 1) * PAIR_ROWS, :], pr,
                                  preferred_element_type=_F32)

    _for_chunks(nc, pv_step)
    for p in range(n_pairs):
        inv = 1.0 / acc_scr[p, 2 * HEAD_DIM:2 * HEAD_DIM + 1, :]
        for hh in range(2):
            rows = slice((2 * p + hh) * HEAD_DIM, (2 * p + hh + 1) * HEAD_DIM)
            o = (acc_scr[p, hh * HEAD_DIM:(hh + 1) * HEAD_DIM, hh * TQB:(hh + 1) * TQB]
                 * inv[:, hh * TQB:(hh + 1) * TQB] * zb_ref[0, rows, :].astype(_F32))
            o_ref[0, rows, :] = o.astype(_BF16)


def _mix_b(qb_t, kb_n, vb_c, zb_t, qi_t, ki_n, wi_t, bias_b):
    B, _, S = qb_t.shape
    nq = S // TQB
    per_q = lambda rows: pl.BlockSpec((1, rows, TQB), lambda b, i: (b, 0, i))
    in_specs = [
        per_q(B_WIDTH),
        pl.BlockSpec((1, S, B_WIDTH), lambda b, i: (b, 0, 0)),
        pl.BlockSpec((1, S // KCHUNK, V_ROWS, KCHUNK), lambda b, i: (b, 0, 0, 0)),
        per_q(B_WIDTH),
        per_q(IDX_WIDTH),
        pl.BlockSpec((1, S, LANES), lambda b, i: (b, 0, 0)),
        per_q(IDX_HEADS),
        pl.BlockSpec((B_HEADS, 3, TQ, TQ), lambda b, i: (0, 0, 0, 0)),
    ]
    scratch = [
        pltpu.VMEM((S, TQB), _F32),
        pltpu.VMEM((S, B_HEADS * TQB), _F32),
        pltpu.VMEM((B_HEADS // 2, PAIR_ROWS, 2 * TQB), _F32),
        pltpu.VMEM((B_HEADS * 8, TQB), _F32),
        pltpu.VMEM((32, TQB), _F32),
        pltpu.VMEM((1, TQB), _F32),
        pltpu.VMEM((1, TQB), jnp.int32),
    ]
    return pl.pallas_call(
        _mix_b_kernel, grid=(B, nq), in_specs=in_specs,
        out_specs=per_q(B_WIDTH),
        out_shape=jax.ShapeDtypeStruct((B, B_WIDTH, S), _BF16),
        scratch_shapes=scratch,
        compiler_params=pltpu.CompilerParams(
            dimension_semantics=("parallel", "arbitrary"), vmem_limit_bytes=VMEM_LIMIT),
        name="mixer_b",
    )(qb_t, kb_n, vb_c, zb_t, qi_t, ki_n, wi_t, bias_b)


def _merge_kernel(x_ref, ya_ref, yb_ref, gt_ref, wpa_ref, wpb_ref, wo_ref, o_ref):
    pa = jnp.dot(wpa_ref[...], ya_ref[0], preferred_element_type=_F32)
    pb = jnp.dot(wpb_ref[...], yb_ref[0], preferred_element_type=_F32)
    merged = (gt_ref[0, :D_MODEL, :].astype(_F32) * pa
              + gt_ref[0, D_MODEL:, :].astype(_F32) * pb).astype(_BF16)
    out_t = jnp.dot(wo_ref[...], merged, preferred_element_type=_F32)
    o_ref[0] = x_ref[0] + out_t.T


def _merge(x, ya_t, yb_t, gt_t, wpa_t, wpb_t, wo_t):
    B, S, D = x.shape
    tm = TM_PROJ
    const = lambda b, t: (0, 0)
    in_specs = [
        pl.BlockSpec((1, tm, D), lambda b, t: (b, t, 0)),
        pl.BlockSpec((1, A_WIDTH, tm), lambda b, t: (b, 0, t)),
        pl.BlockSpec((1, B_WIDTH, tm), lambda b, t: (b, 0, t)),
        pl.BlockSpec((1, 2 * D, tm), lambda b, t: (b, 0, t)),
        pl.BlockSpec((D, A_WIDTH), const),
        pl.BlockSpec((D, B_WIDTH), const),
        pl.BlockSpec((D, D), const),
    ]
    return pl.pallas_call(
        _merge_kernel, grid=(B, S // tm), in_specs=in_specs,
        out_specs=pl.BlockSpec((1, tm, D), lambda b, t: (b, t, 0)),
        out_shape=jax.ShapeDtypeStruct((B, S, D), x.dtype),
        compiler_params=pltpu.CompilerParams(
            dimension_semantics=("parallel", "parallel"), vmem_limit_bytes=VMEM_LIMIT),
        name="merge_out",
    )(x, ya_t, yb_t, gt_t, wpa_t, wpb_t, wo_t)


def _t5_bucket(n):
    n = np.maximum(n, 0)
    max_exact = N_BUCKETS // 2
    nf = np.maximum(n, 1).astype(np.float64)
    large = max_exact + np.floor(np.log(nf / max_exact) / math.log(MAX_DISTANCE / max_exact)
                                 * (N_BUCKETS - max_exact)).astype(np.int64)
    large = np.minimum(large, N_BUCKETS - 1)
    return np.where(n < max_exact, n, large)


def _bias_tables(rel_bias):
    t = np.arange(TQ)[None, :]
    s = np.arange(TQ)[:, None]
    d_prev = t + TQ - s
    d_cur = t - s
    table_a = rel_bias[:, :A_Q_HEADS].astype(_F32).T * LOG2E
    table_b = rel_bias[:, A_Q_HEADS:].astype(_F32).T * LOG2E

    def tile(table, dist, windowed):
        onehot = (_t5_bucket(dist)[..., None] == np.arange(N_BUCKETS)).astype(np.float32)
        b = jnp.einsum("stk,hk->hst", onehot, table, precision=lax.Precision.HIGHEST)
        if windowed:
            b = jnp.where(((dist >= 0) & (dist < WINDOW))[None], b, NEG)
        return b

    bias_a = jnp.concatenate([tile(table_a, d_prev, True), tile(table_a, d_cur, True)], axis=1)
    far = jnp.broadcast_to(table_b[:, N_BUCKETS - 1][:, None, None], (B_HEADS, TQ, TQ))
    bias_b = jnp.stack([far, tile(table_b, d_prev, False), tile(table_b, d_cur, False)], axis=1)
    return bias_a, bias_b


def kernel(x, norm_g, w_in, qnorm_a, knorm_a, sinks_a, qnorm_b, knorm_b, rel_bias,
           w_proj_a, w_proj_b, w_out):
    assert norm_g.shape[0] == 1, "single-layer block"
    B, S, D = x.shape
    assert D == D_MODEL and S % TM_PROJ == 0 and S % KSUPER == 0 and S % TQB == 0 and TQB % TQ == 0
    assert MAX_DISTANCE <= TQ

    wt = w_in[0].T
    n_real = _SEG["kw"][0] + KW_REAL
    wt = jnp.concatenate(
        [wt[:n_real], jnp.zeros((_SEG["kw"][1] - n_real, D), wt.dtype), wt[n_real:]], axis=0)
    wt = wt.astype(_BF16)
    bcast = lambda g, scale: jnp.broadcast_to((g.astype(_F32) * scale)[:, None], (HEAD_DIM, TM_PROJ))
    q_scale = HEAD_DIM ** -0.5
    (qa_t, ka_n, va_t, za_t, qb_t, kb_n, vb_c, zb_t, qi_t, ki_n, wi_t, gt_t) = _inproj(
        x, norm_g[0][None, :].astype(_F32), wt,
        bcast(qnorm_a[0], q_scale * LOG2E), bcast(knorm_a[0], 1.0),
        bcast(qnorm_b[0], q_scale * LOG2E), bcast(knorm_b[0], 1.0))

    bias_a, bias_b = _bias_tables(rel_bias)
    sink_a = jnp.broadcast_to((sinks_a[0].astype(_F32) * LOG2E)[:, None], (A_Q_HEADS, TQ))
    ya_t = _mix_a(qa_t, ka_n, va_t, za_t, bias_a, sink_a)
    yb_t = _mix_b(qb_t, kb_n, vb_c, zb_t, qi_t, ki_n, wi_t, bias_b)
    return _merge(x, ya_t, yb_t, gt_t,
                  w_proj_a[0].T.astype(_BF16), w_proj_b[0].T.astype(_BF16), w_out[0].T.astype(_BF16))
```

```python
import functools
import math

import jax
import jax.numpy as jnp
import numpy as np
from jax import lax
from jax.experimental import pallas as pl
from jax.experimental.pallas import tpu as pltpu

D_MODEL = 1024
HEAD_DIM = 64
A_Q_HEADS = 8
A_KV_HEADS = 2
A_GROUPS = A_Q_HEADS // A_KV_HEADS
A_WIDTH = A_Q_HEADS * HEAD_DIM
A_KV_WIDTH = A_KV_HEADS * HEAD_DIM
B_HEADS = 8
B_WIDTH = B_HEADS * HEAD_DIM
IDX_HEADS = 8
IDX_DIM = 32
IDX_WIDTH = IDX_HEADS * IDX_DIM
WINDOW = 128
TOPK_MAX = 256
N_BUCKETS = 32
MAX_DISTANCE = 128
RMS_EPS = 1e-6

LANES = 128
TQ = 128
A_SUB = 4
TQB = 256
KCHUNK = 256
KSUPER = 2 * KCHUNK
assert KCHUNK == TQB
VALUE_PASSES = 16
SEARCH_CHECKS = (1, 3, 5, 8)
TM_PROJ = 1024
NEG = -1e30
LOG2E = math.log2(math.e)
ONES_ROWS = 16
PAIR_ROWS = 2 * HEAD_DIM + ONES_ROWS
V_ROWS = (B_HEADS // 2) * PAIR_ROWS
VMEM_LIMIT = 56 * 1024 * 1024

_SEG = {}
_off = 0
for _name, _rows in (("qa", A_WIDTH), ("ka", A_KV_WIDTH), ("va", A_KV_WIDTH), ("za", A_WIDTH),
                     ("qb", B_WIDTH), ("kb", B_WIDTH), ("vb", B_WIDTH), ("zb", B_WIDTH),
                     ("qi", IDX_WIDTH), ("kw", LANES), ("gates", 2 * D_MODEL)):
    _SEG[_name] = (_off, _off + _rows)
    _off += _rows
PROJ_ROWS = _off
KW_REAL = IDX_DIM + IDX_HEADS

_F32 = jnp.float32
_BF16 = jnp.bfloat16
_NT = (((1,), (1,)), ((), ()))


def _inproj_kernel(x_ref, g_ref, wt_ref, gqa_ref, gka_ref, gqb_ref, gkb_ref,
                   qa_o, ka_o, va_o, za_o, qb_o, kb_o, vb_o, zb_o, qi_o, ki_o, wi_o, gt_o):
    tm = x_ref.shape[1]
    x = x_ref[0]
    ms = jnp.mean(x * x, axis=-1, keepdims=True)
    h = (x * lax.rsqrt(ms + RMS_EPS) * g_ref[...]).astype(_BF16)

    def proj(lo, hi):
        return lax.dot_general(wt_ref[lo:hi, :], h, _NT, preferred_element_type=_F32)

    def seg(name):
        return proj(*_SEG[name])

    def headnorm(p, gain_ref):
        nh = p.shape[0] // HEAD_DIM
        p3 = p.reshape(nh, HEAD_DIM, tm)
        r = lax.rsqrt(jnp.mean(p3 * p3, axis=1, keepdims=True) + RMS_EPS)
        return (p3 * r * gain_ref[...][None]).reshape(nh * HEAD_DIM, tm)

    def silu(p):
        return p * jax.nn.sigmoid(p)

    qa_o[0] = headnorm(seg("qa"), gqa_ref).astype(_BF16)
    ka_o[0] = headnorm(seg("ka"), gka_ref).T.astype(_BF16)
    va_o[0] = seg("va").astype(_BF16)
    za_o[0] = silu(seg("za")).astype(_BF16)
    qb_o[0] = headnorm(seg("qb"), gqb_ref).astype(_BF16)
    kb_o[0] = headnorm(seg("kb"), gkb_ref).T.astype(_BF16)
    vb = seg("vb").astype(_BF16)
    ones = jnp.ones((ONES_ROWS, KCHUNK), _BF16)
    for c in range(tm // KCHUNK):
        for p in range(B_HEADS // 2):
            vb_o[0, c, p * PAIR_ROWS:(p + 1) * PAIR_ROWS, :] = jnp.concatenate(
                [vb[p * LANES:(p + 1) * LANES, c * KCHUNK:(c + 1) * KCHUNK], ones], axis=0)
    zb_o[0] = silu(seg("zb")).astype(_BF16)
    qi_o[0] = seg("qi").astype(_BF16)
    kw = seg("kw")
    wi_o[0] = kw[IDX_DIM:KW_REAL, :]
    row = lax.broadcasted_iota(jnp.int32, kw.shape, 0)
    ki_o[0] = jnp.where(row < IDX_DIM, kw, 0.0).T.astype(_BF16)
    g_lo = _SEG["gates"][0]
    for c in range(2 * D_MODEL // 512):
        gt_o[0, c * 512:(c + 1) * 512, :] = jax.nn.sigmoid(
            proj(g_lo + c * 512, g_lo + (c + 1) * 512)).astype(_BF16)


def _inproj(x, g, wt, gqa, gka, gqb, gkb):
    B, S, D = x.shape
    tm = TM_PROJ
    nt = S // tm
    const = lambda b, t: (0, 0)
    tr = lambda rows: pl.BlockSpec((1, rows, tm), lambda b, t: (b, 0, t))
    nat = lambda cols: pl.BlockSpec((1, tm, cols), lambda b, t: (b, t, 0))
    sds = jax.ShapeDtypeStruct
    out_shape = (
        sds((B, A_WIDTH, S), _BF16), sds((B, S, A_KV_WIDTH), _BF16), sds((B, A_KV_WIDTH, S), _BF16),
        sds((B, A_WIDTH, S), _BF16),
        sds((B, B_WIDTH, S), _BF16), sds((B, S, B_WIDTH), _BF16),
        sds((B, S // KCHUNK, V_ROWS, KCHUNK), _BF16), sds((B, B_WIDTH, S), _BF16),
        sds((B, IDX_WIDTH, S), _BF16), sds((B, S, LANES), _BF16), sds((B, IDX_HEADS, S), _F32),
        sds((B, 2 * D_MODEL, S), _BF16),
    )
    out_specs = (
        tr(A_WIDTH), nat(A_KV_WIDTH), tr(A_KV_WIDTH), tr(A_WIDTH),
        tr(B_WIDTH), nat(B_WIDTH),
        pl.BlockSpec((1, tm // KCHUNK, V_ROWS, KCHUNK), lambda b, t: (b, t, 0, 0)), tr(B_WIDTH),
        tr(IDX_WIDTH), nat(LANES), tr(IDX_HEADS), tr(2 * D_MODEL),
    )
    in_specs = [
        pl.BlockSpec((1, tm, D), lambda b, t: (b, t, 0)),
        pl.BlockSpec((1, D), const),
        pl.BlockSpec((PROJ_ROWS, D), const, pipeline_mode=pl.Buffered(1)),
        pl.BlockSpec((HEAD_DIM, tm), const), pl.BlockSpec((HEAD_DIM, tm), const),
        pl.BlockSpec((HEAD_DIM, tm), const), pl.BlockSpec((HEAD_DIM, tm), const),
    ]
    return pl.pallas_call(
        _inproj_kernel, grid=(B, nt), in_specs=in_specs, out_specs=out_specs, out_shape=out_shape,
        compiler_params=pltpu.CompilerParams(
            dimension_semantics=("parallel", "parallel"), vmem_limit_bytes=VMEM_LIMIT),
        name="inproj",
    )(x, g, wt, gqa, gka, gqb, gkb)


def _mix_a_kernel(q_ref, kp_ref, kc_ref, vp_ref, vc_ref, z_ref, bias_ref, sink_ref, o_ref):
    i = pl.program_id(1)
    zeros = jnp.zeros((HEAD_DIM, TQ), _BF16)
    units = [(j, g) for j in range(A_SUB) for g in range(A_KV_HEADS)]

    def bands(j):
        cur = slice(j * TQ, (j + 1) * TQ)
        old = slice((j - 1) * TQ, j * TQ)
        k_old = kp_ref[0] if j == 0 else kc_ref[0, old, :]
        v_old = vp_ref[0] if j == 0 else vc_ref[0, :, old]
        return (jnp.concatenate([k_old, kc_ref[0, cur, :]], axis=0),
                jnp.concatenate([v_old, vc_ref[0, :, cur]], axis=1))

    scores = []
    for j, g in units:
        cur = slice(j * TQ, (j + 1) * TQ)
        cols = []
        for hh in range(A_GROUPS):
            h = g * A_GROUPS + hh
            parts = [zeros] * A_KV_HEADS
            parts[g] = q_ref[0, h * HEAD_DIM:(h + 1) * HEAD_DIM, cur]
            cols.append(jnp.concatenate(parts, axis=0))
        rhs = jnp.concatenate(cols, axis=1)
        scores.append(jnp.dot(bands(j)[0], rhs, preferred_element_type=_F32))

    pts, invs = [], []
    for (j, g), sc in zip(units, scores):
        probs, inv = [], []
        for hh in range(A_GROUPS):
            h = g * A_GROUPS + hh
            s = sc[:, hh * TQ:(hh + 1) * TQ] + bias_ref[h]
            if j == 0:
                s = jnp.concatenate([jnp.where(i > 0, s[:TQ], NEG), s[TQ:]], axis=0)
            sink = sink_ref[h:h + 1, :]
            m = jnp.maximum(jnp.max(s, axis=0, keepdims=True), sink)
            p = jnp.exp2(s - m)
            denom = jnp.sum(p, axis=0, keepdims=True) + jnp.exp2(sink - m)
            probs.append(p.astype(_BF16))
            inv.append(1.0 / denom)
        pts.append(jnp.concatenate(probs, axis=1))
        invs.append(inv)

    for (j, g), pt, inv in zip(units, pts, invs):
        cur = slice(j * TQ, (j + 1) * TQ)
        out = jnp.dot(bands(j)[1][g * HEAD_DIM:(g + 1) * HEAD_DIM, :], pt,
                      preferred_element_type=_F32)
        for hh in range(A_GROUPS):
            h = g * A_GROUPS + hh
            rows = slice(h * HEAD_DIM, (h + 1) * HEAD_DIM)
            o = out[:, hh * TQ:(hh + 1) * TQ] * inv[hh] * z_ref[0, rows, cur].astype(_F32)
            o_ref[0, rows, cur] = o.astype(_BF16)


def _mix_a(qa_t, ka_n, va_t, za_t, bias_a, sink_a):
    B, _, S = qa_t.shape
    tqa = A_SUB * TQ
    prev = lambda i: jnp.maximum(i * A_SUB - 1, 0)
    in_specs = [
        pl.BlockSpec((1, A_WIDTH, tqa), lambda b, i: (b, 0, i)),
        pl.BlockSpec((1, TQ, A_KV_WIDTH), lambda b, i: (b, prev(i), 0)),
        pl.BlockSpec((1, tqa, A_KV_WIDTH), lambda b, i: (b, i, 0)),
        pl.BlockSpec((1, A_KV_WIDTH, TQ), lambda b, i: (b, 0, prev(i))),
        pl.BlockSpec((1, A_KV_WIDTH, tqa), lambda b, i: (b, 0, i)),
        pl.BlockSpec((1, A_WIDTH, tqa), lambda b, i: (b, 0, i)),
        pl.BlockSpec((A_Q_HEADS, 2 * TQ, TQ), lambda b, i: (0, 0, 0)),
        pl.BlockSpec((A_Q_HEADS, TQ), lambda b, i: (0, 0)),
    ]
    return pl.pallas_call(
        _mix_a_kernel, grid=(B, S // tqa), in_specs=in_specs,
        out_specs=pl.BlockSpec((1, A_WIDTH, tqa), lambda b, i: (b, 0, i)),
        out_shape=jax.ShapeDtypeStruct((B, A_WIDTH, S), _BF16),
        compiler_params=pltpu.CompilerParams(
            dimension_semantics=("parallel", "parallel"), vmem_limit_bytes=VMEM_LIMIT),
        name="mixer_a",
    )(qa_t, ka_n, ka_n, va_t, va_t, za_t, bias_a, sink_a)


def _key_to_f32(key):
    return pltpu.bitcast(jnp.where(key < 0, key ^ jnp.int32(0x7FFFFFFF), key), _F32)


def _f32_to_key(v):
    bits = pltpu.bitcast(v, jnp.int32)
    return jnp.where(bits < 0, bits ^ jnp.int32(0x7FFFFFFF), bits)


def _colsum8(v):
    return jnp.sum(v.reshape(v.shape[0] // 8, 8, v.shape[1]), axis=0)


def _colmin8(v):
    return jnp.min(v.reshape(v.shape[0] // 8, 8, v.shape[1]), axis=0)


def _colmax8(v):
    return jnp.max(v.reshape(v.shape[0] // 8, 8, v.shape[1]), axis=0)


def _for_chunks(n, step):
    def body(t, carry):
        for u in range(4):
            step(4 * t + u)
        return carry

    lax.fori_loop(0, n // 4, body, 0)
    base = (n // 4) * 4
    for rest in (1, 2, 3):
        @pl.when(n - base == rest)
        def _(rest=rest):
            for u in range(rest):
                step(base + u)


def _for_chunks_marking_last(n, step):
    trips = (n - 1) // 4

    def body(t, carry):
        for u in range(4):
            step(4 * t + u, False)
        return carry

    lax.fori_loop(0, trips, body, 0)
    base = trips * 4
    for rest in (1, 2, 3, 4):
        @pl.when(n - base == rest)
        def _(rest=rest):
            for u in range(rest):
                step(base + u, u == rest - 1)


def _mix_b_kernel(qb_ref, kb_ref, vb_ref, zb_ref, qi_ref, ki_ref, wi_ref, bias_ref, o_ref,
                  keys_scr, sc_scr, acc_scr, m_scr, ext_scr, thr_scr, tie_scr):
    i = pl.program_id(1)
    nc = ((i + 1) * TQB + KCHUNK - 1) // KCHUNK
    ns = (nc + 1) // 2
    top_k = TOPK_MAX

    qi = qi_ref[0]
    zpad = jnp.zeros((LANES - IDX_DIM, TQB), _BF16)
    rhs_i = jnp.concatenate(
        [jnp.concatenate([qi[h * IDX_DIM:(h + 1) * IDX_DIM], zpad], axis=0)
         for h in range(IDX_HEADS)], axis=1)
    w = wi_ref[0] * (IDX_DIM ** -0.5 * IDX_HEADS ** -0.5)
    s_minus_t = (lax.broadcasted_iota(jnp.int32, (KCHUNK, TQB), 0)
                 - lax.broadcasted_iota(jnp.int32, (KCHUNK, TQB), 1))

    def index_step(c, diagonal=False):
        r0 = pl.multiple_of(c * KCHUNK, KCHUNK)
        d = jnp.dot(ki_ref[0, pl.ds(r0, KCHUNK), :], rhs_i,
                    preferred_element_type=_F32)
        acc = w[0:1, :] * jnp.maximum(d[:, 0:TQB], 0.0)
        for h in range(1, IDX_HEADS):
            acc = acc + w[h:h + 1, :] * jnp.maximum(d[:, h * TQB:(h + 1) * TQB], 0.0)
        if diagonal:
            causal = s_minus_t <= (i * TQB - r0)
            score = jnp.where(causal, acc, -jnp.inf)
            lowest = jnp.where(causal, acc, jnp.inf)
        else:
            score = lowest = acc
        keys_scr[pl.ds(r0, KCHUNK), :] = score
        ext_scr[0:8, :] = jnp.maximum(ext_scr[0:8, :], _colmax8(score))
        ext_scr[8:16, :] = jnp.minimum(ext_scr[8:16, :], _colmin8(lowest))
        ext_scr[16:24, :] += _colsum8(jnp.where(score >= 0.0, 1.0, 0.0))
        ext_scr[24:32, :] += _colsum8(jnp.where(score > 0.0, 1.0, 0.0))

    ext_scr[0:8, :] = jnp.full((8, TQB), -jnp.inf, _F32)
    ext_scr[8:16, :] = jnp.full((8, TQB), jnp.inf, _F32)
    ext_scr[16:32, :] = jnp.zeros((16, TQB), _F32)
    _for_chunks_marking_last(nc, index_step)

    @pl.when(nc % 2 == 1)
    def _():
        keys_scr[pl.ds(pl.multiple_of(nc * KCHUNK, KCHUNK), KCHUNK), :] = jnp.full(
            (KCHUNK, TQB), -jnp.inf, _F32)

    def count_rows(src, hit):
        n_acc = 4
        left, right = slice(0, TQ), slice(TQ, TQB)

        def bump(accs, r0, n_rows, diagonal):
            accs = list(accs)
            rows = src[pl.ds(r0, n_rows), :]
            n_full = (n_rows - (KCHUNK - TQ) if diagonal else n_rows) // 8
            for j in range(n_rows // 8):
                a = accs[j % n_acc]
                blk = rows[j * 8:(j + 1) * 8]
                if j < n_full:
                    accs[j % n_acc] = jnp.where(hit(blk, slice(0, TQB)), a + 1, a)
                else:
                    accs[j % n_acc] = jnp.concatenate(
                        [a[:, left], jnp.where(hit(blk[:, right], right), a[:, right] + 1, a[:, right])], axis=1)
            return tuple(accs)

        accs = lax.fori_loop(
            0, (nc - 1) // 2, lambda c, accs: bump(accs, pl.multiple_of(c * KSUPER, KSUPER), KSUPER, False),
            tuple(jnp.zeros((8, TQB), jnp.int32) for _ in range(n_acc)))
        accs = lax.cond(nc % 2 == 1,
                        lambda accs: bump(accs, pl.multiple_of((nc - 1) * KCHUNK, KCHUNK), KCHUNK, True),
                        lambda accs: bump(accs, pl.multiple_of((nc - 2) * KCHUNK, KSUPER), KSUPER, True), accs)
        acc = accs[0]
        for a in accs[1:]:
            acc = acc + a
        return jnp.sum(acc, axis=0, keepdims=True)

    def count_ge(cand, strict=False):
        return count_rows(keys_scr, (lambda r, lanes: r > cand[:, lanes]) if strict
                          else (lambda r, lanes: r >= cand[:, lanes]))

    @pl.when(i * TQB + TQB <= top_k)
    def _():
        thr_scr[...] = jnp.full((1, TQB), -jnp.inf, _F32)
        tie_scr[...] = jnp.full((1, TQB), -1, jnp.int32)

    @pl.when(i * TQB + TQB > top_k)
    def _():
        cnt_nonneg = jnp.sum(ext_scr[16:24, :], axis=0, keepdims=True).astype(jnp.int32)
        cnt_pos = jnp.sum(ext_scr[24:32, :], axis=0, keepdims=True).astype(jnp.int32)
        key_max = _f32_to_key(jnp.max(ext_scr[0:8, :], axis=0, keepdims=True))
        key_min = _f32_to_key(jnp.min(ext_scr[8:16, :], axis=0, keepdims=True))
        n_causal = i * TQB + 1 + lax.broadcasted_iota(jnp.int32, (1, TQB), 1)
        nonneg = cnt_nonneg >= top_k
        lo = jnp.where(nonneg, 0, key_min)
        cnt_lo = jnp.where(nonneg, cnt_nonneg, n_causal)
        hi = jnp.where(nonneg, jnp.where(cnt_pos < top_k, 1, key_max + 1), 0)

        def halve_values(_, state):
            lo_v, hi_v, cnt_lo = state
            mid = lo_v + (hi_v - lo_v) * 0.5
            mid = jnp.where(mid > lo_v, jnp.where(mid < hi_v, mid, lo_v), lo_v)
            cnt = count_ge(mid)
            ok = cnt >= top_k
            return jnp.where(ok, mid, lo_v), jnp.where(ok, hi_v, mid), jnp.where(ok, cnt, cnt_lo)

        def halve_keys(_, state):
            lo, hi, cnt_lo = state
            mid = lo + lax.shift_right_logical(hi - lo, 1)
            cnt = count_ge(_key_to_f32(mid))
            ok = cnt >= top_k
            return jnp.where(ok, mid, lo), jnp.where(ok, hi, mid), jnp.where(ok, cnt, cnt_lo)

        def finished(state):
            lo, hi, cnt_lo = state
            done = jnp.where(cnt_lo == top_k, 1, jnp.where(hi - lo <= 1, 1, 0))
            return jnp.min(done) == 1

        lo_v, hi_v, cnt_lo = lax.fori_loop(0, VALUE_PASSES, halve_values,
                                           (_key_to_f32(lo), _key_to_f32(hi), cnt_lo))
        state = (_f32_to_key(lo_v), _f32_to_key(hi_v), cnt_lo)
        stops = (0,) + SEARCH_CHECKS + (31,)
        state = lax.fori_loop(stops[0], stops[1], halve_keys, state)
        for first, last in zip(stops[1:-1], stops[2:]):
            state = lax.cond(finished(state), lambda s: s,
                             functools.partial(lax.fori_loop, first, last, halve_keys), state)
        lo, _, cnt_ge = state
        thr = _key_to_f32(lo)
        thr_scr[...] = thr
        tie_scr[...] = jnp.full((1, TQB), 2 ** 30, jnp.int32)

        @pl.when(jnp.max(cnt_ge) > top_k)
        def _():
            need = (top_k - count_ge(thr, strict=True)).astype(_F32)
            lower = (lax.broadcasted_iota(jnp.int32, (KCHUNK, KCHUNK), 0)
                     >= lax.broadcasted_iota(jnp.int32, (KCHUNK, KCHUNK), 1))
            lower = jnp.where(lower, 1.0, 0.0).astype(_BF16)

            def prefix_body(c, state):
                before, below = state
                for half in range(KSUPER // KCHUNK):
                    r0 = pl.multiple_of(c * KSUPER + half * KCHUNK, KCHUNK)
                    tied = jnp.where(keys_scr[pl.ds(r0, KCHUNK), :] == thr, 1.0, 0.0).astype(_BF16)
                    upto = jnp.dot(lower, tied, preferred_element_type=_F32) + before
                    below = below + jnp.sum(jnp.where(upto < need, 1, 0).reshape(KCHUNK // 8, 8, TQB), axis=0)
                    before = upto[KCHUNK - 1:KCHUNK, :]
                return before, below

            _, below = lax.fori_loop(0, ns, prefix_body,
                                     (jnp.zeros((1, TQB), _F32), jnp.zeros((8, TQB), jnp.int32)))
            tie_scr[...] = jnp.sum(below, axis=0, keepdims=True)

    thr = thr_scr[...]
    tie = tie_scr[...]
    row = lax.broadcasted_iota(jnp.int32, (KCHUNK, TQB), 0)

    zeros = jnp.zeros((HEAD_DIM, TQB), _BF16)
    n_pairs = B_HEADS // 2
    rhs = []
    for p in range(n_pairs):
        q0 = qb_ref[0, (2 * p) * HEAD_DIM:(2 * p + 1) * HEAD_DIM, :]
        q1 = qb_ref[0, (2 * p + 1) * HEAD_DIM:(2 * p + 2) * HEAD_DIM, :]
        rhs.append(jnp.concatenate([jnp.concatenate([q0, zeros], axis=0),
                                    jnp.concatenate([zeros, q1], axis=0)], axis=1))

    def score_step(c):
        r0 = pl.multiple_of(c * KCHUNK, KCHUNK)
        k = keys_scr[pl.ds(r0, KCHUNK), :]
        mask = jnp.where(k == thr, jnp.where(row <= tie - r0, 0.0, NEG), jnp.where(k > thr, 0.0, NEG))
        for p in range(n_pairs):
            s = jnp.dot(kb_ref[0, pl.ds(r0, KCHUNK), p * LANES:(p + 1) * LANES], rhs[p],
                        preferred_element_type=_F32)
            for hh in range(2):
                h = 2 * p + hh
                parts = []
                for half in range(KCHUNK // TQ):
                    jb = c * (KCHUNK // TQ) + half
                    subs = []
                    for sub in range(TQB // TQ):
                        kind = jnp.clip(jb - (i * (TQB // TQ) + sub) + 2, 0, 2)
                        lanes = slice(sub * TQ, (sub + 1) * TQ)
                        subs.append(s[half * TQ:(half + 1) * TQ, hh * TQB + sub * TQ:hh * TQB + (sub + 1) * TQ]
                                    + bias_ref[h, kind] + mask[half * TQ:(half + 1) * TQ, lanes])
                    parts.append(jnp.concatenate(subs, axis=1))
                sh = jnp.concatenate(parts, axis=0)
                sc_scr[pl.ds(r0, KCHUNK), h * TQB:(h + 1) * TQB] = sh
                m_scr[h * 8:(h + 1) * 8, :] = jnp.maximum(m_scr[h * 8:(h + 1) * 8, :], _colmax8(sh))

    m_scr[...] = jnp.full(m_scr.shape, NEG, _F32)
    _for_chunks(nc, score_step)
    m_row = jnp.concatenate(
        [jnp.max(m_scr[h * 8:(h + 1) * 8, :], axis=0, keepdims=True) for h in range(B_HEADS)],
        axis=1)
    acc_scr[...] = jnp.zeros_like(acc_scr)

    def pv_step(c):
        r0 = pl.multiple_of(c * KCHUNK, KCHUNK)
        for p in range(n_pairs):
            cols = slice(2 * p * TQB, (2 * p + 2) * TQB)
            pr = jnp.exp2(sc_scr[pl.ds(r0, KCHUNK), cols] - m_row[:, cols]).astype(_BF16)
            acc_scr[p] += jnp.dot(vb_ref[0, c, p * PAIR_ROWS:(p + 1) * PAIR_ROWS, :], pr,
                                  preferred_element_type=_F32)

    _for_chunks(nc, pv_step)
    for p in range(n_pairs):
        inv = 1.0 / acc_scr[p, 2 * HEAD_DIM:2 * HEAD_DIM + 1, :]
        for hh in range(2):
            rows = slice((2 * p + hh) * HEAD_DIM, (2 * p + hh + 1) * HEAD_DIM)
            o = (acc_scr[p, hh * HEAD_DIM:(hh + 1) * HEAD_DIM, hh * TQB:(hh + 1) * TQB]
                 * inv[:, hh * TQB:(hh + 1) * TQB] * zb_ref[0, rows, :].astype(_F32))
            o_ref[0, rows, :] = o.astype(_BF16)


def _mix_b(qb_t, kb_n, vb_c, zb_t, qi_t, ki_n, wi_t, bias_b):
    B, _, S = qb_t.shape
    nq = S // TQB
    per_q = lambda rows: pl.BlockSpec((1, rows, TQB), lambda b, i: (b, 0, i))
    in_specs = [
        per_q(B_WIDTH),
        pl.BlockSpec((1, S, B_WIDTH), lambda b, i: (b, 0, 0)),
        pl.BlockSpec((1, S // KCHUNK, V_ROWS, KCHUNK), lambda b, i: (b, 0, 0, 0)),
        per_q(B_WIDTH),
        per_q(IDX_WIDTH),
        pl.BlockSpec((1, S, LANES), lambda b, i: (b, 0, 0)),
        per_q(IDX_HEADS),
        pl.BlockSpec((B_HEADS, 3, TQ, TQ), lambda b, i: (0, 0, 0, 0)),
    ]
    scratch = [
        pltpu.VMEM((S, TQB), _F32),
        pltpu.VMEM((S, B_HEADS * TQB), _F32),
        pltpu.VMEM((B_HEADS // 2, PAIR_ROWS, 2 * TQB), _F32),
        pltpu.VMEM((B_HEADS * 8, TQB), _F32),
        pltpu.VMEM((32, TQB), _F32),
        pltpu.VMEM((1, TQB), _F32),
        pltpu.VMEM((1, TQB), jnp.int32),
    ]
    return pl.pallas_call(
        _mix_b_kernel, grid=(B, nq), in_specs=in_specs,
        out_specs=per_q(B_WIDTH),
        out_shape=jax.ShapeDtypeStruct((B, B_WIDTH, S), _BF16),
        scratch_shapes=scratch,
        compiler_params=pltpu.CompilerParams(
            dimension_semantics=("parallel", "arbitrary"), vmem_limit_bytes=VMEM_LIMIT),
        name="mixer_b",
    )(qb_t, kb_n, vb_c, zb_t, qi_t, ki_n, wi_t, bias_b)


def _merge_kernel(x_ref, ya_ref, yb_ref, gt_ref, wpa_ref, wpb_ref, wo_ref, o_ref):
    pa = jnp.dot(wpa_ref[...], ya_ref[0], preferred_element_type=_F32)
    pb = jnp.dot(wpb_ref[...], yb_ref[0], preferred_element_type=_F32)
    merged = (gt_ref[0, :D_MODEL, :].astype(_F32) * pa
              + gt_ref[0, D_MODEL:, :].astype(_F32) * pb).astype(_BF16)
    out_t = jnp.dot(wo_ref[...], merged, preferred_element_type=_F32)
    o_ref[0] = x_ref[0] + out_t.T


def _merge(x, ya_t, yb_t, gt_t, wpa_t, wpb_t, wo_t):
    B, S, D = x.shape
    tm = TM_PROJ
    const = lambda b, t: (0, 0)
    in_specs = [
        pl.BlockSpec((1, tm, D), lambda b, t: (b, t, 0)),
        pl.BlockSpec((1, A_WIDTH, tm), lambda b, t: (b, 0, t)),
        pl.BlockSpec((1, B_WIDTH, tm), lambda b, t: (b, 0, t)),
        pl.BlockSpec((1, 2 * D, tm), lambda b, t: (b, 0, t)),
        pl.BlockSpec((D, A_WIDTH), const),
        pl.BlockSpec((D, B_WIDTH), const),
        pl.BlockSpec((D, D), const),
    ]
    return pl.pallas_call(
        _merge_kernel, grid=(B, S // tm), in_specs=in_specs,
        out_specs=pl.BlockSpec((1, tm, D), lambda b, t: (b, t, 0)),
        out_shape=jax.ShapeDtypeStruct((B, S, D), x.dtype),
        compiler_params=pltpu.CompilerParams(
            dimension_semantics=("parallel", "parallel"), vmem_limit_bytes=VMEM_LIMIT),
        name="merge_out",
    )(x, ya_t, yb_t, gt_t, wpa_t, wpb_t, wo_t)


def _t5_bucket(n):
    n = np.maximum(n, 0)
    max_exact = N_BUCKETS // 2
    nf = np.maximum(n, 1).astype(np.float64)
    large = max_exact + np.floor(np.log(nf / max_exact) / math.log(MAX_DISTANCE / max_exact)
                                 * (N_BUCKETS - max_exact)).astype(np.int64)
    large = np.minimum(large, N_BUCKETS - 1)
    return np.where(n < max_exact, n, large)


def _bias_tables(rel_bias):
    t = np.arange(TQ)[None, :]
    s = np.arange(TQ)[:, None]
    d_prev = t + TQ - s
    d_cur = t - s
    table_a = rel_bias[:, :A_Q_HEADS].astype(_F32).T * LOG2E
    table_b = rel_bias[:, A_Q_HEADS:].astype(_F32).T * LOG2E

    def tile(table, dist, windowed):
        onehot = (_t5_bucket(dist)[..., None] == np.arange(N_BUCKETS)).astype(np.float32)
        b = jnp.einsum("stk,hk->hst", onehot, table, precision=lax.Precision.HIGHEST)
        if windowed:
            b = jnp.where(((dist >= 0) & (dist < WINDOW))[None], b, NEG)
        return b

    bias_a = jnp.concatenate([tile(table_a, d_prev, True), tile(table_a, d_cur, True)], axis=1)
    far = jnp.broadcast_to(table_b[:, N_BUCKETS - 1][:, None, None], (B_HEADS, TQ, TQ))
    bias_b = jnp.stack([far, tile(table_b, d_prev, False), tile(table_b, d_cur, False)], axis=1)
    return bias_a, bias_b


def kernel(x, norm_g, w_in, qnorm_a, knorm_a, sinks_a, qnorm_b, knorm_b, rel_bias,
           w_proj_a, w_proj_b, w_out):
    assert norm_g.shape[0] == 1, "single-layer block"
    B, S, D = x.shape
    assert D == D_MODEL and S % TM_PROJ == 0 and S % KSUPER == 0 and S % TQB == 0 and TQB % TQ == 0
    assert MAX_DISTANCE <= TQ

    wt = w_in[0].T
    n_real = _SEG["kw"][0] + KW_REAL
    wt = jnp.concatenate(
        [wt[:n_real], jnp.zeros((_SEG["kw"][1] - n_real, D), wt.dtype), wt[n_real:]], axis=0)
    wt = wt.astype(_BF16)
    bcast = lambda g, scale: jnp.broadcast_to((g.astype(_F32) * scale)[:, None], (HEAD_DIM, TM_PROJ))
    q_scale = HEAD_DIM ** -0.5
    (qa_t, ka_n, va_t, za_t, qb_t, kb_n, vb_c, zb_t, qi_t, ki_n, wi_t, gt_t) = _inproj(
        x, norm_g[0][None, :].astype(_F32), wt,
        bcast(qnorm_a[0], q_scale * LOG2E), bcast(knorm_a[0], 1.0),
        bcast(qnorm_b[0], q_scale * LOG2E), bcast(knorm_b[0], 1.0))

    bias_a, bias_b = _bias_tables(rel_bias)
    sink_a = jnp.broadcast_to((sinks_a[0].astype(_F32) * LOG2E)[:, None], (A_Q_HEADS, TQ))
    ya_t = _mix_a(qa_t, ka_n, va_t, za_t, bias_a, sink_a)
    yb_t = _mix_b(qb_t, kb_n, vb_c, zb_t, qi_t, ki_n, wi_t, bias_b)
    return _merge(x, ya_t, yb_t, gt_t,
                  w_proj_a[0].T.astype(_BF16), w_proj_b[0].T.astype(_BF16), w_out[0].T.astype(_BF16))
```

```python
import functools
import math

import jax
import jax.numpy as jnp
import numpy as np
from jax import lax
from jax.experimental import pallas as pl
from jax.experimental.pallas import tpu as pltpu

D_MODEL = 1024
HEAD_DIM = 64
A_Q_HEADS = 8
A_KV_HEADS = 2
A_GROUPS = A_Q_HEADS // A_KV_HEADS
A_WIDTH = A_Q_HEADS * HEAD_DIM
A_KV_WIDTH = A_KV_HEADS * HEAD_DIM
B_HEADS = 8
B_WIDTH = B_HEADS * HEAD_DIM
IDX_HEADS = 8
IDX_DIM = 32
IDX_WIDTH = IDX_HEADS * IDX_DIM
WINDOW = 128
TOPK_MAX = 256
N_BUCKETS = 32
MAX_DISTANCE = 128
RMS_EPS = 1e-6

LANES = 128
TQ = 128
A_SUB = 4
TQB = 256
KCHUNK = 256
KSUPER = 2 * KCHUNK
assert KCHUNK == TQB
VALUE_PASSES = 16
SEARCH_CHECKS = (1, 3, 5, 8)
TM_PROJ = 1024
NEG = -1e30
LOG2E = math.log2(math.e)
ONES_ROWS = 16
PAIR_ROWS = 2 * HEAD_DIM + ONES_ROWS
V_ROWS = (B_HEADS // 2) * PAIR_ROWS
VMEM_LIMIT = 56 * 1024 * 1024

_SEG = {}
_off = 0
for _name, _rows in (("qa", A_WIDTH), ("ka", A_KV_WIDTH), ("va", A_KV_WIDTH), ("za", A_WIDTH),
                     ("qb", B_WIDTH), ("kb", B_WIDTH), ("vb", B_WIDTH), ("zb", B_WIDTH),
                     ("qi", IDX_WIDTH), ("kw", LANES), ("gates", 2 * D_MODEL)):
    _SEG[_name] = (_off, _off + _rows)
    _off += _rows
PROJ_ROWS = _off
KW_REAL = IDX_DIM + IDX_HEADS

_F32 = jnp.float32
_BF16 = jnp.bfloat16
_NT = (((1,), (1,)), ((), ()))


def _inproj_kernel(x_ref, g_ref, wt_ref, gqa_ref, gka_ref, gqb_ref, gkb_ref,
                   qa_o, ka_o, va_o, za_o, qb_o, kb_o, vb_o, zb_o, qi_o, ki_o, wi_o, gt_o):
    tm = x_ref.shape[1]
    x = x_ref[0]
    ms = jnp.mean(x * x, axis=-1, keepdims=True)
    h = (x * lax.rsqrt(ms + RMS_EPS) * g_ref[...]).astype(_BF16)

    def proj(lo, hi):
        return lax.dot_general(wt_ref[lo:hi, :], h, _NT, preferred_element_type=_F32)

    def seg(name):
        return proj(*_SEG[name])

    def headnorm(p, gain_ref):
        nh = p.shape[0] // HEAD_DIM
        p3 = p.reshape(nh, HEAD_DIM, tm)
        r = lax.rsqrt(jnp.mean(p3 * p3, axis=1, keepdims=True) + RMS_EPS)
        return (p3 * r * gain_ref[...][None]).reshape(nh * HEAD_DIM, tm)

    def silu(p):
        return p * jax.nn.sigmoid(p)

    qa_o[0] = headnorm(seg("qa"), gqa_ref).astype(_BF16)
    ka_o[0] = headnorm(seg("ka"), gka_ref).T.astype(_BF16)
    va_o[0] = seg("va").astype(_BF16)
    za_o[0] = silu(seg("za")).astype(_BF16)
    qb_o[0] = headnorm(seg("qb"), gqb_ref).astype(_BF16)
    kb_o[0] = headnorm(seg("kb"), gkb_ref).T.astype(_BF16)
    vb = seg("vb").astype(_BF16)
    ones = jnp.ones((ONES_ROWS, KCHUNK), _BF16)
    for c in range(tm // KCHUNK):
        for p in range(B_HEADS // 2):
            vb_o[0, c, p * PAIR_ROWS:(p + 1) * PAIR_ROWS, :] = jnp.concatenate(
                [vb[p * LANES:(p + 1) * LANES, c * KCHUNK:(c + 1) * KCHUNK], ones], axis=0)
    zb_o[0] = silu(seg("zb")).astype(_BF16)
    qi_o[0] = seg("qi").astype(_BF16)
    kw = seg("kw")
    wi_o[0] = kw[IDX_DIM:KW_REAL, :]
    row = lax.broadcasted_iota(jnp.int32, kw.shape, 0)
    ki_o[0] = jnp.where(row < IDX_DIM, kw, 0.0).T.astype(_BF16)
    g_lo = _SEG["gates"][0]
    for c in range(2 * D_MODEL // 512):
        gt_o[0, c * 512:(c + 1) * 512, :] = jax.nn.sigmoid(
            proj(g_lo + c * 512, g_lo + (c + 1) * 512)).astype(_BF16)


def _inproj(x, g, wt, gqa, gka, gqb, gkb):
    B, S, D = x.shape
    tm = TM_PROJ
    nt = S // tm
    const = lambda b, t: (0, 0)
    tr = lambda rows: pl.BlockSpec((1, rows, tm), lambda b, t: (b, 0, t))
    nat = lambda cols: pl.BlockSpec((1, tm, cols), lambda b, t: (b, t, 0))
    sds = jax.ShapeDtypeStruct
    out_shape = (
        sds((B, A_WIDTH, S), _BF16), sds((B, S, A_KV_WIDTH), _BF16), sds((B, A_KV_WIDTH, S), _BF16),
        sds((B, A_WIDTH, S), _BF16),
        sds((B, B_WIDTH, S), _BF16), sds((B, S, B_WIDTH), _BF16),
        sds((B, S // KCHUNK, V_ROWS, KCHUNK), _BF16), sds((B, B_WIDTH, S), _BF16),
        sds((B, IDX_WIDTH, S), _BF16), sds((B, S, LANES), _BF16), sds((B, IDX_HEADS, S), _F32),
        sds((B, 2 * D_MODEL, S), _BF16),
    )
    out_specs = (
        tr(A_WIDTH), nat(A_KV_WIDTH), tr(A_KV_WIDTH), tr(A_WIDTH),
        tr(B_WIDTH), nat(B_WIDTH),
        pl.BlockSpec((1, tm // KCHUNK, V_ROWS, KCHUNK), lambda b, t: (b, t, 0, 0)), tr(B_WIDTH),
        tr(IDX_WIDTH), nat(LANES), tr(IDX_HEADS), tr(2 * D_MODEL),
    )
    in_specs = [
        pl.BlockSpec((1, tm, D), lambda b, t: (b, t, 0)),
        pl.BlockSpec((1, D), const),
        pl.BlockSpec((PROJ_ROWS, D), const, pipeline_mode=pl.Buffered(1)),
        pl.BlockSpec((HEAD_DIM, tm), const), pl.BlockSpec((HEAD_DIM, tm), const),
        pl.BlockSpec((HEAD_DIM, tm), const), pl.BlockSpec((HEAD_DIM, tm), const),
    ]
    return pl.pallas_call(
        _inproj_kernel, grid=(B, nt), in_specs=in_specs, out_specs=out_specs, out_shape=out_shape,
        compiler_params=pltpu.CompilerParams(
            dimension_semantics=("parallel", "parallel"), vmem_limit_bytes=VMEM_LIMIT,
            allow_input_fusion=[False, False] + [True] * 5),
        name="inproj",
    )(x, g, wt, gqa, gka, gqb, gkb)


def _mix_a_kernel(q_ref, kp_ref, kc_ref, vp_ref, vc_ref, z_ref, bias_ref, sink_ref, o_ref):
    i = pl.program_id(1)
    zeros = jnp.zeros((HEAD_DIM, TQ), _BF16)
    units = [(j, g) for j in range(A_SUB) for g in range(A_KV_HEADS)]

    def bands(j):
        cur = slice(j * TQ, (j + 1) * TQ)
        old = slice((j - 1) * TQ, j * TQ)
        k_old = kp_ref[0] if j == 0 else kc_ref[0, old, :]
        v_old = vp_ref[0] if j == 0 else vc_ref[0, :, old]
        return (jnp.concatenate([k_old, kc_ref[0, cur, :]], axis=0),
                jnp.concatenate([v_old, vc_ref[0, :, cur]], axis=1))

    scores = []
    for j, g in units:
        cur = slice(j * TQ, (j + 1) * TQ)
        cols = []
        for hh in range(A_GROUPS):
            h = g * A_GROUPS + hh
            parts = [zeros] * A_KV_HEADS
            parts[g] = q_ref[0, h * HEAD_DIM:(h + 1) * HEAD_DIM, cur]
            cols.append(jnp.concatenate(parts, axis=0))
        rhs = jnp.concatenate(cols, axis=1)
        scores.append(jnp.dot(bands(j)[0], rhs, preferred_element_type=_F32))

    pts, invs = [], []
    for (j, g), sc in zip(units, scores):
        probs, inv = [], []
        for hh in range(A_GROUPS):
            h = g * A_GROUPS + hh
            s = sc[:, hh * TQ:(hh + 1) * TQ] + bias_ref[h]
            if j == 0:
                s = jnp.concatenate([jnp.where(i > 0, s[:TQ], NEG), s[TQ:]], axis=0)
            sink = sink_ref[h:h + 1, :]
            m = jnp.maximum(jnp.max(s, axis=0, keepdims=True), sink)
            p = jnp.exp2(s - m)
            denom = jnp.sum(p, axis=0, keepdims=True) + jnp.exp2(sink - m)
            probs.append(p.astype(_BF16))
            inv.append(1.0 / denom)
        pts.append(jnp.concatenate(probs, axis=1))
        invs.append(inv)

    for (j, g), pt, inv in zip(units, pts, invs):
        cur = slice(j * TQ, (j + 1) * TQ)
        out = jnp.dot(bands(j)[1][g * HEAD_DIM:(g + 1) * HEAD_DIM, :], pt,
                      preferred_element_type=_F32)
        for hh in range(A_GROUPS):
            h = g * A_GROUPS + hh
            rows = slice(h * HEAD_DIM, (h + 1) * HEAD_DIM)
            o = out[:, hh * TQ:(hh + 1) * TQ] * inv[hh] * z_ref[0, rows, cur].astype(_F32)
            o_ref[0, rows, cur] = o.astype(_BF16)


def _mix_a(qa_t, ka_n, va_t, za_t, bias_a, sink_a):
    B, _, S = qa_t.shape
    tqa = A_SUB * TQ
    prev = lambda i: jnp.maximum(i * A_SUB - 1, 0)
    in_specs = [
        pl.BlockSpec((1, A_WIDTH, tqa), lambda b, i: (b, 0, i)),
        pl.BlockSpec((1, TQ, A_KV_WIDTH), lambda b, i: (b, prev(i), 0)),
        pl.BlockSpec((1, tqa, A_KV_WIDTH), lambda b, i: (b, i, 0)),
        pl.BlockSpec((1, A_KV_WIDTH, TQ), lambda b, i: (b, 0, prev(i))),
        pl.BlockSpec((1, A_KV_WIDTH, tqa), lambda b, i: (b, 0, i)),
        pl.BlockSpec((1, A_WIDTH, tqa), lambda b, i: (b, 0, i)),
        pl.BlockSpec((A_Q_HEADS, 2 * TQ, TQ), lambda b, i: (0, 0, 0)),
        pl.BlockSpec((A_Q_HEADS, TQ), lambda b, i: (0, 0)),
    ]
    return pl.pallas_call(
        _mix_a_kernel, grid=(B, S // tqa), in_specs=in_specs,
        out_specs=pl.BlockSpec((1, A_WIDTH, tqa), lambda b, i: (b, 0, i)),
        out_shape=jax.ShapeDtypeStruct((B, A_WIDTH, S), _BF16),
        compiler_params=pltpu.CompilerParams(
            dimension_semantics=("parallel", "parallel"), vmem_limit_bytes=VMEM_LIMIT),
        name="mixer_a",
    )(qa_t, ka_n, ka_n, va_t, va_t, za_t, bias_a, sink_a)


def _key_to_f32(key):
    return pltpu.bitcast(jnp.where(key < 0, key ^ jnp.int32(0x7FFFFFFF), key), _F32)


def _f32_to_key(v):
    bits = pltpu.bitcast(v, jnp.int32)
    return jnp.where(bits < 0, bits ^ jnp.int32(0x7FFFFFFF), bits)


def _colsum8(v):
    return jnp.sum(v.reshape(v.shape[0] // 8, 8, v.shape[1]), axis=0)


def _colmin8(v):
    return jnp.min(v.reshape(v.shape[0] // 8, 8, v.shape[1]), axis=0)


def _colmax8(v):
    return jnp.max(v.reshape(v.shape[0] // 8, 8, v.shape[1]), axis=0)


def _for_chunks(n, step):
    def body(t, carry):
        for u in range(4):
            step(4 * t + u)
        return carry

    lax.fori_loop(0, n // 4, body, 0)
    base = (n // 4) * 4
    for rest in (1, 2, 3):
        @pl.when(n - base == rest)
        def _(rest=rest):
            for u in range(rest):
                step(base + u)


def _for_chunks_marking_last(n, step):
    trips = (n - 1) // 4

    def body(t, carry):
        for u in range(4):
            step(4 * t + u, False)
        return carry

    lax.fori_loop(0, trips, body, 0)
    base = trips * 4
    for rest in (1, 2, 3, 4):
        @pl.when(n - base == rest)
        def _(rest=rest):
            for u in range(rest):
                step(base + u, u == rest - 1)


def _mix_b_kernel(qb_ref, kb_ref, vb_ref, zb_ref, qi_ref, ki_ref, wi_ref, bias_ref, o_ref,
                  keys_scr, sc_scr, acc_scr, m_scr, ext_scr, thr_scr, tie_scr):
    i = pl.program_id(1)
    nc = ((i + 1) * TQB + KCHUNK - 1) // KCHUNK
    ns = (nc + 1) // 2
    top_k = TOPK_MAX

    qi = qi_ref[0]
    zpad = jnp.zeros((LANES - IDX_DIM, TQB), _BF16)
    rhs_i = jnp.concatenate(
        [jnp.concatenate([qi[h * IDX_DIM:(h + 1) * IDX_DIM], zpad], axis=0)
         for h in range(IDX_HEADS)], axis=1)
    w = wi_ref[0] * (IDX_DIM ** -0.5 * IDX_HEADS ** -0.5)
    s_minus_t = (lax.broadcasted_iota(jnp.int32, (KCHUNK, TQB), 0)
                 - lax.broadcasted_iota(jnp.int32, (KCHUNK, TQB), 1))

    def index_step(c, diagonal=False):
        r0 = pl.multiple_of(c * KCHUNK, KCHUNK)
        d = jnp.dot(ki_ref[0, pl.ds(r0, KCHUNK), :], rhs_i,
                    preferred_element_type=_F32)
        acc = w[0:1, :] * jnp.maximum(d[:, 0:TQB], 0.0)
        for h in range(1, IDX_HEADS):
            acc = acc + w[h:h + 1, :] * jnp.maximum(d[:, h * TQB:(h + 1) * TQB], 0.0)
        if diagonal:
            causal = s_minus_t <= (i * TQB - r0)
            score = jnp.where(causal, acc, -jnp.inf)
            lowest = jnp.where(causal, acc, jnp.inf)
        else:
            score = lowest = acc
        keys_scr[pl.ds(r0, KCHUNK), :] = score
        ext_scr[0:8, :] = jnp.maximum(ext_scr[0:8, :], _colmax8(score))
        ext_scr[8:16, :] = jnp.minimum(ext_scr[8:16, :], _colmin8(lowest))
        ext_scr[16:24, :] += _colsum8(jnp.where(score >= 0.0, 1.0, 0.0))
        ext_scr[24:32, :] += _colsum8(jnp.where(score > 0.0, 1.0, 0.0))

    ext_scr[0:8, :] = jnp.full((8, TQB), -jnp.inf, _F32)
    ext_scr[8:16, :] = jnp.full((8, TQB), jnp.inf, _F32)
    ext_scr[16:32, :] = jnp.zeros((16, TQB), _F32)
    _for_chunks_marking_last(nc, index_step)

    @pl.when(nc % 2 == 1)
    def _():
        keys_scr[pl.ds(pl.multiple_of(nc * KCHUNK, KCHUNK), KCHUNK), :] = jnp.full(
            (KCHUNK, TQB), -jnp.inf, _F32)

    def count_rows(src, hit):
        n_acc = 4
        left, right = slice(0, TQ), slice(TQ, TQB)

        def bump(accs, r0, n_rows, diagonal):
            accs = list(accs)
            rows = src[pl.ds(r0, n_rows), :]
            n_full = (n_rows - (KCHUNK - TQ) if diagonal else n_rows) // 8
            for j in range(n_rows // 8):
                a = accs[j % n_acc]
                blk = rows[j * 8:(j + 1) * 8]
                if j < n_full:
                    accs[j % n_acc] = jnp.where(hit(blk, slice(0, TQB)), a + 1, a)
                else:
                    accs[j % n_acc] = jnp.concatenate(
                        [a[:, left], jnp.where(hit(blk[:, right], right), a[:, right] + 1, a[:, right])], axis=1)
            return tuple(accs)

        accs = lax.fori_loop(
            0, (nc - 1) // 2, lambda c, accs: bump(accs, pl.multiple_of(c * KSUPER, KSUPER), KSUPER, False),
            tuple(jnp.zeros((8, TQB), jnp.int32) for _ in range(n_acc)))
        accs = lax.cond(nc % 2 == 1,
                        lambda accs: bump(accs, pl.multiple_of((nc - 1) * KCHUNK, KCHUNK), KCHUNK, True),
                        lambda accs: bump(accs, pl.multiple_of((nc - 2) * KCHUNK, KSUPER), KSUPER, True), accs)
        acc = accs[0]
        for a in accs[1:]:
            acc = acc + a
        return jnp.sum(acc, axis=0, keepdims=True)

    def count_ge(cand, strict=False):
        return count_rows(keys_scr, (lambda r, lanes: r > cand[:, lanes]) if strict
                          else (lambda r, lanes: r >= cand[:, lanes]))

    @pl.when(i * TQB + TQB <= top_k)
    def _():
        thr_scr[...] = jnp.full((1, TQB), -jnp.inf, _F32)
        tie_scr[...] = jnp.full((1, TQB), -1, jnp.int32)

    @pl.when(i * TQB + TQB > top_k)
    def _():
        cnt_nonneg = jnp.sum(ext_scr[16:24, :], axis=0, keepdims=True).astype(jnp.int32)
        cnt_pos = jnp.sum(ext_scr[24:32, :], axis=0, keepdims=True).astype(jnp.int32)
        key_max = _f32_to_key(jnp.max(ext_scr[0:8, :], axis=0, keepdims=True))
        key_min = _f32_to_key(jnp.min(ext_scr[8:16, :], axis=0, keepdims=True))
        n_causal = i * TQB + 1 + lax.broadcasted_iota(jnp.int32, (1, TQB), 1)
        nonneg = cnt_nonneg >= top_k
        lo = jnp.where(nonneg, 0, key_min)
        cnt_lo = jnp.where(nonneg, cnt_nonneg, n_causal)
        hi = jnp.where(nonneg, jnp.where(cnt_pos < top_k, 1, key_max + 1), 0)

        def halve_values(_, state):
            lo_v, hi_v, cnt_lo = state
            mid = lo_v + (hi_v - lo_v) * 0.5
            mid = jnp.where(mid > lo_v, jnp.where(mid < hi_v, mid, lo_v), lo_v)
            cnt = count_ge(mid)
            ok = cnt >= top_k
            return jnp.where(ok, mid, lo_v), jnp.where(ok, hi_v, mid), jnp.where(ok, cnt, cnt_lo)

        def halve_keys(_, state):
            lo, hi, cnt_lo = state
            mid = lo + lax.shift_right_logical(hi - lo, 1)
            cnt = count_ge(_key_to_f32(mid))
            ok = cnt >= top_k
            return jnp.where(ok, mid, lo), jnp.where(ok, hi, mid), jnp.where(ok, cnt, cnt_lo)

        def finished(state):
            lo, hi, cnt_lo = state
            done = jnp.where(cnt_lo == top_k, 1, jnp.where(hi - lo <= 1, 1, 0))
            return jnp.min(done) == 1

        lo_v, hi_v, cnt_lo = lax.fori_loop(0, VALUE_PASSES, halve_values,
                                           (_key_to_f32(lo), _key_to_f32(hi), cnt_lo))
        state = (_f32_to_key(lo_v), _f32_to_key(hi_v), cnt_lo)
        stops = (0,) + SEARCH_CHECKS + (31,)
        state = lax.fori_loop(stops[0], stops[1], halve_keys, state)
        for first, last in zip(stops[1:-1], stops[2:]):
            state = lax.cond(finished(state), lambda s: s,
                             functools.partial(lax.fori_loop, first, last, halve_keys), state)
        lo, _, cnt_ge = state
        thr = _key_to_f32(lo)
        thr_scr[...] = thr
        tie_scr[...] = jnp.full((1, TQB), 2 ** 30, jnp.int32)

        @pl.when(jnp.max(cnt_ge) > top_k)
        def _():
            need = (top_k - count_ge(thr, strict=True)).astype(_F32)
            lower = (lax.broadcasted_iota(jnp.int32, (KCHUNK, KCHUNK), 0)
                     >= lax.broadcasted_iota(jnp.int32, (KCHUNK, KCHUNK), 1))
            lower = jnp.where(lower, 1.0, 0.0).astype(_BF16)

            def prefix_body(c, state):
                before, below = state
                for half in range(KSUPER // KCHUNK):
                    r0 = pl.multiple_of(c * KSUPER + half * KCHUNK, KCHUNK)
                    tied = jnp.where(keys_scr[pl.ds(r0, KCHUNK), :] == thr, 1.0, 0.0).astype(_BF16)
                    upto = jnp.dot(lower, tied, preferred_element_type=_F32) + before
                    below = below + jnp.sum(jnp.where(upto < need, 1, 0).reshape(KCHUNK // 8, 8, TQB), axis=0)
                    before = upto[KCHUNK - 1:KCHUNK, :]
                return before, below

            _, below = lax.fori_loop(0, ns, prefix_body,
                                     (jnp.zeros((1, TQB), _F32), jnp.zeros((8, TQB), jnp.int32)))
            tie_scr[...] = jnp.sum(below, axis=0, keepdims=True)

    thr = thr_scr[...]
    tie = tie_scr[...]
    row = lax.broadcasted_iota(jnp.int32, (KCHUNK, TQB), 0)

    zeros = jnp.zeros((HEAD_DIM, TQB), _BF16)
    n_pairs = B_HEADS // 2
    rhs = []
    for p in range(n_pairs):
        q0 = qb_ref[0, (2 * p) * HEAD_DIM:(2 * p + 1) * HEAD_DIM, :]
        q1 = qb_ref[0, (2 * p + 1) * HEAD_DIM:(2 * p + 2) * HEAD_DIM, :]
        rhs.append(jnp.concatenate([jnp.concatenate([q0, zeros], axis=0),
                                    jnp.concatenate([zeros, q1], axis=0)], axis=1))

    def score_step(c):
        r0 = pl.multiple_of(c * KCHUNK, KCHUNK)
        k = keys_scr[pl.ds(r0, KCHUNK), :]
        mask = jnp.where(k == thr, jnp.where(row <= tie - r0, 0.0, NEG), jnp.where(k > thr, 0.0, NEG))
        for p in range(n_pairs):
            s = jnp.dot(kb_ref[0, pl.ds(r0, KCHUNK), p * LANES:(p + 1) * LANES], rhs[p],
                        preferred_element_type=_F32)
            for hh in range(2):
                h = 2 * p + hh
                parts = []
                for half in range(KCHUNK // TQ):
                    jb = c * (KCHUNK // TQ) + half
                    subs = []
                    for sub in range(TQB // TQ):
                        kind = jnp.clip(jb - (i * (TQB // TQ) + sub) + 2, 0, 2)
                        lanes = slice(sub * TQ, (sub + 1) * TQ)
                        subs.append(s[half * TQ:(half + 1) * TQ, hh * TQB + sub * TQ:hh * TQB + (sub + 1) * TQ]
                                    + bias_ref[h, kind] + mask[half * TQ:(half + 1) * TQ, lanes])
                    parts.append(jnp.concatenate(subs, axis=1))
                sh = jnp.concatenate(parts, axis=0)
                sc_scr[pl.ds(r0, KCHUNK), h * TQB:(h + 1) * TQB] = sh
                m_scr[h * 8:(h + 1) * 8, :] = jnp.maximum(m_scr[h * 8:(h + 1) * 8, :], _colmax8(sh))

    m_scr[...] = jnp.full(m_scr.shape, NEG, _F32)
    _for_chunks(nc, score_step)
    m_row = jnp.concatenate(
        [jnp.max(m_scr[h * 8:(h + 1) * 8, :], axis=0, keepdims=True) for h in range(B_HEADS)],
        axis=1)
    acc_scr[...] = jnp.zeros_like(acc_scr)

    def pv_step(c):
        r0 = pl.multiple_of(c * KCHUNK, KCHUNK)
        for p in range(n_pairs):
            cols = slice(2 * p * TQB, (2 * p + 2) * TQB)
            pr = jnp.exp2(sc_scr[pl.ds(r0, KCHUNK), cols] - m_row[:, cols]).astype(_BF16)
            acc_scr[p] += jnp.dot(vb_ref[0, c, p * PAIR_ROWS:(p + 1) * PAIR_ROWS, :], pr,
                                  preferred_element_type=_F32)

    _for_chunks(nc, pv_step)
    for p in range(n_pairs):
        inv = 1.0 / acc_scr[p, 2 * HEAD_DIM:2 * HEAD_DIM + 1, :]
        for hh in range(2):
            rows = slice((2 * p + hh) * HEAD_DIM, (2 * p + hh + 1) * HEAD_DIM)
            o = (acc_scr[p, hh * HEAD_DIM:(hh + 1) * HEAD_DIM, hh * TQB:(hh + 1) * TQB]
                 * inv[:, hh * TQB:(hh + 1) * TQB] * zb_ref[0, rows, :].astype(_F32))
            o_ref[0, rows, :] = o.astype(_BF16)


def _mix_b(qb_t, kb_n, vb_c, zb_t, qi_t, ki_n, wi_t, bias_b):
    B, _, S = qb_t.shape
    nq = S // TQB
    per_q = lambda rows: pl.BlockSpec((1, rows, TQB), lambda b, i: (b, 0, i))
    in_specs = [
        per_q(B_WIDTH),
        pl.BlockSpec((1, S, B_WIDTH), lambda b, i: (b, 0, 0)),
        pl.BlockSpec((1, S // KCHUNK, V_ROWS, KCHUNK), lambda b, i: (b, 0, 0, 0)),
        per_q(B_WIDTH),
        per_q(IDX_WIDTH),
        pl.BlockSpec((1, S, LANES), lambda b, i: (b, 0, 0)),
        per_q(IDX_HEADS),
        pl.BlockSpec((B_HEADS, 3, TQ, TQ), lambda b, i: (0, 0, 0, 0)),
    ]
    scratch = [
        pltpu.VMEM((S, TQB), _F32),
        pltpu.VMEM((S, B_HEADS * TQB), _F32),
        pltpu.VMEM((B_HEADS // 2, PAIR_ROWS, 2 * TQB), _F32),
        pltpu.VMEM((B_HEADS * 8, TQB), _F32),
        pltpu.VMEM((32, TQB), _F32),
        pltpu.VMEM((1, TQB), _F32),
        pltpu.VMEM((1, TQB), jnp.int32),
    ]
    return pl.pallas_call(
        _mix_b_kernel, grid=(B, nq), in_specs=in_specs,
        out_specs=per_q(B_WIDTH),
        out_shape=jax.ShapeDtypeStruct((B, B_WIDTH, S), _BF16),
        scratch_shapes=scratch,
        compiler_params=pltpu.CompilerParams(
            dimension_semantics=("parallel", "arbitrary"), vmem_limit_bytes=VMEM_LIMIT),
        name="mixer_b",
    )(qb_t, kb_n, vb_c, zb_t, qi_t, ki_n, wi_t, bias_b)


def _merge_kernel(x_ref, ya_ref, yb_ref, gt_ref, wpa_ref, wpb_ref, wo_ref, o_ref):
    pa = jnp.dot(wpa_ref[...], ya_ref[0], preferred_element_type=_F32)
    pb = jnp.dot(wpb_ref[...], yb_ref[0], preferred_element_type=_F32)
    merged = (gt_ref[0, :D_MODEL, :].astype(_F32) * pa
              + gt_ref[0, D_MODEL:, :].astype(_F32) * pb).astype(_BF16)
    out_t = jnp.dot(wo_ref[...], merged, preferred_element_type=_F32)
    o_ref[0] = x_ref[0] + out_t.T


def _merge(x, ya_t, yb_t, gt_t, wpa_t, wpb_t, wo_t):
    B, S, D = x.shape
    tm = TM_PROJ
    const = lambda b, t: (0, 0)
    in_specs = [
        pl.BlockSpec((1, tm, D), lambda b, t: (b, t, 0)),
        pl.BlockSpec((1, A_WIDTH, tm), lambda b, t: (b, 0, t)),
        pl.BlockSpec((1, B_WIDTH, tm), lambda b, t: (b, 0, t)),
        pl.BlockSpec((1, 2 * D, tm), lambda b, t: (b, 0, t)),
        pl.BlockSpec((D, A_WIDTH), const),
        pl.BlockSpec((D, B_WIDTH), const),
        pl.BlockSpec((D, D), const),
    ]
    return pl.pallas_call(
        _merge_kernel, grid=(B, S // tm), in_specs=in_specs,
        out_specs=pl.BlockSpec((1, tm, D), lambda b, t: (b, t, 0)),
        out_shape=jax.ShapeDtypeStruct((B, S, D), x.dtype),
        compiler_params=pltpu.CompilerParams(
            dimension_semantics=("parallel", "parallel"), vmem_limit_bytes=VMEM_LIMIT,
            allow_input_fusion=[False] * 4 + [True] * 3),
        name="merge_out",
    )(x, ya_t, yb_t, gt_t, wpa_t, wpb_t, wo_t)


def _t5_bucket(n):
    n = np.maximum(n, 0)
    max_exact = N_BUCKETS // 2
    nf = np.maximum(n, 1).astype(np.float64)
    large = max_exact + np.floor(np.log(nf / max_exact) / math.log(MAX_DISTANCE / max_exact)
                                 * (N_BUCKETS - max_exact)).astype(np.int64)
    large = np.minimum(large, N_BUCKETS - 1)
    return np.where(n < max_exact, n, large)


def _bias_tables(rel_bias):
    t = np.arange(TQ)[None, :]
    s = np.arange(TQ)[:, None]
    d_prev = t + TQ - s
    d_cur = t - s
    table_a = rel_bias[:, :A_Q_HEADS].astype(_F32).T * LOG2E
    table_b = rel_bias[:, A_Q_HEADS:].astype(_F32).T * LOG2E

    def tile(table, dist, windowed):
        onehot = (_t5_bucket(dist)[..., None] == np.arange(N_BUCKETS)).astype(np.float32)
        b = jnp.einsum("stk,hk->hst", onehot, table, precision=lax.Precision.HIGHEST)
        if windowed:
            b = jnp.where(((dist >= 0) & (dist < WINDOW))[None], b, NEG)
        return b

    bias_a = jnp.concatenate([tile(table_a, d_prev, True), tile(table_a, d_cur, True)], axis=1)
    far = jnp.broadcast_to(table_b[:, N_BUCKETS - 1][:, None, None], (B_HEADS, TQ, TQ))
    bias_b = jnp.stack([far, tile(table_b, d_prev, False), tile(table_b, d_cur, False)], axis=1)
    return bias_a, bias_b


def kernel(x, norm_g, w_in, qnorm_a, knorm_a, sinks_a, qnorm_b, knorm_b, rel_bias,
           w_proj_a, w_proj_b, w_out):
    assert norm_g.shape[0] == 1, "single-layer block"
    B, S, D = x.shape
    assert D == D_MODEL and S % TM_PROJ == 0 and S % KSUPER == 0 and S % TQB == 0 and TQB % TQ == 0
    assert MAX_DISTANCE <= TQ

    wt = w_in[0].T
    n_real = _SEG["kw"][0] + KW_REAL
    wt = jnp.concatenate(
        [wt[:n_real], jnp.zeros((_SEG["kw"][1] - n_real, D), wt.dtype), wt[n_real:]], axis=0)
    wt = wt.astype(_BF16)
    bcast = lambda g, scale: jnp.broadcast_to((g.astype(_F32) * scale)[:, None], (HEAD_DIM, TM_PROJ))
    q_scale = HEAD_DIM ** -0.5
    (qa_t, ka_n, va_t, za_t, qb_t, kb_n, vb_c, zb_t, qi_t, ki_n, wi_t, gt_t) = _inproj(
        x, norm_g[0][None, :].astype(_F32), wt,
        bcast(qnorm_a[0], q_scale * LOG2E), bcast(knorm_a[0], 1.0),
        bcast(qnorm_b[0], q_scale * LOG2E), bcast(knorm_b[0], 1.0))

    bias_a, bias_b = _bias_tables(rel_bias)
    sink_a = jnp.broadcast_to((sinks_a[0].astype(_F32) * LOG2E)[:, None], (A_Q_HEADS, TQ))
    ya_t = _mix_a(qa_t, ka_n, va_t, za_t, bias_a, sink_a)
    yb_t = _mix_b(qb_t, kb_n, vb_c, zb_t, qi_t, ki_n, wi_t, bias_b)
    return _merge(x, ya_t, yb_t, gt_t,
                  w_proj_a[0].T.astype(_BF16), w_proj_b[0].T.astype(_BF16), w_out[0].T.astype(_BF16))
```

```python
import functools
import math

import jax
import jax.numpy as jnp
import numpy as np
from jax import lax
from jax.experimental import pallas as pl
from jax.experimental.pallas import tpu as pltpu

D_MODEL = 1024
HEAD_DIM = 64
A_Q_HEADS = 8
A_KV_HEADS = 2
A_GROUPS = A_Q_HEADS // A_KV_HEADS
A_WIDTH = A_Q_HEADS * HEAD_DIM
A_KV_WIDTH = A_KV_HEADS * HEAD_DIM
B_HEADS = 8
B_WIDTH = B_HEADS * HEAD_DIM
IDX_HEADS = 8
IDX_DIM = 32
IDX_WIDTH = IDX_HEADS * IDX_DIM
WINDOW = 128
TOPK_MAX = 256
N_BUCKETS = 32
MAX_DISTANCE = 128
RMS_EPS = 1e-6

LANES = 128
TQ = 128
A_SUB = 8
TQB = 256
KCHUNK = 256
KSUPER = 2 * KCHUNK
assert KCHUNK == TQB
VALUE_PASSES = 16
SEARCH_CHECKS = (1, 3, 5, 8)
TM_PROJ = 1024
NEG = -1e30
LOG2E = math.log2(math.e)
ONES_ROWS = 16
PAIR_ROWS = 2 * HEAD_DIM + ONES_ROWS
V_ROWS = (B_HEADS // 2) * PAIR_ROWS
VMEM_LIMIT = 56 * 1024 * 1024

_SEG = {}
_off = 0
for _name, _rows in (("qa", A_WIDTH), ("ka", A_KV_WIDTH), ("va", A_KV_WIDTH), ("za", A_WIDTH),
                     ("qb", B_WIDTH), ("kb", B_WIDTH), ("vb", B_WIDTH), ("zb", B_WIDTH),
                     ("qi", IDX_WIDTH), ("kw", LANES), ("gates", 2 * D_MODEL)):
    _SEG[_name] = (_off, _off + _rows)
    _off += _rows
PROJ_ROWS = _off
KW_REAL = IDX_DIM + IDX_HEADS

_F32 = jnp.float32
_BF16 = jnp.bfloat16
_NT = (((1,), (1,)), ((), ()))


def _inproj_kernel(x_ref, g_ref, wt_ref, gqa_ref, gka_ref, gqb_ref, gkb_ref,
                   qa_o, ka_o, va_o, za_o, qb_o, kb_o, vb_o, zb_o, qi_o, ki_o, wi_o, gt_o):
    tm = x_ref.shape[1]
    x = x_ref[0]
    ms = jnp.mean(x * x, axis=-1, keepdims=True)
    h = (x * lax.rsqrt(ms + RMS_EPS) * g_ref[...]).astype(_BF16)

    def proj(lo, hi):
        return lax.dot_general(wt_ref[lo:hi, :], h, _NT, preferred_element_type=_F32)

    def seg(name):
        return proj(*_SEG[name])

    def headnorm(p, gain_ref):
        nh = p.shape[0] // HEAD_DIM
        p3 = p.reshape(nh, HEAD_DIM, tm)
        r = lax.rsqrt(jnp.mean(p3 * p3, axis=1, keepdims=True) + RMS_EPS)
        return (p3 * r * gain_ref[...][None]).reshape(nh * HEAD_DIM, tm)

    def silu(p):
        return p * jax.nn.sigmoid(p)

    qa_o[0] = headnorm(seg("qa"), gqa_ref).astype(_BF16)
    ka_o[0] = headnorm(seg("ka"), gka_ref).T.astype(_BF16)
    va_o[0] = seg("va").astype(_BF16)
    za_o[0] = silu(seg("za")).astype(_BF16)
    qb_o[0] = headnorm(seg("qb"), gqb_ref).astype(_BF16)
    kb_o[0] = headnorm(seg("kb"), gkb_ref).T.astype(_BF16)
    vb = seg("vb").astype(_BF16)
    ones = jnp.ones((ONES_ROWS, KCHUNK), _BF16)
    for c in range(tm // KCHUNK):
        for p in range(B_HEADS // 2):
            vb_o[0, c, p * PAIR_ROWS:(p + 1) * PAIR_ROWS, :] = jnp.concatenate(
                [vb[p * LANES:(p + 1) * LANES, c * KCHUNK:(c + 1) * KCHUNK], ones], axis=0)
    zb_o[0] = silu(seg("zb")).astype(_BF16)
    qi_o[0] = seg("qi").astype(_BF16)
    kw = seg("kw")
    wi_o[0] = kw[IDX_DIM:KW_REAL, :]
    row = lax.broadcasted_iota(jnp.int32, kw.shape, 0)
    ki_o[0] = jnp.where(row < IDX_DIM, kw, 0.0).T.astype(_BF16)
    g_lo = _SEG["gates"][0]
    for c in range(2 * D_MODEL // 512):
        gt_o[0, c * 512:(c + 1) * 512, :] = jax.nn.sigmoid(
            proj(g_lo + c * 512, g_lo + (c + 1) * 512)).astype(_BF16)


def _inproj(x, g, wt, gqa, gka, gqb, gkb):
    B, S, D = x.shape
    tm = TM_PROJ
    nt = S // tm
    const = lambda b, t: (0, 0)
    tr = lambda rows: pl.BlockSpec((1, rows, tm), lambda b, t: (b, 0, t))
    nat = lambda cols: pl.BlockSpec((1, tm, cols), lambda b, t: (b, t, 0))
    sds = jax.ShapeDtypeStruct
    out_shape = (
        sds((B, A_WIDTH, S), _BF16), sds((B, S, A_KV_WIDTH), _BF16), sds((B, A_KV_WIDTH, S), _BF16),
        sds((B, A_WIDTH, S), _BF16),
        sds((B, B_WIDTH, S), _BF16), sds((B, S, B_WIDTH), _BF16),
        sds((B, S // KCHUNK, V_ROWS, KCHUNK), _BF16), sds((B, B_WIDTH, S), _BF16),
        sds((B, IDX_WIDTH, S), _BF16), sds((B, S, LANES), _BF16), sds((B, IDX_HEADS, S), _F32),
        sds((B, 2 * D_MODEL, S), _BF16),
    )
    out_specs = (
        tr(A_WIDTH), nat(A_KV_WIDTH), tr(A_KV_WIDTH), tr(A_WIDTH),
        tr(B_WIDTH), nat(B_WIDTH),
        pl.BlockSpec((1, tm // KCHUNK, V_ROWS, KCHUNK), lambda b, t: (b, t, 0, 0)), tr(B_WIDTH),
        tr(IDX_WIDTH), nat(LANES), tr(IDX_HEADS), tr(2 * D_MODEL),
    )
    in_specs = [
        pl.BlockSpec((1, tm, D), lambda b, t: (b, t, 0)),
        pl.BlockSpec((1, D), const),
        pl.BlockSpec((PROJ_ROWS, D), const, pipeline_mode=pl.Buffered(1)),
        pl.BlockSpec((HEAD_DIM, tm), const), pl.BlockSpec((HEAD_DIM, tm), const),
        pl.BlockSpec((HEAD_DIM, tm), const), pl.BlockSpec((HEAD_DIM, tm), const),
    ]
    return pl.pallas_call(
        _inproj_kernel, grid=(B, nt), in_specs=in_specs, out_specs=out_specs, out_shape=out_shape,
        compiler_params=pltpu.CompilerParams(
            dimension_semantics=("parallel", "parallel"), vmem_limit_bytes=VMEM_LIMIT),
        name="inproj",
    )(x, g, wt, gqa, gka, gqb, gkb)


def _mix_a_kernel(q_ref, kp_ref, kc_ref, vp_ref, vc_ref, z_ref, bias_ref, sink_ref, o_ref):
    i = pl.program_id(1)
    zeros = jnp.zeros((HEAD_DIM, TQ), _BF16)
    units = [(j, g) for j in range(A_SUB) for g in range(A_KV_HEADS)]

    def bands(j):
        cur = slice(j * TQ, (j + 1) * TQ)
        old = slice((j - 1) * TQ, j * TQ)
        k_old = kp_ref[0] if j == 0 else kc_ref[0, old, :]
        v_old = vp_ref[0] if j == 0 else vc_ref[0, :, old]
        return (jnp.concatenate([k_old, kc_ref[0, cur, :]], axis=0),
                jnp.concatenate([v_old, vc_ref[0, :, cur]], axis=1))

    scores = []
    for j, g in units:
        cur = slice(j * TQ, (j + 1) * TQ)
        cols = []
        for hh in range(A_GROUPS):
            h = g * A_GROUPS + hh
            parts = [zeros] * A_KV_HEADS
            parts[g] = q_ref[0, h * HEAD_DIM:(h + 1) * HEAD_DIM, cur]
            cols.append(jnp.concatenate(parts, axis=0))
        rhs = jnp.concatenate(cols, axis=1)
        scores.append(jnp.dot(bands(j)[0], rhs, preferred_element_type=_F32))

    pts, invs = [], []
    for (j, g), sc in zip(units, scores):
        probs, inv = [], []
        for hh in range(A_GROUPS):
            h = g * A_GROUPS + hh
            s = sc[:, hh * TQ:(hh + 1) * TQ] + bias_ref[h]
            if j == 0:
                s = jnp.concatenate([jnp.where(i > 0, s[:TQ], NEG), s[TQ:]], axis=0)
            sink = sink_ref[h:h + 1, :]
            m = jnp.maximum(jnp.max(s, axis=0, keepdims=True), sink)
            p = jnp.exp2(s - m)
            denom = jnp.sum(p, axis=0, keepdims=True) + jnp.exp2(sink - m)
            probs.append(p.astype(_BF16))
            inv.append(1.0 / denom)
        pts.append(jnp.concatenate(probs, axis=1))
        invs.append(inv)

    for (j, g), pt, inv in zip(units, pts, invs):
        cur = slice(j * TQ, (j + 1) * TQ)
        out = jnp.dot(bands(j)[1][g * HEAD_DIM:(g + 1) * HEAD_DIM, :], pt,
                      preferred_element_type=_F32)
        for hh in range(A_GROUPS):
            h = g * A_GROUPS + hh
            rows = slice(h * HEAD_DIM, (h + 1) * HEAD_DIM)
            o = out[:, hh * TQ:(hh + 1) * TQ] * inv[hh] * z_ref[0, rows, cur].astype(_F32)
            o_ref[0, rows, cur] = o.astype(_BF16)


def _mix_a(qa_t, ka_n, va_t, za_t, bias_a, sink_a):
    B, _, S = qa_t.shape
    tqa = A_SUB * TQ
    prev = lambda i: jnp.maximum(i * A_SUB - 1, 0)
    in_specs = [
        pl.BlockSpec((1, A_WIDTH, tqa), lambda b, i: (b, 0, i)),
        pl.BlockSpec((1, TQ, A_KV_WIDTH), lambda b, i: (b, prev(i), 0)),
        pl.BlockSpec((1, tqa, A_KV_WIDTH), lambda b, i: (b, i, 0)),
        pl.BlockSpec((1, A_KV_WIDTH, TQ), lambda b, i: (b, 0, prev(i))),
        pl.BlockSpec((1, A_KV_WIDTH, tqa), lambda b, i: (b, 0, i)),
        pl.BlockSpec((1, A_WIDTH, tqa), lambda b, i: (b, 0, i)),
        pl.BlockSpec((A_Q_HEADS, 2 * TQ, TQ), lambda b, i: (0, 0, 0)),
        pl.BlockSpec((A_Q_HEADS, TQ), lambda b, i: (0, 0)),
    ]
    return pl.pallas_call(
        _mix_a_kernel, grid=(B, S // tqa), in_specs=in_specs,
        out_specs=pl.BlockSpec((1, A_WIDTH, tqa), lambda b, i: (b, 0, i)),
        out_shape=jax.ShapeDtypeStruct((B, A_WIDTH, S), _BF16),
        compiler_params=pltpu.CompilerParams(
            dimension_semantics=("parallel", "parallel"), vmem_limit_bytes=VMEM_LIMIT),
        name="mixer_a",
    )(qa_t, ka_n, ka_n, va_t, va_t, za_t, bias_a, sink_a)


def _key_to_f32(key):
    return pltpu.bitcast(jnp.where(key < 0, key ^ jnp.int32(0x7FFFFFFF), key), _F32)


def _f32_to_key(v):
    bits = pltpu.bitcast(v, jnp.int32)
    return jnp.where(bits < 0, bits ^ jnp.int32(0x7FFFFFFF), bits)


def _colsum8(v):
    return jnp.sum(v.reshape(v.shape[0] // 8, 8, v.shape[1]), axis=0)


def _colmin8(v):
    return jnp.min(v.reshape(v.shape[0] // 8, 8, v.shape[1]), axis=0)


def _colmax8(v):
    return jnp.max(v.reshape(v.shape[0] // 8, 8, v.shape[1]), axis=0)


def _for_chunks(n, step):
    def body(t, carry):
        for u in range(4):
            step(4 * t + u)
        return carry

    lax.fori_loop(0, n // 4, body, 0)
    base = (n // 4) * 4
    for rest in (1, 2, 3):
        @pl.when(n - base == rest)
        def _(rest=rest):
            for u in range(rest):
                step(base + u)


def _for_chunks_marking_last(n, step):
    trips = (n - 1) // 4

    def body(t, carry):
        for u in range(4):
            step(4 * t + u, False)
        return carry

    lax.fori_loop(0, trips, body, 0)
    base = trips * 4
    for rest in (1, 2, 3, 4):
        @pl.when(n - base == rest)
        def _(rest=rest):
            for u in range(rest):
                step(base + u, u == rest - 1)


def _mix_b_kernel(qb_ref, kb_ref, vb_ref, zb_ref, qi_ref, ki_ref, wi_ref, bias_ref, o_ref,
                  keys_scr, sc_scr, acc_scr, m_scr, ext_scr, thr_scr, tie_scr):
    i = pl.program_id(1)
    nc = ((i + 1) * TQB + KCHUNK - 1) // KCHUNK
    ns = (nc + 1) // 2
    top_k = TOPK_MAX

    qi = qi_ref[0]
    zpad = jnp.zeros((LANES - IDX_DIM, TQB), _BF16)
    rhs_i = jnp.concatenate(
        [jnp.concatenate([qi[h * IDX_DIM:(h + 1) * IDX_DIM], zpad], axis=0)
         for h in range(IDX_HEADS)], axis=1)
    w = wi_ref[0] * (IDX_DIM ** -0.5 * IDX_HEADS ** -0.5)
    s_minus_t = (lax.broadcasted_iota(jnp.int32, (KCHUNK, TQB), 0)
                 - lax.broadcasted_iota(jnp.int32, (KCHUNK, TQB), 1))

    def index_step(c, diagonal=False):
        r0 = pl.multiple_of(c * KCHUNK, KCHUNK)
        d = jnp.dot(ki_ref[0, pl.ds(r0, KCHUNK), :], rhs_i,
                    preferred_element_type=_F32)
        acc = w[0:1, :] * jnp.maximum(d[:, 0:TQB], 0.0)
        for h in range(1, IDX_HEADS):
            acc = acc + w[h:h + 1, :] * jnp.maximum(d[:, h * TQB:(h + 1) * TQB], 0.0)
        if diagonal:
            causal = s_minus_t <= (i * TQB - r0)
            score = jnp.where(causal, acc, -jnp.inf)
            lowest = jnp.where(causal, acc, jnp.inf)
        else:
            score = lowest = acc
        keys_scr[pl.ds(r0, KCHUNK), :] = score
        ext_scr[0:8, :] = jnp.maximum(ext_scr[0:8, :], _colmax8(score))
        ext_scr[8:16, :] = jnp.minimum(ext_scr[8:16, :], _colmin8(lowest))
        ext_scr[16:24, :] += _colsum8(jnp.where(score >= 0.0, 1.0, 0.0))
        ext_scr[24:32, :] += _colsum8(jnp.where(score > 0.0, 1.0, 0.0))

    ext_scr[0:8, :] = jnp.full((8, TQB), -jnp.inf, _F32)
    ext_scr[8:16, :] = jnp.full((8, TQB), jnp.inf, _F32)
    ext_scr[16:32, :] = jnp.zeros((16, TQB), _F32)
    _for_chunks_marking_last(nc, index_step)

    @pl.when(nc % 2 == 1)
    def _():
        keys_scr[pl.ds(pl.multiple_of(nc * KCHUNK, KCHUNK), KCHUNK), :] = jnp.full(
            (KCHUNK, TQB), -jnp.inf, _F32)

    def count_rows(src, hit):
        n_acc = 4
        left, right = slice(0, TQ), slice(TQ, TQB)

        def bump(accs, r0, n_rows, diagonal):
            accs = list(accs)
            rows = src[pl.ds(r0, n_rows), :]
            n_full = (n_rows - (KCHUNK - TQ) if diagonal else n_rows) // 8
            for j in range(n_rows // 8):
                a = accs[j % n_acc]
                blk = rows[j * 8:(j + 1) * 8]
                if j < n_full:
                    accs[j % n_acc] = jnp.where(hit(blk, slice(0, TQB)), a + 1, a)
                else:
                    accs[j % n_acc] = jnp.concatenate(
                        [a[:, left], jnp.where(hit(blk[:, right], right), a[:, right] + 1, a[:, right])], axis=1)
            return tuple(accs)

        accs = lax.fori_loop(
            0, (nc - 1) // 2, lambda c, accs: bump(accs, pl.multiple_of(c * KSUPER, KSUPER), KSUPER, False),
            tuple(jnp.zeros((8, TQB), jnp.int32) for _ in range(n_acc)))
        accs = lax.cond(nc % 2 == 1,
                        lambda accs: bump(accs, pl.multiple_of((nc - 1) * KCHUNK, KCHUNK), KCHUNK, True),
                        lambda accs: bump(accs, pl.multiple_of((nc - 2) * KCHUNK, KSUPER), KSUPER, True), accs)
        acc = accs[0]
        for a in accs[1:]:
            acc = acc + a
        return jnp.sum(acc, axis=0, keepdims=True)

    def count_ge(cand, strict=False):
        return count_rows(keys_scr, (lambda r, lanes: r > cand[:, lanes]) if strict
                          else (lambda r, lanes: r >= cand[:, lanes]))

    @pl.when(i * TQB + TQB <= top_k)
    def _():
        thr_scr[...] = jnp.full((1, TQB), -jnp.inf, _F32)
        tie_scr[...] = jnp.full((1, TQB), -1, jnp.int32)

    @pl.when(i * TQB + TQB > top_k)
    def _():
        cnt_nonneg = jnp.sum(ext_scr[16:24, :], axis=0, keepdims=True).astype(jnp.int32)
        cnt_pos = jnp.sum(ext_scr[24:32, :], axis=0, keepdims=True).astype(jnp.int32)
        key_max = _f32_to_key(jnp.max(ext_scr[0:8, :], axis=0, keepdims=True))
        key_min = _f32_to_key(jnp.min(ext_scr[8:16, :], axis=0, keepdims=True))
        n_causal = i * TQB + 1 + lax.broadcasted_iota(jnp.int32, (1, TQB), 1)
        nonneg = cnt_nonneg >= top_k
        lo = jnp.where(nonneg, 0, key_min)
        cnt_lo = jnp.where(nonneg, cnt_nonneg, n_causal)
        hi = jnp.where(nonneg, jnp.where(cnt_pos < top_k, 1, key_max + 1), 0)

        def halve_values(_, state):
            lo_v, hi_v, cnt_lo = state
            mid = lo_v + (hi_v - lo_v) * 0.5
            mid = jnp.where(mid > lo_v, jnp.where(mid < hi_v, mid, lo_v), lo_v)
            cnt = count_ge(mid)
            ok = cnt >= top_k
            return jnp.where(ok, mid, lo_v), jnp.where(ok, hi_v, mid), jnp.where(ok, cnt, cnt_lo)

        def halve_keys(_, state):
            lo, hi, cnt_lo = state
            mid = lo + lax.shift_right_logical(hi - lo, 1)
            cnt = count_ge(_key_to_f32(mid))
            ok = cnt >= top_k
            return jnp.where(ok, mid, lo), jnp.where(ok, hi, mid), jnp.where(ok, cnt, cnt_lo)

        def finished(state):
            lo, hi, cnt_lo = state
            done = jnp.where(cnt_lo == top_k, 1, jnp.where(hi - lo <= 1, 1, 0))
            return jnp.min(done) == 1

        lo_v, hi_v, cnt_lo = lax.fori_loop(0, VALUE_PASSES, halve_values,
                                           (_key_to_f32(lo), _key_to_f32(hi), cnt_lo))
        state = (_f32_to_key(lo_v), _f32_to_key(hi_v), cnt_lo)
        stops = (0,) + SEARCH_CHECKS + (31,)
        state = lax.fori_loop(stops[0], stops[1], halve_keys, state)
        for first, last in zip(stops[1:-1], stops[2:]):
            state = lax.cond(finished(state), lambda s: s,
                             functools.partial(lax.fori_loop, first, last, halve_keys), state)
        lo, _, cnt_ge = state
        thr = _key_to_f32(lo)
        thr_scr[...] = thr
        tie_scr[...] = jnp.full((1, TQB), 2 ** 30, jnp.int32)

        @pl.when(jnp.max(cnt_ge) > top_k)
        def _():
            need = (top_k - count_ge(thr, strict=True)).astype(_F32)
            lower = (lax.broadcasted_iota(jnp.int32, (KCHUNK, KCHUNK), 0)
                     >= lax.broadcasted_iota(jnp.int32, (KCHUNK, KCHUNK), 1))
            lower = jnp.where(lower, 1.0, 0.0).astype(_BF16)

            def prefix_body(c, state):
                before, below = state
                for half in range(KSUPER // KCHUNK):
                    r0 = pl.multiple_of(c * KSUPER + half * KCHUNK, KCHUNK)
                    tied = jnp.where(keys_scr[pl.ds(r0, KCHUNK), :] == thr, 1.0, 0.0).astype(_BF16)
                    upto = jnp.dot(lower, tied, preferred_element_type=_F32) + before
                    below = below + jnp.sum(jnp.where(upto < need, 1, 0).reshape(KCHUNK // 8, 8, TQB), axis=0)
                    before = upto[KCHUNK - 1:KCHUNK, :]
                return before, below

            _, below = lax.fori_loop(0, ns, prefix_body,
                                     (jnp.zeros((1, TQB), _F32), jnp.zeros((8, TQB), jnp.int32)))
            tie_scr[...] = jnp.sum(below, axis=0, keepdims=True)

    thr = thr_scr[...]
    tie = tie_scr[...]
    row = lax.broadcasted_iota(jnp.int32, (KCHUNK, TQB), 0)

    zeros = jnp.zeros((HEAD_DIM, TQB), _BF16)
    n_pairs = B_HEADS // 2
    rhs = []
    for p in range(n_pairs):
        q0 = qb_ref[0, (2 * p) * HEAD_DIM:(2 * p + 1) * HEAD_DIM, :]
        q1 = qb_ref[0, (2 * p + 1) * HEAD_DIM:(2 * p + 2) * HEAD_DIM, :]
        rhs.append(jnp.concatenate([jnp.concatenate([q0, zeros], axis=0),
                                    jnp.concatenate([zeros, q1], axis=0)], axis=1))

    def score_step(c):
        r0 = pl.multiple_of(c * KCHUNK, KCHUNK)
        k = keys_scr[pl.ds(r0, KCHUNK), :]
        mask = jnp.where(k == thr, jnp.where(row <= tie - r0, 0.0, NEG), jnp.where(k > thr, 0.0, NEG))
        for p in range(n_pairs):
            s = jnp.dot(kb_ref[0, pl.ds(r0, KCHUNK), p * LANES:(p + 1) * LANES], rhs[p],
                        preferred_element_type=_F32)
            for hh in range(2):
                h = 2 * p + hh
                parts = []
                for half in range(KCHUNK // TQ):
                    jb = c * (KCHUNK // TQ) + half
                    subs = []
                    for sub in range(TQB // TQ):
                        kind = jnp.clip(jb - (i * (TQB // TQ) + sub) + 2, 0, 2)
                        lanes = slice(sub * TQ, (sub + 1) * TQ)
                        subs.append(s[half * TQ:(half + 1) * TQ, hh * TQB + sub * TQ:hh * TQB + (sub + 1) * TQ]
                                    + bias_ref[h, kind] + mask[half * TQ:(half + 1) * TQ, lanes])
                    parts.append(jnp.concatenate(subs, axis=1))
                sh = jnp.concatenate(parts, axis=0)
                sc_scr[pl.ds(r0, KCHUNK), h * TQB:(h + 1) * TQB] = sh
                m_scr[h * 8:(h + 1) * 8, :] = jnp.maximum(m_scr[h * 8:(h + 1) * 8, :], _colmax8(sh))

    m_scr[...] = jnp.full(m_scr.shape, NEG, _F32)
    _for_chunks(nc, score_step)
    m_row = jnp.concatenate(
        [jnp.max(m_scr[h * 8:(h + 1) * 8, :], axis=0, keepdims=True) for h in range(B_HEADS)],
        axis=1)
    acc_scr[...] = jnp.zeros_like(acc_scr)

    def pv_step(c):
        r0 = pl.multiple_of(c * KCHUNK, KCHUNK)
        for p in range(n_pairs):
            cols = slice(2 * p * TQB, (2 * p + 2) * TQB)
            pr = jnp.exp2(sc_scr[pl.ds(r0, KCHUNK), cols] - m_row[:, cols]).astype(_BF16)
            acc_scr[p] += jnp.dot(vb_ref[0, c, p * PAIR_ROWS:(p + 1) * PAIR_ROWS, :], pr,
                                  preferred_element_type=_F32)

    _for_chunks(nc, pv_step)
    for p in range(n_pairs):
        inv = 1.0 / acc_scr[p, 2 * HEAD_DIM:2 * HEAD_DIM + 1, :]
        for hh in range(2):
            rows = slice((2 * p + hh) * HEAD_DIM, (2 * p + hh + 1) * HEAD_DIM)
            o = (acc_scr[p, hh * HEAD_DIM:(hh + 1) * HEAD_DIM, hh * TQB:(hh + 1) * TQB]
                 * inv[:, hh * TQB:(hh + 1) * TQB] * zb_ref[0, rows, :].astype(_F32))
            o_ref[0, rows, :] = o.astype(_BF16)


def _mix_b(qb_t, kb_n, vb_c, zb_t, qi_t, ki_n, wi_t, bias_b):
    B, _, S = qb_t.shape
    nq = S // TQB
    per_q = lambda rows: pl.BlockSpec((1, rows, TQB), lambda b, i: (b, 0, i))
    in_specs = [
        per_q(B_WIDTH),
        pl.BlockSpec((1, S, B_WIDTH), lambda b, i: (b, 0, 0)),
        pl.BlockSpec((1, S // KCHUNK, V_ROWS, KCHUNK), lambda b, i: (b, 0, 0, 0)),
        per_q(B_WIDTH),
        per_q(IDX_WIDTH),
        pl.BlockSpec((1, S, LANES), lambda b, i: (b, 0, 0)),
        per_q(IDX_HEADS),
        pl.BlockSpec((B_HEADS, 3, TQ, TQ), lambda b, i: (0, 0, 0, 0)),
    ]
    scratch = [
        pltpu.VMEM((S, TQB), _F32),
        pltpu.VMEM((S, B_HEADS * TQB), _F32),
        pltpu.VMEM((B_HEADS // 2, PAIR_ROWS, 2 * TQB), _F32),
        pltpu.VMEM((B_HEADS * 8, TQB), _F32),
        pltpu.VMEM((32, TQB), _F32),
        pltpu.VMEM((1, TQB), _F32),
        pltpu.VMEM((1, TQB), jnp.int32),
    ]
    return pl.pallas_call(
        _mix_b_kernel, grid=(B, nq), in_specs=in_specs,
        out_specs=per_q(B_WIDTH),
        out_shape=jax.ShapeDtypeStruct((B, B_WIDTH, S), _BF16),
        scratch_shapes=scratch,
        compiler_params=pltpu.CompilerParams(
            dimension_semantics=("parallel", "arbitrary"), vmem_limit_bytes=VMEM_LIMIT),
        name="mixer_b",
    )(qb_t, kb_n, vb_c, zb_t, qi_t, ki_n, wi_t, bias_b)


def _merge_kernel(x_ref, ya_ref, yb_ref, gt_ref, wpa_ref, wpb_ref, wo_ref, o_ref):
    pa = jnp.dot(wpa_ref[...], ya_ref[0], preferred_element_type=_F32)
    pb = jnp.dot(wpb_ref[...], yb_ref[0], preferred_element_type=_F32)
    merged = (gt_ref[0, :D_MODEL, :].astype(_F32) * pa
              + gt_ref[0, D_MODEL:, :].astype(_F32) * pb).astype(_BF16)
    out_t = jnp.dot(wo_ref[...], merged, preferred_element_type=_F32)
    o_ref[0] = x_ref[0] + out_t.T


def _merge(x, ya_t, yb_t, gt_t, wpa_t, wpb_t, wo_t):
    B, S, D = x.shape
    tm = TM_PROJ
    const = lambda b, t: (0, 0)
    in_specs = [
        pl.BlockSpec((1, tm, D), lambda b, t: (b, t, 0)),
        pl.BlockSpec((1, A_WIDTH, tm), lambda b, t: (b, 0, t)),
        pl.BlockSpec((1, B_WIDTH, tm), lambda b, t: (b, 0, t)),
        pl.BlockSpec((1, 2 * D, tm), lambda b, t: (b, 0, t)),
        pl.BlockSpec((D, A_WIDTH), const),
        pl.BlockSpec((D, B_WIDTH), const),
        pl.BlockSpec((D, D), const),
    ]
    return pl.pallas_call(
        _merge_kernel, grid=(B, S // tm), in_specs=in_specs,
        out_specs=pl.BlockSpec((1, tm, D), lambda b, t: (b, t, 0)),
        out_shape=jax.ShapeDtypeStruct((B, S, D), x.dtype),
        compiler_params=pltpu.CompilerParams(
            dimension_semantics=("parallel", "parallel"), vmem_limit_bytes=VMEM_LIMIT),
        name="merge_out",
    )(x, ya_t, yb_t, gt_t, wpa_t, wpb_t, wo_t)


def _t5_bucket(n):
    n = np.maximum(n, 0)
    max_exact = N_BUCKETS // 2
    nf = np.maximum(n, 1).astype(np.float64)
    large = max_exact + np.floor(np.log(nf / max_exact) / math.log(MAX_DISTANCE / max_exact)
                                 * (N_BUCKETS - max_exact)).astype(np.int64)
    large = np.minimum(large, N_BUCKETS - 1)
    return np.where(n < max_exact, n, large)


def _bias_tables(rel_bias):
    t = np.arange(TQ)[None, :]
    s = np.arange(TQ)[:, None]
    d_prev = t + TQ - s
    d_cur = t - s
    table_a = rel_bias[:, :A_Q_HEADS].astype(_F32).T * LOG2E
    table_b = rel_bias[:, A_Q_HEADS:].astype(_F32).T * LOG2E

    def tile(table, dist, windowed):
        onehot = (_t5_bucket(dist)[..., None] == np.arange(N_BUCKETS)).astype(np.float32)
        b = jnp.einsum("stk,hk->hst", onehot, table, precision=lax.Precision.HIGHEST)
        if windowed:
            b = jnp.where(((dist >= 0) & (dist < WINDOW))[None], b, NEG)
        return b

    bias_a = jnp.concatenate([tile(table_a, d_prev, True), tile(table_a, d_cur, True)], axis=1)
    far = jnp.broadcast_to(table_b[:, N_BUCKETS - 1][:, None, None], (B_HEADS, TQ, TQ))
    bias_b = jnp.stack([far, tile(table_b, d_prev, False), tile(table_b, d_cur, False)], axis=1)
    return bias_a, bias_b


def kernel(x, norm_g, w_in, qnorm_a, knorm_a, sinks_a, qnorm_b, knorm_b, rel_bias,
           w_proj_a, w_proj_b, w_out):
    assert norm_g.shape[0] == 1, "single-layer block"
    B, S, D = x.shape
    assert D == D_MODEL and S % TM_PROJ == 0 and S % KSUPER == 0 and S % TQB == 0 and TQB % TQ == 0
    assert MAX_DISTANCE <= TQ

    wt = w_in[0].T
    n_real = _SEG["kw"][0] + KW_REAL
    wt = jnp.concatenate(
        [wt[:n_real], jnp.zeros((_SEG["kw"][1] - n_real, D), wt.dtype), wt[n_real:]], axis=0)
    wt = wt.astype(_BF16)
    bcast = lambda g, scale: jnp.broadcast_to((g.astype(_F32) * scale)[:, None], (HEAD_DIM, TM_PROJ))
    q_scale = HEAD_DIM ** -0.5
    (qa_t, ka_n, va_t, za_t, qb_t, kb_n, vb_c, zb_t, qi_t, ki_n, wi_t, gt_t) = _inproj(
        x, norm_g[0][None, :].astype(_F32), wt,
        bcast(qnorm_a[0], q_scale * LOG2E), bcast(knorm_a[0], 1.0),
        bcast(qnorm_b[0], q_scale * LOG2E), bcast(knorm_b[0], 1.0))

    bias_a, bias_b = _bias_tables(rel_bias)
    sink_a = jnp.broadcast_to((sinks_a[0].astype(_F32) * LOG2E)[:, None], (A_Q_HEADS, TQ))
    ya_t = _mix_a(qa_t, ka_n, va_t, za_t, bias_a, sink_a)
    yb_t = _mix_b(qb_t, kb_n, vb_c, zb_t, qi_t, ki_n, wi_t, bias_b)
    return _merge(x, ya_t, yb_t, gt_t,
                  w_proj_a[0].T.astype(_BF16), w_proj_b[0].T.astype(_BF16), w_out[0].T.astype(_BF16))
```

```python
import functools
import math

import jax
import jax.numpy as jnp
import numpy as np
from jax import lax
from jax.experimental import pallas as pl
from jax.experimental.pallas import tpu as pltpu

D_MODEL = 1024
HEAD_DIM = 64
A_Q_HEADS = 8
A_KV_HEADS = 2
A_GROUPS = A_Q_HEADS // A_KV_HEADS
A_WIDTH = A_Q_HEADS * HEAD_DIM
A_KV_WIDTH = A_KV_HEADS * HEAD_DIM
B_HEADS = 8
B_WIDTH = B_HEADS * HEAD_DIM
IDX_HEADS = 8
IDX_DIM = 32
IDX_WIDTH = IDX_HEADS * IDX_DIM
WINDOW = 128
TOPK_MAX = 256
N_BUCKETS = 32
MAX_DISTANCE = 128
RMS_EPS = 1e-6

LANES = 128
TQ = 128
A_SUB = 16
TQB = 256
KCHUNK = 256
KSUPER = 2 * KCHUNK
assert KCHUNK == TQB
VALUE_PASSES = 16
SEARCH_CHECKS = (1, 3, 5, 8)
TM_PROJ = 1024
NEG = -1e30
LOG2E = math.log2(math.e)
ONES_ROWS = 16
PAIR_ROWS = 2 * HEAD_DIM + ONES_ROWS
V_ROWS = (B_HEADS // 2) * PAIR_ROWS
VMEM_LIMIT = 56 * 1024 * 1024

_SEG = {}
_off = 0
for _name, _rows in (("qa", A_WIDTH), ("ka", A_KV_WIDTH), ("va", A_KV_WIDTH), ("za", A_WIDTH),
                     ("qb", B_WIDTH), ("kb", B_WIDTH), ("vb", B_WIDTH), ("zb", B_WIDTH),
                     ("qi", IDX_WIDTH), ("kw", LANES), ("gates", 2 * D_MODEL)):
    _SEG[_name] = (_off, _off + _rows)
    _off += _rows
PROJ_ROWS = _off
KW_REAL = IDX_DIM + IDX_HEADS

_F32 = jnp.float32
_BF16 = jnp.bfloat16
_NT = (((1,), (1,)), ((), ()))


def _inproj_kernel(x_ref, g_ref, wt_ref, gqa_ref, gka_ref, gqb_ref, gkb_ref,
                   qa_o, ka_o, va_o, za_o, qb_o, kb_o, vb_o, zb_o, qi_o, ki_o, wi_o, gt_o):
    tm = x_ref.shape[1]
    x = x_ref[0]
    ms = jnp.mean(x * x, axis=-1, keepdims=True)
    h = (x * lax.rsqrt(ms + RMS_EPS) * g_ref[...]).astype(_BF16)

    def proj(lo, hi):
        return lax.dot_general(wt_ref[lo:hi, :], h, _NT, preferred_element_type=_F32)

    def seg(name):
        return proj(*_SEG[name])

    def headnorm(p, gain_ref):
        nh = p.shape[0] // HEAD_DIM
        p3 = p.reshape(nh, HEAD_DIM, tm)
        r = lax.rsqrt(jnp.mean(p3 * p3, axis=1, keepdims=True) + RMS_EPS)
        return (p3 * r * gain_ref[...][None]).reshape(nh * HEAD_DIM, tm)

    def silu(p):
        return p * jax.nn.sigmoid(p)

    qa_o[0] = headnorm(seg("qa"), gqa_ref).astype(_BF16)
    ka_o[0] = headnorm(seg("ka"), gka_ref).T.astype(_BF16)
    va_o[0] = seg("va").astype(_BF16)
    za_o[0] = silu(seg("za")).astype(_BF16)
    qb_o[0] = headnorm(seg("qb"), gqb_ref).astype(_BF16)
    kb_o[0] = headnorm(seg("kb"), gkb_ref).T.astype(_BF16)
    vb = seg("vb").astype(_BF16)
    ones = jnp.ones((ONES_ROWS, KCHUNK), _BF16)
    for c in range(tm // KCHUNK):
        for p in range(B_HEADS // 2):
            vb_o[0, c, p * PAIR_ROWS:(p + 1) * PAIR_ROWS, :] = jnp.concatenate(
                [vb[p * LANES:(p + 1) * LANES, c * KCHUNK:(c + 1) * KCHUNK], ones], axis=0)
    zb_o[0] = silu(seg("zb")).astype(_BF16)
    qi_o[0] = seg("qi").astype(_BF16)
    kw = seg("kw")
    wi_o[0] = kw[IDX_DIM:KW_REAL, :]
    row = lax.broadcasted_iota(jnp.int32, kw.shape, 0)
    ki_o[0] = jnp.where(row < IDX_DIM, kw, 0.0).T.astype(_BF16)
    g_lo = _SEG["gates"][0]
    for c in range(2 * D_MODEL // 512):
        gt_o[0, c * 512:(c + 1) * 512, :] = jax.nn.sigmoid(
            proj(g_lo + c * 512, g_lo + (c + 1) * 512)).astype(_BF16)


def _inproj(x, g, wt, gqa, gka, gqb, gkb):
    B, S, D = x.shape
    tm = TM_PROJ
    nt = S // tm
    const = lambda b, t: (0, 0)
    tr = lambda rows: pl.BlockSpec((1, rows, tm), lambda b, t: (b, 0, t))
    nat = lambda cols: pl.BlockSpec((1, tm, cols), lambda b, t: (b, t, 0))
    sds = jax.ShapeDtypeStruct
    out_shape = (
        sds((B, A_WIDTH, S), _BF16), sds((B, S, A_KV_WIDTH), _BF16), sds((B, A_KV_WIDTH, S), _BF16),
        sds((B, A_WIDTH, S), _BF16),
        sds((B, B_WIDTH, S), _BF16), sds((B, S, B_WIDTH), _BF16),
        sds((B, S // KCHUNK, V_ROWS, KCHUNK), _BF16), sds((B, B_WIDTH, S), _BF16),
        sds((B, IDX_WIDTH, S), _BF16), sds((B, S, LANES), _BF16), sds((B, IDX_HEADS, S), _F32),
        sds((B, 2 * D_MODEL, S), _BF16),
    )
    out_specs = (
        tr(A_WIDTH), nat(A_KV_WIDTH), tr(A_KV_WIDTH), tr(A_WIDTH),
        tr(B_WIDTH), nat(B_WIDTH),
        pl.BlockSpec((1, tm // KCHUNK, V_ROWS, KCHUNK), lambda b, t: (b, t, 0, 0)), tr(B_WIDTH),
        tr(IDX_WIDTH), nat(LANES), tr(IDX_HEADS), tr(2 * D_MODEL),
    )
    in_specs = [
        pl.BlockSpec((1, tm, D), lambda b, t: (b, t, 0)),
        pl.BlockSpec((1, D), const),
        pl.BlockSpec((PROJ_ROWS, D), const, pipeline_mode=pl.Buffered(1)),
        pl.BlockSpec((HEAD_DIM, tm), const), pl.BlockSpec((HEAD_DIM, tm), const),
        pl.BlockSpec((HEAD_DIM, tm), const), pl.BlockSpec((HEAD_DIM, tm), const),
    ]
    return pl.pallas_call(
        _inproj_kernel, grid=(B, nt), in_specs=in_specs, out_specs=out_specs, out_shape=out_shape,
        compiler_params=pltpu.CompilerParams(
            dimension_semantics=("parallel", "parallel"), vmem_limit_bytes=VMEM_LIMIT),
        name="inproj",
    )(x, g, wt, gqa, gka, gqb, gkb)


def _mix_a_kernel(q_ref, kp_ref, kc_ref, vp_ref, vc_ref, z_ref, bias_ref, sink_ref, o_ref):
    i = pl.program_id(1)
    zeros = jnp.zeros((HEAD_DIM, TQ), _BF16)
    units = [(j, g) for j in range(A_SUB) for g in range(A_KV_HEADS)]

    def bands(j):
        cur = slice(j * TQ, (j + 1) * TQ)
        old = slice((j - 1) * TQ, j * TQ)
        k_old = kp_ref[0] if j == 0 else kc_ref[0, old, :]
        v_old = vp_ref[0] if j == 0 else vc_ref[0, :, old]
        return (jnp.concatenate([k_old, kc_ref[0, cur, :]], axis=0),
                jnp.concatenate([v_old, vc_ref[0, :, cur]], axis=1))

    scores = []
    for j, g in units:
        cur = slice(j * TQ, (j + 1) * TQ)
        cols = []
        for hh in range(A_GROUPS):
            h = g * A_GROUPS + hh
            parts = [zeros] * A_KV_HEADS
            parts[g] = q_ref[0, h * HEAD_DIM:(h + 1) * HEAD_DIM, cur]
            cols.append(jnp.concatenate(parts, axis=0))
        rhs = jnp.concatenate(cols, axis=1)
        scores.append(jnp.dot(bands(j)[0], rhs, preferred_element_type=_F32))

    pts, invs = [], []
    for (j, g), sc in zip(units, scores):
        probs, inv = [], []
        for hh in range(A_GROUPS):
            h = g * A_GROUPS + hh
            s = sc[:, hh * TQ:(hh + 1) * TQ] + bias_ref[h]
            if j == 0:
                s = jnp.concatenate([jnp.where(i > 0, s[:TQ], NEG), s[TQ:]], axis=0)
            sink = sink_ref[h:h + 1, :]
            m = jnp.maximum(jnp.max(s, axis=0, keepdims=True), sink)
            p = jnp.exp2(s - m)
            denom = jnp.sum(p, axis=0, keepdims=True) + jnp.exp2(sink - m)
            probs.append(p.astype(_BF16))
            inv.append(1.0 / denom)
        pts.append(jnp.concatenate(probs, axis=1))
        invs.append(inv)

    for (j, g), pt, inv in zip(units, pts, invs):
        cur = slice(j * TQ, (j + 1) * TQ)
        out = jnp.dot(bands(j)[1][g * HEAD_DIM:(g + 1) * HEAD_DIM, :], pt,
                      preferred_element_type=_F32)
        for hh in range(A_GROUPS):
            h = g * A_GROUPS + hh
            rows = slice(h * HEAD_DIM, (h + 1) * HEAD_DIM)
            o = out[:, hh * TQ:(hh + 1) * TQ] * inv[hh] * z_ref[0, rows, cur].astype(_F32)
            o_ref[0, rows, cur] = o.astype(_BF16)


def _mix_a(qa_t, ka_n, va_t, za_t, bias_a, sink_a):
    B, _, S = qa_t.shape
    tqa = A_SUB * TQ
    prev = lambda i: jnp.maximum(i * A_SUB - 1, 0)
    in_specs = [
        pl.BlockSpec((1, A_WIDTH, tqa), lambda b, i: (b, 0, i)),
        pl.BlockSpec((1, TQ, A_KV_WIDTH), lambda b, i: (b, prev(i), 0)),
        pl.BlockSpec((1, tqa, A_KV_WIDTH), lambda b, i: (b, i, 0)),
        pl.BlockSpec((1, A_KV_WIDTH, TQ), lambda b, i: (b, 0, prev(i))),
        pl.BlockSpec((1, A_KV_WIDTH, tqa), lambda b, i: (b, 0, i)),
        pl.BlockSpec((1, A_WIDTH, tqa), lambda b, i: (b, 0, i)),
        pl.BlockSpec((A_Q_HEADS, 2 * TQ, TQ), lambda b, i: (0, 0, 0)),
        pl.BlockSpec((A_Q_HEADS, TQ), lambda b, i: (0, 0)),
    ]
    return pl.pallas_call(
        _mix_a_kernel, grid=(B, S // tqa), in_specs=in_specs,
        out_specs=pl.BlockSpec((1, A_WIDTH, tqa), lambda b, i: (b, 0, i)),
        out_shape=jax.ShapeDtypeStruct((B, A_WIDTH, S), _BF16),
        compiler_params=pltpu.CompilerParams(
            dimension_semantics=("parallel", "parallel"), vmem_limit_bytes=VMEM_LIMIT),
        name="mixer_a",
    )(qa_t, ka_n, ka_n, va_t, va_t, za_t, bias_a, sink_a)


def _key_to_f32(key):
    return pltpu.bitcast(jnp.where(key < 0, key ^ jnp.int32(0x7FFFFFFF), key), _F32)


def _f32_to_key(v):
    bits = pltpu.bitcast(v, jnp.int32)
    return jnp.where(bits < 0, bits ^ jnp.int32(0x7FFFFFFF), bits)


def _colsum8(v):
    return jnp.sum(v.reshape(v.shape[0] // 8, 8, v.shape[1]), axis=0)


def _colmin8(v):
    return jnp.min(v.reshape(v.shape[0] // 8, 8, v.shape[1]), axis=0)


def _colmax8(v):
    return jnp.max(v.reshape(v.shape[0] // 8, 8, v.shape[1]), axis=0)


def _for_chunks(n, step):
    def body(t, carry):
        for u in range(4):
            step(4 * t + u)
        return carry

    lax.fori_loop(0, n // 4, body, 0)
    base = (n // 4) * 4
    for rest in (1, 2, 3):
        @pl.when(n - base == rest)
        def _(rest=rest):
            for u in range(rest):
                step(base + u)


def _for_chunks_marking_last(n, step):
    trips = (n - 1) // 4

    def body(t, carry):
        for u in range(4):
            step(4 * t + u, False)
        return carry

    lax.fori_loop(0, trips, body, 0)
    base = trips * 4
    for rest in (1, 2, 3, 4):
        @pl.when(n - base == rest)
        def _(rest=rest):
            for u in range(rest):
                step(base + u, u == rest - 1)


def _mix_b_kernel(qb_ref, kb_ref, vb_ref, zb_ref, qi_ref, ki_ref, wi_ref, bias_ref, o_ref,
                  keys_scr, sc_scr, acc_scr, m_scr, ext_scr, thr_scr, tie_scr):
    i = pl.program_id(1)
    nc = ((i + 1) * TQB + KCHUNK - 1) // KCHUNK
    ns = (nc + 1) // 2
    top_k = TOPK_MAX

    qi = qi_ref[0]
    zpad = jnp.zeros((LANES - IDX_DIM, TQB), _BF16)
    rhs_i = jnp.concatenate(
        [jnp.concatenate([qi[h * IDX_DIM:(h + 1) * IDX_DIM], zpad], axis=0)
         for h in range(IDX_HEADS)], axis=1)
    w = wi_ref[0] * (IDX_DIM ** -0.5 * IDX_HEADS ** -0.5)
    s_minus_t = (lax.broadcasted_iota(jnp.int32, (KCHUNK, TQB), 0)
                 - lax.broadcasted_iota(jnp.int32, (KCHUNK, TQB), 1))

    def index_step(c, diagonal=False):
        r0 = pl.multiple_of(c * KCHUNK, KCHUNK)
        d = jnp.dot(ki_ref[0, pl.ds(r0, KCHUNK), :], rhs_i,
                    preferred_element_type=_F32)
        acc = w[0:1, :] * jnp.maximum(d[:, 0:TQB], 0.0)
        for h in range(1, IDX_HEADS):
            acc = acc + w[h:h + 1, :] * jnp.maximum(d[:, h * TQB:(h + 1) * TQB], 0.0)
        if diagonal:
            causal = s_minus_t <= (i * TQB - r0)
            score = jnp.where(causal, acc, -jnp.inf)
            lowest = jnp.where(causal, acc, jnp.inf)
        else:
            score = lowest = acc
        keys_scr[pl.ds(r0, KCHUNK), :] = score
        ext_scr[0:8, :] = jnp.maximum(ext_scr[0:8, :], _colmax8(score))
        ext_scr[8:16, :] = jnp.minimum(ext_scr[8:16, :], _colmin8(lowest))
        ext_scr[16:24, :] += _colsum8(jnp.where(score >= 0.0, 1.0, 0.0))
        ext_scr[24:32, :] += _colsum8(jnp.where(score > 0.0, 1.0, 0.0))

    ext_scr[0:8, :] = jnp.full((8, TQB), -jnp.inf, _F32)
    ext_scr[8:16, :] = jnp.full((8, TQB), jnp.inf, _F32)
    ext_scr[16:32, :] = jnp.zeros((16, TQB), _F32)
    _for_chunks_marking_last(nc, index_step)

    @pl.when(nc % 2 == 1)
    def _():
        keys_scr[pl.ds(pl.multiple_of(nc * KCHUNK, KCHUNK), KCHUNK), :] = jnp.full(
            (KCHUNK, TQB), -jnp.inf, _F32)

    def count_rows(src, hit):
        n_acc = 4
        left, right = slice(0, TQ), slice(TQ, TQB)

        def bump(accs, r0, n_rows, diagonal):
            accs = list(accs)
            rows = src[pl.ds(r0, n_rows), :]
            n_full = (n_rows - (KCHUNK - TQ) if diagonal else n_rows) // 8
            for j in range(n_rows // 8):
                a = accs[j % n_acc]
                blk = rows[j * 8:(j + 1) * 8]
                if j < n_full:
                    accs[j % n_acc] = jnp.where(hit(blk, slice(0, TQB)), a + 1, a)
                else:
                    accs[j % n_acc] = jnp.concatenate(
                        [a[:, left], jnp.where(hit(blk[:, right], right), a[:, right] + 1, a[:, right])], axis=1)
            return tuple(accs)

        accs = lax.fori_loop(
            0, (nc - 1) // 2, lambda c, accs: bump(accs, pl.multiple_of(c * KSUPER, KSUPER), KSUPER, False),
            tuple(jnp.zeros((8, TQB), jnp.int32) for _ in range(n_acc)))
        accs = lax.cond(nc % 2 == 1,
                        lambda accs: bump(accs, pl.multiple_of((nc - 1) * KCHUNK, KCHUNK), KCHUNK, True),
                        lambda accs: bump(accs, pl.multiple_of((nc - 2) * KCHUNK, KSUPER), KSUPER, True), accs)
        acc = accs[0]
        for a in accs[1:]:
            acc = acc + a
        return jnp.sum(acc, axis=0, keepdims=True)

    def count_ge(cand, strict=False):
        return count_rows(keys_scr, (lambda r, lanes: r > cand[:, lanes]) if strict
                          else (lambda r, lanes: r >= cand[:, lanes]))

    @pl.when(i * TQB + TQB <= top_k)
    def _():
        thr_scr[...] = jnp.full((1, TQB), -jnp.inf, _F32)
        tie_scr[...] = jnp.full((1, TQB), -1, jnp.int32)

    @pl.when(i * TQB + TQB > top_k)
    def _():
        cnt_nonneg = jnp.sum(ext_scr[16:24, :], axis=0, keepdims=True).astype(jnp.int32)
        cnt_pos = jnp.sum(ext_scr[24:32, :], axis=0, keepdims=True).astype(jnp.int32)
        key_max = _f32_to_key(jnp.max(ext_scr[0:8, :], axis=0, keepdims=True))
        key_min = _f32_to_key(jnp.min(ext_scr[8:16, :], axis=0, keepdims=True))
        n_causal = i * TQB + 1 + lax.broadcasted_iota(jnp.int32, (1, TQB), 1)
        nonneg = cnt_nonneg >= top_k
        lo = jnp.where(nonneg, 0, key_min)
        cnt_lo = jnp.where(nonneg, cnt_nonneg, n_causal)
        hi = jnp.where(nonneg, jnp.where(cnt_pos < top_k, 1, key_max + 1), 0)

        def halve_values(_, state):
            lo_v, hi_v, cnt_lo = state
            mid = lo_v + (hi_v - lo_v) * 0.5
            mid = jnp.where(mid > lo_v, jnp.where(mid < hi_v, mid, lo_v), lo_v)
            cnt = count_ge(mid)
            ok = cnt >= top_k
            return jnp.where(ok, mid, lo_v), jnp.where(ok, hi_v, mid), jnp.where(ok, cnt, cnt_lo)

        def halve_keys(_, state):
            lo, hi, cnt_lo = state
            mid = lo + lax.shift_right_logical(hi - lo, 1)
            cnt = count_ge(_key_to_f32(mid))
            ok = cnt >= top_k
            return jnp.where(ok, mid, lo), jnp.where(ok, hi, mid), jnp.where(ok, cnt, cnt_lo)

        def finished(state):
            lo, hi, cnt_lo = state
            done = jnp.where(cnt_lo == top_k, 1, jnp.where(hi - lo <= 1, 1, 0))
            return jnp.min(done) == 1

        lo_v, hi_v, cnt_lo = lax.fori_loop(0, VALUE_PASSES, halve_values,
                                           (_key_to_f32(lo), _key_to_f32(hi), cnt_lo))
        state = (_f32_to_key(lo_v), _f32_to_key(hi_v), cnt_lo)
        stops = (0,) + SEARCH_CHECKS + (31,)
        state = lax.fori_loop(stops[0], stops[1], halve_keys, state)
        for first, last in zip(stops[1:-1], stops[2:]):
            state = lax.cond(finished(state), lambda s: s,
                             functools.partial(lax.fori_loop, first, last, halve_keys), state)
        lo, _, cnt_ge = state
        thr = _key_to_f32(lo)
        thr_scr[...] = thr
        tie_scr[...] = jnp.full((1, TQB), 2 ** 30, jnp.int32)

        @pl.when(jnp.max(cnt_ge) > top_k)
        def _():
            need = (top_k - count_ge(thr, strict=True)).astype(_F32)
            lower = (lax.broadcasted_iota(jnp.int32, (KCHUNK, KCHUNK), 0)
                     >= lax.broadcasted_iota(jnp.int32, (KCHUNK, KCHUNK), 1))
            lower = jnp.where(lower, 1.0, 0.0).astype(_BF16)

            def prefix_body(c, state):
                before, below = state
                for half in range(KSUPER // KCHUNK):
                    r0 = pl.multiple_of(c * KSUPER + half * KCHUNK, KCHUNK)
                    tied = jnp.where(keys_scr[pl.ds(r0, KCHUNK), :] == thr, 1.0, 0.0).astype(_BF16)
                    upto = jnp.dot(lower, tied, preferred_element_type=_F32) + before
                    below = below + jnp.sum(jnp.where(upto < need, 1, 0).reshape(KCHUNK // 8, 8, TQB), axis=0)
                    before = upto[KCHUNK - 1:KCHUNK, :]
                return before, below

            _, below = lax.fori_loop(0, ns, prefix_body,
                                     (jnp.zeros((1, TQB), _F32), jnp.zeros((8, TQB), jnp.int32)))
            tie_scr[...] = jnp.sum(below, axis=0, keepdims=True)

    thr = thr_scr[...]
    tie = tie_scr[...]
    row = lax.broadcasted_iota(jnp.int32, (KCHUNK, TQB), 0)

    zeros = jnp.zeros((HEAD_DIM, TQB), _BF16)
    n_pairs = B_HEADS // 2
    rhs = []
    for p in range(n_pairs):
        q0 = qb_ref[0, (2 * p) * HEAD_DIM:(2 * p + 1) * HEAD_DIM, :]
        q1 = qb_ref[0, (2 * p + 1) * HEAD_DIM:(2 * p + 2) * HEAD_DIM, :]
        rhs.append(jnp.concatenate([jnp.concatenate([q0, zeros], axis=0),
                                    jnp.concatenate([zeros, q1], axis=0)], axis=1))

    def score_step(c):
        r0 = pl.multiple_of(c * KCHUNK, KCHUNK)
        k = keys_scr[pl.ds(r0, KCHUNK), :]
        mask = jnp.where(k == thr, jnp.where(row <= tie - r0, 0.0, NEG), jnp.where(k > thr, 0.0, NEG))
        for p in range(n_pairs):
            s = jnp.dot(kb_ref[0, pl.ds(r0, KCHUNK), p * LANES:(p + 1) * LANES], rhs[p],
                        preferred_element_type=_F32)
            for hh in range(2):
                h = 2 * p + hh
                parts = []
                for half in range(KCHUNK // TQ):
                    jb = c * (KCHUNK // TQ) + half
                    subs = []
                    for sub in range(TQB // TQ):
                        kind = jnp.clip(jb - (i * (TQB // TQ) + sub) + 2, 0, 2)
                        lanes = slice(sub * TQ, (sub + 1) * TQ)
                        subs.append(s[half * TQ:(half + 1) * TQ, hh * TQB + sub * TQ:hh * TQB + (sub + 1) * TQ]
                                    + bias_ref[h, kind] + mask[half * TQ:(half + 1) * TQ, lanes])
                    parts.append(jnp.concatenate(subs, axis=1))
                sh = jnp.concatenate(parts, axis=0)
                sc_scr[pl.ds(r0, KCHUNK), h * TQB:(h + 1) * TQB] = sh
                m_scr[h * 8:(h + 1) * 8, :] = jnp.maximum(m_scr[h * 8:(h + 1) * 8, :], _colmax8(sh))

    m_scr[...] = jnp.full(m_scr.shape, NEG, _F32)
    _for_chunks(nc, score_step)
    m_row = jnp.concatenate(
        [jnp.max(m_scr[h * 8:(h + 1) * 8, :], axis=0, keepdims=True) for h in range(B_HEADS)],
        axis=1)
    acc_scr[...] = jnp.zeros_like(acc_scr)

    def pv_step(c):
        r0 = pl.multiple_of(c * KCHUNK, KCHUNK)
        for p in range(n_pairs):
            cols = slice(2 * p * TQB, (2 * p + 2) * TQB)
            pr = jnp.exp2(sc_scr[pl.ds(r0, KCHUNK), cols] - m_row[:, cols]).astype(_BF16)
            acc_scr[p] += jnp.dot(vb_ref[0, c, p * PAIR_ROWS:(p + 1) * PAIR_ROWS, :], pr,
                                  preferred_element_type=_F32)

    _for_chunks(nc, pv_step)
    for p in range(n_pairs):
        inv = 1.0 / acc_scr[p, 2 * HEAD_DIM:2 * HEAD_DIM + 1, :]
        for hh in range(2):
            rows = slice((2 * p + hh) * HEAD_DIM, (2 * p + hh + 1) * HEAD_DIM)
            o = (acc_scr[p, hh * HEAD_DIM:(hh + 1) * HEAD_DIM, hh * TQB:(hh + 1) * TQB]
                 * inv[:, hh * TQB:(hh + 1) * TQB] * zb_ref[0, rows, :].astype(_F32))
            o_ref[0, rows, :] = o.astype(_BF16)


def _mix_b(qb_t, kb_n, vb_c, zb_t, qi_t, ki_n, wi_t, bias_b):
    B, _, S = qb_t.shape
    nq = S // TQB
    per_q = lambda rows: pl.BlockSpec((1, rows, TQB), lambda b, i: (b, 0, i))
    in_specs = [
        per_q(B_WIDTH),
        pl.BlockSpec((1, S, B_WIDTH), lambda b, i: (b, 0, 0)),
        pl.BlockSpec((1, S // KCHUNK, V_ROWS, KCHUNK), lambda b, i: (b, 0, 0, 0)),
        per_q(B_WIDTH),
        per_q(IDX_WIDTH),
        pl.BlockSpec((1, S, LANES), lambda b, i: (b, 0, 0)),
        per_q(IDX_HEADS),
        pl.BlockSpec((B_HEADS, 3, TQ, TQ), lambda b, i: (0, 0, 0, 0)),
    ]
    scratch = [
        pltpu.VMEM((S, TQB), _F32),
        pltpu.VMEM((S, B_HEADS * TQB), _F32),
        pltpu.VMEM((B_HEADS // 2, PAIR_ROWS, 2 * TQB), _F32),
        pltpu.VMEM((B_HEADS * 8, TQB), _F32),
        pltpu.VMEM((32, TQB), _F32),
        pltpu.VMEM((1, TQB), _F32),
        pltpu.VMEM((1, TQB), jnp.int32),
    ]
    return pl.pallas_call(
        _mix_b_kernel, grid=(B, nq), in_specs=in_specs,
        out_specs=per_q(B_WIDTH),
        out_shape=jax.ShapeDtypeStruct((B, B_WIDTH, S), _BF16),
        scratch_shapes=scratch,
        compiler_params=pltpu.CompilerParams(
            dimension_semantics=("parallel", "arbitrary"), vmem_limit_bytes=VMEM_LIMIT),
        name="mixer_b",
    )(qb_t, kb_n, vb_c, zb_t, qi_t, ki_n, wi_t, bias_b)


def _merge_kernel(x_ref, ya_ref, yb_ref, gt_ref, wpa_ref, wpb_ref, wo_ref, o_ref):
    pa = jnp.dot(wpa_ref[...], ya_ref[0], preferred_element_type=_F32)
    pb = jnp.dot(wpb_ref[...], yb_ref[0], preferred_element_type=_F32)
    merged = (gt_ref[0, :D_MODEL, :].astype(_F32) * pa
              + gt_ref[0, D_MODEL:, :].astype(_F32) * pb).astype(_BF16)
    out_t = jnp.dot(wo_ref[...], merged, preferred_element_type=_F32)
    o_ref[0] = x_ref[0] + out_t.T


def _merge(x, ya_t, yb_t, gt_t, wpa_t, wpb_t, wo_t):
    B, S, D = x.shape
    tm = TM_PROJ
    const = lambda b, t: (0, 0)
    in_specs = [
        pl.BlockSpec((1, tm, D), lambda b, t: (b, t, 0)),
        pl.BlockSpec((1, A_WIDTH, tm), lambda b, t: (b, 0, t)),
        pl.BlockSpec((1, B_WIDTH, tm), lambda b, t: (b, 0, t)),
        pl.BlockSpec((1, 2 * D, tm), lambda b, t: (b, 0, t)),
        pl.BlockSpec((D, A_WIDTH), const),
        pl.BlockSpec((D, B_WIDTH), const),
        pl.BlockSpec((D, D), const),
    ]
    return pl.pallas_call(
        _merge_kernel, grid=(B, S // tm), in_specs=in_specs,
        out_specs=pl.BlockSpec((1, tm, D), lambda b, t: (b, t, 0)),
        out_shape=jax.ShapeDtypeStruct((B, S, D), x.dtype),
        compiler_params=pltpu.CompilerParams(
            dimension_semantics=("parallel", "parallel"), vmem_limit_bytes=VMEM_LIMIT),
        name="merge_out",
    )(x, ya_t, yb_t, gt_t, wpa_t, wpb_t, wo_t)


def _t5_bucket(n):
    n = np.maximum(n, 0)
    max_exact = N_BUCKETS // 2
    nf = np.maximum(n, 1).astype(np.float64)
    large = max_exact + np.floor(np.log(nf / max_exact) / math.log(MAX_DISTANCE / max_exact)
                                 * (N_BUCKETS - max_exact)).astype(np.int64)
    large = np.minimum(large, N_BUCKETS - 1)
    return np.where(n < max_exact, n, large)


def _bias_tables(rel_bias):
    t = np.arange(TQ)[None, :]
    s = np.arange(TQ)[:, None]
    d_prev = t + TQ - s
    d_cur = t - s
    table_a = rel_bias[:, :A_Q_HEADS].astype(_F32).T * LOG2E
    table_b = rel_bias[:, A_Q_HEADS:].astype(_F32).T * LOG2E

    def tile(table, dist, windowed):
        onehot = (_t5_bucket(dist)[..., None] == np.arange(N_BUCKETS)).astype(np.float32)
        b = jnp.einsum("stk,hk->hst", onehot, table, precision=lax.Precision.HIGHEST)
        if windowed:
            b = jnp.where(((dist >= 0) & (dist < WINDOW))[None], b, NEG)
        return b

    bias_a = jnp.concatenate([tile(table_a, d_prev, True), tile(table_a, d_cur, True)], axis=1)
    far = jnp.broadcast_to(table_b[:, N_BUCKETS - 1][:, None, None], (B_HEADS, TQ, TQ))
    bias_b = jnp.stack([far, tile(table_b, d_prev, False), tile(table_b, d_cur, False)], axis=1)
    return bias_a, bias_b


def kernel(x, norm_g, w_in, qnorm_a, knorm_a, sinks_a, qnorm_b, knorm_b, rel_bias,
           w_proj_a, w_proj_b, w_out):
    assert norm_g.shape[0] == 1, "single-layer block"
    B, S, D = x.shape
    assert D == D_MODEL and S % TM_PROJ == 0 and S % KSUPER == 0 and S % TQB == 0 and TQB % TQ == 0
    assert MAX_DISTANCE <= TQ

    wt = w_in[0].T
    n_real = _SEG["kw"][0] + KW_REAL
    wt = jnp.concatenate(
        [wt[:n_real], jnp.zeros((_SEG["kw"][1] - n_real, D), wt.dtype), wt[n_real:]], axis=0)
    wt = wt.astype(_BF16)
    bcast = lambda g, scale: jnp.broadcast_to((g.astype(_F32) * scale)[:, None], (HEAD_DIM, TM_PROJ))
    q_scale = HEAD_DIM ** -0.5
    (qa_t, ka_n, va_t, za_t, qb_t, kb_n, vb_c, zb_t, qi_t, ki_n, wi_t, gt_t) = _inproj(
        x, norm_g[0][None, :].astype(_F32), wt,
        bcast(qnorm_a[0], q_scale * LOG2E), bcast(knorm_a[0], 1.0),
        bcast(qnorm_b[0], q_scale * LOG2E), bcast(knorm_b[0], 1.0))

    bias_a, bias_b = _bias_tables(rel_bias)
    sink_a = jnp.broadcast_to((sinks_a[0].astype(_F32) * LOG2E)[:, None], (A_Q_HEADS, TQ))
    ya_t = _mix_a(qa_t, ka_n, va_t, za_t, bias_a, sink_a)
    yb_t = _mix_b(qb_t, kb_n, vb_c, zb_t, qi_t, ki_n, wi_t, bias_b)
    return _merge(x, ya_t, yb_t, gt_t,
                  w_proj_a[0].T.astype(_BF16), w_proj_b[0].T.astype(_BF16), w_out[0].T.astype(_BF16))
```
